```python
import jax, jax.numpy as jnp
from jax import lax
import numpy as np

D_MODEL = 1024
BATCH = 8
SEQ = 2048
DEPTH = 1
DEC_BATCH = 128
DEC_SEQ = 1
PAST_LEN = 16384
PAGE_SIZE = 128

MIX_WIDTH = D_MODEL
POOL_WIDTH = MIX_WIDTH // 2
POOL_WINDOWS = (2, 4, 8, 16)
N_POOL_GROUPS = len(POOL_WINDOWS)
POOL_GROUP = POOL_WIDTH // N_POOL_GROUPS
POOL_HIST = max(POOL_WINDOWS) - 1
HG_WIDTH = MIX_WIDTH - POOL_WIDTH
HG_HEAD_DIM = 128
HG_HEADS = HG_WIDTH // HG_HEAD_DIM
CHUNK = 64
IN_WIDTH = POOL_WIDTH + 4 * HG_WIDTH
N_EXPERTS = 64
TOP_K = 8
N_GROUPS = 8
TOPK_GROUPS = 4
D_EXPERT = 256
ROUTED_SCALE = 2.5
MOE_BLOCK = 128
EPS = 1e-6

kernel_name = "hybrid_pool_hgrn2_moe_decoder_step"


def rms_norm(x, g):
    xf = x.astype(jnp.float32)
    y = xf * lax.rsqrt(jnp.mean(xf * xf, axis=-1, keepdims=True) + EPS)
    return (y * g.astype(jnp.float32)).astype(x.dtype)


def ada_mod(c, w_ada, b_ada):
    m = jax.nn.silu(c) @ w_ada + b_ada
    return jnp.split(m[:, None, :], 6, axis=-1)


def pool_mix(u_ext, pos0, w_pool, pool_scale):
    B, tot, _ = u_ext.shape
    L = tot - POOL_HIST
    cs = jnp.cumsum(u_ext.astype(jnp.float32), axis=1)
    cs = jnp.pad(cs, ((0, 0), (1, 0), (0, 0)))
    end = cs[:, POOL_HIST + 1:]
    u = u_ext[:, POOL_HIST:].astype(jnp.float32)
    pos = pos0 + jnp.arange(L)
    outs = []
    for g, w in enumerate(POOL_WINDOWS):
        sl = slice(g * POOL_GROUP, (g + 1) * POOL_GROUP)
        start = cs[:, POOL_HIST + 1 - w:POOL_HIST + 1 - w + L, sl]
        cnt = jnp.minimum(pos + 1, w).astype(jnp.float32)[None, :, None]
        outs.append((end[..., sl] - start) / cnt - u[..., sl])
    d = jnp.stack(outs, axis=2)
    y = jnp.einsum('blgc,gcd->blgd', d, w_pool.astype(jnp.float32)).reshape(B, L, POOL_WIDTH)
    return (y * pool_scale.astype(jnp.float32)).astype(u_ext.dtype)


def hgrn2_chunked(q, logf, k, v, s0):
    B, T, H, DK = q.shape
    n = T // CHUNK

    def to_chunks(a):
        return a.reshape(B, n, CHUNK, H, a.shape[-1]).transpose(1, 0, 3, 2, 4)

    causal = jnp.tril(jnp.ones((CHUNK, CHUNK), bool))[:, :, None]

    def step(S, inp):
        qc, gc, kc, vc = inp
        b = jnp.cumsum(gc, axis=2)
        diff = b[:, :, :, None, :] - b[:, :, None, :, :]
        decay = jnp.exp(jnp.where(causal, diff, -jnp.inf))
        A = jnp.einsum('bhik,bhijk,bhjk->bhij', qc, decay, kc)
        o = jnp.einsum('bhij,bhjv->bhiv', A, vc) + jnp.einsum('bhik,bhkv->bhiv', qc * jnp.exp(b), S)
        b_last = b[:, :, -1:, :]
        S = jnp.exp(b_last[:, :, 0, :])[..., None] * S + jnp.einsum('bhjk,bhjv->bhkv', kc * jnp.exp(b_last - b), vc)
        return S, o

    S, o = lax.scan(step, s0, (to_chunks(q), to_chunks(logf), to_chunks(k), to_chunks(v)))
    o = o.transpose(1, 0, 3, 2, 4).reshape(B, T, H, v.shape[-1])
    return o, S


def hgrn2_recurrent(q, logf, k, v, s0):
    def step(S, inp):
        qt, gt, kt, vt = inp
        S = jnp.exp(gt)[..., None] * S + kt[..., None] * vt[..., None, :]
        return S, jnp.einsum('bhk,bhkv->bhv', qt, S)

    S, o = lax.scan(step, s0, tuple(a.transpose(1, 0, 2, 3) for a in (q, logf, k, v)))
    return o.transpose(1, 0, 2, 3), S


def moe_ffn(h, w_router, router_bias, w1, w3, w2, ws1, ws3, ws2):
    T, D = h.shape
    scores = jax.nn.sigmoid(h.astype(jnp.float32) @ w_router.astype(jnp.float32))
    sel = scores + router_bias.astype(jnp.float32)
    grp_score = lax.top_k(sel.reshape(T, N_GROUPS, N_EXPERTS // N_GROUPS), 2)[0].sum(-1)
    _, gidx = lax.top_k(grp_score, TOPK_GROUPS)
    gmask = jax.nn.one_hot(gidx, N_GROUPS).sum(1) > 0
    emask = jnp.repeat(gmask, N_EXPERTS // N_GROUPS, axis=1)
    _, eidx = lax.top_k(jnp.where(emask, sel, -jnp.inf), TOP_K)
    wts = jnp.take_along_axis(scores, eidx, axis=1)
    wts = wts / jnp.sum(wts, axis=-1, keepdims=True) * ROUTED_SCALE
    A = T * TOP_K
    flat_e = eidx.reshape(-1).astype(jnp.int32)
    flat_t = jnp.repeat(jnp.arange(T, dtype=jnp.int32), TOP_K)
    flat_w = wts.reshape(-1)
    order = jnp.argsort(flat_e)
    se = flat_e[order]
    counts = jnp.bincount(flat_e, length=N_EXPERTS)
    start = jnp.cumsum(counts) - counts
    padded = (counts + MOE_BLOCK - 1) // MOE_BLOCK * MOE_BLOCK
    pend = jnp.cumsum(padded)
    pstart = pend - padded
    dest = (pstart[se] + jnp.arange(A) - start[se]).astype(jnp.int32)
    n_blocks = -(-A // MOE_BLOCK) + N_EXPERTS
    n_slots = n_blocks * MOE_BLOCK
    slot_tok = jnp.full((n_slots,), T, jnp.int32).at[dest].set(flat_t[order])
    slot_w = jnp.zeros((n_slots,), jnp.float32).at[dest].set(flat_w[order])
    block_e = jnp.minimum(jnp.searchsorted(pend, jnp.arange(n_blocks) * MOE_BLOCK, side='right'), N_EXPERTS - 1)
    h_pad = jnp.concatenate([h, jnp.zeros((1, D), h.dtype)], axis=0)

    def run_block(args):
        tok, e = args
        xb = h_pad[tok]
        return (jax.nn.silu(xb @ w1[e]) * (xb @ w3[e])) @ w2[e]

    yb = lax.map(run_block, (slot_tok.reshape(n_blocks, MOE_BLOCK), block_e))
    routed = jnp.zeros((T + 1, D), jnp.float32).at[slot_tok].add(
        yb.reshape(n_slots, D).astype(jnp.float32) * slot_w[:, None])[:T]
    shared = (jax.nn.silu(h @ ws1) * (h @ ws3)) @ ws2
    return (routed + shared.astype(jnp.float32)).astype(h.dtype)


def layer(x, c, pool_hist, hg_state, pos0, chunked, lb, w_ada, b_ada, g_pre_mix, g_post_mix, w_in, w_pool,
          pool_scale, g_out_norm, w_out, g_pre_ffn, g_post_ffn, w_router, router_bias, w1, w3, w2, ws1, ws3, ws2):
    B, L, D = x.shape
    sh1, sc1, gt1, sh2, sc2, gt2 = ada_mod(c, w_ada, b_ada)
    h = rms_norm(x, g_pre_mix) * (1 + sc1) + sh1
    proj = h @ w_in
    u = proj[..., :POOL_WIDTH]
    u_ext = jnp.concatenate([pool_hist.astype(u.dtype), u], axis=1)
    y_pool = pool_mix(u_ext, pos0, w_pool, pool_scale)
    new_pool = u_ext[:, -POOL_HIST:]
    rest = proj[..., POOL_WIDTH:].astype(jnp.float32).reshape(B, L, 4, HG_HEADS, HG_HEAD_DIM)
    q_r, f_r, i_r, g_r = rest[:, :, 0], rest[:, :, 1], rest[:, :, 2], rest[:, :, 3]
    lbh = lb.reshape(HG_HEADS, HG_HEAD_DIM)
    f = lbh + (1.0 - lbh) * jax.nn.sigmoid(f_r)
    q = jax.nn.silu(q_r)
    s0 = hg_state.astype(jnp.float32)
    if chunked:
        o, S = hgrn2_chunked(q, jnp.log(f), 1.0 - f, i_r, s0)
    else:
        o, S = hgrn2_recurrent(q, jnp.log(f), 1.0 - f, i_r, s0)
    o = o * lax.rsqrt(jnp.mean(o * o, axis=-1, keepdims=True) + EPS) * g_out_norm.astype(jnp.float32) * jax.nn.silu(g_r)
    mix = jnp.concatenate([y_pool, o.reshape(B, L, HG_WIDTH).astype(x.dtype)], axis=-1) @ w_out
    x = x + gt1 * rms_norm(mix, g_post_mix)
    h2 = rms_norm(x, g_pre_ffn) * (1 + sc2) + sh2
    ff = moe_ffn(h2.reshape(B * L, D), w_router, router_bias, w1, w3, w2, ws1, ws3, ws2).reshape(B, L, D)
    x = x + gt2 * rms_norm(ff, g_post_ffn)
    return x, new_pool, S.astype(hg_state.dtype)


def setup_inputs(seed: int = 0) -> dict:
    key = jax.random.key(seed)
    ks = jax.random.split(key, 32)
    nrm = jax.random.normal
    D = D_MODEL
    f32 = jnp.float32
    return {
        "x_prompt": nrm(ks[0], (BATCH, SEQ, D), f32),
        "x_sample": nrm(ks[1], (DEC_BATCH, DEC_SEQ, D), f32),
        "c_prompt": nrm(ks[2], (BATCH, D), f32),
        "c_sample": nrm(ks[3], (DEC_BATCH, D), f32),
        "state_pool": nrm(ks[4], (DEPTH, DEC_BATCH, POOL_HIST, POOL_WIDTH), f32),
        "state_hgrn": 0.5 * nrm(ks[5], (DEPTH, DEC_BATCH, HG_HEADS, HG_HEAD_DIM, HG_HEAD_DIM), f32),
        "w_ada": 0.5 * D ** -0.5 * nrm(ks[6], (DEPTH, D, 6 * D), f32),
        "b_ada": 0.02 * nrm(ks[7], (DEPTH, 6 * D), f32),
        "g_pre_mix": 1.0 + 0.05 * nrm(ks[8], (DEPTH, D), f32),
        "g_post_mix": 1.0 + 0.05 * nrm(ks[9], (DEPTH, D), f32),
        "w_in": D ** -0.5 * nrm(ks[10], (DEPTH, D, IN_WIDTH), f32),
        "w_pool": POOL_GROUP ** -0.5 * nrm(ks[11], (DEPTH, N_POOL_GROUPS, POOL_GROUP, POOL_GROUP), f32),
        "pool_scale": 1.0 + 0.1 * nrm(ks[12], (DEPTH, POOL_WIDTH), f32),
        "lb_logits": 0.5 * nrm(ks[13], (DEPTH + 1, HG_WIDTH), f32),
        "g_out_norm": 1.0 + 0.05 * nrm(ks[14], (DEPTH, HG_HEAD_DIM), f32),
        "w_out": MIX_WIDTH ** -0.5 * nrm(ks[15], (DEPTH, MIX_WIDTH, D), f32),
        "g_pre_ffn": 1.0 + 0.05 * nrm(ks[16], (DEPTH, D), f32),
        "g_post_ffn": 1.0 + 0.05 * nrm(ks[17], (DEPTH, D), f32),
        "w_router": D ** -0.5 * nrm(ks[18], (DEPTH, D, N_EXPERTS), f32),
        "router_bias": 0.01 * nrm(ks[19], (DEPTH, N_EXPERTS), f32),
        "w_exp_gate": D ** -0.5 * nrm(ks[20], (DEPTH, N_EXPERTS, D, D_EXPERT), f32),
        "w_exp_up": D ** -0.5 * nrm(ks[21], (DEPTH, N_EXPERTS, D, D_EXPERT), f32),
        "w_exp_down": D_EXPERT ** -0.5 * nrm(ks[22], (DEPTH, N_EXPERTS, D_EXPERT, D), f32),
        "w_sh_gate": D ** -0.5 * nrm(ks[23], (DEPTH, D, D_EXPERT), f32),
        "w_sh_up": D ** -0.5 * nrm(ks[24], (DEPTH, D, D_EXPERT), f32),
        "w_sh_down": D_EXPERT ** -0.5 * nrm(ks[25], (DEPTH, D_EXPERT, D), f32),
    }


def reference(x_prompt, x_sample, c_prompt, c_sample, state_pool, state_hgrn, w_ada, b_ada, g_pre_mix, g_post_mix,
              w_in, w_pool, pool_scale, lb_logits, g_out_norm, w_out, g_pre_ffn, g_post_ffn, w_router, router_bias,
              w_exp_gate, w_exp_up, w_exp_down, w_sh_gate, w_sh_up, w_sh_down):
    lb_all = jnp.cumsum(jax.nn.softmax(lb_logits.astype(jnp.float32), axis=0), axis=0)
    bp = x_prompt.shape[0]
    zero_pool = jnp.zeros((bp, POOL_HIST, POOL_WIDTH), x_prompt.dtype)
    zero_hg = jnp.zeros((bp, HG_HEADS, HG_HEAD_DIM, HG_HEAD_DIM), x_prompt.dtype)
    hp, hs = x_prompt, x_sample
    pool_p, hg_p, pool_s, hg_s = [], [], [], []
    for l in range(DEPTH):
        wl = (w_ada[l], b_ada[l], g_pre_mix[l], g_post_mix[l], w_in[l], w_pool[l], pool_scale[l], g_out_norm[l],
              w_out[l], g_pre_ffn[l], g_post_ffn[l], w_router[l], router_bias[l], w_exp_gate[l], w_exp_up[l],
              w_exp_down[l], w_sh_gate[l], w_sh_up[l], w_sh_down[l])
        hp, npool, nhg = layer(hp, c_prompt, zero_pool, zero_hg, 0, True, lb_all[l], *wl)
        pool_p.append(npool)
        hg_p.append(nhg)
        hs, spool, shg = layer(hs, c_sample, state_pool[l], state_hgrn[l], PAST_LEN, False, lb_all[l], *wl)
        pool_s.append(spool)
        hg_s.append(shg)
    return (hp, hs, jnp.stack(pool_p), jnp.stack(hg_p), jnp.stack(pool_s), jnp.stack(hg_s))
```

```python
import functools

import numpy as np
import jax
import jax.numpy as jnp
from jax import lax
from jax.experimental import pallas as pl
from jax.experimental.pallas import tpu as pltpu

F32 = jnp.float32
BF16 = jnp.bfloat16

D_MODEL = 1024
POOL_WIDTH = 512
POOL_WINDOWS = (2, 4, 8, 16)
POOL_GROUP = 128
POOL_HIST = 15
HG_WIDTH = 512
HG_HEADS = 4
HG_HEAD_DIM = 128
IN_WIDTH = POOL_WIDTH + 4 * HG_WIDTH
N_EXPERTS = 64
TOP_K = 8
N_GROUPS = 8
TOPK_GROUPS = 4
GROUP_SIZE = N_EXPERTS // N_GROUPS
D_EXPERT = 256
ROUTED_SCALE = 2.5
EPS = 1e-6

LANES = 128
CHUNK = 64
TIME_TILE = 256
ADA_TILE_N = 512
ROUTER_TILE = 512
MOE_TILE = 1024
SAMPLE_STATE_BLOCK = 8
VMEM_LIMIT = 52 * 1024 * 1024

_LEVELS = (64, 32, 16, 8, 4, 2)
_BLK_CUM = 0
_BLK_END = 1


def _level_blocks():
    c = CHUNK
    i = np.arange(c)[:, None]
    s = np.arange(c)[None, :]
    blocks = [(s <= i), (s > i)]
    index = {}
    for lvl in _LEVELS:
        seg = (i // lvl) * lvl
        mid = seg + lvl // 2
        ref = mid - 1
        index[lvl] = len(blocks)
        blocks.append((i >= mid) & (s > ref) & (s <= i))
        if lvl > 2:
            blocks.append((i < mid) & (s > i) & (s <= ref))
    m = np.concatenate(blocks, axis=0).astype(np.float32)
    return np.concatenate([m, m], axis=1), index


_M2_NP, _LEVEL_INDEX = _level_blocks()


def _dot(a, b):
    return jnp.dot(a, b, preferred_element_type=F32)


def _dot_nt(a, b):
    return lax.dot_general(a, b, (((1,), (1,)), ((), ())), preferred_element_type=F32)


def _split2(x):
    hi = x.astype(BF16)
    lo = (x - hi.astype(F32)).astype(BF16)
    return hi, lo


def _silu(x):
    return x * jax.nn.sigmoid(x)


def _rms(x, g):
    return x * lax.rsqrt(jnp.mean(x * x, axis=-1, keepdims=True) + EPS) * g


def _mods(m):
    return [m[:, j * D_MODEL:(j + 1) * D_MODEL] for j in range(6)]


def _forget_lower_bound(lbl):
    mx = jnp.max(lbl, axis=0, keepdims=True)
    e = jnp.exp(lbl - mx)
    return e[0:1] / jnp.sum(e, axis=0, keepdims=True)


def _router_logits(h2, wrh, wrl):
    hi, lo = _split2(h2)
    return _dot(hi, wrh) + _dot(lo, wrh) + _dot(hi, wrl)


def _ada_kernel(c_ref, w_ref, b_ref, o_ref):
    a_hi, a_lo = _split2(_silu(c_ref[...]))
    w_hi, w_lo = _split2(w_ref[...])
    o_ref[...] = _dot(a_hi, w_hi) + _dot(a_lo, w_hi) + _dot(a_hi, w_lo) + b_ref[...]


def _ada(c_all, w_ada, b_ada):
    rows = c_all.shape[0]
    n = w_ada.shape[1]
    return pl.pallas_call(
        _ada_kernel,
        out_shape=jax.ShapeDtypeStruct((rows, n), F32),
        grid=(n // ADA_TILE_N,),
        in_specs=[pl.BlockSpec((rows, D_MODEL), lambda j: (0, 0)),
                  pl.BlockSpec((D_MODEL, ADA_TILE_N), lambda j: (0, j)),
                  pl.BlockSpec((1, ADA_TILE_N), lambda j: (0, j))],
        out_specs=pl.BlockSpec((rows, ADA_TILE_N), lambda j: (0, j)),
        compiler_params=pltpu.CompilerParams(dimension_semantics=("parallel",),
                                             vmem_limit_bytes=VMEM_LIMIT),
        name="ada",
    )(c_all, w_ada, b_ada)


def _level_masks():
    i = lax.broadcasted_iota(jnp.int32, (CHUNK, CHUNK), 0)
    j = lax.broadcasted_iota(jnp.int32, (CHUNK, CHUNK), 1)
    masks = {}
    for lvl in _LEVELS:
        sh = lvl.bit_length() - 1
        same = (i >> sh) == (j >> sh)
        upper = ((i >> (sh - 1)) & 1) == 1
        lower = ((j >> (sh - 1)) & 1) == 0
        masks[lvl] = same & upper & lower
    return masks, i == j


def _hgrn_chunk_head(q, k, v, e_all, lane0, st, masks, eye):
    def blk(n):
        return e_all[n * CHUNK:(n + 1) * CHUNK, lane0:lane0 + HG_HEAD_DIM]

    b = blk(_BLK_CUM)
    a = jnp.where(eye, _dot_nt(q.astype(BF16), k.astype(BF16)), 0.0)
    for lvl in _LEVELS:
        n = _LEVEL_INDEX[lvl]
        ql = (q * jnp.exp(blk(n))).astype(BF16)
        kl = (k * jnp.exp(blk(n + 1))).astype(BF16) if lvl > 2 else k.astype(BF16)
        a = a + jnp.where(masks[lvl], _dot_nt(ql, kl), 0.0)
    st_b = st.astype(BF16)
    o = _dot(a.astype(BF16), v.astype(BF16)) + _dot_nt((q * jnp.exp(b)).astype(BF16), st_b)
    k_end = (k * jnp.exp(blk(_BLK_END))).astype(BF16)
    decay = jnp.exp(b[CHUNK - 1:CHUNK, :])
    st_new = st * decay + _dot(v.T.astype(BF16), k_end)
    return o, st_new


def _mix_prompt_kernel(x_ref, mod_ref, gpre_ref, gpost_ref, gffn_ref, win_ref, wpool_ref, pscale_ref,
                       lbl_ref, gout_ref, wout_ref, wrh_ref, wrl_ref, m2_ref,
                       x1_ref, h2_ref, lg_ref, pool_ref, st_ref,
                       st_s, ubuf, q_s, k_s, v_s, g_s, o_s):
    t = pl.program_id(1)
    n_t = pl.num_programs(1)
    tt = TIME_TILE

    @pl.when(t == 0)
    def _():
        st_s[...] = jnp.zeros_like(st_s)
        ubuf[0:16, :] = jnp.zeros((16, POOL_WIDTH), F32)

    xt = x_ref[0]
    sh1, sc1, gt1, sh2, sc2, gt2 = _mods(mod_ref[0])
    h = _rms(xt, gpre_ref[...]) * (1.0 + sc1) + sh1
    proj = _dot(h.astype(BF16), win_ref[...])

    u = proj[:, :POOL_WIDTH]
    ubuf[16:16 + tt, :] = u
    pos = (t * tt + lax.broadcasted_iota(jnp.int32, (tt, 1), 0) + 1).astype(F32)
    ys = []
    for g, w in enumerate(POOL_WINDOWS):
        s = ubuf[:, g * POOL_GROUP:(g + 1) * POOL_GROUP]
        off = 0
        for step in range(w.bit_length() - 1):
            sh = 1 << step
            s = s[sh:, :] + s[:-sh, :]
            off += sh
        ws = s[16 - off:16 - off + tt, :]
        cnt = jnp.minimum(pos, float(w))
        d = ws / cnt - u[:, g * POOL_GROUP:(g + 1) * POOL_GROUP]
        ys.append(_dot(d.astype(BF16), wpool_ref[g]))
    y_pool = jnp.concatenate(ys, axis=1) * pscale_ref[...]
    ubuf[0:16, :] = ubuf[tt:tt + 16, :]

    @pl.when(t == n_t - 1)
    def _():
        pool_ref[0] = u[tt - POOL_HIST:, :]

    lb = _forget_lower_bound(lbl_ref[...])
    f = lb + (1.0 - lb) * jax.nn.sigmoid(proj[:, POOL_WIDTH + HG_WIDTH:POOL_WIDTH + 2 * HG_WIDTH])
    q_s[...] = _silu(proj[:, POOL_WIDTH:POOL_WIDTH + HG_WIDTH])
    k_s[...] = 1.0 - f
    v_s[...] = proj[:, POOL_WIDTH + 2 * HG_WIDTH:POOL_WIDTH + 3 * HG_WIDTH]
    g_s[...] = jnp.log(f)
    gate = _silu(proj[:, POOL_WIDTH + 3 * HG_WIDTH:])

    masks, eye = _level_masks()
    m2 = m2_ref[...]

    def chunk_body(c, carry):
        r0 = pl.multiple_of(c * CHUNK, CHUNK)
        g_hi, g_lo = _split2(g_s[pl.ds(r0, CHUNK), :])
        e_all = _dot(m2, jnp.concatenate([g_hi, g_lo], axis=0))
        for hd in range(HG_HEADS):
            lane0 = hd * HG_HEAD_DIM
            q = q_s[pl.ds(r0, CHUNK), lane0:lane0 + HG_HEAD_DIM]
            k = k_s[pl.ds(r0, CHUNK), lane0:lane0 + HG_HEAD_DIM]
            v = v_s[pl.ds(r0, CHUNK), lane0:lane0 + HG_HEAD_DIM]
            o, st_new = _hgrn_chunk_head(q, k, v, e_all, lane0, st_s[hd], masks, eye)
            st_s[hd] = st_new
            o_s[pl.ds(r0, CHUNK), lane0:lane0 + HG_HEAD_DIM] = o
        return carry

    lax.fori_loop(0, tt // CHUNK, chunk_body, 0)

    o = o_s[...]
    os_ = []
    for hd in range(HG_HEADS):
        oh = o[:, hd * HG_HEAD_DIM:(hd + 1) * HG_HEAD_DIM]
        os_.append(oh * lax.rsqrt(jnp.mean(oh * oh, axis=-1, keepdims=True) + EPS) * gout_ref[...])
    o_n = jnp.concatenate(os_, axis=1) * gate

    mix = _dot(y_pool.astype(BF16), wout_ref[0:POOL_WIDTH, :]) + _dot(o_n.astype(BF16), wout_ref[POOL_WIDTH:, :])
    x1 = xt + gt1 * _rms(mix, gpost_ref[...])
    h2 = _rms(x1, gffn_ref[...]) * (1.0 + sc2) + sh2
    x1_ref[0] = x1
    h2_ref[0] = h2.astype(BF16)
    lg_ref[0] = _router_logits(h2, wrh_ref[...], wrl_ref[...])

    @pl.when(t == n_t - 1)
    def _():
        for hd in range(HG_HEADS):
            st_ref[0, hd] = st_s[hd].T


def _full(shape):
    nd = len(shape)
    return pl.BlockSpec(shape, lambda *_: (0,) * nd)


def _mix_prompt(x, mod, gpre, gpost, gffn, win, wpool, pscale, lbl, gout, wout, wrh, wrl, m2):
    b, l, d = x.shape
    tt = TIME_TILE
    tile = lambda i, j: (i, j, 0)
    per_b = lambda i, j: (i, 0, 0)
    return pl.pallas_call(
        _mix_prompt_kernel,
        out_shape=(jax.ShapeDtypeStruct((b, l, d), F32),
                   jax.ShapeDtypeStruct((b, l, d), BF16),
                   jax.ShapeDtypeStruct((b, l, LANES), F32),
                   jax.ShapeDtypeStruct((b, POOL_HIST, POOL_WIDTH), F32),
                   jax.ShapeDtypeStruct((b, HG_HEADS, HG_HEAD_DIM, HG_HEAD_DIM), F32)),
        grid=(b, l // tt),
        in_specs=[pl.BlockSpec((1, tt, d), tile),
                  pl.BlockSpec((1, 1, 6 * d), per_b),
                  _full((1, d)), _full((1, d)), _full((1, d)),
                  _full(win.shape), _full(wpool.shape), _full((1, POOL_WIDTH)),
                  _full(lbl.shape), _full((1, HG_HEAD_DIM)), _full(wout.shape),
                  _full(wrh.shape), _full(wrl.shape), _full(m2.shape)],
        out_specs=(pl.BlockSpec((1, tt, d), tile),
                   pl.BlockSpec((1, tt, d), tile),
                   pl.BlockSpec((1, tt, LANES), tile),
                   pl.BlockSpec((1, POOL_HIST, POOL_WIDTH), per_b),
                   pl.BlockSpec((1, HG_HEADS, HG_HEAD_DIM, HG_HEAD_DIM), lambda i, j: (i, 0, 0, 0))),
        scratch_shapes=[pltpu.VMEM((HG_HEADS, HG_HEAD_DIM, HG_HEAD_DIM), F32),
                        pltpu.VMEM((tt + 16, POOL_WIDTH), F32),
                        pltpu.VMEM((tt, HG_WIDTH), F32), pltpu.VMEM((tt, HG_WIDTH), F32),
                        pltpu.VMEM((tt, HG_WIDTH), F32), pltpu.VMEM((tt, HG_WIDTH), F32),
                        pltpu.VMEM((tt, HG_WIDTH), F32)],
        compiler_params=pltpu.CompilerParams(dimension_semantics=("parallel", "arbitrary"),
                                             vmem_limit_bytes=VMEM_LIMIT),
        name="mix_prompt",
    )(x, mod, gpre, gpost, gffn, win, wpool, pscale, lbl, gout, wout, wrh, wrl, m2)


def _mix_sample_in_kernel(x_ref, mod_ref, gpre_ref, win_ref, wpool_ref, pscale_ref, lbl_ref, hist_ref,
                          ypool_ref, npool_ref, ft_ref, qt_ref, v_ref, gate_ref):
    xt = x_ref[...]
    sh1, sc1 = _mods(mod_ref[...])[:2]
    h = _rms(xt, gpre_ref[...]) * (1.0 + sc1) + sh1
    proj = _dot(h.astype(BF16), win_ref[...])
    u = proj[:, :POOL_WIDTH]
    row = lax.broadcasted_iota(jnp.int32, (hist_ref.shape[0], POOL_HIST, POOL_GROUP), 1)
    ys = []
    for g, w in enumerate(POOL_WINDOWS):
        sl = slice(g * POOL_GROUP, (g + 1) * POOL_GROUP)
        past = jnp.sum(jnp.where(row >= POOL_HIST - (w - 1), hist_ref[:, :, sl], 0.0), axis=1)
        ug = u[:, sl]
        d = (past + ug) / float(w) - ug
        ys.append(_dot(d.astype(BF16), wpool_ref[g]))
    ypool_ref[...] = jnp.concatenate(ys, axis=1) * pscale_ref[...]
    npool_ref[:, 0:POOL_HIST - 1, :] = hist_ref[:, 1:POOL_HIST, :]
    npool_ref[:, POOL_HIST - 1, :] = u

    lb = _forget_lower_bound(lbl_ref[...])
    f = lb + (1.0 - lb) * jax.nn.sigmoid(proj[:, POOL_WIDTH + HG_WIDTH:POOL_WIDTH + 2 * HG_WIDTH])
    ft_ref[...] = f.T
    qt_ref[...] = _silu(proj[:, POOL_WIDTH:POOL_WIDTH + HG_WIDTH]).T
    v_ref[...] = proj[:, POOL_WIDTH + 2 * HG_WIDTH:POOL_WIDTH + 3 * HG_WIDTH]
    gate_ref[...] = _silu(proj[:, POOL_WIDTH + 3 * HG_WIDTH:])


def _mix_sample_state_kernel(s_ref, ft_ref, qt_ref, v_ref, snew_ref, o_ref):
    i = pl.program_id(0)
    lane = lax.broadcasted_iota(jnp.int32, (HG_HEAD_DIM, ft_ref.shape[1]), 1)
    for j in range(SAMPLE_STATE_BLOCK):
        mine = lane == i * SAMPLE_STATE_BLOCK + j
        for hd in range(HG_HEADS):
            r0 = hd * HG_HEAD_DIM
            f = jnp.sum(jnp.where(mine, ft_ref[r0:r0 + HG_HEAD_DIM, :], 0.0), axis=1, keepdims=True)
            q = jnp.sum(jnp.where(mine, qt_ref[r0:r0 + HG_HEAD_DIM, :], 0.0), axis=1, keepdims=True)
            v = v_ref[j:j + 1, r0:r0 + HG_HEAD_DIM]
            s_new = f * s_ref[j, hd] + (1.0 - f) * v
            snew_ref[j, hd] = s_new
            o_ref[j:j + 1, r0:r0 + HG_HEAD_DIM] = jnp.sum(q * s_new, axis=0, keepdims=True)


def _mix_sample_out_kernel(x_ref, mod_ref, o_ref, gate_ref, ypool_ref, gout_ref, wout_ref, gpost_ref,
                           gffn_ref, wrh_ref, wrl_ref, x1_ref, h2_ref, lg_ref):
    _, _, gt1, sh2, sc2, _ = _mods(mod_ref[...])
    o = o_ref[...]
    os_ = []
    for hd in range(HG_HEADS):
        oh = o[:, hd * HG_HEAD_DIM:(hd + 1) * HG_HEAD_DIM]
        os_.append(oh * lax.rsqrt(jnp.mean(oh * oh, axis=-1, keepdims=True) + EPS) * gout_ref[...])
    o_n = jnp.concatenate(os_, axis=1) * gate_ref[...]
    mix = (_dot(ypool_ref[...].astype(BF16), wout_ref[0:POOL_WIDTH, :])
           + _dot(o_n.astype(BF16), wout_ref[POOL_WIDTH:, :]))
    x1 = x_ref[...] + gt1 * _rms(mix, gpost_ref[...])
    h2 = _rms(x1, gffn_ref[...]) * (1.0 + sc2) + sh2
    x1_ref[...] = x1
    h2_ref[...] = h2.astype(BF16)
    lg_ref[...] = _router_logits(h2, wrh_ref[...], wrl_ref[...])


def _mix_sample(x, mod, hist, state, gpre, gpost, gffn, win, wpool, pscale, lbl, gout, wout, wrh, wrl):
    b = x.shape[0]
    bb = SAMPLE_STATE_BLOCK
    cp = pltpu.CompilerParams(vmem_limit_bytes=VMEM_LIMIT)
    ypool, npool, ft, qt, v, gate = pl.pallas_call(
        _mix_sample_in_kernel,
        out_shape=(jax.ShapeDtypeStruct((b, POOL_WIDTH), F32),
                   jax.ShapeDtypeStruct((b, POOL_HIST, POOL_WIDTH), F32),
                   jax.ShapeDtypeStruct((HG_WIDTH, b), F32),
                   jax.ShapeDtypeStruct((HG_WIDTH, b), F32),
                   jax.ShapeDtypeStruct((b, HG_WIDTH), F32),
                   jax.ShapeDtypeStruct((b, HG_WIDTH), F32)),
        compiler_params=cp,
        name="mix_sample_in",
    )(x, mod, gpre, win, wpool, pscale, lbl, hist)

    s_spec = pl.BlockSpec((bb, HG_HEADS, HG_HEAD_DIM, HG_HEAD_DIM), lambda i: (i, 0, 0, 0))
    col_spec = _full((HG_WIDTH, b))
    row_spec = pl.BlockSpec((bb, HG_WIDTH), lambda i: (i, 0))
    s_new, o = pl.pallas_call(
        _mix_sample_state_kernel,
        out_shape=(jax.ShapeDtypeStruct(state.shape, F32), jax.ShapeDtypeStruct((b, HG_WIDTH), F32)),
        grid=(b // bb,),
        in_specs=[s_spec, col_spec, col_spec, row_spec],
        out_specs=(s_spec, row_spec),
        compiler_params=pltpu.CompilerParams(dimension_semantics=("parallel",), vmem_limit_bytes=VMEM_LIMIT),
        name="mix_sample_state",
    )(state, ft, qt, v)

    x1, h2, lg = pl.pallas_call(
        _mix_sample_out_kernel,
        out_shape=(jax.ShapeDtypeStruct((b, D_MODEL), F32),
                   jax.ShapeDtypeStruct((b, D_MODEL), BF16),
                   jax.ShapeDtypeStruct((b, LANES), F32)),
        compiler_params=cp,
        name="mix_sample_out",
    )(x, mod, o, gate, ypool, gout, wout, gpost, gffn, wrh, wrl)
    return x1, h2, lg, npool, s_new


def _first_max(cur, idx, big):
    m = jnp.max(cur, axis=0, keepdims=True)
    first = jnp.min(jnp.where(cur == m, idx, big), axis=0, keepdims=True)
    return idx == first


def _router_kernel(lg_ref, bias_ref, wd_ref):
    n = lg_ref.shape[0]
    logit = lg_ref[...].T[0:N_EXPERTS, :]
    scores = jax.nn.sigmoid(logit)
    sel = scores + bias_ref[0:N_EXPERTS, :]
    neg = -jnp.inf

    sub = lax.broadcasted_iota(jnp.int32, (GROUP_SIZE, n), 0)
    gscore = []
    for g in range(N_GROUPS):
        sg = sel[g * GROUP_SIZE:(g + 1) * GROUP_SIZE, :]
        m1 = jnp.max(sg, axis=0, keepdims=True)
        rest = jnp.where(_first_max(sg, sub, GROUP_SIZE), neg, sg)
        gscore.append(m1 + jnp.max(rest, axis=0, keepdims=True))
    cur = jnp.concatenate(gscore, axis=0)
    gidx = lax.broadcasted_iota(jnp.int32, (N_GROUPS, n), 0)
    gmask = jnp.zeros((N_GROUPS, n), jnp.bool_)
    for _ in range(TOPK_GROUPS):
        pick = _first_max(cur, gidx, N_GROUPS)
        gmask = gmask | pick
        cur = jnp.where(pick, neg, cur)
    emask = jnp.concatenate(
        [jnp.broadcast_to(gmask[g:g + 1, :], (GROUP_SIZE, n)) for g in range(N_GROUPS)], axis=0)

    cur = jnp.where(emask, sel, neg)
    eidx = lax.broadcasted_iota(jnp.int32, (N_EXPERTS, n), 0)
    chosen = jnp.zeros((N_EXPERTS, n), jnp.bool_)
    for _ in range(TOP_K):
        pick = _first_max(cur, eidx, N_EXPERTS)
        chosen = chosen | pick
        cur = jnp.where(pick, neg, cur)
    wsum = jnp.sum(jnp.where(chosen, scores, 0.0), axis=0, keepdims=True)
    wd = jnp.where(chosen, scores / wsum * ROUTED_SCALE, 0.0)
    wd_ref[...] = jnp.concatenate([wd, jnp.zeros((LANES - N_EXPERTS, n), F32)], axis=0).T


def _router(logits, bias_col):
    t = logits.shape[0]
    tile = min(ROUTER_TILE, t)
    return pl.pallas_call(
        _router_kernel,
        out_shape=jax.ShapeDtypeStruct((t, LANES), F32),
        grid=(t // tile,),
        in_specs=[pl.BlockSpec((tile, LANES), lambda i: (i, 0)), _full(bias_col.shape)],
        out_specs=pl.BlockSpec((tile, LANES), lambda i: (i, 0)),
        compiler_params=pltpu.CompilerParams(dimension_semantics=("parallel",), vmem_limit_bytes=VMEM_LIMIT),
        name="router",
    )(logits, bias_col)


def _moe_kernel(h_ref, wd_ref, x1_ref, gt2_ref, gpost_ref, w1_ref, w3_ref, w2_ref, ws1_ref, ws3_ref, ws2_ref,
                out_ref, acc):
    e = pl.program_id(1)
    x = h_ref[...]

    @pl.when(e == 0)
    def _():
        hs = _silu(_dot(x, ws1_ref[...].astype(BF16))) * _dot(x, ws3_ref[...].astype(BF16))
        acc[...] = _dot(hs.astype(BF16), ws2_ref[...].astype(BF16))

    lane = lax.broadcasted_iota(jnp.int32, wd_ref.shape, 1)
    wcol = jnp.sum(jnp.where(lane == e, wd_ref[...], 0.0), axis=1, keepdims=True)
    act = _silu(_dot(x, w1_ref[0].astype(BF16))) * _dot(x, w3_ref[0].astype(BF16)) * wcol
    acc[...] += _dot(act.astype(BF16), w2_ref[0].astype(BF16))

    @pl.when(e == pl.num_programs(1) - 1)
    def _():
        out_ref[...] = x1_ref[...] + gt2_ref[0] * _rms(acc[...], gpost_ref[...])


def _moe(h2, wd, x1, gt2, rows_per_gt2, gpost, w1, w3, w2, ws1, ws3, ws2):
    t = h2.shape[0]
    tile = min(MOE_TILE, t)
    tok = lambda i, e: (i, 0)
    exp = lambda i, e: (e, 0, 0)
    return pl.pallas_call(
        _moe_kernel,
        out_shape=jax.ShapeDtypeStruct((t, D_MODEL), F32),
        grid=(t // tile, N_EXPERTS),
        in_specs=[pl.BlockSpec((tile, D_MODEL), tok),
                  pl.BlockSpec((tile, LANES), tok),
                  pl.BlockSpec((tile, D_MODEL), tok),
                  pl.BlockSpec((1,) + gt2.shape[1:], lambda i, e: (i // rows_per_gt2, 0, 0)),
                  _full((1, D_MODEL)),
                  pl.BlockSpec((1, D_MODEL, D_EXPERT), exp),
                  pl.BlockSpec((1, D_MODEL, D_EXPERT), exp),
                  pl.BlockSpec((1, D_EXPERT, D_MODEL), exp),
                  _full(ws1.shape), _full(ws3.shape), _full(ws2.shape)],
        out_specs=pl.BlockSpec((tile, D_MODEL), tok),
        scratch_shapes=[pltpu.VMEM((tile, D_MODEL), F32)],
        compiler_params=pltpu.CompilerParams(dimension_semantics=("parallel", "arbitrary"),
                                             vmem_limit_bytes=VMEM_LIMIT),
        name="moe",
    )(h2, wd, x1, gt2, gpost, w1, w3, w2, ws1, ws3, ws2)


def kernel(x_prompt, x_sample, c_prompt, c_sample, state_pool, state_hgrn, w_ada, b_ada, g_pre_mix, g_post_mix,
           w_in, w_pool, pool_scale, lb_logits, g_out_norm, w_out, g_pre_ffn, g_post_ffn, w_router, router_bias,
           w_exp_gate, w_exp_up, w_exp_down, w_sh_gate, w_sh_up, w_sh_down):
    assert w_ada.shape[0] == 1 and lb_logits.shape[0] == 2, "single-layer trunk"
    bp, lp, d = x_prompt.shape
    bs = x_sample.shape[0]
    row = lambda a: a[0].reshape(1, -1)

    n_mod = bp + bs
    pad = (-n_mod) % 16
    c_all = jnp.concatenate([c_prompt, c_sample, jnp.zeros((pad, d), F32)], axis=0)
    mod = _ada(c_all, w_ada[0], b_ada)
    mod_p = mod[:bp].reshape(bp, 1, 6 * d)
    mod_s = mod[bp:n_mod]

    win = w_in[0].astype(BF16)
    wout = w_out[0].astype(BF16)
    wpool = w_pool[0].astype(BF16)
    wr = jnp.pad(w_router[0], ((0, 0), (0, LANES - N_EXPERTS)))
    wrh = wr.astype(BF16)
    wrl = (wr - wrh.astype(F32)).astype(BF16)
    m2 = jnp.asarray(_M2_NP, BF16)
    gpre, gpost, gffn, gffn_post = row(g_pre_mix), row(g_post_mix), row(g_pre_ffn), row(g_post_ffn)
    pscale, gout = row(pool_scale), row(g_out_norm)
    bias_col = jnp.pad(router_bias[0], (0, LANES - N_EXPERTS)).reshape(LANES, 1)

    x1_p, h2_p, lg_p, pool_p, st_p = _mix_prompt(x_prompt, mod_p, gpre, gpost, gffn, win, wpool, pscale,
                                                  lb_logits, gout, wout, wrh, wrl, m2)
    x1_s, h2_s, lg_s, pool_s, st_s = _mix_sample(x_sample[:, 0, :], mod_s, state_pool[0], state_hgrn[0],
                                                  gpre, gpost, gffn, win, wpool, pscale, lb_logits, gout, wout,
                                                  wrh, wrl)

    experts = (w_exp_gate[0], w_exp_up[0], w_exp_down[0], w_sh_gate[0], w_sh_up[0], w_sh_down[0])
    tp = bp * lp
    wd_p = _router(lg_p.reshape(tp, LANES), bias_col)
    gt2_p = mod_p[:, :, 5 * d:]
    y_p = _moe(h2_p.reshape(tp, d), wd_p, x1_p.reshape(tp, d), gt2_p, lp // min(MOE_TILE, tp), gffn_post, *experts)
    wd_s = _router(lg_s, bias_col)
    gt2_s = mod_s[:, 5 * d:].reshape(1, bs, d)
    y_s = _moe(h2_s, wd_s, x1_s, gt2_s, 1, gffn_post, *experts)

    return (y_p.reshape(bp, lp, d), y_s.reshape(bs, 1, d), pool_p[None], st_p[None], pool_s[None], st_s[None])
```

```python
import functools

import numpy as np
import jax
import jax.numpy as jnp
from jax import lax
from jax.experimental import pallas as pl
from jax.experimental.pallas import tpu as pltpu

F32 = jnp.float32
BF16 = jnp.bfloat16

D_MODEL = 1024
POOL_WIDTH = 512
POOL_WINDOWS = (2, 4, 8, 16)
POOL_GROUP = 128
POOL_HIST = 15
HG_WIDTH = 512
HG_HEADS = 4
HG_HEAD_DIM = 128
IN_WIDTH = POOL_WIDTH + 4 * HG_WIDTH
N_EXPERTS = 64
TOP_K = 8
N_GROUPS = 8
TOPK_GROUPS = 4
GROUP_SIZE = N_EXPERTS // N_GROUPS
D_EXPERT = 256
ROUTED_SCALE = 2.5
EPS = 1e-6

LANES = 128
CHUNK = 64
TIME_TILE = 256
ADA_TILE_N = 512
ROUTER_TILE = 512
MOE_TILE = 1024
SAMPLE_STATE_BLOCK = 8
PLAN_WINDOW = 1024
RUN_ALIGN = 32
FFN_TILE = 160
REGION_ROWS = PLAN_WINDOW * TOP_K + N_EXPERTS * (RUN_ALIGN - 1) + FFN_TILE
ROW_WORDS = D_MODEL // 2
ROW_SUB = ROW_WORDS // LANES
COMBINE_TILE = 256
SCATTER_UNROLL = 4
VMEM_LIMIT = 58 * 1024 * 1024

_LEVELS = (64, 32, 16, 8, 4, 2)
_BLK_CUM = 0
_BLK_END = 1


def _level_blocks():
    c = CHUNK
    i = np.arange(c)[:, None]
    s = np.arange(c)[None, :]
    blocks = [(s <= i), (s > i)]
    index = {}
    for lvl in _LEVELS:
        seg = (i // lvl) * lvl
        mid = seg + lvl // 2
        ref = mid - 1
        index[lvl] = len(blocks)
        blocks.append((i >= mid) & (s > ref) & (s <= i))
        if lvl > 2:
            blocks.append((i < mid) & (s > i) & (s <= ref))
    m = np.concatenate(blocks, axis=0).astype(np.float32)
    return np.concatenate([m, m], axis=1), index


_M2_NP, _LEVEL_INDEX = _level_blocks()


def _dot(a, b):
    return jnp.dot(a, b, preferred_element_type=F32)


def _dot_nt(a, b):
    return lax.dot_general(a, b, (((1,), (1,)), ((), ())), preferred_element_type=F32)


def _split2(x):
    hi = x.astype(BF16)
    lo = (x - hi.astype(F32)).astype(BF16)
    return hi, lo


def _pack_pair(a, b):
    lo = lax.bitcast_convert_type(a.astype(BF16).astype(F32), jnp.uint32)
    hi = lax.bitcast_convert_type(b.astype(BF16).astype(F32), jnp.uint32)
    return (lo >> 16) | hi


def _unpack_pair(words):
    lo = lax.bitcast_convert_type(words << 16, F32)
    hi = lax.bitcast_convert_type(words & jnp.uint32(0xFFFF0000), F32)
    return lo, hi


def _silu(x):
    return x * jax.nn.sigmoid(x)


def _rms(x, g):
    return x * lax.rsqrt(jnp.mean(x * x, axis=-1, keepdims=True) + EPS) * g


def _mods(m):
    return [m[:, j * D_MODEL:(j + 1) * D_MODEL] for j in range(6)]


def _forget_lower_bound(lbl):
    mx = jnp.max(lbl, axis=0, keepdims=True)
    e = jnp.exp(lbl - mx)
    return e[0:1] / jnp.sum(e, axis=0, keepdims=True)


def _router_logits(h2, wrh, wrl):
    hi, lo = _split2(h2)
    return _dot(hi, wrh) + _dot(lo, wrh) + _dot(hi, wrl)


def _ada_kernel(c_ref, w_ref, b_ref, o_ref):
    a_hi, a_lo = _split2(_silu(c_ref[...]))
    w_hi, w_lo = _split2(w_ref[...])
    o_ref[...] = _dot(a_hi, w_hi) + _dot(a_lo, w_hi) + _dot(a_hi, w_lo) + b_ref[...]


def _ada(c_all, w_ada, b_ada):
    rows = c_all.shape[0]
    n = w_ada.shape[1]
    return pl.pallas_call(
        _ada_kernel,
        out_shape=jax.ShapeDtypeStruct((rows, n), F32),
        grid=(n // ADA_TILE_N,),
        in_specs=[pl.BlockSpec((rows, D_MODEL), lambda j: (0, 0)),
                  pl.BlockSpec((D_MODEL, ADA_TILE_N), lambda j: (0, j)),
                  pl.BlockSpec((1, ADA_TILE_N), lambda j: (0, j))],
        out_specs=pl.BlockSpec((rows, ADA_TILE_N), lambda j: (0, j)),
        compiler_params=pltpu.CompilerParams(dimension_semantics=("parallel",),
                                             vmem_limit_bytes=VMEM_LIMIT),
        name="ada",
    )(c_all, w_ada, b_ada)


def _level_masks():
    i = lax.broadcasted_iota(jnp.int32, (CHUNK, CHUNK), 0)
    j = lax.broadcasted_iota(jnp.int32, (CHUNK, CHUNK), 1)
    masks = {}
    for lvl in _LEVELS:
        sh = lvl.bit_length() - 1
        same = (i >> sh) == (j >> sh)
        upper = ((i >> (sh - 1)) & 1) == 1
        lower = ((j >> (sh - 1)) & 1) == 0
        masks[lvl] = same & upper & lower
    return masks, i == j


def _hgrn_chunk_head(q, k, v, e_all, lane0, st, masks, eye):
    def blk(n):
        return e_all[n * CHUNK:(n + 1) * CHUNK, lane0:lane0 + HG_HEAD_DIM]

    b = blk(_BLK_CUM)
    a = jnp.where(eye, _dot_nt(q.astype(BF16), k.astype(BF16)), 0.0)
    for lvl in _LEVELS:
        n = _LEVEL_INDEX[lvl]
        ql = (q * jnp.exp(blk(n))).astype(BF16)
        kl = (k * jnp.exp(blk(n + 1))).astype(BF16) if lvl > 2 else k.astype(BF16)
        a = a + jnp.where(masks[lvl], _dot_nt(ql, kl), 0.0)
    st_b = st.astype(BF16)
    o = _dot(a.astype(BF16), v.astype(BF16)) + _dot_nt((q * jnp.exp(b)).astype(BF16), st_b)
    k_end = (k * jnp.exp(blk(_BLK_END))).astype(BF16)
    decay = jnp.exp(b[CHUNK - 1:CHUNK, :])
    st_new = st * decay + _dot(v.T.astype(BF16), k_end)
    return o, st_new


def _mix_prompt_kernel(x_ref, mod_ref, gpre_ref, gpost_ref, gffn_ref, win_ref, wpool_ref, pscale_ref,
                       lbl_ref, gout_ref, wout_ref, wrh_ref, wrl_ref, m2_ref,
                       x1_ref, h2_ref, lg_ref, pool_ref, st_ref,
                       st_s, ubuf, q_s, k_s, v_s, g_s, o_s):
    t = pl.program_id(1)
    n_t = pl.num_programs(1)
    tt = TIME_TILE

    @pl.when(t == 0)
    def _():
        st_s[...] = jnp.zeros_like(st_s)
        ubuf[0:16, :] = jnp.zeros((16, POOL_WIDTH), F32)

    xt = x_ref[0]
    sh1, sc1, gt1, sh2, sc2, gt2 = _mods(mod_ref[0])
    h = _rms(xt, gpre_ref[...]) * (1.0 + sc1) + sh1
    proj = _dot(h.astype(BF16), win_ref[...])

    u = proj[:, :POOL_WIDTH]
    ubuf[16:16 + tt, :] = u
    pos = (t * tt + lax.broadcasted_iota(jnp.int32, (tt, 1), 0) + 1).astype(F32)
    ys = []
    for g, w in enumerate(POOL_WINDOWS):
        s = ubuf[:, g * POOL_GROUP:(g + 1) * POOL_GROUP]
        off = 0
        for step in range(w.bit_length() - 1):
            sh = 1 << step
            s = s[sh:, :] + s[:-sh, :]
            off += sh
        ws = s[16 - off:16 - off + tt, :]
        cnt = jnp.minimum(pos, float(w))
        d = ws / cnt - u[:, g * POOL_GROUP:(g + 1) * POOL_GROUP]
        ys.append(_dot(d.astype(BF16), wpool_ref[g]))
    y_pool = jnp.concatenate(ys, axis=1) * pscale_ref[...]
    ubuf[0:16, :] = ubuf[tt:tt + 16, :]

    @pl.when(t == n_t - 1)
    def _():
        pool_ref[0] = u[tt - POOL_HIST:, :]

    lb = _forget_lower_bound(lbl_ref[...])
    f = lb + (1.0 - lb) * jax.nn.sigmoid(proj[:, POOL_WIDTH + HG_WIDTH:POOL_WIDTH + 2 * HG_WIDTH])
    q_s[...] = _silu(proj[:, POOL_WIDTH:POOL_WIDTH + HG_WIDTH])
    k_s[...] = 1.0 - f
    v_s[...] = proj[:, POOL_WIDTH + 2 * HG_WIDTH:POOL_WIDTH + 3 * HG_WIDTH]
    g_s[...] = jnp.log(f)
    gate = _silu(proj[:, POOL_WIDTH + 3 * HG_WIDTH:])

    masks, eye = _level_masks()
    m2 = m2_ref[...]

    def chunk_body(c, carry):
        r0 = pl.multiple_of(c * CHUNK, CHUNK)
        g_hi, g_lo = _split2(g_s[pl.ds(r0, CHUNK), :])
        e_all = _dot(m2, jnp.concatenate([g_hi, g_lo], axis=0))
        for hd in range(HG_HEADS):
            lane0 = hd * HG_HEAD_DIM
            q = q_s[pl.ds(r0, CHUNK), lane0:lane0 + HG_HEAD_DIM]
            k = k_s[pl.ds(r0, CHUNK), lane0:lane0 + HG_HEAD_DIM]
            v = v_s[pl.ds(r0, CHUNK), lane0:lane0 + HG_HEAD_DIM]
            o, st_new = _hgrn_chunk_head(q, k, v, e_all, lane0, st_s[hd], masks, eye)
            st_s[hd] = st_new
            o_s[pl.ds(r0, CHUNK), lane0:lane0 + HG_HEAD_DIM] = o
        return carry

    lax.fori_loop(0, tt // CHUNK, chunk_body, 0)

    o = o_s[...]
    os_ = []
    for hd in range(HG_HEADS):
        oh = o[:, hd * HG_HEAD_DIM:(hd + 1) * HG_HEAD_DIM]
        os_.append(oh * lax.rsqrt(jnp.mean(oh * oh, axis=-1, keepdims=True) + EPS) * gout_ref[...])
    o_n = jnp.concatenate(os_, axis=1) * gate

    mix = _dot(y_pool.astype(BF16), wout_ref[0:POOL_WIDTH, :]) + _dot(o_n.astype(BF16), wout_ref[POOL_WIDTH:, :])
    x1 = xt + gt1 * _rms(mix, gpost_ref[...])
    h2 = _rms(x1, gffn_ref[...]) * (1.0 + sc2) + sh2
    x1_ref[0] = x1
    h2_ref[0] = _pack_pair(h2[:, :ROW_WORDS], h2[:, ROW_WORDS:])
    lg_ref[0] = _router_logits(h2, wrh_ref[...], wrl_ref[...])

    @pl.when(t == n_t - 1)
    def _():
        for hd in range(HG_HEADS):
            st_ref[0, hd] = st_s[hd].T


def _full(shape):
    nd = len(shape)
    return pl.BlockSpec(shape, lambda *_: (0,) * nd)


def _mix_prompt(x, mod, gpre, gpost, gffn, win, wpool, pscale, lbl, gout, wout, wrh, wrl, m2):
    b, l, d = x.shape
    tt = TIME_TILE
    tile = lambda i, j: (i, j, 0)
    per_b = lambda i, j: (i, 0, 0)
    return pl.pallas_call(
        _mix_prompt_kernel,
        out_shape=(jax.ShapeDtypeStruct((b, l, d), F32),
                   jax.ShapeDtypeStruct((b, l, ROW_WORDS), jnp.uint32),
                   jax.ShapeDtypeStruct((b, l, LANES), F32),
                   jax.ShapeDtypeStruct((b, POOL_HIST, POOL_WIDTH), F32),
                   jax.ShapeDtypeStruct((b, HG_HEADS, HG_HEAD_DIM, HG_HEAD_DIM), F32)),
        grid=(b, l // tt),
        in_specs=[pl.BlockSpec((1, tt, d), tile),
                  pl.BlockSpec((1, 1, 6 * d), per_b),
                  _full((1, d)), _full((1, d)), _full((1, d)),
                  _full(win.shape), _full(wpool.shape), _full((1, POOL_WIDTH)),
                  _full(lbl.shape), _full((1, HG_HEAD_DIM)), _full(wout.shape),
                  _full(wrh.shape), _full(wrl.shape), _full(m2.shape)],
        out_specs=(pl.BlockSpec((1, tt, d), tile),
                   pl.BlockSpec((1, tt, ROW_WORDS), tile),
                   pl.BlockSpec((1, tt, LANES), tile),
                   pl.BlockSpec((1, POOL_HIST, POOL_WIDTH), per_b),
                   pl.BlockSpec((1, HG_HEADS, HG_HEAD_DIM, HG_HEAD_DIM), lambda i, j: (i, 0, 0, 0))),
        scratch_shapes=[pltpu.VMEM((HG_HEADS, HG_HEAD_DIM, HG_HEAD_DIM), F32),
                        pltpu.VMEM((tt + 16, POOL_WIDTH), F32),
                        pltpu.VMEM((tt, HG_WIDTH), F32), pltpu.VMEM((tt, HG_WIDTH), F32),
                        pltpu.VMEM((tt, HG_WIDTH), F32), pltpu.VMEM((tt, HG_WIDTH), F32),
                        pltpu.VMEM((tt, HG_WIDTH), F32)],
        compiler_params=pltpu.CompilerParams(dimension_semantics=("parallel", "arbitrary"),
                                             vmem_limit_bytes=VMEM_LIMIT),
        name="mix_prompt",
    )(x, mod, gpre, gpost, gffn, win, wpool, pscale, lbl, gout, wout, wrh, wrl, m2)


def _mix_sample_in_kernel(x_ref, mod_ref, gpre_ref, win_ref, wpool_ref, pscale_ref, lbl_ref, hist_ref,
                          ypool_ref, npool_ref, ft_ref, qt_ref, v_ref, gate_ref):
    xt = x_ref[...]
    sh1, sc1 = _mods(mod_ref[...])[:2]
    h = _rms(xt, gpre_ref[...]) * (1.0 + sc1) + sh1
    proj = _dot(h.astype(BF16), win_ref[...])
    u = proj[:, :POOL_WIDTH]
    row = lax.broadcasted_iota(jnp.int32, (hist_ref.shape[0], POOL_HIST, POOL_GROUP), 1)
    ys = []
    for g, w in enumerate(POOL_WINDOWS):
        sl = slice(g * POOL_GROUP, (g + 1) * POOL_GROUP)
        past = jnp.sum(jnp.where(row >= POOL_HIST - (w - 1), hist_ref[:, :, sl], 0.0), axis=1)
        ug = u[:, sl]
        d = (past + ug) / float(w) - ug
        ys.append(_dot(d.astype(BF16), wpool_ref[g]))
    ypool_ref[...] = jnp.concatenate(ys, axis=1) * pscale_ref[...]
    npool_ref[:, 0:POOL_HIST - 1, :] = hist_ref[:, 1:POOL_HIST, :]
    npool_ref[:, POOL_HIST - 1, :] = u

    lb = _forget_lower_bound(lbl_ref[...])
    f = lb + (1.0 - lb) * jax.nn.sigmoid(proj[:, POOL_WIDTH + HG_WIDTH:POOL_WIDTH + 2 * HG_WIDTH])
    ft_ref[...] = f.T
    qt_ref[...] = _silu(proj[:, POOL_WIDTH:POOL_WIDTH + HG_WIDTH]).T
    v_ref[...] = proj[:, POOL_WIDTH + 2 * HG_WIDTH:POOL_WIDTH + 3 * HG_WIDTH]
    gate_ref[...] = _silu(proj[:, POOL_WIDTH + 3 * HG_WIDTH:])


def _mix_sample_state_kernel(s_ref, ft_ref, qt_ref, v_ref, snew_ref, o_ref):
    i = pl.program_id(0)
    lane = lax.broadcasted_iota(jnp.int32, (HG_HEAD_DIM, ft_ref.shape[1]), 1)
    for j in range(SAMPLE_STATE_BLOCK):
        mine = lane == i * SAMPLE_STATE_BLOCK + j
        for hd in range(HG_HEADS):
            r0 = hd * HG_HEAD_DIM
            f = jnp.sum(jnp.where(mine, ft_ref[r0:r0 + HG_HEAD_DIM, :], 0.0), axis=1, keepdims=True)
            q = jnp.sum(jnp.where(mine, qt_ref[r0:r0 + HG_HEAD_DIM, :], 0.0), axis=1, keepdims=True)
            v = v_ref[j:j + 1, r0:r0 + HG_HEAD_DIM]
            s_new = f * s_ref[j, hd] + (1.0 - f) * v
            snew_ref[j, hd] = s_new
            o_ref[j:j + 1, r0:r0 + HG_HEAD_DIM] = jnp.sum(q * s_new, axis=0, keepdims=True)


def _mix_sample_out_kernel(x_ref, mod_ref, o_ref, gate_ref, ypool_ref, gout_ref, wout_ref, gpost_ref,
                           gffn_ref, wrh_ref, wrl_ref, x1_ref, h2_ref, lg_ref):
    _, _, gt1, sh2, sc2, _ = _mods(mod_ref[...])
    o = o_ref[...]
    os_ = []
    for hd in range(HG_HEADS):
        oh = o[:, hd * HG_HEAD_DIM:(hd + 1) * HG_HEAD_DIM]
        os_.append(oh * lax.rsqrt(jnp.mean(oh * oh, axis=-1, keepdims=True) + EPS) * gout_ref[...])
    o_n = jnp.concatenate(os_, axis=1) * gate_ref[...]
    mix = (_dot(ypool_ref[...].astype(BF16), wout_ref[0:POOL_WIDTH, :])
           + _dot(o_n.astype(BF16), wout_ref[POOL_WIDTH:, :]))
    x1 = x_ref[...] + gt1 * _rms(mix, gpost_ref[...])
    h2 = _rms(x1, gffn_ref[...]) * (1.0 + sc2) + sh2
    x1_ref[...] = x1
    h2_ref[...] = h2.astype(BF16)
    lg_ref[...] = _router_logits(h2, wrh_ref[...], wrl_ref[...])


def _mix_sample(x, mod, hist, state, gpre, gpost, gffn, win, wpool, pscale, lbl, gout, wout, wrh, wrl):
    b = x.shape[0]
    bb = SAMPLE_STATE_BLOCK
    cp = pltpu.CompilerParams(vmem_limit_bytes=VMEM_LIMIT)
    ypool, npool, ft, qt, v, gate = pl.pallas_call(
        _mix_sample_in_kernel,
        out_shape=(jax.ShapeDtypeStruct((b, POOL_WIDTH), F32),
                   jax.ShapeDtypeStruct((b, POOL_HIST, POOL_WIDTH), F32),
                   jax.ShapeDtypeStruct((HG_WIDTH, b), F32),
                   jax.ShapeDtypeStruct((HG_WIDTH, b), F32),
                   jax.ShapeDtypeStruct((b, HG_WIDTH), F32),
                   jax.ShapeDtypeStruct((b, HG_WIDTH), F32)),
        compiler_params=cp,
        name="mix_sample_in",
    )(x, mod, gpre, win, wpool, pscale, lbl, hist)

    s_spec = pl.BlockSpec((bb, HG_HEADS, HG_HEAD_DIM, HG_HEAD_DIM), lambda i: (i, 0, 0, 0))
    col_spec = _full((HG_WIDTH, b))
    row_spec = pl.BlockSpec((bb, HG_WIDTH), lambda i: (i, 0))
    s_new, o = pl.pallas_call(
        _mix_sample_state_kernel,
        out_shape=(jax.ShapeDtypeStruct(state.shape, F32), jax.ShapeDtypeStruct((b, HG_WIDTH), F32)),
        grid=(b // bb,),
        in_specs=[s_spec, col_spec, col_spec, row_spec],
        out_specs=(s_spec, row_spec),
        compiler_params=pltpu.CompilerParams(dimension_semantics=("parallel",), vmem_limit_bytes=VMEM_LIMIT),
        name="mix_sample_state",
    )(state, ft, qt, v)

    x1, h2, lg = pl.pallas_call(
        _mix_sample_out_kernel,
        out_shape=(jax.ShapeDtypeStruct((b, D_MODEL), F32),
                   jax.ShapeDtypeStruct((b, D_MODEL), BF16),
                   jax.ShapeDtypeStruct((b, LANES), F32)),
        compiler_params=cp,
        name="mix_sample_out",
    )(x, mod, o, gate, ypool, gout, wout, gpost, gffn, wrh, wrl)
    return x1, h2, lg, npool, s_new


def _first_max(cur, idx, big):
    m = jnp.max(cur, axis=0, keepdims=True)
    first = jnp.min(jnp.where(cur == m, idx, big), axis=0, keepdims=True)
    return idx == first


def _route(logit, bias):
    n = logit.shape[1]
    scores = jax.nn.sigmoid(logit)
    sel = scores + bias
    neg = -jnp.inf

    sub = lax.broadcasted_iota(jnp.int32, (GROUP_SIZE, n), 0)
    gscore = []
    for g in range(N_GROUPS):
        sg = sel[g * GROUP_SIZE:(g + 1) * GROUP_SIZE, :]
        m1 = jnp.max(sg, axis=0, keepdims=True)
        rest = jnp.where(_first_max(sg, sub, GROUP_SIZE), neg, sg)
        gscore.append(m1 + jnp.max(rest, axis=0, keepdims=True))
    cur = jnp.concatenate(gscore, axis=0)
    gidx = lax.broadcasted_iota(jnp.int32, (N_GROUPS, n), 0)
    gmask = jnp.zeros((N_GROUPS, n), jnp.bool_)
    for _ in range(TOPK_GROUPS):
        pick = _first_max(cur, gidx, N_GROUPS)
        gmask = gmask | pick
        cur = jnp.where(pick, neg, cur)
    emask = jnp.concatenate(
        [jnp.broadcast_to(gmask[g:g + 1, :], (GROUP_SIZE, n)) for g in range(N_GROUPS)], axis=0)

    cur = jnp.where(emask, sel, neg)
    eidx = lax.broadcasted_iota(jnp.int32, (N_EXPERTS, n), 0)
    picks = []
    for _ in range(TOP_K):
        pick = _first_max(cur, eidx, N_EXPERTS)
        picks.append(pick)
        cur = jnp.where(pick, neg, cur)
    return scores, picks


def _any(masks):
    return functools.reduce(jnp.logical_or, masks)


def _router_kernel(lg_ref, bias_ref, wd_ref):
    n = lg_ref.shape[0]
    scores, picks = _route(lg_ref[...].T[0:N_EXPERTS, :], bias_ref[0:N_EXPERTS, :])
    chosen = _any(picks)
    wsum = jnp.sum(jnp.where(chosen, scores, 0.0), axis=0, keepdims=True)
    wd = jnp.where(chosen, scores / wsum * ROUTED_SCALE, 0.0)
    wd_ref[...] = jnp.concatenate([wd, jnp.zeros((LANES - N_EXPERTS, n), F32)], axis=0).T


def _router(logits, bias_col):
    t = logits.shape[0]
    tile = min(ROUTER_TILE, t)
    return pl.pallas_call(
        _router_kernel,
        out_shape=jax.ShapeDtypeStruct((t, LANES), F32),
        grid=(t // tile,),
        in_specs=[pl.BlockSpec((tile, LANES), lambda i: (i, 0)), _full(bias_col.shape)],
        out_specs=pl.BlockSpec((tile, LANES), lambda i: (i, 0)),
        compiler_params=pltpu.CompilerParams(dimension_semantics=("parallel",), vmem_limit_bytes=VMEM_LIMIT),
        name="router",
    )(logits, bias_col)


def _plan_kernel(lg_ref, bias_ref, upper_ref, dest_ref, wsel_ref, cnt_ref, loff_ref):
    scores, picks = _route(lg_ref[...].T[0:N_EXPERTS, :], bias_ref[0:N_EXPERTS, :])
    chosen = _any(picks)
    chosen_f = jnp.where(chosen, 1.0, 0.0)
    cnt = jnp.sum(chosen_f, axis=1, keepdims=True)
    units = jnp.floor((cnt + (RUN_ALIGN - 1)) / RUN_ALIGN)
    ei = lax.broadcasted_iota(jnp.int32, (N_EXPERTS, N_EXPERTS), 0)
    ej = lax.broadcasted_iota(jnp.int32, (N_EXPERTS, N_EXPERTS), 1)
    before = jnp.where(ej < ei, 1.0, 0.0).astype(BF16)
    loff = RUN_ALIGN * _dot(before, jnp.broadcast_to(units, (N_EXPERTS, LANES)).astype(BF16))[:, 0:1]
    slot = loff + _dot(chosen_f.astype(BF16), upper_ref[...])
    wsum = jnp.sum(jnp.where(chosen, scores, 0.0), axis=0, keepdims=True)
    pick_sum = lambda v: jnp.concatenate(
        [jnp.sum(jnp.where(p, v, 0.0), axis=0, keepdims=True) for p in picks], axis=0)
    dest_ref[...] = pick_sum(slot).astype(jnp.int32)
    wsel_ref[...] = pick_sum(scores) / wsum * ROUTED_SCALE
    cnt_ref[0] = jnp.broadcast_to(cnt, (N_EXPERTS, LANES))
    loff_ref[0] = jnp.broadcast_to(loff, (N_EXPERTS, LANES))


def _plan(logits, bias_col, upper):
    t = logits.shape[0]
    nw = t // PLAN_WINDOW
    per_w = pl.BlockSpec((1, N_EXPERTS, LANES), lambda w: (w, 0, 0))
    picks = pl.BlockSpec((TOP_K, PLAN_WINDOW), lambda w: (0, w))
    return pl.pallas_call(
        _plan_kernel,
        out_shape=(jax.ShapeDtypeStruct((TOP_K, t), jnp.int32), jax.ShapeDtypeStruct((TOP_K, t), F32),
                   jax.ShapeDtypeStruct((nw, N_EXPERTS, LANES), F32),
                   jax.ShapeDtypeStruct((nw, N_EXPERTS, LANES), F32)),
        grid=(nw,),
        in_specs=[pl.BlockSpec((PLAN_WINDOW, LANES), lambda w: (w, 0)), _full(bias_col.shape), _full(upper.shape)],
        out_specs=(picks, picks, per_w, per_w),
        compiler_params=pltpu.CompilerParams(dimension_semantics=("parallel",), vmem_limit_bytes=VMEM_LIMIT),
        name="plan",
    )(logits, bias_col, upper)


def _slot_rows(slot):
    return pl.ds(pl.multiple_of(slot * ROW_SUB, ROW_SUB), ROW_SUB)


def _dispatch_kernel(dest_ref, h_ref, x_ref, dest_s, rows_s, sem):
    plan_copy = pltpu.make_async_copy(dest_ref.at[0, 0], dest_s, sem)
    plan_copy.start()
    x_ref[...] = jnp.zeros_like(x_ref)
    for j in range(ROW_SUB):
        rows_s[pl.ds(j, PLAN_WINDOW, stride=ROW_SUB), :] = h_ref[:, j * LANES:(j + 1) * LANES]
    plan_copy.wait()

    def body(i, carry):
        for u in range(SCATTER_UNROLL):
            t = i * SCATTER_UNROLL + u
            row = rows_s[_slot_rows(t), :]
            for k in range(TOP_K):
                x_ref[_slot_rows(dest_s[k * PLAN_WINDOW + t]), :] = row
        return carry

    lax.fori_loop(0, PLAN_WINDOW // SCATTER_UNROLL, body, 0)


def _dispatch(dest_w, h2):
    nw = dest_w.shape[0]
    region = REGION_ROWS * ROW_SUB
    return pl.pallas_call(
        _dispatch_kernel,
        out_shape=jax.ShapeDtypeStruct((nw * region, LANES), jnp.uint32),
        grid=(nw,),
        in_specs=[pl.BlockSpec((1, 1, TOP_K * PLAN_WINDOW), lambda w: (w, 0, 0)),
                  pl.BlockSpec((PLAN_WINDOW, ROW_WORDS), lambda w: (w, 0))],
        out_specs=pl.BlockSpec((region, LANES), lambda w: (w, 0)),
        scratch_shapes=[pltpu.SMEM((TOP_K * PLAN_WINDOW,), jnp.int32),
                        pltpu.VMEM((PLAN_WINDOW * ROW_SUB, LANES), jnp.uint32),
                        pltpu.SemaphoreType.DMA],
        compiler_params=pltpu.CompilerParams(dimension_semantics=("arbitrary",), vmem_limit_bytes=VMEM_LIMIT),
        name="dispatch",
    )(dest_w, h2)


def _unpack_rows(ref, n):
    parts = [_unpack_pair(ref[pl.ds(j, n, stride=ROW_SUB), :]) for j in range(ROW_SUB)]
    return jnp.concatenate([p[0] for p in parts] + [p[1] for p in parts], axis=1)


def _ffn_kernel(te_ref, to_ref, tn_ref, x_hbm, w1_ref, w3_ref, w2_ref, y_hbm, w13_s, w2_s, xbuf, ybuf, rsem, sem):
    del x_hbm
    i = pl.program_id(0)
    last = pl.num_programs(0) - 1
    e = te_ref[i]
    block = RUN_ALIGN * ROW_SUB

    def fetch(item, start):
        src = y_hbm.at[pl.ds(pl.multiple_of(to_ref[item] * block, block), FFN_TILE * ROW_SUB)]
        cp = pltpu.make_async_copy(src, xbuf.at[item % 2], rsem.at[item % 2])
        if start:
            cp.start()
        else:
            cp.wait()

    @pl.when(i == 0)
    def _():
        fetch(0, True)

    nxt = jnp.minimum(i + 1, last)

    @pl.when((i < last) & (tn_ref[nxt] > 0))
    def _():
        fetch(nxt, True)

    def writeback(item, start):
        for b in range(FFN_TILE // RUN_ALIGN):
            @pl.when(b < tn_ref[item])
            def _():
                dst = pl.multiple_of(to_ref[item] * block, block) + b * block
                cp = pltpu.make_async_copy(ybuf.at[item % 2, pl.ds(b * block, block)],
                                           y_hbm.at[pl.ds(dst, block)], sem.at[item % 2])
                if start:
                    cp.start()
                else:
                    cp.wait()

    @pl.when(i >= 2)
    def _():
        writeback(i - 2, False)

    @pl.when((i == 0) | (e != te_ref[jnp.maximum(i - 1, 0)]))
    def _():
        w13_s[:, 0:D_EXPERT] = w1_ref[0].astype(BF16)
        w13_s[:, D_EXPERT:] = w3_ref[0].astype(BF16)
        w2_s[...] = w2_ref[0].astype(BF16)

    @pl.when(tn_ref[i] > 0)
    def _():
        fetch(i, False)
        x = _unpack_rows(xbuf.at[i % 2], FFN_TILE).astype(BF16)
        h = _dot(x, w13_s[...])
        act = _silu(h[:, :D_EXPERT]) * h[:, D_EXPERT:]
        y = _dot(act.astype(BF16), w2_s[...])
        out = ybuf.at[i % 2]
        for j in range(ROW_SUB):
            out[pl.ds(j, FFN_TILE, stride=ROW_SUB), :] = _pack_pair(
                y[:, j * LANES:(j + 1) * LANES], y[:, ROW_WORDS + j * LANES:ROW_WORDS + (j + 1) * LANES])
        writeback(i, True)

    @pl.when(i == last)
    def _():
        @pl.when(i >= 1)
        def _():
            writeback(i - 1, False)
        writeback(i, False)


def _expert_ffn(item_expert, item_off, item_blocks, x_slots, w1, w3, w2):
    of_expert = lambda i, te, to, tn: (te[i], 0, 0)
    tile = pltpu.VMEM((2, FFN_TILE * ROW_SUB, LANES), jnp.uint32)
    grid_spec = pltpu.PrefetchScalarGridSpec(
        num_scalar_prefetch=3,
        grid=(item_expert.shape[0],),
        in_specs=[pl.BlockSpec(memory_space=pl.ANY),
                  pl.BlockSpec((1, D_MODEL, D_EXPERT), of_expert),
                  pl.BlockSpec((1, D_MODEL, D_EXPERT), of_expert),
                  pl.BlockSpec((1, D_EXPERT, D_MODEL), of_expert)],
        out_specs=pl.BlockSpec(memory_space=pl.ANY),
        scratch_shapes=[pltpu.VMEM((D_MODEL, 2 * D_EXPERT), BF16), pltpu.VMEM((D_EXPERT, D_MODEL), BF16),
                        tile, tile, pltpu.SemaphoreType.DMA((2,)), pltpu.SemaphoreType.DMA((2,))])
    return pl.pallas_call(
        _ffn_kernel,
        out_shape=jax.ShapeDtypeStruct(x_slots.shape, jnp.uint32),
        grid_spec=grid_spec,
        input_output_aliases={3: 0},
        compiler_params=pltpu.CompilerParams(dimension_semantics=("arbitrary",), vmem_limit_bytes=VMEM_LIMIT),
        name="expert_ffn",
    )(item_expert, item_off, item_blocks, x_slots, w1, w3, w2)


def _work_items(cnt, loff):
    nw = cnt.shape[0]
    cnt = cnt.astype(jnp.int32).T.reshape(-1)
    loff = loff.astype(jnp.int32).T.reshape(-1)
    n_items = nw * N_EXPERTS + (nw * PLAN_WINDOW * TOP_K) // FFN_TILE
    tiles = (cnt + FFN_TILE - 1) // FFN_TILE
    ends = jnp.cumsum(tiles)
    total = ends[-1]
    i = jnp.arange(n_items, dtype=jnp.int32)
    ic = jnp.minimum(i, total - 1)
    p = jnp.searchsorted(ends, ic, side="right").astype(jnp.int32)
    r = ic - (ends[p] - tiles[p])
    row = (p % nw) * REGION_ROWS + loff[p] + r * FFN_TILE
    own = jnp.minimum(cnt[p] - r * FFN_TILE, FFN_TILE)
    blocks = jnp.where(i < total, (own + RUN_ALIGN - 1) // RUN_ALIGN, 0)
    return p // nw, row // RUN_ALIGN, blocks.astype(jnp.int32)


def _combine_kernel(dest_ref, wsel_ref, y_ref, h_ref, x1_ref, gt2_ref, gpost_ref, ws13_ref, ws2_ref, out_ref,
                    dest_s, wsel_s, lo_s, hi_s, sem):
    s = pl.program_id(1)

    @pl.when(s == 0)
    def _():
        copies = (pltpu.make_async_copy(dest_ref.at[0, 0], dest_s, sem.at[0]),
                  pltpu.make_async_copy(wsel_ref.at[0, 0], wsel_s, sem.at[1]))
        for c in copies:
            c.start()
        for c in copies:
            c.wait()

    base = s * COMBINE_TILE

    def body(i, carry):
        for u in range(SCATTER_UNROLL):
            t = i * SCATTER_UNROLL + u
            acc_lo = jnp.zeros((ROW_SUB, LANES), F32)
            acc_hi = jnp.zeros((ROW_SUB, LANES), F32)
            for k in range(TOP_K):
                at = k * PLAN_WINDOW + base + t
                lo, hi = _unpack_pair(y_ref[_slot_rows(dest_s[at]), :])
                w = wsel_s[at]
                acc_lo = acc_lo + w * lo
                acc_hi = acc_hi + w * hi
            lo_s[_slot_rows(t), :] = acc_lo
            hi_s[_slot_rows(t), :] = acc_hi
        return carry

    lax.fori_loop(0, COMBINE_TILE // SCATTER_UNROLL, body, 0)
    rows = lambda ref: [ref[pl.ds(j, COMBINE_TILE, stride=ROW_SUB), :] for j in range(ROW_SUB)]
    routed = jnp.concatenate(rows(lo_s) + rows(hi_s), axis=1)
    h_lo, h_hi = _unpack_pair(h_ref[...])
    x = jnp.concatenate([h_lo, h_hi], axis=1).astype(BF16)
    hs = _dot(x, ws13_ref[...])
    act = _silu(hs[:, :D_EXPERT]) * hs[:, D_EXPERT:]
    ff = routed + _dot(act.astype(BF16), ws2_ref[...])
    out_ref[...] = x1_ref[...] + gt2_ref[0] * _rms(ff, gpost_ref[...])


def _combine(dest_w, wsel_w, y_slots, h2, x1, gt2, tokens_per_gt2, gpost, ws13, ws2):
    nw = dest_w.shape[0]
    sub = PLAN_WINDOW // COMBINE_TILE
    plan = pl.BlockSpec((1, 1, TOP_K * PLAN_WINDOW), lambda w, s: (w, 0, 0))
    tok = lambda w, s: (w * sub + s, 0)
    return pl.pallas_call(
        _combine_kernel,
        out_shape=jax.ShapeDtypeStruct(x1.shape, F32),
        grid=(nw, sub),
        in_specs=[plan, plan,
                  pl.BlockSpec((REGION_ROWS * ROW_SUB, LANES), lambda w, s: (w, 0)),
                  pl.BlockSpec((COMBINE_TILE, ROW_WORDS), tok),
                  pl.BlockSpec((COMBINE_TILE, D_MODEL), tok),
                  pl.BlockSpec((1, 1, D_MODEL), lambda w, s: ((w * PLAN_WINDOW) // tokens_per_gt2, 0, 0)),
                  _full((1, D_MODEL)), _full(ws13.shape), _full(ws2.shape)],
        out_specs=pl.BlockSpec((COMBINE_TILE, D_MODEL), tok),
        scratch_shapes=[pltpu.SMEM((TOP_K * PLAN_WINDOW,), jnp.int32),
                        pltpu.SMEM((TOP_K * PLAN_WINDOW,), F32),
                        pltpu.VMEM((COMBINE_TILE * ROW_SUB, LANES), F32),
                        pltpu.VMEM((COMBINE_TILE * ROW_SUB, LANES), F32),
                        pltpu.SemaphoreType.DMA((2,))],
        compiler_params=pltpu.CompilerParams(dimension_semantics=("arbitrary", "arbitrary"),
                                             vmem_limit_bytes=VMEM_LIMIT),
        name="combine",
    )(dest_w, wsel_w, y_slots, h2, x1, gt2, gpost, ws13, ws2)


def _moe_kernel(h_ref, wd_ref, x1_ref, gt2_ref, gpost_ref, w1_ref, w3_ref, w2_ref, ws1_ref, ws3_ref, ws2_ref,
                out_ref, acc):
    e = pl.program_id(1)
    x = h_ref[...]

    @pl.when(e == 0)
    def _():
        hs = _silu(_dot(x, ws1_ref[...].astype(BF16))) * _dot(x, ws3_ref[...].astype(BF16))
        acc[...] = _dot(hs.astype(BF16), ws2_ref[...].astype(BF16))

    lane = lax.broadcasted_iota(jnp.int32, wd_ref.shape, 1)
    wcol = jnp.sum(jnp.where(lane == e, wd_ref[...], 0.0), axis=1, keepdims=True)
    act = _silu(_dot(x, w1_ref[0].astype(BF16))) * _dot(x, w3_ref[0].astype(BF16)) * wcol
    acc[...] += _dot(act.astype(BF16), w2_ref[0].astype(BF16))

    @pl.when(e == pl.num_programs(1) - 1)
    def _():
        out_ref[...] = x1_ref[...] + gt2_ref[0] * _rms(acc[...], gpost_ref[...])


def _moe(h2, wd, x1, gt2, rows_per_gt2, gpost, w1, w3, w2, ws1, ws3, ws2):
    t = h2.shape[0]
    tile = min(MOE_TILE, t)
    tok = lambda i, e: (i, 0)
    exp = lambda i, e: (e, 0, 0)
    return pl.pallas_call(
        _moe_kernel,
        out_shape=jax.ShapeDtypeStruct((t, D_MODEL), F32),
        grid=(t // tile, N_EXPERTS),
        in_specs=[pl.BlockSpec((tile, D_MODEL), tok),
                  pl.BlockSpec((tile, LANES), tok),
                  pl.BlockSpec((tile, D_MODEL), tok),
                  pl.BlockSpec((1,) + gt2.shape[1:], lambda i, e: (i // rows_per_gt2, 0, 0)),
                  _full((1, D_MODEL)),
                  pl.BlockSpec((1, D_MODEL, D_EXPERT), exp),
                  pl.BlockSpec((1, D_MODEL, D_EXPERT), exp),
                  pl.BlockSpec((1, D_EXPERT, D_MODEL), exp),
                  _full(ws1.shape), _full(ws3.shape), _full(ws2.shape)],
        out_specs=pl.BlockSpec((tile, D_MODEL), tok),
        scratch_shapes=[pltpu.VMEM((tile, D_MODEL), F32)],
        compiler_params=pltpu.CompilerParams(dimension_semantics=("parallel", "arbitrary"),
                                             vmem_limit_bytes=VMEM_LIMIT),
        name="moe",
    )(h2, wd, x1, gt2, gpost, w1, w3, w2, ws1, ws3, ws2)


def kernel(x_prompt, x_sample, c_prompt, c_sample, state_pool, state_hgrn, w_ada, b_ada, g_pre_mix, g_post_mix,
           w_in, w_pool, pool_scale, lb_logits, g_out_norm, w_out, g_pre_ffn, g_post_ffn, w_router, router_bias,
           w_exp_gate, w_exp_up, w_exp_down, w_sh_gate, w_sh_up, w_sh_down):
    assert w_ada.shape[0] == 1 and lb_logits.shape[0] == 2, "single-layer trunk"
    bp, lp, d = x_prompt.shape
    bs = x_sample.shape[0]
    row = lambda a: a[0].reshape(1, -1)

    n_mod = bp + bs
    pad = (-n_mod) % 16
    c_all = jnp.concatenate([c_prompt, c_sample, jnp.zeros((pad, d), F32)], axis=0)
    mod = _ada(c_all, w_ada[0], b_ada)
    mod_p = mod[:bp].reshape(bp, 1, 6 * d)
    mod_s = mod[bp:n_mod]

    win = w_in[0].astype(BF16)
    wout = w_out[0].astype(BF16)
    wpool = w_pool[0].astype(BF16)
    wr = jnp.pad(w_router[0], ((0, 0), (0, LANES - N_EXPERTS)))
    wrh = wr.astype(BF16)
    wrl = (wr - wrh.astype(F32)).astype(BF16)
    m2 = jnp.asarray(_M2_NP, BF16)
    gpre, gpost, gffn, gffn_post = row(g_pre_mix), row(g_post_mix), row(g_pre_ffn), row(g_post_ffn)
    pscale, gout = row(pool_scale), row(g_out_norm)
    bias_col = jnp.pad(router_bias[0], (0, LANES - N_EXPERTS)).reshape(LANES, 1)

    x1_p, h2_p, lg_p, pool_p, st_p = _mix_prompt(x_prompt, mod_p, gpre, gpost, gffn, win, wpool, pscale,
                                                  lb_logits, gout, wout, wrh, wrl, m2)
    x1_s, h2_s, lg_s, pool_s, st_s = _mix_sample(x_sample[:, 0, :], mod_s, state_pool[0], state_hgrn[0],
                                                  gpre, gpost, gffn, win, wpool, pscale, lb_logits, gout, wout,
                                                  wrh, wrl)

    experts = (w_exp_gate[0], w_exp_up[0], w_exp_down[0], w_sh_gate[0], w_sh_up[0], w_sh_down[0])
    tp = bp * lp
    nw = tp // PLAN_WINDOW
    assert lp % PLAN_WINDOW == 0
    upper = jnp.asarray(np.triu(np.ones((PLAN_WINDOW, PLAN_WINDOW), np.float32), 1), BF16)
    dest, wsel, cnt, loff = _plan(lg_p.reshape(tp, LANES), bias_col, upper)
    per_window = lambda a: a.reshape(TOP_K, nw, PLAN_WINDOW).transpose(1, 0, 2).reshape(nw, 1, TOP_K * PLAN_WINDOW)
    dest_w, wsel_w = per_window(dest), per_window(wsel)
    h2_rows = h2_p.reshape(tp, ROW_WORDS)
    x_slots = _dispatch(dest_w, h2_rows)
    y_slots = _expert_ffn(*_work_items(cnt[:, :, 0], loff[:, :, 0]), x_slots, *experts[:3])
    ws13 = jnp.concatenate([experts[3], experts[4]], axis=1).astype(BF16)
    gt2_p = mod_p[:, :, 5 * d:]
    y_p = _combine(dest_w, wsel_w, y_slots, h2_rows, x1_p.reshape(tp, d), gt2_p, lp, gffn_post,
                   ws13, experts[5].astype(BF16))
    wd_s = _router(lg_s, bias_col)
    gt2_s = mod_s[:, 5 * d:].reshape(1, bs, d)
    y_s = _moe(h2_s, wd_s, x1_s, gt2_s, 1, gffn_post, *experts)

    return (y_p.reshape(bp, lp, d), y_s.reshape(bs, 1, d), pool_p[None], st_p[None], pool_s[None], st_s[None])
```

```python
import functools

import numpy as np
import jax
import jax.numpy as jnp
from jax import lax
from jax.experimental import pallas as pl
from jax.experimental.pallas import tpu as pltpu

F32 = jnp.float32
BF16 = jnp.bfloat16

D_MODEL = 1024
POOL_WIDTH = 512
POOL_WINDOWS = (2, 4, 8, 16)
POOL_GROUP = 128
POOL_HIST = 15
HG_WIDTH = 512
HG_HEADS = 4
HG_HEAD_DIM = 128
IN_WIDTH = POOL_WIDTH + 4 * HG_WIDTH
N_EXPERTS = 64
TOP_K = 8
N_GROUPS = 8
TOPK_GROUPS = 4
GROUP_SIZE = N_EXPERTS // N_GROUPS
D_EXPERT = 256
ROUTED_SCALE = 2.5
EPS = 1e-6

LANES = 128
CHUNK = 64
TIME_TILE = 256
ADA_TILE_N = 512
ROUTER_TILE = 512
MOE_TILE = 1024
SAMPLE_STATE_BLOCK = 8
PLAN_WINDOW = 1024
RUN_ALIGN = 32
FFN_TILE = 160
REGION_ROWS = PLAN_WINDOW * TOP_K + N_EXPERTS * (RUN_ALIGN - 1) + FFN_TILE
ROW_WORDS = D_MODEL // 2
ROW_SUB = ROW_WORDS // LANES
COMBINE_TILE = 256
SCATTER_UNROLL = 4
FFN_DEPTH = 3
VMEM_LIMIT = 58 * 1024 * 1024

_LEVELS = (64, 32, 16, 8, 4, 2)
_BLK_CUM = 0
_BLK_END = 1


def _level_blocks():
    c = CHUNK
    i = np.arange(c)[:, None]
    s = np.arange(c)[None, :]
    blocks = [(s <= i), (s > i)]
    index = {}
    for lvl in _LEVELS:
        seg = (i // lvl) * lvl
        mid = seg + lvl // 2
        ref = mid - 1
        index[lvl] = len(blocks)
        blocks.append((i >= mid) & (s > ref) & (s <= i))
        if lvl > 2:
            blocks.append((i < mid) & (s > i) & (s <= ref))
    m = np.concatenate(blocks, axis=0).astype(np.float32)
    return np.concatenate([m, m], axis=1), index


_M2_NP, _LEVEL_INDEX = _level_blocks()


def _dot(a, b):
    return jnp.dot(a, b, preferred_element_type=F32)


def _dot_nt(a, b):
    return lax.dot_general(a, b, (((1,), (1,)), ((), ())), preferred_element_type=F32)


def _split2(x):
    hi = x.astype(BF16)
    lo = (x - hi.astype(F32)).astype(BF16)
    return hi, lo


def _pack_pair(a, b):
    lo = lax.bitcast_convert_type(a.astype(BF16).astype(F32), jnp.uint32)
    hi = lax.bitcast_convert_type(b.astype(BF16).astype(F32), jnp.uint32)
    return (lo >> 16) | hi


def _unpack_pair(words):
    lo = lax.bitcast_convert_type(words << 16, F32)
    hi = lax.bitcast_convert_type(words & jnp.uint32(0xFFFF0000), F32)
    return lo, hi


def _silu(x):
    return x * jax.nn.sigmoid(x)


def _rms(x, g):
    return x * lax.rsqrt(jnp.mean(x * x, axis=-1, keepdims=True) + EPS) * g


def _mods(m):
    return [m[:, j * D_MODEL:(j + 1) * D_MODEL] for j in range(6)]


def _forget_lower_bound(lbl):
    mx = jnp.max(lbl, axis=0, keepdims=True)
    e = jnp.exp(lbl - mx)
    return e[0:1] / jnp.sum(e, axis=0, keepdims=True)


def _router_logits(h2, wrh, wrl):
    hi, lo = _split2(h2)
    return _dot(hi, wrh) + _dot(lo, wrh) + _dot(hi, wrl)


def _ada_kernel(c_ref, w_ref, b_ref, o_ref):
    a_hi, a_lo = _split2(_silu(c_ref[...]))
    w_hi, w_lo = _split2(w_ref[...])
    o_ref[...] = _dot(a_hi, w_hi) + _dot(a_lo, w_hi) + _dot(a_hi, w_lo) + b_ref[...]


def _ada(c_all, w_ada, b_ada):
    rows = c_all.shape[0]
    n = w_ada.shape[1]
    return pl.pallas_call(
        _ada_kernel,
        out_shape=jax.ShapeDtypeStruct((rows, n), F32),
        grid=(n // ADA_TILE_N,),
        in_specs=[pl.BlockSpec((rows, D_MODEL), lambda j: (0, 0)),
                  pl.BlockSpec((D_MODEL, ADA_TILE_N), lambda j: (0, j)),
                  pl.BlockSpec((1, ADA_TILE_N), lambda j: (0, j))],
        out_specs=pl.BlockSpec((rows, ADA_TILE_N), lambda j: (0, j)),
        compiler_params=pltpu.CompilerParams(dimension_semantics=("parallel",),
                                             vmem_limit_bytes=VMEM_LIMIT),
        name="ada",
    )(c_all, w_ada, b_ada)


def _level_masks():
    i = lax.broadcasted_iota(jnp.int32, (CHUNK, CHUNK), 0)
    j = lax.broadcasted_iota(jnp.int32, (CHUNK, CHUNK), 1)
    masks = {}
    for lvl in _LEVELS:
        sh = lvl.bit_length() - 1
        same = (i >> sh) == (j >> sh)
        upper = ((i >> (sh - 1)) & 1) == 1
        lower = ((j >> (sh - 1)) & 1) == 0
        masks[lvl] = same & upper & lower
    return masks, i == j


def _hgrn_chunk_head(q, k, v, e_all, lane0, st, masks, eye):
    def blk(n):
        return e_all[n * CHUNK:(n + 1) * CHUNK, lane0:lane0 + HG_HEAD_DIM]

    b = blk(_BLK_CUM)
    a = jnp.where(eye, _dot_nt(q.astype(BF16), k.astype(BF16)), 0.0)
    for lvl in _LEVELS:
        n = _LEVEL_INDEX[lvl]
        ql = (q * jnp.exp(blk(n))).astype(BF16)
        kl = (k * jnp.exp(blk(n + 1))).astype(BF16) if lvl > 2 else k.astype(BF16)
        a = a + jnp.where(masks[lvl], _dot_nt(ql, kl), 0.0)
    st_b = st.astype(BF16)
    o = _dot(a.astype(BF16), v.astype(BF16)) + _dot_nt((q * jnp.exp(b)).astype(BF16), st_b)
    k_end = (k * jnp.exp(blk(_BLK_END))).astype(BF16)
    decay = jnp.exp(b[CHUNK - 1:CHUNK, :])
    st_new = st * decay + _dot(v.T.astype(BF16), k_end)
    return o, st_new


def _mix_prompt_kernel(x_ref, mod_ref, gpre_ref, gpost_ref, gffn_ref, win_ref, wpool_ref, pscale_ref,
                       lbl_ref, gout_ref, wout_ref, wrh_ref, wrl_ref, m2_ref,
                       x1_ref, h2_ref, lg_ref, pool_ref, st_ref,
                       st_s, ubuf, q_s, k_s, v_s, g_s, o_s):
    t = pl.program_id(1)
    n_t = pl.num_programs(1)
    tt = TIME_TILE

    @pl.when(t == 0)
    def _():
        st_s[...] = jnp.zeros_like(st_s)
        ubuf[0:16, :] = jnp.zeros((16, POOL_WIDTH), F32)

    xt = x_ref[0]
    sh1, sc1, gt1, sh2, sc2, gt2 = _mods(mod_ref[0])
    h = _rms(xt, gpre_ref[...]) * (1.0 + sc1) + sh1
    proj = _dot(h.astype(BF16), win_ref[...])

    u = proj[:, :POOL_WIDTH]
    ubuf[16:16 + tt, :] = u
    pos = (t * tt + lax.broadcasted_iota(jnp.int32, (tt, 1), 0) + 1).astype(F32)
    ys = []
    for g, w in enumerate(POOL_WINDOWS):
        s = ubuf[:, g * POOL_GROUP:(g + 1) * POOL_GROUP]
        off = 0
        for step in range(w.bit_length() - 1):
            sh = 1 << step
            s = s[sh:, :] + s[:-sh, :]
            off += sh
        ws = s[16 - off:16 - off + tt, :]
        cnt = jnp.minimum(pos, float(w))
        d = ws / cnt - u[:, g * POOL_GROUP:(g + 1) * POOL_GROUP]
        ys.append(_dot(d.astype(BF16), wpool_ref[g]))
    y_pool = jnp.concatenate(ys, axis=1) * pscale_ref[...]
    ubuf[0:16, :] = ubuf[tt:tt + 16, :]

    @pl.when(t == n_t - 1)
    def _():
        pool_ref[0] = u[tt - POOL_HIST:, :]

    lb = _forget_lower_bound(lbl_ref[...])
    f = lb + (1.0 - lb) * jax.nn.sigmoid(proj[:, POOL_WIDTH + HG_WIDTH:POOL_WIDTH + 2 * HG_WIDTH])
    q_s[...] = _silu(proj[:, POOL_WIDTH:POOL_WIDTH + HG_WIDTH])
    k_s[...] = 1.0 - f
    v_s[...] = proj[:, POOL_WIDTH + 2 * HG_WIDTH:POOL_WIDTH + 3 * HG_WIDTH]
    g_s[...] = jnp.log(f)
    gate = _silu(proj[:, POOL_WIDTH + 3 * HG_WIDTH:])

    masks, eye = _level_masks()
    m2 = m2_ref[...]

    def chunk_body(c, carry):
        r0 = pl.multiple_of(c * CHUNK, CHUNK)
        g_hi, g_lo = _split2(g_s[pl.ds(r0, CHUNK), :])
        e_all = _dot(m2, jnp.concatenate([g_hi, g_lo], axis=0))
        for hd in range(HG_HEADS):
            lane0 = hd * HG_HEAD_DIM
            q = q_s[pl.ds(r0, CHUNK), lane0:lane0 + HG_HEAD_DIM]
            k = k_s[pl.ds(r0, CHUNK), lane0:lane0 + HG_HEAD_DIM]
            v = v_s[pl.ds(r0, CHUNK), lane0:lane0 + HG_HEAD_DIM]
            o, st_new = _hgrn_chunk_head(q, k, v, e_all, lane0, st_s[hd], masks, eye)
            st_s[hd] = st_new
            o_s[pl.ds(r0, CHUNK), lane0:lane0 + HG_HEAD_DIM] = o
        return carry

    lax.fori_loop(0, tt // CHUNK, chunk_body, 0)

    o = o_s[...]
    os_ = []
    for hd in range(HG_HEADS):
        oh = o[:, hd * HG_HEAD_DIM:(hd + 1) * HG_HEAD_DIM]
        os_.append(oh * lax.rsqrt(jnp.mean(oh * oh, axis=-1, keepdims=True) + EPS) * gout_ref[...])
    o_n = jnp.concatenate(os_, axis=1) * gate

    mix = _dot(y_pool.astype(BF16), wout_ref[0:POOL_WIDTH, :]) + _dot(o_n.astype(BF16), wout_ref[POOL_WIDTH:, :])
    x1 = xt + gt1 * _rms(mix, gpost_ref[...])
    h2 = _rms(x1, gffn_ref[...]) * (1.0 + sc2) + sh2
    x1_ref[0] = x1
    h2_ref[0] = _pack_pair(h2[:, :ROW_WORDS], h2[:, ROW_WORDS:])
    lg_ref[0] = _router_logits(h2, wrh_ref[...], wrl_ref[...])

    @pl.when(t == n_t - 1)
    def _():
        for hd in range(HG_HEADS):
            st_ref[0, hd] = st_s[hd].T


def _full(shape):
    nd = len(shape)
    return pl.BlockSpec(shape, lambda *_: (0,) * nd)


def _mix_prompt(x, mod, gpre, gpost, gffn, win, wpool, pscale, lbl, gout, wout, wrh, wrl, m2):
    b, l, d = x.shape
    tt = TIME_TILE
    tile = lambda i, j: (i, j, 0)
    per_b = lambda i, j: (i, 0, 0)
    return pl.pallas_call(
        _mix_prompt_kernel,
        out_shape=(jax.ShapeDtypeStruct((b, l, d), F32),
                   jax.ShapeDtypeStruct((b, l, ROW_WORDS), jnp.uint32),
                   jax.ShapeDtypeStruct((b, l, LANES), F32),
                   jax.ShapeDtypeStruct((b, POOL_HIST, POOL_WIDTH), F32),
                   jax.ShapeDtypeStruct((b, HG_HEADS, HG_HEAD_DIM, HG_HEAD_DIM), F32)),
        grid=(b, l // tt),
        in_specs=[pl.BlockSpec((1, tt, d), tile),
                  pl.BlockSpec((1, 1, 6 * d), per_b),
                  _full((1, d)), _full((1, d)), _full((1, d)),
                  _full(win.shape), _full(wpool.shape), _full((1, POOL_WIDTH)),
                  _full(lbl.shape), _full((1, HG_HEAD_DIM)), _full(wout.shape),
                  _full(wrh.shape), _full(wrl.shape), _full(m2.shape)],
        out_specs=(pl.BlockSpec((1, tt, d), tile),
                   pl.BlockSpec((1, tt, ROW_WORDS), tile),
                   pl.BlockSpec((1, tt, LANES), tile),
                   pl.BlockSpec((1, POOL_HIST, POOL_WIDTH), per_b),
                   pl.BlockSpec((1, HG_HEADS, HG_HEAD_DIM, HG_HEAD_DIM), lambda i, j: (i, 0, 0, 0))),
        scratch_shapes=[pltpu.VMEM((HG_HEADS, HG_HEAD_DIM, HG_HEAD_DIM), F32),
                        pltpu.VMEM((tt + 16, POOL_WIDTH), F32),
                        pltpu.VMEM((tt, HG_WIDTH), F32), pltpu.VMEM((tt, HG_WIDTH), F32),
                        pltpu.VMEM((tt, HG_WIDTH), F32), pltpu.VMEM((tt, HG_WIDTH), F32),
                        pltpu.VMEM((tt, HG_WIDTH), F32)],
        compiler_params=pltpu.CompilerParams(dimension_semantics=("parallel", "arbitrary"),
                                             vmem_limit_bytes=VMEM_LIMIT),
        name="mix_prompt",
    )(x, mod, gpre, gpost, gffn, win, wpool, pscale, lbl, gout, wout, wrh, wrl, m2)


def _mix_sample_in_kernel(x_ref, mod_ref, gpre_ref, win_ref, wpool_ref, pscale_ref, lbl_ref, hist_ref,
                          ypool_ref, npool_ref, ft_ref, qt_ref, v_ref, gate_ref):
    xt = x_ref[...]
    sh1, sc1 = _mods(mod_ref[...])[:2]
    h = _rms(xt, gpre_ref[...]) * (1.0 + sc1) + sh1
    proj = _dot(h.astype(BF16), win_ref[...])
    u = proj[:, :POOL_WIDTH]
    row = lax.broadcasted_iota(jnp.int32, (hist_ref.shape[0], POOL_HIST, POOL_GROUP), 1)
    ys = []
    for g, w in enumerate(POOL_WINDOWS):
        sl = slice(g * POOL_GROUP, (g + 1) * POOL_GROUP)
        past = jnp.sum(jnp.where(row >= POOL_HIST - (w - 1), hist_ref[:, :, sl], 0.0), axis=1)
        ug = u[:, sl]
        d = (past + ug) / float(w) - ug
        ys.append(_dot(d.astype(BF16), wpool_ref[g]))
    ypool_ref[...] = jnp.concatenate(ys, axis=1) * pscale_ref[...]
    npool_ref[:, 0:POOL_HIST - 1, :] = hist_ref[:, 1:POOL_HIST, :]
    npool_ref[:, POOL_HIST - 1, :] = u

    lb = _forget_lower_bound(lbl_ref[...])
    f = lb + (1.0 - lb) * jax.nn.sigmoid(proj[:, POOL_WIDTH + HG_WIDTH:POOL_WIDTH + 2 * HG_WIDTH])
    ft_ref[...] = f.T
    qt_ref[...] = _silu(proj[:, POOL_WIDTH:POOL_WIDTH + HG_WIDTH]).T
    v_ref[...] = proj[:, POOL_WIDTH + 2 * HG_WIDTH:POOL_WIDTH + 3 * HG_WIDTH]
    gate_ref[...] = _silu(proj[:, POOL_WIDTH + 3 * HG_WIDTH:])


def _mix_sample_state_kernel(s_ref, ft_ref, qt_ref, v_ref, snew_ref, o_ref):
    i = pl.program_id(0)
    lane = lax.broadcasted_iota(jnp.int32, (HG_HEAD_DIM, ft_ref.shape[1]), 1)
    for j in range(SAMPLE_STATE_BLOCK):
        mine = lane == i * SAMPLE_STATE_BLOCK + j
        for hd in range(HG_HEADS):
            r0 = hd * HG_HEAD_DIM
            f = jnp.sum(jnp.where(mine, ft_ref[r0:r0 + HG_HEAD_DIM, :], 0.0), axis=1, keepdims=True)
            q = jnp.sum(jnp.where(mine, qt_ref[r0:r0 + HG_HEAD_DIM, :], 0.0), axis=1, keepdims=True)
            v = v_ref[j:j + 1, r0:r0 + HG_HEAD_DIM]
            s_new = f * s_ref[j, hd] + (1.0 - f) * v
            snew_ref[j, hd] = s_new
            o_ref[j:j + 1, r0:r0 + HG_HEAD_DIM] = jnp.sum(q * s_new, axis=0, keepdims=True)


def _mix_sample_out_kernel(x_ref, mod_ref, o_ref, gate_ref, ypool_ref, gout_ref, wout_ref, gpost_ref,
                           gffn_ref, wrh_ref, wrl_ref, x1_ref, h2_ref, lg_ref):
    _, _, gt1, sh2, sc2, _ = _mods(mod_ref[...])
    o = o_ref[...]
    os_ = []
    for hd in range(HG_HEADS):
        oh = o[:, hd * HG_HEAD_DIM:(hd + 1) * HG_HEAD_DIM]
        os_.append(oh * lax.rsqrt(jnp.mean(oh * oh, axis=-1, keepdims=True) + EPS) * gout_ref[...])
    o_n = jnp.concatenate(os_, axis=1) * gate_ref[...]
    mix = (_dot(ypool_ref[...].astype(BF16), wout_ref[0:POOL_WIDTH, :])
           + _dot(o_n.astype(BF16), wout_ref[POOL_WIDTH:, :]))
    x1 = x_ref[...] + gt1 * _rms(mix, gpost_ref[...])
    h2 = _rms(x1, gffn_ref[...]) * (1.0 + sc2) + sh2
    x1_ref[...] = x1
    h2_ref[...] = h2.astype(BF16)
    lg_ref[...] = _router_logits(h2, wrh_ref[...], wrl_ref[...])


def _mix_sample(x, mod, hist, state, gpre, gpost, gffn, win, wpool, pscale, lbl, gout, wout, wrh, wrl):
    b = x.shape[0]
    bb = SAMPLE_STATE_BLOCK
    cp = pltpu.CompilerParams(vmem_limit_bytes=VMEM_LIMIT)
    ypool, npool, ft, qt, v, gate = pl.pallas_call(
        _mix_sample_in_kernel,
        out_shape=(jax.ShapeDtypeStruct((b, POOL_WIDTH), F32),
                   jax.ShapeDtypeStruct((b, POOL_HIST, POOL_WIDTH), F32),
                   jax.ShapeDtypeStruct((HG_WIDTH, b), F32),
                   jax.ShapeDtypeStruct((HG_WIDTH, b), F32),
                   jax.ShapeDtypeStruct((b, HG_WIDTH), F32),
                   jax.ShapeDtypeStruct((b, HG_WIDTH), F32)),
        compiler_params=cp,
        name="mix_sample_in",
    )(x, mod, gpre, win, wpool, pscale, lbl, hist)

    s_spec = pl.BlockSpec((bb, HG_HEADS, HG_HEAD_DIM, HG_HEAD_DIM), lambda i: (i, 0, 0, 0))
    col_spec = _full((HG_WIDTH, b))
    row_spec = pl.BlockSpec((bb, HG_WIDTH), lambda i: (i, 0))
    s_new, o = pl.pallas_call(
        _mix_sample_state_kernel,
        out_shape=(jax.ShapeDtypeStruct(state.shape, F32), jax.ShapeDtypeStruct((b, HG_WIDTH), F32)),
        grid=(b // bb,),
        in_specs=[s_spec, col_spec, col_spec, row_spec],
        out_specs=(s_spec, row_spec),
        compiler_params=pltpu.CompilerParams(dimension_semantics=("parallel",), vmem_limit_bytes=VMEM_LIMIT),
        name="mix_sample_state",
    )(state, ft, qt, v)

    x1, h2, lg = pl.pallas_call(
        _mix_sample_out_kernel,
        out_shape=(jax.ShapeDtypeStruct((b, D_MODEL), F32),
                   jax.ShapeDtypeStruct((b, D_MODEL), BF16),
                   jax.ShapeDtypeStruct((b, LANES), F32)),
        compiler_params=cp,
        name="mix_sample_out",
    )(x, mod, o, gate, ypool, gout, wout, gpost, gffn, wrh, wrl)
    return x1, h2, lg, npool, s_new


def _first_max(cur, idx, big):
    m = jnp.max(cur, axis=0, keepdims=True)
    first = jnp.min(jnp.where(cur == m, idx, big), axis=0, keepdims=True)
    return idx == first


def _route(logit, bias):
    n = logit.shape[1]
    scores = jax.nn.sigmoid(logit)
    sel = scores + bias
    neg = -jnp.inf

    sub = lax.broadcasted_iota(jnp.int32, (GROUP_SIZE, n), 0)
    gscore = []
    for g in range(N_GROUPS):
        sg = sel[g * GROUP_SIZE:(g + 1) * GROUP_SIZE, :]
        m1 = jnp.max(sg, axis=0, keepdims=True)
        rest = jnp.where(_first_max(sg, sub, GROUP_SIZE), neg, sg)
        gscore.append(m1 + jnp.max(rest, axis=0, keepdims=True))
    cur = jnp.concatenate(gscore, axis=0)
    gidx = lax.broadcasted_iota(jnp.int32, (N_GROUPS, n), 0)
    gmask = jnp.zeros((N_GROUPS, n), jnp.bool_)
    for _ in range(TOPK_GROUPS):
        pick = _first_max(cur, gidx, N_GROUPS)
        gmask = gmask | pick
        cur = jnp.where(pick, neg, cur)
    emask = jnp.concatenate(
        [jnp.broadcast_to(gmask[g:g + 1, :], (GROUP_SIZE, n)) for g in range(N_GROUPS)], axis=0)

    cur = jnp.where(emask, sel, neg)
    eidx = lax.broadcasted_iota(jnp.int32, (N_EXPERTS, n), 0)
    picks = []
    for _ in range(TOP_K):
        pick = _first_max(cur, eidx, N_EXPERTS)
        picks.append(pick)
        cur = jnp.where(pick, neg, cur)
    return scores, picks


def _any(masks):
    return functools.reduce(jnp.logical_or, masks)


def _router_kernel(lg_ref, bias_ref, wd_ref):
    n = lg_ref.shape[0]
    scores, picks = _route(lg_ref[...].T[0:N_EXPERTS, :], bias_ref[0:N_EXPERTS, :])
    chosen = _any(picks)
    wsum = jnp.sum(jnp.where(chosen, scores, 0.0), axis=0, keepdims=True)
    wd = jnp.where(chosen, scores / wsum * ROUTED_SCALE, 0.0)
    wd_ref[...] = jnp.concatenate([wd, jnp.zeros((LANES - N_EXPERTS, n), F32)], axis=0).T


def _router(logits, bias_col):
    t = logits.shape[0]
    tile = min(ROUTER_TILE, t)
    return pl.pallas_call(
        _router_kernel,
        out_shape=jax.ShapeDtypeStruct((t, LANES), F32),
        grid=(t // tile,),
        in_specs=[pl.BlockSpec((tile, LANES), lambda i: (i, 0)), _full(bias_col.shape)],
        out_specs=pl.BlockSpec((tile, LANES), lambda i: (i, 0)),
        compiler_params=pltpu.CompilerParams(dimension_semantics=("parallel",), vmem_limit_bytes=VMEM_LIMIT),
        name="router",
    )(logits, bias_col)


def _plan_kernel(lg_ref, bias_ref, upper_ref, dest_ref, wsel_ref, cnt_ref, loff_ref):
    scores, picks = _route(lg_ref[...].T[0:N_EXPERTS, :], bias_ref[0:N_EXPERTS, :])
    chosen = _any(picks)
    chosen_f = jnp.where(chosen, 1.0, 0.0)
    cnt = jnp.sum(chosen_f, axis=1, keepdims=True)
    units = jnp.floor((cnt + (RUN_ALIGN - 1)) / RUN_ALIGN)
    ei = lax.broadcasted_iota(jnp.int32, (N_EXPERTS, N_EXPERTS), 0)
    ej = lax.broadcasted_iota(jnp.int32, (N_EXPERTS, N_EXPERTS), 1)
    before = jnp.where(ej < ei, 1.0, 0.0).astype(BF16)
    loff = RUN_ALIGN * _dot(before, jnp.broadcast_to(units, (N_EXPERTS, LANES)).astype(BF16))[:, 0:1]
    slot = loff + _dot(chosen_f.astype(BF16), upper_ref[...])
    wsum = jnp.sum(jnp.where(chosen, scores, 0.0), axis=0, keepdims=True)
    pick_sum = lambda v: jnp.concatenate(
        [jnp.sum(jnp.where(p, v, 0.0), axis=0, keepdims=True) for p in picks], axis=0)
    dest_ref[...] = (pick_sum(slot) * ROW_SUB).astype(jnp.int32)
    wsel_ref[...] = pick_sum(scores) / wsum * ROUTED_SCALE
    cnt_ref[0] = jnp.broadcast_to(cnt, (N_EXPERTS, LANES))
    loff_ref[0] = jnp.broadcast_to(loff, (N_EXPERTS, LANES))


def _plan(logits, bias_col, upper):
    t = logits.shape[0]
    nw = t // PLAN_WINDOW
    per_w = pl.BlockSpec((1, N_EXPERTS, LANES), lambda w: (w, 0, 0))
    picks = pl.BlockSpec((TOP_K, PLAN_WINDOW), lambda w: (0, w))
    return pl.pallas_call(
        _plan_kernel,
        out_shape=(jax.ShapeDtypeStruct((TOP_K, t), jnp.int32), jax.ShapeDtypeStruct((TOP_K, t), F32),
                   jax.ShapeDtypeStruct((nw, N_EXPERTS, LANES), F32),
                   jax.ShapeDtypeStruct((nw, N_EXPERTS, LANES), F32)),
        grid=(nw,),
        in_specs=[pl.BlockSpec((PLAN_WINDOW, LANES), lambda w: (w, 0)), _full(bias_col.shape), _full(upper.shape)],
        out_specs=(picks, picks, per_w, per_w),
        compiler_params=pltpu.CompilerParams(dimension_semantics=("parallel",), vmem_limit_bytes=VMEM_LIMIT),
        name="plan",
    )(logits, bias_col, upper)


def _slot_rows(slot):
    return pl.ds(pl.multiple_of(slot * ROW_SUB, ROW_SUB), ROW_SUB)


def _rows_at(offset):
    return pl.ds(pl.multiple_of(offset, ROW_SUB), ROW_SUB)


def _dispatch_kernel(dest_ref, h_ref, x_ref, dest_s, rows_s, sem):
    plan_copy = pltpu.make_async_copy(dest_ref.at[0, 0], dest_s, sem)
    plan_copy.start()
    x_ref[...] = jnp.zeros_like(x_ref)
    for j in range(ROW_SUB):
        rows_s[pl.ds(j, PLAN_WINDOW, stride=ROW_SUB), :] = h_ref[:, j * LANES:(j + 1) * LANES]
    plan_copy.wait()

    def body(i, carry):
        for u in range(SCATTER_UNROLL):
            t = i * SCATTER_UNROLL + u
            row = rows_s[_slot_rows(t), :]
            for k in range(TOP_K):
                x_ref[_rows_at(dest_s[k * PLAN_WINDOW + t]), :] = row
        return carry

    lax.fori_loop(0, PLAN_WINDOW // SCATTER_UNROLL, body, 0)


def _dispatch(dest_w, h2):
    nw = dest_w.shape[0]
    region = REGION_ROWS * ROW_SUB
    return pl.pallas_call(
        _dispatch_kernel,
        out_shape=jax.ShapeDtypeStruct((nw * region, LANES), jnp.uint32),
        grid=(nw,),
        in_specs=[pl.BlockSpec((1, 1, TOP_K * PLAN_WINDOW), lambda w: (w, 0, 0)),
                  pl.BlockSpec((PLAN_WINDOW, ROW_WORDS), lambda w: (w, 0))],
        out_specs=pl.BlockSpec((region, LANES), lambda w: (w, 0)),
        scratch_shapes=[pltpu.SMEM((TOP_K * PLAN_WINDOW,), jnp.int32),
                        pltpu.VMEM((PLAN_WINDOW * ROW_SUB, LANES), jnp.uint32),
                        pltpu.SemaphoreType.DMA],
        compiler_params=pltpu.CompilerParams(dimension_semantics=("arbitrary",), vmem_limit_bytes=VMEM_LIMIT),
        name="dispatch",
    )(dest_w, h2)


def _unpack_rows(ref, n):
    parts = [_unpack_pair(ref[pl.ds(j, n, stride=ROW_SUB), :]) for j in range(ROW_SUB)]
    return jnp.concatenate([p[0] for p in parts] + [p[1] for p in parts], axis=1)


def _ffn_kernel(es_ref, to_ref, tn_ref, x_hbm, w1_ref, w3_ref, w2_ref, y_hbm, w13_s, w2_s, xbuf, ybuf, rsem, wsem):
    del x_hbm
    e = pl.program_id(0)
    first, end, total = es_ref[e], es_ref[e + 1], es_ref[N_EXPERTS]
    block = RUN_ALIGN * ROW_SUB
    ring = FFN_DEPTH + 1

    def fetch(item, start):
        src = y_hbm.at[pl.ds(pl.multiple_of(to_ref[item] * block, block), FFN_TILE * ROW_SUB)]
        cp = pltpu.make_async_copy(src, xbuf.at[item % ring], rsem.at[item % ring])
        if start:
            cp.start()
        else:
            cp.wait()

    def writeback(item, start):
        for b in range(FFN_TILE // RUN_ALIGN):
            @pl.when(b < tn_ref[item])
            def _():
                dst = pl.multiple_of(to_ref[item] * block, block) + b * block
                cp = pltpu.make_async_copy(ybuf.at[item % ring, pl.ds(b * block, block)],
                                           y_hbm.at[pl.ds(dst, block)], wsem.at[item % ring])
                if start:
                    cp.start()
                else:
                    cp.wait()

    @pl.when(e == 0)
    def _():
        for d in range(FFN_DEPTH):
            @pl.when(d < total)
            def _():
                fetch(d, True)

    w13_s[:, 0:D_EXPERT] = w1_ref[0].astype(BF16)
    w13_s[:, D_EXPERT:] = w3_ref[0].astype(BF16)
    w2_s[...] = w2_ref[0].astype(BF16)

    def item_body(i, carry):
        @pl.when(i + FFN_DEPTH < total)
        def _():
            fetch(i + FFN_DEPTH, True)

        @pl.when(i >= ring)
        def _():
            writeback(i - ring, False)

        fetch(i, False)
        x = _unpack_rows(xbuf.at[i % ring], FFN_TILE).astype(BF16)
        h = _dot(x, w13_s[...])
        act = _silu(h[:, :D_EXPERT]) * h[:, D_EXPERT:]
        y = _dot(act.astype(BF16), w2_s[...])
        out = ybuf.at[i % ring]
        for j in range(ROW_SUB):
            out[pl.ds(j, FFN_TILE, stride=ROW_SUB), :] = _pack_pair(
                y[:, j * LANES:(j + 1) * LANES], y[:, ROW_WORDS + j * LANES:ROW_WORDS + (j + 1) * LANES])
        writeback(i, True)
        return carry

    lax.fori_loop(first, end, item_body, 0)

    @pl.when(e == pl.num_programs(0) - 1)
    def _():
        for d in range(ring):
            @pl.when(total - 1 - d >= 0)
            def _():
                writeback(total - 1 - d, False)


def _expert_ffn(expert_start, item_off, item_blocks, x_slots, w1, w3, w2):
    of_expert = lambda e, es, to, tn: (e, 0, 0)
    tiles = pltpu.VMEM((FFN_DEPTH + 1, FFN_TILE * ROW_SUB, LANES), jnp.uint32)
    sems = pltpu.SemaphoreType.DMA((FFN_DEPTH + 1,))
    grid_spec = pltpu.PrefetchScalarGridSpec(
        num_scalar_prefetch=3,
        grid=(N_EXPERTS,),
        in_specs=[pl.BlockSpec(memory_space=pl.ANY),
                  pl.BlockSpec((1, D_MODEL, D_EXPERT), of_expert),
                  pl.BlockSpec((1, D_MODEL, D_EXPERT), of_expert),
                  pl.BlockSpec((1, D_EXPERT, D_MODEL), of_expert)],
        out_specs=pl.BlockSpec(memory_space=pl.ANY),
        scratch_shapes=[pltpu.VMEM((D_MODEL, 2 * D_EXPERT), BF16), pltpu.VMEM((D_EXPERT, D_MODEL), BF16),
                        tiles, tiles, sems, sems])
    return pl.pallas_call(
        _ffn_kernel,
        out_shape=jax.ShapeDtypeStruct(x_slots.shape, jnp.uint32),
        grid_spec=grid_spec,
        input_output_aliases={3: 0},
        compiler_params=pltpu.CompilerParams(dimension_semantics=("arbitrary",), vmem_limit_bytes=VMEM_LIMIT),
        name="expert_ffn",
    )(expert_start, item_off, item_blocks, x_slots, w1, w3, w2)


def _work_items(cnt, loff):
    nw = cnt.shape[0]
    cnt = cnt.astype(jnp.int32).T.reshape(-1)
    loff = loff.astype(jnp.int32).T.reshape(-1)
    region = jnp.tile(jnp.arange(nw, dtype=jnp.int32) * REGION_ROWS, N_EXPERTS)
    n_items = nw * N_EXPERTS + (nw * PLAN_WINDOW * TOP_K) // FFN_TILE
    tiles = (cnt + FFN_TILE - 1) // FFN_TILE
    ends = jnp.cumsum(tiles)
    starts = ends - tiles
    i = jnp.arange(n_items, dtype=jnp.int32)[:, None]
    mine = (starts[None, :] <= i) & (i < ends[None, :])
    of_pair = lambda v: jnp.sum(jnp.where(mine, v[None, :], 0), axis=1)
    r = i[:, 0] - of_pair(starts)
    row = of_pair(region + loff) + r * FFN_TILE
    own = jnp.clip(of_pair(cnt) - r * FFN_TILE, 0, FFN_TILE)
    expert_start = jnp.concatenate([jnp.zeros((1,), jnp.int32), ends.reshape(N_EXPERTS, nw)[:, -1]])
    return expert_start, row // RUN_ALIGN, (own + RUN_ALIGN - 1) // RUN_ALIGN


def _combine_kernel(dest_ref, wsel_ref, y_ref, h_ref, x1_ref, gt2_ref, gpost_ref, ws13_ref, ws2_ref, out_ref,
                    dest_s, wsel_s, lo_s, hi_s, sem):
    s = pl.program_id(1)

    @pl.when(s == 0)
    def _():
        copies = (pltpu.make_async_copy(dest_ref.at[0, 0], dest_s, sem.at[0]),
                  pltpu.make_async_copy(wsel_ref.at[0, 0], wsel_s, sem.at[1]))
        for c in copies:
            c.start()
        for c in copies:
            c.wait()

    base = s * COMBINE_TILE

    def body(i, carry):
        for u in range(SCATTER_UNROLL):
            t = i * SCATTER_UNROLL + u
            acc_lo = jnp.zeros((ROW_SUB, LANES), F32)
            acc_hi = jnp.zeros((ROW_SUB, LANES), F32)
            for k in range(TOP_K):
                at = k * PLAN_WINDOW + base + t
                lo, hi = _unpack_pair(y_ref[_rows_at(dest_s[at]), :])
                w = wsel_s[at]
                acc_lo = acc_lo + w * lo
                acc_hi = acc_hi + w * hi
            lo_s[_slot_rows(t), :] = acc_lo
            hi_s[_slot_rows(t), :] = acc_hi
        return carry

    lax.fori_loop(0, COMBINE_TILE // SCATTER_UNROLL, body, 0)
    rows = lambda ref: [ref[pl.ds(j, COMBINE_TILE, stride=ROW_SUB), :] for j in range(ROW_SUB)]
    routed = jnp.concatenate(rows(lo_s) + rows(hi_s), axis=1)
    h_lo, h_hi = _unpack_pair(h_ref[...])
    x = jnp.concatenate([h_lo, h_hi], axis=1).astype(BF16)
    hs = _dot(x, ws13_ref[...])
    act = _silu(hs[:, :D_EXPERT]) * hs[:, D_EXPERT:]
    ff = routed + _dot(act.astype(BF16), ws2_ref[...])
    out_ref[...] = x1_ref[...] + gt2_ref[0] * _rms(ff, gpost_ref[...])


def _combine(dest_w, wsel_w, y_slots, h2, x1, gt2, tokens_per_gt2, gpost, ws13, ws2):
    nw = dest_w.shape[0]
    sub = PLAN_WINDOW // COMBINE_TILE
    plan = pl.BlockSpec((1, 1, TOP_K * PLAN_WINDOW), lambda w, s: (w, 0, 0))
    tok = lambda w, s: (w * sub + s, 0)
    return pl.pallas_call(
        _combine_kernel,
        out_shape=jax.ShapeDtypeStruct(x1.shape, F32),
        grid=(nw, sub),
        in_specs=[plan, plan,
                  pl.BlockSpec((REGION_ROWS * ROW_SUB, LANES), lambda w, s: (w, 0)),
                  pl.BlockSpec((COMBINE_TILE, ROW_WORDS), tok),
                  pl.BlockSpec((COMBINE_TILE, D_MODEL), tok),
                  pl.BlockSpec((1, 1, D_MODEL), lambda w, s: ((w * PLAN_WINDOW) // tokens_per_gt2, 0, 0)),
                  _full((1, D_MODEL)), _full(ws13.shape), _full(ws2.shape)],
        out_specs=pl.BlockSpec((COMBINE_TILE, D_MODEL), tok),
        scratch_shapes=[pltpu.SMEM((TOP_K * PLAN_WINDOW,), jnp.int32),
                        pltpu.SMEM((TOP_K * PLAN_WINDOW,), F32),
                        pltpu.VMEM((COMBINE_TILE * ROW_SUB, LANES), F32),
                        pltpu.VMEM((COMBINE_TILE * ROW_SUB, LANES), F32),
                        pltpu.SemaphoreType.DMA((2,))],
        compiler_params=pltpu.CompilerParams(dimension_semantics=("arbitrary", "arbitrary"),
                                             vmem_limit_bytes=VMEM_LIMIT),
        name="combine",
    )(dest_w, wsel_w, y_slots, h2, x1, gt2, gpost, ws13, ws2)


def _moe_kernel(h_ref, wd_ref, x1_ref, gt2_ref, gpost_ref, w1_ref, w3_ref, w2_ref, ws1_ref, ws3_ref, ws2_ref,
                out_ref, acc):
    e = pl.program_id(1)
    x = h_ref[...]

    @pl.when(e == 0)
    def _():
        hs = _silu(_dot(x, ws1_ref[...].astype(BF16))) * _dot(x, ws3_ref[...].astype(BF16))
        acc[...] = _dot(hs.astype(BF16), ws2_ref[...].astype(BF16))

    lane = lax.broadcasted_iota(jnp.int32, wd_ref.shape, 1)
    wcol = jnp.sum(jnp.where(lane == e, wd_ref[...], 0.0), axis=1, keepdims=True)
    act = _silu(_dot(x, w1_ref[0].astype(BF16))) * _dot(x, w3_ref[0].astype(BF16)) * wcol
    acc[...] += _dot(act.astype(BF16), w2_ref[0].astype(BF16))

    @pl.when(e == pl.num_programs(1) - 1)
    def _():
        out_ref[...] = x1_ref[...] + gt2_ref[0] * _rms(acc[...], gpost_ref[...])


def _moe(h2, wd, x1, gt2, rows_per_gt2, gpost, w1, w3, w2, ws1, ws3, ws2):
    t = h2.shape[0]
    tile = min(MOE_TILE, t)
    tok = lambda i, e: (i, 0)
    exp = lambda i, e: (e, 0, 0)
    return pl.pallas_call(
        _moe_kernel,
        out_shape=jax.ShapeDtypeStruct((t, D_MODEL), F32),
        grid=(t // tile, N_EXPERTS),
        in_specs=[pl.BlockSpec((tile, D_MODEL), tok),
                  pl.BlockSpec((tile, LANES), tok),
                  pl.BlockSpec((tile, D_MODEL), tok),
                  pl.BlockSpec((1,) + gt2.shape[1:], lambda i, e: (i // rows_per_gt2, 0, 0)),
                  _full((1, D_MODEL)),
                  pl.BlockSpec((1, D_MODEL, D_EXPERT), exp),
                  pl.BlockSpec((1, D_MODEL, D_EXPERT), exp),
                  pl.BlockSpec((1, D_EXPERT, D_MODEL), exp),
                  _full(ws1.shape), _full(ws3.shape), _full(ws2.shape)],
        out_specs=pl.BlockSpec((tile, D_MODEL), tok),
        scratch_shapes=[pltpu.VMEM((tile, D_MODEL), F32)],
        compiler_params=pltpu.CompilerParams(dimension_semantics=("parallel", "arbitrary"),
                                             vmem_limit_bytes=VMEM_LIMIT),
        name="moe",
    )(h2, wd, x1, gt2, gpost, w1, w3, w2, ws1, ws3, ws2)


def kernel(x_prompt, x_sample, c_prompt, c_sample, state_pool, state_hgrn, w_ada, b_ada, g_pre_mix, g_post_mix,
           w_in, w_pool, pool_scale, lb_logits, g_out_norm, w_out, g_pre_ffn, g_post_ffn, w_router, router_bias,
           w_exp_gate, w_exp_up, w_exp_down, w_sh_gate, w_sh_up, w_sh_down):
    assert w_ada.shape[0] == 1 and lb_logits.shape[0] == 2, "single-layer trunk"
    bp, lp, d = x_prompt.shape
    bs = x_sample.shape[0]
    row = lambda a: a[0].reshape(1, -1)

    n_mod = bp + bs
    pad = (-n_mod) % 16
    c_all = jnp.concatenate([c_prompt, c_sample, jnp.zeros((pad, d), F32)], axis=0)
    mod = _ada(c_all, w_ada[0], b_ada)
    mod_p = mod[:bp].reshape(bp, 1, 6 * d)
    mod_s = mod[bp:n_mod]

    win = w_in[0].astype(BF16)
    wout = w_out[0].astype(BF16)
    wpool = w_pool[0].astype(BF16)
    wr = jnp.pad(w_router[0], ((0, 0), (0, LANES - N_EXPERTS)))
    wrh = wr.astype(BF16)
    wrl = (wr - wrh.astype(F32)).astype(BF16)
    m2 = jnp.asarray(_M2_NP, BF16)
    gpre, gpost, gffn, gffn_post = row(g_pre_mix), row(g_post_mix), row(g_pre_ffn), row(g_post_ffn)
    pscale, gout = row(pool_scale), row(g_out_norm)
    bias_col = jnp.pad(router_bias[0], (0, LANES - N_EXPERTS)).reshape(LANES, 1)

    x1_p, h2_p, lg_p, pool_p, st_p = _mix_prompt(x_prompt, mod_p, gpre, gpost, gffn, win, wpool, pscale,
                                                  lb_logits, gout, wout, wrh, wrl, m2)
    x1_s, h2_s, lg_s, pool_s, st_s = _mix_sample(x_sample[:, 0, :], mod_s, state_pool[0], state_hgrn[0],
                                                  gpre, gpost, gffn, win, wpool, pscale, lb_logits, gout, wout,
                                                  wrh, wrl)

    experts = (w_exp_gate[0], w_exp_up[0], w_exp_down[0], w_sh_gate[0], w_sh_up[0], w_sh_down[0])
    tp = bp * lp
    nw = tp // PLAN_WINDOW
    assert lp % PLAN_WINDOW == 0
    upper = jnp.asarray(np.triu(np.ones((PLAN_WINDOW, PLAN_WINDOW), np.float32), 1), BF16)
    dest, wsel, cnt, loff = _plan(lg_p.reshape(tp, LANES), bias_col, upper)
    per_window = lambda a: a.reshape(TOP_K, nw, PLAN_WINDOW).transpose(1, 0, 2).reshape(nw, 1, TOP_K * PLAN_WINDOW)
    dest_w, wsel_w = per_window(dest), per_window(wsel)
    h2_rows = h2_p.reshape(tp, ROW_WORDS)
    x_slots = _dispatch(dest_w, h2_rows)
    y_slots = _expert_ffn(*_work_items(cnt[:, :, 0], loff[:, :, 0]), x_slots, *experts[:3])
    ws13 = jnp.concatenate([experts[3], experts[4]], axis=1).astype(BF16)
    gt2_p = mod_p[:, :, 5 * d:]
    y_p = _combine(dest_w, wsel_w, y_slots, h2_rows, x1_p.reshape(tp, d), gt2_p, lp, gffn_post,
                   ws13, experts[5].astype(BF16))
    wd_s = _router(lg_s, bias_col)
    gt2_s = mod_s[:, 5 * d:].reshape(1, bs, d)
    y_s = _moe(h2_s, wd_s, x1_s, gt2_s, 1, gffn_post, *experts)

    return (y_p.reshape(bp, lp, d), y_s.reshape(bs, 1, d), pool_p[None], st_p[None], pool_s[None], st_s[None])
```

```python
import functools

import numpy as np
import jax
import jax.numpy as jnp
from jax import lax
from jax.experimental import pallas as pl
from jax.experimental.pallas import tpu as pltpu

F32 = jnp.float32
BF16 = jnp.bfloat16

D_MODEL = 1024
POOL_WIDTH = 512
POOL_WINDOWS = (2, 4, 8, 16)
POOL_GROUP = 128
POOL_HIST = 15
HG_WIDTH = 512
HG_HEADS = 4
HG_HEAD_DIM = 128
IN_WIDTH = POOL_WIDTH + 4 * HG_WIDTH
N_EXPERTS = 64
TOP_K = 8
N_GROUPS = 8
TOPK_GROUPS = 4
GROUP_SIZE = N_EXPERTS // N_GROUPS
D_EXPERT = 256
ROUTED_SCALE = 2.5
EPS = 1e-6

LANES = 128
CHUNK = 64
TIME_TILE = 256
ADA_TILE_N = 512
ROUTER_TILE = 512
MOE_TILE = 1024
SAMPLE_STATE_BLOCK = 8
PLAN_WINDOW = 1024
RUN_ALIGN = 32
FFN_TILE = 160
REGION_ROWS = PLAN_WINDOW * TOP_K + N_EXPERTS * (RUN_ALIGN - 1) + FFN_TILE
ROW_WORDS = D_MODEL // 2
ROW_SUB = ROW_WORDS // LANES
COMBINE_TILE = 256
SCATTER_UNROLL = 4
FFN_GROUP = 4
VMEM_LIMIT = 58 * 1024 * 1024

_LEVELS = (64, 32, 16, 8, 4, 2)
_BLK_CUM = 0
_BLK_END = 1


def _level_blocks():
    c = CHUNK
    i = np.arange(c)[:, None]
    s = np.arange(c)[None, :]
    blocks = [(s <= i), (s > i)]
    index = {}
    for lvl in _LEVELS:
        seg = (i // lvl) * lvl
        mid = seg + lvl // 2
        ref = mid - 1
        index[lvl] = len(blocks)
        blocks.append((i >= mid) & (s > ref) & (s <= i))
        if lvl > 2:
            blocks.append((i < mid) & (s > i) & (s <= ref))
    m = np.concatenate(blocks, axis=0).astype(np.float32)
    return np.concatenate([m, m], axis=1), index


_M2_NP, _LEVEL_INDEX = _level_blocks()


def _dot(a, b):
    return jnp.dot(a, b, preferred_element_type=F32)


def _dot_nt(a, b):
    return lax.dot_general(a, b, (((1,), (1,)), ((), ())), preferred_element_type=F32)


def _split2(x):
    hi = x.astype(BF16)
    lo = (x - hi.astype(F32)).astype(BF16)
    return hi, lo


def _pack_pair(a, b):
    lo = lax.bitcast_convert_type(a.astype(BF16).astype(F32), jnp.uint32)
    hi = lax.bitcast_convert_type(b.astype(BF16).astype(F32), jnp.uint32)
    return (lo >> 16) | hi


def _unpack_pair(words):
    lo = lax.bitcast_convert_type(words << 16, F32)
    hi = lax.bitcast_convert_type(words & jnp.uint32(0xFFFF0000), F32)
    return lo, hi


def _silu(x):
    return x * jax.nn.sigmoid(x)


def _rms(x, g):
    return x * lax.rsqrt(jnp.mean(x * x, axis=-1, keepdims=True) + EPS) * g


def _mods(m):
    return [m[:, j * D_MODEL:(j + 1) * D_MODEL] for j in range(6)]


def _forget_lower_bound(lbl):
    mx = jnp.max(lbl, axis=0, keepdims=True)
    e = jnp.exp(lbl - mx)
    return e[0:1] / jnp.sum(e, axis=0, keepdims=True)


def _router_logits(h2, wrh, wrl):
    hi, lo = _split2(h2)
    return _dot(hi, wrh) + _dot(lo, wrh) + _dot(hi, wrl)


def _ada_kernel(c_ref, w_ref, b_ref, o_ref):
    a_hi, a_lo = _split2(_silu(c_ref[...]))
    w_hi, w_lo = _split2(w_ref[...])
    o_ref[...] = _dot(a_hi, w_hi) + _dot(a_lo, w_hi) + _dot(a_hi, w_lo) + b_ref[...]


def _ada(c_all, w_ada, b_ada):
    rows = c_all.shape[0]
    n = w_ada.shape[1]
    return pl.pallas_call(
        _ada_kernel,
        out_shape=jax.ShapeDtypeStruct((rows, n), F32),
        grid=(n // ADA_TILE_N,),
        in_specs=[pl.BlockSpec((rows, D_MODEL), lambda j: (0, 0)),
                  pl.BlockSpec((D_MODEL, ADA_TILE_N), lambda j: (0, j)),
                  pl.BlockSpec((1, ADA_TILE_N), lambda j: (0, j))],
        out_specs=pl.BlockSpec((rows, ADA_TILE_N), lambda j: (0, j)),
        compiler_params=pltpu.CompilerParams(dimension_semantics=("parallel",),
                                             vmem_limit_bytes=VMEM_LIMIT),
        name="ada",
    )(c_all, w_ada, b_ada)


def _level_masks():
    i = lax.broadcasted_iota(jnp.int32, (CHUNK, CHUNK), 0)
    j = lax.broadcasted_iota(jnp.int32, (CHUNK, CHUNK), 1)
    masks = {}
    for lvl in _LEVELS:
        sh = lvl.bit_length() - 1
        same = (i >> sh) == (j >> sh)
        upper = ((i >> (sh - 1)) & 1) == 1
        lower = ((j >> (sh - 1)) & 1) == 0
        masks[lvl] = same & upper & lower
    return masks, i == j


def _hgrn_chunk_head(q, k, v, e_all, lane0, st, masks, eye):
    def blk(n):
        return e_all[n * CHUNK:(n + 1) * CHUNK, lane0:lane0 + HG_HEAD_DIM]

    b = blk(_BLK_CUM)
    a = jnp.where(eye, _dot_nt(q.astype(BF16), k.astype(BF16)), 0.0)
    for lvl in _LEVELS:
        n = _LEVEL_INDEX[lvl]
        ql = (q * jnp.exp(blk(n))).astype(BF16)
        kl = (k * jnp.exp(blk(n + 1))).astype(BF16) if lvl > 2 else k.astype(BF16)
        a = a + jnp.where(masks[lvl], _dot_nt(ql, kl), 0.0)
    st_b = st.astype(BF16)
    o = _dot(a.astype(BF16), v.astype(BF16)) + _dot_nt((q * jnp.exp(b)).astype(BF16), st_b)
    k_end = (k * jnp.exp(blk(_BLK_END))).astype(BF16)
    decay = jnp.exp(b[CHUNK - 1:CHUNK, :])
    st_new = st * decay + _dot(v.T.astype(BF16), k_end)
    return o, st_new


def _mix_prompt_kernel(x_ref, mod_ref, gpre_ref, gpost_ref, gffn_ref, win_ref, wpool_ref, pscale_ref,
                       lbl_ref, gout_ref, wout_ref, wrh_ref, wrl_ref, m2_ref,
                       x1_ref, h2_ref, lg_ref, pool_ref, st_ref,
                       st_s, ubuf, q_s, k_s, v_s, g_s, o_s):
    t = pl.program_id(1)
    n_t = pl.num_programs(1)
    tt = TIME_TILE

    @pl.when(t == 0)
    def _():
        st_s[...] = jnp.zeros_like(st_s)
        ubuf[0:16, :] = jnp.zeros((16, POOL_WIDTH), F32)

    xt = x_ref[0]
    sh1, sc1, gt1, sh2, sc2, gt2 = _mods(mod_ref[0])
    h = _rms(xt, gpre_ref[...]) * (1.0 + sc1) + sh1
    proj = _dot(h.astype(BF16), win_ref[...])

    u = proj[:, :POOL_WIDTH]
    ubuf[16:16 + tt, :] = u
    pos = (t * tt + lax.broadcasted_iota(jnp.int32, (tt, 1), 0) + 1).astype(F32)
    ys = []
    for g, w in enumerate(POOL_WINDOWS):
        s = ubuf[:, g * POOL_GROUP:(g + 1) * POOL_GROUP]
        off = 0
        for step in range(w.bit_length() - 1):
            sh = 1 << step
            s = s[sh:, :] + s[:-sh, :]
            off += sh
        ws = s[16 - off:16 - off + tt, :]
        cnt = jnp.minimum(pos, float(w))
        d = ws / cnt - u[:, g * POOL_GROUP:(g + 1) * POOL_GROUP]
        ys.append(_dot(d.astype(BF16), wpool_ref[g]))
    y_pool = jnp.concatenate(ys, axis=1) * pscale_ref[...]
    ubuf[0:16, :] = ubuf[tt:tt + 16, :]

    @pl.when(t == n_t - 1)
    def _():
        pool_ref[0] = u[tt - POOL_HIST:, :]

    lb = _forget_lower_bound(lbl_ref[...])
    f = lb + (1.0 - lb) * jax.nn.sigmoid(proj[:, POOL_WIDTH + HG_WIDTH:POOL_WIDTH + 2 * HG_WIDTH])
    q_s[...] = _silu(proj[:, POOL_WIDTH:POOL_WIDTH + HG_WIDTH])
    k_s[...] = 1.0 - f
    v_s[...] = proj[:, POOL_WIDTH + 2 * HG_WIDTH:POOL_WIDTH + 3 * HG_WIDTH]
    g_s[...] = jnp.log(f)
    gate = _silu(proj[:, POOL_WIDTH + 3 * HG_WIDTH:])

    masks, eye = _level_masks()
    m2 = m2_ref[...]

    def chunk_body(c, carry):
        r0 = pl.multiple_of(c * CHUNK, CHUNK)
        g_hi, g_lo = _split2(g_s[pl.ds(r0, CHUNK), :])
        e_all = _dot(m2, jnp.concatenate([g_hi, g_lo], axis=0))
        for hd in range(HG_HEADS):
            lane0 = hd * HG_HEAD_DIM
            q = q_s[pl.ds(r0, CHUNK), lane0:lane0 + HG_HEAD_DIM]
            k = k_s[pl.ds(r0, CHUNK), lane0:lane0 + HG_HEAD_DIM]
            v = v_s[pl.ds(r0, CHUNK), lane0:lane0 + HG_HEAD_DIM]
            o, st_new = _hgrn_chunk_head(q, k, v, e_all, lane0, st_s[hd], masks, eye)
            st_s[hd] = st_new
            o_s[pl.ds(r0, CHUNK), lane0:lane0 + HG_HEAD_DIM] = o
        return carry

    lax.fori_loop(0, tt // CHUNK, chunk_body, 0)

    o = o_s[...]
    os_ = []
    for hd in range(HG_HEADS):
        oh = o[:, hd * HG_HEAD_DIM:(hd + 1) * HG_HEAD_DIM]
        os_.append(oh * lax.rsqrt(jnp.mean(oh * oh, axis=-1, keepdims=True) + EPS) * gout_ref[...])
    o_n = jnp.concatenate(os_, axis=1) * gate

    mix = _dot(y_pool.astype(BF16), wout_ref[0:POOL_WIDTH, :]) + _dot(o_n.astype(BF16), wout_ref[POOL_WIDTH:, :])
    x1 = xt + gt1 * _rms(mix, gpost_ref[...])
    h2 = _rms(x1, gffn_ref[...]) * (1.0 + sc2) + sh2
    x1_ref[0] = x1
    h2_ref[0] = _pack_pair(h2[:, :ROW_WORDS], h2[:, ROW_WORDS:])
    lg_ref[0] = _router_logits(h2, wrh_ref[...], wrl_ref[...])

    @pl.when(t == n_t - 1)
    def _():
        for hd in range(HG_HEADS):
            st_ref[0, hd] = st_s[hd].T


def _full(shape):
    nd = len(shape)
    return pl.BlockSpec(shape, lambda *_: (0,) * nd)


def _mix_prompt(x, mod, gpre, gpost, gffn, win, wpool, pscale, lbl, gout, wout, wrh, wrl, m2):
    b, l, d = x.shape
    tt = TIME_TILE
    tile = lambda i, j: (i, j, 0)
    per_b = lambda i, j: (i, 0, 0)
    return pl.pallas_call(
        _mix_prompt_kernel,
        out_shape=(jax.ShapeDtypeStruct((b, l, d), F32),
                   jax.ShapeDtypeStruct((b, l, ROW_WORDS), jnp.uint32),
                   jax.ShapeDtypeStruct((b, l, LANES), F32),
                   jax.ShapeDtypeStruct((b, POOL_HIST, POOL_WIDTH), F32),
                   jax.ShapeDtypeStruct((b, HG_HEADS, HG_HEAD_DIM, HG_HEAD_DIM), F32)),
        grid=(b, l // tt),
        in_specs=[pl.BlockSpec((1, tt, d), tile),
                  pl.BlockSpec((1, 1, 6 * d), per_b),
                  _full((1, d)), _full((1, d)), _full((1, d)),
                  _full(win.shape), _full(wpool.shape), _full((1, POOL_WIDTH)),
                  _full(lbl.shape), _full((1, HG_HEAD_DIM)), _full(wout.shape),
                  _full(wrh.shape), _full(wrl.shape), _full(m2.shape)],
        out_specs=(pl.BlockSpec((1, tt, d), tile),
                   pl.BlockSpec((1, tt, ROW_WORDS), tile),
                   pl.BlockSpec((1, tt, LANES), tile),
                   pl.BlockSpec((1, POOL_HIST, POOL_WIDTH), per_b),
                   pl.BlockSpec((1, HG_HEADS, HG_HEAD_DIM, HG_HEAD_DIM), lambda i, j: (i, 0, 0, 0))),
        scratch_shapes=[pltpu.VMEM((HG_HEADS, HG_HEAD_DIM, HG_HEAD_DIM), F32),
                        pltpu.VMEM((tt + 16, POOL_WIDTH), F32),
                        pltpu.VMEM((tt, HG_WIDTH), F32), pltpu.VMEM((tt, HG_WIDTH), F32),
                        pltpu.VMEM((tt, HG_WIDTH), F32), pltpu.VMEM((tt, HG_WIDTH), F32),
                        pltpu.VMEM((tt, HG_WIDTH), F32)],
        compiler_params=pltpu.CompilerParams(dimension_semantics=("parallel", "arbitrary"),
                                             vmem_limit_bytes=VMEM_LIMIT),
        name="mix_prompt",
    )(x, mod, gpre, gpost, gffn, win, wpool, pscale, lbl, gout, wout, wrh, wrl, m2)


def _mix_sample_in_kernel(x_ref, mod_ref, gpre_ref, win_ref, wpool_ref, pscale_ref, lbl_ref, hist_ref,
                          ypool_ref, npool_ref, ft_ref, qt_ref, v_ref, gate_ref):
    xt = x_ref[...]
    sh1, sc1 = _mods(mod_ref[...])[:2]
    h = _rms(xt, gpre_ref[...]) * (1.0 + sc1) + sh1
    proj = _dot(h.astype(BF16), win_ref[...])
    u = proj[:, :POOL_WIDTH]
    row = lax.broadcasted_iota(jnp.int32, (hist_ref.shape[0], POOL_HIST, POOL_GROUP), 1)
    ys = []
    for g, w in enumerate(POOL_WINDOWS):
        sl = slice(g * POOL_GROUP, (g + 1) * POOL_GROUP)
        past = jnp.sum(jnp.where(row >= POOL_HIST - (w - 1), hist_ref[:, :, sl], 0.0), axis=1)
        ug = u[:, sl]
        d = (past + ug) / float(w) - ug
        ys.append(_dot(d.astype(BF16), wpool_ref[g]))
    ypool_ref[...] = jnp.concatenate(ys, axis=1) * pscale_ref[...]
    npool_ref[:, 0:POOL_HIST - 1, :] = hist_ref[:, 1:POOL_HIST, :]
    npool_ref[:, POOL_HIST - 1, :] = u

    lb = _forget_lower_bound(lbl_ref[...])
    f = lb + (1.0 - lb) * jax.nn.sigmoid(proj[:, POOL_WIDTH + HG_WIDTH:POOL_WIDTH + 2 * HG_WIDTH])
    ft_ref[...] = f.T
    qt_ref[...] = _silu(proj[:, POOL_WIDTH:POOL_WIDTH + HG_WIDTH]).T
    v_ref[...] = proj[:, POOL_WIDTH + 2 * HG_WIDTH:POOL_WIDTH + 3 * HG_WIDTH]
    gate_ref[...] = _silu(proj[:, POOL_WIDTH + 3 * HG_WIDTH:])


def _mix_sample_state_kernel(s_ref, ft_ref, qt_ref, v_ref, snew_ref, o_ref):
    i = pl.program_id(0)
    lane = lax.broadcasted_iota(jnp.int32, (HG_HEAD_DIM, ft_ref.shape[1]), 1)
    for j in range(SAMPLE_STATE_BLOCK):
        mine = lane == i * SAMPLE_STATE_BLOCK + j
        for hd in range(HG_HEADS):
            r0 = hd * HG_HEAD_DIM
            f = jnp.sum(jnp.where(mine, ft_ref[r0:r0 + HG_HEAD_DIM, :], 0.0), axis=1, keepdims=True)
            q = jnp.sum(jnp.where(mine, qt_ref[r0:r0 + HG_HEAD_DIM, :], 0.0), axis=1, keepdims=True)
            v = v_ref[j:j + 1, r0:r0 + HG_HEAD_DIM]
            s_new = f * s_ref[j, hd] + (1.0 - f) * v
            snew_ref[j, hd] = s_new
            o_ref[j:j + 1, r0:r0 + HG_HEAD_DIM] = jnp.sum(q * s_new, axis=0, keepdims=True)


def _mix_sample_out_kernel(x_ref, mod_ref, o_ref, gate_ref, ypool_ref, gout_ref, wout_ref, gpost_ref,
                           gffn_ref, wrh_ref, wrl_ref, x1_ref, h2_ref, lg_ref):
    _, _, gt1, sh2, sc2, _ = _mods(mod_ref[...])
    o = o_ref[...]
    os_ = []
    for hd in range(HG_HEADS):
        oh = o[:, hd * HG_HEAD_DIM:(hd + 1) * HG_HEAD_DIM]
        os_.append(oh * lax.rsqrt(jnp.mean(oh * oh, axis=-1, keepdims=True) + EPS) * gout_ref[...])
    o_n = jnp.concatenate(os_, axis=1) * gate_ref[...]
    mix = (_dot(ypool_ref[...].astype(BF16), wout_ref[0:POOL_WIDTH, :])
           + _dot(o_n.astype(BF16), wout_ref[POOL_WIDTH:, :]))
    x1 = x_ref[...] + gt1 * _rms(mix, gpost_ref[...])
    h2 = _rms(x1, gffn_ref[...]) * (1.0 + sc2) + sh2
    x1_ref[...] = x1
    h2_ref[...] = h2.astype(BF16)
    lg_ref[...] = _router_logits(h2, wrh_ref[...], wrl_ref[...])


def _mix_sample(x, mod, hist, state, gpre, gpost, gffn, win, wpool, pscale, lbl, gout, wout, wrh, wrl):
    b = x.shape[0]
    bb = SAMPLE_STATE_BLOCK
    cp = pltpu.CompilerParams(vmem_limit_bytes=VMEM_LIMIT)
    ypool, npool, ft, qt, v, gate = pl.pallas_call(
        _mix_sample_in_kernel,
        out_shape=(jax.ShapeDtypeStruct((b, POOL_WIDTH), F32),
                   jax.ShapeDtypeStruct((b, POOL_HIST, POOL_WIDTH), F32),
                   jax.ShapeDtypeStruct((HG_WIDTH, b), F32),
                   jax.ShapeDtypeStruct((HG_WIDTH, b), F32),
                   jax.ShapeDtypeStruct((b, HG_WIDTH), F32),
                   jax.ShapeDtypeStruct((b, HG_WIDTH), F32)),
        compiler_params=cp,
        name="mix_sample_in",
    )(x, mod, gpre, win, wpool, pscale, lbl, hist)

    s_spec = pl.BlockSpec((bb, HG_HEADS, HG_HEAD_DIM, HG_HEAD_DIM), lambda i: (i, 0, 0, 0))
    col_spec = _full((HG_WIDTH, b))
    row_spec = pl.BlockSpec((bb, HG_WIDTH), lambda i: (i, 0))
    s_new, o = pl.pallas_call(
        _mix_sample_state_kernel,
        out_shape=(jax.ShapeDtypeStruct(state.shape, F32), jax.ShapeDtypeStruct((b, HG_WIDTH), F32)),
        grid=(b // bb,),
        in_specs=[s_spec, col_spec, col_spec, row_spec],
        out_specs=(s_spec, row_spec),
        compiler_params=pltpu.CompilerParams(dimension_semantics=("parallel",), vmem_limit_bytes=VMEM_LIMIT),
        name="mix_sample_state",
    )(state, ft, qt, v)

    x1, h2, lg = pl.pallas_call(
        _mix_sample_out_kernel,
        out_shape=(jax.ShapeDtypeStruct((b, D_MODEL), F32),
                   jax.ShapeDtypeStruct((b, D_MODEL), BF16),
                   jax.ShapeDtypeStruct((b, LANES), F32)),
        compiler_params=cp,
        name="mix_sample_out",
    )(x, mod, o, gate, ypool, gout, wout, gpost, gffn, wrh, wrl)
    return x1, h2, lg, npool, s_new


def _first_max(cur, idx, big):
    m = jnp.max(cur, axis=0, keepdims=True)
    first = jnp.min(jnp.where(cur == m, idx, big), axis=0, keepdims=True)
    return idx == first


def _route(logit, bias):
    n = logit.shape[1]
    scores = jax.nn.sigmoid(logit)
    sel = scores + bias
    neg = -jnp.inf

    sub = lax.broadcasted_iota(jnp.int32, (GROUP_SIZE, n), 0)
    gscore = []
    for g in range(N_GROUPS):
        sg = sel[g * GROUP_SIZE:(g + 1) * GROUP_SIZE, :]
        m1 = jnp.max(sg, axis=0, keepdims=True)
        rest = jnp.where(_first_max(sg, sub, GROUP_SIZE), neg, sg)
        gscore.append(m1 + jnp.max(rest, axis=0, keepdims=True))
    cur = jnp.concatenate(gscore, axis=0)
    gidx = lax.broadcasted_iota(jnp.int32, (N_GROUPS, n), 0)
    gmask = jnp.zeros((N_GROUPS, n), jnp.bool_)
    for _ in range(TOPK_GROUPS):
        pick = _first_max(cur, gidx, N_GROUPS)
        gmask = gmask | pick
        cur = jnp.where(pick, neg, cur)
    emask = jnp.concatenate(
        [jnp.broadcast_to(gmask[g:g + 1, :], (GROUP_SIZE, n)) for g in range(N_GROUPS)], axis=0)

    cur = jnp.where(emask, sel, neg)
    eidx = lax.broadcasted_iota(jnp.int32, (N_EXPERTS, n), 0)
    picks = []
    for _ in range(TOP_K):
        pick = _first_max(cur, eidx, N_EXPERTS)
        picks.append(pick)
        cur = jnp.where(pick, neg, cur)
    return scores, picks


def _any(masks):
    return functools.reduce(jnp.logical_or, masks)


def _router_kernel(lg_ref, bias_ref, wd_ref):
    n = lg_ref.shape[0]
    scores, picks = _route(lg_ref[...].T[0:N_EXPERTS, :], bias_ref[0:N_EXPERTS, :])
    chosen = _any(picks)
    wsum = jnp.sum(jnp.where(chosen, scores, 0.0), axis=0, keepdims=True)
    wd = jnp.where(chosen, scores / wsum * ROUTED_SCALE, 0.0)
    wd_ref[...] = jnp.concatenate([wd, jnp.zeros((LANES - N_EXPERTS, n), F32)], axis=0).T


def _router(logits, bias_col):
    t = logits.shape[0]
    tile = min(ROUTER_TILE, t)
    return pl.pallas_call(
        _router_kernel,
        out_shape=jax.ShapeDtypeStruct((t, LANES), F32),
        grid=(t // tile,),
        in_specs=[pl.BlockSpec((tile, LANES), lambda i: (i, 0)), _full(bias_col.shape)],
        out_specs=pl.BlockSpec((tile, LANES), lambda i: (i, 0)),
        compiler_params=pltpu.CompilerParams(dimension_semantics=("parallel",), vmem_limit_bytes=VMEM_LIMIT),
        name="router",
    )(logits, bias_col)


def _plan_kernel(lg_ref, bias_ref, upper_ref, dest_ref, wsel_ref, cnt_ref, loff_ref):
    scores, picks = _route(lg_ref[...].T[0:N_EXPERTS, :], bias_ref[0:N_EXPERTS, :])
    chosen = _any(picks)
    chosen_f = jnp.where(chosen, 1.0, 0.0)
    cnt = jnp.sum(chosen_f, axis=1, keepdims=True)
    units = jnp.floor((cnt + (RUN_ALIGN - 1)) / RUN_ALIGN)
    ei = lax.broadcasted_iota(jnp.int32, (N_EXPERTS, N_EXPERTS), 0)
    ej = lax.broadcasted_iota(jnp.int32, (N_EXPERTS, N_EXPERTS), 1)
    before = jnp.where(ej < ei, 1.0, 0.0).astype(BF16)
    loff = RUN_ALIGN * _dot(before, jnp.broadcast_to(units, (N_EXPERTS, LANES)).astype(BF16))[:, 0:1]
    slot = loff + _dot(chosen_f.astype(BF16), upper_ref[...])
    wsum = jnp.sum(jnp.where(chosen, scores, 0.0), axis=0, keepdims=True)
    pick_sum = lambda v: jnp.concatenate(
        [jnp.sum(jnp.where(p, v, 0.0), axis=0, keepdims=True) for p in picks], axis=0)
    dest_ref[...] = (pick_sum(slot) * ROW_SUB).astype(jnp.int32)
    wsel_ref[...] = pick_sum(scores) / wsum * ROUTED_SCALE
    cnt_ref[0] = jnp.broadcast_to(cnt, (N_EXPERTS, LANES))
    loff_ref[0] = jnp.broadcast_to(loff, (N_EXPERTS, LANES))


def _plan(logits, bias_col, upper):
    t = logits.shape[0]
    nw = t // PLAN_WINDOW
    per_w = pl.BlockSpec((1, N_EXPERTS, LANES), lambda w: (w, 0, 0))
    picks = pl.BlockSpec((TOP_K, PLAN_WINDOW), lambda w: (0, w))
    return pl.pallas_call(
        _plan_kernel,
        out_shape=(jax.ShapeDtypeStruct((TOP_K, t), jnp.int32), jax.ShapeDtypeStruct((TOP_K, t), F32),
                   jax.ShapeDtypeStruct((nw, N_EXPERTS, LANES), F32),
                   jax.ShapeDtypeStruct((nw, N_EXPERTS, LANES), F32)),
        grid=(nw,),
        in_specs=[pl.BlockSpec((PLAN_WINDOW, LANES), lambda w: (w, 0)), _full(bias_col.shape), _full(upper.shape)],
        out_specs=(picks, picks, per_w, per_w),
        compiler_params=pltpu.CompilerParams(dimension_semantics=("parallel",), vmem_limit_bytes=VMEM_LIMIT),
        name="plan",
    )(logits, bias_col, upper)


def _slot_rows(slot):
    return pl.ds(pl.multiple_of(slot * ROW_SUB, ROW_SUB), ROW_SUB)


def _rows_at(offset):
    return pl.ds(pl.multiple_of(offset, ROW_SUB), ROW_SUB)


def _dispatch_kernel(dest_ref, h_ref, x_ref, dest_s, rows_s, sem):
    plan_copy = pltpu.make_async_copy(dest_ref.at[0, 0], dest_s, sem)
    plan_copy.start()
    x_ref[...] = jnp.zeros_like(x_ref)
    for j in range(ROW_SUB):
        rows_s[pl.ds(j, PLAN_WINDOW, stride=ROW_SUB), :] = h_ref[:, j * LANES:(j + 1) * LANES]
    plan_copy.wait()

    def body(i, carry):
        for u in range(SCATTER_UNROLL):
            t = i * SCATTER_UNROLL + u
            row = rows_s[_slot_rows(t), :]
            for k in range(TOP_K):
                x_ref[_rows_at(dest_s[k * PLAN_WINDOW + t]), :] = row
        return carry

    lax.fori_loop(0, PLAN_WINDOW // SCATTER_UNROLL, body, 0)


def _dispatch(dest_w, h2):
    nw = dest_w.shape[0]
    region = REGION_ROWS * ROW_SUB
    return pl.pallas_call(
        _dispatch_kernel,
        out_shape=jax.ShapeDtypeStruct((nw * region, LANES), jnp.uint32),
        grid=(nw,),
        in_specs=[pl.BlockSpec((1, 1, TOP_K * PLAN_WINDOW), lambda w: (w, 0, 0)),
                  pl.BlockSpec((PLAN_WINDOW, ROW_WORDS), lambda w: (w, 0))],
        out_specs=pl.BlockSpec((region, LANES), lambda w: (w, 0)),
        scratch_shapes=[pltpu.SMEM((TOP_K * PLAN_WINDOW,), jnp.int32),
                        pltpu.VMEM((PLAN_WINDOW * ROW_SUB, LANES), jnp.uint32),
                        pltpu.SemaphoreType.DMA],
        compiler_params=pltpu.CompilerParams(dimension_semantics=("arbitrary",), vmem_limit_bytes=VMEM_LIMIT),
        name="dispatch",
    )(dest_w, h2)


def _unpack_rows(ref, n):
    parts = [_unpack_pair(ref[pl.ds(j, n, stride=ROW_SUB), :]) for j in range(ROW_SUB)]
    return jnp.concatenate([p[0] for p in parts] + [p[1] for p in parts], axis=1)


def _ffn_kernel(gq_ref, gs_ref, to_ref, tn_ref, x_hbm, w1_ref, w3_ref, w2_ref, y_hbm,
                w13_s, w2_s, xbuf, ybuf, rsem, wsem):
    del x_hbm
    e = pl.program_id(0)
    n_groups = gq_ref[N_EXPERTS]
    block = RUN_ALIGN * ROW_SUB
    tile = FFN_TILE * ROW_SUB

    def fetch(q, start):
        for j in range(FFN_GROUP):
            item = gs_ref[q] + j

            @pl.when(item < gs_ref[q + 1])
            def _():
                src = y_hbm.at[pl.ds(pl.multiple_of(to_ref[item] * block, block), tile)]
                cp = pltpu.make_async_copy(src, xbuf.at[q % 2, pl.ds(j * tile, tile)], rsem.at[q % 2])
                if start:
                    cp.start()
                else:
                    cp.wait()

    def writeback(q, start):
        for j in range(FFN_GROUP):
            item = gs_ref[q] + j
            for b in range(FFN_TILE // RUN_ALIGN):
                @pl.when((item < gs_ref[q + 1]) & (b < tn_ref[item]))
                def _():
                    dst = pl.multiple_of(to_ref[item] * block, block) + b * block
                    cp = pltpu.make_async_copy(ybuf.at[q % 2, pl.ds(j * tile + b * block, block)],
                                               y_hbm.at[pl.ds(dst, block)], wsem.at[q % 2])
                    if start:
                        cp.start()
                    else:
                        cp.wait()

    @pl.when(e == 0)
    def _():
        xbuf[...] = jnp.zeros_like(xbuf)
        fetch(0, True)

    w13_s[:, 0:D_EXPERT] = w1_ref[0].astype(BF16)
    w13_s[:, D_EXPERT:] = w3_ref[0].astype(BF16)
    w2_s[...] = w2_ref[0].astype(BF16)

    def group_body(q, carry):
        @pl.when(q + 1 < n_groups)
        def _():
            fetch(q + 1, True)

        @pl.when(q >= 2)
        def _():
            writeback(q - 2, False)

        fetch(q, False)
        x = _unpack_rows(xbuf.at[q % 2], FFN_GROUP * FFN_TILE).astype(BF16)
        h = _dot(x, w13_s[...])
        act = _silu(h[:, :D_EXPERT]) * h[:, D_EXPERT:]
        y = _dot(act.astype(BF16), w2_s[...])
        out = ybuf.at[q % 2]
        for j in range(ROW_SUB):
            out[pl.ds(j, FFN_GROUP * FFN_TILE, stride=ROW_SUB), :] = _pack_pair(
                y[:, j * LANES:(j + 1) * LANES], y[:, ROW_WORDS + j * LANES:ROW_WORDS + (j + 1) * LANES])
        writeback(q, True)
        return carry

    lax.fori_loop(gq_ref[e], gq_ref[e + 1], group_body, 0)

    @pl.when(e == pl.num_programs(0) - 1)
    def _():
        for d in range(2):
            @pl.when(n_groups - 1 - d >= 0)
            def _():
                writeback(n_groups - 1 - d, False)


def _expert_ffn(group_of_expert, group_start, item_off, item_blocks, x_slots, w1, w3, w2):
    of_expert = lambda e, *_: (e, 0, 0)
    tiles = pltpu.VMEM((2, FFN_GROUP * FFN_TILE * ROW_SUB, LANES), jnp.uint32)
    sems = pltpu.SemaphoreType.DMA((2,))
    grid_spec = pltpu.PrefetchScalarGridSpec(
        num_scalar_prefetch=4,
        grid=(N_EXPERTS,),
        in_specs=[pl.BlockSpec(memory_space=pl.ANY),
                  pl.BlockSpec((1, D_MODEL, D_EXPERT), of_expert),
                  pl.BlockSpec((1, D_MODEL, D_EXPERT), of_expert),
                  pl.BlockSpec((1, D_EXPERT, D_MODEL), of_expert)],
        out_specs=pl.BlockSpec(memory_space=pl.ANY),
        scratch_shapes=[pltpu.VMEM((D_MODEL, 2 * D_EXPERT), BF16), pltpu.VMEM((D_EXPERT, D_MODEL), BF16),
                        tiles, tiles, sems, sems])
    return pl.pallas_call(
        _ffn_kernel,
        out_shape=jax.ShapeDtypeStruct(x_slots.shape, jnp.uint32),
        grid_spec=grid_spec,
        input_output_aliases={4: 0},
        compiler_params=pltpu.CompilerParams(dimension_semantics=("arbitrary",), vmem_limit_bytes=VMEM_LIMIT),
        name="expert_ffn",
    )(group_of_expert, group_start, item_off, item_blocks, x_slots, w1, w3, w2)


def _work_items(cnt, loff):
    nw = cnt.shape[0]
    cnt = cnt.astype(jnp.int32).T.reshape(-1)
    loff = loff.astype(jnp.int32).T.reshape(-1)
    region = jnp.tile(jnp.arange(nw, dtype=jnp.int32) * REGION_ROWS, N_EXPERTS)
    n_items = nw * N_EXPERTS + (nw * PLAN_WINDOW * TOP_K) // FFN_TILE
    tiles = (cnt + FFN_TILE - 1) // FFN_TILE
    ends = jnp.cumsum(tiles)
    starts = ends - tiles
    i = jnp.arange(n_items + FFN_GROUP, dtype=jnp.int32)[:, None]
    mine = (starts[None, :] <= i) & (i < ends[None, :])
    of_pair = lambda v: jnp.sum(jnp.where(mine, v[None, :], 0), axis=1)
    r = i[:, 0] - of_pair(starts)
    row = of_pair(region + loff) + r * FFN_TILE
    own = jnp.clip(of_pair(cnt) - r * FFN_TILE, 0, FFN_TILE)

    zero = jnp.zeros((1,), jnp.int32)
    item_end = ends.reshape(N_EXPERTS, nw)[:, -1]
    item_start = jnp.concatenate([zero, item_end[:-1]])
    group_end = jnp.cumsum((item_end - item_start + FFN_GROUP - 1) // FFN_GROUP)
    group_first = jnp.concatenate([zero, group_end[:-1]])
    q = jnp.arange(N_EXPERTS + n_items // FFN_GROUP + 1, dtype=jnp.int32)[:, None]
    has = (group_first[None, :] <= q) & (q < group_end[None, :])
    of_expert = lambda v: jnp.sum(jnp.where(has, v[None, :], 0), axis=1)
    group_start = jnp.where(q[:, 0] < group_end[-1],
                            of_expert(item_start) + (q[:, 0] - of_expert(group_first)) * FFN_GROUP, item_end[-1])
    return (jnp.concatenate([zero, group_end]), group_start, row // RUN_ALIGN, (own + RUN_ALIGN - 1) // RUN_ALIGN)


def _combine_kernel(dest_ref, wsel_ref, y_ref, h_ref, x1_ref, gt2_ref, gpost_ref, ws13_ref, ws2_ref, out_ref,
                    dest_s, wsel_s, lo_s, hi_s, sem):
    s = pl.program_id(1)

    @pl.when(s == 0)
    def _():
        copies = (pltpu.make_async_copy(dest_ref.at[0, 0], dest_s, sem.at[0]),
                  pltpu.make_async_copy(wsel_ref.at[0, 0], wsel_s, sem.at[1]))
        for c in copies:
            c.start()
        for c in copies:
            c.wait()

    base = s * COMBINE_TILE

    def body(i, carry):
        for u in range(SCATTER_UNROLL):
            t = i * SCATTER_UNROLL + u
            acc_lo = jnp.zeros((ROW_SUB, LANES), F32)
            acc_hi = jnp.zeros((ROW_SUB, LANES), F32)
            for k in range(TOP_K):
                at = k * PLAN_WINDOW + base + t
                lo, hi = _unpack_pair(y_ref[_rows_at(dest_s[at]), :])
                w = wsel_s[at]
                acc_lo = acc_lo + w * lo
                acc_hi = acc_hi + w * hi
            lo_s[_slot_rows(t), :] = acc_lo
            hi_s[_slot_rows(t), :] = acc_hi
        return carry

    lax.fori_loop(0, COMBINE_TILE // SCATTER_UNROLL, body, 0)
    rows = lambda ref: [ref[pl.ds(j, COMBINE_TILE, stride=ROW_SUB), :] for j in range(ROW_SUB)]
    routed = jnp.concatenate(rows(lo_s) + rows(hi_s), axis=1)
    h_lo, h_hi = _unpack_pair(h_ref[...])
    x = jnp.concatenate([h_lo, h_hi], axis=1).astype(BF16)
    hs = _dot(x, ws13_ref[...])
    act = _silu(hs[:, :D_EXPERT]) * hs[:, D_EXPERT:]
    ff = routed + _dot(act.astype(BF16), ws2_ref[...])
    out_ref[...] = x1_ref[...] + gt2_ref[0] * _rms(ff, gpost_ref[...])


def _combine(dest_w, wsel_w, y_slots, h2, x1, gt2, tokens_per_gt2, gpost, ws13, ws2):
    nw = dest_w.shape[0]
    sub = PLAN_WINDOW // COMBINE_TILE
    plan = pl.BlockSpec((1, 1, TOP_K * PLAN_WINDOW), lambda w, s: (w, 0, 0))
    tok = lambda w, s: (w * sub + s, 0)
    return pl.pallas_call(
        _combine_kernel,
        out_shape=jax.ShapeDtypeStruct(x1.shape, F32),
        grid=(nw, sub),
        in_specs=[plan, plan,
                  pl.BlockSpec((REGION_ROWS * ROW_SUB, LANES), lambda w, s: (w, 0)),
                  pl.BlockSpec((COMBINE_TILE, ROW_WORDS), tok),
                  pl.BlockSpec((COMBINE_TILE, D_MODEL), tok),
                  pl.BlockSpec((1, 1, D_MODEL), lambda w, s: ((w * PLAN_WINDOW) // tokens_per_gt2, 0, 0)),
                  _full((1, D_MODEL)), _full(ws13.shape), _full(ws2.shape)],
        out_specs=pl.BlockSpec((COMBINE_TILE, D_MODEL), tok),
        scratch_shapes=[pltpu.SMEM((TOP_K * PLAN_WINDOW,), jnp.int32),
                        pltpu.SMEM((TOP_K * PLAN_WINDOW,), F32),
                        pltpu.VMEM((COMBINE_TILE * ROW_SUB, LANES), F32),
                        pltpu.VMEM((COMBINE_TILE * ROW_SUB, LANES), F32),
                        pltpu.SemaphoreType.DMA((2,))],
        compiler_params=pltpu.CompilerParams(dimension_semantics=("arbitrary", "arbitrary"),
                                             vmem_limit_bytes=VMEM_LIMIT),
        name="combine",
    )(dest_w, wsel_w, y_slots, h2, x1, gt2, gpost, ws13, ws2)


def _moe_kernel(h_ref, wd_ref, x1_ref, gt2_ref, gpost_ref, w1_ref, w3_ref, w2_ref, ws1_ref, ws3_ref, ws2_ref,
                out_ref, acc):
    e = pl.program_id(1)
    x = h_ref[...]

    @pl.when(e == 0)
    def _():
        hs = _silu(_dot(x, ws1_ref[...].astype(BF16))) * _dot(x, ws3_ref[...].astype(BF16))
        acc[...] = _dot(hs.astype(BF16), ws2_ref[...].astype(BF16))

    lane = lax.broadcasted_iota(jnp.int32, wd_ref.shape, 1)
    wcol = jnp.sum(jnp.where(lane == e, wd_ref[...], 0.0), axis=1, keepdims=True)
    act = _silu(_dot(x, w1_ref[0].astype(BF16))) * _dot(x, w3_ref[0].astype(BF16)) * wcol
    acc[...] += _dot(act.astype(BF16), w2_ref[0].astype(BF16))

    @pl.when(e == pl.num_programs(1) - 1)
    def _():
        out_ref[...] = x1_ref[...] + gt2_ref[0] * _rms(acc[...], gpost_ref[...])


def _moe(h2, wd, x1, gt2, rows_per_gt2, gpost, w1, w3, w2, ws1, ws3, ws2):
    t = h2.shape[0]
    tile = min(MOE_TILE, t)
    tok = lambda i, e: (i, 0)
    exp = lambda i, e: (e, 0, 0)
    return pl.pallas_call(
        _moe_kernel,
        out_shape=jax.ShapeDtypeStruct((t, D_MODEL), F32),
        grid=(t // tile, N_EXPERTS),
        in_specs=[pl.BlockSpec((tile, D_MODEL), tok),
                  pl.BlockSpec((tile, LANES), tok),
                  pl.BlockSpec((tile, D_MODEL), tok),
                  pl.BlockSpec((1,) + gt2.shape[1:], lambda i, e: (i // rows_per_gt2, 0, 0)),
                  _full((1, D_MODEL)),
                  pl.BlockSpec((1, D_MODEL, D_EXPERT), exp),
                  pl.BlockSpec((1, D_MODEL, D_EXPERT), exp),
                  pl.BlockSpec((1, D_EXPERT, D_MODEL), exp),
                  _full(ws1.shape), _full(ws3.shape), _full(ws2.shape)],
        out_specs=pl.BlockSpec((tile, D_MODEL), tok),
        scratch_shapes=[pltpu.VMEM((tile, D_MODEL), F32)],
        compiler_params=pltpu.CompilerParams(dimension_semantics=("parallel", "arbitrary"),
                                             vmem_limit_bytes=VMEM_LIMIT),
        name="moe",
    )(h2, wd, x1, gt2, gpost, w1, w3, w2, ws1, ws3, ws2)


def kernel(x_prompt, x_sample, c_prompt, c_sample, state_pool, state_hgrn, w_ada, b_ada, g_pre_mix, g_post_mix,
           w_in, w_pool, pool_scale, lb_logits, g_out_norm, w_out, g_pre_ffn, g_post_ffn, w_router, router_bias,
           w_exp_gate, w_exp_up, w_exp_down, w_sh_gate, w_sh_up, w_sh_down):
    assert w_ada.shape[0] == 1 and lb_logits.shape[0] == 2, "single-layer trunk"
    bp, lp, d = x_prompt.shape
    bs = x_sample.shape[0]
    row = lambda a: a[0].reshape(1, -1)

    n_mod = bp + bs
    pad = (-n_mod) % 16
    c_all = jnp.concatenate([c_prompt, c_sample, jnp.zeros((pad, d), F32)], axis=0)
    mod = _ada(c_all, w_ada[0], b_ada)
    mod_p = mod[:bp].reshape(bp, 1, 6 * d)
    mod_s = mod[bp:n_mod]

    win = w_in[0].astype(BF16)
    wout = w_out[0].astype(BF16)
    wpool = w_pool[0].astype(BF16)
    wr = jnp.pad(w_router[0], ((0, 0), (0, LANES - N_EXPERTS)))
    wrh = wr.astype(BF16)
    wrl = (wr - wrh.astype(F32)).astype(BF16)
    m2 = jnp.asarray(_M2_NP, BF16)
    gpre, gpost, gffn, gffn_post = row(g_pre_mix), row(g_post_mix), row(g_pre_ffn), row(g_post_ffn)
    pscale, gout = row(pool_scale), row(g_out_norm)
    bias_col = jnp.pad(router_bias[0], (0, LANES - N_EXPERTS)).reshape(LANES, 1)

    x1_p, h2_p, lg_p, pool_p, st_p = _mix_prompt(x_prompt, mod_p, gpre, gpost, gffn, win, wpool, pscale,
                                                  lb_logits, gout, wout, wrh, wrl, m2)
    x1_s, h2_s, lg_s, pool_s, st_s = _mix_sample(x_sample[:, 0, :], mod_s, state_pool[0], state_hgrn[0],
                                                  gpre, gpost, gffn, win, wpool, pscale, lb_logits, gout, wout,
                                                  wrh, wrl)

    experts = (w_exp_gate[0], w_exp_up[0], w_exp_down[0], w_sh_gate[0], w_sh_up[0], w_sh_down[0])
    tp = bp * lp
    nw = tp // PLAN_WINDOW
    assert lp % PLAN_WINDOW == 0
    upper = jnp.asarray(np.triu(np.ones((PLAN_WINDOW, PLAN_WINDOW), np.float32), 1), BF16)
    dest, wsel, cnt, loff = _plan(lg_p.reshape(tp, LANES), bias_col, upper)
    per_window = lambda a: a.reshape(TOP_K, nw, PLAN_WINDOW).transpose(1, 0, 2).reshape(nw, 1, TOP_K * PLAN_WINDOW)
    dest_w, wsel_w = per_window(dest), per_window(wsel)
    h2_rows = h2_p.reshape(tp, ROW_WORDS)
    x_slots = _dispatch(dest_w, h2_rows)
    y_slots = _expert_ffn(*_work_items(cnt[:, :, 0], loff[:, :, 0]), x_slots, *experts[:3])
    ws13 = jnp.concatenate([experts[3], experts[4]], axis=1).astype(BF16)
    gt2_p = mod_p[:, :, 5 * d:]
    y_p = _combine(dest_w, wsel_w, y_slots, h2_rows, x1_p.reshape(tp, d), gt2_p, lp, gffn_post,
                   ws13, experts[5].astype(BF16))
    wd_s = _router(lg_s, bias_col)
    gt2_s = mod_s[:, 5 * d:].reshape(1, bs, d)
    y_s = _moe(h2_s, wd_s, x1_s, gt2_s, 1, gffn_post, *experts)

    return (y_p.reshape(bp, lp, d), y_s.reshape(bs, 1, d), pool_p[None], st_p[None], pool_s[None], st_s[None])
```

```python
import functools

import numpy as np
import jax
import jax.numpy as jnp
from jax import lax
from jax.experimental import pallas as pl
from jax.experimental.pallas import tpu as pltpu

F32 = jnp.float32
BF16 = jnp.bfloat16

D_MODEL = 1024
POOL_WIDTH = 512
POOL_WINDOWS = (2, 4, 8, 16)
POOL_GROUP = 128
POOL_HIST = 15
HG_WIDTH = 512
HG_HEADS = 4
HG_HEAD_DIM = 128
IN_WIDTH = POOL_WIDTH + 4 * HG_WIDTH
N_EXPERTS = 64
TOP_K = 8
N_GROUPS = 8
TOPK_GROUPS = 4
GROUP_SIZE = N_EXPERTS // N_GROUPS
D_EXPERT = 256
ROUTED_SCALE = 2.5
EPS = 1e-6

LANES = 128
CHUNK = 64
TIME_TILE = 256
ADA_TILE_N = 512
ROUTER_TILE = 512
MOE_TILE = 1024
SAMPLE_STATE_BLOCK = 8
PLAN_WINDOW = 1024
RUN_ALIGN = 32
FFN_TILE = 160
REGION_ROWS = PLAN_WINDOW * TOP_K + N_EXPERTS * (RUN_ALIGN - 1) + FFN_TILE
ROW_WORDS = D_MODEL // 2
ROW_SUB = ROW_WORDS // LANES
COMBINE_TILE = 256
SCATTER_UNROLL = 4
FFN_GROUP = 4
VMEM_LIMIT = 58 * 1024 * 1024

_LEVELS = (64, 32, 16, 8, 4, 2)
_BLK_CUM = 0
_BLK_END = 1


def _level_blocks():
    c = CHUNK
    i = np.arange(c)[:, None]
    s = np.arange(c)[None, :]
    blocks = [(s <= i), (s > i)]
    index = {}
    for lvl in _LEVELS:
        seg = (i // lvl) * lvl
        mid = seg + lvl // 2
        ref = mid - 1
        index[lvl] = len(blocks)
        blocks.append((i >= mid) & (s > ref) & (s <= i))
        if lvl > 2:
            blocks.append((i < mid) & (s > i) & (s <= ref))
    m = np.concatenate(blocks, axis=0).astype(np.float32)
    return np.concatenate([m, m], axis=1), index


_M2_NP, _LEVEL_INDEX = _level_blocks()


def _dot(a, b):
    return jnp.dot(a, b, preferred_element_type=F32)


def _dot_nt(a, b):
    return lax.dot_general(a, b, (((1,), (1,)), ((), ())), preferred_element_type=F32)


def _split2(x):
    hi = x.astype(BF16)
    lo = (x - hi.astype(F32)).astype(BF16)
    return hi, lo


def _pack_pair(a, b):
    lo = lax.bitcast_convert_type(a.astype(BF16).astype(F32), jnp.uint32)
    hi = lax.bitcast_convert_type(b.astype(BF16).astype(F32), jnp.uint32)
    return (lo >> 16) | hi


def _unpack_pair(words):
    lo = lax.bitcast_convert_type(words << 16, F32)
    hi = lax.bitcast_convert_type(words & jnp.uint32(0xFFFF0000), F32)
    return lo, hi


def _silu(x):
    return x * jax.nn.sigmoid(x)


def _rms(x, g):
    return x * lax.rsqrt(jnp.mean(x * x, axis=-1, keepdims=True) + EPS) * g


def _mods(m):
    return [m[:, j * D_MODEL:(j + 1) * D_MODEL] for j in range(6)]


def _forget_lower_bound(lbl):
    mx = jnp.max(lbl, axis=0, keepdims=True)
    e = jnp.exp(lbl - mx)
    return e[0:1] / jnp.sum(e, axis=0, keepdims=True)


def _router_logits(h2, wrh, wrl):
    hi, lo = _split2(h2)
    return _dot(hi, wrh) + _dot(lo, wrh) + _dot(hi, wrl)


def _ada_kernel(c_ref, w_ref, b_ref, o_ref):
    a_hi, a_lo = _split2(_silu(c_ref[...]))
    w_hi, w_lo = _split2(w_ref[...])
    o_ref[...] = _dot(a_hi, w_hi) + _dot(a_lo, w_hi) + _dot(a_hi, w_lo) + b_ref[...]


def _ada(c_all, w_ada, b_ada):
    rows = c_all.shape[0]
    n = w_ada.shape[1]
    return pl.pallas_call(
        _ada_kernel,
        out_shape=jax.ShapeDtypeStruct((rows, n), F32),
        grid=(n // ADA_TILE_N,),
        in_specs=[pl.BlockSpec((rows, D_MODEL), lambda j: (0, 0)),
                  pl.BlockSpec((D_MODEL, ADA_TILE_N), lambda j: (0, j)),
                  pl.BlockSpec((1, ADA_TILE_N), lambda j: (0, j))],
        out_specs=pl.BlockSpec((rows, ADA_TILE_N), lambda j: (0, j)),
        compiler_params=pltpu.CompilerParams(dimension_semantics=("parallel",),
                                             vmem_limit_bytes=VMEM_LIMIT),
        name="ada",
    )(c_all, w_ada, b_ada)


def _level_masks():
    i = lax.broadcasted_iota(jnp.int32, (CHUNK, CHUNK), 0)
    j = lax.broadcasted_iota(jnp.int32, (CHUNK, CHUNK), 1)
    masks = {}
    for lvl in _LEVELS:
        sh = lvl.bit_length() - 1
        same = (i >> sh) == (j >> sh)
        upper = ((i >> (sh - 1)) & 1) == 1
        lower = ((j >> (sh - 1)) & 1) == 0
        masks[lvl] = same & upper & lower
    return masks, i == j


def _hgrn_chunk_head(q, k, v, e_all, lane0, st, masks, eye):
    def blk(n):
        return e_all[n * CHUNK:(n + 1) * CHUNK, lane0:lane0 + HG_HEAD_DIM]

    b = blk(_BLK_CUM)
    a = jnp.where(eye, _dot_nt(q.astype(BF16), k.astype(BF16)), 0.0)
    for lvl in _LEVELS:
        n = _LEVEL_INDEX[lvl]
        ql = (q * jnp.exp(blk(n))).astype(BF16)
        kl = (k * jnp.exp(blk(n + 1))).astype(BF16) if lvl > 2 else k.astype(BF16)
        a = a + jnp.where(masks[lvl], _dot_nt(ql, kl), 0.0)
    st_b = st.astype(BF16)
    o = _dot(a.astype(BF16), v.astype(BF16)) + _dot_nt((q * jnp.exp(b)).astype(BF16), st_b)
    k_end = (k * jnp.exp(blk(_BLK_END))).astype(BF16)
    decay = jnp.exp(b[CHUNK - 1:CHUNK, :])
    st_new = st * decay + _dot(v.T.astype(BF16), k_end)
    return o, st_new


def _mix_prompt_kernel(x_ref, mod_ref, gpre_ref, gpost_ref, gffn_ref, win_ref, wpool_ref, pscale_ref,
                       lbl_ref, gout_ref, wout_ref, wrh_ref, wrl_ref, m2_ref,
                       x1_ref, h2_ref, lg_ref, pool_ref, st_ref,
                       st_s, ubuf, q_s, k_s, v_s, g_s, o_s):
    t = pl.program_id(1)
    n_t = pl.num_programs(1)
    tt = TIME_TILE

    @pl.when(t == 0)
    def _():
        st_s[...] = jnp.zeros_like(st_s)
        ubuf[0:16, :] = jnp.zeros((16, POOL_WIDTH), F32)

    xt = x_ref[0]
    sh1, sc1, gt1, sh2, sc2, gt2 = _mods(mod_ref[0])
    h = _rms(xt, gpre_ref[...]) * (1.0 + sc1) + sh1
    proj = _dot(h.astype(BF16), win_ref[...])

    u = proj[:, :POOL_WIDTH]
    ubuf[16:16 + tt, :] = u
    pos = (t * tt + lax.broadcasted_iota(jnp.int32, (tt, 1), 0) + 1).astype(F32)
    ys = []
    for g, w in enumerate(POOL_WINDOWS):
        s = ubuf[:, g * POOL_GROUP:(g + 1) * POOL_GROUP]
        off = 0
        for step in range(w.bit_length() - 1):
            sh = 1 << step
            s = s[sh:, :] + s[:-sh, :]
            off += sh
        ws = s[16 - off:16 - off + tt, :]
        cnt = jnp.minimum(pos, float(w))
        d = ws / cnt - u[:, g * POOL_GROUP:(g + 1) * POOL_GROUP]
        ys.append(_dot(d.astype(BF16), wpool_ref[g]))
    y_pool = jnp.concatenate(ys, axis=1) * pscale_ref[...]
    ubuf[0:16, :] = ubuf[tt:tt + 16, :]

    @pl.when(t == n_t - 1)
    def _():
        pool_ref[0] = u[tt - POOL_HIST:, :]

    lb = _forget_lower_bound(lbl_ref[...])
    f = lb + (1.0 - lb) * jax.nn.sigmoid(proj[:, POOL_WIDTH + HG_WIDTH:POOL_WIDTH + 2 * HG_WIDTH])
    q_s[...] = _silu(proj[:, POOL_WIDTH:POOL_WIDTH + HG_WIDTH])
    k_s[...] = 1.0 - f
    v_s[...] = proj[:, POOL_WIDTH + 2 * HG_WIDTH:POOL_WIDTH + 3 * HG_WIDTH]
    g_s[...] = jnp.log(f)
    gate = _silu(proj[:, POOL_WIDTH + 3 * HG_WIDTH:])

    masks, eye = _level_masks()
    m2 = m2_ref[...]

    states = [st_s[hd] for hd in range(HG_HEADS)]
    for c in range(tt // CHUNK):
        rows = slice(c * CHUNK, (c + 1) * CHUNK)
        g_hi, g_lo = _split2(g_s[rows, :])
        e_all = _dot(m2, jnp.concatenate([g_hi, g_lo], axis=0))
        for hd in range(HG_HEADS):
            lane0 = hd * HG_HEAD_DIM
            lanes = slice(lane0, lane0 + HG_HEAD_DIM)
            o, states[hd] = _hgrn_chunk_head(q_s[rows, lanes], k_s[rows, lanes], v_s[rows, lanes], e_all, lane0,
                                             states[hd], masks, eye)
            o_s[rows, lanes] = o
    for hd in range(HG_HEADS):
        st_s[hd] = states[hd]

    o = o_s[...]
    os_ = []
    for hd in range(HG_HEADS):
        oh = o[:, hd * HG_HEAD_DIM:(hd + 1) * HG_HEAD_DIM]
        os_.append(oh * lax.rsqrt(jnp.mean(oh * oh, axis=-1, keepdims=True) + EPS) * gout_ref[...])
    o_n = jnp.concatenate(os_, axis=1) * gate

    mix = _dot(y_pool.astype(BF16), wout_ref[0:POOL_WIDTH, :]) + _dot(o_n.astype(BF16), wout_ref[POOL_WIDTH:, :])
    x1 = xt + gt1 * _rms(mix, gpost_ref[...])
    h2 = _rms(x1, gffn_ref[...]) * (1.0 + sc2) + sh2
    x1_ref[0] = x1
    h2_ref[0] = _pack_pair(h2[:, :ROW_WORDS], h2[:, ROW_WORDS:])
    lg_ref[0] = _router_logits(h2, wrh_ref[...], wrl_ref[...])

    @pl.when(t == n_t - 1)
    def _():
        for hd in range(HG_HEADS):
            st_ref[0, hd] = st_s[hd].T


def _full(shape):
    nd = len(shape)
    return pl.BlockSpec(shape, lambda *_: (0,) * nd)


def _mix_prompt(x, mod, gpre, gpost, gffn, win, wpool, pscale, lbl, gout, wout, wrh, wrl, m2):
    b, l, d = x.shape
    tt = TIME_TILE
    tile = lambda i, j: (i, j, 0)
    per_b = lambda i, j: (i, 0, 0)
    return pl.pallas_call(
        _mix_prompt_kernel,
        out_shape=(jax.ShapeDtypeStruct((b, l, d), F32),
                   jax.ShapeDtypeStruct((b, l, ROW_WORDS), jnp.uint32),
                   jax.ShapeDtypeStruct((b, l, LANES), F32),
                   jax.ShapeDtypeStruct((b, POOL_HIST, POOL_WIDTH), F32),
                   jax.ShapeDtypeStruct((b, HG_HEADS, HG_HEAD_DIM, HG_HEAD_DIM), F32)),
        grid=(b, l // tt),
        in_specs=[pl.BlockSpec((1, tt, d), tile),
                  pl.BlockSpec((1, 1, 6 * d), per_b),
                  _full((1, d)), _full((1, d)), _full((1, d)),
                  _full(win.shape), _full(wpool.shape), _full((1, POOL_WIDTH)),
                  _full(lbl.shape), _full((1, HG_HEAD_DIM)), _full(wout.shape),
                  _full(wrh.shape), _full(wrl.shape), _full(m2.shape)],
        out_specs=(pl.BlockSpec((1, tt, d), tile),
                   pl.BlockSpec((1, tt, ROW_WORDS), tile),
                   pl.BlockSpec((1, tt, LANES), tile),
                   pl.BlockSpec((1, POOL_HIST, POOL_WIDTH), per_b),
                   pl.BlockSpec((1, HG_HEADS, HG_HEAD_DIM, HG_HEAD_DIM), lambda i, j: (i, 0, 0, 0))),
        scratch_shapes=[pltpu.VMEM((HG_HEADS, HG_HEAD_DIM, HG_HEAD_DIM), F32),
                        pltpu.VMEM((tt + 16, POOL_WIDTH), F32),
                        pltpu.VMEM((tt, HG_WIDTH), F32), pltpu.VMEM((tt, HG_WIDTH), F32),
                        pltpu.VMEM((tt, HG_WIDTH), F32), pltpu.VMEM((tt, HG_WIDTH), F32),
                        pltpu.VMEM((tt, HG_WIDTH), F32)],
        compiler_params=pltpu.CompilerParams(dimension_semantics=("parallel", "arbitrary"),
                                             vmem_limit_bytes=VMEM_LIMIT),
        name="mix_prompt",
    )(x, mod, gpre, gpost, gffn, win, wpool, pscale, lbl, gout, wout, wrh, wrl, m2)


def _mix_sample_in_kernel(x_ref, mod_ref, gpre_ref, win_ref, wpool_ref, pscale_ref, lbl_ref, hist_ref,
                          ypool_ref, npool_ref, ft_ref, qt_ref, v_ref, gate_ref):
    xt = x_ref[...]
    sh1, sc1 = _mods(mod_ref[...])[:2]
    h = _rms(xt, gpre_ref[...]) * (1.0 + sc1) + sh1
    proj = _dot(h.astype(BF16), win_ref[...])
    u = proj[:, :POOL_WIDTH]
    row = lax.broadcasted_iota(jnp.int32, (hist_ref.shape[0], POOL_HIST, POOL_GROUP), 1)
    ys = []
    for g, w in enumerate(POOL_WINDOWS):
        sl = slice(g * POOL_GROUP, (g + 1) * POOL_GROUP)
        past = jnp.sum(jnp.where(row >= POOL_HIST - (w - 1), hist_ref[:, :, sl], 0.0), axis=1)
        ug = u[:, sl]
        d = (past + ug) / float(w) - ug
        ys.append(_dot(d.astype(BF16), wpool_ref[g]))
    ypool_ref[...] = jnp.concatenate(ys, axis=1) * pscale_ref[...]
    npool_ref[:, 0:POOL_HIST - 1, :] = hist_ref[:, 1:POOL_HIST, :]
    npool_ref[:, POOL_HIST - 1, :] = u

    lb = _forget_lower_bound(lbl_ref[...])
    f = lb + (1.0 - lb) * jax.nn.sigmoid(proj[:, POOL_WIDTH + HG_WIDTH:POOL_WIDTH + 2 * HG_WIDTH])
    ft_ref[...] = f.T
    qt_ref[...] = _silu(proj[:, POOL_WIDTH:POOL_WIDTH + HG_WIDTH]).T
    v_ref[...] = proj[:, POOL_WIDTH + 2 * HG_WIDTH:POOL_WIDTH + 3 * HG_WIDTH]
    gate_ref[...] = _silu(proj[:, POOL_WIDTH + 3 * HG_WIDTH:])


def _mix_sample_state_kernel(s_ref, ft_ref, qt_ref, v_ref, snew_ref, o_ref):
    i = pl.program_id(0)
    lane = lax.broadcasted_iota(jnp.int32, (HG_HEAD_DIM, ft_ref.shape[1]), 1)
    for j in range(SAMPLE_STATE_BLOCK):
        mine = lane == i * SAMPLE_STATE_BLOCK + j
        for hd in range(HG_HEADS):
            r0 = hd * HG_HEAD_DIM
            f = jnp.sum(jnp.where(mine, ft_ref[r0:r0 + HG_HEAD_DIM, :], 0.0), axis=1, keepdims=True)
            q = jnp.sum(jnp.where(mine, qt_ref[r0:r0 + HG_HEAD_DIM, :], 0.0), axis=1, keepdims=True)
            v = v_ref[j:j + 1, r0:r0 + HG_HEAD_DIM]
            s_new = f * s_ref[j, hd] + (1.0 - f) * v
            snew_ref[j, hd] = s_new
            o_ref[j:j + 1, r0:r0 + HG_HEAD_DIM] = jnp.sum(q * s_new, axis=0, keepdims=True)


def _mix_sample_out_kernel(x_ref, mod_ref, o_ref, gate_ref, ypool_ref, gout_ref, wout_ref, gpost_ref,
                           gffn_ref, wrh_ref, wrl_ref, x1_ref, h2_ref, lg_ref):
    _, _, gt1, sh2, sc2, _ = _mods(mod_ref[...])
    o = o_ref[...]
    os_ = []
    for hd in range(HG_HEADS):
        oh = o[:, hd * HG_HEAD_DIM:(hd + 1) * HG_HEAD_DIM]
        os_.append(oh * lax.rsqrt(jnp.mean(oh * oh, axis=-1, keepdims=True) + EPS) * gout_ref[...])
    o_n = jnp.concatenate(os_, axis=1) * gate_ref[...]
    mix = (_dot(ypool_ref[...].astype(BF16), wout_ref[0:POOL_WIDTH, :])
           + _dot(o_n.astype(BF16), wout_ref[POOL_WIDTH:, :]))
    x1 = x_ref[...] + gt1 * _rms(mix, gpost_ref[...])
    h2 = _rms(x1, gffn_ref[...]) * (1.0 + sc2) + sh2
    x1_ref[...] = x1
    h2_ref[...] = h2.astype(BF16)
    lg_ref[...] = _router_logits(h2, wrh_ref[...], wrl_ref[...])


def _mix_sample(x, mod, hist, state, gpre, gpost, gffn, win, wpool, pscale, lbl, gout, wout, wrh, wrl):
    b = x.shape[0]
    bb = SAMPLE_STATE_BLOCK
    cp = pltpu.CompilerParams(vmem_limit_bytes=VMEM_LIMIT)
    ypool, npool, ft, qt, v, gate = pl.pallas_call(
        _mix_sample_in_kernel,
        out_shape=(jax.ShapeDtypeStruct((b, POOL_WIDTH), F32),
                   jax.ShapeDtypeStruct((b, POOL_HIST, POOL_WIDTH), F32),
                   jax.ShapeDtypeStruct((HG_WIDTH, b), F32),
                   jax.ShapeDtypeStruct((HG_WIDTH, b), F32),
                   jax.ShapeDtypeStruct((b, HG_WIDTH), F32),
                   jax.ShapeDtypeStruct((b, HG_WIDTH), F32)),
        compiler_params=cp,
        name="mix_sample_in",
    )(x, mod, gpre, win, wpool, pscale, lbl, hist)

    s_spec = pl.BlockSpec((bb, HG_HEADS, HG_HEAD_DIM, HG_HEAD_DIM), lambda i: (i, 0, 0, 0))
    col_spec = _full((HG_WIDTH, b))
    row_spec = pl.BlockSpec((bb, HG_WIDTH), lambda i: (i, 0))
    s_new, o = pl.pallas_call(
        _mix_sample_state_kernel,
        out_shape=(jax.ShapeDtypeStruct(state.shape, F32), jax.ShapeDtypeStruct((b, HG_WIDTH), F32)),
        grid=(b // bb,),
        in_specs=[s_spec, col_spec, col_spec, row_spec],
        out_specs=(s_spec, row_spec),
        compiler_params=pltpu.CompilerParams(dimension_semantics=("parallel",), vmem_limit_bytes=VMEM_LIMIT),
        name="mix_sample_state",
    )(state, ft, qt, v)

    x1, h2, lg = pl.pallas_call(
        _mix_sample_out_kernel,
        out_shape=(jax.ShapeDtypeStruct((b, D_MODEL), F32),
                   jax.ShapeDtypeStruct((b, D_MODEL), BF16),
                   jax.ShapeDtypeStruct((b, LANES), F32)),
        compiler_params=cp,
        name="mix_sample_out",
    )(x, mod, o, gate, ypool, gout, wout, gpost, gffn, wrh, wrl)
    return x1, h2, lg, npool, s_new


def _first_max(cur, idx, big):
    m = jnp.max(cur, axis=0, keepdims=True)
    first = jnp.min(jnp.where(cur == m, idx, big), axis=0, keepdims=True)
    return idx == first


def _route(logit, bias):
    n = logit.shape[1]
    scores = jax.nn.sigmoid(logit)
    sel = scores + bias
    neg = -jnp.inf

    sub = lax.broadcasted_iota(jnp.int32, (GROUP_SIZE, n), 0)
    gscore = []
    for g in range(N_GROUPS):
        sg = sel[g * GROUP_SIZE:(g + 1) * GROUP_SIZE, :]
        m1 = jnp.max(sg, axis=0, keepdims=True)
        rest = jnp.where(_first_max(sg, sub, GROUP_SIZE), neg, sg)
        gscore.append(m1 + jnp.max(rest, axis=0, keepdims=True))
    cur = jnp.concatenate(gscore, axis=0)
    gidx = lax.broadcasted_iota(jnp.int32, (N_GROUPS, n), 0)
    gmask = jnp.zeros((N_GROUPS, n), jnp.bool_)
    for _ in range(TOPK_GROUPS):
        pick = _first_max(cur, gidx, N_GROUPS)
        gmask = gmask | pick
        cur = jnp.where(pick, neg, cur)
    emask = jnp.concatenate(
        [jnp.broadcast_to(gmask[g:g + 1, :], (GROUP_SIZE, n)) for g in range(N_GROUPS)], axis=0)

    cur = jnp.where(emask, sel, neg)
    eidx = lax.broadcasted_iota(jnp.int32, (N_EXPERTS, n), 0)
    picks = []
    for _ in range(TOP_K):
        pick = _first_max(cur, eidx, N_EXPERTS)
        picks.append(pick)
        cur = jnp.where(pick, neg, cur)
    return scores, picks


def _any(masks):
    return functools.reduce(jnp.logical_or, masks)


def _router_kernel(lg_ref, bias_ref, wd_ref):
    n = lg_ref.shape[0]
    scores, picks = _route(lg_ref[...].T[0:N_EXPERTS, :], bias_ref[0:N_EXPERTS, :])
    chosen = _any(picks)
    wsum = jnp.sum(jnp.where(chosen, scores, 0.0), axis=0, keepdims=True)
    wd = jnp.where(chosen, scores / wsum * ROUTED_SCALE, 0.0)
    wd_ref[...] = jnp.concatenate([wd, jnp.zeros((LANES - N_EXPERTS, n), F32)], axis=0).T


def _router(logits, bias_col):
    t = logits.shape[0]
    tile = min(ROUTER_TILE, t)
    return pl.pallas_call(
        _router_kernel,
        out_shape=jax.ShapeDtypeStruct((t, LANES), F32),
        grid=(t // tile,),
        in_specs=[pl.BlockSpec((tile, LANES), lambda i: (i, 0)), _full(bias_col.shape)],
        out_specs=pl.BlockSpec((tile, LANES), lambda i: (i, 0)),
        compiler_params=pltpu.CompilerParams(dimension_semantics=("parallel",), vmem_limit_bytes=VMEM_LIMIT),
        name="router",
    )(logits, bias_col)


def _plan_kernel(lg_ref, bias_ref, upper_ref, dest_ref, wsel_ref, cnt_ref, loff_ref):
    scores, picks = _route(lg_ref[...].T[0:N_EXPERTS, :], bias_ref[0:N_EXPERTS, :])
    chosen = _any(picks)
    chosen_f = jnp.where(chosen, 1.0, 0.0)
    cnt = jnp.sum(chosen_f, axis=1, keepdims=True)
    units = jnp.floor((cnt + (RUN_ALIGN - 1)) / RUN_ALIGN)
    ei = lax.broadcasted_iota(jnp.int32, (N_EXPERTS, N_EXPERTS), 0)
    ej = lax.broadcasted_iota(jnp.int32, (N_EXPERTS, N_EXPERTS), 1)
    before = jnp.where(ej < ei, 1.0, 0.0).astype(BF16)
    loff = RUN_ALIGN * _dot(before, jnp.broadcast_to(units, (N_EXPERTS, LANES)).astype(BF16))[:, 0:1]
    slot = loff + _dot(chosen_f.astype(BF16), upper_ref[...])
    wsum = jnp.sum(jnp.where(chosen, scores, 0.0), axis=0, keepdims=True)
    pick_sum = lambda v: jnp.concatenate(
        [jnp.sum(jnp.where(p, v, 0.0), axis=0, keepdims=True) for p in picks], axis=0)
    dest_ref[...] = (pick_sum(slot) * ROW_SUB).astype(jnp.int32)
    wsel_ref[...] = pick_sum(scores) / wsum * ROUTED_SCALE
    cnt_ref[0] = jnp.broadcast_to(cnt, (N_EXPERTS, LANES))
    loff_ref[0] = jnp.broadcast_to(loff, (N_EXPERTS, LANES))


def _plan(logits, bias_col, upper):
    t = logits.shape[0]
    nw = t // PLAN_WINDOW
    per_w = pl.BlockSpec((1, N_EXPERTS, LANES), lambda w: (w, 0, 0))
    picks = pl.BlockSpec((TOP_K, PLAN_WINDOW), lambda w: (0, w))
    return pl.pallas_call(
        _plan_kernel,
        out_shape=(jax.ShapeDtypeStruct((TOP_K, t), jnp.int32), jax.ShapeDtypeStruct((TOP_K, t), F32),
                   jax.ShapeDtypeStruct((nw, N_EXPERTS, LANES), F32),
                   jax.ShapeDtypeStruct((nw, N_EXPERTS, LANES), F32)),
        grid=(nw,),
        in_specs=[pl.BlockSpec((PLAN_WINDOW, LANES), lambda w: (w, 0)), _full(bias_col.shape), _full(upper.shape)],
        out_specs=(picks, picks, per_w, per_w),
        compiler_params=pltpu.CompilerParams(dimension_semantics=("parallel",), vmem_limit_bytes=VMEM_LIMIT),
        name="plan",
    )(logits, bias_col, upper)


def _slot_rows(slot):
    return pl.ds(pl.multiple_of(slot * ROW_SUB, ROW_SUB), ROW_SUB)


def _rows_at(offset):
    return pl.ds(pl.multiple_of(offset, ROW_SUB), ROW_SUB)


def _dispatch_kernel(dest_ref, h_ref, x_ref, dest_s, rows_s, sem):
    plan_copy = pltpu.make_async_copy(dest_ref.at[0, 0], dest_s, sem)
    plan_copy.start()
    x_ref[...] = jnp.zeros_like(x_ref)
    for j in range(ROW_SUB):
        rows_s[pl.ds(j, PLAN_WINDOW, stride=ROW_SUB), :] = h_ref[:, j * LANES:(j + 1) * LANES]
    plan_copy.wait()

    def body(i, carry):
        for u in range(SCATTER_UNROLL):
            t = i * SCATTER_UNROLL + u
            row = rows_s[_slot_rows(t), :]
            for k in range(TOP_K):
                x_ref[_rows_at(dest_s[k * PLAN_WINDOW + t]), :] = row
        return carry

    lax.fori_loop(0, PLAN_WINDOW // SCATTER_UNROLL, body, 0)


def _dispatch(dest_w, h2):
    nw = dest_w.shape[0]
    region = REGION_ROWS * ROW_SUB
    return pl.pallas_call(
        _dispatch_kernel,
        out_shape=jax.ShapeDtypeStruct((nw * region, LANES), jnp.uint32),
        grid=(nw,),
        in_specs=[pl.BlockSpec((1, 1, TOP_K * PLAN_WINDOW), lambda w: (w, 0, 0)),
                  pl.BlockSpec((PLAN_WINDOW, ROW_WORDS), lambda w: (w, 0))],
        out_specs=pl.BlockSpec((region, LANES), lambda w: (w, 0)),
        scratch_shapes=[pltpu.SMEM((TOP_K * PLAN_WINDOW,), jnp.int32),
                        pltpu.VMEM((PLAN_WINDOW * ROW_SUB, LANES), jnp.uint32),
                        pltpu.SemaphoreType.DMA],
        compiler_params=pltpu.CompilerParams(dimension_semantics=("arbitrary",), vmem_limit_bytes=VMEM_LIMIT),
        name="dispatch",
    )(dest_w, h2)


def _unpack_rows(ref, n):
    parts = [_unpack_pair(ref[pl.ds(j, n, stride=ROW_SUB), :]) for j in range(ROW_SUB)]
    return jnp.concatenate([p[0] for p in parts] + [p[1] for p in parts], axis=1)


def _ffn_kernel(gq_ref, gs_ref, to_ref, tn_ref, x_hbm, w1_ref, w3_ref, w2_ref, y_hbm,
                w13_s, w2_s, xbuf, ybuf, rsem, wsem):
    del x_hbm
    e = pl.program_id(0)
    n_groups = gq_ref[N_EXPERTS]
    block = RUN_ALIGN * ROW_SUB
    tile = FFN_TILE * ROW_SUB

    def go(cp, start):
        if start:
            cp.start()
        else:
            cp.wait()

    def fetch(q, start):
        def one(j):
            src = y_hbm.at[pl.ds(pl.multiple_of(to_ref[gs_ref[q] + j] * block, block), tile)]
            go(pltpu.make_async_copy(src, xbuf.at[q % 2, pl.ds(j * tile, tile)], rsem.at[q % 2]), start)

        size = gs_ref[q + 1] - gs_ref[q]

        @pl.when(size == FFN_GROUP)
        def _():
            for j in range(FFN_GROUP):
                one(j)

        @pl.when(size < FFN_GROUP)
        def _():
            for j in range(FFN_GROUP - 1):
                pl.when(j < size)(functools.partial(one, j))

    def writeback(q, start):
        most = FFN_TILE // RUN_ALIGN - 1

        def blocks(j, b, n):
            dst = pl.multiple_of(to_ref[gs_ref[q] + j] * block, block) + b * block
            go(pltpu.make_async_copy(ybuf.at[q % 2, pl.ds(j * tile + b * block, n * block)],
                                     y_hbm.at[pl.ds(dst, n * block)], wsem.at[q % 2]), start)

        for j in range(FFN_GROUP):
            item = gs_ref[q] + j
            n = jnp.where(item < gs_ref[q + 1], tn_ref[item], 0)
            pl.when(n >= most)(functools.partial(blocks, j, 0, most))
            pl.when(n > most)(functools.partial(blocks, j, most, 1))

            @pl.when((n > 0) & (n < most))
            def _():
                for b in range(most - 1):
                    pl.when(b < n)(functools.partial(blocks, j, b, 1))

    @pl.when(e == 0)
    def _():
        xbuf[...] = jnp.zeros_like(xbuf)
        fetch(0, True)

    w13_s[:, 0:D_EXPERT] = w1_ref[0].astype(BF16)
    w13_s[:, D_EXPERT:] = w3_ref[0].astype(BF16)
    w2_s[...] = w2_ref[0].astype(BF16)

    def group_body(q, carry):
        @pl.when(q + 1 < n_groups)
        def _():
            fetch(q + 1, True)

        @pl.when(q >= 2)
        def _():
            writeback(q - 2, False)

        fetch(q, False)
        x = _unpack_rows(xbuf.at[q % 2], FFN_GROUP * FFN_TILE).astype(BF16)
        h = _dot(x, w13_s[...])
        act = _silu(h[:, :D_EXPERT]) * h[:, D_EXPERT:]
        y = _dot(act.astype(BF16), w2_s[...])
        out = ybuf.at[q % 2]
        for j in range(ROW_SUB):
            out[pl.ds(j, FFN_GROUP * FFN_TILE, stride=ROW_SUB), :] = _pack_pair(
                y[:, j * LANES:(j + 1) * LANES], y[:, ROW_WORDS + j * LANES:ROW_WORDS + (j + 1) * LANES])
        writeback(q, True)
        return carry

    lax.fori_loop(gq_ref[e], gq_ref[e + 1], group_body, 0)

    @pl.when(e == pl.num_programs(0) - 1)
    def _():
        for d in range(2):
            @pl.when(n_groups - 1 - d >= 0)
            def _():
                writeback(n_groups - 1 - d, False)


def _expert_ffn(group_of_expert, group_start, item_off, item_blocks, x_slots, w1, w3, w2):
    of_expert = lambda e, *_: (e, 0, 0)
    tiles = pltpu.VMEM((2, FFN_GROUP * FFN_TILE * ROW_SUB, LANES), jnp.uint32)
    sems = pltpu.SemaphoreType.DMA((2,))
    grid_spec = pltpu.PrefetchScalarGridSpec(
        num_scalar_prefetch=4,
        grid=(N_EXPERTS,),
        in_specs=[pl.BlockSpec(memory_space=pl.ANY),
                  pl.BlockSpec((1, D_MODEL, D_EXPERT), of_expert),
                  pl.BlockSpec((1, D_MODEL, D_EXPERT), of_expert),
                  pl.BlockSpec((1, D_EXPERT, D_MODEL), of_expert)],
        out_specs=pl.BlockSpec(memory_space=pl.ANY),
        scratch_shapes=[pltpu.VMEM((D_MODEL, 2 * D_EXPERT), BF16), pltpu.VMEM((D_EXPERT, D_MODEL), BF16),
                        tiles, tiles, sems, sems])
    return pl.pallas_call(
        _ffn_kernel,
        out_shape=jax.ShapeDtypeStruct(x_slots.shape, jnp.uint32),
        grid_spec=grid_spec,
        input_output_aliases={4: 0},
        compiler_params=pltpu.CompilerParams(dimension_semantics=("arbitrary",), vmem_limit_bytes=VMEM_LIMIT),
        name="expert_ffn",
    )(group_of_expert, group_start, item_off, item_blocks, x_slots, w1, w3, w2)


def _work_items(cnt, loff):
    nw = cnt.shape[0]
    cnt = cnt.astype(jnp.int32).T.reshape(-1)
    loff = loff.astype(jnp.int32).T.reshape(-1)
    region = jnp.tile(jnp.arange(nw, dtype=jnp.int32) * REGION_ROWS, N_EXPERTS)
    n_items = nw * N_EXPERTS + (nw * PLAN_WINDOW * TOP_K) // FFN_TILE
    tiles = (cnt + FFN_TILE - 1) // FFN_TILE
    ends = jnp.cumsum(tiles)
    starts = ends - tiles
    i = jnp.arange(n_items + FFN_GROUP, dtype=jnp.int32)[:, None]
    mine = (starts[None, :] <= i) & (i < ends[None, :])
    of_pair = lambda v: jnp.sum(jnp.where(mine, v[None, :], 0), axis=1)
    r = i[:, 0] - of_pair(starts)
    row = of_pair(region + loff) + r * FFN_TILE
    own = jnp.clip(of_pair(cnt) - r * FFN_TILE, 0, FFN_TILE)

    zero = jnp.zeros((1,), jnp.int32)
    item_end = ends.reshape(N_EXPERTS, nw)[:, -1]
    item_start = jnp.concatenate([zero, item_end[:-1]])
    group_end = jnp.cumsum((item_end - item_start + FFN_GROUP - 1) // FFN_GROUP)
    group_first = jnp.concatenate([zero, group_end[:-1]])
    q = jnp.arange(N_EXPERTS + n_items // FFN_GROUP + 1, dtype=jnp.int32)[:, None]
    has = (group_first[None, :] <= q) & (q < group_end[None, :])
    of_expert = lambda v: jnp.sum(jnp.where(has, v[None, :], 0), axis=1)
    group_start = jnp.where(q[:, 0] < group_end[-1],
                            of_expert(item_start) + (q[:, 0] - of_expert(group_first)) * FFN_GROUP, item_end[-1])
    return (jnp.concatenate([zero, group_end]), group_start, row // RUN_ALIGN, (own + RUN_ALIGN - 1) // RUN_ALIGN)


def _combine_kernel(dest_ref, wsel_ref, y_ref, h_ref, x1_ref, gt2_ref, gpost_ref, ws13_ref, ws2_ref, out_ref,
                    dest_s, wsel_s, lo_s, hi_s, sem):
    s = pl.program_id(1)

    @pl.when(s == 0)
    def _():
        copies = (pltpu.make_async_copy(dest_ref.at[0, 0], dest_s, sem.at[0]),
                  pltpu.make_async_copy(wsel_ref.at[0, 0], wsel_s, sem.at[1]))
        for c in copies:
            c.start()
        for c in copies:
            c.wait()

    base = s * COMBINE_TILE

    def body(i, carry):
        for u in range(SCATTER_UNROLL):
            t = i * SCATTER_UNROLL + u
            acc_lo = jnp.zeros((ROW_SUB, LANES), F32)
            acc_hi = jnp.zeros((ROW_SUB, LANES), F32)
            for k in range(TOP_K):
                at = k * PLAN_WINDOW + base + t
                lo, hi = _unpack_pair(y_ref[_rows_at(dest_s[at]), :])
                w = wsel_s[at]
                acc_lo = acc_lo + w * lo
                acc_hi = acc_hi + w * hi
            lo_s[_slot_rows(t), :] = acc_lo
            hi_s[_slot_rows(t), :] = acc_hi
        return carry

    lax.fori_loop(0, COMBINE_TILE // SCATTER_UNROLL, body, 0)
    rows = lambda ref: [ref[pl.ds(j, COMBINE_TILE, stride=ROW_SUB), :] for j in range(ROW_SUB)]
    routed = jnp.concatenate(rows(lo_s) + rows(hi_s), axis=1)
    h_lo, h_hi = _unpack_pair(h_ref[...])
    x = jnp.concatenate([h_lo, h_hi], axis=1).astype(BF16)
    hs = _dot(x, ws13_ref[...])
    act = _silu(hs[:, :D_EXPERT]) * hs[:, D_EXPERT:]
    ff = routed + _dot(act.astype(BF16), ws2_ref[...])
    out_ref[...] = x1_ref[...] + gt2_ref[0] * _rms(ff, gpost_ref[...])


def _combine(dest_w, wsel_w, y_slots, h2, x1, gt2, tokens_per_gt2, gpost, ws13, ws2):
    nw = dest_w.shape[0]
    sub = PLAN_WINDOW // COMBINE_TILE
    plan = pl.BlockSpec((1, 1, TOP_K * PLAN_WINDOW), lambda w, s: (w, 0, 0))
    tok = lambda w, s: (w * sub + s, 0)
    return pl.pallas_call(
        _combine_kernel,
        out_shape=jax.ShapeDtypeStruct(x1.shape, F32),
        grid=(nw, sub),
        in_specs=[plan, plan,
                  pl.BlockSpec((REGION_ROWS * ROW_SUB, LANES), lambda w, s: (w, 0)),
                  pl.BlockSpec((COMBINE_TILE, ROW_WORDS), tok),
                  pl.BlockSpec((COMBINE_TILE, D_MODEL), tok),
                  pl.BlockSpec((1, 1, D_MODEL), lambda w, s: ((w * PLAN_WINDOW) // tokens_per_gt2, 0, 0)),
                  _full((1, D_MODEL)), _full(ws13.shape), _full(ws2.shape)],
        out_specs=pl.BlockSpec((COMBINE_TILE, D_MODEL), tok),
        scratch_shapes=[pltpu.SMEM((TOP_K * PLAN_WINDOW,), jnp.int32),
                        pltpu.SMEM((TOP_K * PLAN_WINDOW,), F32),
                        pltpu.VMEM((COMBINE_TILE * ROW_SUB, LANES), F32),
                        pltpu.VMEM((COMBINE_TILE * ROW_SUB, LANES), F32),
                        pltpu.SemaphoreType.DMA((2,))],
        compiler_params=pltpu.CompilerParams(dimension_semantics=("arbitrary", "arbitrary"),
                                             vmem_limit_bytes=VMEM_LIMIT),
        name="combine",
    )(dest_w, wsel_w, y_slots, h2, x1, gt2, gpost, ws13, ws2)


def _moe_kernel(h_ref, wd_ref, x1_ref, gt2_ref, gpost_ref, w1_ref, w3_ref, w2_ref, ws1_ref, ws3_ref, ws2_ref,
                out_ref, acc):
    e = pl.program_id(1)
    x = h_ref[...]

    @pl.when(e == 0)
    def _():
        hs = _silu(_dot(x, ws1_ref[...].astype(BF16))) * _dot(x, ws3_ref[...].astype(BF16))
        acc[...] = _dot(hs.astype(BF16), ws2_ref[...].astype(BF16))

    lane = lax.broadcasted_iota(jnp.int32, wd_ref.shape, 1)
    wcol = jnp.sum(jnp.where(lane == e, wd_ref[...], 0.0), axis=1, keepdims=True)
    act = _silu(_dot(x, w1_ref[0].astype(BF16))) * _dot(x, w3_ref[0].astype(BF16)) * wcol
    acc[...] += _dot(act.astype(BF16), w2_ref[0].astype(BF16))

    @pl.when(e == pl.num_programs(1) - 1)
    def _():
        out_ref[...] = x1_ref[...] + gt2_ref[0] * _rms(acc[...], gpost_ref[...])


def _moe(h2, wd, x1, gt2, rows_per_gt2, gpost, w1, w3, w2, ws1, ws3, ws2):
    t = h2.shape[0]
    tile = min(MOE_TILE, t)
    tok = lambda i, e: (i, 0)
    exp = lambda i, e: (e, 0, 0)
    return pl.pallas_call(
        _moe_kernel,
        out_shape=jax.ShapeDtypeStruct((t, D_MODEL), F32),
        grid=(t // tile, N_EXPERTS),
        in_specs=[pl.BlockSpec((tile, D_MODEL), tok),
                  pl.BlockSpec((tile, LANES), tok),
                  pl.BlockSpec((tile, D_MODEL), tok),
                  pl.BlockSpec((1,) + gt2.shape[1:], lambda i, e: (i // rows_per_gt2, 0, 0)),
                  _full((1, D_MODEL)),
                  pl.BlockSpec((1, D_MODEL, D_EXPERT), exp),
                  pl.BlockSpec((1, D_MODEL, D_EXPERT), exp),
                  pl.BlockSpec((1, D_EXPERT, D_MODEL), exp),
                  _full(ws1.shape), _full(ws3.shape), _full(ws2.shape)],
        out_specs=pl.BlockSpec((tile, D_MODEL), tok),
        scratch_shapes=[pltpu.VMEM((tile, D_MODEL), F32)],
        compiler_params=pltpu.CompilerParams(dimension_semantics=("parallel", "arbitrary"),
                                             vmem_limit_bytes=VMEM_LIMIT),
        name="moe",
    )(h2, wd, x1, gt2, gpost, w1, w3, w2, ws1, ws3, ws2)


def kernel(x_prompt, x_sample, c_prompt, c_sample, state_pool, state_hgrn, w_ada, b_ada, g_pre_mix, g_post_mix,
           w_in, w_pool, pool_scale, lb_logits, g_out_norm, w_out, g_pre_ffn, g_post_ffn, w_router, router_bias,
           w_exp_gate, w_exp_up, w_exp_down, w_sh_gate, w_sh_up, w_sh_down):
    assert w_ada.shape[0] == 1 and lb_logits.shape[0] == 2, "single-layer trunk"
    bp, lp, d = x_prompt.shape
    bs = x_sample.shape[0]
    row = lambda a: a[0].reshape(1, -1)

    n_mod = bp + bs
    pad = (-n_mod) % 16
    c_all = jnp.concatenate([c_prompt, c_sample, jnp.zeros((pad, d), F32)], axis=0)
    mod = _ada(c_all, w_ada[0], b_ada)
    mod_p = mod[:bp].reshape(bp, 1, 6 * d)
    mod_s = mod[bp:n_mod]

    win = w_in[0].astype(BF16)
    wout = w_out[0].astype(BF16)
    wpool = w_pool[0].astype(BF16)
    wr = jnp.pad(w_router[0], ((0, 0), (0, LANES - N_EXPERTS)))
    wrh = wr.astype(BF16)
    wrl = (wr - wrh.astype(F32)).astype(BF16)
    m2 = jnp.asarray(_M2_NP, BF16)
    gpre, gpost, gffn, gffn_post = row(g_pre_mix), row(g_post_mix), row(g_pre_ffn), row(g_post_ffn)
    pscale, gout = row(pool_scale), row(g_out_norm)
    bias_col = jnp.pad(router_bias[0], (0, LANES - N_EXPERTS)).reshape(LANES, 1)

    x1_p, h2_p, lg_p, pool_p, st_p = _mix_prompt(x_prompt, mod_p, gpre, gpost, gffn, win, wpool, pscale,
                                                  lb_logits, gout, wout, wrh, wrl, m2)
    x1_s, h2_s, lg_s, pool_s, st_s = _mix_sample(x_sample[:, 0, :], mod_s, state_pool[0], state_hgrn[0],
                                                  gpre, gpost, gffn, win, wpool, pscale, lb_logits, gout, wout,
                                                  wrh, wrl)

    experts = (w_exp_gate[0], w_exp_up[0], w_exp_down[0], w_sh_gate[0], w_sh_up[0], w_sh_down[0])
    tp = bp * lp
    nw = tp // PLAN_WINDOW
    assert lp % PLAN_WINDOW == 0
    upper = jnp.asarray(np.triu(np.ones((PLAN_WINDOW, PLAN_WINDOW), np.float32), 1), BF16)
    dest, wsel, cnt, loff = _plan(lg_p.reshape(tp, LANES), bias_col, upper)
    per_window = lambda a: a.reshape(TOP_K, nw, PLAN_WINDOW).transpose(1, 0, 2).reshape(nw, 1, TOP_K * PLAN_WINDOW)
    dest_w, wsel_w = per_window(dest), per_window(wsel)
    h2_rows = h2_p.reshape(tp, ROW_WORDS)
    x_slots = _dispatch(dest_w, h2_rows)
    y_slots = _expert_ffn(*_work_items(cnt[:, :, 0], loff[:, :, 0]), x_slots, *experts[:3])
    ws13 = jnp.concatenate([experts[3], experts[4]], axis=1).astype(BF16)
    gt2_p = mod_p[:, :, 5 * d:]
    y_p = _combine(dest_w, wsel_w, y_slots, h2_rows, x1_p.reshape(tp, d), gt2_p, lp, gffn_post,
                   ws13, experts[5].astype(BF16))
    wd_s = _router(lg_s, bias_col)
    gt2_s = mod_s[:, 5 * d:].reshape(1, bs, d)
    y_s = _moe(h2_s, wd_s, x1_s, gt2_s, 1, gffn_post, *experts)

    return (y_p.reshape(bp, lp, d), y_s.reshape(bs, 1, d), pool_p[None], st_p[None], pool_s[None], st_s[None])
```

```python
import functools

import numpy as np
import jax
import jax.numpy as jnp
from jax import lax
from jax.experimental import pallas as pl
from jax.experimental.pallas import tpu as pltpu

F32 = jnp.float32
BF16 = jnp.bfloat16

D_MODEL = 1024
POOL_WIDTH = 512
POOL_WINDOWS = (2, 4, 8, 16)
POOL_GROUP = 128
POOL_HIST = 15
HG_WIDTH = 512
HG_HEADS = 4
HG_HEAD_DIM = 128
IN_WIDTH = POOL_WIDTH + 4 * HG_WIDTH
N_EXPERTS = 64
TOP_K = 8
N_GROUPS = 8
TOPK_GROUPS = 4
GROUP_SIZE = N_EXPERTS // N_GROUPS
D_EXPERT = 256
ROUTED_SCALE = 2.5
EPS = 1e-6

LANES = 128
CHUNK = 64
TIME_TILE = 256
ADA_TILE_N = 512
ROUTER_TILE = 512
MOE_TILE = 1024
SAMPLE_STATE_BLOCK = 8
PLAN_WINDOW = 1024
RUN_ALIGN = 32
FFN_TILE = 160
REGION_ROWS = PLAN_WINDOW * TOP_K + N_EXPERTS * (RUN_ALIGN - 1) + FFN_TILE
ROW_WORDS = D_MODEL // 2
ROW_SUB = ROW_WORDS // LANES
COMBINE_TILE = 256
SCATTER_UNROLL = 4
FFN_GROUP = 4
FFN_RING = 4
VMEM_LIMIT = 58 * 1024 * 1024

_LEVELS = (64, 32, 16, 8, 4, 2)
_BLK_CUM = 0
_BLK_END = 1


def _level_blocks():
    c = CHUNK
    i = np.arange(c)[:, None]
    s = np.arange(c)[None, :]
    blocks = [(s <= i), (s > i)]
    index = {}
    for lvl in _LEVELS:
        seg = (i // lvl) * lvl
        mid = seg + lvl // 2
        ref = mid - 1
        index[lvl] = len(blocks)
        blocks.append((i >= mid) & (s > ref) & (s <= i))
        if lvl > 2:
            blocks.append((i < mid) & (s > i) & (s <= ref))
    m = np.concatenate(blocks, axis=0).astype(np.float32)
    return np.concatenate([m, m], axis=1), index


_M2_NP, _LEVEL_INDEX = _level_blocks()


def _dot(a, b):
    return jnp.dot(a, b, preferred_element_type=F32)


def _dot_nt(a, b):
    return lax.dot_general(a, b, (((1,), (1,)), ((), ())), preferred_element_type=F32)


def _split2(x):
    hi = x.astype(BF16)
    lo = (x - hi.astype(F32)).astype(BF16)
    return hi, lo


def _pack_pair(a, b):
    lo = lax.bitcast_convert_type(a.astype(BF16).astype(F32), jnp.uint32)
    hi = lax.bitcast_convert_type(b.astype(BF16).astype(F32), jnp.uint32)
    return (lo >> 16) | hi


def _unpack_pair(words):
    lo = lax.bitcast_convert_type(words << 16, F32)
    hi = lax.bitcast_convert_type(words & jnp.uint32(0xFFFF0000), F32)
    return lo, hi


def _silu(x):
    return x * jax.nn.sigmoid(x)


def _rms(x, g):
    return x * lax.rsqrt(jnp.mean(x * x, axis=-1, keepdims=True) + EPS) * g


def _mods(m):
    return [m[:, j * D_MODEL:(j + 1) * D_MODEL] for j in range(6)]


def _forget_lower_bound(lbl):
    mx = jnp.max(lbl, axis=0, keepdims=True)
    e = jnp.exp(lbl - mx)
    return e[0:1] / jnp.sum(e, axis=0, keepdims=True)


def _router_logits(h2, wrh, wrl):
    hi, lo = _split2(h2)
    return _dot(hi, wrh) + _dot(lo, wrh) + _dot(hi, wrl)


def _ada_kernel(c_ref, w_ref, b_ref, o_ref):
    a_hi, a_lo = _split2(_silu(c_ref[...]))
    w_hi, w_lo = _split2(w_ref[...])
    o_ref[...] = _dot(a_hi, w_hi) + _dot(a_lo, w_hi) + _dot(a_hi, w_lo) + b_ref[...]


def _ada(c_all, w_ada, b_ada):
    rows = c_all.shape[0]
    n = w_ada.shape[1]
    return pl.pallas_call(
        _ada_kernel,
        out_shape=jax.ShapeDtypeStruct((rows, n), F32),
        grid=(n // ADA_TILE_N,),
        in_specs=[pl.BlockSpec((rows, D_MODEL), lambda j: (0, 0)),
                  pl.BlockSpec((D_MODEL, ADA_TILE_N), lambda j: (0, j)),
                  pl.BlockSpec((1, ADA_TILE_N), lambda j: (0, j))],
        out_specs=pl.BlockSpec((rows, ADA_TILE_N), lambda j: (0, j)),
        compiler_params=pltpu.CompilerParams(dimension_semantics=("parallel",),
                                             vmem_limit_bytes=VMEM_LIMIT),
        name="ada",
    )(c_all, w_ada, b_ada)


def _level_masks():
    i = lax.broadcasted_iota(jnp.int32, (CHUNK, CHUNK), 0)
    j = lax.broadcasted_iota(jnp.int32, (CHUNK, CHUNK), 1)
    masks = {}
    for lvl in _LEVELS:
        sh = lvl.bit_length() - 1
        same = (i >> sh) == (j >> sh)
        upper = ((i >> (sh - 1)) & 1) == 1
        lower = ((j >> (sh - 1)) & 1) == 0
        masks[lvl] = same & upper & lower
    return masks, i == j


def _hgrn_chunk_head(q, k, v, e_all, lane0, st, masks, eye):
    def blk(n):
        return e_all[n * CHUNK:(n + 1) * CHUNK, lane0:lane0 + HG_HEAD_DIM]

    b = blk(_BLK_CUM)
    a = jnp.where(eye, _dot_nt(q.astype(BF16), k.astype(BF16)), 0.0)
    for lvl in _LEVELS:
        n = _LEVEL_INDEX[lvl]
        ql = (q * jnp.exp(blk(n))).astype(BF16)
        kl = (k * jnp.exp(blk(n + 1))).astype(BF16) if lvl > 2 else k.astype(BF16)
        a = a + jnp.where(masks[lvl], _dot_nt(ql, kl), 0.0)
    st_b = st.astype(BF16)
    o = _dot(a.astype(BF16), v.astype(BF16)) + _dot_nt((q * jnp.exp(b)).astype(BF16), st_b)
    k_end = (k * jnp.exp(blk(_BLK_END))).astype(BF16)
    decay = jnp.exp(b[CHUNK - 1:CHUNK, :])
    st_new = st * decay + _dot(v.T.astype(BF16), k_end)
    return o, st_new


def _mix_prompt_kernel(x_ref, mod_ref, gpre_ref, gpost_ref, gffn_ref, win_ref, wpool_ref, pscale_ref,
                       lbl_ref, gout_ref, wout_ref, wrh_ref, wrl_ref, m2_ref,
                       x1_ref, h2_ref, lg_ref, pool_ref, st_ref,
                       st_s, ubuf, q_s, k_s, v_s, g_s, o_s):
    t = pl.program_id(1)
    n_t = pl.num_programs(1)
    tt = TIME_TILE

    @pl.when(t == 0)
    def _():
        st_s[...] = jnp.zeros_like(st_s)
        ubuf[0:16, :] = jnp.zeros((16, POOL_WIDTH), F32)

    xt = x_ref[0]
    sh1, sc1, gt1, sh2, sc2, gt2 = _mods(mod_ref[0])
    h = _rms(xt, gpre_ref[...]) * (1.0 + sc1) + sh1
    proj = _dot(h.astype(BF16), win_ref[...])

    u = proj[:, :POOL_WIDTH]
    ubuf[16:16 + tt, :] = u
    pos = (t * tt + lax.broadcasted_iota(jnp.int32, (tt, 1), 0) + 1).astype(F32)
    ys = []
    for g, w in enumerate(POOL_WINDOWS):
        s = ubuf[:, g * POOL_GROUP:(g + 1) * POOL_GROUP]
        off = 0
        for step in range(w.bit_length() - 1):
            sh = 1 << step
            s = s[sh:, :] + s[:-sh, :]
            off += sh
        ws = s[16 - off:16 - off + tt, :]
        cnt = jnp.minimum(pos, float(w))
        d = ws / cnt - u[:, g * POOL_GROUP:(g + 1) * POOL_GROUP]
        ys.append(_dot(d.astype(BF16), wpool_ref[g]))
    y_pool = jnp.concatenate(ys, axis=1) * pscale_ref[...]
    ubuf[0:16, :] = ubuf[tt:tt + 16, :]

    @pl.when(t == n_t - 1)
    def _():
        pool_ref[0] = u[tt - POOL_HIST:, :]

    lb = _forget_lower_bound(lbl_ref[...])
    f = lb + (1.0 - lb) * jax.nn.sigmoid(proj[:, POOL_WIDTH + HG_WIDTH:POOL_WIDTH + 2 * HG_WIDTH])
    q_s[...] = _silu(proj[:, POOL_WIDTH:POOL_WIDTH + HG_WIDTH])
    k_s[...] = 1.0 - f
    v_s[...] = proj[:, POOL_WIDTH + 2 * HG_WIDTH:POOL_WIDTH + 3 * HG_WIDTH]
    g_s[...] = jnp.log(f)
    gate = _silu(proj[:, POOL_WIDTH + 3 * HG_WIDTH:])

    masks, eye = _level_masks()
    m2 = m2_ref[...]

    states = [st_s[hd] for hd in range(HG_HEADS)]
    for c in range(tt // CHUNK):
        rows = slice(c * CHUNK, (c + 1) * CHUNK)
        g_hi, g_lo = _split2(g_s[rows, :])
        e_all = _dot(m2, jnp.concatenate([g_hi, g_lo], axis=0))
        for hd in range(HG_HEADS):
            lane0 = hd * HG_HEAD_DIM
            lanes = slice(lane0, lane0 + HG_HEAD_DIM)
            o, states[hd] = _hgrn_chunk_head(q_s[rows, lanes], k_s[rows, lanes], v_s[rows, lanes], e_all, lane0,
                                             states[hd], masks, eye)
            o_s[rows, lanes] = o
    for hd in range(HG_HEADS):
        st_s[hd] = states[hd]

    o = o_s[...]
    os_ = []
    for hd in range(HG_HEADS):
        oh = o[:, hd * HG_HEAD_DIM:(hd + 1) * HG_HEAD_DIM]
        os_.append(oh * lax.rsqrt(jnp.mean(oh * oh, axis=-1, keepdims=True) + EPS) * gout_ref[...])
    o_n = jnp.concatenate(os_, axis=1) * gate

    mix = _dot(y_pool.astype(BF16), wout_ref[0:POOL_WIDTH, :]) + _dot(o_n.astype(BF16), wout_ref[POOL_WIDTH:, :])
    x1 = xt + gt1 * _rms(mix, gpost_ref[...])
    h2 = _rms(x1, gffn_ref[...]) * (1.0 + sc2) + sh2
    x1_ref[0] = x1
    h2_ref[0] = _pack_pair(h2[:, :ROW_WORDS], h2[:, ROW_WORDS:])
    lg_ref[0] = _router_logits(h2, wrh_ref[...], wrl_ref[...])

    @pl.when(t == n_t - 1)
    def _():
        for hd in range(HG_HEADS):
            st_ref[0, hd] = st_s[hd].T


def _full(shape):
    nd = len(shape)
    return pl.BlockSpec(shape, lambda *_: (0,) * nd)


def _mix_prompt(x, mod, gpre, gpost, gffn, win, wpool, pscale, lbl, gout, wout, wrh, wrl, m2):
    b, l, d = x.shape
    tt = TIME_TILE
    tile = lambda i, j: (i, j, 0)
    per_b = lambda i, j: (i, 0, 0)
    return pl.pallas_call(
        _mix_prompt_kernel,
        out_shape=(jax.ShapeDtypeStruct((b, l, d), F32),
                   jax.ShapeDtypeStruct((b, l, ROW_WORDS), jnp.uint32),
                   jax.ShapeDtypeStruct((b, l, LANES), F32),
                   jax.ShapeDtypeStruct((b, POOL_HIST, POOL_WIDTH), F32),
                   jax.ShapeDtypeStruct((b, HG_HEADS, HG_HEAD_DIM, HG_HEAD_DIM), F32)),
        grid=(b, l // tt),
        in_specs=[pl.BlockSpec((1, tt, d), tile),
                  pl.BlockSpec((1, 1, 6 * d), per_b),
                  _full((1, d)), _full((1, d)), _full((1, d)),
                  _full(win.shape), _full(wpool.shape), _full((1, POOL_WIDTH)),
                  _full(lbl.shape), _full((1, HG_HEAD_DIM)), _full(wout.shape),
                  _full(wrh.shape), _full(wrl.shape), _full(m2.shape)],
        out_specs=(pl.BlockSpec((1, tt, d), tile),
                   pl.BlockSpec((1, tt, ROW_WORDS), tile),
                   pl.BlockSpec((1, tt, LANES), tile),
                   pl.BlockSpec((1, POOL_HIST, POOL_WIDTH), per_b),
                   pl.BlockSpec((1, HG_HEADS, HG_HEAD_DIM, HG_HEAD_DIM), lambda i, j: (i, 0, 0, 0))),
        scratch_shapes=[pltpu.VMEM((HG_HEADS, HG_HEAD_DIM, HG_HEAD_DIM), F32),
                        pltpu.VMEM((tt + 16, POOL_WIDTH), F32),
                        pltpu.VMEM((tt, HG_WIDTH), F32), pltpu.VMEM((tt, HG_WIDTH), F32),
                        pltpu.VMEM((tt, HG_WIDTH), F32), pltpu.VMEM((tt, HG_WIDTH), F32),
                        pltpu.VMEM((tt, HG_WIDTH), F32)],
        compiler_params=pltpu.CompilerParams(dimension_semantics=("parallel", "arbitrary"),
                                             vmem_limit_bytes=VMEM_LIMIT),
        name="mix_prompt",
    )(x, mod, gpre, gpost, gffn, win, wpool, pscale, lbl, gout, wout, wrh, wrl, m2)


def _mix_sample_in_kernel(x_ref, mod_ref, gpre_ref, win_ref, wpool_ref, pscale_ref, lbl_ref, hist_ref,
                          ypool_ref, npool_ref, ft_ref, qt_ref, v_ref, gate_ref):
    xt = x_ref[...]
    sh1, sc1 = _mods(mod_ref[...])[:2]
    h = _rms(xt, gpre_ref[...]) * (1.0 + sc1) + sh1
    proj = _dot(h.astype(BF16), win_ref[...])
    u = proj[:, :POOL_WIDTH]
    row = lax.broadcasted_iota(jnp.int32, (hist_ref.shape[0], POOL_HIST, POOL_GROUP), 1)
    ys = []
    for g, w in enumerate(POOL_WINDOWS):
        sl = slice(g * POOL_GROUP, (g + 1) * POOL_GROUP)
        past = jnp.sum(jnp.where(row >= POOL_HIST - (w - 1), hist_ref[:, :, sl], 0.0), axis=1)
        ug = u[:, sl]
        d = (past + ug) / float(w) - ug
        ys.append(_dot(d.astype(BF16), wpool_ref[g]))
    ypool_ref[...] = jnp.concatenate(ys, axis=1) * pscale_ref[...]
    npool_ref[:, 0:POOL_HIST - 1, :] = hist_ref[:, 1:POOL_HIST, :]
    npool_ref[:, POOL_HIST - 1, :] = u

    lb = _forget_lower_bound(lbl_ref[...])
    f = lb + (1.0 - lb) * jax.nn.sigmoid(proj[:, POOL_WIDTH + HG_WIDTH:POOL_WIDTH + 2 * HG_WIDTH])
    ft_ref[...] = f.T
    qt_ref[...] = _silu(proj[:, POOL_WIDTH:POOL_WIDTH + HG_WIDTH]).T
    v_ref[...] = proj[:, POOL_WIDTH + 2 * HG_WIDTH:POOL_WIDTH + 3 * HG_WIDTH]
    gate_ref[...] = _silu(proj[:, POOL_WIDTH + 3 * HG_WIDTH:])


def _mix_sample_state_kernel(s_ref, ft_ref, qt_ref, v_ref, snew_ref, o_ref):
    i = pl.program_id(0)
    lane = lax.broadcasted_iota(jnp.int32, (HG_HEAD_DIM, ft_ref.shape[1]), 1)
    for j in range(SAMPLE_STATE_BLOCK):
        mine = lane == i * SAMPLE_STATE_BLOCK + j
        for hd in range(HG_HEADS):
            r0 = hd * HG_HEAD_DIM
            f = jnp.sum(jnp.where(mine, ft_ref[r0:r0 + HG_HEAD_DIM, :], 0.0), axis=1, keepdims=True)
            q = jnp.sum(jnp.where(mine, qt_ref[r0:r0 + HG_HEAD_DIM, :], 0.0), axis=1, keepdims=True)
            v = v_ref[j:j + 1, r0:r0 + HG_HEAD_DIM]
            s_new = f * s_ref[j, hd] + (1.0 - f) * v
            snew_ref[j, hd] = s_new
            o_ref[j:j + 1, r0:r0 + HG_HEAD_DIM] = jnp.sum(q * s_new, axis=0, keepdims=True)


def _mix_sample_out_kernel(x_ref, mod_ref, o_ref, gate_ref, ypool_ref, gout_ref, wout_ref, gpost_ref,
                           gffn_ref, wrh_ref, wrl_ref, x1_ref, h2_ref, lg_ref):
    _, _, gt1, sh2, sc2, _ = _mods(mod_ref[...])
    o = o_ref[...]
    os_ = []
    for hd in range(HG_HEADS):
        oh = o[:, hd * HG_HEAD_DIM:(hd + 1) * HG_HEAD_DIM]
        os_.append(oh * lax.rsqrt(jnp.mean(oh * oh, axis=-1, keepdims=True) + EPS) * gout_ref[...])
    o_n = jnp.concatenate(os_, axis=1) * gate_ref[...]
    mix = (_dot(ypool_ref[...].astype(BF16), wout_ref[0:POOL_WIDTH, :])
           + _dot(o_n.astype(BF16), wout_ref[POOL_WIDTH:, :]))
    x1 = x_ref[...] + gt1 * _rms(mix, gpost_ref[...])
    h2 = _rms(x1, gffn_ref[...]) * (1.0 + sc2) + sh2
    x1_ref[...] = x1
    h2_ref[...] = h2.astype(BF16)
    lg_ref[...] = _router_logits(h2, wrh_ref[...], wrl_ref[...])


def _mix_sample(x, mod, hist, state, gpre, gpost, gffn, win, wpool, pscale, lbl, gout, wout, wrh, wrl):
    b = x.shape[0]
    bb = SAMPLE_STATE_BLOCK
    cp = pltpu.CompilerParams(vmem_limit_bytes=VMEM_LIMIT)
    ypool, npool, ft, qt, v, gate = pl.pallas_call(
        _mix_sample_in_kernel,
        out_shape=(jax.ShapeDtypeStruct((b, POOL_WIDTH), F32),
                   jax.ShapeDtypeStruct((b, POOL_HIST, POOL_WIDTH), F32),
                   jax.ShapeDtypeStruct((HG_WIDTH, b), F32),
                   jax.ShapeDtypeStruct((HG_WIDTH, b), F32),
                   jax.ShapeDtypeStruct((b, HG_WIDTH), F32),
                   jax.ShapeDtypeStruct((b, HG_WIDTH), F32)),
        compiler_params=cp,
        name="mix_sample_in",
    )(x, mod, gpre, win, wpool, pscale, lbl, hist)

    s_spec = pl.BlockSpec((bb, HG_HEADS, HG_HEAD_DIM, HG_HEAD_DIM), lambda i: (i, 0, 0, 0))
    col_spec = _full((HG_WIDTH, b))
    row_spec = pl.BlockSpec((bb, HG_WIDTH), lambda i: (i, 0))
    s_new, o = pl.pallas_call(
        _mix_sample_state_kernel,
        out_shape=(jax.ShapeDtypeStruct(state.shape, F32), jax.ShapeDtypeStruct((b, HG_WIDTH), F32)),
        grid=(b // bb,),
        in_specs=[s_spec, col_spec, col_spec, row_spec],
        out_specs=(s_spec, row_spec),
        compiler_params=pltpu.CompilerParams(dimension_semantics=("parallel",), vmem_limit_bytes=VMEM_LIMIT),
        name="mix_sample_state",
    )(state, ft, qt, v)

    x1, h2, lg = pl.pallas_call(
        _mix_sample_out_kernel,
        out_shape=(jax.ShapeDtypeStruct((b, D_MODEL), F32),
                   jax.ShapeDtypeStruct((b, D_MODEL), BF16),
                   jax.ShapeDtypeStruct((b, LANES), F32)),
        compiler_params=cp,
        name="mix_sample_out",
    )(x, mod, o, gate, ypool, gout, wout, gpost, gffn, wrh, wrl)
    return x1, h2, lg, npool, s_new


def _first_max(cur, idx, big):
    m = jnp.max(cur, axis=0, keepdims=True)
    first = jnp.min(jnp.where(cur == m, idx, big), axis=0, keepdims=True)
    return idx == first


def _route(logit, bias):
    n = logit.shape[1]
    scores = jax.nn.sigmoid(logit)
    sel = scores + bias
    neg = -jnp.inf

    sub = lax.broadcasted_iota(jnp.int32, (GROUP_SIZE, n), 0)
    gscore = []
    for g in range(N_GROUPS):
        sg = sel[g * GROUP_SIZE:(g + 1) * GROUP_SIZE, :]
        m1 = jnp.max(sg, axis=0, keepdims=True)
        rest = jnp.where(_first_max(sg, sub, GROUP_SIZE), neg, sg)
        gscore.append(m1 + jnp.max(rest, axis=0, keepdims=True))
    cur = jnp.concatenate(gscore, axis=0)
    gidx = lax.broadcasted_iota(jnp.int32, (N_GROUPS, n), 0)
    gmask = jnp.zeros((N_GROUPS, n), jnp.bool_)
    for _ in range(TOPK_GROUPS):
        pick = _first_max(cur, gidx, N_GROUPS)
        gmask = gmask | pick
        cur = jnp.where(pick, neg, cur)
    emask = jnp.concatenate(
        [jnp.broadcast_to(gmask[g:g + 1, :], (GROUP_SIZE, n)) for g in range(N_GROUPS)], axis=0)

    cur = jnp.where(emask, sel, neg)
    eidx = lax.broadcasted_iota(jnp.int32, (N_EXPERTS, n), 0)
    picks = []
    for _ in range(TOP_K):
        pick = _first_max(cur, eidx, N_EXPERTS)
        picks.append(pick)
        cur = jnp.where(pick, neg, cur)
    return scores, picks


def _any(masks):
    return functools.reduce(jnp.logical_or, masks)


def _router_kernel(lg_ref, bias_ref, wd_ref):
    n = lg_ref.shape[0]
    scores, picks = _route(lg_ref[...].T[0:N_EXPERTS, :], bias_ref[0:N_EXPERTS, :])
    chosen = _any(picks)
    wsum = jnp.sum(jnp.where(chosen, scores, 0.0), axis=0, keepdims=True)
    wd = jnp.where(chosen, scores / wsum * ROUTED_SCALE, 0.0)
    wd_ref[...] = jnp.concatenate([wd, jnp.zeros((LANES - N_EXPERTS, n), F32)], axis=0).T


def _router(logits, bias_col):
    t = logits.shape[0]
    tile = min(ROUTER_TILE, t)
    return pl.pallas_call(
        _router_kernel,
        out_shape=jax.ShapeDtypeStruct((t, LANES), F32),
        grid=(t // tile,),
        in_specs=[pl.BlockSpec((tile, LANES), lambda i: (i, 0)), _full(bias_col.shape)],
        out_specs=pl.BlockSpec((tile, LANES), lambda i: (i, 0)),
        compiler_params=pltpu.CompilerParams(dimension_semantics=("parallel",), vmem_limit_bytes=VMEM_LIMIT),
        name="router",
    )(logits, bias_col)


def _plan_kernel(lg_ref, bias_ref, upper_ref, dest_ref, wsel_ref, cnt_ref, loff_ref):
    scores, picks = _route(lg_ref[...].T[0:N_EXPERTS, :], bias_ref[0:N_EXPERTS, :])
    chosen = _any(picks)
    chosen_f = jnp.where(chosen, 1.0, 0.0)
    cnt = jnp.sum(chosen_f, axis=1, keepdims=True)
    units = jnp.floor((cnt + (RUN_ALIGN - 1)) / RUN_ALIGN)
    ei = lax.broadcasted_iota(jnp.int32, (N_EXPERTS, N_EXPERTS), 0)
    ej = lax.broadcasted_iota(jnp.int32, (N_EXPERTS, N_EXPERTS), 1)
    before = jnp.where(ej < ei, 1.0, 0.0).astype(BF16)
    loff = RUN_ALIGN * _dot(before, jnp.broadcast_to(units, (N_EXPERTS, LANES)).astype(BF16))[:, 0:1]
    slot = loff + _dot(chosen_f.astype(BF16), upper_ref[...])
    wsum = jnp.sum(jnp.where(chosen, scores, 0.0), axis=0, keepdims=True)
    pick_sum = lambda v: jnp.concatenate(
        [jnp.sum(jnp.where(p, v, 0.0), axis=0, keepdims=True) for p in picks], axis=0)
    dest_ref[...] = (pick_sum(slot) * ROW_SUB).astype(jnp.int32)
    wsel_ref[...] = pick_sum(scores) / wsum * ROUTED_SCALE
    cnt_ref[0] = jnp.broadcast_to(cnt, (N_EXPERTS, LANES))
    loff_ref[0] = jnp.broadcast_to(loff, (N_EXPERTS, LANES))


def _plan(logits, bias_col, upper):
    t = logits.shape[0]
    nw = t // PLAN_WINDOW
    per_w = pl.BlockSpec((1, N_EXPERTS, LANES), lambda w: (w, 0, 0))
    picks = pl.BlockSpec((TOP_K, PLAN_WINDOW), lambda w: (0, w))
    return pl.pallas_call(
        _plan_kernel,
        out_shape=(jax.ShapeDtypeStruct((TOP_K, t), jnp.int32), jax.ShapeDtypeStruct((TOP_K, t), F32),
                   jax.ShapeDtypeStruct((nw, N_EXPERTS, LANES), F32),
                   jax.ShapeDtypeStruct((nw, N_EXPERTS, LANES), F32)),
        grid=(nw,),
        in_specs=[pl.BlockSpec((PLAN_WINDOW, LANES), lambda w: (w, 0)), _full(bias_col.shape), _full(upper.shape)],
        out_specs=(picks, picks, per_w, per_w),
        compiler_params=pltpu.CompilerParams(dimension_semantics=("parallel",), vmem_limit_bytes=VMEM_LIMIT),
        name="plan",
    )(logits, bias_col, upper)


def _slot_rows(slot):
    return pl.ds(pl.multiple_of(slot * ROW_SUB, ROW_SUB), ROW_SUB)


def _rows_at(offset):
    return pl.ds(pl.multiple_of(offset, ROW_SUB), ROW_SUB)


def _dispatch_kernel(dest_ref, h_ref, x_ref, dest_s, rows_s, sem):
    plan_copy = pltpu.make_async_copy(dest_ref.at[0, 0], dest_s, sem)
    plan_copy.start()
    x_ref[...] = jnp.zeros_like(x_ref)
    for j in range(ROW_SUB):
        rows_s[pl.ds(j, PLAN_WINDOW, stride=ROW_SUB), :] = h_ref[:, j * LANES:(j + 1) * LANES]
    plan_copy.wait()

    def body(i, carry):
        for u in range(SCATTER_UNROLL):
            t = i * SCATTER_UNROLL + u
            row = rows_s[_slot_rows(t), :]
            for k in range(TOP_K):
                x_ref[_rows_at(dest_s[t * TOP_K + k]), :] = row
        return carry

    lax.fori_loop(0, PLAN_WINDOW // SCATTER_UNROLL, body, 0)


def _dispatch(dest_w, h2):
    nw = dest_w.shape[0]
    region = REGION_ROWS * ROW_SUB
    return pl.pallas_call(
        _dispatch_kernel,
        out_shape=jax.ShapeDtypeStruct((nw * region, LANES), jnp.uint32),
        grid=(nw,),
        in_specs=[pl.BlockSpec((1, 1, TOP_K * PLAN_WINDOW), lambda w: (w, 0, 0)),
                  pl.BlockSpec((PLAN_WINDOW, ROW_WORDS), lambda w: (w, 0))],
        out_specs=pl.BlockSpec((region, LANES), lambda w: (w, 0)),
        scratch_shapes=[pltpu.SMEM((TOP_K * PLAN_WINDOW,), jnp.int32),
                        pltpu.VMEM((PLAN_WINDOW * ROW_SUB, LANES), jnp.uint32),
                        pltpu.SemaphoreType.DMA],
        compiler_params=pltpu.CompilerParams(dimension_semantics=("arbitrary",), vmem_limit_bytes=VMEM_LIMIT),
        name="dispatch",
    )(dest_w, h2)


def _unpack_rows(ref, n):
    parts = [_unpack_pair(ref[pl.ds(j, n, stride=ROW_SUB), :]) for j in range(ROW_SUB)]
    return jnp.concatenate([p[0] for p in parts] + [p[1] for p in parts], axis=1)


def _ffn_kernel(gq_ref, gs_ref, to_ref, tn_ref, x_hbm, w1_ref, w3_ref, w2_ref, y_hbm,
                w13_s, w2_s, xbuf, ybuf, rsem, wsem):
    del x_hbm
    e = pl.program_id(0)
    n_groups = gq_ref[N_EXPERTS]
    block = RUN_ALIGN * ROW_SUB
    tile = FFN_TILE * ROW_SUB

    def go(cp, start):
        if start:
            cp.start()
        else:
            cp.wait()

    def fetch(q, start):
        def one(j):
            src = y_hbm.at[pl.ds(pl.multiple_of(to_ref[gs_ref[q] + j] * block, block), tile)]
            go(pltpu.make_async_copy(src, xbuf.at[q % FFN_RING, pl.ds(j * tile, tile)], rsem.at[q % FFN_RING]), start)

        size = gs_ref[q + 1] - gs_ref[q]

        @pl.when(size == FFN_GROUP)
        def _():
            for j in range(FFN_GROUP):
                one(j)

        @pl.when(size < FFN_GROUP)
        def _():
            for j in range(FFN_GROUP - 1):
                pl.when(j < size)(functools.partial(one, j))

    def writeback(q, start):
        most = FFN_TILE // RUN_ALIGN - 1

        def blocks(j, b, n):
            dst = pl.multiple_of(to_ref[gs_ref[q] + j] * block, block) + b * block
            go(pltpu.make_async_copy(ybuf.at[q % FFN_RING, pl.ds(j * tile + b * block, n * block)],
                                     y_hbm.at[pl.ds(dst, n * block)], wsem.at[q % FFN_RING]), start)

        for j in range(FFN_GROUP):
            item = gs_ref[q] + j
            n = jnp.where(item < gs_ref[q + 1], tn_ref[item], 0)
            pl.when(n >= most)(functools.partial(blocks, j, 0, most))
            pl.when(n > most)(functools.partial(blocks, j, most, 1))

            @pl.when((n > 0) & (n < most))
            def _():
                for b in range(most - 1):
                    pl.when(b < n)(functools.partial(blocks, j, b, 1))

    @pl.when(e == 0)
    def _():
        xbuf[...] = jnp.zeros_like(xbuf)
        for d in range(FFN_RING - 1):
            pl.when(d < n_groups)(functools.partial(fetch, d, True))

    w13_s[:, 0:D_EXPERT] = w1_ref[0].astype(BF16)
    w13_s[:, D_EXPERT:] = w3_ref[0].astype(BF16)
    w2_s[...] = w2_ref[0].astype(BF16)

    def group_body(q, carry):
        @pl.when(q + FFN_RING - 1 < n_groups)
        def _():
            fetch(q + FFN_RING - 1, True)

        @pl.when(q >= FFN_RING)
        def _():
            writeback(q - FFN_RING, False)

        fetch(q, False)
        x = _unpack_rows(xbuf.at[q % FFN_RING], FFN_GROUP * FFN_TILE).astype(BF16)
        h = _dot(x, w13_s[...])
        act = _silu(h[:, :D_EXPERT]) * h[:, D_EXPERT:]
        y = _dot(act.astype(BF16), w2_s[...])
        out = ybuf.at[q % FFN_RING]
        for j in range(ROW_SUB):
            out[pl.ds(j, FFN_GROUP * FFN_TILE, stride=ROW_SUB), :] = _pack_pair(
                y[:, j * LANES:(j + 1) * LANES], y[:, ROW_WORDS + j * LANES:ROW_WORDS + (j + 1) * LANES])
        writeback(q, True)
        return carry

    lax.fori_loop(gq_ref[e], gq_ref[e + 1], group_body, 0)

    @pl.when(e == pl.num_programs(0) - 1)
    def _():
        for d in range(FFN_RING):
            @pl.when(n_groups - 1 - d >= 0)
            def _():
                writeback(n_groups - 1 - d, False)


def _expert_ffn(group_of_expert, group_start, item_off, item_blocks, x_slots, w1, w3, w2):
    of_expert = lambda e, *_: (e, 0, 0)
    tiles = pltpu.VMEM((FFN_RING, FFN_GROUP * FFN_TILE * ROW_SUB, LANES), jnp.uint32)
    sems = pltpu.SemaphoreType.DMA((FFN_RING,))
    grid_spec = pltpu.PrefetchScalarGridSpec(
        num_scalar_prefetch=4,
        grid=(N_EXPERTS,),
        in_specs=[pl.BlockSpec(memory_space=pl.ANY),
                  pl.BlockSpec((1, D_MODEL, D_EXPERT), of_expert),
                  pl.BlockSpec((1, D_MODEL, D_EXPERT), of_expert),
                  pl.BlockSpec((1, D_EXPERT, D_MODEL), of_expert)],
        out_specs=pl.BlockSpec(memory_space=pl.ANY),
        scratch_shapes=[pltpu.VMEM((D_MODEL, 2 * D_EXPERT), BF16), pltpu.VMEM((D_EXPERT, D_MODEL), BF16),
                        tiles, tiles, sems, sems])
    return pl.pallas_call(
        _ffn_kernel,
        out_shape=jax.ShapeDtypeStruct(x_slots.shape, jnp.uint32),
        grid_spec=grid_spec,
        input_output_aliases={4: 0},
        compiler_params=pltpu.CompilerParams(dimension_semantics=("arbitrary",), vmem_limit_bytes=VMEM_LIMIT),
        name="expert_ffn",
    )(group_of_expert, group_start, item_off, item_blocks, x_slots, w1, w3, w2)


def _work_items(cnt, loff):
    nw = cnt.shape[0]
    cnt = cnt.astype(jnp.int32).T.reshape(-1)
    loff = loff.astype(jnp.int32).T.reshape(-1)
    region = jnp.tile(jnp.arange(nw, dtype=jnp.int32) * REGION_ROWS, N_EXPERTS)
    n_items = nw * N_EXPERTS + (nw * PLAN_WINDOW * TOP_K) // FFN_TILE
    tiles = (cnt + FFN_TILE - 1) // FFN_TILE
    ends = jnp.cumsum(tiles)
    starts = ends - tiles
    i = jnp.arange(n_items + FFN_GROUP, dtype=jnp.int32)[:, None]
    mine = (starts[None, :] <= i) & (i < ends[None, :])
    of_pair = lambda v: jnp.sum(jnp.where(mine, v[None, :], 0), axis=1)
    r = i[:, 0] - of_pair(starts)
    row = of_pair(region + loff) + r * FFN_TILE
    own = jnp.clip(of_pair(cnt) - r * FFN_TILE, 0, FFN_TILE)

    zero = jnp.zeros((1,), jnp.int32)
    item_end = ends.reshape(N_EXPERTS, nw)[:, -1]
    item_start = jnp.concatenate([zero, item_end[:-1]])
    group_end = jnp.cumsum((item_end - item_start + FFN_GROUP - 1) // FFN_GROUP)
    group_first = jnp.concatenate([zero, group_end[:-1]])
    q = jnp.arange(N_EXPERTS + n_items // FFN_GROUP + 1, dtype=jnp.int32)[:, None]
    has = (group_first[None, :] <= q) & (q < group_end[None, :])
    of_expert = lambda v: jnp.sum(jnp.where(has, v[None, :], 0), axis=1)
    group_start = jnp.where(q[:, 0] < group_end[-1],
                            of_expert(item_start) + (q[:, 0] - of_expert(group_first)) * FFN_GROUP, item_end[-1])
    return (jnp.concatenate([zero, group_end]), group_start, row // RUN_ALIGN, (own + RUN_ALIGN - 1) // RUN_ALIGN)


def _combine_kernel(dest_ref, wsel_ref, y_ref, h_ref, x1_ref, gt2_ref, gpost_ref, ws13_ref, ws2_ref, out_ref,
                    dest_s, wsel_s, lo_s, hi_s, sem):
    s = pl.program_id(1)

    @pl.when(s == 0)
    def _():
        copies = (pltpu.make_async_copy(dest_ref.at[0, 0], dest_s, sem.at[0]),
                  pltpu.make_async_copy(wsel_ref.at[0, 0], wsel_s, sem.at[1]))
        for c in copies:
            c.start()
        for c in copies:
            c.wait()

    base = s * COMBINE_TILE

    def body(i, carry):
        for u in range(SCATTER_UNROLL):
            t = i * SCATTER_UNROLL + u
            acc_lo = jnp.zeros((ROW_SUB, LANES), F32)
            acc_hi = jnp.zeros((ROW_SUB, LANES), F32)
            for k in range(TOP_K):
                at = (base + t) * TOP_K + k
                lo, hi = _unpack_pair(y_ref[_rows_at(dest_s[at]), :])
                w = wsel_s[at]
                acc_lo = acc_lo + w * lo
                acc_hi = acc_hi + w * hi
            lo_s[_slot_rows(t), :] = acc_lo
            hi_s[_slot_rows(t), :] = acc_hi
        return carry

    lax.fori_loop(0, COMBINE_TILE // SCATTER_UNROLL, body, 0)
    rows = lambda ref: [ref[pl.ds(j, COMBINE_TILE, stride=ROW_SUB), :] for j in range(ROW_SUB)]
    routed = jnp.concatenate(rows(lo_s) + rows(hi_s), axis=1)
    h_lo, h_hi = _unpack_pair(h_ref[...])
    x = jnp.concatenate([h_lo, h_hi], axis=1).astype(BF16)
    hs = _dot(x, ws13_ref[...])
    act = _silu(hs[:, :D_EXPERT]) * hs[:, D_EXPERT:]
    ff = routed + _dot(act.astype(BF16), ws2_ref[...])
    out_ref[...] = x1_ref[...] + gt2_ref[0] * _rms(ff, gpost_ref[...])


def _combine(dest_w, wsel_w, y_slots, h2, x1, gt2, tokens_per_gt2, gpost, ws13, ws2):
    nw = dest_w.shape[0]
    sub = PLAN_WINDOW // COMBINE_TILE
    plan = pl.BlockSpec((1, 1, TOP_K * PLAN_WINDOW), lambda w, s: (w, 0, 0))
    tok = lambda w, s: (w * sub + s, 0)
    return pl.pallas_call(
        _combine_kernel,
        out_shape=jax.ShapeDtypeStruct(x1.shape, F32),
        grid=(nw, sub),
        in_specs=[plan, plan,
                  pl.BlockSpec((REGION_ROWS * ROW_SUB, LANES), lambda w, s: (w, 0)),
                  pl.BlockSpec((COMBINE_TILE, ROW_WORDS), tok),
                  pl.BlockSpec((COMBINE_TILE, D_MODEL), tok),
                  pl.BlockSpec((1, 1, D_MODEL), lambda w, s: ((w * PLAN_WINDOW) // tokens_per_gt2, 0, 0)),
                  _full((1, D_MODEL)), _full(ws13.shape), _full(ws2.shape)],
        out_specs=pl.BlockSpec((COMBINE_TILE, D_MODEL), tok),
        scratch_shapes=[pltpu.SMEM((TOP_K * PLAN_WINDOW,), jnp.int32),
                        pltpu.SMEM((TOP_K * PLAN_WINDOW,), F32),
                        pltpu.VMEM((COMBINE_TILE * ROW_SUB, LANES), F32),
                        pltpu.VMEM((COMBINE_TILE * ROW_SUB, LANES), F32),
                        pltpu.SemaphoreType.DMA((2,))],
        compiler_params=pltpu.CompilerParams(dimension_semantics=("arbitrary", "arbitrary"),
                                             vmem_limit_bytes=VMEM_LIMIT),
        name="combine",
    )(dest_w, wsel_w, y_slots, h2, x1, gt2, gpost, ws13, ws2)


def _moe_kernel(h_ref, wd_ref, x1_ref, gt2_ref, gpost_ref, w1_ref, w3_ref, w2_ref, ws1_ref, ws3_ref, ws2_ref,
                out_ref, acc):
    e = pl.program_id(1)
    x = h_ref[...]

    @pl.when(e == 0)
    def _():
        hs = _silu(_dot(x, ws1_ref[...].astype(BF16))) * _dot(x, ws3_ref[...].astype(BF16))
        acc[...] = _dot(hs.astype(BF16), ws2_ref[...].astype(BF16))

    lane = lax.broadcasted_iota(jnp.int32, wd_ref.shape, 1)
    wcol = jnp.sum(jnp.where(lane == e, wd_ref[...], 0.0), axis=1, keepdims=True)
    act = _silu(_dot(x, w1_ref[0].astype(BF16))) * _dot(x, w3_ref[0].astype(BF16)) * wcol
    acc[...] += _dot(act.astype(BF16), w2_ref[0].astype(BF16))

    @pl.when(e == pl.num_programs(1) - 1)
    def _():
        out_ref[...] = x1_ref[...] + gt2_ref[0] * _rms(acc[...], gpost_ref[...])


def _moe(h2, wd, x1, gt2, rows_per_gt2, gpost, w1, w3, w2, ws1, ws3, ws2):
    t = h2.shape[0]
    tile = min(MOE_TILE, t)
    tok = lambda i, e: (i, 0)
    exp = lambda i, e: (e, 0, 0)
    return pl.pallas_call(
        _moe_kernel,
        out_shape=jax.ShapeDtypeStruct((t, D_MODEL), F32),
        grid=(t // tile, N_EXPERTS),
        in_specs=[pl.BlockSpec((tile, D_MODEL), tok),
                  pl.BlockSpec((tile, LANES), tok),
                  pl.BlockSpec((tile, D_MODEL), tok),
                  pl.BlockSpec((1,) + gt2.shape[1:], lambda i, e: (i // rows_per_gt2, 0, 0)),
                  _full((1, D_MODEL)),
                  pl.BlockSpec((1, D_MODEL, D_EXPERT), exp),
                  pl.BlockSpec((1, D_MODEL, D_EXPERT), exp),
                  pl.BlockSpec((1, D_EXPERT, D_MODEL), exp),
                  _full(ws1.shape), _full(ws3.shape), _full(ws2.shape)],
        out_specs=pl.BlockSpec((tile, D_MODEL), tok),
        scratch_shapes=[pltpu.VMEM((tile, D_MODEL), F32)],
        compiler_params=pltpu.CompilerParams(dimension_semantics=("parallel", "arbitrary"),
                                             vmem_limit_bytes=VMEM_LIMIT),
        name="moe",
    )(h2, wd, x1, gt2, gpost, w1, w3, w2, ws1, ws3, ws2)


def kernel(x_prompt, x_sample, c_prompt, c_sample, state_pool, state_hgrn, w_ada, b_ada, g_pre_mix, g_post_mix,
           w_in, w_pool, pool_scale, lb_logits, g_out_norm, w_out, g_pre_ffn, g_post_ffn, w_router, router_bias,
           w_exp_gate, w_exp_up, w_exp_down, w_sh_gate, w_sh_up, w_sh_down):
    assert w_ada.shape[0] == 1 and lb_logits.shape[0] == 2, "single-layer trunk"
    bp, lp, d = x_prompt.shape
    bs = x_sample.shape[0]
    row = lambda a: a[0].reshape(1, -1)

    n_mod = bp + bs
    pad = (-n_mod) % 16
    c_all = jnp.concatenate([c_prompt, c_sample, jnp.zeros((pad, d), F32)], axis=0)
    mod = _ada(c_all, w_ada[0], b_ada)
    mod_p = mod[:bp].reshape(bp, 1, 6 * d)
    mod_s = mod[bp:n_mod]

    win = w_in[0].astype(BF16)
    wout = w_out[0].astype(BF16)
    wpool = w_pool[0].astype(BF16)
    wr = jnp.pad(w_router[0], ((0, 0), (0, LANES - N_EXPERTS)))
    wrh = wr.astype(BF16)
    wrl = (wr - wrh.astype(F32)).astype(BF16)
    m2 = jnp.asarray(_M2_NP, BF16)
    gpre, gpost, gffn, gffn_post = row(g_pre_mix), row(g_post_mix), row(g_pre_ffn), row(g_post_ffn)
    pscale, gout = row(pool_scale), row(g_out_norm)
    bias_col = jnp.pad(router_bias[0], (0, LANES - N_EXPERTS)).reshape(LANES, 1)

    x1_p, h2_p, lg_p, pool_p, st_p = _mix_prompt(x_prompt, mod_p, gpre, gpost, gffn, win, wpool, pscale,
                                                  lb_logits, gout, wout, wrh, wrl, m2)
    x1_s, h2_s, lg_s, pool_s, st_s = _mix_sample(x_sample[:, 0, :], mod_s, state_pool[0], state_hgrn[0],
                                                  gpre, gpost, gffn, win, wpool, pscale, lb_logits, gout, wout,
                                                  wrh, wrl)

    experts = (w_exp_gate[0], w_exp_up[0], w_exp_down[0], w_sh_gate[0], w_sh_up[0], w_sh_down[0])
    tp = bp * lp
    nw = tp // PLAN_WINDOW
    assert lp % PLAN_WINDOW == 0
    upper = jnp.asarray(np.triu(np.ones((PLAN_WINDOW, PLAN_WINDOW), np.float32), 1), BF16)
    dest, wsel, cnt, loff = _plan(lg_p.reshape(tp, LANES), bias_col, upper)
    per_window = lambda a: a.T.reshape(nw, 1, PLAN_WINDOW * TOP_K)
    dest_w, wsel_w = per_window(dest), per_window(wsel)
    h2_rows = h2_p.reshape(tp, ROW_WORDS)
    x_slots = _dispatch(dest_w, h2_rows)
    y_slots = _expert_ffn(*_work_items(cnt[:, :, 0], loff[:, :, 0]), x_slots, *experts[:3])
    ws13 = jnp.concatenate([experts[3], experts[4]], axis=1).astype(BF16)
    gt2_p = mod_p[:, :, 5 * d:]
    y_p = _combine(dest_w, wsel_w, y_slots, h2_rows, x1_p.reshape(tp, d), gt2_p, lp, gffn_post,
                   ws13, experts[5].astype(BF16))
    wd_s = _router(lg_s, bias_col)
    gt2_s = mod_s[:, 5 * d:].reshape(1, bs, d)
    y_s = _moe(h2_s, wd_s, x1_s, gt2_s, 1, gffn_post, *experts)

    return (y_p.reshape(bp, lp, d), y_s.reshape(bs, 1, d), pool_p[None], st_p[None], pool_s[None], st_s[None])
```

```python
import functools

import numpy as np
import jax
import jax.numpy as jnp
from jax import lax
from jax.experimental import pallas as pl
from jax.experimental.pallas import tpu as pltpu

F32 = jnp.float32
BF16 = jnp.bfloat16

D_MODEL = 1024
POOL_WIDTH = 512
POOL_WINDOWS = (2, 4, 8, 16)
POOL_GROUP = 128
POOL_HIST = 15
HG_WIDTH = 512
HG_HEADS = 4
HG_HEAD_DIM = 128
IN_WIDTH = POOL_WIDTH + 4 * HG_WIDTH
N_EXPERTS = 64
TOP_K = 8
N_GROUPS = 8
TOPK_GROUPS = 4
GROUP_SIZE = N_EXPERTS // N_GROUPS
D_EXPERT = 256
ROUTED_SCALE = 2.5
EPS = 1e-6

LANES = 128
CHUNK = 64
TIME_TILE = 256
ADA_TILE_N = 512
ROUTER_TILE = 512
MOE_TILE = 1024
SAMPLE_STATE_BLOCK = 8
PLAN_WINDOW = 1024
RUN_ALIGN = 32
FFN_TILE = 160
REGION_ROWS = PLAN_WINDOW * TOP_K + N_EXPERTS * (RUN_ALIGN - 1) + FFN_TILE
ROW_WORDS = D_MODEL // 2
ROW_SUB = ROW_WORDS // LANES
COMBINE_TILE = 256
SCATTER_UNROLL = 8
FFN_GROUP = 8
FFN_RING = 3
VMEM_LIMIT = 58 * 1024 * 1024

_LEVELS = (64, 32, 16, 8, 4, 2)
_BLK_CUM = 0
_BLK_END = 1


def _level_blocks():
    c = CHUNK
    i = np.arange(c)[:, None]
    s = np.arange(c)[None, :]
    blocks = [(s <= i), (s > i)]
    index = {}
    for lvl in _LEVELS:
        seg = (i // lvl) * lvl
        mid = seg + lvl // 2
        ref = mid - 1
        index[lvl] = len(blocks)
        blocks.append((i >= mid) & (s > ref) & (s <= i))
        if lvl > 2:
            blocks.append((i < mid) & (s > i) & (s <= ref))
    m = np.concatenate(blocks, axis=0).astype(np.float32)
    return np.concatenate([m, m], axis=1), index


_M2_NP, _LEVEL_INDEX = _level_blocks()


def _dot(a, b):
    return jnp.dot(a, b, preferred_element_type=F32)


def _dot_nt(a, b):
    return lax.dot_general(a, b, (((1,), (1,)), ((), ())), preferred_element_type=F32)


def _split2(x):
    hi = x.astype(BF16)
    lo = (x - hi.astype(F32)).astype(BF16)
    return hi, lo


def _pack_pair(a, b):
    lo = lax.bitcast_convert_type(a.astype(BF16).astype(F32), jnp.uint32)
    hi = lax.bitcast_convert_type(b.astype(BF16).astype(F32), jnp.uint32)
    return (lo >> 16) | hi


def _unpack_pair(words):
    lo = lax.bitcast_convert_type(words << 16, F32)
    hi = lax.bitcast_convert_type(words & jnp.uint32(0xFFFF0000), F32)
    return lo, hi


def _silu(x):
    return x * jax.nn.sigmoid(x)


def _rms(x, g):
    return x * lax.rsqrt(jnp.mean(x * x, axis=-1, keepdims=True) + EPS) * g


def _mods(m):
    return [m[:, j * D_MODEL:(j + 1) * D_MODEL] for j in range(6)]


def _forget_lower_bound(lbl):
    mx = jnp.max(lbl, axis=0, keepdims=True)
    e = jnp.exp(lbl - mx)
    return e[0:1] / jnp.sum(e, axis=0, keepdims=True)


def _router_logits(h2, wrh, wrl):
    hi, lo = _split2(h2)
    return _dot(hi, wrh) + _dot(lo, wrh) + _dot(hi, wrl)


def _ada_kernel(c_ref, w_ref, b_ref, o_ref):
    a_hi, a_lo = _split2(_silu(c_ref[...]))
    w_hi, w_lo = _split2(w_ref[...])
    o_ref[...] = _dot(a_hi, w_hi) + _dot(a_lo, w_hi) + _dot(a_hi, w_lo) + b_ref[...]


def _ada(c_all, w_ada, b_ada):
    rows = c_all.shape[0]
    n = w_ada.shape[1]
    return pl.pallas_call(
        _ada_kernel,
        out_shape=jax.ShapeDtypeStruct((rows, n), F32),
        grid=(n // ADA_TILE_N,),
        in_specs=[pl.BlockSpec((rows, D_MODEL), lambda j: (0, 0)),
                  pl.BlockSpec((D_MODEL, ADA_TILE_N), lambda j: (0, j)),
                  pl.BlockSpec((1, ADA_TILE_N), lambda j: (0, j))],
        out_specs=pl.BlockSpec((rows, ADA_TILE_N), lambda j: (0, j)),
        compiler_params=pltpu.CompilerParams(dimension_semantics=("parallel",),
                                             vmem_limit_bytes=VMEM_LIMIT),
        name="ada",
    )(c_all, w_ada, b_ada)


def _level_masks():
    i = lax.broadcasted_iota(jnp.int32, (CHUNK, CHUNK), 0)
    j = lax.broadcasted_iota(jnp.int32, (CHUNK, CHUNK), 1)
    masks = {}
    for lvl in _LEVELS:
        sh = lvl.bit_length() - 1
        same = (i >> sh) == (j >> sh)
        upper = ((i >> (sh - 1)) & 1) == 1
        lower = ((j >> (sh - 1)) & 1) == 0
        masks[lvl] = same & upper & lower
    return masks, i == j


def _hgrn_chunk_head(q, k, v, e_all, lane0, st, masks, eye):
    def blk(n):
        return e_all[n * CHUNK:(n + 1) * CHUNK, lane0:lane0 + HG_HEAD_DIM]

    b = blk(_BLK_CUM)
    a = jnp.where(eye, _dot_nt(q.astype(BF16), k.astype(BF16)), 0.0)
    for lvl in _LEVELS:
        n = _LEVEL_INDEX[lvl]
        ql = (q * jnp.exp(blk(n))).astype(BF16)
        kl = (k * jnp.exp(blk(n + 1))).astype(BF16) if lvl > 2 else k.astype(BF16)
        a = a + jnp.where(masks[lvl], _dot_nt(ql, kl), 0.0)
    st_b = st.astype(BF16)
    o = _dot(a.astype(BF16), v.astype(BF16)) + _dot_nt((q * jnp.exp(b)).astype(BF16), st_b)
    k_end = (k * jnp.exp(blk(_BLK_END))).astype(BF16)
    decay = jnp.exp(b[CHUNK - 1:CHUNK, :])
    st_new = st * decay + _dot(v.T.astype(BF16), k_end)
    return o, st_new


def _mix_prompt_kernel(x_ref, mod_ref, gpre_ref, gpost_ref, gffn_ref, win_ref, wpool_ref, pscale_ref,
                       lbl_ref, gout_ref, wout_ref, wrh_ref, wrl_ref, m2_ref,
                       x1_ref, h2_ref, lg_ref, pool_ref, st_ref,
                       st_s, ubuf, q_s, k_s, v_s, g_s, o_s):
    t = pl.program_id(1)
    n_t = pl.num_programs(1)
    tt = TIME_TILE

    @pl.when(t == 0)
    def _():
        st_s[...] = jnp.zeros_like(st_s)
        ubuf[0:16, :] = jnp.zeros((16, POOL_WIDTH), F32)

    xt = x_ref[0]
    sh1, sc1, gt1, sh2, sc2, gt2 = _mods(mod_ref[0])
    h = _rms(xt, gpre_ref[...]) * (1.0 + sc1) + sh1
    proj = _dot(h.astype(BF16), win_ref[...])

    u = proj[:, :POOL_WIDTH]
    ubuf[16:16 + tt, :] = u
    pos = (t * tt + lax.broadcasted_iota(jnp.int32, (tt, 1), 0) + 1).astype(F32)
    ys = []
    for g, w in enumerate(POOL_WINDOWS):
        s = ubuf[:, g * POOL_GROUP:(g + 1) * POOL_GROUP]
        off = 0
        for step in range(w.bit_length() - 1):
            sh = 1 << step
            s = s[sh:, :] + s[:-sh, :]
            off += sh
        ws = s[16 - off:16 - off + tt, :]
        cnt = jnp.minimum(pos, float(w))
        d = ws / cnt - u[:, g * POOL_GROUP:(g + 1) * POOL_GROUP]
        ys.append(_dot(d.astype(BF16), wpool_ref[g]))
    y_pool = jnp.concatenate(ys, axis=1) * pscale_ref[...]
    ubuf[0:16, :] = ubuf[tt:tt + 16, :]

    @pl.when(t == n_t - 1)
    def _():
        pool_ref[0] = u[tt - POOL_HIST:, :]

    lb = _forget_lower_bound(lbl_ref[...])
    f = lb + (1.0 - lb) * jax.nn.sigmoid(proj[:, POOL_WIDTH + HG_WIDTH:POOL_WIDTH + 2 * HG_WIDTH])
    q_s[...] = _silu(proj[:, POOL_WIDTH:POOL_WIDTH + HG_WIDTH])
    k_s[...] = 1.0 - f
    v_s[...] = proj[:, POOL_WIDTH + 2 * HG_WIDTH:POOL_WIDTH + 3 * HG_WIDTH]
    g_s[...] = jnp.log(f)
    gate = _silu(proj[:, POOL_WIDTH + 3 * HG_WIDTH:])

    masks, eye = _level_masks()
    m2 = m2_ref[...]

    states = [st_s[hd] for hd in range(HG_HEADS)]
    for c in range(tt // CHUNK):
        rows = slice(c * CHUNK, (c + 1) * CHUNK)
        g_hi, g_lo = _split2(g_s[rows, :])
        e_all = _dot(m2, jnp.concatenate([g_hi, g_lo], axis=0))
        for hd in range(HG_HEADS):
            lane0 = hd * HG_HEAD_DIM
            lanes = slice(lane0, lane0 + HG_HEAD_DIM)
            o, states[hd] = _hgrn_chunk_head(q_s[rows, lanes], k_s[rows, lanes], v_s[rows, lanes], e_all, lane0,
                                             states[hd], masks, eye)
            o_s[rows, lanes] = o
    for hd in range(HG_HEADS):
        st_s[hd] = states[hd]

    o = o_s[...]
    os_ = []
    for hd in range(HG_HEADS):
        oh = o[:, hd * HG_HEAD_DIM:(hd + 1) * HG_HEAD_DIM]
        os_.append(oh * lax.rsqrt(jnp.mean(oh * oh, axis=-1, keepdims=True) + EPS) * gout_ref[...])
    o_n = jnp.concatenate(os_, axis=1) * gate

    mix = _dot(y_pool.astype(BF16), wout_ref[0:POOL_WIDTH, :]) + _dot(o_n.astype(BF16), wout_ref[POOL_WIDTH:, :])
    x1 = xt + gt1 * _rms(mix, gpost_ref[...])
    h2 = _rms(x1, gffn_ref[...]) * (1.0 + sc2) + sh2
    x1_ref[0] = x1
    h2_ref[0] = _pack_pair(h2[:, :ROW_WORDS], h2[:, ROW_WORDS:])
    lg_ref[0] = _router_logits(h2, wrh_ref[...], wrl_ref[...])

    @pl.when(t == n_t - 1)
    def _():
        for hd in range(HG_HEADS):
            st_ref[0, hd] = st_s[hd].T


def _full(shape):
    nd = len(shape)
    return pl.BlockSpec(shape, lambda *_: (0,) * nd)


def _mix_prompt(x, mod, gpre, gpost, gffn, win, wpool, pscale, lbl, gout, wout, wrh, wrl, m2):
    b, l, d = x.shape
    tt = TIME_TILE
    tile = lambda i, j: (i, j, 0)
    per_b = lambda i, j: (i, 0, 0)
    return pl.pallas_call(
        _mix_prompt_kernel,
        out_shape=(jax.ShapeDtypeStruct((b, l, d), F32),
                   jax.ShapeDtypeStruct((b, l, ROW_WORDS), jnp.uint32),
                   jax.ShapeDtypeStruct((b, l, LANES), F32),
                   jax.ShapeDtypeStruct((b, POOL_HIST, POOL_WIDTH), F32),
                   jax.ShapeDtypeStruct((b, HG_HEADS, HG_HEAD_DIM, HG_HEAD_DIM), F32)),
        grid=(b, l // tt),
        in_specs=[pl.BlockSpec((1, tt, d), tile),
                  pl.BlockSpec((1, 1, 6 * d), per_b),
                  _full((1, d)), _full((1, d)), _full((1, d)),
                  _full(win.shape), _full(wpool.shape), _full((1, POOL_WIDTH)),
                  _full(lbl.shape), _full((1, HG_HEAD_DIM)), _full(wout.shape),
                  _full(wrh.shape), _full(wrl.shape), _full(m2.shape)],
        out_specs=(pl.BlockSpec((1, tt, d), tile),
                   pl.BlockSpec((1, tt, ROW_WORDS), tile),
                   pl.BlockSpec((1, tt, LANES), tile),
                   pl.BlockSpec((1, POOL_HIST, POOL_WIDTH), per_b),
                   pl.BlockSpec((1, HG_HEADS, HG_HEAD_DIM, HG_HEAD_DIM), lambda i, j: (i, 0, 0, 0))),
        scratch_shapes=[pltpu.VMEM((HG_HEADS, HG_HEAD_DIM, HG_HEAD_DIM), F32),
                        pltpu.VMEM((tt + 16, POOL_WIDTH), F32),
                        pltpu.VMEM((tt, HG_WIDTH), F32), pltpu.VMEM((tt, HG_WIDTH), F32),
                        pltpu.VMEM((tt, HG_WIDTH), F32), pltpu.VMEM((tt, HG_WIDTH), F32),
                        pltpu.VMEM((tt, HG_WIDTH), F32)],
        compiler_params=pltpu.CompilerParams(dimension_semantics=("parallel", "arbitrary"),
                                             vmem_limit_bytes=VMEM_LIMIT),
        name="mix_prompt",
    )(x, mod, gpre, gpost, gffn, win, wpool, pscale, lbl, gout, wout, wrh, wrl, m2)


def _mix_sample_in_kernel(x_ref, mod_ref, gpre_ref, win_ref, wpool_ref, pscale_ref, lbl_ref, hist_ref,
                          ypool_ref, npool_ref, ft_ref, qt_ref, v_ref, gate_ref):
    xt = x_ref[...]
    sh1, sc1 = _mods(mod_ref[...])[:2]
    h = _rms(xt, gpre_ref[...]) * (1.0 + sc1) + sh1
    proj = _dot(h.astype(BF16), win_ref[...])
    u = proj[:, :POOL_WIDTH]
    row = lax.broadcasted_iota(jnp.int32, (hist_ref.shape[0], POOL_HIST, POOL_GROUP), 1)
    ys = []
    for g, w in enumerate(POOL_WINDOWS):
        sl = slice(g * POOL_GROUP, (g + 1) * POOL_GROUP)
        past = jnp.sum(jnp.where(row >= POOL_HIST - (w - 1), hist_ref[:, :, sl], 0.0), axis=1)
        ug = u[:, sl]
        d = (past + ug) / float(w) - ug
        ys.append(_dot(d.astype(BF16), wpool_ref[g]))
    ypool_ref[...] = jnp.concatenate(ys, axis=1) * pscale_ref[...]
    npool_ref[:, 0:POOL_HIST - 1, :] = hist_ref[:, 1:POOL_HIST, :]
    npool_ref[:, POOL_HIST - 1, :] = u

    lb = _forget_lower_bound(lbl_ref[...])
    f = lb + (1.0 - lb) * jax.nn.sigmoid(proj[:, POOL_WIDTH + HG_WIDTH:POOL_WIDTH + 2 * HG_WIDTH])
    ft_ref[...] = f.T
    qt_ref[...] = _silu(proj[:, POOL_WIDTH:POOL_WIDTH + HG_WIDTH]).T
    v_ref[...] = proj[:, POOL_WIDTH + 2 * HG_WIDTH:POOL_WIDTH + 3 * HG_WIDTH]
    gate_ref[...] = _silu(proj[:, POOL_WIDTH + 3 * HG_WIDTH:])


def _mix_sample_state_kernel(s_ref, ft_ref, qt_ref, v_ref, snew_ref, o_ref):
    i = pl.program_id(0)
    lane = lax.broadcasted_iota(jnp.int32, (HG_HEAD_DIM, ft_ref.shape[1]), 1)
    for j in range(SAMPLE_STATE_BLOCK):
        mine = lane == i * SAMPLE_STATE_BLOCK + j
        for hd in range(HG_HEADS):
            r0 = hd * HG_HEAD_DIM
            f = jnp.sum(jnp.where(mine, ft_ref[r0:r0 + HG_HEAD_DIM, :], 0.0), axis=1, keepdims=True)
            q = jnp.sum(jnp.where(mine, qt_ref[r0:r0 + HG_HEAD_DIM, :], 0.0), axis=1, keepdims=True)
            v = v_ref[j:j + 1, r0:r0 + HG_HEAD_DIM]
            s_new = f * s_ref[j, hd] + (1.0 - f) * v
            snew_ref[j, hd] = s_new
            o_ref[j:j + 1, r0:r0 + HG_HEAD_DIM] = jnp.sum(q * s_new, axis=0, keepdims=True)


def _mix_sample_out_kernel(x_ref, mod_ref, o_ref, gate_ref, ypool_ref, gout_ref, wout_ref, gpost_ref,
                           gffn_ref, wrh_ref, wrl_ref, x1_ref, h2_ref, lg_ref):
    _, _, gt1, sh2, sc2, _ = _mods(mod_ref[...])
    o = o_ref[...]
    os_ = []
    for hd in range(HG_HEADS):
        oh = o[:, hd * HG_HEAD_DIM:(hd + 1) * HG_HEAD_DIM]
        os_.append(oh * lax.rsqrt(jnp.mean(oh * oh, axis=-1, keepdims=True) + EPS) * gout_ref[...])
    o_n = jnp.concatenate(os_, axis=1) * gate_ref[...]
    mix = (_dot(ypool_ref[...].astype(BF16), wout_ref[0:POOL_WIDTH, :])
           + _dot(o_n.astype(BF16), wout_ref[POOL_WIDTH:, :]))
    x1 = x_ref[...] + gt1 * _rms(mix, gpost_ref[...])
    h2 = _rms(x1, gffn_ref[...]) * (1.0 + sc2) + sh2
    x1_ref[...] = x1
    h2_ref[...] = h2.astype(BF16)
    lg_ref[...] = _router_logits(h2, wrh_ref[...], wrl_ref[...])


def _mix_sample(x, mod, hist, state, gpre, gpost, gffn, win, wpool, pscale, lbl, gout, wout, wrh, wrl):
    b = x.shape[0]
    bb = SAMPLE_STATE_BLOCK
    cp = pltpu.CompilerParams(vmem_limit_bytes=VMEM_LIMIT)
    ypool, npool, ft, qt, v, gate = pl.pallas_call(
        _mix_sample_in_kernel,
        out_shape=(jax.ShapeDtypeStruct((b, POOL_WIDTH), F32),
                   jax.ShapeDtypeStruct((b, POOL_HIST, POOL_WIDTH), F32),
                   jax.ShapeDtypeStruct((HG_WIDTH, b), F32),
                   jax.ShapeDtypeStruct((HG_WIDTH, b), F32),
                   jax.ShapeDtypeStruct((b, HG_WIDTH), F32),
                   jax.ShapeDtypeStruct((b, HG_WIDTH), F32)),
        compiler_params=cp,
        name="mix_sample_in",
    )(x, mod, gpre, win, wpool, pscale, lbl, hist)

    s_spec = pl.BlockSpec((bb, HG_HEADS, HG_HEAD_DIM, HG_HEAD_DIM), lambda i: (i, 0, 0, 0))
    col_spec = _full((HG_WIDTH, b))
    row_spec = pl.BlockSpec((bb, HG_WIDTH), lambda i: (i, 0))
    s_new, o = pl.pallas_call(
        _mix_sample_state_kernel,
        out_shape=(jax.ShapeDtypeStruct(state.shape, F32), jax.ShapeDtypeStruct((b, HG_WIDTH), F32)),
        grid=(b // bb,),
        in_specs=[s_spec, col_spec, col_spec, row_spec],
        out_specs=(s_spec, row_spec),
        compiler_params=pltpu.CompilerParams(dimension_semantics=("parallel",), vmem_limit_bytes=VMEM_LIMIT),
        name="mix_sample_state",
    )(state, ft, qt, v)

    x1, h2, lg = pl.pallas_call(
        _mix_sample_out_kernel,
        out_shape=(jax.ShapeDtypeStruct((b, D_MODEL), F32),
                   jax.ShapeDtypeStruct((b, D_MODEL), BF16),
                   jax.ShapeDtypeStruct((b, LANES), F32)),
        compiler_params=cp,
        name="mix_sample_out",
    )(x, mod, o, gate, ypool, gout, wout, gpost, gffn, wrh, wrl)
    return x1, h2, lg, npool, s_new


def _first_max(cur, idx, big):
    m = jnp.max(cur, axis=0, keepdims=True)
    first = jnp.min(jnp.where(cur == m, idx, big), axis=0, keepdims=True)
    return idx == first


def _route(logit, bias):
    n = logit.shape[1]
    scores = jax.nn.sigmoid(logit)
    sel = scores + bias
    neg = -jnp.inf

    sub = lax.broadcasted_iota(jnp.int32, (GROUP_SIZE, n), 0)
    gscore = []
    for g in range(N_GROUPS):
        sg = sel[g * GROUP_SIZE:(g + 1) * GROUP_SIZE, :]
        m1 = jnp.max(sg, axis=0, keepdims=True)
        rest = jnp.where(_first_max(sg, sub, GROUP_SIZE), neg, sg)
        gscore.append(m1 + jnp.max(rest, axis=0, keepdims=True))
    cur = jnp.concatenate(gscore, axis=0)
    gidx = lax.broadcasted_iota(jnp.int32, (N_GROUPS, n), 0)
    gmask = jnp.zeros((N_GROUPS, n), jnp.bool_)
    for _ in range(TOPK_GROUPS):
        pick = _first_max(cur, gidx, N_GROUPS)
        gmask = gmask | pick
        cur = jnp.where(pick, neg, cur)
    emask = jnp.concatenate(
        [jnp.broadcast_to(gmask[g:g + 1, :], (GROUP_SIZE, n)) for g in range(N_GROUPS)], axis=0)

    cur = jnp.where(emask, sel, neg)
    eidx = lax.broadcasted_iota(jnp.int32, (N_EXPERTS, n), 0)
    picks = []
    for _ in range(TOP_K):
        pick = _first_max(cur, eidx, N_EXPERTS)
        picks.append(pick)
        cur = jnp.where(pick, neg, cur)
    return scores, picks


def _any(masks):
    return functools.reduce(jnp.logical_or, masks)


def _router_kernel(lg_ref, bias_ref, wd_ref):
    n = lg_ref.shape[0]
    scores, picks = _route(lg_ref[...].T[0:N_EXPERTS, :], bias_ref[0:N_EXPERTS, :])
    chosen = _any(picks)
    wsum = jnp.sum(jnp.where(chosen, scores, 0.0), axis=0, keepdims=True)
    wd = jnp.where(chosen, scores / wsum * ROUTED_SCALE, 0.0)
    wd_ref[...] = jnp.concatenate([wd, jnp.zeros((LANES - N_EXPERTS, n), F32)], axis=0).T


def _router(logits, bias_col):
    t = logits.shape[0]
    tile = min(ROUTER_TILE, t)
    return pl.pallas_call(
        _router_kernel,
        out_shape=jax.ShapeDtypeStruct((t, LANES), F32),
        grid=(t // tile,),
        in_specs=[pl.BlockSpec((tile, LANES), lambda i: (i, 0)), _full(bias_col.shape)],
        out_specs=pl.BlockSpec((tile, LANES), lambda i: (i, 0)),
        compiler_params=pltpu.CompilerParams(dimension_semantics=("parallel",), vmem_limit_bytes=VMEM_LIMIT),
        name="router",
    )(logits, bias_col)


def _plan_kernel(lg_ref, bias_ref, upper_ref, dest_ref, wsel_ref, cnt_ref, loff_ref):
    scores, picks = _route(lg_ref[...].T[0:N_EXPERTS, :], bias_ref[0:N_EXPERTS, :])
    chosen = _any(picks)
    chosen_f = jnp.where(chosen, 1.0, 0.0)
    cnt = jnp.sum(chosen_f, axis=1, keepdims=True)
    units = jnp.floor((cnt + (RUN_ALIGN - 1)) / RUN_ALIGN)
    ei = lax.broadcasted_iota(jnp.int32, (N_EXPERTS, N_EXPERTS), 0)
    ej = lax.broadcasted_iota(jnp.int32, (N_EXPERTS, N_EXPERTS), 1)
    before = jnp.where(ej < ei, 1.0, 0.0).astype(BF16)
    loff = RUN_ALIGN * _dot(before, jnp.broadcast_to(units, (N_EXPERTS, LANES)).astype(BF16))[:, 0:1]
    slot = loff + _dot(chosen_f.astype(BF16), upper_ref[...])
    wsum = jnp.sum(jnp.where(chosen, scores, 0.0), axis=0, keepdims=True)
    pick_sum = lambda v: jnp.concatenate(
        [jnp.sum(jnp.where(p, v, 0.0), axis=0, keepdims=True) for p in picks], axis=0)
    dest_ref[...] = (pick_sum(slot) * ROW_SUB).astype(jnp.int32)
    wsel_ref[...] = pick_sum(scores) / wsum * ROUTED_SCALE
    cnt_ref[0] = jnp.broadcast_to(cnt, (N_EXPERTS, LANES))
    loff_ref[0] = jnp.broadcast_to(loff, (N_EXPERTS, LANES))


def _plan(logits, bias_col, upper):
    t = logits.shape[0]
    nw = t // PLAN_WINDOW
    per_w = pl.BlockSpec((1, N_EXPERTS, LANES), lambda w: (w, 0, 0))
    picks = pl.BlockSpec((TOP_K, PLAN_WINDOW), lambda w: (0, w))
    return pl.pallas_call(
        _plan_kernel,
        out_shape=(jax.ShapeDtypeStruct((TOP_K, t), jnp.int32), jax.ShapeDtypeStruct((TOP_K, t), F32),
                   jax.ShapeDtypeStruct((nw, N_EXPERTS, LANES), F32),
                   jax.ShapeDtypeStruct((nw, N_EXPERTS, LANES), F32)),
        grid=(nw,),
        in_specs=[pl.BlockSpec((PLAN_WINDOW, LANES), lambda w: (w, 0)), _full(bias_col.shape), _full(upper.shape)],
        out_specs=(picks, picks, per_w, per_w),
        compiler_params=pltpu.CompilerParams(dimension_semantics=("parallel",), vmem_limit_bytes=VMEM_LIMIT),
        name="plan",
    )(logits, bias_col, upper)


def _slot_rows(slot):
    return pl.ds(pl.multiple_of(slot * ROW_SUB, ROW_SUB), ROW_SUB)


def _rows_at(offset):
    return pl.ds(pl.multiple_of(offset, ROW_SUB), ROW_SUB)


def _dispatch_kernel(dest_ref, h_ref, x_ref, dest_s, rows_s, sem):
    plan_copy = pltpu.make_async_copy(dest_ref.at[0, 0], dest_s, sem)
    plan_copy.start()
    x_ref[...] = jnp.zeros_like(x_ref)
    for j in range(ROW_SUB):
        rows_s[pl.ds(j, PLAN_WINDOW, stride=ROW_SUB), :] = h_ref[:, j * LANES:(j + 1) * LANES]
    plan_copy.wait()

    def body(i, carry):
        for u in range(SCATTER_UNROLL):
            t = i * SCATTER_UNROLL + u
            row = rows_s[_slot_rows(t), :]
            for k in range(TOP_K):
                x_ref[_rows_at(dest_s[t * TOP_K + k]), :] = row
        return carry

    lax.fori_loop(0, PLAN_WINDOW // SCATTER_UNROLL, body, 0)


def _dispatch(dest_w, h2):
    nw = dest_w.shape[0]
    region = REGION_ROWS * ROW_SUB
    return pl.pallas_call(
        _dispatch_kernel,
        out_shape=jax.ShapeDtypeStruct((nw * region, LANES), jnp.uint32),
        grid=(nw,),
        in_specs=[pl.BlockSpec((1, 1, TOP_K * PLAN_WINDOW), lambda w: (w, 0, 0)),
                  pl.BlockSpec((PLAN_WINDOW, ROW_WORDS), lambda w: (w, 0))],
        out_specs=pl.BlockSpec((region, LANES), lambda w: (w, 0)),
        scratch_shapes=[pltpu.SMEM((TOP_K * PLAN_WINDOW,), jnp.int32),
                        pltpu.VMEM((PLAN_WINDOW * ROW_SUB, LANES), jnp.uint32),
                        pltpu.SemaphoreType.DMA],
        compiler_params=pltpu.CompilerParams(dimension_semantics=("arbitrary",), vmem_limit_bytes=VMEM_LIMIT),
        name="dispatch",
    )(dest_w, h2)


def _unpack_rows(ref, n):
    parts = [_unpack_pair(ref[pl.ds(j, n, stride=ROW_SUB), :]) for j in range(ROW_SUB)]
    return jnp.concatenate([p[0] for p in parts] + [p[1] for p in parts], axis=1)


def _ffn_kernel(gq_ref, gs_ref, to_ref, tn_ref, x_hbm, w1_ref, w3_ref, w2_ref, y_hbm,
                w13_s, w2_s, xbuf, ybuf, rsem, wsem):
    del x_hbm
    e = pl.program_id(0)
    n_groups = gq_ref[N_EXPERTS]
    block = RUN_ALIGN * ROW_SUB
    tile = FFN_TILE * ROW_SUB

    def go(cp, start):
        if start:
            cp.start()
        else:
            cp.wait()

    def fetch(q, start):
        def one(j):
            src = y_hbm.at[pl.ds(pl.multiple_of(to_ref[gs_ref[q] + j] * block, block), tile)]
            go(pltpu.make_async_copy(src, xbuf.at[q % FFN_RING, pl.ds(j * tile, tile)], rsem.at[q % FFN_RING]), start)

        size = gs_ref[q + 1] - gs_ref[q]

        @pl.when(size == FFN_GROUP)
        def _():
            for j in range(FFN_GROUP):
                one(j)

        @pl.when(size < FFN_GROUP)
        def _():
            for j in range(FFN_GROUP - 1):
                pl.when(j < size)(functools.partial(one, j))

    def writeback(q, start):
        most = FFN_TILE // RUN_ALIGN - 1

        def blocks(j, b, n):
            dst = pl.multiple_of(to_ref[gs_ref[q] + j] * block, block) + b * block
            go(pltpu.make_async_copy(ybuf.at[q % FFN_RING, pl.ds(j * tile + b * block, n * block)],
                                     y_hbm.at[pl.ds(dst, n * block)], wsem.at[q % FFN_RING]), start)

        for j in range(FFN_GROUP):
            item = gs_ref[q] + j
            n = jnp.where(item < gs_ref[q + 1], tn_ref[item], 0)
            pl.when(n >= most)(functools.partial(blocks, j, 0, most))
            pl.when(n > most)(functools.partial(blocks, j, most, 1))

            @pl.when((n > 0) & (n < most))
            def _():
                for b in range(most - 1):
                    pl.when(b < n)(functools.partial(blocks, j, b, 1))

    @pl.when(e == 0)
    def _():
        xbuf[...] = jnp.zeros_like(xbuf)
        for d in range(FFN_RING - 1):
            pl.when(d < n_groups)(functools.partial(fetch, d, True))

    w13_s[:, 0:D_EXPERT] = w1_ref[0].astype(BF16)
    w13_s[:, D_EXPERT:] = w3_ref[0].astype(BF16)
    w2_s[...] = w2_ref[0].astype(BF16)

    def group_body(q, carry):
        @pl.when(q + FFN_RING - 1 < n_groups)
        def _():
            fetch(q + FFN_RING - 1, True)

        @pl.when(q >= FFN_RING)
        def _():
            writeback(q - FFN_RING, False)

        fetch(q, False)
        x = _unpack_rows(xbuf.at[q % FFN_RING], FFN_GROUP * FFN_TILE).astype(BF16)
        h = _dot(x, w13_s[...])
        act = _silu(h[:, :D_EXPERT]) * h[:, D_EXPERT:]
        y = _dot(act.astype(BF16), w2_s[...])
        out = ybuf.at[q % FFN_RING]
        for j in range(ROW_SUB):
            out[pl.ds(j, FFN_GROUP * FFN_TILE, stride=ROW_SUB), :] = _pack_pair(
                y[:, j * LANES:(j + 1) * LANES], y[:, ROW_WORDS + j * LANES:ROW_WORDS + (j + 1) * LANES])
        writeback(q, True)
        return carry

    lax.fori_loop(gq_ref[e], gq_ref[e + 1], group_body, 0)

    @pl.when(e == pl.num_programs(0) - 1)
    def _():
        for d in range(FFN_RING):
            @pl.when(n_groups - 1 - d >= 0)
            def _():
                writeback(n_groups - 1 - d, False)


def _expert_ffn(group_of_expert, group_start, item_off, item_blocks, x_slots, w1, w3, w2):
    of_expert = lambda e, *_: (e, 0, 0)
    tiles = pltpu.VMEM((FFN_RING, FFN_GROUP * FFN_TILE * ROW_SUB, LANES), jnp.uint32)
    sems = pltpu.SemaphoreType.DMA((FFN_RING,))
    grid_spec = pltpu.PrefetchScalarGridSpec(
        num_scalar_prefetch=4,
        grid=(N_EXPERTS,),
        in_specs=[pl.BlockSpec(memory_space=pl.ANY),
                  pl.BlockSpec((1, D_MODEL, D_EXPERT), of_expert),
                  pl.BlockSpec((1, D_MODEL, D_EXPERT), of_expert),
                  pl.BlockSpec((1, D_EXPERT, D_MODEL), of_expert)],
        out_specs=pl.BlockSpec(memory_space=pl.ANY),
        scratch_shapes=[pltpu.VMEM((D_MODEL, 2 * D_EXPERT), BF16), pltpu.VMEM((D_EXPERT, D_MODEL), BF16),
                        tiles, tiles, sems, sems])
    return pl.pallas_call(
        _ffn_kernel,
        out_shape=jax.ShapeDtypeStruct(x_slots.shape, jnp.uint32),
        grid_spec=grid_spec,
        input_output_aliases={4: 0},
        compiler_params=pltpu.CompilerParams(dimension_semantics=("arbitrary",), vmem_limit_bytes=VMEM_LIMIT),
        name="expert_ffn",
    )(group_of_expert, group_start, item_off, item_blocks, x_slots, w1, w3, w2)


def _work_items(cnt, loff):
    nw = cnt.shape[0]
    cnt = cnt.astype(jnp.int32).T.reshape(-1)
    loff = loff.astype(jnp.int32).T.reshape(-1)
    region = jnp.tile(jnp.arange(nw, dtype=jnp.int32) * REGION_ROWS, N_EXPERTS)
    n_items = nw * N_EXPERTS + (nw * PLAN_WINDOW * TOP_K) // FFN_TILE
    tiles = (cnt + FFN_TILE - 1) // FFN_TILE
    ends = jnp.cumsum(tiles)
    starts = ends - tiles
    i = jnp.arange(n_items + FFN_GROUP, dtype=jnp.int32)[:, None]
    mine = (starts[None, :] <= i) & (i < ends[None, :])
    of_pair = lambda v: jnp.sum(jnp.where(mine, v[None, :], 0), axis=1)
    r = i[:, 0] - of_pair(starts)
    row = of_pair(region + loff) + r * FFN_TILE
    own = jnp.clip(of_pair(cnt) - r * FFN_TILE, 0, FFN_TILE)

    zero = jnp.zeros((1,), jnp.int32)
    item_end = ends.reshape(N_EXPERTS, nw)[:, -1]
    item_start = jnp.concatenate([zero, item_end[:-1]])
    group_end = jnp.cumsum((item_end - item_start + FFN_GROUP - 1) // FFN_GROUP)
    group_first = jnp.concatenate([zero, group_end[:-1]])
    q = jnp.arange(N_EXPERTS + n_items // FFN_GROUP + 1, dtype=jnp.int32)[:, None]
    has = (group_first[None, :] <= q) & (q < group_end[None, :])
    of_expert = lambda v: jnp.sum(jnp.where(has, v[None, :], 0), axis=1)
    group_start = jnp.where(q[:, 0] < group_end[-1],
                            of_expert(item_start) + (q[:, 0] - of_expert(group_first)) * FFN_GROUP, item_end[-1])
    return (jnp.concatenate([zero, group_end]), group_start, row // RUN_ALIGN, (own + RUN_ALIGN - 1) // RUN_ALIGN)


def _combine_kernel(dest_ref, wsel_ref, y_ref, h_ref, x1_ref, gt2_ref, gpost_ref, ws13_ref, ws2_ref, out_ref,
                    dest_s, wsel_s, lo_s, hi_s, sem):
    s = pl.program_id(1)

    @pl.when(s == 0)
    def _():
        copies = (pltpu.make_async_copy(dest_ref.at[0, 0], dest_s, sem.at[0]),
                  pltpu.make_async_copy(wsel_ref.at[0, 0], wsel_s, sem.at[1]))
        for c in copies:
            c.start()
        for c in copies:
            c.wait()

    base = s * COMBINE_TILE

    def body(i, carry):
        for u in range(SCATTER_UNROLL):
            t = i * SCATTER_UNROLL + u
            acc_lo = jnp.zeros((ROW_SUB, LANES), F32)
            acc_hi = jnp.zeros((ROW_SUB, LANES), F32)
            for k in range(TOP_K):
                at = (base + t) * TOP_K + k
                lo, hi = _unpack_pair(y_ref[_rows_at(dest_s[at]), :])
                w = wsel_s[at]
                acc_lo = acc_lo + w * lo
                acc_hi = acc_hi + w * hi
            lo_s[_slot_rows(t), :] = acc_lo
            hi_s[_slot_rows(t), :] = acc_hi
        return carry

    lax.fori_loop(0, COMBINE_TILE // SCATTER_UNROLL, body, 0)
    rows = lambda ref: [ref[pl.ds(j, COMBINE_TILE, stride=ROW_SUB), :] for j in range(ROW_SUB)]
    routed = jnp.concatenate(rows(lo_s) + rows(hi_s), axis=1)
    h_lo, h_hi = _unpack_pair(h_ref[...])
    x = jnp.concatenate([h_lo, h_hi], axis=1).astype(BF16)
    hs = _dot(x, ws13_ref[...])
    act = _silu(hs[:, :D_EXPERT]) * hs[:, D_EXPERT:]
    ff = routed + _dot(act.astype(BF16), ws2_ref[...])
    out_ref[...] = x1_ref[...] + gt2_ref[0] * _rms(ff, gpost_ref[...])


def _combine(dest_w, wsel_w, y_slots, h2, x1, gt2, tokens_per_gt2, gpost, ws13, ws2):
    nw = dest_w.shape[0]
    sub = PLAN_WINDOW // COMBINE_TILE
    plan = pl.BlockSpec((1, 1, TOP_K * PLAN_WINDOW), lambda w, s: (w, 0, 0))
    tok = lambda w, s: (w * sub + s, 0)
    return pl.pallas_call(
        _combine_kernel,
        out_shape=jax.ShapeDtypeStruct(x1.shape, F32),
        grid=(nw, sub),
        in_specs=[plan, plan,
                  pl.BlockSpec((REGION_ROWS * ROW_SUB, LANES), lambda w, s: (w, 0)),
                  pl.BlockSpec((COMBINE_TILE, ROW_WORDS), tok),
                  pl.BlockSpec((COMBINE_TILE, D_MODEL), tok),
                  pl.BlockSpec((1, 1, D_MODEL), lambda w, s: ((w * PLAN_WINDOW) // tokens_per_gt2, 0, 0)),
                  _full((1, D_MODEL)), _full(ws13.shape), _full(ws2.shape)],
        out_specs=pl.BlockSpec((COMBINE_TILE, D_MODEL), tok),
        scratch_shapes=[pltpu.SMEM((TOP_K * PLAN_WINDOW,), jnp.int32),
                        pltpu.SMEM((TOP_K * PLAN_WINDOW,), F32),
                        pltpu.VMEM((COMBINE_TILE * ROW_SUB, LANES), F32),
                        pltpu.VMEM((COMBINE_TILE * ROW_SUB, LANES), F32),
                        pltpu.SemaphoreType.DMA((2,))],
        compiler_params=pltpu.CompilerParams(dimension_semantics=("arbitrary", "arbitrary"),
                                             vmem_limit_bytes=VMEM_LIMIT),
        name="combine",
    )(dest_w, wsel_w, y_slots, h2, x1, gt2, gpost, ws13, ws2)


def _moe_kernel(h_ref, wd_ref, x1_ref, gt2_ref, gpost_ref, w1_ref, w3_ref, w2_ref, ws1_ref, ws3_ref, ws2_ref,
                out_ref, acc):
    e = pl.program_id(1)
    x = h_ref[...]

    @pl.when(e == 0)
    def _():
        hs = _silu(_dot(x, ws1_ref[...].astype(BF16))) * _dot(x, ws3_ref[...].astype(BF16))
        acc[...] = _dot(hs.astype(BF16), ws2_ref[...].astype(BF16))

    lane = lax.broadcasted_iota(jnp.int32, wd_ref.shape, 1)
    wcol = jnp.sum(jnp.where(lane == e, wd_ref[...], 0.0), axis=1, keepdims=True)
    act = _silu(_dot(x, w1_ref[0].astype(BF16))) * _dot(x, w3_ref[0].astype(BF16)) * wcol
    acc[...] += _dot(act.astype(BF16), w2_ref[0].astype(BF16))

    @pl.when(e == pl.num_programs(1) - 1)
    def _():
        out_ref[...] = x1_ref[...] + gt2_ref[0] * _rms(acc[...], gpost_ref[...])


def _moe(h2, wd, x1, gt2, rows_per_gt2, gpost, w1, w3, w2, ws1, ws3, ws2):
    t = h2.shape[0]
    tile = min(MOE_TILE, t)
    tok = lambda i, e: (i, 0)
    exp = lambda i, e: (e, 0, 0)
    return pl.pallas_call(
        _moe_kernel,
        out_shape=jax.ShapeDtypeStruct((t, D_MODEL), F32),
        grid=(t // tile, N_EXPERTS),
        in_specs=[pl.BlockSpec((tile, D_MODEL), tok),
                  pl.BlockSpec((tile, LANES), tok),
                  pl.BlockSpec((tile, D_MODEL), tok),
                  pl.BlockSpec((1,) + gt2.shape[1:], lambda i, e: (i // rows_per_gt2, 0, 0)),
                  _full((1, D_MODEL)),
                  pl.BlockSpec((1, D_MODEL, D_EXPERT), exp),
                  pl.BlockSpec((1, D_MODEL, D_EXPERT), exp),
                  pl.BlockSpec((1, D_EXPERT, D_MODEL), exp),
                  _full(ws1.shape), _full(ws3.shape), _full(ws2.shape)],
        out_specs=pl.BlockSpec((tile, D_MODEL), tok),
        scratch_shapes=[pltpu.VMEM((tile, D_MODEL), F32)],
        compiler_params=pltpu.CompilerParams(dimension_semantics=("parallel", "arbitrary"),
                                             vmem_limit_bytes=VMEM_LIMIT),
        name="moe",
    )(h2, wd, x1, gt2, gpost, w1, w3, w2, ws1, ws3, ws2)


def kernel(x_prompt, x_sample, c_prompt, c_sample, state_pool, state_hgrn, w_ada, b_ada, g_pre_mix, g_post_mix,
           w_in, w_pool, pool_scale, lb_logits, g_out_norm, w_out, g_pre_ffn, g_post_ffn, w_router, router_bias,
           w_exp_gate, w_exp_up, w_exp_down, w_sh_gate, w_sh_up, w_sh_down):
    assert w_ada.shape[0] == 1 and lb_logits.shape[0] == 2, "single-layer trunk"
    bp, lp, d = x_prompt.shape
    bs = x_sample.shape[0]
    row = lambda a: a[0].reshape(1, -1)

    n_mod = bp + bs
    pad = (-n_mod) % 16
    c_all = jnp.concatenate([c_prompt, c_sample, jnp.zeros((pad, d), F32)], axis=0)
    mod = _ada(c_all, w_ada[0], b_ada)
    mod_p = mod[:bp].reshape(bp, 1, 6 * d)
    mod_s = mod[bp:n_mod]

    win = w_in[0].astype(BF16)
    wout = w_out[0].astype(BF16)
    wpool = w_pool[0].astype(BF16)
    wr = jnp.pad(w_router[0], ((0, 0), (0, LANES - N_EXPERTS)))
    wrh = wr.astype(BF16)
    wrl = (wr - wrh.astype(F32)).astype(BF16)
    m2 = jnp.asarray(_M2_NP, BF16)
    gpre, gpost, gffn, gffn_post = row(g_pre_mix), row(g_post_mix), row(g_pre_ffn), row(g_post_ffn)
    pscale, gout = row(pool_scale), row(g_out_norm)
    bias_col = jnp.pad(router_bias[0], (0, LANES - N_EXPERTS)).reshape(LANES, 1)

    x1_p, h2_p, lg_p, pool_p, st_p = _mix_prompt(x_prompt, mod_p, gpre, gpost, gffn, win, wpool, pscale,
                                                  lb_logits, gout, wout, wrh, wrl, m2)
    x1_s, h2_s, lg_s, pool_s, st_s = _mix_sample(x_sample[:, 0, :], mod_s, state_pool[0], state_hgrn[0],
                                                  gpre, gpost, gffn, win, wpool, pscale, lb_logits, gout, wout,
                                                  wrh, wrl)

    experts = (w_exp_gate[0], w_exp_up[0], w_exp_down[0], w_sh_gate[0], w_sh_up[0], w_sh_down[0])
    tp = bp * lp
    nw = tp // PLAN_WINDOW
    assert lp % PLAN_WINDOW == 0
    upper = jnp.asarray(np.triu(np.ones((PLAN_WINDOW, PLAN_WINDOW), np.float32), 1), BF16)
    dest, wsel, cnt, loff = _plan(lg_p.reshape(tp, LANES), bias_col, upper)
    per_window = lambda a: a.T.reshape(nw, 1, PLAN_WINDOW * TOP_K)
    dest_w, wsel_w = per_window(dest), per_window(wsel)
    h2_rows = h2_p.reshape(tp, ROW_WORDS)
    x_slots = _dispatch(dest_w, h2_rows)
    y_slots = _expert_ffn(*_work_items(cnt[:, :, 0], loff[:, :, 0]), x_slots, *experts[:3])
    ws13 = jnp.concatenate([experts[3], experts[4]], axis=1).astype(BF16)
    gt2_p = mod_p[:, :, 5 * d:]
    y_p = _combine(dest_w, wsel_w, y_slots, h2_rows, x1_p.reshape(tp, d), gt2_p, lp, gffn_post,
                   ws13, experts[5].astype(BF16))
    wd_s = _router(lg_s, bias_col)
    gt2_s = mod_s[:, 5 * d:].reshape(1, bs, d)
    y_s = _moe(h2_s, wd_s, x1_s, gt2_s, 1, gffn_post, *experts)

    return (y_p.reshape(bp, lp, d), y_s.reshape(bs, 1, d), pool_p[None], st_p[None], pool_s[None], st_s[None])
```

```python
import functools

import numpy as np
import jax
import jax.numpy as jnp
from jax import lax
from jax.experimental import pallas as pl
from jax.experimental.pallas import tpu as pltpu

F32 = jnp.float32
BF16 = jnp.bfloat16

D_MODEL = 1024
POOL_WIDTH = 512
POOL_WINDOWS = (2, 4, 8, 16)
POOL_GROUP = 128
POOL_HIST = 15
HG_WIDTH = 512
HG_HEADS = 4
HG_HEAD_DIM = 128
IN_WIDTH = POOL_WIDTH + 4 * HG_WIDTH
N_EXPERTS = 64
TOP_K = 8
N_GROUPS = 8
TOPK_GROUPS = 4
GROUP_SIZE = N_EXPERTS // N_GROUPS
D_EXPERT = 256
ROUTED_SCALE = 2.5
EPS = 1e-6

LANES = 128
CHUNK = 64
TIME_TILE = 256
ADA_TILE_N = 512
ROUTER_TILE = 512
SAMPLE_STATE_BLOCK = 8
PLAN_WINDOW = 1024
RUN_ALIGN = 32
FFN_TILE = 160
REGION_ROWS = PLAN_WINDOW * TOP_K + N_EXPERTS * (RUN_ALIGN - 1) + FFN_TILE
ROW_WORDS = D_MODEL // 2
ROW_SUB = ROW_WORDS // LANES
COMBINE_TILE = 256
SCATTER_UNROLL = 4
COMBINE_UNROLL = 8
FFN_GROUP = 4
FFN_RING = 6
VMEM_LIMIT = 58 * 1024 * 1024

_LEVELS = (64, 32, 16, 8, 4, 2)
_BLK_CUM = 0
_BLK_END = 1


def _level_blocks():
    c = CHUNK
    i = np.arange(c)[:, None]
    s = np.arange(c)[None, :]
    blocks = [(s <= i), (s > i)]
    index = {}
    for lvl in _LEVELS:
        seg = (i // lvl) * lvl
        mid = seg + lvl // 2
        ref = mid - 1
        index[lvl] = len(blocks)
        blocks.append((i >= mid) & (s > ref) & (s <= i))
        if lvl > 2:
            blocks.append((i < mid) & (s > i) & (s <= ref))
    m = np.concatenate(blocks, axis=0).astype(np.float32)
    return np.concatenate([m, m], axis=1), index


_M2_NP, _LEVEL_INDEX = _level_blocks()


def _dot(a, b):
    return jnp.dot(a, b, preferred_element_type=F32)


def _dot_nt(a, b):
    return lax.dot_general(a, b, (((1,), (1,)), ((), ())), preferred_element_type=F32)


def _split2(x):
    hi = x.astype(BF16)
    lo = (x - hi.astype(F32)).astype(BF16)
    return hi, lo


def _pack_pair(a, b):
    lo = lax.bitcast_convert_type(a.astype(BF16).astype(F32), jnp.uint32)
    hi = lax.bitcast_convert_type(b.astype(BF16).astype(F32), jnp.uint32)
    return (lo >> 16) | hi


def _unpack_pair(words):
    lo = lax.bitcast_convert_type(words << 16, F32)
    hi = lax.bitcast_convert_type(words & jnp.uint32(0xFFFF0000), F32)
    return lo, hi


def _silu(x):
    return x * jax.nn.sigmoid(x)


def _rms(x, g):
    return x * lax.rsqrt(jnp.mean(x * x, axis=-1, keepdims=True) + EPS) * g


def _mods(m):
    return [m[:, j * D_MODEL:(j + 1) * D_MODEL] for j in range(6)]


def _forget_lower_bound(lbl):
    mx = jnp.max(lbl, axis=0, keepdims=True)
    e = jnp.exp(lbl - mx)
    return e[0:1] / jnp.sum(e, axis=0, keepdims=True)


def _router_logits(h2, wrh, wrl):
    hi, lo = _split2(h2)
    return _dot(hi, wrh) + _dot(lo, wrh) + _dot(hi, wrl)


def _ada_kernel(c_ref, w_ref, b_ref, o_ref):
    a_hi, a_lo = _split2(_silu(c_ref[...]))
    w_hi, w_lo = _split2(w_ref[...])
    o_ref[...] = _dot(a_hi, w_hi) + _dot(a_lo, w_hi) + _dot(a_hi, w_lo) + b_ref[...]


def _ada(c_all, w_ada, b_ada):
    rows = c_all.shape[0]
    n = w_ada.shape[1]
    return pl.pallas_call(
        _ada_kernel,
        out_shape=jax.ShapeDtypeStruct((rows, n), F32),
        grid=(n // ADA_TILE_N,),
        in_specs=[pl.BlockSpec((rows, D_MODEL), lambda j: (0, 0)),
                  pl.BlockSpec((D_MODEL, ADA_TILE_N), lambda j: (0, j)),
                  pl.BlockSpec((1, ADA_TILE_N), lambda j: (0, j))],
        out_specs=pl.BlockSpec((rows, ADA_TILE_N), lambda j: (0, j)),
        compiler_params=pltpu.CompilerParams(dimension_semantics=("parallel",),
                                             vmem_limit_bytes=VMEM_LIMIT),
        name="ada",
    )(c_all, w_ada, b_ada)


def _level_masks():
    i = lax.broadcasted_iota(jnp.int32, (CHUNK, CHUNK), 0)
    j = lax.broadcasted_iota(jnp.int32, (CHUNK, CHUNK), 1)
    masks = {}
    for lvl in _LEVELS:
        sh = lvl.bit_length() - 1
        same = (i >> sh) == (j >> sh)
        upper = ((i >> (sh - 1)) & 1) == 1
        lower = ((j >> (sh - 1)) & 1) == 0
        masks[lvl] = same & upper & lower
    return masks, i == j


def _hgrn_chunk_head(q, k, v, e_all, lane0, st, masks, eye):
    def blk(n):
        return e_all[n * CHUNK:(n + 1) * CHUNK, lane0:lane0 + HG_HEAD_DIM]

    b = blk(_BLK_CUM)
    a = jnp.where(eye, _dot_nt(q.astype(BF16), k.astype(BF16)), 0.0)
    for lvl in _LEVELS:
        n = _LEVEL_INDEX[lvl]
        ql = (q * jnp.exp(blk(n))).astype(BF16)
        kl = (k * jnp.exp(blk(n + 1))).astype(BF16) if lvl > 2 else k.astype(BF16)
        a = a + jnp.where(masks[lvl], _dot_nt(ql, kl), 0.0)
    st_b = st.astype(BF16)
    o = _dot(a.astype(BF16), v.astype(BF16)) + _dot_nt((q * jnp.exp(b)).astype(BF16), st_b)
    k_end = (k * jnp.exp(blk(_BLK_END))).astype(BF16)
    decay = jnp.exp(b[CHUNK - 1:CHUNK, :])
    st_new = st * decay + _dot(v.T.astype(BF16), k_end)
    return o, st_new


def _mix_prompt_kernel(x_ref, mod_ref, gpre_ref, gpost_ref, gffn_ref, win_ref, wpool_ref, pscale_ref,
                       lbl_ref, gout_ref, wout_ref, wrh_ref, wrl_ref, m2_ref,
                       x1_ref, h2_ref, lg_ref, pool_ref, st_ref,
                       st_s, ubuf, q_s, k_s, v_s, g_s, o_s):
    t = pl.program_id(1)
    n_t = pl.num_programs(1)
    tt = TIME_TILE

    @pl.when(t == 0)
    def _():
        st_s[...] = jnp.zeros_like(st_s)
        ubuf[0:16, :] = jnp.zeros((16, POOL_WIDTH), F32)

    xt = x_ref[0]
    sh1, sc1, gt1, sh2, sc2, gt2 = _mods(mod_ref[0])
    h = _rms(xt, gpre_ref[...]) * (1.0 + sc1) + sh1
    proj = _dot(h.astype(BF16), win_ref[...])

    u = proj[:, :POOL_WIDTH]
    ubuf[16:16 + tt, :] = u
    pos = (t * tt + lax.broadcasted_iota(jnp.int32, (tt, 1), 0) + 1).astype(F32)
    ys = []
    for g, w in enumerate(POOL_WINDOWS):
        s = ubuf[:, g * POOL_GROUP:(g + 1) * POOL_GROUP]
        off = 0
        for step in range(w.bit_length() - 1):
            sh = 1 << step
            s = s[sh:, :] + s[:-sh, :]
            off += sh
        ws = s[16 - off:16 - off + tt, :]
        cnt = jnp.minimum(pos, float(w))
        d = ws / cnt - u[:, g * POOL_GROUP:(g + 1) * POOL_GROUP]
        ys.append(_dot(d.astype(BF16), wpool_ref[g]))
    y_pool = jnp.concatenate(ys, axis=1) * pscale_ref[...]
    ubuf[0:16, :] = ubuf[tt:tt + 16, :]

    @pl.when(t == n_t - 1)
    def _():
        pool_ref[0] = u[tt - POOL_HIST:, :]

    lb = _forget_lower_bound(lbl_ref[...])
    f = lb + (1.0 - lb) * jax.nn.sigmoid(proj[:, POOL_WIDTH + HG_WIDTH:POOL_WIDTH + 2 * HG_WIDTH])
    q_s[...] = _silu(proj[:, POOL_WIDTH:POOL_WIDTH + HG_WIDTH])
    k_s[...] = 1.0 - f
    v_s[...] = proj[:, POOL_WIDTH + 2 * HG_WIDTH:POOL_WIDTH + 3 * HG_WIDTH]
    g_s[...] = jnp.log(f)
    gate = _silu(proj[:, POOL_WIDTH + 3 * HG_WIDTH:])

    masks, eye = _level_masks()
    m2 = m2_ref[...]

    states = [st_s[hd] for hd in range(HG_HEADS)]
    for c in range(tt // CHUNK):
        rows = slice(c * CHUNK, (c + 1) * CHUNK)
        g_hi, g_lo = _split2(g_s[rows, :])
        e_all = _dot(m2, jnp.concatenate([g_hi, g_lo], axis=0))
        for hd in range(HG_HEADS):
            lane0 = hd * HG_HEAD_DIM
            lanes = slice(lane0, lane0 + HG_HEAD_DIM)
            o, states[hd] = _hgrn_chunk_head(q_s[rows, lanes], k_s[rows, lanes], v_s[rows, lanes], e_all, lane0,
                                             states[hd], masks, eye)
            o_s[rows, lanes] = o
    for hd in range(HG_HEADS):
        st_s[hd] = states[hd]

    o = o_s[...]
    os_ = []
    for hd in range(HG_HEADS):
        oh = o[:, hd * HG_HEAD_DIM:(hd + 1) * HG_HEAD_DIM]
        os_.append(oh * lax.rsqrt(jnp.mean(oh * oh, axis=-1, keepdims=True) + EPS) * gout_ref[...])
    o_n = jnp.concatenate(os_, axis=1) * gate

    mix = _dot(y_pool.astype(BF16), wout_ref[0:POOL_WIDTH, :]) + _dot(o_n.astype(BF16), wout_ref[POOL_WIDTH:, :])
    x1 = xt + gt1 * _rms(mix, gpost_ref[...])
    h2 = _rms(x1, gffn_ref[...]) * (1.0 + sc2) + sh2
    x1_ref[0] = x1
    h2_ref[0] = _pack_pair(h2[:, :ROW_WORDS], h2[:, ROW_WORDS:])
    lg_ref[0] = _router_logits(h2, wrh_ref[...], wrl_ref[...])

    @pl.when(t == n_t - 1)
    def _():
        for hd in range(HG_HEADS):
            st_ref[0, hd] = st_s[hd].T


def _full(shape):
    nd = len(shape)
    return pl.BlockSpec(shape, lambda *_: (0,) * nd)


def _mix_prompt(x, mod, gpre, gpost, gffn, win, wpool, pscale, lbl, gout, wout, wrh, wrl, m2):
    b, l, d = x.shape
    tt = TIME_TILE
    tile = lambda i, j: (i, j, 0)
    per_b = lambda i, j: (i, 0, 0)
    return pl.pallas_call(
        _mix_prompt_kernel,
        out_shape=(jax.ShapeDtypeStruct((b, l, d), F32),
                   jax.ShapeDtypeStruct((b, l, ROW_WORDS), jnp.uint32),
                   jax.ShapeDtypeStruct((b, l, LANES), F32),
                   jax.ShapeDtypeStruct((b, POOL_HIST, POOL_WIDTH), F32),
                   jax.ShapeDtypeStruct((b, HG_HEADS, HG_HEAD_DIM, HG_HEAD_DIM), F32)),
        grid=(b, l // tt),
        in_specs=[pl.BlockSpec((1, tt, d), tile),
                  pl.BlockSpec((1, 1, 6 * d), per_b),
                  _full((1, d)), _full((1, d)), _full((1, d)),
                  _full(win.shape), _full(wpool.shape), _full((1, POOL_WIDTH)),
                  _full(lbl.shape), _full((1, HG_HEAD_DIM)), _full(wout.shape),
                  _full(wrh.shape), _full(wrl.shape), _full(m2.shape)],
        out_specs=(pl.BlockSpec((1, tt, d), tile),
                   pl.BlockSpec((1, tt, ROW_WORDS), tile),
                   pl.BlockSpec((1, tt, LANES), tile),
                   pl.BlockSpec((1, POOL_HIST, POOL_WIDTH), per_b),
                   pl.BlockSpec((1, HG_HEADS, HG_HEAD_DIM, HG_HEAD_DIM), lambda i, j: (i, 0, 0, 0))),
        scratch_shapes=[pltpu.VMEM((HG_HEADS, HG_HEAD_DIM, HG_HEAD_DIM), F32),
                        pltpu.VMEM((tt + 16, POOL_WIDTH), F32),
                        pltpu.VMEM((tt, HG_WIDTH), F32), pltpu.VMEM((tt, HG_WIDTH), F32),
                        pltpu.VMEM((tt, HG_WIDTH), F32), pltpu.VMEM((tt, HG_WIDTH), F32),
                        pltpu.VMEM((tt, HG_WIDTH), F32)],
        compiler_params=pltpu.CompilerParams(dimension_semantics=("parallel", "arbitrary"),
                                             vmem_limit_bytes=VMEM_LIMIT),
        name="mix_prompt",
    )(x, mod, gpre, gpost, gffn, win, wpool, pscale, lbl, gout, wout, wrh, wrl, m2)


def _mix_sample_in_kernel(x_ref, mod_ref, gpre_ref, win_ref, wpool_ref, pscale_ref, lbl_ref, hist_ref,
                          ypool_ref, npool_ref, ft_ref, qt_ref, v_ref, gate_ref):
    xt = x_ref[...]
    sh1, sc1 = _mods(mod_ref[...])[:2]
    h = _rms(xt, gpre_ref[...]) * (1.0 + sc1) + sh1
    proj = _dot(h.astype(BF16), win_ref[...])
    u = proj[:, :POOL_WIDTH]
    row = lax.broadcasted_iota(jnp.int32, (hist_ref.shape[0], POOL_HIST, POOL_GROUP), 1)
    ys = []
    for g, w in enumerate(POOL_WINDOWS):
        sl = slice(g * POOL_GROUP, (g + 1) * POOL_GROUP)
        past = jnp.sum(jnp.where(row >= POOL_HIST - (w - 1), hist_ref[:, :, sl], 0.0), axis=1)
        ug = u[:, sl]
        d = (past + ug) / float(w) - ug
        ys.append(_dot(d.astype(BF16), wpool_ref[g]))
    ypool_ref[...] = jnp.concatenate(ys, axis=1) * pscale_ref[...]
    npool_ref[:, 0:POOL_HIST - 1, :] = hist_ref[:, 1:POOL_HIST, :]
    npool_ref[:, POOL_HIST - 1, :] = u

    lb = _forget_lower_bound(lbl_ref[...])
    f = lb + (1.0 - lb) * jax.nn.sigmoid(proj[:, POOL_WIDTH + HG_WIDTH:POOL_WIDTH + 2 * HG_WIDTH])
    ft_ref[...] = f.T
    qt_ref[...] = _silu(proj[:, POOL_WIDTH:POOL_WIDTH + HG_WIDTH]).T
    v_ref[...] = proj[:, POOL_WIDTH + 2 * HG_WIDTH:POOL_WIDTH + 3 * HG_WIDTH]
    gate_ref[...] = _silu(proj[:, POOL_WIDTH + 3 * HG_WIDTH:])


def _mix_sample_state_kernel(s_ref, ft_ref, qt_ref, v_ref, snew_ref, o_ref):
    i = pl.program_id(0)
    lane = lax.broadcasted_iota(jnp.int32, (HG_HEAD_DIM, ft_ref.shape[1]), 1)
    for j in range(SAMPLE_STATE_BLOCK):
        mine = lane == i * SAMPLE_STATE_BLOCK + j
        for hd in range(HG_HEADS):
            r0 = hd * HG_HEAD_DIM
            f = jnp.sum(jnp.where(mine, ft_ref[r0:r0 + HG_HEAD_DIM, :], 0.0), axis=1, keepdims=True)
            q = jnp.sum(jnp.where(mine, qt_ref[r0:r0 + HG_HEAD_DIM, :], 0.0), axis=1, keepdims=True)
            v = v_ref[j:j + 1, r0:r0 + HG_HEAD_DIM]
            s_new = f * s_ref[j, hd] + (1.0 - f) * v
            snew_ref[j, hd] = s_new
            o_ref[j:j + 1, r0:r0 + HG_HEAD_DIM] = jnp.sum(q * s_new, axis=0, keepdims=True)


def _mix_sample_out_kernel(x_ref, mod_ref, o_ref, gate_ref, ypool_ref, gout_ref, wout_ref, gpost_ref,
                           gffn_ref, wrh_ref, wrl_ref, x1_ref, h2_ref, lg_ref):
    _, _, gt1, sh2, sc2, _ = _mods(mod_ref[...])
    o = o_ref[...]
    os_ = []
    for hd in range(HG_HEADS):
        oh = o[:, hd * HG_HEAD_DIM:(hd + 1) * HG_HEAD_DIM]
        os_.append(oh * lax.rsqrt(jnp.mean(oh * oh, axis=-1, keepdims=True) + EPS) * gout_ref[...])
    o_n = jnp.concatenate(os_, axis=1) * gate_ref[...]
    mix = (_dot(ypool_ref[...].astype(BF16), wout_ref[0:POOL_WIDTH, :])
           + _dot(o_n.astype(BF16), wout_ref[POOL_WIDTH:, :]))
    x1 = x_ref[...] + gt1 * _rms(mix, gpost_ref[...])
    h2 = _rms(x1, gffn_ref[...]) * (1.0 + sc2) + sh2
    x1_ref[...] = x1
    h2_ref[...] = h2.astype(BF16)
    lg_ref[...] = _router_logits(h2, wrh_ref[...], wrl_ref[...])


def _mix_sample(x, mod, hist, state, gpre, gpost, gffn, win, wpool, pscale, lbl, gout, wout, wrh, wrl):
    b = x.shape[0]
    bb = SAMPLE_STATE_BLOCK
    cp = pltpu.CompilerParams(vmem_limit_bytes=VMEM_LIMIT)
    ypool, npool, ft, qt, v, gate = pl.pallas_call(
        _mix_sample_in_kernel,
        out_shape=(jax.ShapeDtypeStruct((b, POOL_WIDTH), F32),
                   jax.ShapeDtypeStruct((b, POOL_HIST, POOL_WIDTH), F32),
                   jax.ShapeDtypeStruct((HG_WIDTH, b), F32),
                   jax.ShapeDtypeStruct((HG_WIDTH, b), F32),
                   jax.ShapeDtypeStruct((b, HG_WIDTH), F32),
                   jax.ShapeDtypeStruct((b, HG_WIDTH), F32)),
        compiler_params=cp,
        name="mix_sample_in",
    )(x, mod, gpre, win, wpool, pscale, lbl, hist)

    s_spec = pl.BlockSpec((bb, HG_HEADS, HG_HEAD_DIM, HG_HEAD_DIM), lambda i: (i, 0, 0, 0))
    col_spec = _full((HG_WIDTH, b))
    row_spec = pl.BlockSpec((bb, HG_WIDTH), lambda i: (i, 0))
    s_new, o = pl.pallas_call(
        _mix_sample_state_kernel,
        out_shape=(jax.ShapeDtypeStruct(state.shape, F32), jax.ShapeDtypeStruct((b, HG_WIDTH), F32)),
        grid=(b // bb,),
        in_specs=[s_spec, col_spec, col_spec, row_spec],
        out_specs=(s_spec, row_spec),
        compiler_params=pltpu.CompilerParams(dimension_semantics=("parallel",), vmem_limit_bytes=VMEM_LIMIT),
        name="mix_sample_state",
    )(state, ft, qt, v)

    x1, h2, lg = pl.pallas_call(
        _mix_sample_out_kernel,
        out_shape=(jax.ShapeDtypeStruct((b, D_MODEL), F32),
                   jax.ShapeDtypeStruct((b, D_MODEL), BF16),
                   jax.ShapeDtypeStruct((b, LANES), F32)),
        compiler_params=cp,
        name="mix_sample_out",
    )(x, mod, o, gate, ypool, gout, wout, gpost, gffn, wrh, wrl)
    return x1, h2, lg, npool, s_new


def _first_max(cur, idx, big):
    m = jnp.max(cur, axis=0, keepdims=True)
    first = jnp.min(jnp.where(cur == m, idx, big), axis=0, keepdims=True)
    return idx == first


def _route(logit, bias):
    n = logit.shape[1]
    scores = jax.nn.sigmoid(logit)
    sel = scores + bias
    neg = -jnp.inf

    sub = lax.broadcasted_iota(jnp.int32, (GROUP_SIZE, n), 0)
    gscore = []
    for g in range(N_GROUPS):
        sg = sel[g * GROUP_SIZE:(g + 1) * GROUP_SIZE, :]
        m1 = jnp.max(sg, axis=0, keepdims=True)
        rest = jnp.where(_first_max(sg, sub, GROUP_SIZE), neg, sg)
        gscore.append(m1 + jnp.max(rest, axis=0, keepdims=True))
    cur = jnp.concatenate(gscore, axis=0)
    gidx = lax.broadcasted_iota(jnp.int32, (N_GROUPS, n), 0)
    gmask = jnp.zeros((N_GROUPS, n), jnp.bool_)
    for _ in range(TOPK_GROUPS):
        pick = _first_max(cur, gidx, N_GROUPS)
        gmask = gmask | pick
        cur = jnp.where(pick, neg, cur)
    emask = jnp.concatenate(
        [jnp.broadcast_to(gmask[g:g + 1, :], (GROUP_SIZE, n)) for g in range(N_GROUPS)], axis=0)

    cur = jnp.where(emask, sel, neg)
    eidx = lax.broadcasted_iota(jnp.int32, (N_EXPERTS, n), 0)
    picks = []
    for _ in range(TOP_K):
        pick = _first_max(cur, eidx, N_EXPERTS)
        picks.append(pick)
        cur = jnp.where(pick, neg, cur)
    return scores, picks


def _any(masks):
    return functools.reduce(jnp.logical_or, masks)


def _router_kernel(lg_ref, bias_ref, wd_ref):
    n = lg_ref.shape[0]
    scores, picks = _route(lg_ref[...].T[0:N_EXPERTS, :], bias_ref[0:N_EXPERTS, :])
    chosen = _any(picks)
    wsum = jnp.sum(jnp.where(chosen, scores, 0.0), axis=0, keepdims=True)
    wd = jnp.where(chosen, scores / wsum * ROUTED_SCALE, 0.0)
    wd_ref[...] = jnp.concatenate([wd, jnp.zeros((LANES - N_EXPERTS, n), F32)], axis=0).T


def _router(logits, bias_col):
    t = logits.shape[0]
    tile = min(ROUTER_TILE, t)
    return pl.pallas_call(
        _router_kernel,
        out_shape=jax.ShapeDtypeStruct((t, LANES), F32),
        grid=(t // tile,),
        in_specs=[pl.BlockSpec((tile, LANES), lambda i: (i, 0)), _full(bias_col.shape)],
        out_specs=pl.BlockSpec((tile, LANES), lambda i: (i, 0)),
        compiler_params=pltpu.CompilerParams(dimension_semantics=("parallel",), vmem_limit_bytes=VMEM_LIMIT),
        name="router",
    )(logits, bias_col)


def _plan_kernel(lg_ref, bias_ref, upper_ref, dest_ref, wsel_ref, cnt_ref, loff_ref):
    scores, picks = _route(lg_ref[...].T[0:N_EXPERTS, :], bias_ref[0:N_EXPERTS, :])
    chosen = _any(picks)
    chosen_f = jnp.where(chosen, 1.0, 0.0)
    cnt = jnp.sum(chosen_f, axis=1, keepdims=True)
    units = jnp.floor((cnt + (RUN_ALIGN - 1)) / RUN_ALIGN)
    ei = lax.broadcasted_iota(jnp.int32, (N_EXPERTS, N_EXPERTS), 0)
    ej = lax.broadcasted_iota(jnp.int32, (N_EXPERTS, N_EXPERTS), 1)
    before = jnp.where(ej < ei, 1.0, 0.0).astype(BF16)
    loff = RUN_ALIGN * _dot(before, jnp.broadcast_to(units, (N_EXPERTS, LANES)).astype(BF16))[:, 0:1]
    slot = loff + _dot(chosen_f.astype(BF16), upper_ref[...])
    wsum = jnp.sum(jnp.where(chosen, scores, 0.0), axis=0, keepdims=True)
    pick_sum = lambda v: jnp.concatenate(
        [jnp.sum(jnp.where(p, v, 0.0), axis=0, keepdims=True) for p in picks], axis=0)
    dest_ref[...] = (pick_sum(slot) * ROW_SUB).astype(jnp.int32)
    wsel_ref[...] = pick_sum(scores) / wsum * ROUTED_SCALE
    cnt_ref[0] = jnp.broadcast_to(cnt, (N_EXPERTS, LANES))
    loff_ref[0] = jnp.broadcast_to(loff, (N_EXPERTS, LANES))


def _plan(logits, bias_col, upper):
    t = logits.shape[0]
    nw = t // PLAN_WINDOW
    per_w = pl.BlockSpec((1, N_EXPERTS, LANES), lambda w: (w, 0, 0))
    picks = pl.BlockSpec((TOP_K, PLAN_WINDOW), lambda w: (0, w))
    return pl.pallas_call(
        _plan_kernel,
        out_shape=(jax.ShapeDtypeStruct((TOP_K, t), jnp.int32), jax.ShapeDtypeStruct((TOP_K, t), F32),
                   jax.ShapeDtypeStruct((nw, N_EXPERTS, LANES), F32),
                   jax.ShapeDtypeStruct((nw, N_EXPERTS, LANES), F32)),
        grid=(nw,),
        in_specs=[pl.BlockSpec((PLAN_WINDOW, LANES), lambda w: (w, 0)), _full(bias_col.shape), _full(upper.shape)],
        out_specs=(picks, picks, per_w, per_w),
        compiler_params=pltpu.CompilerParams(dimension_semantics=("parallel",), vmem_limit_bytes=VMEM_LIMIT),
        name="plan",
    )(logits, bias_col, upper)


def _slot_rows(slot):
    return pl.ds(pl.multiple_of(slot * ROW_SUB, ROW_SUB), ROW_SUB)


def _rows_at(offset):
    return pl.ds(pl.multiple_of(offset, ROW_SUB), ROW_SUB)


def _dispatch_kernel(dest_ref, h_ref, x_ref, dest_s, rows_s, sem):
    plan_copy = pltpu.make_async_copy(dest_ref.at[0, 0], dest_s, sem)
    plan_copy.start()
    x_ref[...] = jnp.zeros_like(x_ref)
    for j in range(ROW_SUB):
        rows_s[pl.ds(j, PLAN_WINDOW, stride=ROW_SUB), :] = h_ref[:, j * LANES:(j + 1) * LANES]
    plan_copy.wait()

    def body(i, carry):
        for u in range(SCATTER_UNROLL):
            t = i * SCATTER_UNROLL + u
            row = rows_s[_slot_rows(t), :]
            for k in range(TOP_K):
                x_ref[_rows_at(dest_s[t * TOP_K + k]), :] = row
        return carry

    lax.fori_loop(0, PLAN_WINDOW // SCATTER_UNROLL, body, 0)


def _dispatch(dest_w, h2):
    nw = dest_w.shape[0]
    region = REGION_ROWS * ROW_SUB
    return pl.pallas_call(
        _dispatch_kernel,
        out_shape=jax.ShapeDtypeStruct((nw * region, LANES), jnp.uint32),
        grid=(nw,),
        in_specs=[pl.BlockSpec((1, 1, TOP_K * PLAN_WINDOW), lambda w: (w, 0, 0)),
                  pl.BlockSpec((PLAN_WINDOW, ROW_WORDS), lambda w: (w, 0))],
        out_specs=pl.BlockSpec((region, LANES), lambda w: (w, 0)),
        scratch_shapes=[pltpu.SMEM((TOP_K * PLAN_WINDOW,), jnp.int32),
                        pltpu.VMEM((PLAN_WINDOW * ROW_SUB, LANES), jnp.uint32),
                        pltpu.SemaphoreType.DMA],
        compiler_params=pltpu.CompilerParams(dimension_semantics=("arbitrary",), vmem_limit_bytes=VMEM_LIMIT),
        name="dispatch",
    )(dest_w, h2)


def _unpack_rows(ref, n):
    parts = [_unpack_pair(ref[pl.ds(j, n, stride=ROW_SUB), :]) for j in range(ROW_SUB)]
    return jnp.concatenate([p[0] for p in parts] + [p[1] for p in parts], axis=1)


def _ffn_kernel(gq_ref, gs_ref, to_ref, tn_ref, x_hbm, w1_ref, w3_ref, w2_ref,
                hs_ref, wds_ref, x1s_ref, gt2s_ref, gpost_ref, ws1_ref, ws3_ref, ws2_ref, y_hbm, ys_ref,
                w13_s, w2_s, xbuf, ybuf, rsem, wsem, acc_s):
    del x_hbm
    e = pl.program_id(0)
    n_groups = gq_ref[N_EXPERTS]
    block = RUN_ALIGN * ROW_SUB
    tile = FFN_TILE * ROW_SUB

    def go(cp, start):
        if start:
            cp.start()
        else:
            cp.wait()

    def fetch(q, start):
        def one(j):
            src = y_hbm.at[pl.ds(pl.multiple_of(to_ref[gs_ref[q] + j] * block, block), tile)]
            go(pltpu.make_async_copy(src, xbuf.at[q % FFN_RING, pl.ds(j * tile, tile)], rsem.at[q % FFN_RING]), start)

        size = gs_ref[q + 1] - gs_ref[q]

        @pl.when(size == FFN_GROUP)
        def _():
            for j in range(FFN_GROUP):
                one(j)

        @pl.when(size < FFN_GROUP)
        def _():
            for j in range(FFN_GROUP - 1):
                pl.when(j < size)(functools.partial(one, j))

    def writeback(q, start):
        most = FFN_TILE // RUN_ALIGN - 1

        def blocks(j, b, n):
            dst = pl.multiple_of(to_ref[gs_ref[q] + j] * block, block) + b * block
            go(pltpu.make_async_copy(ybuf.at[q % FFN_RING, pl.ds(j * tile + b * block, n * block)],
                                     y_hbm.at[pl.ds(dst, n * block)], wsem.at[q % FFN_RING]), start)

        for j in range(FFN_GROUP):
            item = gs_ref[q] + j
            n = jnp.where(item < gs_ref[q + 1], tn_ref[item], 0)
            pl.when(n >= most)(functools.partial(blocks, j, 0, most))
            pl.when(n > most)(functools.partial(blocks, j, most, 1))

            @pl.when((n > 0) & (n < most))
            def _():
                for b in range(most - 1):
                    pl.when(b < n)(functools.partial(blocks, j, b, 1))

    @pl.when(e == 0)
    def _():
        xbuf[...] = jnp.zeros_like(xbuf)
        for d in range(FFN_RING - 1):
            pl.when(d < n_groups)(functools.partial(fetch, d, True))

    w13_s[:, 0:D_EXPERT] = w1_ref[0].astype(BF16)
    w13_s[:, D_EXPERT:] = w3_ref[0].astype(BF16)
    w2_s[...] = w2_ref[0].astype(BF16)

    xs = hs_ref[...]

    @pl.when(e == 0)
    def _():
        sh = _silu(_dot(xs, ws1_ref[...].astype(BF16))) * _dot(xs, ws3_ref[...].astype(BF16))
        acc_s[...] = _dot(sh.astype(BF16), ws2_ref[...].astype(BF16))

    lane = lax.broadcasted_iota(jnp.int32, wds_ref.shape, 1)
    wcol = jnp.sum(jnp.where(lane == e, wds_ref[...], 0.0), axis=1, keepdims=True)
    hd = _dot(xs, w13_s[...])
    acc_s[...] += _dot((_silu(hd[:, :D_EXPERT]) * hd[:, D_EXPERT:] * wcol).astype(BF16), w2_s[...])

    @pl.when(e == pl.num_programs(0) - 1)
    def _():
        ys_ref[...] = x1s_ref[...] + gt2s_ref[...] * _rms(acc_s[...], gpost_ref[...])

    def group_body(q, carry):
        @pl.when(q + FFN_RING - 1 < n_groups)
        def _():
            fetch(q + FFN_RING - 1, True)

        @pl.when(q >= FFN_RING)
        def _():
            writeback(q - FFN_RING, False)

        fetch(q, False)
        x = _unpack_rows(xbuf.at[q % FFN_RING], FFN_GROUP * FFN_TILE).astype(BF16)
        h = _dot(x, w13_s[...])
        act = _silu(h[:, :D_EXPERT]) * h[:, D_EXPERT:]
        y = _dot(act.astype(BF16), w2_s[...])
        out = ybuf.at[q % FFN_RING]
        for j in range(ROW_SUB):
            out[pl.ds(j, FFN_GROUP * FFN_TILE, stride=ROW_SUB), :] = _pack_pair(
                y[:, j * LANES:(j + 1) * LANES], y[:, ROW_WORDS + j * LANES:ROW_WORDS + (j + 1) * LANES])
        writeback(q, True)
        return carry

    lax.fori_loop(gq_ref[e], gq_ref[e + 1], group_body, 0)

    @pl.when(e == pl.num_programs(0) - 1)
    def _():
        for d in range(FFN_RING):
            @pl.when(n_groups - 1 - d >= 0)
            def _():
                writeback(n_groups - 1 - d, False)


def _expert_ffn(group_of_expert, group_start, item_off, item_blocks, x_slots, w1, w3, w2,
                h2_s, wd_s, x1_s, gt2_s, gpost, ws1, ws3, ws2):
    of_expert = lambda e, *_: (e, 0, 0)
    whole = lambda a: pl.BlockSpec(a.shape, lambda e, *_: (0,) * a.ndim)
    tiles = pltpu.VMEM((FFN_RING, FFN_GROUP * FFN_TILE * ROW_SUB, LANES), jnp.uint32)
    sems = pltpu.SemaphoreType.DMA((FFN_RING,))
    dense = (h2_s, wd_s, x1_s, gt2_s, gpost, ws1, ws3, ws2)
    grid_spec = pltpu.PrefetchScalarGridSpec(
        num_scalar_prefetch=4,
        grid=(N_EXPERTS,),
        in_specs=[pl.BlockSpec(memory_space=pl.ANY),
                  pl.BlockSpec((1, D_MODEL, D_EXPERT), of_expert),
                  pl.BlockSpec((1, D_MODEL, D_EXPERT), of_expert),
                  pl.BlockSpec((1, D_EXPERT, D_MODEL), of_expert)] + [whole(a) for a in dense],
        out_specs=(pl.BlockSpec(memory_space=pl.ANY), whole(x1_s)),
        scratch_shapes=[pltpu.VMEM((D_MODEL, 2 * D_EXPERT), BF16), pltpu.VMEM((D_EXPERT, D_MODEL), BF16),
                        tiles, tiles, sems, sems, pltpu.VMEM(x1_s.shape, F32)])
    return pl.pallas_call(
        _ffn_kernel,
        out_shape=(jax.ShapeDtypeStruct(x_slots.shape, jnp.uint32), jax.ShapeDtypeStruct(x1_s.shape, F32)),
        grid_spec=grid_spec,
        input_output_aliases={4: 0},
        compiler_params=pltpu.CompilerParams(dimension_semantics=("arbitrary",), vmem_limit_bytes=VMEM_LIMIT),
        name="expert_ffn",
    )(group_of_expert, group_start, item_off, item_blocks, x_slots, w1, w3, w2, *dense)


def _work_items(cnt, loff):
    nw = cnt.shape[0]
    cnt = cnt.astype(jnp.int32).T.reshape(-1)
    loff = loff.astype(jnp.int32).T.reshape(-1)
    region = jnp.tile(jnp.arange(nw, dtype=jnp.int32) * REGION_ROWS, N_EXPERTS)
    n_items = nw * N_EXPERTS + (nw * PLAN_WINDOW * TOP_K) // FFN_TILE
    tiles = (cnt + FFN_TILE - 1) // FFN_TILE
    ends = jnp.cumsum(tiles)
    starts = ends - tiles
    i = jnp.arange(n_items + FFN_GROUP, dtype=jnp.int32)[:, None]
    mine = (starts[None, :] <= i) & (i < ends[None, :])
    of_pair = lambda v: jnp.sum(jnp.where(mine, v[None, :], 0), axis=1)
    r = i[:, 0] - of_pair(starts)
    row = of_pair(region + loff) + r * FFN_TILE
    own = jnp.clip(of_pair(cnt) - r * FFN_TILE, 0, FFN_TILE)

    zero = jnp.zeros((1,), jnp.int32)
    item_end = ends.reshape(N_EXPERTS, nw)[:, -1]
    item_start = jnp.concatenate([zero, item_end[:-1]])
    group_end = jnp.cumsum((item_end - item_start + FFN_GROUP - 1) // FFN_GROUP)
    group_first = jnp.concatenate([zero, group_end[:-1]])
    q = jnp.arange(N_EXPERTS + n_items // FFN_GROUP + 1, dtype=jnp.int32)[:, None]
    has = (group_first[None, :] <= q) & (q < group_end[None, :])
    of_expert = lambda v: jnp.sum(jnp.where(has, v[None, :], 0), axis=1)
    group_start = jnp.where(q[:, 0] < group_end[-1],
                            of_expert(item_start) + (q[:, 0] - of_expert(group_first)) * FFN_GROUP, item_end[-1])
    return (jnp.concatenate([zero, group_end]), group_start, row // RUN_ALIGN, (own + RUN_ALIGN - 1) // RUN_ALIGN)


def _combine_kernel(dest_ref, wsel_ref, y_ref, h_ref, x1_ref, gt2_ref, gpost_ref, ws13_ref, ws2_ref, out_ref,
                    dest_s, wsel_s, lo_s, hi_s, sem):
    s = pl.program_id(1)

    @pl.when(s == 0)
    def _():
        copies = (pltpu.make_async_copy(dest_ref.at[0, 0], dest_s, sem.at[0]),
                  pltpu.make_async_copy(wsel_ref.at[0, 0], wsel_s, sem.at[1]))
        for c in copies:
            c.start()
        for c in copies:
            c.wait()

    base = s * COMBINE_TILE

    def body(i, carry):
        for u in range(COMBINE_UNROLL):
            t = i * COMBINE_UNROLL + u
            acc_lo = jnp.zeros((ROW_SUB, LANES), F32)
            acc_hi = jnp.zeros((ROW_SUB, LANES), F32)
            for k in range(TOP_K):
                at = (base + t) * TOP_K + k
                lo, hi = _unpack_pair(y_ref[_rows_at(dest_s[at]), :])
                w = wsel_s[at]
                acc_lo = acc_lo + w * lo
                acc_hi = acc_hi + w * hi
            lo_s[_slot_rows(t), :] = acc_lo
            hi_s[_slot_rows(t), :] = acc_hi
        return carry

    lax.fori_loop(0, COMBINE_TILE // COMBINE_UNROLL, body, 0)
    rows = lambda ref: [ref[pl.ds(j, COMBINE_TILE, stride=ROW_SUB), :] for j in range(ROW_SUB)]
    routed = jnp.concatenate(rows(lo_s) + rows(hi_s), axis=1)
    h_lo, h_hi = _unpack_pair(h_ref[...])
    x = jnp.concatenate([h_lo, h_hi], axis=1).astype(BF16)
    hs = _dot(x, ws13_ref[...])
    act = _silu(hs[:, :D_EXPERT]) * hs[:, D_EXPERT:]
    ff = routed + _dot(act.astype(BF16), ws2_ref[...])
    out_ref[...] = x1_ref[...] + gt2_ref[0] * _rms(ff, gpost_ref[...])


def _combine(dest_w, wsel_w, y_slots, h2, x1, gt2, tokens_per_gt2, gpost, ws13, ws2):
    nw = dest_w.shape[0]
    sub = PLAN_WINDOW // COMBINE_TILE
    plan = pl.BlockSpec((1, 1, TOP_K * PLAN_WINDOW), lambda w, s: (w, 0, 0))
    tok = lambda w, s: (w * sub + s, 0)
    return pl.pallas_call(
        _combine_kernel,
        out_shape=jax.ShapeDtypeStruct(x1.shape, F32),
        grid=(nw, sub),
        in_specs=[plan, plan,
                  pl.BlockSpec((REGION_ROWS * ROW_SUB, LANES), lambda w, s: (w, 0)),
                  pl.BlockSpec((COMBINE_TILE, ROW_WORDS), tok),
                  pl.BlockSpec((COMBINE_TILE, D_MODEL), tok),
                  pl.BlockSpec((1, 1, D_MODEL), lambda w, s: ((w * PLAN_WINDOW) // tokens_per_gt2, 0, 0)),
                  _full((1, D_MODEL)), _full(ws13.shape), _full(ws2.shape)],
        out_specs=pl.BlockSpec((COMBINE_TILE, D_MODEL), tok),
        scratch_shapes=[pltpu.SMEM((TOP_K * PLAN_WINDOW,), jnp.int32),
                        pltpu.SMEM((TOP_K * PLAN_WINDOW,), F32),
                        pltpu.VMEM((COMBINE_TILE * ROW_SUB, LANES), F32),
                        pltpu.VMEM((COMBINE_TILE * ROW_SUB, LANES), F32),
                        pltpu.SemaphoreType.DMA((2,))],
        compiler_params=pltpu.CompilerParams(dimension_semantics=("arbitrary", "arbitrary"),
                                             vmem_limit_bytes=VMEM_LIMIT),
        name="combine",
    )(dest_w, wsel_w, y_slots, h2, x1, gt2, gpost, ws13, ws2)


def kernel(x_prompt, x_sample, c_prompt, c_sample, state_pool, state_hgrn, w_ada, b_ada, g_pre_mix, g_post_mix,
           w_in, w_pool, pool_scale, lb_logits, g_out_norm, w_out, g_pre_ffn, g_post_ffn, w_router, router_bias,
           w_exp_gate, w_exp_up, w_exp_down, w_sh_gate, w_sh_up, w_sh_down):
    assert w_ada.shape[0] == 1 and lb_logits.shape[0] == 2, "single-layer trunk"
    bp, lp, d = x_prompt.shape
    bs = x_sample.shape[0]
    row = lambda a: a[0].reshape(1, -1)

    n_mod = bp + bs
    pad = (-n_mod) % 16
    c_all = jnp.concatenate([c_prompt, c_sample, jnp.zeros((pad, d), F32)], axis=0)
    mod = _ada(c_all, w_ada[0], b_ada)
    mod_p = mod[:bp].reshape(bp, 1, 6 * d)
    mod_s = mod[bp:n_mod]

    win = w_in[0].astype(BF16)
    wout = w_out[0].astype(BF16)
    wpool = w_pool[0].astype(BF16)
    wr = jnp.pad(w_router[0], ((0, 0), (0, LANES - N_EXPERTS)))
    wrh = wr.astype(BF16)
    wrl = (wr - wrh.astype(F32)).astype(BF16)
    m2 = jnp.asarray(_M2_NP, BF16)
    gpre, gpost, gffn, gffn_post = row(g_pre_mix), row(g_post_mix), row(g_pre_ffn), row(g_post_ffn)
    pscale, gout = row(pool_scale), row(g_out_norm)
    bias_col = jnp.pad(router_bias[0], (0, LANES - N_EXPERTS)).reshape(LANES, 1)

    x1_p, h2_p, lg_p, pool_p, st_p = _mix_prompt(x_prompt, mod_p, gpre, gpost, gffn, win, wpool, pscale,
                                                  lb_logits, gout, wout, wrh, wrl, m2)
    x1_s, h2_s, lg_s, pool_s, st_s = _mix_sample(x_sample[:, 0, :], mod_s, state_pool[0], state_hgrn[0],
                                                  gpre, gpost, gffn, win, wpool, pscale, lb_logits, gout, wout,
                                                  wrh, wrl)

    experts = (w_exp_gate[0], w_exp_up[0], w_exp_down[0], w_sh_gate[0], w_sh_up[0], w_sh_down[0])
    tp = bp * lp
    nw = tp // PLAN_WINDOW
    assert lp % PLAN_WINDOW == 0
    upper = jnp.asarray(np.triu(np.ones((PLAN_WINDOW, PLAN_WINDOW), np.float32), 1), BF16)
    dest, wsel, cnt, loff = _plan(lg_p.reshape(tp, LANES), bias_col, upper)
    per_window = lambda a: a.T.reshape(nw, 1, PLAN_WINDOW * TOP_K)
    dest_w, wsel_w = per_window(dest), per_window(wsel)
    h2_rows = h2_p.reshape(tp, ROW_WORDS)
    x_slots = _dispatch(dest_w, h2_rows)
    wd_s = _router(lg_s, bias_col)
    y_slots, y_s = _expert_ffn(*_work_items(cnt[:, :, 0], loff[:, :, 0]), x_slots, *experts[:3],
                               h2_s, wd_s, x1_s, mod_s[:, 5 * d:], gffn_post, *experts[3:])
    ws13 = jnp.concatenate([experts[3], experts[4]], axis=1).astype(BF16)
    gt2_p = mod_p[:, :, 5 * d:]
    y_p = _combine(dest_w, wsel_w, y_slots, h2_rows, x1_p.reshape(tp, d), gt2_p, lp, gffn_post,
                   ws13, experts[5].astype(BF16))

    return (y_p.reshape(bp, lp, d), y_s.reshape(bs, 1, d), pool_p[None], st_p[None], pool_s[None], st_s[None])
```

```python
import functools

import numpy as np
import jax
import jax.numpy as jnp
from jax import lax
from jax.experimental import pallas as pl
from jax.experimental.pallas import tpu as pltpu

F32 = jnp.float32
BF16 = jnp.bfloat16

D_MODEL = 1024
POOL_WIDTH = 512
POOL_WINDOWS = (2, 4, 8, 16)
POOL_GROUP = 128
POOL_HIST = 15
HG_WIDTH = 512
HG_HEADS = 4
HG_HEAD_DIM = 128
IN_WIDTH = POOL_WIDTH + 4 * HG_WIDTH
N_EXPERTS = 64
TOP_K = 8
N_GROUPS = 8
TOPK_GROUPS = 4
GROUP_SIZE = N_EXPERTS // N_GROUPS
D_EXPERT = 256
ROUTED_SCALE = 2.5
EPS = 1e-6

LANES = 128
CHUNK = 64
TIME_TILE = 256
ADA_TILE_N = 512
ROUTER_TILE = 512
SAMPLE_STATE_BLOCK = 8
PLAN_WINDOW = 1024
RUN_ALIGN = 32
FFN_TILE = 160
REGION_ROWS = PLAN_WINDOW * TOP_K + N_EXPERTS * (RUN_ALIGN - 1) + FFN_TILE
ROW_WORDS = D_MODEL // 2
ROW_SUB = ROW_WORDS // LANES
COMBINE_TILE = 256
SCATTER_UNROLL = 4
COMBINE_UNROLL = 8
FFN_GROUP = 4
FFN_SPLIT = 1
FFN_RING = 6
VMEM_LIMIT = 58 * 1024 * 1024

_LEVELS = (64, 32, 16, 8, 4, 2)
_BLK_CUM = 0
_BLK_END = 1


def _level_blocks():
    c = CHUNK
    i = np.arange(c)[:, None]
    s = np.arange(c)[None, :]
    blocks = [(s <= i), (s > i)]
    index = {}
    for lvl in _LEVELS:
        seg = (i // lvl) * lvl
        mid = seg + lvl // 2
        ref = mid - 1
        index[lvl] = len(blocks)
        blocks.append((i >= mid) & (s > ref) & (s <= i))
        if lvl > 2:
            blocks.append((i < mid) & (s > i) & (s <= ref))
    m = np.concatenate(blocks, axis=0).astype(np.float32)
    return np.concatenate([m, m], axis=1), index


_M2_NP, _LEVEL_INDEX = _level_blocks()


def _dot(a, b):
    return jnp.dot(a, b, preferred_element_type=F32)


def _dot_nt(a, b):
    return lax.dot_general(a, b, (((1,), (1,)), ((), ())), preferred_element_type=F32)


def _split2(x):
    hi = x.astype(BF16)
    lo = (x - hi.astype(F32)).astype(BF16)
    return hi, lo


def _pack_pair(a, b):
    lo = lax.bitcast_convert_type(a.astype(BF16).astype(F32), jnp.uint32)
    hi = lax.bitcast_convert_type(b.astype(BF16).astype(F32), jnp.uint32)
    return (lo >> 16) | hi


def _unpack_pair(words):
    lo = lax.bitcast_convert_type(words << 16, F32)
    hi = lax.bitcast_convert_type(words & jnp.uint32(0xFFFF0000), F32)
    return lo, hi


def _silu(x):
    return x * jax.nn.sigmoid(x)


def _rms(x, g):
    return x * lax.rsqrt(jnp.mean(x * x, axis=-1, keepdims=True) + EPS) * g


def _mods(m):
    return [m[:, j * D_MODEL:(j + 1) * D_MODEL] for j in range(6)]


def _forget_lower_bound(lbl):
    mx = jnp.max(lbl, axis=0, keepdims=True)
    e = jnp.exp(lbl - mx)
    return e[0:1] / jnp.sum(e, axis=0, keepdims=True)


def _router_logits(h2, wrh, wrl):
    hi, lo = _split2(h2)
    return _dot(hi, wrh) + _dot(lo, wrh) + _dot(hi, wrl)


def _ada_kernel(c_ref, w_ref, b_ref, o_ref):
    a_hi, a_lo = _split2(_silu(c_ref[...]))
    w_hi, w_lo = _split2(w_ref[...])
    o_ref[...] = _dot(a_hi, w_hi) + _dot(a_lo, w_hi) + _dot(a_hi, w_lo) + b_ref[...]


def _ada(c_all, w_ada, b_ada):
    rows = c_all.shape[0]
    n = w_ada.shape[1]
    return pl.pallas_call(
        _ada_kernel,
        out_shape=jax.ShapeDtypeStruct((rows, n), F32),
        grid=(n // ADA_TILE_N,),
        in_specs=[pl.BlockSpec((rows, D_MODEL), lambda j: (0, 0)),
                  pl.BlockSpec((D_MODEL, ADA_TILE_N), lambda j: (0, j)),
                  pl.BlockSpec((1, ADA_TILE_N), lambda j: (0, j))],
        out_specs=pl.BlockSpec((rows, ADA_TILE_N), lambda j: (0, j)),
        compiler_params=pltpu.CompilerParams(dimension_semantics=("parallel",),
                                             vmem_limit_bytes=VMEM_LIMIT),
        name="ada",
    )(c_all, w_ada, b_ada)


def _level_masks():
    i = lax.broadcasted_iota(jnp.int32, (CHUNK, 2 * CHUNK), 0)
    j = lax.broadcasted_iota(jnp.int32, (CHUNK, 2 * CHUNK), 1) & (CHUNK - 1)
    masks = {}
    for lvl in _LEVELS:
        sh = lvl.bit_length() - 1
        same = (i >> sh) == (j >> sh)
        upper = ((i >> (sh - 1)) & 1) == 1
        lower = ((j >> (sh - 1)) & 1) == 0
        masks[lvl] = same & upper & lower
    return masks, i == j


def _by_head(m):
    z = jnp.zeros((m.shape[0], HG_HEAD_DIM), m.dtype)
    return jnp.concatenate([jnp.concatenate([m[:, :HG_HEAD_DIM], z], axis=1),
                            jnp.concatenate([z, m[:, HG_HEAD_DIM:]], axis=1)], axis=0)


def _hgrn_chunk_pair(q, k, v, e_all, lane0, sts, masks, eye):
    hd = HG_HEAD_DIM

    def blk(n):
        return e_all[n * CHUNK:(n + 1) * CHUNK, lane0:lane0 + 2 * hd]

    b = blk(_BLK_CUM)
    kb = k.astype(BF16)
    a = jnp.where(eye, _dot_nt(q.astype(BF16), _by_head(kb)), 0.0)
    for lvl in _LEVELS:
        n = _LEVEL_INDEX[lvl]
        ql = (q * jnp.exp(blk(n))).astype(BF16)
        kl = (k * jnp.exp(blk(n + 1))).astype(BF16) if lvl > 2 else kb
        a = a + jnp.where(masks[lvl], _dot_nt(ql, _by_head(kl)), 0.0)
    st_both = jnp.concatenate([jnp.concatenate([sts[0], jnp.zeros_like(sts[0])], axis=1),
                               jnp.concatenate([jnp.zeros_like(sts[1]), sts[1]], axis=1)], axis=0).astype(BF16)
    o = _dot(a.astype(BF16), _by_head(v.astype(BF16))) + _dot_nt((q * jnp.exp(b)).astype(BF16), st_both)
    k_end = (k * jnp.exp(blk(_BLK_END))).astype(BF16)
    decay = jnp.exp(b[CHUNK - 1:CHUNK, :])
    new = []
    for h in range(2):
        lanes = slice(h * hd, (h + 1) * hd)
        new.append(sts[h] * decay[:, lanes] + _dot(v[:, lanes].T.astype(BF16), k_end[:, lanes]))
    return o, new


def _mix_prompt_kernel(x_ref, mod_ref, gpre_ref, gpost_ref, gffn_ref, win_ref, wpool_ref, pscale_ref,
                       lbl_ref, gout_ref, wout_ref, wrh_ref, wrl_ref, m2_ref,
                       x1_ref, h2_ref, lg_ref, pool_ref, st_ref,
                       st_s, ubuf, q_s, k_s, v_s, g_s, o_s):
    t = pl.program_id(1)
    n_t = pl.num_programs(1)
    tt = TIME_TILE

    @pl.when(t == 0)
    def _():
        st_s[...] = jnp.zeros_like(st_s)
        ubuf[0:16, :] = jnp.zeros((16, POOL_WIDTH), F32)

    xt = x_ref[0]
    sh1, sc1, gt1, sh2, sc2, gt2 = _mods(mod_ref[0])
    h = _rms(xt, gpre_ref[...]) * (1.0 + sc1) + sh1
    proj = _dot(h.astype(BF16), win_ref[...])

    u = proj[:, :POOL_WIDTH]
    ubuf[16:16 + tt, :] = u
    pos = (t * tt + lax.broadcasted_iota(jnp.int32, (tt, 1), 0) + 1).astype(F32)
    ys = []
    for g, w in enumerate(POOL_WINDOWS):
        s = ubuf[:, g * POOL_GROUP:(g + 1) * POOL_GROUP]
        off = 0
        for step in range(w.bit_length() - 1):
            sh = 1 << step
            s = s[sh:, :] + s[:-sh, :]
            off += sh
        ws = s[16 - off:16 - off + tt, :]
        cnt = jnp.minimum(pos, float(w))
        d = ws / cnt - u[:, g * POOL_GROUP:(g + 1) * POOL_GROUP]
        ys.append(_dot(d.astype(BF16), wpool_ref[g]))
    y_pool = jnp.concatenate(ys, axis=1) * pscale_ref[...]
    ubuf[0:16, :] = ubuf[tt:tt + 16, :]

    @pl.when(t == n_t - 1)
    def _():
        pool_ref[0] = u[tt - POOL_HIST:, :]

    lb = _forget_lower_bound(lbl_ref[...])
    f = lb + (1.0 - lb) * jax.nn.sigmoid(proj[:, POOL_WIDTH + HG_WIDTH:POOL_WIDTH + 2 * HG_WIDTH])
    q_s[...] = _silu(proj[:, POOL_WIDTH:POOL_WIDTH + HG_WIDTH])
    k_s[...] = 1.0 - f
    v_s[...] = proj[:, POOL_WIDTH + 2 * HG_WIDTH:POOL_WIDTH + 3 * HG_WIDTH]
    g_s[...] = jnp.log(f)
    gate = _silu(proj[:, POOL_WIDTH + 3 * HG_WIDTH:])

    masks, eye = _level_masks()
    m2 = m2_ref[...]

    states = [st_s[hd] for hd in range(HG_HEADS)]
    for c in range(tt // CHUNK):
        rows = slice(c * CHUNK, (c + 1) * CHUNK)
        g_hi, g_lo = _split2(g_s[rows, :])
        e_all = _dot(m2, jnp.concatenate([g_hi, g_lo], axis=0))
        for hd in range(0, HG_HEADS, 2):
            lane0 = hd * HG_HEAD_DIM
            lanes = slice(lane0, lane0 + 2 * HG_HEAD_DIM)
            o, states[hd:hd + 2] = _hgrn_chunk_pair(q_s[rows, lanes], k_s[rows, lanes], v_s[rows, lanes], e_all,
                                                    lane0, states[hd:hd + 2], masks, eye)
            o_s[rows, lanes] = o
    for hd in range(HG_HEADS):
        st_s[hd] = states[hd]

    o = o_s[...]
    os_ = []
    for hd in range(HG_HEADS):
        oh = o[:, hd * HG_HEAD_DIM:(hd + 1) * HG_HEAD_DIM]
        os_.append(oh * lax.rsqrt(jnp.mean(oh * oh, axis=-1, keepdims=True) + EPS) * gout_ref[...])
    o_n = jnp.concatenate(os_, axis=1) * gate

    mix = _dot(y_pool.astype(BF16), wout_ref[0:POOL_WIDTH, :]) + _dot(o_n.astype(BF16), wout_ref[POOL_WIDTH:, :])
    x1 = xt + gt1 * _rms(mix, gpost_ref[...])
    h2 = _rms(x1, gffn_ref[...]) * (1.0 + sc2) + sh2
    x1_ref[0] = x1
    h2_ref[0] = _pack_pair(h2[:, :ROW_WORDS], h2[:, ROW_WORDS:])
    lg_ref[0] = _router_logits(h2, wrh_ref[...], wrl_ref[...])

    @pl.when(t == n_t - 1)
    def _():
        for hd in range(HG_HEADS):
            st_ref[0, hd] = st_s[hd].T


def _full(shape):
    nd = len(shape)
    return pl.BlockSpec(shape, lambda *_: (0,) * nd)


def _mix_prompt(x, mod, gpre, gpost, gffn, win, wpool, pscale, lbl, gout, wout, wrh, wrl, m2):
    b, l, d = x.shape
    tt = TIME_TILE
    tile = lambda i, j: (i, j, 0)
    per_b = lambda i, j: (i, 0, 0)
    return pl.pallas_call(
        _mix_prompt_kernel,
        out_shape=(jax.ShapeDtypeStruct((b, l, d), F32),
                   jax.ShapeDtypeStruct((b, l, ROW_WORDS), jnp.uint32),
                   jax.ShapeDtypeStruct((b, l, LANES), F32),
                   jax.ShapeDtypeStruct((b, POOL_HIST, POOL_WIDTH), F32),
                   jax.ShapeDtypeStruct((b, HG_HEADS, HG_HEAD_DIM, HG_HEAD_DIM), F32)),
        grid=(b, l // tt),
        in_specs=[pl.BlockSpec((1, tt, d), tile),
                  pl.BlockSpec((1, 1, 6 * d), per_b),
                  _full((1, d)), _full((1, d)), _full((1, d)),
                  _full(win.shape), _full(wpool.shape), _full((1, POOL_WIDTH)),
                  _full(lbl.shape), _full((1, HG_HEAD_DIM)), _full(wout.shape),
                  _full(wrh.shape), _full(wrl.shape), _full(m2.shape)],
        out_specs=(pl.BlockSpec((1, tt, d), tile),
                   pl.BlockSpec((1, tt, ROW_WORDS), tile),
                   pl.BlockSpec((1, tt, LANES), tile),
                   pl.BlockSpec((1, POOL_HIST, POOL_WIDTH), per_b),
                   pl.BlockSpec((1, HG_HEADS, HG_HEAD_DIM, HG_HEAD_DIM), lambda i, j: (i, 0, 0, 0))),
        scratch_shapes=[pltpu.VMEM((HG_HEADS, HG_HEAD_DIM, HG_HEAD_DIM), F32),
                        pltpu.VMEM((tt + 16, POOL_WIDTH), F32),
                        pltpu.VMEM((tt, HG_WIDTH), F32), pltpu.VMEM((tt, HG_WIDTH), F32),
                        pltpu.VMEM((tt, HG_WIDTH), F32), pltpu.VMEM((tt, HG_WIDTH), F32),
                        pltpu.VMEM((tt, HG_WIDTH), F32)],
        compiler_params=pltpu.CompilerParams(dimension_semantics=("parallel", "arbitrary"),
                                             vmem_limit_bytes=VMEM_LIMIT),
        name="mix_prompt",
    )(x, mod, gpre, gpost, gffn, win, wpool, pscale, lbl, gout, wout, wrh, wrl, m2)


def _mix_sample_in_kernel(x_ref, mod_ref, gpre_ref, win_ref, wpool_ref, pscale_ref, lbl_ref, hist_ref,
                          ypool_ref, npool_ref, ft_ref, qt_ref, v_ref, gate_ref):
    xt = x_ref[...]
    sh1, sc1 = _mods(mod_ref[...])[:2]
    h = _rms(xt, gpre_ref[...]) * (1.0 + sc1) + sh1
    proj = _dot(h.astype(BF16), win_ref[...])
    u = proj[:, :POOL_WIDTH]
    row = lax.broadcasted_iota(jnp.int32, (hist_ref.shape[0], POOL_HIST, POOL_GROUP), 1)
    ys = []
    for g, w in enumerate(POOL_WINDOWS):
        sl = slice(g * POOL_GROUP, (g + 1) * POOL_GROUP)
        past = jnp.sum(jnp.where(row >= POOL_HIST - (w - 1), hist_ref[:, :, sl], 0.0), axis=1)
        ug = u[:, sl]
        d = (past + ug) / float(w) - ug
        ys.append(_dot(d.astype(BF16), wpool_ref[g]))
    ypool_ref[...] = jnp.concatenate(ys, axis=1) * pscale_ref[...]
    npool_ref[:, 0:POOL_HIST - 1, :] = hist_ref[:, 1:POOL_HIST, :]
    npool_ref[:, POOL_HIST - 1, :] = u

    lb = _forget_lower_bound(lbl_ref[...])
    f = lb + (1.0 - lb) * jax.nn.sigmoid(proj[:, POOL_WIDTH + HG_WIDTH:POOL_WIDTH + 2 * HG_WIDTH])
    ft_ref[...] = f.T
    qt_ref[...] = _silu(proj[:, POOL_WIDTH:POOL_WIDTH + HG_WIDTH]).T
    v_ref[...] = proj[:, POOL_WIDTH + 2 * HG_WIDTH:POOL_WIDTH + 3 * HG_WIDTH]
    gate_ref[...] = _silu(proj[:, POOL_WIDTH + 3 * HG_WIDTH:])


def _mix_sample_state_kernel(s_ref, ft_ref, qt_ref, v_ref, snew_ref, o_ref):
    i = pl.program_id(0)
    lane = lax.broadcasted_iota(jnp.int32, (HG_HEAD_DIM, ft_ref.shape[1]), 1)
    for j in range(SAMPLE_STATE_BLOCK):
        mine = lane == i * SAMPLE_STATE_BLOCK + j
        for hd in range(HG_HEADS):
            r0 = hd * HG_HEAD_DIM
            f = jnp.sum(jnp.where(mine, ft_ref[r0:r0 + HG_HEAD_DIM, :], 0.0), axis=1, keepdims=True)
            q = jnp.sum(jnp.where(mine, qt_ref[r0:r0 + HG_HEAD_DIM, :], 0.0), axis=1, keepdims=True)
            v = v_ref[j:j + 1, r0:r0 + HG_HEAD_DIM]
            s_new = f * s_ref[j, hd] + (1.0 - f) * v
            snew_ref[j, hd] = s_new
            o_ref[j:j + 1, r0:r0 + HG_HEAD_DIM] = jnp.sum(q * s_new, axis=0, keepdims=True)


def _mix_sample_out_kernel(x_ref, mod_ref, o_ref, gate_ref, ypool_ref, gout_ref, wout_ref, gpost_ref,
                           gffn_ref, wrh_ref, wrl_ref, x1_ref, h2_ref, lg_ref):
    _, _, gt1, sh2, sc2, _ = _mods(mod_ref[...])
    o = o_ref[...]
    os_ = []
    for hd in range(HG_HEADS):
        oh = o[:, hd * HG_HEAD_DIM:(hd + 1) * HG_HEAD_DIM]
        os_.append(oh * lax.rsqrt(jnp.mean(oh * oh, axis=-1, keepdims=True) + EPS) * gout_ref[...])
    o_n = jnp.concatenate(os_, axis=1) * gate_ref[...]
    mix = (_dot(ypool_ref[...].astype(BF16), wout_ref[0:POOL_WIDTH, :])
           + _dot(o_n.astype(BF16), wout_ref[POOL_WIDTH:, :]))
    x1 = x_ref[...] + gt1 * _rms(mix, gpost_ref[...])
    h2 = _rms(x1, gffn_ref[...]) * (1.0 + sc2) + sh2
    x1_ref[...] = x1
    h2_ref[...] = h2.astype(BF16)
    lg_ref[...] = _router_logits(h2, wrh_ref[...], wrl_ref[...])


def _mix_sample(x, mod, hist, state, gpre, gpost, gffn, win, wpool, pscale, lbl, gout, wout, wrh, wrl):
    b = x.shape[0]
    bb = SAMPLE_STATE_BLOCK
    cp = pltpu.CompilerParams(vmem_limit_bytes=VMEM_LIMIT)
    ypool, npool, ft, qt, v, gate = pl.pallas_call(
        _mix_sample_in_kernel,
        out_shape=(jax.ShapeDtypeStruct((b, POOL_WIDTH), F32),
                   jax.ShapeDtypeStruct((b, POOL_HIST, POOL_WIDTH), F32),
                   jax.ShapeDtypeStruct((HG_WIDTH, b), F32),
                   jax.ShapeDtypeStruct((HG_WIDTH, b), F32),
                   jax.ShapeDtypeStruct((b, HG_WIDTH), F32),
                   jax.ShapeDtypeStruct((b, HG_WIDTH), F32)),
        compiler_params=cp,
        name="mix_sample_in",
    )(x, mod, gpre, win, wpool, pscale, lbl, hist)

    s_spec = pl.BlockSpec((bb, HG_HEADS, HG_HEAD_DIM, HG_HEAD_DIM), lambda i: (i, 0, 0, 0))
    col_spec = _full((HG_WIDTH, b))
    row_spec = pl.BlockSpec((bb, HG_WIDTH), lambda i: (i, 0))
    s_new, o = pl.pallas_call(
        _mix_sample_state_kernel,
        out_shape=(jax.ShapeDtypeStruct(state.shape, F32), jax.ShapeDtypeStruct((b, HG_WIDTH), F32)),
        grid=(b // bb,),
        in_specs=[s_spec, col_spec, col_spec, row_spec],
        out_specs=(s_spec, row_spec),
        compiler_params=pltpu.CompilerParams(dimension_semantics=("parallel",), vmem_limit_bytes=VMEM_LIMIT),
        name="mix_sample_state",
    )(state, ft, qt, v)

    x1, h2, lg = pl.pallas_call(
        _mix_sample_out_kernel,
        out_shape=(jax.ShapeDtypeStruct((b, D_MODEL), F32),
                   jax.ShapeDtypeStruct((b, D_MODEL), BF16),
                   jax.ShapeDtypeStruct((b, LANES), F32)),
        compiler_params=cp,
        name="mix_sample_out",
    )(x, mod, o, gate, ypool, gout, wout, gpost, gffn, wrh, wrl)
    return x1, h2, lg, npool, s_new


def _first_max(cur, idx, big):
    m = jnp.max(cur, axis=0, keepdims=True)
    first = jnp.min(jnp.where(cur == m, idx, big), axis=0, keepdims=True)
    return idx == first


def _route(logit, bias):
    n = logit.shape[1]
    scores = jax.nn.sigmoid(logit)
    sel = scores + bias
    neg = -jnp.inf

    sub = lax.broadcasted_iota(jnp.int32, (GROUP_SIZE, n), 0)
    gscore = []
    for g in range(N_GROUPS):
        sg = sel[g * GROUP_SIZE:(g + 1) * GROUP_SIZE, :]
        m1 = jnp.max(sg, axis=0, keepdims=True)
        rest = jnp.where(_first_max(sg, sub, GROUP_SIZE), neg, sg)
        gscore.append(m1 + jnp.max(rest, axis=0, keepdims=True))
    cur = jnp.concatenate(gscore, axis=0)
    gidx = lax.broadcasted_iota(jnp.int32, (N_GROUPS, n), 0)
    gmask = jnp.zeros((N_GROUPS, n), jnp.bool_)
    for _ in range(TOPK_GROUPS):
        pick = _first_max(cur, gidx, N_GROUPS)
        gmask = gmask | pick
        cur = jnp.where(pick, neg, cur)
    emask = jnp.concatenate(
        [jnp.broadcast_to(gmask[g:g + 1, :], (GROUP_SIZE, n)) for g in range(N_GROUPS)], axis=0)

    cur = jnp.where(emask, sel, neg)
    eidx = lax.broadcasted_iota(jnp.int32, (N_EXPERTS, n), 0)
    picks = []
    for _ in range(TOP_K):
        pick = _first_max(cur, eidx, N_EXPERTS)
        picks.append(pick)
        cur = jnp.where(pick, neg, cur)
    return scores, picks


def _any(masks):
    return functools.reduce(jnp.logical_or, masks)


def _router_kernel(lg_ref, bias_ref, wd_ref):
    n = lg_ref.shape[0]
    scores, picks = _route(lg_ref[...].T[0:N_EXPERTS, :], bias_ref[0:N_EXPERTS, :])
    chosen = _any(picks)
    wsum = jnp.sum(jnp.where(chosen, scores, 0.0), axis=0, keepdims=True)
    wd = jnp.where(chosen, scores / wsum * ROUTED_SCALE, 0.0)
    wd_ref[...] = jnp.concatenate([wd, jnp.zeros((LANES - N_EXPERTS, n), F32)], axis=0).T


def _router(logits, bias_col):
    t = logits.shape[0]
    tile = min(ROUTER_TILE, t)
    return pl.pallas_call(
        _router_kernel,
        out_shape=jax.ShapeDtypeStruct((t, LANES), F32),
        grid=(t // tile,),
        in_specs=[pl.BlockSpec((tile, LANES), lambda i: (i, 0)), _full(bias_col.shape)],
        out_specs=pl.BlockSpec((tile, LANES), lambda i: (i, 0)),
        compiler_params=pltpu.CompilerParams(dimension_semantics=("parallel",), vmem_limit_bytes=VMEM_LIMIT),
        name="router",
    )(logits, bias_col)


def _plan_kernel(lg_ref, bias_ref, upper_ref, dest_ref, wsel_ref, cnt_ref, loff_ref):
    scores, picks = _route(lg_ref[...].T[0:N_EXPERTS, :], bias_ref[0:N_EXPERTS, :])
    chosen = _any(picks)
    chosen_f = jnp.where(chosen, 1.0, 0.0)
    cnt = jnp.sum(chosen_f, axis=1, keepdims=True)
    units = jnp.floor((cnt + (RUN_ALIGN - 1)) / RUN_ALIGN)
    ei = lax.broadcasted_iota(jnp.int32, (N_EXPERTS, N_EXPERTS), 0)
    ej = lax.broadcasted_iota(jnp.int32, (N_EXPERTS, N_EXPERTS), 1)
    before = jnp.where(ej < ei, 1.0, 0.0).astype(BF16)
    loff = RUN_ALIGN * _dot(before, jnp.broadcast_to(units, (N_EXPERTS, LANES)).astype(BF16))[:, 0:1]
    slot = loff + _dot(chosen_f.astype(BF16), upper_ref[...])
    wsum = jnp.sum(jnp.where(chosen, scores, 0.0), axis=0, keepdims=True)
    pick_sum = lambda v: jnp.concatenate(
        [jnp.sum(jnp.where(p, v, 0.0), axis=0, keepdims=True) for p in picks], axis=0)
    dest_ref[0] = (pick_sum(slot) * ROW_SUB).astype(jnp.int32)
    wsel_ref[0] = pick_sum(scores) / wsum * ROUTED_SCALE
    cnt_ref[0] = jnp.broadcast_to(cnt, (N_EXPERTS, LANES))
    loff_ref[0] = jnp.broadcast_to(loff, (N_EXPERTS, LANES))


def _plan(logits, bias_col, upper):
    t = logits.shape[0]
    nw = t // PLAN_WINDOW
    per_w = pl.BlockSpec((1, N_EXPERTS, LANES), lambda w: (w, 0, 0))
    picks = pl.BlockSpec((1, TOP_K, PLAN_WINDOW), lambda w: (w, 0, 0))
    return pl.pallas_call(
        _plan_kernel,
        out_shape=(jax.ShapeDtypeStruct((nw, TOP_K, PLAN_WINDOW), jnp.int32),
                   jax.ShapeDtypeStruct((nw, TOP_K, PLAN_WINDOW), F32),
                   jax.ShapeDtypeStruct((nw, N_EXPERTS, LANES), F32),
                   jax.ShapeDtypeStruct((nw, N_EXPERTS, LANES), F32)),
        grid=(nw,),
        in_specs=[pl.BlockSpec((PLAN_WINDOW, LANES), lambda w: (w, 0)), _full(bias_col.shape), _full(upper.shape)],
        out_specs=(picks, picks, per_w, per_w),
        compiler_params=pltpu.CompilerParams(dimension_semantics=("parallel",), vmem_limit_bytes=VMEM_LIMIT),
        name="plan",
    )(logits, bias_col, upper)


def _slot_rows(slot):
    return pl.ds(pl.multiple_of(slot * ROW_SUB, ROW_SUB), ROW_SUB)


def _rows_at(offset):
    return pl.ds(pl.multiple_of(offset, ROW_SUB), ROW_SUB)


def _dispatch_kernel(dest_ref, h_ref, x_ref, dest_s, rows_s, sem):
    plan_copy = pltpu.make_async_copy(dest_ref.at[0], dest_s, sem)
    plan_copy.start()
    x_ref[...] = jnp.zeros_like(x_ref)
    for j in range(ROW_SUB):
        rows_s[pl.ds(j, PLAN_WINDOW, stride=ROW_SUB), :] = h_ref[:, j * LANES:(j + 1) * LANES]
    plan_copy.wait()

    def body(i, carry):
        for u in range(SCATTER_UNROLL):
            t = i * SCATTER_UNROLL + u
            row = rows_s[_slot_rows(t), :]
            for k in range(TOP_K):
                x_ref[_rows_at(dest_s[k, t]), :] = row
        return carry

    lax.fori_loop(0, PLAN_WINDOW // SCATTER_UNROLL, body, 0)


def _dispatch(dest_w, h2):
    nw = dest_w.shape[0]
    region = REGION_ROWS * ROW_SUB
    return pl.pallas_call(
        _dispatch_kernel,
        out_shape=jax.ShapeDtypeStruct((nw * region, LANES), jnp.uint32),
        grid=(nw,),
        in_specs=[pl.BlockSpec((1, TOP_K, PLAN_WINDOW), lambda w: (w, 0, 0)),
                  pl.BlockSpec((PLAN_WINDOW, ROW_WORDS), lambda w: (w, 0))],
        out_specs=pl.BlockSpec((region, LANES), lambda w: (w, 0)),
        scratch_shapes=[pltpu.SMEM((TOP_K, PLAN_WINDOW), jnp.int32),
                        pltpu.VMEM((PLAN_WINDOW * ROW_SUB, LANES), jnp.uint32),
                        pltpu.SemaphoreType.DMA],
        compiler_params=pltpu.CompilerParams(dimension_semantics=("arbitrary",), vmem_limit_bytes=VMEM_LIMIT),
        name="dispatch",
    )(dest_w, h2)


def _unpack_rows(ref, n):
    parts = [_unpack_pair(ref[pl.ds(j, n, stride=ROW_SUB), :]) for j in range(ROW_SUB)]
    return jnp.concatenate([p[0] for p in parts] + [p[1] for p in parts], axis=1)


def _ffn_kernel(gq_ref, gs_ref, to_ref, tn_ref, x_hbm, w1_ref, w3_ref, w2_ref,
                hs_ref, wds_ref, x1s_ref, gt2s_ref, gpost_ref, ws1_ref, ws3_ref, ws2_ref, y_hbm, ys_ref,
                w13_s, w2_s, xbuf, ybuf, rsem, wsem, acc_s):
    del x_hbm
    e = pl.program_id(0)
    n_groups = gq_ref[N_EXPERTS]
    block = RUN_ALIGN * ROW_SUB
    tile = FFN_TILE * ROW_SUB

    def go(cp, start):
        if start:
            cp.start()
        else:
            cp.wait()

    def fetch(q, start):
        def one(j):
            src = y_hbm.at[pl.ds(pl.multiple_of(to_ref[gs_ref[q] + j] * block, block), tile)]
            go(pltpu.make_async_copy(src, xbuf.at[q % FFN_RING, pl.ds(j * tile, tile)], rsem.at[q % FFN_RING]), start)

        size = gs_ref[q + 1] - gs_ref[q]

        @pl.when(size == FFN_GROUP)
        def _():
            for j in range(FFN_GROUP):
                one(j)

        @pl.when(size < FFN_GROUP)
        def _():
            for j in range(FFN_GROUP - 1):
                pl.when(j < size)(functools.partial(one, j))

    def writeback(q, start):
        most = FFN_TILE // RUN_ALIGN - 1

        def blocks(j, b, n):
            dst = pl.multiple_of(to_ref[gs_ref[q] + j] * block, block) + b * block
            go(pltpu.make_async_copy(ybuf.at[q % FFN_RING, pl.ds(j * tile + b * block, n * block)],
                                     y_hbm.at[pl.ds(dst, n * block)], wsem.at[q % FFN_RING]), start)

        for j in range(FFN_GROUP):
            item = gs_ref[q] + j
            n = jnp.where(item < gs_ref[q + 1], tn_ref[item], 0)
            pl.when(n >= most)(functools.partial(blocks, j, 0, most))
            pl.when(n > most)(functools.partial(blocks, j, most, 1))

            @pl.when((n > 0) & (n < most))
            def _():
                for b in range(most - 1):
                    pl.when(b < n)(functools.partial(blocks, j, b, 1))

    @pl.when(e == 0)
    def _():
        xbuf[...] = jnp.zeros_like(xbuf)
        for d in range(FFN_RING - 1):
            pl.when(d < n_groups)(functools.partial(fetch, d, True))

    w13_s[:, 0:D_EXPERT] = w1_ref[0].astype(BF16)
    w13_s[:, D_EXPERT:] = w3_ref[0].astype(BF16)
    w2_s[...] = w2_ref[0].astype(BF16)

    xs = hs_ref[...]

    @pl.when(e == 0)
    def _():
        sh = _silu(_dot(xs, ws1_ref[...].astype(BF16))) * _dot(xs, ws3_ref[...].astype(BF16))
        acc_s[...] = _dot(sh.astype(BF16), ws2_ref[...].astype(BF16))

    lane = lax.broadcasted_iota(jnp.int32, wds_ref.shape, 1)
    wcol = jnp.sum(jnp.where(lane == e, wds_ref[...], 0.0), axis=1, keepdims=True)
    hd = _dot(xs, w13_s[...])
    acc_s[...] += _dot((_silu(hd[:, :D_EXPERT]) * hd[:, D_EXPERT:] * wcol).astype(BF16), w2_s[...])

    @pl.when(e == pl.num_programs(0) - 1)
    def _():
        ys_ref[...] = x1s_ref[...] + gt2s_ref[...] * _rms(acc_s[...], gpost_ref[...])

    def group_body(q, carry):
        @pl.when(q + FFN_RING - 1 < n_groups)
        def _():
            fetch(q + FFN_RING - 1, True)

        @pl.when(q >= FFN_RING)
        def _():
            writeback(q - FFN_RING, False)

        fetch(q, False)
        rows = FFN_GROUP * FFN_TILE // FFN_SPLIT
        for part in range(FFN_SPLIT):
            span = pl.ds(part * rows * ROW_SUB, rows * ROW_SUB)
            x = _unpack_rows(xbuf.at[q % FFN_RING, span], rows).astype(BF16)
            h = _dot(x, w13_s[...])
            act = _silu(h[:, :D_EXPERT]) * h[:, D_EXPERT:]
            y = _dot(act.astype(BF16), w2_s[...])
            out = ybuf.at[q % FFN_RING, span]
            for j in range(ROW_SUB):
                out[pl.ds(j, rows, stride=ROW_SUB), :] = _pack_pair(
                    y[:, j * LANES:(j + 1) * LANES], y[:, ROW_WORDS + j * LANES:ROW_WORDS + (j + 1) * LANES])
        writeback(q, True)
        return carry

    lax.fori_loop(gq_ref[e], gq_ref[e + 1], group_body, 0)

    @pl.when(e == pl.num_programs(0) - 1)
    def _():
        for d in range(FFN_RING):
            @pl.when(n_groups - 1 - d >= 0)
            def _():
                writeback(n_groups - 1 - d, False)


def _expert_ffn(group_of_expert, group_start, item_off, item_blocks, x_slots, w1, w3, w2,
                h2_s, wd_s, x1_s, gt2_s, gpost, ws1, ws3, ws2):
    of_expert = lambda e, *_: (e, 0, 0)
    whole = lambda a: pl.BlockSpec(a.shape, lambda e, *_: (0,) * a.ndim)
    tiles = pltpu.VMEM((FFN_RING, FFN_GROUP * FFN_TILE * ROW_SUB, LANES), jnp.uint32)
    sems = pltpu.SemaphoreType.DMA((FFN_RING,))
    dense = (h2_s, wd_s, x1_s, gt2_s, gpost, ws1, ws3, ws2)
    grid_spec = pltpu.PrefetchScalarGridSpec(
        num_scalar_prefetch=4,
        grid=(N_EXPERTS,),
        in_specs=[pl.BlockSpec(memory_space=pl.ANY),
                  pl.BlockSpec((1, D_MODEL, D_EXPERT), of_expert),
                  pl.BlockSpec((1, D_MODEL, D_EXPERT), of_expert),
                  pl.BlockSpec((1, D_EXPERT, D_MODEL), of_expert)] + [whole(a) for a in dense],
        out_specs=(pl.BlockSpec(memory_space=pl.ANY), whole(x1_s)),
        scratch_shapes=[pltpu.VMEM((D_MODEL, 2 * D_EXPERT), BF16), pltpu.VMEM((D_EXPERT, D_MODEL), BF16),
                        tiles, tiles, sems, sems, pltpu.VMEM(x1_s.shape, F32)])
    return pl.pallas_call(
        _ffn_kernel,
        out_shape=(jax.ShapeDtypeStruct(x_slots.shape, jnp.uint32), jax.ShapeDtypeStruct(x1_s.shape, F32)),
        grid_spec=grid_spec,
        input_output_aliases={4: 0},
        compiler_params=pltpu.CompilerParams(dimension_semantics=("arbitrary",), vmem_limit_bytes=VMEM_LIMIT),
        name="expert_ffn",
    )(group_of_expert, group_start, item_off, item_blocks, x_slots, w1, w3, w2, *dense)


def _work_items(cnt, loff):
    nw = cnt.shape[0]
    cnt = cnt.astype(jnp.int32).T.reshape(-1)
    loff = loff.astype(jnp.int32).T.reshape(-1)
    region = jnp.tile(jnp.arange(nw, dtype=jnp.int32) * REGION_ROWS, N_EXPERTS)
    n_items = nw * N_EXPERTS + (nw * PLAN_WINDOW * TOP_K) // FFN_TILE
    tiles = (cnt + FFN_TILE - 1) // FFN_TILE
    ends = jnp.cumsum(tiles)
    starts = ends - tiles
    i = jnp.arange(n_items + FFN_GROUP, dtype=jnp.int32)[:, None]
    mine = (starts[None, :] <= i) & (i < ends[None, :])
    of_pair = lambda v: jnp.sum(jnp.where(mine, v[None, :], 0), axis=1)
    r = i[:, 0] - of_pair(starts)
    row = of_pair(region + loff) + r * FFN_TILE
    own = jnp.clip(of_pair(cnt) - r * FFN_TILE, 0, FFN_TILE)

    zero = jnp.zeros((1,), jnp.int32)
    item_end = ends.reshape(N_EXPERTS, nw)[:, -1]
    item_start = jnp.concatenate([zero, item_end[:-1]])
    group_end = jnp.cumsum((item_end - item_start + FFN_GROUP - 1) // FFN_GROUP)
    group_first = jnp.concatenate([zero, group_end[:-1]])
    q = jnp.arange(N_EXPERTS + n_items // FFN_GROUP + 1, dtype=jnp.int32)[:, None]
    has = (group_first[None, :] <= q) & (q < group_end[None, :])
    of_expert = lambda v: jnp.sum(jnp.where(has, v[None, :], 0), axis=1)
    group_start = jnp.where(q[:, 0] < group_end[-1],
                            of_expert(item_start) + (q[:, 0] - of_expert(group_first)) * FFN_GROUP, item_end[-1])
    return (jnp.concatenate([zero, group_end]), group_start, row // RUN_ALIGN, (own + RUN_ALIGN - 1) // RUN_ALIGN)


def _combine_kernel(dest_ref, wsel_ref, y_ref, h_ref, x1_ref, gt2_ref, gpost_ref, ws13_ref, ws2_ref, out_ref,
                    dest_s, wsel_s, lo_s, hi_s, sem):
    s = pl.program_id(1)

    @pl.when(s == 0)
    def _():
        copies = (pltpu.make_async_copy(dest_ref.at[0], dest_s, sem.at[0]),
                  pltpu.make_async_copy(wsel_ref.at[0], wsel_s, sem.at[1]))
        for c in copies:
            c.start()
        for c in copies:
            c.wait()

    base = s * COMBINE_TILE

    def body(i, carry):
        for u in range(COMBINE_UNROLL):
            t = i * COMBINE_UNROLL + u
            acc_lo = jnp.zeros((ROW_SUB, LANES), F32)
            acc_hi = jnp.zeros((ROW_SUB, LANES), F32)
            for k in range(TOP_K):
                lo, hi = _unpack_pair(y_ref[_rows_at(dest_s[k, base + t]), :])
                w = wsel_s[k, base + t]
                acc_lo = acc_lo + w * lo
                acc_hi = acc_hi + w * hi
            lo_s[_slot_rows(t), :] = acc_lo
            hi_s[_slot_rows(t), :] = acc_hi
        return carry

    lax.fori_loop(0, COMBINE_TILE // COMBINE_UNROLL, body, 0)
    rows = lambda ref: [ref[pl.ds(j, COMBINE_TILE, stride=ROW_SUB), :] for j in range(ROW_SUB)]
    routed = jnp.concatenate(rows(lo_s) + rows(hi_s), axis=1)
    h_lo, h_hi = _unpack_pair(h_ref[...])
    x = jnp.concatenate([h_lo, h_hi], axis=1).astype(BF16)
    hs = _dot(x, ws13_ref[...])
    act = _silu(hs[:, :D_EXPERT]) * hs[:, D_EXPERT:]
    ff = routed + _dot(act.astype(BF16), ws2_ref[...])
    out_ref[...] = x1_ref[...] + gt2_ref[0] * _rms(ff, gpost_ref[...])


def _combine(dest_w, wsel_w, y_slots, h2, x1, gt2, tokens_per_gt2, gpost, ws13, ws2):
    nw = dest_w.shape[0]
    sub = PLAN_WINDOW // COMBINE_TILE
    plan = pl.BlockSpec((1, TOP_K, PLAN_WINDOW), lambda w, s: (w, 0, 0))
    tok = lambda w, s: (w * sub + s, 0)
    return pl.pallas_call(
        _combine_kernel,
        out_shape=jax.ShapeDtypeStruct(x1.shape, F32),
        grid=(nw, sub),
        in_specs=[plan, plan,
                  pl.BlockSpec((REGION_ROWS * ROW_SUB, LANES), lambda w, s: (w, 0)),
                  pl.BlockSpec((COMBINE_TILE, ROW_WORDS), tok),
                  pl.BlockSpec((COMBINE_TILE, D_MODEL), tok),
                  pl.BlockSpec((1, 1, D_MODEL), lambda w, s: ((w * PLAN_WINDOW) // tokens_per_gt2, 0, 0)),
                  _full((1, D_MODEL)), _full(ws13.shape), _full(ws2.shape)],
        out_specs=pl.BlockSpec((COMBINE_TILE, D_MODEL), tok),
        scratch_shapes=[pltpu.SMEM((TOP_K, PLAN_WINDOW), jnp.int32),
                        pltpu.SMEM((TOP_K, PLAN_WINDOW), F32),
                        pltpu.VMEM((COMBINE_TILE * ROW_SUB, LANES), F32),
                        pltpu.VMEM((COMBINE_TILE * ROW_SUB, LANES), F32),
                        pltpu.SemaphoreType.DMA((2,))],
        compiler_params=pltpu.CompilerParams(dimension_semantics=("arbitrary", "arbitrary"),
                                             vmem_limit_bytes=VMEM_LIMIT),
        name="combine",
    )(dest_w, wsel_w, y_slots, h2, x1, gt2, gpost, ws13, ws2)


def kernel(x_prompt, x_sample, c_prompt, c_sample, state_pool, state_hgrn, w_ada, b_ada, g_pre_mix, g_post_mix,
           w_in, w_pool, pool_scale, lb_logits, g_out_norm, w_out, g_pre_ffn, g_post_ffn, w_router, router_bias,
           w_exp_gate, w_exp_up, w_exp_down, w_sh_gate, w_sh_up, w_sh_down):
    assert w_ada.shape[0] == 1 and lb_logits.shape[0] == 2, "single-layer trunk"
    bp, lp, d = x_prompt.shape
    bs = x_sample.shape[0]
    row = lambda a: a[0].reshape(1, -1)

    n_mod = bp + bs
    pad = (-n_mod) % 16
    c_all = jnp.concatenate([c_prompt, c_sample, jnp.zeros((pad, d), F32)], axis=0)
    mod = _ada(c_all, w_ada[0], b_ada)
    mod_p = mod[:bp].reshape(bp, 1, 6 * d)
    mod_s = mod[bp:n_mod]

    win = w_in[0].astype(BF16)
    wout = w_out[0].astype(BF16)
    wpool = w_pool[0].astype(BF16)
    wr = jnp.pad(w_router[0], ((0, 0), (0, LANES - N_EXPERTS)))
    wrh = wr.astype(BF16)
    wrl = (wr - wrh.astype(F32)).astype(BF16)
    m2 = jnp.asarray(_M2_NP, BF16)
    gpre, gpost, gffn, gffn_post = row(g_pre_mix), row(g_post_mix), row(g_pre_ffn), row(g_post_ffn)
    pscale, gout = row(pool_scale), row(g_out_norm)
    bias_col = jnp.pad(router_bias[0], (0, LANES - N_EXPERTS)).reshape(LANES, 1)

    x1_p, h2_p, lg_p, pool_p, st_p = _mix_prompt(x_prompt, mod_p, gpre, gpost, gffn, win, wpool, pscale,
                                                  lb_logits, gout, wout, wrh, wrl, m2)
    x1_s, h2_s, lg_s, pool_s, st_s = _mix_sample(x_sample[:, 0, :], mod_s, state_pool[0], state_hgrn[0],
                                                  gpre, gpost, gffn, win, wpool, pscale, lb_logits, gout, wout,
                                                  wrh, wrl)

    experts = (w_exp_gate[0], w_exp_up[0], w_exp_down[0], w_sh_gate[0], w_sh_up[0], w_sh_down[0])
    tp = bp * lp
    nw = tp // PLAN_WINDOW
    assert lp % PLAN_WINDOW == 0
    upper = jnp.asarray(np.triu(np.ones((PLAN_WINDOW, PLAN_WINDOW), np.float32), 1), BF16)
    dest, wsel, cnt, loff = _plan(lg_p.reshape(tp, LANES), bias_col, upper)
    dest_w, wsel_w = dest, wsel
    h2_rows = h2_p.reshape(tp, ROW_WORDS)
    x_slots = _dispatch(dest_w, h2_rows)
    wd_s = _router(lg_s, bias_col)
    y_slots, y_s = _expert_ffn(*_work_items(cnt[:, :, 0], loff[:, :, 0]), x_slots, *experts[:3],
                               h2_s, wd_s, x1_s, mod_s[:, 5 * d:], gffn_post, *experts[3:])
    ws13 = jnp.concatenate([experts[3], experts[4]], axis=1).astype(BF16)
    gt2_p = mod_p[:, :, 5 * d:]
    y_p = _combine(dest_w, wsel_w, y_slots, h2_rows, x1_p.reshape(tp, d), gt2_p, lp, gffn_post,
                   ws13, experts[5].astype(BF16))

    return (y_p.reshape(bp, lp, d), y_s.reshape(bs, 1, d), pool_p[None], st_p[None], pool_s[None], st_s[None])
```

```python
import functools

import numpy as np
import jax
import jax.numpy as jnp
from jax import lax
from jax.experimental import pallas as pl
from jax.experimental.pallas import tpu as pltpu

F32 = jnp.float32
BF16 = jnp.bfloat16

D_MODEL = 1024
POOL_WIDTH = 512
POOL_WINDOWS = (2, 4, 8, 16)
POOL_GROUP = 128
POOL_HIST = 15
HG_WIDTH = 512
HG_HEADS = 4
HG_HEAD_DIM = 128
IN_WIDTH = POOL_WIDTH + 4 * HG_WIDTH
N_EXPERTS = 64
TOP_K = 8
N_GROUPS = 8
TOPK_GROUPS = 4
GROUP_SIZE = N_EXPERTS // N_GROUPS
D_EXPERT = 256
ROUTED_SCALE = 2.5
EPS = 1e-6

LANES = 128
CHUNK = 64
TIME_TILE = 256
ADA_TILE_N = 512
ROUTER_TILE = 512
SAMPLE_STATE_BLOCK = 8
PLAN_WINDOW = 1024
RUN_ALIGN = 32
FFN_TILE = 160
REGION_ROWS = PLAN_WINDOW * TOP_K + N_EXPERTS * (RUN_ALIGN - 1) + FFN_TILE
ROW_WORDS = D_MODEL // 2
ROW_SUB = ROW_WORDS // LANES
COMBINE_TILE = 256
SCATTER_UNROLL = 4
COMBINE_UNROLL = 8
FFN_GROUP = 4
FFN_SPLIT = 1
FFN_RING = 6
VMEM_LIMIT = 58 * 1024 * 1024

_LEVELS = (64, 32, 16, 8, 4, 2)
_BLK_CUM = 0
_BLK_END = 1


def _level_blocks():
    c = CHUNK
    i = np.arange(c)[:, None]
    s = np.arange(c)[None, :]
    blocks = [(s <= i), (s > i)]
    index = {}
    for lvl in _LEVELS:
        seg = (i // lvl) * lvl
        mid = seg + lvl // 2
        ref = mid - 1
        index[lvl] = len(blocks)
        blocks.append((i >= mid) & (s > ref) & (s <= i))
        if lvl > 2:
            blocks.append((i < mid) & (s > i) & (s <= ref))
    m = np.concatenate(blocks, axis=0).astype(np.float32)
    return np.concatenate([m, m], axis=1), index


_M2_NP, _LEVEL_INDEX = _level_blocks()


def _dot(a, b):
    return jnp.dot(a, b, preferred_element_type=F32)


def _dot_nt(a, b):
    return lax.dot_general(a, b, (((1,), (1,)), ((), ())), preferred_element_type=F32)


def _split2(x):
    hi = x.astype(BF16)
    lo = (x - hi.astype(F32)).astype(BF16)
    return hi, lo


def _pack_pair(a, b):
    lo = lax.bitcast_convert_type(a.astype(BF16).astype(F32), jnp.uint32)
    hi = lax.bitcast_convert_type(b.astype(BF16).astype(F32), jnp.uint32)
    return (lo >> 16) | hi


def _unpack_pair(words):
    lo = lax.bitcast_convert_type(words << 16, F32)
    hi = lax.bitcast_convert_type(words & jnp.uint32(0xFFFF0000), F32)
    return lo, hi


def _silu(x):
    return x * jax.nn.sigmoid(x)


def _rms(x, g):
    return x * lax.rsqrt(jnp.mean(x * x, axis=-1, keepdims=True) + EPS) * g


def _mods(m):
    return [m[:, j * D_MODEL:(j + 1) * D_MODEL] for j in range(6)]


def _forget_lower_bound(lbl):
    mx = jnp.max(lbl, axis=0, keepdims=True)
    e = jnp.exp(lbl - mx)
    return e[0:1] / jnp.sum(e, axis=0, keepdims=True)


def _router_logits(h2, wrh, wrl):
    hi, lo = _split2(h2)
    return _dot(hi, wrh) + _dot(lo, wrh) + _dot(hi, wrl)


def _ada_kernel(c_ref, w_ref, b_ref, o_ref):
    a_hi, a_lo = _split2(_silu(c_ref[...]))
    w_hi, w_lo = _split2(w_ref[...])
    o_ref[...] = _dot(a_hi, w_hi) + _dot(a_lo, w_hi) + _dot(a_hi, w_lo) + b_ref[...]


def _ada(c_all, w_ada, b_ada):
    rows = c_all.shape[0]
    n = w_ada.shape[1]
    return pl.pallas_call(
        _ada_kernel,
        out_shape=jax.ShapeDtypeStruct((rows, n), F32),
        grid=(n // ADA_TILE_N,),
        in_specs=[pl.BlockSpec((rows, D_MODEL), lambda j: (0, 0)),
                  pl.BlockSpec((D_MODEL, ADA_TILE_N), lambda j: (0, j)),
                  pl.BlockSpec((1, ADA_TILE_N), lambda j: (0, j))],
        out_specs=pl.BlockSpec((rows, ADA_TILE_N), lambda j: (0, j)),
        compiler_params=pltpu.CompilerParams(dimension_semantics=("parallel",),
                                             vmem_limit_bytes=VMEM_LIMIT),
        name="ada",
    )(c_all, w_ada, b_ada)


def _level_masks():
    i = lax.broadcasted_iota(jnp.int32, (CHUNK, 2 * CHUNK), 0)
    j = lax.broadcasted_iota(jnp.int32, (CHUNK, 2 * CHUNK), 1) & (CHUNK - 1)
    masks = {}
    for lvl in _LEVELS:
        sh = lvl.bit_length() - 1
        same = (i >> sh) == (j >> sh)
        upper = ((i >> (sh - 1)) & 1) == 1
        lower = ((j >> (sh - 1)) & 1) == 0
        masks[lvl] = same & upper & lower
    return masks, i == j


def _by_head(m):
    z = jnp.zeros((m.shape[0], HG_HEAD_DIM), m.dtype)
    return jnp.concatenate([jnp.concatenate([m[:, :HG_HEAD_DIM], z], axis=1),
                            jnp.concatenate([z, m[:, HG_HEAD_DIM:]], axis=1)], axis=0)


def _hgrn_chunk_pair(q, k, v, e_all, lane0, sts, masks, eye):
    hd = HG_HEAD_DIM

    def blk(n):
        return e_all[n * CHUNK:(n + 1) * CHUNK, lane0:lane0 + 2 * hd]

    b = blk(_BLK_CUM)
    kb = k.astype(BF16)
    a = jnp.where(eye, _dot_nt(q.astype(BF16), _by_head(kb)), 0.0)
    for lvl in _LEVELS:
        n = _LEVEL_INDEX[lvl]
        ql = (q * jnp.exp(blk(n))).astype(BF16)
        kl = (k * jnp.exp(blk(n + 1))).astype(BF16) if lvl > 2 else kb
        a = a + jnp.where(masks[lvl], _dot_nt(ql, _by_head(kl)), 0.0)
    st_both = jnp.concatenate([jnp.concatenate([sts[0], jnp.zeros_like(sts[0])], axis=1),
                               jnp.concatenate([jnp.zeros_like(sts[1]), sts[1]], axis=1)], axis=0).astype(BF16)
    o = _dot(a.astype(BF16), _by_head(v.astype(BF16))) + _dot_nt((q * jnp.exp(b)).astype(BF16), st_both)
    k_end = (k * jnp.exp(blk(_BLK_END))).astype(BF16)
    decay = jnp.exp(b[CHUNK - 1:CHUNK, :])
    new = []
    for h in range(2):
        lanes = slice(h * hd, (h + 1) * hd)
        new.append(sts[h] * decay[:, lanes] + _dot(v[:, lanes].T.astype(BF16), k_end[:, lanes]))
    return o, new


def _mix_prompt_kernel(x_ref, mod_ref, gpre_ref, gpost_ref, gffn_ref, win_ref, wpool_ref, pscale_ref,
                       lbl_ref, gout_ref, wout_ref, wrh_ref, wrl_ref, m2_ref,
                       x1_ref, h2_ref, lg_ref, pool_ref, st_ref,
                       st_s, ubuf, q_s, k_s, v_s, g_s, o_s):
    t = pl.program_id(1)
    n_t = pl.num_programs(1)
    tt = TIME_TILE

    @pl.when(t == 0)
    def _():
        st_s[...] = jnp.zeros_like(st_s)
        ubuf[0:16, :] = jnp.zeros((16, POOL_WIDTH), F32)

    xt = x_ref[0]
    sh1, sc1, gt1, sh2, sc2, gt2 = _mods(mod_ref[0])
    h = _rms(xt, gpre_ref[...]) * (1.0 + sc1) + sh1
    proj = _dot(h.astype(BF16), win_ref[...])

    u = proj[:, :POOL_WIDTH]
    ubuf[16:16 + tt, :] = u
    pos = (t * tt + lax.broadcasted_iota(jnp.int32, (tt, 1), 0) + 1).astype(F32)
    ys = []
    for g, w in enumerate(POOL_WINDOWS):
        s = ubuf[:, g * POOL_GROUP:(g + 1) * POOL_GROUP]
        off = 0
        for step in range(w.bit_length() - 1):
            sh = 1 << step
            s = s[sh:, :] + s[:-sh, :]
            off += sh
        ws = s[16 - off:16 - off + tt, :]
        cnt = jnp.minimum(pos, float(w))
        d = ws / cnt - u[:, g * POOL_GROUP:(g + 1) * POOL_GROUP]
        ys.append(_dot(d.astype(BF16), wpool_ref[g]))
    y_pool = jnp.concatenate(ys, axis=1) * pscale_ref[...]
    ubuf[0:16, :] = ubuf[tt:tt + 16, :]

    @pl.when(t == n_t - 1)
    def _():
        pool_ref[0] = u[tt - POOL_HIST:, :]

    lb = _forget_lower_bound(lbl_ref[...])
    f = lb + (1.0 - lb) * jax.nn.sigmoid(proj[:, POOL_WIDTH + HG_WIDTH:POOL_WIDTH + 2 * HG_WIDTH])
    q_s[...] = _silu(proj[:, POOL_WIDTH:POOL_WIDTH + HG_WIDTH])
    k_s[...] = 1.0 - f
    v_s[...] = proj[:, POOL_WIDTH + 2 * HG_WIDTH:POOL_WIDTH + 3 * HG_WIDTH]
    g_s[...] = jnp.log(f)
    gate = _silu(proj[:, POOL_WIDTH + 3 * HG_WIDTH:])

    masks, eye = _level_masks()
    m2 = m2_ref[...]

    states = [st_s[hd] for hd in range(HG_HEADS)]
    for c in range(tt // CHUNK):
        rows = slice(c * CHUNK, (c + 1) * CHUNK)
        g_hi, g_lo = _split2(g_s[rows, :])
        e_all = _dot(m2, jnp.concatenate([g_hi, g_lo], axis=0))
        for hd in range(0, HG_HEADS, 2):
            lane0 = hd * HG_HEAD_DIM
            lanes = slice(lane0, lane0 + 2 * HG_HEAD_DIM)
            o, states[hd:hd + 2] = _hgrn_chunk_pair(q_s[rows, lanes], k_s[rows, lanes], v_s[rows, lanes], e_all,
                                                    lane0, states[hd:hd + 2], masks, eye)
            o_s[rows, lanes] = o
    for hd in range(HG_HEADS):
        st_s[hd] = states[hd]

    o = o_s[...]
    os_ = []
    for hd in range(HG_HEADS):
        oh = o[:, hd * HG_HEAD_DIM:(hd + 1) * HG_HEAD_DIM]
        os_.append(oh * lax.rsqrt(jnp.mean(oh * oh, axis=-1, keepdims=True) + EPS) * gout_ref[...])
    o_n = jnp.concatenate(os_, axis=1) * gate

    mix = _dot(y_pool.astype(BF16), wout_ref[0:POOL_WIDTH, :]) + _dot(o_n.astype(BF16), wout_ref[POOL_WIDTH:, :])
    x1 = xt + gt1 * _rms(mix, gpost_ref[...])
    h2 = _rms(x1, gffn_ref[...]) * (1.0 + sc2) + sh2
    x1_ref[0] = x1
    h2_ref[0] = _pack_pair(h2[:, :ROW_WORDS], h2[:, ROW_WORDS:])
    lg_ref[0] = _router_logits(h2, wrh_ref[...], wrl_ref[...])

    @pl.when(t == n_t - 1)
    def _():
        for hd in range(HG_HEADS):
            st_ref[0, hd] = st_s[hd].T


def _full(shape):
    nd = len(shape)
    return pl.BlockSpec(shape, lambda *_: (0,) * nd)


def _mix_prompt(x, mod, gpre, gpost, gffn, win, wpool, pscale, lbl, gout, wout, wrh, wrl, m2):
    b, l, d = x.shape
    tt = TIME_TILE
    tile = lambda i, j: (i, j, 0)
    per_b = lambda i, j: (i, 0, 0)
    return pl.pallas_call(
        _mix_prompt_kernel,
        out_shape=(jax.ShapeDtypeStruct((b, l, d), F32),
                   jax.ShapeDtypeStruct((b, l, ROW_WORDS), jnp.uint32),
                   jax.ShapeDtypeStruct((b, l, LANES), F32),
                   jax.ShapeDtypeStruct((b, POOL_HIST, POOL_WIDTH), F32),
                   jax.ShapeDtypeStruct((b, HG_HEADS, HG_HEAD_DIM, HG_HEAD_DIM), F32)),
        grid=(b, l // tt),
        in_specs=[pl.BlockSpec((1, tt, d), tile),
                  pl.BlockSpec((1, 1, 6 * d), per_b),
                  _full((1, d)), _full((1, d)), _full((1, d)),
                  _full(win.shape), _full(wpool.shape), _full((1, POOL_WIDTH)),
                  _full(lbl.shape), _full((1, HG_HEAD_DIM)), _full(wout.shape),
                  _full(wrh.shape), _full(wrl.shape), _full(m2.shape)],
        out_specs=(pl.BlockSpec((1, tt, d), tile),
                   pl.BlockSpec((1, tt, ROW_WORDS), tile),
                   pl.BlockSpec((1, tt, LANES), tile),
                   pl.BlockSpec((1, POOL_HIST, POOL_WIDTH), per_b),
                   pl.BlockSpec((1, HG_HEADS, HG_HEAD_DIM, HG_HEAD_DIM), lambda i, j: (i, 0, 0, 0))),
        scratch_shapes=[pltpu.VMEM((HG_HEADS, HG_HEAD_DIM, HG_HEAD_DIM), F32),
                        pltpu.VMEM((tt + 16, POOL_WIDTH), F32),
                        pltpu.VMEM((tt, HG_WIDTH), F32), pltpu.VMEM((tt, HG_WIDTH), F32),
                        pltpu.VMEM((tt, HG_WIDTH), F32), pltpu.VMEM((tt, HG_WIDTH), F32),
                        pltpu.VMEM((tt, HG_WIDTH), F32)],
        compiler_params=pltpu.CompilerParams(dimension_semantics=("parallel", "arbitrary"),
                                             vmem_limit_bytes=VMEM_LIMIT),
        name="mix_prompt",
    )(x, mod, gpre, gpost, gffn, win, wpool, pscale, lbl, gout, wout, wrh, wrl, m2)


def _mix_sample_in_kernel(x_ref, mod_ref, gpre_ref, win_ref, wpool_ref, pscale_ref, lbl_ref, hist_ref,
                          ypool_ref, npool_ref, ft_ref, qt_ref, v_ref, gate_ref):
    xt = x_ref[...]
    sh1, sc1 = _mods(mod_ref[...])[:2]
    h = _rms(xt, gpre_ref[...]) * (1.0 + sc1) + sh1
    proj = _dot(h.astype(BF16), win_ref[...])
    u = proj[:, :POOL_WIDTH]
    row = lax.broadcasted_iota(jnp.int32, (hist_ref.shape[0], POOL_HIST, POOL_GROUP), 1)
    ys = []
    for g, w in enumerate(POOL_WINDOWS):
        sl = slice(g * POOL_GROUP, (g + 1) * POOL_GROUP)
        past = jnp.sum(jnp.where(row >= POOL_HIST - (w - 1), hist_ref[:, :, sl], 0.0), axis=1)
        ug = u[:, sl]
        d = (past + ug) / float(w) - ug
        ys.append(_dot(d.astype(BF16), wpool_ref[g]))
    ypool_ref[...] = jnp.concatenate(ys, axis=1) * pscale_ref[...]
    npool_ref[:, 0:POOL_HIST - 1, :] = hist_ref[:, 1:POOL_HIST, :]
    npool_ref[:, POOL_HIST - 1, :] = u

    lb = _forget_lower_bound(lbl_ref[...])
    f = lb + (1.0 - lb) * jax.nn.sigmoid(proj[:, POOL_WIDTH + HG_WIDTH:POOL_WIDTH + 2 * HG_WIDTH])
    ft_ref[...] = f.T
    qt_ref[...] = _silu(proj[:, POOL_WIDTH:POOL_WIDTH + HG_WIDTH]).T
    v_ref[...] = proj[:, POOL_WIDTH + 2 * HG_WIDTH:POOL_WIDTH + 3 * HG_WIDTH]
    gate_ref[...] = _silu(proj[:, POOL_WIDTH + 3 * HG_WIDTH:])


def _mix_sample_state_kernel(s_ref, ft_ref, qt_ref, v_ref, snew_ref, o_ref):
    i = pl.program_id(0)
    lane = lax.broadcasted_iota(jnp.int32, (HG_HEAD_DIM, ft_ref.shape[1]), 1)
    for j in range(SAMPLE_STATE_BLOCK):
        mine = lane == i * SAMPLE_STATE_BLOCK + j
        for hd in range(HG_HEADS):
            r0 = hd * HG_HEAD_DIM
            f = jnp.sum(jnp.where(mine, ft_ref[r0:r0 + HG_HEAD_DIM, :], 0.0), axis=1, keepdims=True)
            q = jnp.sum(jnp.where(mine, qt_ref[r0:r0 + HG_HEAD_DIM, :], 0.0), axis=1, keepdims=True)
            v = v_ref[j:j + 1, r0:r0 + HG_HEAD_DIM]
            s_new = f * s_ref[j, hd] + (1.0 - f) * v
            snew_ref[j, hd] = s_new
            o_ref[j:j + 1, r0:r0 + HG_HEAD_DIM] = jnp.sum(q * s_new, axis=0, keepdims=True)


def _mix_sample_out_kernel(x_ref, mod_ref, o_ref, gate_ref, ypool_ref, gout_ref, wout_ref, gpost_ref,
                           gffn_ref, wrh_ref, wrl_ref, x1_ref, h2_ref, lg_ref):
    _, _, gt1, sh2, sc2, _ = _mods(mod_ref[...])
    o = o_ref[...]
    os_ = []
    for hd in range(HG_HEADS):
        oh = o[:, hd * HG_HEAD_DIM:(hd + 1) * HG_HEAD_DIM]
        os_.append(oh * lax.rsqrt(jnp.mean(oh * oh, axis=-1, keepdims=True) + EPS) * gout_ref[...])
    o_n = jnp.concatenate(os_, axis=1) * gate_ref[...]
    mix = (_dot(ypool_ref[...].astype(BF16), wout_ref[0:POOL_WIDTH, :])
           + _dot(o_n.astype(BF16), wout_ref[POOL_WIDTH:, :]))
    x1 = x_ref[...] + gt1 * _rms(mix, gpost_ref[...])
    h2 = _rms(x1, gffn_ref[...]) * (1.0 + sc2) + sh2
    x1_ref[...] = x1
    h2_ref[...] = h2.astype(BF16)
    lg_ref[...] = _router_logits(h2, wrh_ref[...], wrl_ref[...])


def _mix_sample(x, mod, hist, state, gpre, gpost, gffn, win, wpool, pscale, lbl, gout, wout, wrh, wrl):
    b = x.shape[0]
    bb = SAMPLE_STATE_BLOCK
    cp = pltpu.CompilerParams(vmem_limit_bytes=VMEM_LIMIT)
    ypool, npool, ft, qt, v, gate = pl.pallas_call(
        _mix_sample_in_kernel,
        out_shape=(jax.ShapeDtypeStruct((b, POOL_WIDTH), F32),
                   jax.ShapeDtypeStruct((b, POOL_HIST, POOL_WIDTH), F32),
                   jax.ShapeDtypeStruct((HG_WIDTH, b), F32),
                   jax.ShapeDtypeStruct((HG_WIDTH, b), F32),
                   jax.ShapeDtypeStruct((b, HG_WIDTH), F32),
                   jax.ShapeDtypeStruct((b, HG_WIDTH), F32)),
        compiler_params=cp,
        name="mix_sample_in",
    )(x, mod, gpre, win, wpool, pscale, lbl, hist)

    s_spec = pl.BlockSpec((bb, HG_HEADS, HG_HEAD_DIM, HG_HEAD_DIM), lambda i: (i, 0, 0, 0))
    col_spec = _full((HG_WIDTH, b))
    row_spec = pl.BlockSpec((bb, HG_WIDTH), lambda i: (i, 0))
    s_new, o = pl.pallas_call(
        _mix_sample_state_kernel,
        out_shape=(jax.ShapeDtypeStruct(state.shape, F32), jax.ShapeDtypeStruct((b, HG_WIDTH), F32)),
        grid=(b // bb,),
        in_specs=[s_spec, col_spec, col_spec, row_spec],
        out_specs=(s_spec, row_spec),
        compiler_params=pltpu.CompilerParams(dimension_semantics=("parallel",), vmem_limit_bytes=VMEM_LIMIT),
        name="mix_sample_state",
    )(state, ft, qt, v)

    x1, h2, lg = pl.pallas_call(
        _mix_sample_out_kernel,
        out_shape=(jax.ShapeDtypeStruct((b, D_MODEL), F32),
                   jax.ShapeDtypeStruct((b, D_MODEL), BF16),
                   jax.ShapeDtypeStruct((b, LANES), F32)),
        compiler_params=cp,
        name="mix_sample_out",
    )(x, mod, o, gate, ypool, gout, wout, gpost, gffn, wrh, wrl)
    return x1, h2, lg, npool, s_new


def _first_max(cur, idx, big):
    m = jnp.max(cur, axis=0, keepdims=True)
    first = jnp.min(jnp.where(cur == m, idx, big), axis=0, keepdims=True)
    return idx == first


def _route(logit, bias):
    n = logit.shape[1]
    scores = jax.nn.sigmoid(logit)
    sel = scores + bias
    neg = -jnp.inf

    sub = lax.broadcasted_iota(jnp.int32, (GROUP_SIZE, n), 0)
    gscore = []
    for g in range(N_GROUPS):
        sg = sel[g * GROUP_SIZE:(g + 1) * GROUP_SIZE, :]
        m1 = jnp.max(sg, axis=0, keepdims=True)
        rest = jnp.where(_first_max(sg, sub, GROUP_SIZE), neg, sg)
        gscore.append(m1 + jnp.max(rest, axis=0, keepdims=True))
    cur = jnp.concatenate(gscore, axis=0)
    gidx = lax.broadcasted_iota(jnp.int32, (N_GROUPS, n), 0)
    gmask = jnp.zeros((N_GROUPS, n), jnp.bool_)
    for _ in range(TOPK_GROUPS):
        pick = _first_max(cur, gidx, N_GROUPS)
        gmask = gmask | pick
        cur = jnp.where(pick, neg, cur)
    emask = jnp.concatenate(
        [jnp.broadcast_to(gmask[g:g + 1, :], (GROUP_SIZE, n)) for g in range(N_GROUPS)], axis=0)

    cur = jnp.where(emask, sel, neg)
    eidx = lax.broadcasted_iota(jnp.int32, (N_EXPERTS, n), 0)
    picks = []
    for _ in range(TOP_K):
        pick = _first_max(cur, eidx, N_EXPERTS)
        picks.append(pick)
        cur = jnp.where(pick, neg, cur)
    return scores, picks


def _any(masks):
    return functools.reduce(jnp.logical_or, masks)


def _router_kernel(lg_ref, bias_ref, wd_ref):
    n = lg_ref.shape[0]
    scores, picks = _route(lg_ref[...].T[0:N_EXPERTS, :], bias_ref[0:N_EXPERTS, :])
    chosen = _any(picks)
    wsum = jnp.sum(jnp.where(chosen, scores, 0.0), axis=0, keepdims=True)
    wd = jnp.where(chosen, scores / wsum * ROUTED_SCALE, 0.0)
    wd_ref[...] = jnp.concatenate([wd, jnp.zeros((LANES - N_EXPERTS, n), F32)], axis=0).T


def _router(logits, bias_col):
    t = logits.shape[0]
    tile = min(ROUTER_TILE, t)
    return pl.pallas_call(
        _router_kernel,
        out_shape=jax.ShapeDtypeStruct((t, LANES), F32),
        grid=(t // tile,),
        in_specs=[pl.BlockSpec((tile, LANES), lambda i: (i, 0)), _full(bias_col.shape)],
        out_specs=pl.BlockSpec((tile, LANES), lambda i: (i, 0)),
        compiler_params=pltpu.CompilerParams(dimension_semantics=("parallel",), vmem_limit_bytes=VMEM_LIMIT),
        name="router",
    )(logits, bias_col)


def _plan_kernel(lg_ref, bias_ref, upper_ref, dest_ref, wsel_ref, cnt_ref, loff_ref):
    scores, picks = _route(lg_ref[...].T[0:N_EXPERTS, :], bias_ref[0:N_EXPERTS, :])
    chosen = _any(picks)
    chosen_f = jnp.where(chosen, 1.0, 0.0)
    cnt = jnp.sum(chosen_f, axis=1, keepdims=True)
    units = jnp.floor((cnt + (RUN_ALIGN - 1)) / RUN_ALIGN)
    ei = lax.broadcasted_iota(jnp.int32, (N_EXPERTS, N_EXPERTS), 0)
    ej = lax.broadcasted_iota(jnp.int32, (N_EXPERTS, N_EXPERTS), 1)
    before = jnp.where(ej < ei, 1.0, 0.0).astype(BF16)
    loff = RUN_ALIGN * _dot(before, jnp.broadcast_to(units, (N_EXPERTS, LANES)).astype(BF16))[:, 0:1]
    slot = loff + _dot(chosen_f.astype(BF16), upper_ref[...])
    wsum = jnp.sum(jnp.where(chosen, scores, 0.0), axis=0, keepdims=True)
    pick_sum = lambda v: jnp.concatenate(
        [jnp.sum(jnp.where(p, v, 0.0), axis=0, keepdims=True) for p in picks], axis=0)
    dest_ref[0] = (pick_sum(slot) * ROW_SUB).astype(jnp.int32)
    wsel_ref[0] = pick_sum(scores) / wsum * ROUTED_SCALE
    cnt_ref[0] = jnp.broadcast_to(cnt, (N_EXPERTS, LANES))
    loff_ref[0] = jnp.broadcast_to(loff, (N_EXPERTS, LANES))


def _plan(logits, bias_col, upper):
    t = logits.shape[0]
    nw = t // PLAN_WINDOW
    per_w = pl.BlockSpec((1, N_EXPERTS, LANES), lambda w: (w, 0, 0))
    picks = pl.BlockSpec((1, TOP_K, PLAN_WINDOW), lambda w: (w, 0, 0))
    return pl.pallas_call(
        _plan_kernel,
        out_shape=(jax.ShapeDtypeStruct((nw, TOP_K, PLAN_WINDOW), jnp.int32),
                   jax.ShapeDtypeStruct((nw, TOP_K, PLAN_WINDOW), F32),
                   jax.ShapeDtypeStruct((nw, N_EXPERTS, LANES), F32),
                   jax.ShapeDtypeStruct((nw, N_EXPERTS, LANES), F32)),
        grid=(nw,),
        in_specs=[pl.BlockSpec((PLAN_WINDOW, LANES), lambda w: (w, 0)), _full(bias_col.shape), _full(upper.shape)],
        out_specs=(picks, picks, per_w, per_w),
        compiler_params=pltpu.CompilerParams(dimension_semantics=("parallel",), vmem_limit_bytes=VMEM_LIMIT),
        name="plan",
    )(logits, bias_col, upper)


def _slot_rows(slot):
    return pl.ds(pl.multiple_of(slot * ROW_SUB, ROW_SUB), ROW_SUB)


def _table_copies(table_ref, flat_s, sem):
    return [pltpu.make_async_copy(table_ref.at[0, k], flat_s.at[pl.ds(k * PLAN_WINDOW, PLAN_WINDOW)], sem)
            for k in range(TOP_K)]


def _rows_at(offset):
    return pl.ds(pl.multiple_of(offset, ROW_SUB), ROW_SUB)


def _dispatch_kernel(dest_ref, h_ref, x_ref, dest_s, rows_s, sem):
    plan_copies = _table_copies(dest_ref, dest_s, sem)
    for c in plan_copies:
        c.start()
    x_ref[...] = jnp.zeros_like(x_ref)
    for j in range(ROW_SUB):
        rows_s[pl.ds(j, PLAN_WINDOW, stride=ROW_SUB), :] = h_ref[:, j * LANES:(j + 1) * LANES]
    for c in plan_copies:
        c.wait()

    def body(i, carry):
        for u in range(SCATTER_UNROLL):
            t = i * SCATTER_UNROLL + u
            row = rows_s[_slot_rows(t), :]
            for k in range(TOP_K):
                x_ref[_rows_at(dest_s[k * PLAN_WINDOW + t]), :] = row
        return carry

    lax.fori_loop(0, PLAN_WINDOW // SCATTER_UNROLL, body, 0)


def _dispatch(dest_w, h2):
    nw = dest_w.shape[0]
    region = REGION_ROWS * ROW_SUB
    return pl.pallas_call(
        _dispatch_kernel,
        out_shape=jax.ShapeDtypeStruct((nw * region, LANES), jnp.uint32),
        grid=(nw,),
        in_specs=[pl.BlockSpec((1, TOP_K, PLAN_WINDOW), lambda w: (w, 0, 0)),
                  pl.BlockSpec((PLAN_WINDOW, ROW_WORDS), lambda w: (w, 0))],
        out_specs=pl.BlockSpec((region, LANES), lambda w: (w, 0)),
        scratch_shapes=[pltpu.SMEM((TOP_K * PLAN_WINDOW,), jnp.int32),
                        pltpu.VMEM((PLAN_WINDOW * ROW_SUB, LANES), jnp.uint32),
                        pltpu.SemaphoreType.DMA],
        compiler_params=pltpu.CompilerParams(dimension_semantics=("arbitrary",), vmem_limit_bytes=VMEM_LIMIT),
        name="dispatch",
    )(dest_w, h2)


def _unpack_rows(ref, n):
    parts = [_unpack_pair(ref[pl.ds(j, n, stride=ROW_SUB), :]) for j in range(ROW_SUB)]
    return jnp.concatenate([p[0] for p in parts] + [p[1] for p in parts], axis=1)


def _ffn_kernel(gq_ref, gs_ref, to_ref, tn_ref, x_hbm, w1_ref, w3_ref, w2_ref,
                hs_ref, wds_ref, x1s_ref, gt2s_ref, gpost_ref, ws1_ref, ws3_ref, ws2_ref, y_hbm, ys_ref,
                w13_s, w2_s, xbuf, ybuf, rsem, wsem, acc_s):
    del x_hbm
    e = pl.program_id(0)
    n_groups = gq_ref[N_EXPERTS]
    block = RUN_ALIGN * ROW_SUB
    tile = FFN_TILE * ROW_SUB

    def go(cp, start):
        if start:
            cp.start()
        else:
            cp.wait()

    def fetch(q, start):
        def one(j):
            src = y_hbm.at[pl.ds(pl.multiple_of(to_ref[gs_ref[q] + j] * block, block), tile)]
            go(pltpu.make_async_copy(src, xbuf.at[q % FFN_RING, pl.ds(j * tile, tile)], rsem.at[q % FFN_RING]), start)

        size = gs_ref[q + 1] - gs_ref[q]

        @pl.when(size == FFN_GROUP)
        def _():
            for j in range(FFN_GROUP):
                one(j)

        @pl.when(size < FFN_GROUP)
        def _():
            for j in range(FFN_GROUP - 1):
                pl.when(j < size)(functools.partial(one, j))

    def writeback(q, start):
        most = FFN_TILE // RUN_ALIGN - 1

        def blocks(j, b, n):
            dst = pl.multiple_of(to_ref[gs_ref[q] + j] * block, block) + b * block
            go(pltpu.make_async_copy(ybuf.at[q % FFN_RING, pl.ds(j * tile + b * block, n * block)],
                                     y_hbm.at[pl.ds(dst, n * block)], wsem.at[q % FFN_RING]), start)

        for j in range(FFN_GROUP):
            item = gs_ref[q] + j
            n = jnp.where(item < gs_ref[q + 1], tn_ref[item], 0)
            pl.when(n >= most)(functools.partial(blocks, j, 0, most))
            pl.when(n > most)(functools.partial(blocks, j, most, 1))

            @pl.when((n > 0) & (n < most))
            def _():
                for b in range(most - 1):
                    pl.when(b < n)(functools.partial(blocks, j, b, 1))

    @pl.when(e == 0)
    def _():
        xbuf[...] = jnp.zeros_like(xbuf)
        for d in range(FFN_RING - 1):
            pl.when(d < n_groups)(functools.partial(fetch, d, True))

    w13_s[:, 0:D_EXPERT] = w1_ref[0].astype(BF16)
    w13_s[:, D_EXPERT:] = w3_ref[0].astype(BF16)
    w2_s[...] = w2_ref[0].astype(BF16)

    xs = hs_ref[...]

    @pl.when(e == 0)
    def _():
        sh = _silu(_dot(xs, ws1_ref[...].astype(BF16))) * _dot(xs, ws3_ref[...].astype(BF16))
        acc_s[...] = _dot(sh.astype(BF16), ws2_ref[...].astype(BF16))

    lane = lax.broadcasted_iota(jnp.int32, wds_ref.shape, 1)
    wcol = jnp.sum(jnp.where(lane == e, wds_ref[...], 0.0), axis=1, keepdims=True)
    hd = _dot(xs, w13_s[...])
    acc_s[...] += _dot((_silu(hd[:, :D_EXPERT]) * hd[:, D_EXPERT:] * wcol).astype(BF16), w2_s[...])

    @pl.when(e == pl.num_programs(0) - 1)
    def _():
        ys_ref[...] = x1s_ref[...] + gt2s_ref[...] * _rms(acc_s[...], gpost_ref[...])

    def group_body(q, carry):
        @pl.when(q + FFN_RING - 1 < n_groups)
        def _():
            fetch(q + FFN_RING - 1, True)

        @pl.when(q >= FFN_RING)
        def _():
            writeback(q - FFN_RING, False)

        fetch(q, False)
        rows = FFN_GROUP * FFN_TILE // FFN_SPLIT
        for part in range(FFN_SPLIT):
            span = pl.ds(part * rows * ROW_SUB, rows * ROW_SUB)
            x = _unpack_rows(xbuf.at[q % FFN_RING, span], rows).astype(BF16)
            h = _dot(x, w13_s[...])
            act = _silu(h[:, :D_EXPERT]) * h[:, D_EXPERT:]
            y = _dot(act.astype(BF16), w2_s[...])
            out = ybuf.at[q % FFN_RING, span]
            for j in range(ROW_SUB):
                out[pl.ds(j, rows, stride=ROW_SUB), :] = _pack_pair(
                    y[:, j * LANES:(j + 1) * LANES], y[:, ROW_WORDS + j * LANES:ROW_WORDS + (j + 1) * LANES])
        writeback(q, True)
        return carry

    lax.fori_loop(gq_ref[e], gq_ref[e + 1], group_body, 0)

    @pl.when(e == pl.num_programs(0) - 1)
    def _():
        for d in range(FFN_RING):
            @pl.when(n_groups - 1 - d >= 0)
            def _():
                writeback(n_groups - 1 - d, False)


def _expert_ffn(group_of_expert, group_start, item_off, item_blocks, x_slots, w1, w3, w2,
                h2_s, wd_s, x1_s, gt2_s, gpost, ws1, ws3, ws2):
    of_expert = lambda e, *_: (e, 0, 0)
    whole = lambda a: pl.BlockSpec(a.shape, lambda e, *_: (0,) * a.ndim)
    tiles = pltpu.VMEM((FFN_RING, FFN_GROUP * FFN_TILE * ROW_SUB, LANES), jnp.uint32)
    sems = pltpu.SemaphoreType.DMA((FFN_RING,))
    dense = (h2_s, wd_s, x1_s, gt2_s, gpost, ws1, ws3, ws2)
    grid_spec = pltpu.PrefetchScalarGridSpec(
        num_scalar_prefetch=4,
        grid=(N_EXPERTS,),
        in_specs=[pl.BlockSpec(memory_space=pl.ANY),
                  pl.BlockSpec((1, D_MODEL, D_EXPERT), of_expert),
                  pl.BlockSpec((1, D_MODEL, D_EXPERT), of_expert),
                  pl.BlockSpec((1, D_EXPERT, D_MODEL), of_expert)] + [whole(a) for a in dense],
        out_specs=(pl.BlockSpec(memory_space=pl.ANY), whole(x1_s)),
        scratch_shapes=[pltpu.VMEM((D_MODEL, 2 * D_EXPERT), BF16), pltpu.VMEM((D_EXPERT, D_MODEL), BF16),
                        tiles, tiles, sems, sems, pltpu.VMEM(x1_s.shape, F32)])
    return pl.pallas_call(
        _ffn_kernel,
        out_shape=(jax.ShapeDtypeStruct(x_slots.shape, jnp.uint32), jax.ShapeDtypeStruct(x1_s.shape, F32)),
        grid_spec=grid_spec,
        input_output_aliases={4: 0},
        compiler_params=pltpu.CompilerParams(dimension_semantics=("arbitrary",), vmem_limit_bytes=VMEM_LIMIT),
        name="expert_ffn",
    )(group_of_expert, group_start, item_off, item_blocks, x_slots, w1, w3, w2, *dense)


def _work_items(cnt, loff):
    nw = cnt.shape[0]
    cnt = cnt.astype(jnp.int32).T.reshape(-1)
    loff = loff.astype(jnp.int32).T.reshape(-1)
    region = jnp.tile(jnp.arange(nw, dtype=jnp.int32) * REGION_ROWS, N_EXPERTS)
    n_items = nw * N_EXPERTS + (nw * PLAN_WINDOW * TOP_K) // FFN_TILE
    tiles = (cnt + FFN_TILE - 1) // FFN_TILE
    ends = jnp.cumsum(tiles)
    starts = ends - tiles
    i = jnp.arange(n_items + FFN_GROUP, dtype=jnp.int32)[:, None]
    mine = (starts[None, :] <= i) & (i < ends[None, :])
    of_pair = lambda v: jnp.sum(jnp.where(mine, v[None, :], 0), axis=1)
    r = i[:, 0] - of_pair(starts)
    row = of_pair(region + loff) + r * FFN_TILE
    own = jnp.clip(of_pair(cnt) - r * FFN_TILE, 0, FFN_TILE)

    zero = jnp.zeros((1,), jnp.int32)
    item_end = ends.reshape(N_EXPERTS, nw)[:, -1]
    item_start = jnp.concatenate([zero, item_end[:-1]])
    group_end = jnp.cumsum((item_end - item_start + FFN_GROUP - 1) // FFN_GROUP)
    group_first = jnp.concatenate([zero, group_end[:-1]])
    q = jnp.arange(N_EXPERTS + n_items // FFN_GROUP + 1, dtype=jnp.int32)[:, None]
    has = (group_first[None, :] <= q) & (q < group_end[None, :])
    of_expert = lambda v: jnp.sum(jnp.where(has, v[None, :], 0), axis=1)
    group_start = jnp.where(q[:, 0] < group_end[-1],
                            of_expert(item_start) + (q[:, 0] - of_expert(group_first)) * FFN_GROUP, item_end[-1])
    return (jnp.concatenate([zero, group_end]), group_start, row // RUN_ALIGN, (own + RUN_ALIGN - 1) // RUN_ALIGN)


def _combine_kernel(dest_ref, wsel_ref, y_ref, h_ref, x1_ref, gt2_ref, gpost_ref, ws13_ref, ws2_ref, out_ref,
                    dest_s, wsel_s, lo_s, hi_s, sem):
    s = pl.program_id(1)

    @pl.when(s == 0)
    def _():
        copies = _table_copies(dest_ref, dest_s, sem.at[0]) + _table_copies(wsel_ref, wsel_s, sem.at[1])
        for c in copies:
            c.start()
        for c in copies:
            c.wait()

    base = s * COMBINE_TILE

    def body(i, carry):
        for u in range(COMBINE_UNROLL):
            t = i * COMBINE_UNROLL + u
            acc_lo = jnp.zeros((ROW_SUB, LANES), F32)
            acc_hi = jnp.zeros((ROW_SUB, LANES), F32)
            for k in range(TOP_K):
                at = k * PLAN_WINDOW + base + t
                lo, hi = _unpack_pair(y_ref[_rows_at(dest_s[at]), :])
                w = wsel_s[at]
                acc_lo = acc_lo + w * lo
                acc_hi = acc_hi + w * hi
            lo_s[_slot_rows(t), :] = acc_lo
            hi_s[_slot_rows(t), :] = acc_hi
        return carry

    lax.fori_loop(0, COMBINE_TILE // COMBINE_UNROLL, body, 0)
    rows = lambda ref: [ref[pl.ds(j, COMBINE_TILE, stride=ROW_SUB), :] for j in range(ROW_SUB)]
    routed = jnp.concatenate(rows(lo_s) + rows(hi_s), axis=1)
    h_lo, h_hi = _unpack_pair(h_ref[...])
    x = jnp.concatenate([h_lo, h_hi], axis=1).astype(BF16)
    hs = _dot(x, ws13_ref[...])
    act = _silu(hs[:, :D_EXPERT]) * hs[:, D_EXPERT:]
    ff = routed + _dot(act.astype(BF16), ws2_ref[...])
    out_ref[...] = x1_ref[...] + gt2_ref[0] * _rms(ff, gpost_ref[...])


def _combine(dest_w, wsel_w, y_slots, h2, x1, gt2, tokens_per_gt2, gpost, ws13, ws2):
    nw = dest_w.shape[0]
    sub = PLAN_WINDOW // COMBINE_TILE
    plan = pl.BlockSpec((1, TOP_K, PLAN_WINDOW), lambda w, s: (w, 0, 0))
    tok = lambda w, s: (w * sub + s, 0)
    return pl.pallas_call(
        _combine_kernel,
        out_shape=jax.ShapeDtypeStruct(x1.shape, F32),
        grid=(nw, sub),
        in_specs=[plan, plan,
                  pl.BlockSpec((REGION_ROWS * ROW_SUB, LANES), lambda w, s: (w, 0)),
                  pl.BlockSpec((COMBINE_TILE, ROW_WORDS), tok),
                  pl.BlockSpec((COMBINE_TILE, D_MODEL), tok),
                  pl.BlockSpec((1, 1, D_MODEL), lambda w, s: ((w * PLAN_WINDOW) // tokens_per_gt2, 0, 0)),
                  _full((1, D_MODEL)), _full(ws13.shape), _full(ws2.shape)],
        out_specs=pl.BlockSpec((COMBINE_TILE, D_MODEL), tok),
        scratch_shapes=[pltpu.SMEM((TOP_K * PLAN_WINDOW,), jnp.int32),
                        pltpu.SMEM((TOP_K * PLAN_WINDOW,), F32),
                        pltpu.VMEM((COMBINE_TILE * ROW_SUB, LANES), F32),
                        pltpu.VMEM((COMBINE_TILE * ROW_SUB, LANES), F32),
                        pltpu.SemaphoreType.DMA((2,))],
        compiler_params=pltpu.CompilerParams(dimension_semantics=("arbitrary", "arbitrary"),
                                             vmem_limit_bytes=VMEM_LIMIT),
        name="combine",
    )(dest_w, wsel_w, y_slots, h2, x1, gt2, gpost, ws13, ws2)


def kernel(x_prompt, x_sample, c_prompt, c_sample, state_pool, state_hgrn, w_ada, b_ada, g_pre_mix, g_post_mix,
           w_in, w_pool, pool_scale, lb_logits, g_out_norm, w_out, g_pre_ffn, g_post_ffn, w_router, router_bias,
           w_exp_gate, w_exp_up, w_exp_down, w_sh_gate, w_sh_up, w_sh_down):
    assert w_ada.shape[0] == 1 and lb_logits.shape[0] == 2, "single-layer trunk"
    bp, lp, d = x_prompt.shape
    bs = x_sample.shape[0]
    row = lambda a: a[0].reshape(1, -1)

    n_mod = bp + bs
    pad = (-n_mod) % 16
    c_all = jnp.concatenate([c_prompt, c_sample, jnp.zeros((pad, d), F32)], axis=0)
    mod = _ada(c_all, w_ada[0], b_ada)
    mod_p = mod[:bp].reshape(bp, 1, 6 * d)
    mod_s = mod[bp:n_mod]

    win = w_in[0].astype(BF16)
    wout = w_out[0].astype(BF16)
    wpool = w_pool[0].astype(BF16)
    wr = jnp.pad(w_router[0], ((0, 0), (0, LANES - N_EXPERTS)))
    wrh = wr.astype(BF16)
    wrl = (wr - wrh.astype(F32)).astype(BF16)
    m2 = jnp.asarray(_M2_NP, BF16)
    gpre, gpost, gffn, gffn_post = row(g_pre_mix), row(g_post_mix), row(g_pre_ffn), row(g_post_ffn)
    pscale, gout = row(pool_scale), row(g_out_norm)
    bias_col = jnp.pad(router_bias[0], (0, LANES - N_EXPERTS)).reshape(LANES, 1)

    x1_p, h2_p, lg_p, pool_p, st_p = _mix_prompt(x_prompt, mod_p, gpre, gpost, gffn, win, wpool, pscale,
                                                  lb_logits, gout, wout, wrh, wrl, m2)
    x1_s, h2_s, lg_s, pool_s, st_s = _mix_sample(x_sample[:, 0, :], mod_s, state_pool[0], state_hgrn[0],
                                                  gpre, gpost, gffn, win, wpool, pscale, lb_logits, gout, wout,
                                                  wrh, wrl)

    experts = (w_exp_gate[0], w_exp_up[0], w_exp_down[0], w_sh_gate[0], w_sh_up[0], w_sh_down[0])
    tp = bp * lp
    nw = tp // PLAN_WINDOW
    assert lp % PLAN_WINDOW == 0
    upper = jnp.asarray(np.triu(np.ones((PLAN_WINDOW, PLAN_WINDOW), np.float32), 1), BF16)
    dest, wsel, cnt, loff = _plan(lg_p.reshape(tp, LANES), bias_col, upper)
    dest_w, wsel_w = dest, wsel
    h2_rows = h2_p.reshape(tp, ROW_WORDS)
    x_slots = _dispatch(dest_w, h2_rows)
    wd_s = _router(lg_s, bias_col)
    y_slots, y_s = _expert_ffn(*_work_items(cnt[:, :, 0], loff[:, :, 0]), x_slots, *experts[:3],
                               h2_s, wd_s, x1_s, mod_s[:, 5 * d:], gffn_post, *experts[3:])
    ws13 = jnp.concatenate([experts[3], experts[4]], axis=1).astype(BF16)
    gt2_p = mod_p[:, :, 5 * d:]
    y_p = _combine(dest_w, wsel_w, y_slots, h2_rows, x1_p.reshape(tp, d), gt2_p, lp, gffn_post,
                   ws13, experts[5].astype(BF16))

    return (y_p.reshape(bp, lp, d), y_s.reshape(bs, 1, d), pool_p[None], st_p[None], pool_s[None], st_s[None])
```

```python
import functools

import numpy as np
import jax
import jax.numpy as jnp
from jax import lax
from jax.experimental import pallas as pl
from jax.experimental.pallas import tpu as pltpu

F32 = jnp.float32
BF16 = jnp.bfloat16

D_MODEL = 1024
POOL_WIDTH = 512
POOL_WINDOWS = (2, 4, 8, 16)
POOL_GROUP = 128
POOL_HIST = 15
HG_WIDTH = 512
HG_HEADS = 4
HG_HEAD_DIM = 128
IN_WIDTH = POOL_WIDTH + 4 * HG_WIDTH
N_EXPERTS = 64
TOP_K = 8
N_GROUPS = 8
TOPK_GROUPS = 4
GROUP_SIZE = N_EXPERTS // N_GROUPS
D_EXPERT = 256
ROUTED_SCALE = 2.5
EPS = 1e-6

LANES = 128
CHUNK = 64
TIME_TILE = 256
ADA_TILE_N = 1024
ROUTER_TILE = 512
SAMPLE_STATE_BLOCK = 8
PLAN_WINDOW = 1024
RUN_ALIGN = 32
FFN_TILE = 160
REGION_ROWS = PLAN_WINDOW * TOP_K + N_EXPERTS * (RUN_ALIGN - 1) + FFN_TILE
ROW_WORDS = D_MODEL // 2
ROW_SUB = ROW_WORDS // LANES
COMBINE_TILE = 256
SCATTER_UNROLL = 4
COMBINE_UNROLL = 16
FFN_GROUP = 4
FFN_SPLIT = 1
FFN_RING = 6
VMEM_LIMIT = 58 * 1024 * 1024

_LEVELS = (64, 32, 16, 8, 4, 2)
_BLK_CUM = 0


def _level_blocks():
    c = CHUNK
    i = np.arange(c)[:, None]
    s = np.arange(c)[None, :]
    blocks = [(s <= i)]
    index = {}
    for lvl in _LEVELS:
        ref = (i // lvl) * lvl + lvl // 2 - 1
        index[lvl] = len(blocks)
        blocks.append(s <= ref)
    m = np.concatenate(blocks, axis=0).astype(np.float32)
    return np.concatenate([m, m], axis=1), index


_M2_NP, _LEVEL_INDEX = _level_blocks()


def _dot(a, b):
    return jnp.dot(a, b, preferred_element_type=F32)


def _dot_nt(a, b):
    return lax.dot_general(a, b, (((1,), (1,)), ((), ())), preferred_element_type=F32)


def _split2(x):
    hi = x.astype(BF16)
    lo = (x - hi.astype(F32)).astype(BF16)
    return hi, lo


def _pack_pair(a, b):
    lo = lax.bitcast_convert_type(a.astype(BF16).astype(F32), jnp.uint32)
    hi = lax.bitcast_convert_type(b.astype(BF16).astype(F32), jnp.uint32)
    return (lo >> 16) | hi


def _unpack_pair(words):
    lo = lax.bitcast_convert_type(words << 16, F32)
    hi = lax.bitcast_convert_type(words & jnp.uint32(0xFFFF0000), F32)
    return lo, hi


def _silu(x):
    return x * jax.nn.sigmoid(x)


def _rms(x, g):
    return x * lax.rsqrt(jnp.mean(x * x, axis=-1, keepdims=True) + EPS) * g


def _mods(m):
    return [m[:, j * D_MODEL:(j + 1) * D_MODEL] for j in range(6)]


def _forget_lower_bound(lbl):
    mx = jnp.max(lbl, axis=0, keepdims=True)
    e = jnp.exp(lbl - mx)
    return e[0:1] / jnp.sum(e, axis=0, keepdims=True)


def _router_logits(h2, wrh, wrl):
    hi, lo = _split2(h2)
    return _dot(hi, wrh) + _dot(lo, wrh) + _dot(hi, wrl)


def _ada_kernel(c_ref, w_ref, b_ref, o_ref):
    a_hi, a_lo = _split2(_silu(c_ref[...]))
    w_hi, w_lo = _split2(w_ref[...])
    o_ref[...] = _dot(a_hi, w_hi) + _dot(a_lo, w_hi) + _dot(a_hi, w_lo) + b_ref[...]


def _ada(c_all, w_ada, b_ada):
    rows = c_all.shape[0]
    n = w_ada.shape[1]
    return pl.pallas_call(
        _ada_kernel,
        out_shape=jax.ShapeDtypeStruct((rows, n), F32),
        grid=(n // ADA_TILE_N,),
        in_specs=[pl.BlockSpec((rows, D_MODEL), lambda j: (0, 0)),
                  pl.BlockSpec((D_MODEL, ADA_TILE_N), lambda j: (0, j)),
                  pl.BlockSpec((1, ADA_TILE_N), lambda j: (0, j))],
        out_specs=pl.BlockSpec((rows, ADA_TILE_N), lambda j: (0, j)),
        compiler_params=pltpu.CompilerParams(dimension_semantics=("parallel",),
                                             vmem_limit_bytes=VMEM_LIMIT),
        name="ada",
    )(c_all, w_ada, b_ada)


def _level_masks():
    i = lax.broadcasted_iota(jnp.int32, (CHUNK, 2 * CHUNK), 0)
    j = lax.broadcasted_iota(jnp.int32, (CHUNK, 2 * CHUNK), 1) & (CHUNK - 1)
    masks = {}
    for lvl in _LEVELS:
        sh = lvl.bit_length() - 1
        same = (i >> sh) == (j >> sh)
        upper = ((i >> (sh - 1)) & 1) == 1
        lower = ((j >> (sh - 1)) & 1) == 0
        masks[lvl] = same & upper & lower
    return masks, i == j


def _by_head(m):
    z = jnp.zeros((m.shape[0], HG_HEAD_DIM), m.dtype)
    return jnp.concatenate([jnp.concatenate([m[:, :HG_HEAD_DIM], z], axis=1),
                            jnp.concatenate([z, m[:, HG_HEAD_DIM:]], axis=1)], axis=0)


def _hgrn_chunk_pair(q, k, v, e_all, lane0, sts, masks, eye):
    hd = HG_HEAD_DIM

    def blk(n):
        return e_all[n * CHUNK:(n + 1) * CHUNK, lane0:lane0 + 2 * hd]

    b = blk(_BLK_CUM)
    kb = k.astype(BF16)
    a = jnp.where(eye, _dot_nt(q.astype(BF16), _by_head(kb)), 0.0)
    for lvl in _LEVELS:
        d = b - blk(_LEVEL_INDEX[lvl])
        ql = (q * jnp.exp(jnp.minimum(d, 0.0))).astype(BF16)
        kl = (k * jnp.exp(jnp.minimum(-d, 0.0))).astype(BF16) if lvl > 2 else kb
        a = a + jnp.where(masks[lvl], _dot_nt(ql, _by_head(kl)), 0.0)
    st_both = jnp.concatenate([jnp.concatenate([sts[0], jnp.zeros_like(sts[0])], axis=1),
                               jnp.concatenate([jnp.zeros_like(sts[1]), sts[1]], axis=1)], axis=0).astype(BF16)
    o = _dot(a.astype(BF16), _by_head(v.astype(BF16))) + _dot_nt((q * jnp.exp(b)).astype(BF16), st_both)
    b_last = b[CHUNK - 1:CHUNK, :]
    k_end = (k * jnp.exp(b_last - b)).astype(BF16)
    decay = jnp.exp(b_last)
    new = []
    for h in range(2):
        lanes = slice(h * hd, (h + 1) * hd)
        new.append(sts[h] * decay[:, lanes] + _dot(v[:, lanes].T.astype(BF16), k_end[:, lanes]))
    return o, new


def _mix_prompt_kernel(x_ref, mod_ref, gpre_ref, gpost_ref, gffn_ref, win_ref, wpool_ref, pscale_ref,
                       lbl_ref, gout_ref, wout_ref, wrh_ref, wrl_ref, m2_ref,
                       x1_ref, h2_ref, lg_ref, pool_ref, st_ref,
                       st_s, ubuf, q_s, k_s, v_s, g_s, o_s):
    t = pl.program_id(1)
    n_t = pl.num_programs(1)
    tt = TIME_TILE

    @pl.when(t == 0)
    def _():
        st_s[...] = jnp.zeros_like(st_s)
        ubuf[0:16, :] = jnp.zeros((16, POOL_WIDTH), F32)

    xt = x_ref[0]
    sh1, sc1, gt1, sh2, sc2, gt2 = _mods(mod_ref[0])
    h = _rms(xt, gpre_ref[...]) * (1.0 + sc1) + sh1
    proj = _dot(h.astype(BF16), win_ref[...])

    u = proj[:, :POOL_WIDTH]
    ubuf[16:16 + tt, :] = u
    pos = (t * tt + lax.broadcasted_iota(jnp.int32, (tt, 1), 0) + 1).astype(F32)
    ys = []
    for g, w in enumerate(POOL_WINDOWS):
        s = ubuf[:, g * POOL_GROUP:(g + 1) * POOL_GROUP]
        off = 0
        for step in range(w.bit_length() - 1):
            sh = 1 << step
            s = s[sh:, :] + s[:-sh, :]
            off += sh
        ws = s[16 - off:16 - off + tt, :]
        cnt = jnp.minimum(pos, float(w))
        d = ws / cnt - u[:, g * POOL_GROUP:(g + 1) * POOL_GROUP]
        ys.append(_dot(d.astype(BF16), wpool_ref[g]))
    y_pool = jnp.concatenate(ys, axis=1) * pscale_ref[...]
    ubuf[0:16, :] = ubuf[tt:tt + 16, :]

    @pl.when(t == n_t - 1)
    def _():
        pool_ref[0] = u[tt - POOL_HIST:, :]

    lb = _forget_lower_bound(lbl_ref[...])
    f = lb + (1.0 - lb) * jax.nn.sigmoid(proj[:, POOL_WIDTH + HG_WIDTH:POOL_WIDTH + 2 * HG_WIDTH])
    q_s[...] = _silu(proj[:, POOL_WIDTH:POOL_WIDTH + HG_WIDTH])
    k_s[...] = 1.0 - f
    v_s[...] = proj[:, POOL_WIDTH + 2 * HG_WIDTH:POOL_WIDTH + 3 * HG_WIDTH]
    g_s[...] = jnp.log(f)
    gate = _silu(proj[:, POOL_WIDTH + 3 * HG_WIDTH:])

    masks, eye = _level_masks()
    m2 = m2_ref[...]

    states = [st_s[hd] for hd in range(HG_HEADS)]
    for c in range(tt // CHUNK):
        rows = slice(c * CHUNK, (c + 1) * CHUNK)
        g_hi, g_lo = _split2(g_s[rows, :])
        e_all = _dot(m2, jnp.concatenate([g_hi, g_lo], axis=0))
        for hd in range(0, HG_HEADS, 2):
            lane0 = hd * HG_HEAD_DIM
            lanes = slice(lane0, lane0 + 2 * HG_HEAD_DIM)
            o, states[hd:hd + 2] = _hgrn_chunk_pair(q_s[rows, lanes], k_s[rows, lanes], v_s[rows, lanes], e_all,
                                                    lane0, states[hd:hd + 2], masks, eye)
            o_s[rows, lanes] = o
    for hd in range(HG_HEADS):
        st_s[hd] = states[hd]

    o = o_s[...]
    os_ = []
    for hd in range(HG_HEADS):
        oh = o[:, hd * HG_HEAD_DIM:(hd + 1) * HG_HEAD_DIM]
        os_.append(oh * lax.rsqrt(jnp.mean(oh * oh, axis=-1, keepdims=True) + EPS) * gout_ref[...])
    o_n = jnp.concatenate(os_, axis=1) * gate

    mix = _dot(y_pool.astype(BF16), wout_ref[0:POOL_WIDTH, :]) + _dot(o_n.astype(BF16), wout_ref[POOL_WIDTH:, :])
    x1 = xt + gt1 * _rms(mix, gpost_ref[...])
    h2 = _rms(x1, gffn_ref[...]) * (1.0 + sc2) + sh2
    x1_ref[0] = x1
    h2_ref[0] = _pack_pair(h2[:, :ROW_WORDS], h2[:, ROW_WORDS:])
    lg_ref[0] = _router_logits(h2, wrh_ref[...], wrl_ref[...])

    @pl.when(t == n_t - 1)
    def _():
        for hd in range(HG_HEADS):
            st_ref[0, hd] = st_s[hd].T


def _full(shape):
    nd = len(shape)
    return pl.BlockSpec(shape, lambda *_: (0,) * nd)


def _mix_prompt(x, mod, gpre, gpost, gffn, win, wpool, pscale, lbl, gout, wout, wrh, wrl, m2):
    b, l, d = x.shape
    tt = TIME_TILE
    tile = lambda i, j: (i, j, 0)
    per_b = lambda i, j: (i, 0, 0)
    return pl.pallas_call(
        _mix_prompt_kernel,
        out_shape=(jax.ShapeDtypeStruct((b, l, d), F32),
                   jax.ShapeDtypeStruct((b, l, ROW_WORDS), jnp.uint32),
                   jax.ShapeDtypeStruct((b, l, LANES), F32),
                   jax.ShapeDtypeStruct((b, POOL_HIST, POOL_WIDTH), F32),
                   jax.ShapeDtypeStruct((b, HG_HEADS, HG_HEAD_DIM, HG_HEAD_DIM), F32)),
        grid=(b, l // tt),
        in_specs=[pl.BlockSpec((1, tt, d), tile),
                  pl.BlockSpec((1, 1, 6 * d), per_b),
                  _full((1, d)), _full((1, d)), _full((1, d)),
                  _full(win.shape), _full(wpool.shape), _full((1, POOL_WIDTH)),
                  _full(lbl.shape), _full((1, HG_HEAD_DIM)), _full(wout.shape),
                  _full(wrh.shape), _full(wrl.shape), _full(m2.shape)],
        out_specs=(pl.BlockSpec((1, tt, d), tile),
                   pl.BlockSpec((1, tt, ROW_WORDS), tile),
                   pl.BlockSpec((1, tt, LANES), tile),
                   pl.BlockSpec((1, POOL_HIST, POOL_WIDTH), per_b),
                   pl.BlockSpec((1, HG_HEADS, HG_HEAD_DIM, HG_HEAD_DIM), lambda i, j: (i, 0, 0, 0))),
        scratch_shapes=[pltpu.VMEM((HG_HEADS, HG_HEAD_DIM, HG_HEAD_DIM), F32),
                        pltpu.VMEM((tt + 16, POOL_WIDTH), F32),
                        pltpu.VMEM((tt, HG_WIDTH), F32), pltpu.VMEM((tt, HG_WIDTH), F32),
                        pltpu.VMEM((tt, HG_WIDTH), F32), pltpu.VMEM((tt, HG_WIDTH), F32),
                        pltpu.VMEM((tt, HG_WIDTH), F32)],
        compiler_params=pltpu.CompilerParams(dimension_semantics=("parallel", "arbitrary"),
                                             vmem_limit_bytes=VMEM_LIMIT),
        name="mix_prompt",
    )(x, mod, gpre, gpost, gffn, win, wpool, pscale, lbl, gout, wout, wrh, wrl, m2)


def _mix_sample_in_kernel(x_ref, mod_ref, gpre_ref, win_ref, wpool_ref, pscale_ref, lbl_ref, hist_ref,
                          ypool_ref, npool_ref, ft_ref, qt_ref, v_ref, gate_ref):
    xt = x_ref[...]
    sh1, sc1 = _mods(mod_ref[...])[:2]
    h = _rms(xt, gpre_ref[...]) * (1.0 + sc1) + sh1
    proj = _dot(h.astype(BF16), win_ref[...])
    u = proj[:, :POOL_WIDTH]
    row = lax.broadcasted_iota(jnp.int32, (hist_ref.shape[0], POOL_HIST, POOL_GROUP), 1)
    ys = []
    for g, w in enumerate(POOL_WINDOWS):
        sl = slice(g * POOL_GROUP, (g + 1) * POOL_GROUP)
        past = jnp.sum(jnp.where(row >= POOL_HIST - (w - 1), hist_ref[:, :, sl], 0.0), axis=1)
        ug = u[:, sl]
        d = (past + ug) / float(w) - ug
        ys.append(_dot(d.astype(BF16), wpool_ref[g]))
    ypool_ref[...] = jnp.concatenate(ys, axis=1) * pscale_ref[...]
    npool_ref[:, 0:POOL_HIST - 1, :] = hist_ref[:, 1:POOL_HIST, :]
    npool_ref[:, POOL_HIST - 1, :] = u

    lb = _forget_lower_bound(lbl_ref[...])
    f = lb + (1.0 - lb) * jax.nn.sigmoid(proj[:, POOL_WIDTH + HG_WIDTH:POOL_WIDTH + 2 * HG_WIDTH])
    ft_ref[...] = f.T
    qt_ref[...] = _silu(proj[:, POOL_WIDTH:POOL_WIDTH + HG_WIDTH]).T
    v_ref[...] = proj[:, POOL_WIDTH + 2 * HG_WIDTH:POOL_WIDTH + 3 * HG_WIDTH]
    gate_ref[...] = _silu(proj[:, POOL_WIDTH + 3 * HG_WIDTH:])


def _mix_sample_state_kernel(s_ref, ft_ref, qt_ref, v_ref, snew_ref, o_ref):
    i = pl.program_id(0)
    lane = lax.broadcasted_iota(jnp.int32, (HG_HEAD_DIM, ft_ref.shape[1]), 1)
    for j in range(SAMPLE_STATE_BLOCK):
        mine = lane == i * SAMPLE_STATE_BLOCK + j
        for hd in range(HG_HEADS):
            r0 = hd * HG_HEAD_DIM
            f = jnp.sum(jnp.where(mine, ft_ref[r0:r0 + HG_HEAD_DIM, :], 0.0), axis=1, keepdims=True)
            q = jnp.sum(jnp.where(mine, qt_ref[r0:r0 + HG_HEAD_DIM, :], 0.0), axis=1, keepdims=True)
            v = v_ref[j:j + 1, r0:r0 + HG_HEAD_DIM]
            s_new = f * s_ref[j, hd] + (1.0 - f) * v
            snew_ref[j, hd] = s_new
            o_ref[j:j + 1, r0:r0 + HG_HEAD_DIM] = jnp.sum(q * s_new, axis=0, keepdims=True)


def _mix_sample_out_kernel(x_ref, mod_ref, o_ref, gate_ref, ypool_ref, gout_ref, wout_ref, gpost_ref,
                           gffn_ref, wrh_ref, wrl_ref, x1_ref, h2_ref, lg_ref):
    _, _, gt1, sh2, sc2, _ = _mods(mod_ref[...])
    o = o_ref[...]
    os_ = []
    for hd in range(HG_HEADS):
        oh = o[:, hd * HG_HEAD_DIM:(hd + 1) * HG_HEAD_DIM]
        os_.append(oh * lax.rsqrt(jnp.mean(oh * oh, axis=-1, keepdims=True) + EPS) * gout_ref[...])
    o_n = jnp.concatenate(os_, axis=1) * gate_ref[...]
    mix = (_dot(ypool_ref[...].astype(BF16), wout_ref[0:POOL_WIDTH, :])
           + _dot(o_n.astype(BF16), wout_ref[POOL_WIDTH:, :]))
    x1 = x_ref[...] + gt1 * _rms(mix, gpost_ref[...])
    h2 = _rms(x1, gffn_ref[...]) * (1.0 + sc2) + sh2
    x1_ref[...] = x1
    h2_ref[...] = h2.astype(BF16)
    lg_ref[...] = _router_logits(h2, wrh_ref[...], wrl_ref[...])


def _mix_sample(x, mod, hist, state, gpre, gpost, gffn, win, wpool, pscale, lbl, gout, wout, wrh, wrl):
    b = x.shape[0]
    bb = SAMPLE_STATE_BLOCK
    cp = pltpu.CompilerParams(vmem_limit_bytes=VMEM_LIMIT)
    ypool, npool, ft, qt, v, gate = pl.pallas_call(
        _mix_sample_in_kernel,
        out_shape=(jax.ShapeDtypeStruct((b, POOL_WIDTH), F32),
                   jax.ShapeDtypeStruct((b, POOL_HIST, POOL_WIDTH), F32),
                   jax.ShapeDtypeStruct((HG_WIDTH, b), F32),
                   jax.ShapeDtypeStruct((HG_WIDTH, b), F32),
                   jax.ShapeDtypeStruct((b, HG_WIDTH), F32),
                   jax.ShapeDtypeStruct((b, HG_WIDTH), F32)),
        compiler_params=cp,
        name="mix_sample_in",
    )(x, mod, gpre, win, wpool, pscale, lbl, hist)

    s_spec = pl.BlockSpec((bb, HG_HEADS, HG_HEAD_DIM, HG_HEAD_DIM), lambda i: (i, 0, 0, 0))
    col_spec = _full((HG_WIDTH, b))
    row_spec = pl.BlockSpec((bb, HG_WIDTH), lambda i: (i, 0))
    s_new, o = pl.pallas_call(
        _mix_sample_state_kernel,
        out_shape=(jax.ShapeDtypeStruct(state.shape, F32), jax.ShapeDtypeStruct((b, HG_WIDTH), F32)),
        grid=(b // bb,),
        in_specs=[s_spec, col_spec, col_spec, row_spec],
        out_specs=(s_spec, row_spec),
        compiler_params=pltpu.CompilerParams(dimension_semantics=("parallel",), vmem_limit_bytes=VMEM_LIMIT),
        name="mix_sample_state",
    )(state, ft, qt, v)

    x1, h2, lg = pl.pallas_call(
        _mix_sample_out_kernel,
        out_shape=(jax.ShapeDtypeStruct((b, D_MODEL), F32),
                   jax.ShapeDtypeStruct((b, D_MODEL), BF16),
                   jax.ShapeDtypeStruct((b, LANES), F32)),
        compiler_params=cp,
        name="mix_sample_out",
    )(x, mod, o, gate, ypool, gout, wout, gpost, gffn, wrh, wrl)
    return x1, h2, lg, npool, s_new


def _first_max(cur, idx, big):
    m = jnp.max(cur, axis=0, keepdims=True)
    first = jnp.min(jnp.where(cur == m, idx, big), axis=0, keepdims=True)
    return idx == first


def _route(logit, bias):
    n = logit.shape[1]
    scores = jax.nn.sigmoid(logit)
    sel = scores + bias
    neg = -jnp.inf

    sub = lax.broadcasted_iota(jnp.int32, (GROUP_SIZE, n), 0)
    gscore = []
    for g in range(N_GROUPS):
        sg = sel[g * GROUP_SIZE:(g + 1) * GROUP_SIZE, :]
        m1 = jnp.max(sg, axis=0, keepdims=True)
        rest = jnp.where(_first_max(sg, sub, GROUP_SIZE), neg, sg)
        gscore.append(m1 + jnp.max(rest, axis=0, keepdims=True))
    cur = jnp.concatenate(gscore, axis=0)
    gidx = lax.broadcasted_iota(jnp.int32, (N_GROUPS, n), 0)
    gmask = jnp.zeros((N_GROUPS, n), jnp.bool_)
    for _ in range(TOPK_GROUPS):
        pick = _first_max(cur, gidx, N_GROUPS)
        gmask = gmask | pick
        cur = jnp.where(pick, neg, cur)
    emask = jnp.concatenate(
        [jnp.broadcast_to(gmask[g:g + 1, :], (GROUP_SIZE, n)) for g in range(N_GROUPS)], axis=0)

    cur = jnp.where(emask, sel, neg)
    eidx = lax.broadcasted_iota(jnp.int32, (N_EXPERTS, n), 0)
    picks = []
    for _ in range(TOP_K):
        pick = _first_max(cur, eidx, N_EXPERTS)
        picks.append(pick)
        cur = jnp.where(pick, neg, cur)
    return scores, picks


def _any(masks):
    return functools.reduce(jnp.logical_or, masks)


def _router_kernel(lg_ref, bias_ref, wd_ref):
    n = lg_ref.shape[0]
    scores, picks = _route(lg_ref[...].T[0:N_EXPERTS, :], bias_ref[0:N_EXPERTS, :])
    chosen = _any(picks)
    wsum = jnp.sum(jnp.where(chosen, scores, 0.0), axis=0, keepdims=True)
    wd = jnp.where(chosen, scores / wsum * ROUTED_SCALE, 0.0)
    wd_ref[...] = jnp.concatenate([wd, jnp.zeros((LANES - N_EXPERTS, n), F32)], axis=0).T


def _router(logits, bias_col):
    t = logits.shape[0]
    tile = min(ROUTER_TILE, t)
    return pl.pallas_call(
        _router_kernel,
        out_shape=jax.ShapeDtypeStruct((t, LANES), F32),
        grid=(t // tile,),
        in_specs=[pl.BlockSpec((tile, LANES), lambda i: (i, 0)), _full(bias_col.shape)],
        out_specs=pl.BlockSpec((tile, LANES), lambda i: (i, 0)),
        compiler_params=pltpu.CompilerParams(dimension_semantics=("parallel",), vmem_limit_bytes=VMEM_LIMIT),
        name="router",
    )(logits, bias_col)


def _plan_kernel(lg_ref, bias_ref, upper_ref, dest_ref, wsel_ref, cnt_ref, loff_ref):
    scores, picks = _route(lg_ref[...].T[0:N_EXPERTS, :], bias_ref[0:N_EXPERTS, :])
    chosen = _any(picks)
    chosen_f = jnp.where(chosen, 1.0, 0.0)
    cnt = jnp.sum(chosen_f, axis=1, keepdims=True)
    units = jnp.floor((cnt + (RUN_ALIGN - 1)) / RUN_ALIGN)
    ei = lax.broadcasted_iota(jnp.int32, (N_EXPERTS, N_EXPERTS), 0)
    ej = lax.broadcasted_iota(jnp.int32, (N_EXPERTS, N_EXPERTS), 1)
    before = jnp.where(ej < ei, 1.0, 0.0).astype(BF16)
    loff = RUN_ALIGN * _dot(before, jnp.broadcast_to(units, (N_EXPERTS, LANES)).astype(BF16))[:, 0:1]
    slot = loff + _dot(chosen_f.astype(BF16), upper_ref[...])
    wsum = jnp.sum(jnp.where(chosen, scores, 0.0), axis=0, keepdims=True)
    pick_sum = lambda v: jnp.concatenate(
        [jnp.sum(jnp.where(p, v, 0.0), axis=0, keepdims=True) for p in picks], axis=0)
    dest_ref[0] = (pick_sum(slot) * ROW_SUB).astype(jnp.int32)
    wsel_ref[0] = pick_sum(scores) / wsum * ROUTED_SCALE
    cnt_ref[0] = jnp.broadcast_to(cnt, (N_EXPERTS, LANES))
    loff_ref[0] = jnp.broadcast_to(loff, (N_EXPERTS, LANES))


def _plan(logits, bias_col, upper):
    t = logits.shape[0]
    nw = t // PLAN_WINDOW
    per_w = pl.BlockSpec((1, N_EXPERTS, LANES), lambda w: (w, 0, 0))
    picks = pl.BlockSpec((1, TOP_K, PLAN_WINDOW), lambda w: (w, 0, 0))
    return pl.pallas_call(
        _plan_kernel,
        out_shape=(jax.ShapeDtypeStruct((nw, TOP_K, PLAN_WINDOW), jnp.int32),
                   jax.ShapeDtypeStruct((nw, TOP_K, PLAN_WINDOW), F32),
                   jax.ShapeDtypeStruct((nw, N_EXPERTS, LANES), F32),
                   jax.ShapeDtypeStruct((nw, N_EXPERTS, LANES), F32)),
        grid=(nw,),
        in_specs=[pl.BlockSpec((PLAN_WINDOW, LANES), lambda w: (w, 0)), _full(bias_col.shape), _full(upper.shape)],
        out_specs=(picks, picks, per_w, per_w),
        compiler_params=pltpu.CompilerParams(dimension_semantics=("parallel",), vmem_limit_bytes=VMEM_LIMIT),
        name="plan",
    )(logits, bias_col, upper)


def _slot_rows(slot):
    return pl.ds(pl.multiple_of(slot * ROW_SUB, ROW_SUB), ROW_SUB)


def _table_copies(table_ref, flat_s, sem):
    return [pltpu.make_async_copy(table_ref.at[0, k], flat_s.at[pl.ds(k * PLAN_WINDOW, PLAN_WINDOW)], sem)
            for k in range(TOP_K)]


def _rows_at(offset):
    return pl.ds(pl.multiple_of(offset, ROW_SUB), ROW_SUB)


def _dispatch_kernel(dest_ref, h_ref, x_ref, dest_s, rows_s, sem):
    plan_copies = _table_copies(dest_ref, dest_s, sem)
    for c in plan_copies:
        c.start()
    x_ref[...] = jnp.zeros_like(x_ref)
    for j in range(ROW_SUB):
        rows_s[pl.ds(j, PLAN_WINDOW, stride=ROW_SUB), :] = h_ref[:, j * LANES:(j + 1) * LANES]
    for c in plan_copies:
        c.wait()

    def body(i, carry):
        for u in range(SCATTER_UNROLL):
            t = i * SCATTER_UNROLL + u
            row = rows_s[_slot_rows(t), :]
            for k in range(TOP_K):
                x_ref[_rows_at(dest_s[k * PLAN_WINDOW + t]), :] = row
        return carry

    lax.fori_loop(0, PLAN_WINDOW // SCATTER_UNROLL, body, 0)


def _dispatch(dest_w, h2):
    nw = dest_w.shape[0]
    region = REGION_ROWS * ROW_SUB
    return pl.pallas_call(
        _dispatch_kernel,
        out_shape=jax.ShapeDtypeStruct((nw * region, LANES), jnp.uint32),
        grid=(nw,),
        in_specs=[pl.BlockSpec((1, TOP_K, PLAN_WINDOW), lambda w: (w, 0, 0)),
                  pl.BlockSpec((PLAN_WINDOW, ROW_WORDS), lambda w: (w, 0))],
        out_specs=pl.BlockSpec((region, LANES), lambda w: (w, 0)),
        scratch_shapes=[pltpu.SMEM((TOP_K * PLAN_WINDOW,), jnp.int32),
                        pltpu.VMEM((PLAN_WINDOW * ROW_SUB, LANES), jnp.uint32),
                        pltpu.SemaphoreType.DMA],
        compiler_params=pltpu.CompilerParams(dimension_semantics=("arbitrary",), vmem_limit_bytes=VMEM_LIMIT),
        name="dispatch",
    )(dest_w, h2)


def _unpack_rows(ref, n):
    parts = [_unpack_pair(ref[pl.ds(j, n, stride=ROW_SUB), :]) for j in range(ROW_SUB)]
    return jnp.concatenate([p[0] for p in parts] + [p[1] for p in parts], axis=1)


def _ffn_kernel(gq_ref, gs_ref, to_ref, tn_ref, x_hbm, w1_ref, w3_ref, w2_ref,
                hs_ref, wds_ref, x1s_ref, gt2s_ref, gpost_ref, ws1_ref, ws3_ref, ws2_ref, y_hbm, ys_ref,
                w13_s, w2_s, xbuf, ybuf, rsem, wsem, acc_s):
    del x_hbm
    e = pl.program_id(0)
    n_groups = gq_ref[N_EXPERTS]
    block = RUN_ALIGN * ROW_SUB
    tile = FFN_TILE * ROW_SUB

    def go(cp, start):
        if start:
            cp.start()
        else:
            cp.wait()

    def fetch(q, start):
        def one(j):
            src = y_hbm.at[pl.ds(pl.multiple_of(to_ref[gs_ref[q] + j] * block, block), tile)]
            go(pltpu.make_async_copy(src, xbuf.at[q % FFN_RING, pl.ds(j * tile, tile)], rsem.at[q % FFN_RING]), start)

        size = gs_ref[q + 1] - gs_ref[q]

        @pl.when(size == FFN_GROUP)
        def _():
            for j in range(FFN_GROUP):
                one(j)

        @pl.when(size < FFN_GROUP)
        def _():
            for j in range(FFN_GROUP - 1):
                pl.when(j < size)(functools.partial(one, j))

    def writeback(q, start):
        most = FFN_TILE // RUN_ALIGN - 1

        def blocks(j, b, n):
            dst = pl.multiple_of(to_ref[gs_ref[q] + j] * block, block) + b * block
            go(pltpu.make_async_copy(ybuf.at[q % FFN_RING, pl.ds(j * tile + b * block, n * block)],
                                     y_hbm.at[pl.ds(dst, n * block)], wsem.at[q % FFN_RING]), start)

        for j in range(FFN_GROUP):
            item = gs_ref[q] + j
            n = jnp.where(item < gs_ref[q + 1], tn_ref[item], 0)
            pl.when(n >= most)(functools.partial(blocks, j, 0, most))
            pl.when(n > most)(functools.partial(blocks, j, most, 1))

            @pl.when((n > 0) & (n < most))
            def _():
                for b in range(most - 1):
                    pl.when(b < n)(functools.partial(blocks, j, b, 1))

    @pl.when(e == 0)
    def _():
        xbuf[...] = jnp.zeros_like(xbuf)
        for d in range(FFN_RING - 1):
            pl.when(d < n_groups)(functools.partial(fetch, d, True))

    w13_s[:, 0:D_EXPERT] = w1_ref[0].astype(BF16)
    w13_s[:, D_EXPERT:] = w3_ref[0].astype(BF16)
    w2_s[...] = w2_ref[0].astype(BF16)

    xs = hs_ref[...]

    @pl.when(e == 0)
    def _():
        sh = _silu(_dot(xs, ws1_ref[...].astype(BF16))) * _dot(xs, ws3_ref[...].astype(BF16))
        acc_s[...] = _dot(sh.astype(BF16), ws2_ref[...].astype(BF16))

    lane = lax.broadcasted_iota(jnp.int32, wds_ref.shape, 1)
    wcol = jnp.sum(jnp.where(lane == e, wds_ref[...], 0.0), axis=1, keepdims=True)
    hd = _dot(xs, w13_s[...])
    acc_s[...] += _dot((_silu(hd[:, :D_EXPERT]) * hd[:, D_EXPERT:] * wcol).astype(BF16), w2_s[...])

    @pl.when(e == pl.num_programs(0) - 1)
    def _():
        ys_ref[...] = x1s_ref[...] + gt2s_ref[...] * _rms(acc_s[...], gpost_ref[...])

    def group_body(q, carry):
        @pl.when(q + FFN_RING - 1 < n_groups)
        def _():
            fetch(q + FFN_RING - 1, True)

        @pl.when(q >= FFN_RING)
        def _():
            writeback(q - FFN_RING, False)

        fetch(q, False)
        rows = FFN_GROUP * FFN_TILE // FFN_SPLIT
        for part in range(FFN_SPLIT):
            span = pl.ds(part * rows * ROW_SUB, rows * ROW_SUB)
            x = _unpack_rows(xbuf.at[q % FFN_RING, span], rows).astype(BF16)
            h = _dot(x, w13_s[...])
            act = _silu(h[:, :D_EXPERT]) * h[:, D_EXPERT:]
            y = _dot(act.astype(BF16), w2_s[...])
            out = ybuf.at[q % FFN_RING, span]
            for j in range(ROW_SUB):
                out[pl.ds(j, rows, stride=ROW_SUB), :] = _pack_pair(
                    y[:, j * LANES:(j + 1) * LANES], y[:, ROW_WORDS + j * LANES:ROW_WORDS + (j + 1) * LANES])
        writeback(q, True)
        return carry

    lax.fori_loop(gq_ref[e], gq_ref[e + 1], group_body, 0)

    @pl.when(e == pl.num_programs(0) - 1)
    def _():
        for d in range(FFN_RING):
            @pl.when(n_groups - 1 - d >= 0)
            def _():
                writeback(n_groups - 1 - d, False)


def _expert_ffn(group_of_expert, group_start, item_off, item_blocks, x_slots, w1, w3, w2,
                h2_s, wd_s, x1_s, gt2_s, gpost, ws1, ws3, ws2):
    of_expert = lambda e, *_: (e, 0, 0)
    whole = lambda a: pl.BlockSpec(a.shape, lambda e, *_: (0,) * a.ndim)
    tiles = pltpu.VMEM((FFN_RING, FFN_GROUP * FFN_TILE * ROW_SUB, LANES), jnp.uint32)
    sems = pltpu.SemaphoreType.DMA((FFN_RING,))
    dense = (h2_s, wd_s, x1_s, gt2_s, gpost, ws1, ws3, ws2)
    grid_spec = pltpu.PrefetchScalarGridSpec(
        num_scalar_prefetch=4,
        grid=(N_EXPERTS,),
        in_specs=[pl.BlockSpec(memory_space=pl.ANY),
                  pl.BlockSpec((1, D_MODEL, D_EXPERT), of_expert),
                  pl.BlockSpec((1, D_MODEL, D_EXPERT), of_expert),
                  pl.BlockSpec((1, D_EXPERT, D_MODEL), of_expert)] + [whole(a) for a in dense],
        out_specs=(pl.BlockSpec(memory_space=pl.ANY), whole(x1_s)),
        scratch_shapes=[pltpu.VMEM((D_MODEL, 2 * D_EXPERT), BF16), pltpu.VMEM((D_EXPERT, D_MODEL), BF16),
                        tiles, tiles, sems, sems, pltpu.VMEM(x1_s.shape, F32)])
    return pl.pallas_call(
        _ffn_kernel,
        out_shape=(jax.ShapeDtypeStruct(x_slots.shape, jnp.uint32), jax.ShapeDtypeStruct(x1_s.shape, F32)),
        grid_spec=grid_spec,
        input_output_aliases={4: 0},
        compiler_params=pltpu.CompilerParams(dimension_semantics=("arbitrary",), vmem_limit_bytes=VMEM_LIMIT),
        name="expert_ffn",
    )(group_of_expert, group_start, item_off, item_blocks, x_slots, w1, w3, w2, *dense)


def _work_items(cnt, loff):
    nw = cnt.shape[0]
    cnt = cnt.astype(jnp.int32).T.reshape(-1)
    loff = loff.astype(jnp.int32).T.reshape(-1)
    region = jnp.tile(jnp.arange(nw, dtype=jnp.int32) * REGION_ROWS, N_EXPERTS)
    n_items = nw * N_EXPERTS + (nw * PLAN_WINDOW * TOP_K) // FFN_TILE
    tiles = (cnt + FFN_TILE - 1) // FFN_TILE
    ends = jnp.cumsum(tiles)
    starts = ends - tiles
    i = jnp.arange(n_items + FFN_GROUP, dtype=jnp.int32)[:, None]
    mine = (starts[None, :] <= i) & (i < ends[None, :])
    of_pair = lambda v: jnp.sum(jnp.where(mine, v[None, :], 0), axis=1)
    r = i[:, 0] - of_pair(starts)
    row = of_pair(region + loff) + r * FFN_TILE
    own = jnp.clip(of_pair(cnt) - r * FFN_TILE, 0, FFN_TILE)

    zero = jnp.zeros((1,), jnp.int32)
    item_end = ends.reshape(N_EXPERTS, nw)[:, -1]
    item_start = jnp.concatenate([zero, item_end[:-1]])
    group_end = jnp.cumsum((item_end - item_start + FFN_GROUP - 1) // FFN_GROUP)
    group_first = jnp.concatenate([zero, group_end[:-1]])
    q = jnp.arange(N_EXPERTS + n_items // FFN_GROUP + 1, dtype=jnp.int32)[:, None]
    has = (group_first[None, :] <= q) & (q < group_end[None, :])
    of_expert = lambda v: jnp.sum(jnp.where(has, v[None, :], 0), axis=1)
    group_start = jnp.where(q[:, 0] < group_end[-1],
                            of_expert(item_start) + (q[:, 0] - of_expert(group_first)) * FFN_GROUP, item_end[-1])
    return (jnp.concatenate([zero, group_end]), group_start, row // RUN_ALIGN, (own + RUN_ALIGN - 1) // RUN_ALIGN)


def _combine_kernel(dest_ref, wsel_ref, y_ref, h_ref, x1_ref, gt2_ref, gpost_ref, ws13_ref, ws2_ref, out_ref,
                    dest_s, wsel_s, lo_s, hi_s, sem):
    s = pl.program_id(1)

    @pl.when(s == 0)
    def _():
        copies = _table_copies(dest_ref, dest_s, sem.at[0]) + _table_copies(wsel_ref, wsel_s, sem.at[1])
        for c in copies:
            c.start()
        for c in copies:
            c.wait()

    base = s * COMBINE_TILE

    def body(i, carry):
        for u in range(COMBINE_UNROLL):
            t = i * COMBINE_UNROLL + u
            acc_lo = jnp.zeros((ROW_SUB, LANES), F32)
            acc_hi = jnp.zeros((ROW_SUB, LANES), F32)
            for k in range(TOP_K):
                at = k * PLAN_WINDOW + base + t
                lo, hi = _unpack_pair(y_ref[_rows_at(dest_s[at]), :])
                w = wsel_s[at]
                acc_lo = acc_lo + w * lo
                acc_hi = acc_hi + w * hi
            lo_s[_slot_rows(t), :] = acc_lo
            hi_s[_slot_rows(t), :] = acc_hi
        return carry

    lax.fori_loop(0, COMBINE_TILE // COMBINE_UNROLL, body, 0)
    rows = lambda ref: [ref[pl.ds(j, COMBINE_TILE, stride=ROW_SUB), :] for j in range(ROW_SUB)]
    routed = jnp.concatenate(rows(lo_s) + rows(hi_s), axis=1)
    h_lo, h_hi = _unpack_pair(h_ref[...])
    x = jnp.concatenate([h_lo, h_hi], axis=1).astype(BF16)
    hs = _dot(x, ws13_ref[...])
    act = _silu(hs[:, :D_EXPERT]) * hs[:, D_EXPERT:]
    ff = routed + _dot(act.astype(BF16), ws2_ref[...])
    out_ref[...] = x1_ref[...] + gt2_ref[0] * _rms(ff, gpost_ref[...])


def _combine(dest_w, wsel_w, y_slots, h2, x1, gt2, tokens_per_gt2, gpost, ws13, ws2):
    nw = dest_w.shape[0]
    sub = PLAN_WINDOW // COMBINE_TILE
    plan = pl.BlockSpec((1, TOP_K, PLAN_WINDOW), lambda w, s: (w, 0, 0))
    tok = lambda w, s: (w * sub + s, 0)
    return pl.pallas_call(
        _combine_kernel,
        out_shape=jax.ShapeDtypeStruct(x1.shape, F32),
        grid=(nw, sub),
        in_specs=[plan, plan,
                  pl.BlockSpec((REGION_ROWS * ROW_SUB, LANES), lambda w, s: (w, 0)),
                  pl.BlockSpec((COMBINE_TILE, ROW_WORDS), tok),
                  pl.BlockSpec((COMBINE_TILE, D_MODEL), tok),
                  pl.BlockSpec((1, 1, D_MODEL), lambda w, s: ((w * PLAN_WINDOW) // tokens_per_gt2, 0, 0)),
                  _full((1, D_MODEL)), _full(ws13.shape), _full(ws2.shape)],
        out_specs=pl.BlockSpec((COMBINE_TILE, D_MODEL), tok),
        scratch_shapes=[pltpu.SMEM((TOP_K * PLAN_WINDOW,), jnp.int32),
                        pltpu.SMEM((TOP_K * PLAN_WINDOW,), F32),
                        pltpu.VMEM((COMBINE_TILE * ROW_SUB, LANES), F32),
                        pltpu.VMEM((COMBINE_TILE * ROW_SUB, LANES), F32),
                        pltpu.SemaphoreType.DMA((2,))],
        compiler_params=pltpu.CompilerParams(dimension_semantics=("arbitrary", "arbitrary"),
                                             vmem_limit_bytes=VMEM_LIMIT),
        name="combine",
    )(dest_w, wsel_w, y_slots, h2, x1, gt2, gpost, ws13, ws2)


def kernel(x_prompt, x_sample, c_prompt, c_sample, state_pool, state_hgrn, w_ada, b_ada, g_pre_mix, g_post_mix,
           w_in, w_pool, pool_scale, lb_logits, g_out_norm, w_out, g_pre_ffn, g_post_ffn, w_router, router_bias,
           w_exp_gate, w_exp_up, w_exp_down, w_sh_gate, w_sh_up, w_sh_down):
    assert w_ada.shape[0] == 1 and lb_logits.shape[0] == 2, "single-layer trunk"
    bp, lp, d = x_prompt.shape
    bs = x_sample.shape[0]
    row = lambda a: a[0].reshape(1, -1)

    n_mod = bp + bs
    pad = (-n_mod) % 16
    c_all = jnp.concatenate([c_prompt, c_sample, jnp.zeros((pad, d), F32)], axis=0)
    mod = _ada(c_all, w_ada[0], b_ada)
    mod_p = mod[:bp].reshape(bp, 1, 6 * d)
    mod_s = mod[bp:n_mod]

    win = w_in[0].astype(BF16)
    wout = w_out[0].astype(BF16)
    wpool = w_pool[0].astype(BF16)
    wr = jnp.pad(w_router[0], ((0, 0), (0, LANES - N_EXPERTS)))
    wrh = wr.astype(BF16)
    wrl = (wr - wrh.astype(F32)).astype(BF16)
    m2 = jnp.asarray(_M2_NP, BF16)
    gpre, gpost, gffn, gffn_post = row(g_pre_mix), row(g_post_mix), row(g_pre_ffn), row(g_post_ffn)
    pscale, gout = row(pool_scale), row(g_out_norm)
    bias_col = jnp.pad(router_bias[0], (0, LANES - N_EXPERTS)).reshape(LANES, 1)

    x1_p, h2_p, lg_p, pool_p, st_p = _mix_prompt(x_prompt, mod_p, gpre, gpost, gffn, win, wpool, pscale,
                                                  lb_logits, gout, wout, wrh, wrl, m2)
    x1_s, h2_s, lg_s, pool_s, st_s = _mix_sample(x_sample[:, 0, :], mod_s, state_pool[0], state_hgrn[0],
                                                  gpre, gpost, gffn, win, wpool, pscale, lb_logits, gout, wout,
                                                  wrh, wrl)

    experts = (w_exp_gate[0], w_exp_up[0], w_exp_down[0], w_sh_gate[0], w_sh_up[0], w_sh_down[0])
    tp = bp * lp
    nw = tp // PLAN_WINDOW
    assert lp % PLAN_WINDOW == 0
    upper = jnp.asarray(np.triu(np.ones((PLAN_WINDOW, PLAN_WINDOW), np.float32), 1), BF16)
    dest, wsel, cnt, loff = _plan(lg_p.reshape(tp, LANES), bias_col, upper)
    dest_w, wsel_w = dest, wsel
    h2_rows = h2_p.reshape(tp, ROW_WORDS)
    x_slots = _dispatch(dest_w, h2_rows)
    wd_s = _router(lg_s, bias_col)
    y_slots, y_s = _expert_ffn(*_work_items(cnt[:, :, 0], loff[:, :, 0]), x_slots, *experts[:3],
                               h2_s, wd_s, x1_s, mod_s[:, 5 * d:], gffn_post, *experts[3:])
    ws13 = jnp.concatenate([experts[3], experts[4]], axis=1).astype(BF16)
    gt2_p = mod_p[:, :, 5 * d:]
    y_p = _combine(dest_w, wsel_w, y_slots, h2_rows, x1_p.reshape(tp, d), gt2_p, lp, gffn_post,
                   ws13, experts[5].astype(BF16))

    return (y_p.reshape(bp, lp, d), y_s.reshape(bs, 1, d), pool_p[None], st_p[None], pool_s[None], st_s[None])
```

```python
import functools

import numpy as np
import jax
import jax.numpy as jnp
from jax import lax
from jax.experimental import pallas as pl
from jax.experimental.pallas import tpu as pltpu

F32 = jnp.float32
BF16 = jnp.bfloat16

D_MODEL = 1024
POOL_WIDTH = 512
POOL_WINDOWS = (2, 4, 8, 16)
POOL_GROUP = 128
POOL_HIST = 15
HG_WIDTH = 512
HG_HEADS = 4
HG_HEAD_DIM = 128
IN_WIDTH = POOL_WIDTH + 4 * HG_WIDTH
N_EXPERTS = 64
TOP_K = 8
N_GROUPS = 8
TOPK_GROUPS = 4
GROUP_SIZE = N_EXPERTS // N_GROUPS
D_EXPERT = 256
ROUTED_SCALE = 2.5
EPS = 1e-6

LANES = 128
CHUNK = 64
TIME_TILE = 256
ADA_TILE_N = 1024
ROUTER_TILE = 512
SAMPLE_STATE_BLOCK = 8
PLAN_WINDOW = 1024
RUN_ALIGN = 32
FFN_TILE = 160
REGION_ROWS = PLAN_WINDOW * TOP_K + N_EXPERTS * (RUN_ALIGN - 1) + FFN_TILE
ROW_WORDS = D_MODEL // 2
ROW_SUB = ROW_WORDS // LANES
COMBINE_TILE = 256
SCATTER_UNROLL = 4
COMBINE_UNROLL = 16
FFN_GROUP = 4
FFN_SPLIT = 1
FFN_RING = 6
VMEM_LIMIT = 58 * 1024 * 1024

_LEVELS = (64, 32, 16, 8, 4, 2)
_BLK_CUM = 0


def _level_blocks():
    c = CHUNK
    i = np.arange(c)[:, None]
    s = np.arange(c)[None, :]
    blocks = [(s <= i)]
    index = {}
    for lvl in _LEVELS:
        ref = (i // lvl) * lvl + lvl // 2 - 1
        index[lvl] = len(blocks)
        blocks.append(s <= ref)
    m = np.concatenate(blocks, axis=0).astype(np.float32)
    return np.concatenate([m, m], axis=1), index


_M2_NP, _LEVEL_INDEX = _level_blocks()


def _dot(a, b):
    return jnp.dot(a, b, preferred_element_type=F32)


def _dot_nt(a, b):
    return lax.dot_general(a, b, (((1,), (1,)), ((), ())), preferred_element_type=F32)


def _split2(x):
    hi = x.astype(BF16)
    lo = (x - hi.astype(F32)).astype(BF16)
    return hi, lo


def _pack_pair(a, b):
    lo = lax.bitcast_convert_type(a.astype(BF16).astype(F32), jnp.uint32)
    hi = lax.bitcast_convert_type(b.astype(BF16).astype(F32), jnp.uint32)
    return (lo >> 16) | hi


def _unpack_pair(words):
    lo = lax.bitcast_convert_type(words << 16, F32)
    hi = lax.bitcast_convert_type(words & jnp.uint32(0xFFFF0000), F32)
    return lo, hi


def _silu(x):
    return x * jax.nn.sigmoid(x)


def _rms(x, g):
    return x * lax.rsqrt(jnp.mean(x * x, axis=-1, keepdims=True) + EPS) * g


def _mods(m):
    return [m[:, j * D_MODEL:(j + 1) * D_MODEL] for j in range(6)]


def _forget_lower_bound(lbl):
    mx = jnp.max(lbl, axis=0, keepdims=True)
    e = jnp.exp(lbl - mx)
    return e[0:1] / jnp.sum(e, axis=0, keepdims=True)


def _router_logits(h2, wrh, wrl):
    hi, lo = _split2(h2)
    return _dot(hi, wrh) + _dot(lo, wrh) + _dot(hi, wrl)


def _ada_kernel(c_ref, w_ref, b_ref, o_ref):
    a_hi, a_lo = _split2(_silu(c_ref[...]))
    w_hi, w_lo = _split2(w_ref[...])
    o_ref[...] = _dot(a_hi, w_hi) + _dot(a_lo, w_hi) + _dot(a_hi, w_lo) + b_ref[...]


def _ada(c_all, w_ada, b_ada):
    rows = c_all.shape[0]
    n = w_ada.shape[1]
    return pl.pallas_call(
        _ada_kernel,
        out_shape=jax.ShapeDtypeStruct((rows, n), F32),
        grid=(n // ADA_TILE_N,),
        in_specs=[pl.BlockSpec((rows, D_MODEL), lambda j: (0, 0)),
                  pl.BlockSpec((D_MODEL, ADA_TILE_N), lambda j: (0, j)),
                  pl.BlockSpec((1, ADA_TILE_N), lambda j: (0, j))],
        out_specs=pl.BlockSpec((rows, ADA_TILE_N), lambda j: (0, j)),
        compiler_params=pltpu.CompilerParams(dimension_semantics=("parallel",),
                                             vmem_limit_bytes=VMEM_LIMIT),
        name="ada",
    )(c_all, w_ada, b_ada)


def _level_masks():
    i = lax.broadcasted_iota(jnp.int32, (CHUNK, 2 * CHUNK), 0)
    j = lax.broadcasted_iota(jnp.int32, (CHUNK, 2 * CHUNK), 1) & (CHUNK - 1)
    masks = {}
    for lvl in _LEVELS:
        sh = lvl.bit_length() - 1
        same = (i >> sh) == (j >> sh)
        upper = ((i >> (sh - 1)) & 1) == 1
        lower = ((j >> (sh - 1)) & 1) == 0
        masks[lvl] = same & upper & lower
    return masks, i == j


def _by_head(m):
    z = jnp.zeros((m.shape[0], HG_HEAD_DIM), m.dtype)
    return jnp.concatenate([jnp.concatenate([m[:, :HG_HEAD_DIM], z], axis=1),
                            jnp.concatenate([z, m[:, HG_HEAD_DIM:]], axis=1)], axis=0)


def _hgrn_chunk_pair(q, k, v, e_all, lane0, sts, masks, eye):
    hd = HG_HEAD_DIM

    def blk(n):
        return e_all[n * CHUNK:(n + 1) * CHUNK, lane0:lane0 + 2 * hd]

    b = blk(_BLK_CUM)
    kb = k.astype(BF16)
    a = jnp.where(eye, _dot_nt(q.astype(BF16), _by_head(kb)), 0.0)
    for lvl in _LEVELS:
        d = b - blk(_LEVEL_INDEX[lvl])
        ql = (q * jnp.exp(jnp.minimum(d, 0.0))).astype(BF16)
        kl = (k * jnp.exp(jnp.minimum(-d, 0.0))).astype(BF16) if lvl > 2 else kb
        a = a + jnp.where(masks[lvl], _dot_nt(ql, _by_head(kl)), 0.0)
    st_both = jnp.concatenate([jnp.concatenate([sts[0], jnp.zeros_like(sts[0])], axis=1),
                               jnp.concatenate([jnp.zeros_like(sts[1]), sts[1]], axis=1)], axis=0).astype(BF16)
    o = _dot(a.astype(BF16), _by_head(v.astype(BF16))) + _dot_nt((q * jnp.exp(b)).astype(BF16), st_both)
    b_last = b[CHUNK - 1:CHUNK, :]
    k_end = (k * jnp.exp(b_last - b)).astype(BF16)
    decay = jnp.exp(b_last)
    new = []
    for h in range(2):
        lanes = slice(h * hd, (h + 1) * hd)
        new.append(sts[h] * decay[:, lanes] + _dot(v[:, lanes].T.astype(BF16), k_end[:, lanes]))
    return o, new


def _mix_prompt_kernel(x_ref, mod_ref, gpre_ref, gpost_ref, gffn_ref, win_ref, wpool_ref, pscale_ref,
                       lbl_ref, gout_ref, wout_ref, wrh_ref, wrl_ref, m2_ref,
                       x1_ref, h2_ref, lg_ref, pool_ref, st_ref,
                       st_s, ubuf, q_s, k_s, v_s, g_s, o_s):
    t = pl.program_id(1)
    n_t = pl.num_programs(1)
    tt = TIME_TILE

    @pl.when(t == 0)
    def _():
        st_s[...] = jnp.zeros_like(st_s)
        ubuf[0:16, :] = jnp.zeros((16, POOL_WIDTH), F32)

    xt = x_ref[0]
    sh1, sc1, gt1, sh2, sc2, gt2 = _mods(mod_ref[0])
    h = _rms(xt, gpre_ref[...]) * (1.0 + sc1) + sh1
    proj = _dot(h.astype(BF16), win_ref[...])

    u = proj[:, :POOL_WIDTH]
    ubuf[16:16 + tt, :] = u
    pos = (t * tt + lax.broadcasted_iota(jnp.int32, (tt, 1), 0) + 1).astype(F32)
    ys = []
    for g, w in enumerate(POOL_WINDOWS):
        s = ubuf[:, g * POOL_GROUP:(g + 1) * POOL_GROUP]
        off = 0
        for step in range(w.bit_length() - 1):
            sh = 1 << step
            s = s[sh:, :] + s[:-sh, :]
            off += sh
        ws = s[16 - off:16 - off + tt, :]
        cnt = jnp.minimum(pos, float(w))
        d = ws / cnt - u[:, g * POOL_GROUP:(g + 1) * POOL_GROUP]
        ys.append(_dot(d.astype(BF16), wpool_ref[g]))
    y_pool = jnp.concatenate(ys, axis=1) * pscale_ref[...]
    ubuf[0:16, :] = ubuf[tt:tt + 16, :]

    @pl.when(t == n_t - 1)
    def _():
        pool_ref[0] = u[tt - POOL_HIST:, :]

    lb = _forget_lower_bound(lbl_ref[...])
    f = lb + (1.0 - lb) * jax.nn.sigmoid(proj[:, POOL_WIDTH + HG_WIDTH:POOL_WIDTH + 2 * HG_WIDTH])
    q_s[...] = _silu(proj[:, POOL_WIDTH:POOL_WIDTH + HG_WIDTH])
    k_s[...] = 1.0 - f
    v_s[...] = proj[:, POOL_WIDTH + 2 * HG_WIDTH:POOL_WIDTH + 3 * HG_WIDTH]
    g_s[...] = jnp.log(f)
    gate = _silu(proj[:, POOL_WIDTH + 3 * HG_WIDTH:])

    masks, eye = _level_masks()
    m2 = m2_ref[...]

    states = [st_s[hd] for hd in range(HG_HEADS)]
    for c in range(tt // CHUNK):
        rows = slice(c * CHUNK, (c + 1) * CHUNK)
        g_hi, g_lo = _split2(g_s[rows, :])
        e_all = _dot(m2, jnp.concatenate([g_hi, g_lo], axis=0))
        for hd in range(0, HG_HEADS, 2):
            lane0 = hd * HG_HEAD_DIM
            lanes = slice(lane0, lane0 + 2 * HG_HEAD_DIM)
            o, states[hd:hd + 2] = _hgrn_chunk_pair(q_s[rows, lanes], k_s[rows, lanes], v_s[rows, lanes], e_all,
                                                    lane0, states[hd:hd + 2], masks, eye)
            o_s[rows, lanes] = o
    for hd in range(HG_HEADS):
        st_s[hd] = states[hd]

    o = o_s[...]
    os_ = []
    for hd in range(HG_HEADS):
        oh = o[:, hd * HG_HEAD_DIM:(hd + 1) * HG_HEAD_DIM]
        os_.append(oh * lax.rsqrt(jnp.mean(oh * oh, axis=-1, keepdims=True) + EPS) * gout_ref[...])
    o_n = jnp.concatenate(os_, axis=1) * gate

    mix = _dot(y_pool.astype(BF16), wout_ref[0:POOL_WIDTH, :]) + _dot(o_n.astype(BF16), wout_ref[POOL_WIDTH:, :])
    x1 = xt + gt1 * _rms(mix, gpost_ref[...])
    h2 = _rms(x1, gffn_ref[...]) * (1.0 + sc2) + sh2
    x1_ref[0] = x1
    h2_ref[0] = _pack_pair(h2[:, :ROW_WORDS], h2[:, ROW_WORDS:])
    lg_ref[0] = _router_logits(h2, wrh_ref[...], wrl_ref[...])

    @pl.when(t == n_t - 1)
    def _():
        for hd in range(HG_HEADS):
            st_ref[0, hd] = st_s[hd].T


def _full(shape):
    nd = len(shape)
    return pl.BlockSpec(shape, lambda *_: (0,) * nd)


def _mix_prompt(x, mod, gpre, gpost, gffn, win, wpool, pscale, lbl, gout, wout, wrh, wrl, m2):
    b, l, d = x.shape
    tt = TIME_TILE
    tile = lambda i, j: (i, j, 0)
    per_b = lambda i, j: (i, 0, 0)
    return pl.pallas_call(
        _mix_prompt_kernel,
        out_shape=(jax.ShapeDtypeStruct((b, l, d), F32),
                   jax.ShapeDtypeStruct((b, l, ROW_WORDS), jnp.uint32),
                   jax.ShapeDtypeStruct((b, l, LANES), F32),
                   jax.ShapeDtypeStruct((b, POOL_HIST, POOL_WIDTH), F32),
                   jax.ShapeDtypeStruct((b, HG_HEADS, HG_HEAD_DIM, HG_HEAD_DIM), F32)),
        grid=(b, l // tt),
        in_specs=[pl.BlockSpec((1, tt, d), tile),
                  pl.BlockSpec((1, 1, 6 * d), per_b),
                  _full((1, d)), _full((1, d)), _full((1, d)),
                  _full(win.shape), _full(wpool.shape), _full((1, POOL_WIDTH)),
                  _full(lbl.shape), _full((1, HG_HEAD_DIM)), _full(wout.shape),
                  _full(wrh.shape), _full(wrl.shape), _full(m2.shape)],
        out_specs=(pl.BlockSpec((1, tt, d), tile),
                   pl.BlockSpec((1, tt, ROW_WORDS), tile),
                   pl.BlockSpec((1, tt, LANES), tile),
                   pl.BlockSpec((1, POOL_HIST, POOL_WIDTH), per_b),
                   pl.BlockSpec((1, HG_HEADS, HG_HEAD_DIM, HG_HEAD_DIM), lambda i, j: (i, 0, 0, 0))),
        scratch_shapes=[pltpu.VMEM((HG_HEADS, HG_HEAD_DIM, HG_HEAD_DIM), F32),
                        pltpu.VMEM((tt + 16, POOL_WIDTH), F32),
                        pltpu.VMEM((tt, HG_WIDTH), F32), pltpu.VMEM((tt, HG_WIDTH), F32),
                        pltpu.VMEM((tt, HG_WIDTH), F32), pltpu.VMEM((tt, HG_WIDTH), F32),
                        pltpu.VMEM((tt, HG_WIDTH), F32)],
        compiler_params=pltpu.CompilerParams(dimension_semantics=("parallel", "arbitrary"),
                                             vmem_limit_bytes=VMEM_LIMIT),
        name="mix_prompt",
    )(x, mod, gpre, gpost, gffn, win, wpool, pscale, lbl, gout, wout, wrh, wrl, m2)


def _mix_sample_in_kernel(x_ref, mod_ref, gpre_ref, win_ref, wpool_ref, pscale_ref, lbl_ref, hist_ref,
                          ypool_ref, npool_ref, ft_ref, qt_ref, v_ref, gate_ref):
    xt = x_ref[...]
    sh1, sc1 = _mods(mod_ref[...])[:2]
    h = _rms(xt, gpre_ref[...]) * (1.0 + sc1) + sh1
    proj = _dot(h.astype(BF16), win_ref[...])
    u = proj[:, :POOL_WIDTH]
    row = lax.broadcasted_iota(jnp.int32, (hist_ref.shape[0], POOL_HIST, POOL_GROUP), 1)
    ys = []
    for g, w in enumerate(POOL_WINDOWS):
        sl = slice(g * POOL_GROUP, (g + 1) * POOL_GROUP)
        past = jnp.sum(jnp.where(row >= POOL_HIST - (w - 1), hist_ref[:, :, sl], 0.0), axis=1)
        ug = u[:, sl]
        d = (past + ug) / float(w) - ug
        ys.append(_dot(d.astype(BF16), wpool_ref[g]))
    ypool_ref[...] = jnp.concatenate(ys, axis=1) * pscale_ref[...]
    npool_ref[:, 0:POOL_HIST - 1, :] = hist_ref[:, 1:POOL_HIST, :]
    npool_ref[:, POOL_HIST - 1, :] = u

    lb = _forget_lower_bound(lbl_ref[...])
    f = lb + (1.0 - lb) * jax.nn.sigmoid(proj[:, POOL_WIDTH + HG_WIDTH:POOL_WIDTH + 2 * HG_WIDTH])
    ft_ref[...] = f.T
    qt_ref[...] = _silu(proj[:, POOL_WIDTH:POOL_WIDTH + HG_WIDTH]).T
    v_ref[...] = proj[:, POOL_WIDTH + 2 * HG_WIDTH:POOL_WIDTH + 3 * HG_WIDTH]
    gate_ref[...] = _silu(proj[:, POOL_WIDTH + 3 * HG_WIDTH:])


def _mix_sample_state_kernel(s_ref, ft_ref, qt_ref, v_ref, snew_ref, o_ref):
    i = pl.program_id(0)
    lane = lax.broadcasted_iota(jnp.int32, (HG_HEAD_DIM, ft_ref.shape[1]), 1)
    for j in range(SAMPLE_STATE_BLOCK):
        mine = lane == i * SAMPLE_STATE_BLOCK + j
        for hd in range(HG_HEADS):
            r0 = hd * HG_HEAD_DIM
            f = jnp.sum(jnp.where(mine, ft_ref[r0:r0 + HG_HEAD_DIM, :], 0.0), axis=1, keepdims=True)
            q = jnp.sum(jnp.where(mine, qt_ref[r0:r0 + HG_HEAD_DIM, :], 0.0), axis=1, keepdims=True)
            v = v_ref[j:j + 1, r0:r0 + HG_HEAD_DIM]
            s_new = f * s_ref[j, hd] + (1.0 - f) * v
            snew_ref[j, hd] = s_new
            o_ref[j:j + 1, r0:r0 + HG_HEAD_DIM] = jnp.sum(q * s_new, axis=0, keepdims=True)


def _mix_sample_out_kernel(x_ref, mod_ref, o_ref, gate_ref, ypool_ref, gout_ref, wout_ref, gpost_ref,
                           gffn_ref, wrh_ref, wrl_ref, x1_ref, h2_ref, lg_ref):
    _, _, gt1, sh2, sc2, _ = _mods(mod_ref[...])
    o = o_ref[...]
    os_ = []
    for hd in range(HG_HEADS):
        oh = o[:, hd * HG_HEAD_DIM:(hd + 1) * HG_HEAD_DIM]
        os_.append(oh * lax.rsqrt(jnp.mean(oh * oh, axis=-1, keepdims=True) + EPS) * gout_ref[...])
    o_n = jnp.concatenate(os_, axis=1) * gate_ref[...]
    mix = (_dot(ypool_ref[...].astype(BF16), wout_ref[0:POOL_WIDTH, :])
           + _dot(o_n.astype(BF16), wout_ref[POOL_WIDTH:, :]))
    x1 = x_ref[...] + gt1 * _rms(mix, gpost_ref[...])
    h2 = _rms(x1, gffn_ref[...]) * (1.0 + sc2) + sh2
    x1_ref[...] = x1
    h2_ref[...] = h2.astype(BF16)
    lg_ref[...] = _router_logits(h2, wrh_ref[...], wrl_ref[...])


def _mix_sample(x, mod, hist, state, gpre, gpost, gffn, win, wpool, pscale, lbl, gout, wout, wrh, wrl):
    b = x.shape[0]
    bb = SAMPLE_STATE_BLOCK
    cp = pltpu.CompilerParams(vmem_limit_bytes=VMEM_LIMIT)
    ypool, npool, ft, qt, v, gate = pl.pallas_call(
        _mix_sample_in_kernel,
        out_shape=(jax.ShapeDtypeStruct((b, POOL_WIDTH), F32),
                   jax.ShapeDtypeStruct((b, POOL_HIST, POOL_WIDTH), F32),
                   jax.ShapeDtypeStruct((HG_WIDTH, b), F32),
                   jax.ShapeDtypeStruct((HG_WIDTH, b), F32),
                   jax.ShapeDtypeStruct((b, HG_WIDTH), F32),
                   jax.ShapeDtypeStruct((b, HG_WIDTH), F32)),
        compiler_params=cp,
        name="mix_sample_in",
    )(x, mod, gpre, win, wpool, pscale, lbl, hist)

    s_spec = pl.BlockSpec((bb, HG_HEADS, HG_HEAD_DIM, HG_HEAD_DIM), lambda i: (i, 0, 0, 0))
    col_spec = _full((HG_WIDTH, b))
    row_spec = pl.BlockSpec((bb, HG_WIDTH), lambda i: (i, 0))
    s_new, o = pl.pallas_call(
        _mix_sample_state_kernel,
        out_shape=(jax.ShapeDtypeStruct(state.shape, F32), jax.ShapeDtypeStruct((b, HG_WIDTH), F32)),
        grid=(b // bb,),
        in_specs=[s_spec, col_spec, col_spec, row_spec],
        out_specs=(s_spec, row_spec),
        compiler_params=pltpu.CompilerParams(dimension_semantics=("parallel",), vmem_limit_bytes=VMEM_LIMIT),
        name="mix_sample_state",
    )(state, ft, qt, v)

    x1, h2, lg = pl.pallas_call(
        _mix_sample_out_kernel,
        out_shape=(jax.ShapeDtypeStruct((b, D_MODEL), F32),
                   jax.ShapeDtypeStruct((b, D_MODEL), BF16),
                   jax.ShapeDtypeStruct((b, LANES), F32)),
        compiler_params=cp,
        name="mix_sample_out",
    )(x, mod, o, gate, ypool, gout, wout, gpost, gffn, wrh, wrl)
    return x1, h2, lg, npool, s_new


def _first_max(cur, idx, big):
    m = jnp.max(cur, axis=0, keepdims=True)
    first = jnp.min(jnp.where(cur == m, idx, big), axis=0, keepdims=True)
    return idx == first


def _route(logit, bias):
    n = logit.shape[1]
    scores = jax.nn.sigmoid(logit)
    sel = scores + bias
    neg = -jnp.inf

    sub = lax.broadcasted_iota(jnp.int32, (GROUP_SIZE, n), 0)
    gscore = []
    for g in range(N_GROUPS):
        sg = sel[g * GROUP_SIZE:(g + 1) * GROUP_SIZE, :]
        m1 = jnp.max(sg, axis=0, keepdims=True)
        rest = jnp.where(_first_max(sg, sub, GROUP_SIZE), neg, sg)
        gscore.append(m1 + jnp.max(rest, axis=0, keepdims=True))
    cur = jnp.concatenate(gscore, axis=0)
    gidx = lax.broadcasted_iota(jnp.int32, (N_GROUPS, n), 0)
    gmask = jnp.zeros((N_GROUPS, n), jnp.bool_)
    for _ in range(TOPK_GROUPS):
        pick = _first_max(cur, gidx, N_GROUPS)
        gmask = gmask | pick
        cur = jnp.where(pick, neg, cur)
    emask = jnp.concatenate(
        [jnp.broadcast_to(gmask[g:g + 1, :], (GROUP_SIZE, n)) for g in range(N_GROUPS)], axis=0)

    cur = jnp.where(emask, sel, neg)
    eidx = lax.broadcasted_iota(jnp.int32, (N_EXPERTS, n), 0)
    picks = []
    for _ in range(TOP_K):
        pick = _first_max(cur, eidx, N_EXPERTS)
        picks.append(pick)
        cur = jnp.where(pick, neg, cur)
    return scores, picks


def _any(masks):
    return functools.reduce(jnp.logical_or, masks)


def _router_kernel(lg_ref, bias_ref, wd_ref):
    n = lg_ref.shape[0]
    scores, picks = _route(lg_ref[...].T[0:N_EXPERTS, :], bias_ref[0:N_EXPERTS, :])
    chosen = _any(picks)
    wsum = jnp.sum(jnp.where(chosen, scores, 0.0), axis=0, keepdims=True)
    wd = jnp.where(chosen, scores / wsum * ROUTED_SCALE, 0.0)
    wd_ref[...] = jnp.concatenate([wd, jnp.zeros((LANES - N_EXPERTS, n), F32)], axis=0).T


def _router(logits, bias_col):
    t = logits.shape[0]
    tile = min(ROUTER_TILE, t)
    return pl.pallas_call(
        _router_kernel,
        out_shape=jax.ShapeDtypeStruct((t, LANES), F32),
        grid=(t // tile,),
        in_specs=[pl.BlockSpec((tile, LANES), lambda i: (i, 0)), _full(bias_col.shape)],
        out_specs=pl.BlockSpec((tile, LANES), lambda i: (i, 0)),
        compiler_params=pltpu.CompilerParams(dimension_semantics=("parallel",), vmem_limit_bytes=VMEM_LIMIT),
        name="router",
    )(logits, bias_col)


def _plan(lg_ref, bias_ref, upper_ref, dest_ref, wsel_ref, cnt_ref, loff_ref):
    scores, picks = _route(lg_ref[...].T[0:N_EXPERTS, :], bias_ref[0:N_EXPERTS, :])
    chosen = _any(picks)
    chosen_f = jnp.where(chosen, 1.0, 0.0)
    cnt = jnp.sum(chosen_f, axis=1, keepdims=True)
    units = jnp.floor((cnt + (RUN_ALIGN - 1)) / RUN_ALIGN)
    ei = lax.broadcasted_iota(jnp.int32, (N_EXPERTS, N_EXPERTS), 0)
    ej = lax.broadcasted_iota(jnp.int32, (N_EXPERTS, N_EXPERTS), 1)
    before = jnp.where(ej < ei, 1.0, 0.0).astype(BF16)
    loff = RUN_ALIGN * _dot(before, jnp.broadcast_to(units, (N_EXPERTS, LANES)).astype(BF16))[:, 0:1]
    slot = loff + _dot(chosen_f.astype(BF16), upper_ref[...])
    wsum = jnp.sum(jnp.where(chosen, scores, 0.0), axis=0, keepdims=True)
    pick_sum = lambda v: jnp.concatenate(
        [jnp.sum(jnp.where(p, v, 0.0), axis=0, keepdims=True) for p in picks], axis=0)
    dest_ref[0] = (pick_sum(slot) * ROW_SUB).astype(jnp.int32)
    wsel_ref[0] = pick_sum(scores) / wsum * ROUTED_SCALE
    cnt_ref[0] = jnp.broadcast_to(cnt, (N_EXPERTS, LANES))
    loff_ref[0] = jnp.broadcast_to(loff, (N_EXPERTS, LANES))


def _slot_rows(slot):
    return pl.ds(pl.multiple_of(slot * ROW_SUB, ROW_SUB), ROW_SUB)


def _table_copies(table_ref, flat_s, sem):
    return [pltpu.make_async_copy(table_ref.at[0, k], flat_s.at[pl.ds(k * PLAN_WINDOW, PLAN_WINDOW)], sem)
            for k in range(TOP_K)]


def _rows_at(offset):
    return pl.ds(pl.multiple_of(offset, ROW_SUB), ROW_SUB)


def _dispatch_kernel(lg_ref, bias_ref, upper_ref, h_ref, dest_ref, wsel_ref, cnt_ref, loff_ref, x_ref,
                     dest_s, rows_s, sem):
    _plan(lg_ref, bias_ref, upper_ref, dest_ref, wsel_ref, cnt_ref, loff_ref)
    plan_copies = _table_copies(dest_ref, dest_s, sem)
    for c in plan_copies:
        c.start()
    x_ref[...] = jnp.zeros_like(x_ref)
    for j in range(ROW_SUB):
        rows_s[pl.ds(j, PLAN_WINDOW, stride=ROW_SUB), :] = h_ref[:, j * LANES:(j + 1) * LANES]
    for c in plan_copies:
        c.wait()

    def body(i, carry):
        for u in range(SCATTER_UNROLL):
            t = i * SCATTER_UNROLL + u
            row = rows_s[_slot_rows(t), :]
            for k in range(TOP_K):
                x_ref[_rows_at(dest_s[k * PLAN_WINDOW + t]), :] = row
        return carry

    lax.fori_loop(0, PLAN_WINDOW // SCATTER_UNROLL, body, 0)


def _dispatch(logits, bias_col, upper, h2):
    nw = logits.shape[0] // PLAN_WINDOW
    region = REGION_ROWS * ROW_SUB
    per_w = pl.BlockSpec((1, N_EXPERTS, LANES), lambda w: (w, 0, 0))
    picks = pl.BlockSpec((1, TOP_K, PLAN_WINDOW), lambda w: (w, 0, 0))
    return pl.pallas_call(
        _dispatch_kernel,
        out_shape=(jax.ShapeDtypeStruct((nw, TOP_K, PLAN_WINDOW), jnp.int32),
                   jax.ShapeDtypeStruct((nw, TOP_K, PLAN_WINDOW), F32),
                   jax.ShapeDtypeStruct((nw, N_EXPERTS, LANES), F32),
                   jax.ShapeDtypeStruct((nw, N_EXPERTS, LANES), F32),
                   jax.ShapeDtypeStruct((nw * region, LANES), jnp.uint32)),
        grid=(nw,),
        in_specs=[pl.BlockSpec((PLAN_WINDOW, LANES), lambda w: (w, 0)), _full(bias_col.shape), _full(upper.shape),
                  pl.BlockSpec((PLAN_WINDOW, ROW_WORDS), lambda w: (w, 0))],
        out_specs=(picks, picks, per_w, per_w, pl.BlockSpec((region, LANES), lambda w: (w, 0))),
        scratch_shapes=[pltpu.SMEM((TOP_K * PLAN_WINDOW,), jnp.int32),
                        pltpu.VMEM((PLAN_WINDOW * ROW_SUB, LANES), jnp.uint32),
                        pltpu.SemaphoreType.DMA],
        compiler_params=pltpu.CompilerParams(dimension_semantics=("arbitrary",), vmem_limit_bytes=VMEM_LIMIT),
        name="dispatch",
    )(logits, bias_col, upper, h2)


def _unpack_rows(ref, n):
    parts = [_unpack_pair(ref[pl.ds(j, n, stride=ROW_SUB), :]) for j in range(ROW_SUB)]
    return jnp.concatenate([p[0] for p in parts] + [p[1] for p in parts], axis=1)


def _ffn_kernel(gq_ref, gs_ref, to_ref, tn_ref, x_hbm, w1_ref, w3_ref, w2_ref,
                hs_ref, wds_ref, x1s_ref, gt2s_ref, gpost_ref, ws1_ref, ws3_ref, ws2_ref, y_hbm, ys_ref,
                w13_s, w2_s, xbuf, ybuf, rsem, wsem, acc_s):
    del x_hbm
    e = pl.program_id(0)
    n_groups = gq_ref[N_EXPERTS]
    block = RUN_ALIGN * ROW_SUB
    tile = FFN_TILE * ROW_SUB

    def go(cp, start):
        if start:
            cp.start()
        else:
            cp.wait()

    def fetch(q, start):
        def one(j):
            src = y_hbm.at[pl.ds(pl.multiple_of(to_ref[gs_ref[q] + j] * block, block), tile)]
            go(pltpu.make_async_copy(src, xbuf.at[q % FFN_RING, pl.ds(j * tile, tile)], rsem.at[q % FFN_RING]), start)

        size = gs_ref[q + 1] - gs_ref[q]

        @pl.when(size == FFN_GROUP)
        def _():
            for j in range(FFN_GROUP):
                one(j)

        @pl.when(size < FFN_GROUP)
        def _():
            for j in range(FFN_GROUP - 1):
                pl.when(j < size)(functools.partial(one, j))

    def writeback(q, start):
        most = FFN_TILE // RUN_ALIGN - 1

        def blocks(j, b, n):
            dst = pl.multiple_of(to_ref[gs_ref[q] + j] * block, block) + b * block
            go(pltpu.make_async_copy(ybuf.at[q % FFN_RING, pl.ds(j * tile + b * block, n * block)],
                                     y_hbm.at[pl.ds(dst, n * block)], wsem.at[q % FFN_RING]), start)

        for j in range(FFN_GROUP):
            item = gs_ref[q] + j
            n = jnp.where(item < gs_ref[q + 1], tn_ref[item], 0)
            pl.when(n >= most)(functools.partial(blocks, j, 0, most))
            pl.when(n > most)(functools.partial(blocks, j, most, 1))

            @pl.when((n > 0) & (n < most))
            def _():
                for b in range(most - 1):
                    pl.when(b < n)(functools.partial(blocks, j, b, 1))

    @pl.when(e == 0)
    def _():
        xbuf[...] = jnp.zeros_like(xbuf)
        for d in range(FFN_RING - 1):
            pl.when(d < n_groups)(functools.partial(fetch, d, True))

    w13_s[:, 0:D_EXPERT] = w1_ref[0].astype(BF16)
    w13_s[:, D_EXPERT:] = w3_ref[0].astype(BF16)
    w2_s[...] = w2_ref[0].astype(BF16)

    xs = hs_ref[...]

    @pl.when(e == 0)
    def _():
        sh = _silu(_dot(xs, ws1_ref[...].astype(BF16))) * _dot(xs, ws3_ref[...].astype(BF16))
        acc_s[...] = _dot(sh.astype(BF16), ws2_ref[...].astype(BF16))

    lane = lax.broadcasted_iota(jnp.int32, wds_ref.shape, 1)
    wcol = jnp.sum(jnp.where(lane == e, wds_ref[...], 0.0), axis=1, keepdims=True)
    hd = _dot(xs, w13_s[...])
    acc_s[...] += _dot((_silu(hd[:, :D_EXPERT]) * hd[:, D_EXPERT:] * wcol).astype(BF16), w2_s[...])

    @pl.when(e == pl.num_programs(0) - 1)
    def _():
        ys_ref[...] = x1s_ref[...] + gt2s_ref[...] * _rms(acc_s[...], gpost_ref[...])

    def group_body(q, carry):
        @pl.when(q + FFN_RING - 1 < n_groups)
        def _():
            fetch(q + FFN_RING - 1, True)

        @pl.when(q >= FFN_RING)
        def _():
            writeback(q - FFN_RING, False)

        fetch(q, False)
        rows = FFN_GROUP * FFN_TILE // FFN_SPLIT
        for part in range(FFN_SPLIT):
            span = pl.ds(part * rows * ROW_SUB, rows * ROW_SUB)
            x = _unpack_rows(xbuf.at[q % FFN_RING, span], rows).astype(BF16)
            h = _dot(x, w13_s[...])
            act = _silu(h[:, :D_EXPERT]) * h[:, D_EXPERT:]
            y = _dot(act.astype(BF16), w2_s[...])
            out = ybuf.at[q % FFN_RING, span]
            for j in range(ROW_SUB):
                out[pl.ds(j, rows, stride=ROW_SUB), :] = _pack_pair(
                    y[:, j * LANES:(j + 1) * LANES], y[:, ROW_WORDS + j * LANES:ROW_WORDS + (j + 1) * LANES])
        writeback(q, True)
        return carry

    lax.fori_loop(gq_ref[e], gq_ref[e + 1], group_body, 0)

    @pl.when(e == pl.num_programs(0) - 1)
    def _():
        for d in range(FFN_RING):
            @pl.when(n_groups - 1 - d >= 0)
            def _():
                writeback(n_groups - 1 - d, False)


def _expert_ffn(group_of_expert, group_start, item_off, item_blocks, x_slots, w1, w3, w2,
                h2_s, wd_s, x1_s, gt2_s, gpost, ws1, ws3, ws2):
    of_expert = lambda e, *_: (e, 0, 0)
    whole = lambda a: pl.BlockSpec(a.shape, lambda e, *_: (0,) * a.ndim)
    tiles = pltpu.VMEM((FFN_RING, FFN_GROUP * FFN_TILE * ROW_SUB, LANES), jnp.uint32)
    sems = pltpu.SemaphoreType.DMA((FFN_RING,))
    dense = (h2_s, wd_s, x1_s, gt2_s, gpost, ws1, ws3, ws2)
    grid_spec = pltpu.PrefetchScalarGridSpec(
        num_scalar_prefetch=4,
        grid=(N_EXPERTS,),
        in_specs=[pl.BlockSpec(memory_space=pl.ANY),
                  pl.BlockSpec((1, D_MODEL, D_EXPERT), of_expert),
                  pl.BlockSpec((1, D_MODEL, D_EXPERT), of_expert),
                  pl.BlockSpec((1, D_EXPERT, D_MODEL), of_expert)] + [whole(a) for a in dense],
        out_specs=(pl.BlockSpec(memory_space=pl.ANY), whole(x1_s)),
        scratch_shapes=[pltpu.VMEM((D_MODEL, 2 * D_EXPERT), BF16), pltpu.VMEM((D_EXPERT, D_MODEL), BF16),
                        tiles, tiles, sems, sems, pltpu.VMEM(x1_s.shape, F32)])
    return pl.pallas_call(
        _ffn_kernel,
        out_shape=(jax.ShapeDtypeStruct(x_slots.shape, jnp.uint32), jax.ShapeDtypeStruct(x1_s.shape, F32)),
        grid_spec=grid_spec,
        input_output_aliases={4: 0},
        compiler_params=pltpu.CompilerParams(dimension_semantics=("arbitrary",), vmem_limit_bytes=VMEM_LIMIT),
        name="expert_ffn",
    )(group_of_expert, group_start, item_off, item_blocks, x_slots, w1, w3, w2, *dense)


def _work_items(cnt, loff):
    nw = cnt.shape[0]
    cnt = cnt.astype(jnp.int32).T.reshape(-1)
    loff = loff.astype(jnp.int32).T.reshape(-1)
    region = jnp.tile(jnp.arange(nw, dtype=jnp.int32) * REGION_ROWS, N_EXPERTS)
    n_items = nw * N_EXPERTS + (nw * PLAN_WINDOW * TOP_K) // FFN_TILE
    tiles = (cnt + FFN_TILE - 1) // FFN_TILE
    ends = jnp.cumsum(tiles)
    starts = ends - tiles
    i = jnp.arange(n_items + FFN_GROUP, dtype=jnp.int32)[:, None]
    mine = (starts[None, :] <= i) & (i < ends[None, :])
    of_pair = lambda v: jnp.sum(jnp.where(mine, v[None, :], 0), axis=1)
    r = i[:, 0] - of_pair(starts)
    row = of_pair(region + loff) + r * FFN_TILE
    own = jnp.clip(of_pair(cnt) - r * FFN_TILE, 0, FFN_TILE)

    zero = jnp.zeros((1,), jnp.int32)
    item_end = ends.reshape(N_EXPERTS, nw)[:, -1]
    item_start = jnp.concatenate([zero, item_end[:-1]])
    group_end = jnp.cumsum((item_end - item_start + FFN_GROUP - 1) // FFN_GROUP)
    group_first = jnp.concatenate([zero, group_end[:-1]])
    q = jnp.arange(N_EXPERTS + n_items // FFN_GROUP + 1, dtype=jnp.int32)[:, None]
    has = (group_first[None, :] <= q) & (q < group_end[None, :])
    of_expert = lambda v: jnp.sum(jnp.where(has, v[None, :], 0), axis=1)
    group_start = jnp.where(q[:, 0] < group_end[-1],
                            of_expert(item_start) + (q[:, 0] - of_expert(group_first)) * FFN_GROUP, item_end[-1])
    return (jnp.concatenate([zero, group_end]), group_start, row // RUN_ALIGN, (own + RUN_ALIGN - 1) // RUN_ALIGN)


def _combine_kernel(dest_ref, wsel_ref, y_ref, h_ref, x1_ref, gt2_ref, gpost_ref, ws13_ref, ws2_ref, out_ref,
                    dest_s, wsel_s, lo_s, hi_s, sem):
    s = pl.program_id(1)

    @pl.when(s == 0)
    def _():
        copies = _table_copies(dest_ref, dest_s, sem.at[0]) + _table_copies(wsel_ref, wsel_s, sem.at[1])
        for c in copies:
            c.start()
        for c in copies:
            c.wait()

    base = s * COMBINE_TILE

    def body(i, carry):
        for u in range(COMBINE_UNROLL):
            t = i * COMBINE_UNROLL + u
            acc_lo = jnp.zeros((ROW_SUB, LANES), F32)
            acc_hi = jnp.zeros((ROW_SUB, LANES), F32)
            for k in range(TOP_K):
                at = k * PLAN_WINDOW + base + t
                lo, hi = _unpack_pair(y_ref[_rows_at(dest_s[at]), :])
                w = wsel_s[at]
                acc_lo = acc_lo + w * lo
                acc_hi = acc_hi + w * hi
            lo_s[_slot_rows(t), :] = acc_lo
            hi_s[_slot_rows(t), :] = acc_hi
        return carry

    lax.fori_loop(0, COMBINE_TILE // COMBINE_UNROLL, body, 0)
    rows = lambda ref: [ref[pl.ds(j, COMBINE_TILE, stride=ROW_SUB), :] for j in range(ROW_SUB)]
    routed = jnp.concatenate(rows(lo_s) + rows(hi_s), axis=1)
    h_lo, h_hi = _unpack_pair(h_ref[...])
    x = jnp.concatenate([h_lo, h_hi], axis=1).astype(BF16)
    hs = _dot(x, ws13_ref[...])
    act = _silu(hs[:, :D_EXPERT]) * hs[:, D_EXPERT:]
    ff = routed + _dot(act.astype(BF16), ws2_ref[...])
    out_ref[...] = x1_ref[...] + gt2_ref[0] * _rms(ff, gpost_ref[...])


def _combine(dest_w, wsel_w, y_slots, h2, x1, gt2, tokens_per_gt2, gpost, ws13, ws2):
    nw = dest_w.shape[0]
    sub = PLAN_WINDOW // COMBINE_TILE
    plan = pl.BlockSpec((1, TOP_K, PLAN_WINDOW), lambda w, s: (w, 0, 0))
    tok = lambda w, s: (w * sub + s, 0)
    return pl.pallas_call(
        _combine_kernel,
        out_shape=jax.ShapeDtypeStruct(x1.shape, F32),
        grid=(nw, sub),
        in_specs=[plan, plan,
                  pl.BlockSpec((REGION_ROWS * ROW_SUB, LANES), lambda w, s: (w, 0)),
                  pl.BlockSpec((COMBINE_TILE, ROW_WORDS), tok),
                  pl.BlockSpec((COMBINE_TILE, D_MODEL), tok),
                  pl.BlockSpec((1, 1, D_MODEL), lambda w, s: ((w * PLAN_WINDOW) // tokens_per_gt2, 0, 0)),
                  _full((1, D_MODEL)), _full(ws13.shape), _full(ws2.shape)],
        out_specs=pl.BlockSpec((COMBINE_TILE, D_MODEL), tok),
        scratch_shapes=[pltpu.SMEM((TOP_K * PLAN_WINDOW,), jnp.int32),
                        pltpu.SMEM((TOP_K * PLAN_WINDOW,), F32),
                        pltpu.VMEM((COMBINE_TILE * ROW_SUB, LANES), F32),
                        pltpu.VMEM((COMBINE_TILE * ROW_SUB, LANES), F32),
                        pltpu.SemaphoreType.DMA((2,))],
        compiler_params=pltpu.CompilerParams(dimension_semantics=("arbitrary", "arbitrary"),
                                             vmem_limit_bytes=VMEM_LIMIT),
        name="combine",
    )(dest_w, wsel_w, y_slots, h2, x1, gt2, gpost, ws13, ws2)


def kernel(x_prompt, x_sample, c_prompt, c_sample, state_pool, state_hgrn, w_ada, b_ada, g_pre_mix, g_post_mix,
           w_in, w_pool, pool_scale, lb_logits, g_out_norm, w_out, g_pre_ffn, g_post_ffn, w_router, router_bias,
           w_exp_gate, w_exp_up, w_exp_down, w_sh_gate, w_sh_up, w_sh_down):
    assert w_ada.shape[0] == 1 and lb_logits.shape[0] == 2, "single-layer trunk"
    bp, lp, d = x_prompt.shape
    bs = x_sample.shape[0]
    row = lambda a: a[0].reshape(1, -1)

    n_mod = bp + bs
    pad = (-n_mod) % 16
    c_all = jnp.concatenate([c_prompt, c_sample, jnp.zeros((pad, d), F32)], axis=0)
    mod = _ada(c_all, w_ada[0], b_ada)
    mod_p = mod[:bp].reshape(bp, 1, 6 * d)
    mod_s = mod[bp:n_mod]

    win = w_in[0].astype(BF16)
    wout = w_out[0].astype(BF16)
    wpool = w_pool[0].astype(BF16)
    wr = jnp.pad(w_router[0], ((0, 0), (0, LANES - N_EXPERTS)))
    wrh = wr.astype(BF16)
    wrl = (wr - wrh.astype(F32)).astype(BF16)
    m2 = jnp.asarray(_M2_NP, BF16)
    gpre, gpost, gffn, gffn_post = row(g_pre_mix), row(g_post_mix), row(g_pre_ffn), row(g_post_ffn)
    pscale, gout = row(pool_scale), row(g_out_norm)
    bias_col = jnp.pad(router_bias[0], (0, LANES - N_EXPERTS)).reshape(LANES, 1)

    x1_p, h2_p, lg_p, pool_p, st_p = _mix_prompt(x_prompt, mod_p, gpre, gpost, gffn, win, wpool, pscale,
                                                  lb_logits, gout, wout, wrh, wrl, m2)
    x1_s, h2_s, lg_s, pool_s, st_s = _mix_sample(x_sample[:, 0, :], mod_s, state_pool[0], state_hgrn[0],
                                                  gpre, gpost, gffn, win, wpool, pscale, lb_logits, gout, wout,
                                                  wrh, wrl)

    experts = (w_exp_gate[0], w_exp_up[0], w_exp_down[0], w_sh_gate[0], w_sh_up[0], w_sh_down[0])
    tp = bp * lp
    nw = tp // PLAN_WINDOW
    assert lp % PLAN_WINDOW == 0
    upper = jnp.asarray(np.triu(np.ones((PLAN_WINDOW, PLAN_WINDOW), np.float32), 1), BF16)
    h2_rows = h2_p.reshape(tp, ROW_WORDS)
    dest_w, wsel_w, cnt, loff, x_slots = _dispatch(lg_p.reshape(tp, LANES), bias_col, upper, h2_rows)
    wd_s = _router(lg_s, bias_col)
    y_slots, y_s = _expert_ffn(*_work_items(cnt[:, :, 0], loff[:, :, 0]), x_slots, *experts[:3],
                               h2_s, wd_s, x1_s, mod_s[:, 5 * d:], gffn_post, *experts[3:])
    ws13 = jnp.concatenate([experts[3], experts[4]], axis=1).astype(BF16)
    gt2_p = mod_p[:, :, 5 * d:]
    y_p = _combine(dest_w, wsel_w, y_slots, h2_rows, x1_p.reshape(tp, d), gt2_p, lp, gffn_post,
                   ws13, experts[5].astype(BF16))

    return (y_p.reshape(bp, lp, d), y_s.reshape(bs, 1, d), pool_p[None], st_p[None], pool_s[None], st_s[None])
```

```python
import functools

import numpy as np
import jax
import jax.numpy as jnp
from jax import lax
from jax.experimental import pallas as pl
from jax.experimental.pallas import tpu as pltpu

F32 = jnp.float32
BF16 = jnp.bfloat16

D_MODEL = 1024
POOL_WIDTH = 512
POOL_WINDOWS = (2, 4, 8, 16)
POOL_GROUP = 128
POOL_HIST = 15
HG_WIDTH = 512
HG_HEADS = 4
HG_HEAD_DIM = 128
IN_WIDTH = POOL_WIDTH + 4 * HG_WIDTH
N_EXPERTS = 64
TOP_K = 8
N_GROUPS = 8
TOPK_GROUPS = 4
GROUP_SIZE = N_EXPERTS // N_GROUPS
D_EXPERT = 256
ROUTED_SCALE = 2.5
EPS = 1e-6

LANES = 128
CHUNK = 64
TIME_TILE = 256
ADA_TILE_N = 1024
ROUTER_TILE = 512
SAMPLE_STATE_BLOCK = 16
PLAN_WINDOW = 1024
RUN_ALIGN = 32
FFN_TILE = 160
REGION_ROWS = PLAN_WINDOW * TOP_K + N_EXPERTS * (RUN_ALIGN - 1) + FFN_TILE
ROW_WORDS = D_MODEL // 2
ROW_SUB = ROW_WORDS // LANES
COMBINE_TILE = 256
REGION_PARTS = 4
SCATTER_UNROLL = 4
COMBINE_UNROLL = 16
FFN_GROUP = 4
FFN_SPLIT = 1
FFN_RING = 6
VMEM_LIMIT = 58 * 1024 * 1024

_LEVELS = (64, 32, 16, 8, 4, 2)
_BLK_CUM = 0


def _level_blocks():
    c = CHUNK
    i = np.arange(c)[:, None]
    s = np.arange(c)[None, :]
    blocks = [(s <= i)]
    index = {}
    for lvl in _LEVELS:
        ref = (i // lvl) * lvl + lvl // 2 - 1
        index[lvl] = len(blocks)
        blocks.append(s <= ref)
    m = np.concatenate(blocks, axis=0).astype(np.float32)
    return np.concatenate([m, m], axis=1), index


_M2_NP, _LEVEL_INDEX = _level_blocks()


def _dot(a, b):
    return jnp.dot(a, b, preferred_element_type=F32)


def _dot_nt(a, b):
    return lax.dot_general(a, b, (((1,), (1,)), ((), ())), preferred_element_type=F32)


def _split2(x):
    hi = x.astype(BF16)
    lo = (x - hi.astype(F32)).astype(BF16)
    return hi, lo


def _pack_pair(a, b):
    lo = lax.bitcast_convert_type(a.astype(BF16).astype(F32), jnp.uint32)
    hi = lax.bitcast_convert_type(b.astype(BF16).astype(F32), jnp.uint32)
    return (lo >> 16) | hi


def _unpack_pair(words):
    lo = lax.bitcast_convert_type(words << 16, F32)
    hi = lax.bitcast_convert_type(words & jnp.uint32(0xFFFF0000), F32)
    return lo, hi


def _silu(x):
    return x * jax.nn.sigmoid(x)


def _rms(x, g):
    return x * lax.rsqrt(jnp.mean(x * x, axis=-1, keepdims=True) + EPS) * g


def _mods(m):
    return [m[:, j * D_MODEL:(j + 1) * D_MODEL] for j in range(6)]


def _forget_lower_bound(lbl):
    mx = jnp.max(lbl, axis=0, keepdims=True)
    e = jnp.exp(lbl - mx)
    return e[0:1] / jnp.sum(e, axis=0, keepdims=True)


def _router_logits(h2, wrh, wrl):
    hi, lo = _split2(h2)
    return _dot(hi, wrh) + _dot(lo, wrh) + _dot(hi, wrl)


def _ada_kernel(c_ref, w_ref, b_ref, o_ref):
    a_hi, a_lo = _split2(_silu(c_ref[...]))
    w_hi, w_lo = _split2(w_ref[...])
    o_ref[...] = _dot(a_hi, w_hi) + _dot(a_lo, w_hi) + _dot(a_hi, w_lo) + b_ref[...]


def _ada(c_all, w_ada, b_ada):
    rows = c_all.shape[0]
    n = w_ada.shape[1]
    return pl.pallas_call(
        _ada_kernel,
        out_shape=jax.ShapeDtypeStruct((rows, n), F32),
        grid=(n // ADA_TILE_N,),
        in_specs=[pl.BlockSpec((rows, D_MODEL), lambda j: (0, 0)),
                  pl.BlockSpec((D_MODEL, ADA_TILE_N), lambda j: (0, j)),
                  pl.BlockSpec((1, ADA_TILE_N), lambda j: (0, j))],
        out_specs=pl.BlockSpec((rows, ADA_TILE_N), lambda j: (0, j)),
        compiler_params=pltpu.CompilerParams(dimension_semantics=("parallel",),
                                             vmem_limit_bytes=VMEM_LIMIT),
        name="ada",
    )(c_all, w_ada, b_ada)


def _level_masks():
    i = lax.broadcasted_iota(jnp.int32, (CHUNK, 2 * CHUNK), 0)
    j = lax.broadcasted_iota(jnp.int32, (CHUNK, 2 * CHUNK), 1) & (CHUNK - 1)
    masks = {}
    for lvl in _LEVELS:
        sh = lvl.bit_length() - 1
        same = (i >> sh) == (j >> sh)
        upper = ((i >> (sh - 1)) & 1) == 1
        lower = ((j >> (sh - 1)) & 1) == 0
        masks[lvl] = same & upper & lower
    return masks, i == j


def _by_head(m):
    z = jnp.zeros((m.shape[0], HG_HEAD_DIM), m.dtype)
    return jnp.concatenate([jnp.concatenate([m[:, :HG_HEAD_DIM], z], axis=1),
                            jnp.concatenate([z, m[:, HG_HEAD_DIM:]], axis=1)], axis=0)


def _hgrn_chunk_pair(q, k, v, e_all, lane0, sts, masks, eye):
    hd = HG_HEAD_DIM

    def blk(n):
        return e_all[n * CHUNK:(n + 1) * CHUNK, lane0:lane0 + 2 * hd]

    b = blk(_BLK_CUM)
    kb = k.astype(BF16)
    a = jnp.where(eye, _dot_nt(q.astype(BF16), _by_head(kb)), 0.0)
    for lvl in _LEVELS:
        d = b - blk(_LEVEL_INDEX[lvl])
        ql = (q * jnp.exp(jnp.minimum(d, 0.0))).astype(BF16)
        kl = (k * jnp.exp(jnp.minimum(-d, 0.0))).astype(BF16) if lvl > 2 else kb
        a = a + jnp.where(masks[lvl], _dot_nt(ql, _by_head(kl)), 0.0)
    st_both = jnp.concatenate([jnp.concatenate([sts[0], jnp.zeros_like(sts[0])], axis=1),
                               jnp.concatenate([jnp.zeros_like(sts[1]), sts[1]], axis=1)], axis=0).astype(BF16)
    o = _dot(a.astype(BF16), _by_head(v.astype(BF16))) + _dot_nt((q * jnp.exp(b)).astype(BF16), st_both)
    b_last = b[CHUNK - 1:CHUNK, :]
    k_end = (k * jnp.exp(b_last - b)).astype(BF16)
    decay = jnp.exp(b_last)
    new = []
    for h in range(2):
        lanes = slice(h * hd, (h + 1) * hd)
        new.append(sts[h] * decay[:, lanes] + _dot(v[:, lanes].T.astype(BF16), k_end[:, lanes]))
    return o, new


def _mix_prompt_kernel(x_ref, mod_ref, gpre_ref, gpost_ref, gffn_ref, win_ref, wpool_ref, pscale_ref,
                       lbl_ref, gout_ref, wout_ref, wrh_ref, wrl_ref, m2_ref,
                       x1_ref, h2_ref, lg_ref, pool_ref, st_ref,
                       st_s, ubuf, q_s, k_s, v_s, g_s, o_s):
    t = pl.program_id(1)
    n_t = pl.num_programs(1)
    tt = TIME_TILE

    @pl.when(t == 0)
    def _():
        st_s[...] = jnp.zeros_like(st_s)
        ubuf[0:16, :] = jnp.zeros((16, POOL_WIDTH), F32)

    xt = x_ref[0]
    sh1, sc1, gt1, sh2, sc2, gt2 = _mods(mod_ref[0])
    h = _rms(xt, gpre_ref[...]) * (1.0 + sc1) + sh1
    proj = _dot(h.astype(BF16), win_ref[...])

    u = proj[:, :POOL_WIDTH]
    ubuf[16:16 + tt, :] = u
    pos = (t * tt + lax.broadcasted_iota(jnp.int32, (tt, 1), 0) + 1).astype(F32)
    ys = []
    for g, w in enumerate(POOL_WINDOWS):
        s = ubuf[:, g * POOL_GROUP:(g + 1) * POOL_GROUP]
        off = 0
        for step in range(w.bit_length() - 1):
            sh = 1 << step
            s = s[sh:, :] + s[:-sh, :]
            off += sh
        ws = s[16 - off:16 - off + tt, :]
        cnt = jnp.minimum(pos, float(w))
        d = ws / cnt - u[:, g * POOL_GROUP:(g + 1) * POOL_GROUP]
        ys.append(_dot(d.astype(BF16), wpool_ref[g]))
    y_pool = jnp.concatenate(ys, axis=1) * pscale_ref[...]
    ubuf[0:16, :] = ubuf[tt:tt + 16, :]

    @pl.when(t == n_t - 1)
    def _():
        pool_ref[0] = u[tt - POOL_HIST:, :]

    lb = _forget_lower_bound(lbl_ref[...])
    f = lb + (1.0 - lb) * jax.nn.sigmoid(proj[:, POOL_WIDTH + HG_WIDTH:POOL_WIDTH + 2 * HG_WIDTH])
    q_s[...] = _silu(proj[:, POOL_WIDTH:POOL_WIDTH + HG_WIDTH])
    k_s[...] = 1.0 - f
    v_s[...] = proj[:, POOL_WIDTH + 2 * HG_WIDTH:POOL_WIDTH + 3 * HG_WIDTH]
    g_s[...] = jnp.log(f)
    gate = _silu(proj[:, POOL_WIDTH + 3 * HG_WIDTH:])

    masks, eye = _level_masks()
    m2 = m2_ref[...]

    states = [st_s[hd] for hd in range(HG_HEADS)]
    for c in range(tt // CHUNK):
        rows = slice(c * CHUNK, (c + 1) * CHUNK)
        g_hi, g_lo = _split2(g_s[rows, :])
        e_all = _dot(m2, jnp.concatenate([g_hi, g_lo], axis=0))
        for hd in range(0, HG_HEADS, 2):
            lane0 = hd * HG_HEAD_DIM
            lanes = slice(lane0, lane0 + 2 * HG_HEAD_DIM)
            o, states[hd:hd + 2] = _hgrn_chunk_pair(q_s[rows, lanes], k_s[rows, lanes], v_s[rows, lanes], e_all,
                                                    lane0, states[hd:hd + 2], masks, eye)
            o_s[rows, lanes] = o
    for hd in range(HG_HEADS):
        st_s[hd] = states[hd]

    o = o_s[...]
    os_ = []
    for hd in range(HG_HEADS):
        oh = o[:, hd * HG_HEAD_DIM:(hd + 1) * HG_HEAD_DIM]
        os_.append(oh * lax.rsqrt(jnp.mean(oh * oh, axis=-1, keepdims=True) + EPS) * gout_ref[...])
    o_n = jnp.concatenate(os_, axis=1) * gate

    mix = _dot(y_pool.astype(BF16), wout_ref[0:POOL_WIDTH, :]) + _dot(o_n.astype(BF16), wout_ref[POOL_WIDTH:, :])
    x1 = xt + gt1 * _rms(mix, gpost_ref[...])
    h2 = _rms(x1, gffn_ref[...]) * (1.0 + sc2) + sh2
    x1_ref[0] = x1
    h2_ref[0] = _pack_pair(h2[:, :ROW_WORDS], h2[:, ROW_WORDS:])
    lg_ref[0] = _router_logits(h2, wrh_ref[...], wrl_ref[...])

    @pl.when(t == n_t - 1)
    def _():
        for hd in range(HG_HEADS):
            st_ref[0, hd] = st_s[hd].T


def _full(shape):
    nd = len(shape)
    return pl.BlockSpec(shape, lambda *_: (0,) * nd)


def _mix_prompt(x, mod, gpre, gpost, gffn, win, wpool, pscale, lbl, gout, wout, wrh, wrl, m2):
    b, l, d = x.shape
    tt = TIME_TILE
    tile = lambda i, j: (i, j, 0)
    per_b = lambda i, j: (i, 0, 0)
    return pl.pallas_call(
        _mix_prompt_kernel,
        out_shape=(jax.ShapeDtypeStruct((b, l, d), F32),
                   jax.ShapeDtypeStruct((b, l, ROW_WORDS), jnp.uint32),
                   jax.ShapeDtypeStruct((b, l, LANES), F32),
                   jax.ShapeDtypeStruct((b, POOL_HIST, POOL_WIDTH), F32),
                   jax.ShapeDtypeStruct((b, HG_HEADS, HG_HEAD_DIM, HG_HEAD_DIM), F32)),
        grid=(b, l // tt),
        in_specs=[pl.BlockSpec((1, tt, d), tile),
                  pl.BlockSpec((1, 1, 6 * d), per_b),
                  _full((1, d)), _full((1, d)), _full((1, d)),
                  _full(win.shape), _full(wpool.shape), _full((1, POOL_WIDTH)),
                  _full(lbl.shape), _full((1, HG_HEAD_DIM)), _full(wout.shape),
                  _full(wrh.shape), _full(wrl.shape), _full(m2.shape)],
        out_specs=(pl.BlockSpec((1, tt, d), tile),
                   pl.BlockSpec((1, tt, ROW_WORDS), tile),
                   pl.BlockSpec((1, tt, LANES), tile),
                   pl.BlockSpec((1, POOL_HIST, POOL_WIDTH), per_b),
                   pl.BlockSpec((1, HG_HEADS, HG_HEAD_DIM, HG_HEAD_DIM), lambda i, j: (i, 0, 0, 0))),
        scratch_shapes=[pltpu.VMEM((HG_HEADS, HG_HEAD_DIM, HG_HEAD_DIM), F32),
                        pltpu.VMEM((tt + 16, POOL_WIDTH), F32),
                        pltpu.VMEM((tt, HG_WIDTH), F32), pltpu.VMEM((tt, HG_WIDTH), F32),
                        pltpu.VMEM((tt, HG_WIDTH), F32), pltpu.VMEM((tt, HG_WIDTH), F32),
                        pltpu.VMEM((tt, HG_WIDTH), F32)],
        compiler_params=pltpu.CompilerParams(dimension_semantics=("parallel", "arbitrary"),
                                             vmem_limit_bytes=VMEM_LIMIT),
        name="mix_prompt",
    )(x, mod, gpre, gpost, gffn, win, wpool, pscale, lbl, gout, wout, wrh, wrl, m2)


def _mix_sample_in_kernel(x_ref, mod_ref, gpre_ref, win_ref, wpool_ref, pscale_ref, lbl_ref, hist_ref,
                          ypool_ref, npool_ref, ft_ref, qt_ref, v_ref, gate_ref):
    xt = x_ref[...]
    sh1, sc1 = _mods(mod_ref[...])[:2]
    h = _rms(xt, gpre_ref[...]) * (1.0 + sc1) + sh1
    proj = _dot(h.astype(BF16), win_ref[...])
    u = proj[:, :POOL_WIDTH]
    row = lax.broadcasted_iota(jnp.int32, (hist_ref.shape[0], POOL_HIST, POOL_GROUP), 1)
    ys = []
    for g, w in enumerate(POOL_WINDOWS):
        sl = slice(g * POOL_GROUP, (g + 1) * POOL_GROUP)
        past = jnp.sum(jnp.where(row >= POOL_HIST - (w - 1), hist_ref[:, :, sl], 0.0), axis=1)
        ug = u[:, sl]
        d = (past + ug) / float(w) - ug
        ys.append(_dot(d.astype(BF16), wpool_ref[g]))
    ypool_ref[...] = jnp.concatenate(ys, axis=1) * pscale_ref[...]
    npool_ref[:, 0:POOL_HIST - 1, :] = hist_ref[:, 1:POOL_HIST, :]
    npool_ref[:, POOL_HIST - 1, :] = u

    lb = _forget_lower_bound(lbl_ref[...])
    f = lb + (1.0 - lb) * jax.nn.sigmoid(proj[:, POOL_WIDTH + HG_WIDTH:POOL_WIDTH + 2 * HG_WIDTH])
    ft_ref[...] = f.T
    qt_ref[...] = _silu(proj[:, POOL_WIDTH:POOL_WIDTH + HG_WIDTH]).T
    v_ref[...] = proj[:, POOL_WIDTH + 2 * HG_WIDTH:POOL_WIDTH + 3 * HG_WIDTH]
    gate_ref[...] = _silu(proj[:, POOL_WIDTH + 3 * HG_WIDTH:])


def _mix_sample_state_kernel(s_ref, ft_ref, qt_ref, v_ref, snew_ref, o_ref):
    i = pl.program_id(0)
    lane = lax.broadcasted_iota(jnp.int32, (HG_HEAD_DIM, ft_ref.shape[1]), 1)
    for j in range(SAMPLE_STATE_BLOCK):
        mine = lane == i * SAMPLE_STATE_BLOCK + j
        for hd in range(HG_HEADS):
            r0 = hd * HG_HEAD_DIM
            f = jnp.sum(jnp.where(mine, ft_ref[r0:r0 + HG_HEAD_DIM, :], 0.0), axis=1, keepdims=True)
            q = jnp.sum(jnp.where(mine, qt_ref[r0:r0 + HG_HEAD_DIM, :], 0.0), axis=1, keepdims=True)
            v = v_ref[j:j + 1, r0:r0 + HG_HEAD_DIM]
            s_new = f * s_ref[j, hd] + (1.0 - f) * v
            snew_ref[j, hd] = s_new
            o_ref[j:j + 1, r0:r0 + HG_HEAD_DIM] = jnp.sum(q * s_new, axis=0, keepdims=True)


def _mix_sample_out_kernel(x_ref, mod_ref, o_ref, gate_ref, ypool_ref, gout_ref, wout_ref, gpost_ref,
                           gffn_ref, wrh_ref, wrl_ref, x1_ref, h2_ref, lg_ref):
    _, _, gt1, sh2, sc2, _ = _mods(mod_ref[...])
    o = o_ref[...]
    os_ = []
    for hd in range(HG_HEADS):
        oh = o[:, hd * HG_HEAD_DIM:(hd + 1) * HG_HEAD_DIM]
        os_.append(oh * lax.rsqrt(jnp.mean(oh * oh, axis=-1, keepdims=True) + EPS) * gout_ref[...])
    o_n = jnp.concatenate(os_, axis=1) * gate_ref[...]
    mix = (_dot(ypool_ref[...].astype(BF16), wout_ref[0:POOL_WIDTH, :])
           + _dot(o_n.astype(BF16), wout_ref[POOL_WIDTH:, :]))
    x1 = x_ref[...] + gt1 * _rms(mix, gpost_ref[...])
    h2 = _rms(x1, gffn_ref[...]) * (1.0 + sc2) + sh2
    x1_ref[...] = x1
    h2_ref[...] = h2.astype(BF16)
    lg_ref[...] = _router_logits(h2, wrh_ref[...], wrl_ref[...])


def _mix_sample(x, mod, hist, state, gpre, gpost, gffn, win, wpool, pscale, lbl, gout, wout, wrh, wrl):
    b = x.shape[0]
    bb = SAMPLE_STATE_BLOCK
    cp = pltpu.CompilerParams(vmem_limit_bytes=VMEM_LIMIT)
    ypool, npool, ft, qt, v, gate = pl.pallas_call(
        _mix_sample_in_kernel,
        out_shape=(jax.ShapeDtypeStruct((b, POOL_WIDTH), F32),
                   jax.ShapeDtypeStruct((b, POOL_HIST, POOL_WIDTH), F32),
                   jax.ShapeDtypeStruct((HG_WIDTH, b), F32),
                   jax.ShapeDtypeStruct((HG_WIDTH, b), F32),
                   jax.ShapeDtypeStruct((b, HG_WIDTH), F32),
                   jax.ShapeDtypeStruct((b, HG_WIDTH), F32)),
        compiler_params=cp,
        name="mix_sample_in",
    )(x, mod, gpre, win, wpool, pscale, lbl, hist)

    s_spec = pl.BlockSpec((bb, HG_HEADS, HG_HEAD_DIM, HG_HEAD_DIM), lambda i: (i, 0, 0, 0))
    col_spec = _full((HG_WIDTH, b))
    row_spec = pl.BlockSpec((bb, HG_WIDTH), lambda i: (i, 0))
    s_new, o = pl.pallas_call(
        _mix_sample_state_kernel,
        out_shape=(jax.ShapeDtypeStruct(state.shape, F32), jax.ShapeDtypeStruct((b, HG_WIDTH), F32)),
        grid=(b // bb,),
        in_specs=[s_spec, col_spec, col_spec, row_spec],
        out_specs=(s_spec, row_spec),
        compiler_params=pltpu.CompilerParams(dimension_semantics=("parallel",), vmem_limit_bytes=VMEM_LIMIT),
        name="mix_sample_state",
    )(state, ft, qt, v)

    x1, h2, lg = pl.pallas_call(
        _mix_sample_out_kernel,
        out_shape=(jax.ShapeDtypeStruct((b, D_MODEL), F32),
                   jax.ShapeDtypeStruct((b, D_MODEL), BF16),
                   jax.ShapeDtypeStruct((b, LANES), F32)),
        compiler_params=cp,
        name="mix_sample_out",
    )(x, mod, o, gate, ypool, gout, wout, gpost, gffn, wrh, wrl)
    return x1, h2, lg, npool, s_new


def _first_max(cur, idx, big):
    m = jnp.max(cur, axis=0, keepdims=True)
    first = jnp.min(jnp.where(cur == m, idx, big), axis=0, keepdims=True)
    return idx == first


def _route(logit, bias):
    n = logit.shape[1]
    scores = jax.nn.sigmoid(logit)
    sel = scores + bias
    neg = -jnp.inf

    sub = lax.broadcasted_iota(jnp.int32, (GROUP_SIZE, n), 0)
    gscore = []
    for g in range(N_GROUPS):
        sg = sel[g * GROUP_SIZE:(g + 1) * GROUP_SIZE, :]
        m1 = jnp.max(sg, axis=0, keepdims=True)
        rest = jnp.where(_first_max(sg, sub, GROUP_SIZE), neg, sg)
        gscore.append(m1 + jnp.max(rest, axis=0, keepdims=True))
    cur = jnp.concatenate(gscore, axis=0)
    gidx = lax.broadcasted_iota(jnp.int32, (N_GROUPS, n), 0)
    gmask = jnp.zeros((N_GROUPS, n), jnp.bool_)
    for _ in range(TOPK_GROUPS):
        pick = _first_max(cur, gidx, N_GROUPS)
        gmask = gmask | pick
        cur = jnp.where(pick, neg, cur)
    emask = jnp.concatenate(
        [jnp.broadcast_to(gmask[g:g + 1, :], (GROUP_SIZE, n)) for g in range(N_GROUPS)], axis=0)

    cur = jnp.where(emask, sel, neg)
    eidx = lax.broadcasted_iota(jnp.int32, (N_EXPERTS, n), 0)
    picks = []
    for _ in range(TOP_K):
        pick = _first_max(cur, eidx, N_EXPERTS)
        picks.append(pick)
        cur = jnp.where(pick, neg, cur)
    return scores, picks


def _any(masks):
    return functools.reduce(jnp.logical_or, masks)


def _router_kernel(lg_ref, bias_ref, wd_ref):
    n = lg_ref.shape[0]
    scores, picks = _route(lg_ref[...].T[0:N_EXPERTS, :], bias_ref[0:N_EXPERTS, :])
    chosen = _any(picks)
    wsum = jnp.sum(jnp.where(chosen, scores, 0.0), axis=0, keepdims=True)
    wd = jnp.where(chosen, scores / wsum * ROUTED_SCALE, 0.0)
    wd_ref[...] = jnp.concatenate([wd, jnp.zeros((LANES - N_EXPERTS, n), F32)], axis=0).T


def _router(logits, bias_col):
    t = logits.shape[0]
    tile = min(ROUTER_TILE, t)
    return pl.pallas_call(
        _router_kernel,
        out_shape=jax.ShapeDtypeStruct((t, LANES), F32),
        grid=(t // tile,),
        in_specs=[pl.BlockSpec((tile, LANES), lambda i: (i, 0)), _full(bias_col.shape)],
        out_specs=pl.BlockSpec((tile, LANES), lambda i: (i, 0)),
        compiler_params=pltpu.CompilerParams(dimension_semantics=("parallel",), vmem_limit_bytes=VMEM_LIMIT),
        name="router",
    )(logits, bias_col)


def _plan_kernel(lg_ref, bias_ref, upper_ref, dest_ref, wsel_ref, cnt_ref, loff_ref):
    scores, picks = _route(lg_ref[...].T[0:N_EXPERTS, :], bias_ref[0:N_EXPERTS, :])
    chosen = _any(picks)
    chosen_f = jnp.where(chosen, 1.0, 0.0)
    cnt = jnp.sum(chosen_f, axis=1, keepdims=True)
    units = jnp.floor((cnt + (RUN_ALIGN - 1)) / RUN_ALIGN)
    ei = lax.broadcasted_iota(jnp.int32, (N_EXPERTS, N_EXPERTS), 0)
    ej = lax.broadcasted_iota(jnp.int32, (N_EXPERTS, N_EXPERTS), 1)
    before = jnp.where(ej < ei, 1.0, 0.0).astype(BF16)
    loff = RUN_ALIGN * _dot(before, jnp.broadcast_to(units, (N_EXPERTS, LANES)).astype(BF16))[:, 0:1]
    slot = loff + _dot(chosen_f.astype(BF16), upper_ref[...])
    wsum = jnp.sum(jnp.where(chosen, scores, 0.0), axis=0, keepdims=True)
    pick_sum = lambda v: jnp.concatenate(
        [jnp.sum(jnp.where(p, v, 0.0), axis=0, keepdims=True) for p in picks], axis=0)
    dest_ref[0] = (pick_sum(slot) * ROW_SUB).astype(jnp.int32)
    wsel_ref[0] = pick_sum(scores) / wsum * ROUTED_SCALE
    cnt_ref[0] = jnp.broadcast_to(cnt, (N_EXPERTS, LANES))
    loff_ref[0] = jnp.broadcast_to(loff, (N_EXPERTS, LANES))


def _plan(logits, bias_col, upper):
    t = logits.shape[0]
    nw = t // PLAN_WINDOW
    per_w = pl.BlockSpec((1, N_EXPERTS, LANES), lambda w: (w, 0, 0))
    picks = pl.BlockSpec((1, TOP_K, PLAN_WINDOW), lambda w: (w, 0, 0))
    return pl.pallas_call(
        _plan_kernel,
        out_shape=(jax.ShapeDtypeStruct((nw, TOP_K, PLAN_WINDOW), jnp.int32),
                   jax.ShapeDtypeStruct((nw, TOP_K, PLAN_WINDOW), F32),
                   jax.ShapeDtypeStruct((nw, N_EXPERTS, LANES), F32),
                   jax.ShapeDtypeStruct((nw, N_EXPERTS, LANES), F32)),
        grid=(nw,),
        in_specs=[pl.BlockSpec((PLAN_WINDOW, LANES), lambda w: (w, 0)), _full(bias_col.shape), _full(upper.shape)],
        out_specs=(picks, picks, per_w, per_w),
        compiler_params=pltpu.CompilerParams(dimension_semantics=("parallel",), vmem_limit_bytes=VMEM_LIMIT),
        name="plan",
    )(logits, bias_col, upper)


def _slot_rows(slot):
    return pl.ds(pl.multiple_of(slot * ROW_SUB, ROW_SUB), ROW_SUB)


def _table_copies(table_ref, flat_s, sem):
    return [pltpu.make_async_copy(table_ref.at[0, k], flat_s.at[pl.ds(k * PLAN_WINDOW, PLAN_WINDOW)], sem)
            for k in range(TOP_K)]


def _rows_at(offset):
    return pl.ds(pl.multiple_of(offset, ROW_SUB), ROW_SUB)


def _dispatch_kernel(dest_ref, h_ref, x_ref, dest_s, rows_s, sem):
    plan_copies = _table_copies(dest_ref, dest_s, sem)
    for c in plan_copies:
        c.start()
    x_ref[...] = jnp.zeros_like(x_ref)
    for j in range(ROW_SUB):
        rows_s[pl.ds(j, PLAN_WINDOW, stride=ROW_SUB), :] = h_ref[:, j * LANES:(j + 1) * LANES]
    for c in plan_copies:
        c.wait()

    def body(i, carry):
        for u in range(SCATTER_UNROLL):
            t = i * SCATTER_UNROLL + u
            row = rows_s[_slot_rows(t), :]
            for k in range(TOP_K):
                x_ref[_rows_at(dest_s[k * PLAN_WINDOW + t]), :] = row
        return carry

    lax.fori_loop(0, PLAN_WINDOW // SCATTER_UNROLL, body, 0)


def _dispatch(dest_w, h2):
    nw = dest_w.shape[0]
    region = REGION_ROWS * ROW_SUB
    return pl.pallas_call(
        _dispatch_kernel,
        out_shape=jax.ShapeDtypeStruct((nw * region, LANES), jnp.uint32),
        grid=(nw,),
        in_specs=[pl.BlockSpec((1, TOP_K, PLAN_WINDOW), lambda w: (w, 0, 0)),
                  pl.BlockSpec((PLAN_WINDOW, ROW_WORDS), lambda w: (w, 0))],
        out_specs=pl.BlockSpec((region, LANES), lambda w: (w, 0)),
        scratch_shapes=[pltpu.SMEM((TOP_K * PLAN_WINDOW,), jnp.int32),
                        pltpu.VMEM((PLAN_WINDOW * ROW_SUB, LANES), jnp.uint32),
                        pltpu.SemaphoreType.DMA],
        compiler_params=pltpu.CompilerParams(dimension_semantics=("arbitrary",), vmem_limit_bytes=VMEM_LIMIT),
        name="dispatch",
    )(dest_w, h2)


def _unpack_rows(ref, n):
    parts = [_unpack_pair(ref[pl.ds(j, n, stride=ROW_SUB), :]) for j in range(ROW_SUB)]
    return jnp.concatenate([p[0] for p in parts] + [p[1] for p in parts], axis=1)


def _ffn_kernel(gq_ref, gs_ref, to_ref, tn_ref, x_hbm, w1_ref, w3_ref, w2_ref,
                hs_ref, wds_ref, x1s_ref, gt2s_ref, gpost_ref, ws1_ref, ws3_ref, ws2_ref, y_hbm, ys_ref,
                w13_s, w2_s, xbuf, ybuf, rsem, wsem, acc_s):
    del x_hbm
    e = pl.program_id(0)
    n_groups = gq_ref[N_EXPERTS]
    block = RUN_ALIGN * ROW_SUB
    tile = FFN_TILE * ROW_SUB

    def go(cp, start):
        if start:
            cp.start()
        else:
            cp.wait()

    def fetch(q, start):
        def one(j):
            src = y_hbm.at[pl.ds(pl.multiple_of(to_ref[gs_ref[q] + j] * block, block), tile)]
            go(pltpu.make_async_copy(src, xbuf.at[q % FFN_RING, pl.ds(j * tile, tile)], rsem.at[q % FFN_RING]), start)

        size = gs_ref[q + 1] - gs_ref[q]

        @pl.when(size == FFN_GROUP)
        def _():
            for j in range(FFN_GROUP):
                one(j)

        @pl.when(size < FFN_GROUP)
        def _():
            for j in range(FFN_GROUP - 1):
                pl.when(j < size)(functools.partial(one, j))

    def writeback(q, start):
        most = FFN_TILE // RUN_ALIGN - 1

        def blocks(j, b, n):
            dst = pl.multiple_of(to_ref[gs_ref[q] + j] * block, block) + b * block
            go(pltpu.make_async_copy(ybuf.at[q % FFN_RING, pl.ds(j * tile + b * block, n * block)],
                                     y_hbm.at[pl.ds(dst, n * block)], wsem.at[q % FFN_RING]), start)

        for j in range(FFN_GROUP):
            item = gs_ref[q] + j
            n = jnp.where(item < gs_ref[q + 1], tn_ref[item], 0)
            pl.when(n >= most)(functools.partial(blocks, j, 0, most))
            pl.when(n > most)(functools.partial(blocks, j, most, 1))

            @pl.when((n > 0) & (n < most))
            def _():
                for b in range(most - 1):
                    pl.when(b < n)(functools.partial(blocks, j, b, 1))

    @pl.when(e == 0)
    def _():
        xbuf[...] = jnp.zeros_like(xbuf)
        for d in range(FFN_RING - 1):
            pl.when(d < n_groups)(functools.partial(fetch, d, True))

    w13_s[:, 0:D_EXPERT] = w1_ref[0].astype(BF16)
    w13_s[:, D_EXPERT:] = w3_ref[0].astype(BF16)
    w2_s[...] = w2_ref[0].astype(BF16)

    xs = hs_ref[...]

    @pl.when(e == 0)
    def _():
        sh = _silu(_dot(xs, ws1_ref[...].astype(BF16))) * _dot(xs, ws3_ref[...].astype(BF16))
        acc_s[...] = _dot(sh.astype(BF16), ws2_ref[...].astype(BF16))

    lane = lax.broadcasted_iota(jnp.int32, wds_ref.shape, 1)
    wcol = jnp.sum(jnp.where(lane == e, wds_ref[...], 0.0), axis=1, keepdims=True)
    hd = _dot(xs, w13_s[...])
    acc_s[...] += _dot((_silu(hd[:, :D_EXPERT]) * hd[:, D_EXPERT:] * wcol).astype(BF16), w2_s[...])

    @pl.when(e == pl.num_programs(0) - 1)
    def _():
        ys_ref[...] = x1s_ref[...] + gt2s_ref[...] * _rms(acc_s[...], gpost_ref[...])

    def group_body(q, carry):
        @pl.when(q + FFN_RING - 1 < n_groups)
        def _():
            fetch(q + FFN_RING - 1, True)

        @pl.when(q >= FFN_RING)
        def _():
            writeback(q - FFN_RING, False)

        fetch(q, False)
        rows = FFN_GROUP * FFN_TILE // FFN_SPLIT
        for part in range(FFN_SPLIT):
            span = pl.ds(part * rows * ROW_SUB, rows * ROW_SUB)
            x = _unpack_rows(xbuf.at[q % FFN_RING, span], rows).astype(BF16)
            h = _dot(x, w13_s[...])
            act = _silu(h[:, :D_EXPERT]) * h[:, D_EXPERT:]
            y = _dot(act.astype(BF16), w2_s[...])
            out = ybuf.at[q % FFN_RING, span]
            for j in range(ROW_SUB):
                out[pl.ds(j, rows, stride=ROW_SUB), :] = _pack_pair(
                    y[:, j * LANES:(j + 1) * LANES], y[:, ROW_WORDS + j * LANES:ROW_WORDS + (j + 1) * LANES])
        writeback(q, True)
        return carry

    lax.fori_loop(gq_ref[e], gq_ref[e + 1], group_body, 0)

    @pl.when(e == pl.num_programs(0) - 1)
    def _():
        for d in range(FFN_RING):
            @pl.when(n_groups - 1 - d >= 0)
            def _():
                writeback(n_groups - 1 - d, False)


def _expert_ffn(group_of_expert, group_start, item_off, item_blocks, x_slots, w1, w3, w2,
                h2_s, wd_s, x1_s, gt2_s, gpost, ws1, ws3, ws2):
    of_expert = lambda e, *_: (e, 0, 0)
    whole = lambda a: pl.BlockSpec(a.shape, lambda e, *_: (0,) * a.ndim)
    tiles = pltpu.VMEM((FFN_RING, FFN_GROUP * FFN_TILE * ROW_SUB, LANES), jnp.uint32)
    sems = pltpu.SemaphoreType.DMA((FFN_RING,))
    dense = (h2_s, wd_s, x1_s, gt2_s, gpost, ws1, ws3, ws2)
    grid_spec = pltpu.PrefetchScalarGridSpec(
        num_scalar_prefetch=4,
        grid=(N_EXPERTS,),
        in_specs=[pl.BlockSpec(memory_space=pl.ANY),
                  pl.BlockSpec((1, D_MODEL, D_EXPERT), of_expert),
                  pl.BlockSpec((1, D_MODEL, D_EXPERT), of_expert),
                  pl.BlockSpec((1, D_EXPERT, D_MODEL), of_expert)] + [whole(a) for a in dense],
        out_specs=(pl.BlockSpec(memory_space=pl.ANY), whole(x1_s)),
        scratch_shapes=[pltpu.VMEM((D_MODEL, 2 * D_EXPERT), BF16), pltpu.VMEM((D_EXPERT, D_MODEL), BF16),
                        tiles, tiles, sems, sems, pltpu.VMEM(x1_s.shape, F32)])
    return pl.pallas_call(
        _ffn_kernel,
        out_shape=(jax.ShapeDtypeStruct(x_slots.shape, jnp.uint32), jax.ShapeDtypeStruct(x1_s.shape, F32)),
        grid_spec=grid_spec,
        input_output_aliases={4: 0},
        compiler_params=pltpu.CompilerParams(dimension_semantics=("arbitrary",), vmem_limit_bytes=VMEM_LIMIT),
        name="expert_ffn",
    )(group_of_expert, group_start, item_off, item_blocks, x_slots, w1, w3, w2, *dense)


def _work_items(cnt, loff):
    nw = cnt.shape[0]
    cnt = cnt.astype(jnp.int32).T.reshape(-1)
    loff = loff.astype(jnp.int32).T.reshape(-1)
    region = jnp.tile(jnp.arange(nw, dtype=jnp.int32) * REGION_ROWS, N_EXPERTS)
    n_items = nw * N_EXPERTS + (nw * PLAN_WINDOW * TOP_K) // FFN_TILE
    tiles = (cnt + FFN_TILE - 1) // FFN_TILE
    ends = jnp.cumsum(tiles)
    starts = ends - tiles
    i = jnp.arange(n_items + FFN_GROUP, dtype=jnp.int32)[:, None]
    mine = (starts[None, :] <= i) & (i < ends[None, :])
    of_pair = lambda v: jnp.sum(jnp.where(mine, v[None, :], 0), axis=1)
    r = i[:, 0] - of_pair(starts)
    row = of_pair(region + loff) + r * FFN_TILE
    own = jnp.clip(of_pair(cnt) - r * FFN_TILE, 0, FFN_TILE)

    zero = jnp.zeros((1,), jnp.int32)
    item_end = ends.reshape(N_EXPERTS, nw)[:, -1]
    item_start = jnp.concatenate([zero, item_end[:-1]])
    group_end = jnp.cumsum((item_end - item_start + FFN_GROUP - 1) // FFN_GROUP)
    group_first = jnp.concatenate([zero, group_end[:-1]])
    q = jnp.arange(N_EXPERTS + n_items // FFN_GROUP + 1, dtype=jnp.int32)[:, None]
    has = (group_first[None, :] <= q) & (q < group_end[None, :])
    of_expert = lambda v: jnp.sum(jnp.where(has, v[None, :], 0), axis=1)
    group_start = jnp.where(q[:, 0] < group_end[-1],
                            of_expert(item_start) + (q[:, 0] - of_expert(group_first)) * FFN_GROUP, item_end[-1])
    return (jnp.concatenate([zero, group_end]), group_start, row // RUN_ALIGN, (own + RUN_ALIGN - 1) // RUN_ALIGN)


def _combine_kernel(dest_ref, wsel_ref, y_hbm, h_ref, x1_ref, gt2_ref, gpost_ref, ws13_ref, ws2_ref, out_ref,
                    dest_s, wsel_s, lo_s, hi_s, ybuf, sem, ysem):
    w, s = pl.program_id(0), pl.program_id(1)
    region = REGION_ROWS * ROW_SUB
    part = region // REGION_PARTS

    def region_copies(win):
        return [pltpu.make_async_copy(y_hbm.at[pl.ds(win * region + p * part, part)],
                                      ybuf.at[win % 2, pl.ds(p * part, part)], ysem.at[win % 2, p])
                for p in range(REGION_PARTS)]

    @pl.when(s == 0)
    def _():
        copies = _table_copies(dest_ref, dest_s, sem.at[0]) + _table_copies(wsel_ref, wsel_s, sem.at[1])
        for c in copies:
            c.start()

        @pl.when(w == 0)
        def _():
            for c in region_copies(w):
                c.start()

        @pl.when(w + 1 < pl.num_programs(0))
        def _():
            for c in region_copies(w + 1):
                c.start()

        for c in copies + region_copies(w):
            c.wait()

    y_ref = ybuf.at[w % 2]

    base = s * COMBINE_TILE

    def body(i, carry):
        for u in range(COMBINE_UNROLL):
            t = i * COMBINE_UNROLL + u
            acc_lo = jnp.zeros((ROW_SUB, LANES), F32)
            acc_hi = jnp.zeros((ROW_SUB, LANES), F32)
            for k in range(TOP_K):
                at = k * PLAN_WINDOW + base + t
                lo, hi = _unpack_pair(y_ref[_rows_at(dest_s[at]), :])
                w = wsel_s[at]
                acc_lo = acc_lo + w * lo
                acc_hi = acc_hi + w * hi
            lo_s[_slot_rows(t), :] = acc_lo
            hi_s[_slot_rows(t), :] = acc_hi
        return carry

    lax.fori_loop(0, COMBINE_TILE // COMBINE_UNROLL, body, 0)
    rows = lambda ref: [ref[pl.ds(j, COMBINE_TILE, stride=ROW_SUB), :] for j in range(ROW_SUB)]
    routed = jnp.concatenate(rows(lo_s) + rows(hi_s), axis=1)
    h_lo, h_hi = _unpack_pair(h_ref[...])
    x = jnp.concatenate([h_lo, h_hi], axis=1).astype(BF16)
    hs = _dot(x, ws13_ref[...])
    act = _silu(hs[:, :D_EXPERT]) * hs[:, D_EXPERT:]
    ff = routed + _dot(act.astype(BF16), ws2_ref[...])
    out_ref[...] = x1_ref[...] + gt2_ref[0] * _rms(ff, gpost_ref[...])


def _combine(dest_w, wsel_w, y_slots, h2, x1, gt2, tokens_per_gt2, gpost, ws13, ws2):
    nw = dest_w.shape[0]
    sub = PLAN_WINDOW // COMBINE_TILE
    plan = pl.BlockSpec((1, TOP_K, PLAN_WINDOW), lambda w, s: (w, 0, 0))
    tok = lambda w, s: (w * sub + s, 0)
    return pl.pallas_call(
        _combine_kernel,
        out_shape=jax.ShapeDtypeStruct(x1.shape, F32),
        grid=(nw, sub),
        in_specs=[plan, plan,
                  pl.BlockSpec(memory_space=pl.ANY),
                  pl.BlockSpec((COMBINE_TILE, ROW_WORDS), tok),
                  pl.BlockSpec((COMBINE_TILE, D_MODEL), tok),
                  pl.BlockSpec((1, 1, D_MODEL), lambda w, s: ((w * PLAN_WINDOW) // tokens_per_gt2, 0, 0)),
                  _full((1, D_MODEL)), _full(ws13.shape), _full(ws2.shape)],
        out_specs=pl.BlockSpec((COMBINE_TILE, D_MODEL), tok),
        scratch_shapes=[pltpu.SMEM((TOP_K * PLAN_WINDOW,), jnp.int32),
                        pltpu.SMEM((TOP_K * PLAN_WINDOW,), F32),
                        pltpu.VMEM((COMBINE_TILE * ROW_SUB, LANES), F32),
                        pltpu.VMEM((COMBINE_TILE * ROW_SUB, LANES), F32),
                        pltpu.VMEM((2, REGION_ROWS * ROW_SUB, LANES), jnp.uint32),
                        pltpu.SemaphoreType.DMA((2,)), pltpu.SemaphoreType.DMA((2, REGION_PARTS))],
        compiler_params=pltpu.CompilerParams(dimension_semantics=("arbitrary", "arbitrary"),
                                             vmem_limit_bytes=VMEM_LIMIT),
        name="combine",
    )(dest_w, wsel_w, y_slots, h2, x1, gt2, gpost, ws13, ws2)


def kernel(x_prompt, x_sample, c_prompt, c_sample, state_pool, state_hgrn, w_ada, b_ada, g_pre_mix, g_post_mix,
           w_in, w_pool, pool_scale, lb_logits, g_out_norm, w_out, g_pre_ffn, g_post_ffn, w_router, router_bias,
           w_exp_gate, w_exp_up, w_exp_down, w_sh_gate, w_sh_up, w_sh_down):
    assert w_ada.shape[0] == 1 and lb_logits.shape[0] == 2, "single-layer trunk"
    bp, lp, d = x_prompt.shape
    bs = x_sample.shape[0]
    row = lambda a: a[0].reshape(1, -1)

    n_mod = bp + bs
    pad = (-n_mod) % 16
    c_all = jnp.concatenate([c_prompt, c_sample, jnp.zeros((pad, d), F32)], axis=0)
    mod = _ada(c_all, w_ada[0], b_ada)
    mod_p = mod[:bp].reshape(bp, 1, 6 * d)
    mod_s = mod[bp:n_mod]

    win = w_in[0].astype(BF16)
    wout = w_out[0].astype(BF16)
    wpool = w_pool[0].astype(BF16)
    wr = jnp.pad(w_router[0], ((0, 0), (0, LANES - N_EXPERTS)))
    wrh = wr.astype(BF16)
    wrl = (wr - wrh.astype(F32)).astype(BF16)
    m2 = jnp.asarray(_M2_NP, BF16)
    gpre, gpost, gffn, gffn_post = row(g_pre_mix), row(g_post_mix), row(g_pre_ffn), row(g_post_ffn)
    pscale, gout = row(pool_scale), row(g_out_norm)
    bias_col = jnp.pad(router_bias[0], (0, LANES - N_EXPERTS)).reshape(LANES, 1)

    x1_p, h2_p, lg_p, pool_p, st_p = _mix_prompt(x_prompt, mod_p, gpre, gpost, gffn, win, wpool, pscale,
                                                  lb_logits, gout, wout, wrh, wrl, m2)
    x1_s, h2_s, lg_s, pool_s, st_s = _mix_sample(x_sample[:, 0, :], mod_s, state_pool[0], state_hgrn[0],
                                                  gpre, gpost, gffn, win, wpool, pscale, lb_logits, gout, wout,
                                                  wrh, wrl)

    experts = (w_exp_gate[0], w_exp_up[0], w_exp_down[0], w_sh_gate[0], w_sh_up[0], w_sh_down[0])
    tp = bp * lp
    nw = tp // PLAN_WINDOW
    assert lp % PLAN_WINDOW == 0
    upper = jnp.asarray(np.triu(np.ones((PLAN_WINDOW, PLAN_WINDOW), np.float32), 1), BF16)
    dest, wsel, cnt, loff = _plan(lg_p.reshape(tp, LANES), bias_col, upper)
    dest_w, wsel_w = dest, wsel
    h2_rows = h2_p.reshape(tp, ROW_WORDS)
    x_slots = _dispatch(dest_w, h2_rows)
    wd_s = _router(lg_s, bias_col)
    y_slots, y_s = _expert_ffn(*_work_items(cnt[:, :, 0], loff[:, :, 0]), x_slots, *experts[:3],
                               h2_s, wd_s, x1_s, mod_s[:, 5 * d:], gffn_post, *experts[3:])
    ws13 = jnp.concatenate([experts[3], experts[4]], axis=1).astype(BF16)
    gt2_p = mod_p[:, :, 5 * d:]
    y_p = _combine(dest_w, wsel_w, y_slots, h2_rows, x1_p.reshape(tp, d), gt2_p, lp, gffn_post,
                   ws13, experts[5].astype(BF16))

    return (y_p.reshape(bp, lp, d), y_s.reshape(bs, 1, d), pool_p[None], st_p[None], pool_s[None], st_s[None])
```

```python
import functools

import numpy as np
import jax
import jax.numpy as jnp
from jax import lax
from jax.experimental import pallas as pl
from jax.experimental.pallas import tpu as pltpu

F32 = jnp.float32
BF16 = jnp.bfloat16

D_MODEL = 1024
POOL_WIDTH = 512
POOL_WINDOWS = (2, 4, 8, 16)
POOL_GROUP = 128
POOL_HIST = 15
HG_WIDTH = 512
HG_HEADS = 4
HG_HEAD_DIM = 128
IN_WIDTH = POOL_WIDTH + 4 * HG_WIDTH
N_EXPERTS = 64
TOP_K = 8
N_GROUPS = 8
TOPK_GROUPS = 4
GROUP_SIZE = N_EXPERTS // N_GROUPS
D_EXPERT = 256
ROUTED_SCALE = 2.5
EPS = 1e-6

LANES = 128
CHUNK = 64
TIME_TILE = 256
ADA_TILE_N = 1024
ROUTER_TILE = 512
SAMPLE_STATE_BLOCK = 16
PLAN_WINDOW = 1024
RUN_ALIGN = 32
FFN_TILE = 160
REGION_ROWS = PLAN_WINDOW * TOP_K + N_EXPERTS * (RUN_ALIGN - 1) + FFN_TILE
ROW_WORDS = D_MODEL // 2
ROW_SUB = ROW_WORDS // LANES
COMBINE_TILE = 256
REGION_PARTS = 4
SCATTER_UNROLL = 4
COMBINE_UNROLL = 16
FFN_GROUP = 4
FFN_SPLIT = 1
FFN_RING = 6
VMEM_LIMIT = 58 * 1024 * 1024

_LEVELS = (64, 32, 16, 8, 4, 2)
_BLK_CUM = 0


def _level_blocks():
    c = CHUNK
    i = np.arange(c)[:, None]
    s = np.arange(c)[None, :]
    blocks = [(s <= i)]
    index = {}
    for lvl in _LEVELS:
        ref = (i // lvl) * lvl + lvl // 2 - 1
        index[lvl] = len(blocks)
        blocks.append(s <= ref)
    m = np.concatenate(blocks, axis=0).astype(np.float32)
    return np.concatenate([m, m], axis=1), index


_M2_NP, _LEVEL_INDEX = _level_blocks()


def _dot(a, b):
    return jnp.dot(a, b, preferred_element_type=F32)


def _dot_nt(a, b):
    return lax.dot_general(a, b, (((1,), (1,)), ((), ())), preferred_element_type=F32)


def _split2(x):
    hi = x.astype(BF16)
    lo = (x - hi.astype(F32)).astype(BF16)
    return hi, lo


def _pack_pair(a, b):
    lo = lax.bitcast_convert_type(a.astype(BF16).astype(F32), jnp.uint32)
    hi = lax.bitcast_convert_type(b.astype(BF16).astype(F32), jnp.uint32)
    return (lo >> 16) | hi


def _unpack_pair(words):
    lo = lax.bitcast_convert_type(words << 16, F32)
    hi = lax.bitcast_convert_type(words & jnp.uint32(0xFFFF0000), F32)
    return lo, hi


def _silu(x):
    return x * jax.nn.sigmoid(x)


def _rms(x, g):
    return x * lax.rsqrt(jnp.mean(x * x, axis=-1, keepdims=True) + EPS) * g


def _mods(m):
    return [m[:, j * D_MODEL:(j + 1) * D_MODEL] for j in range(6)]


def _forget_lower_bound(lbl):
    mx = jnp.max(lbl, axis=0, keepdims=True)
    e = jnp.exp(lbl - mx)
    return e[0:1] / jnp.sum(e, axis=0, keepdims=True)


def _router_logits(h2, wrh, wrl):
    hi, lo = _split2(h2)
    return _dot(hi, wrh) + _dot(lo, wrh) + _dot(hi, wrl)


def _ada_kernel(c_ref, w_ref, b_ref, o_ref):
    a_hi, a_lo = _split2(_silu(c_ref[...]))
    w_hi, w_lo = _split2(w_ref[...])
    o_ref[...] = _dot(a_hi, w_hi) + _dot(a_lo, w_hi) + _dot(a_hi, w_lo) + b_ref[...]


def _ada(c_all, w_ada, b_ada):
    rows = c_all.shape[0]
    n = w_ada.shape[1]
    return pl.pallas_call(
        _ada_kernel,
        out_shape=jax.ShapeDtypeStruct((rows, n), F32),
        grid=(n // ADA_TILE_N,),
        in_specs=[pl.BlockSpec((rows, D_MODEL), lambda j: (0, 0)),
                  pl.BlockSpec((D_MODEL, ADA_TILE_N), lambda j: (0, j)),
                  pl.BlockSpec((1, ADA_TILE_N), lambda j: (0, j))],
        out_specs=pl.BlockSpec((rows, ADA_TILE_N), lambda j: (0, j)),
        compiler_params=pltpu.CompilerParams(dimension_semantics=("parallel",),
                                             vmem_limit_bytes=VMEM_LIMIT),
        name="ada",
    )(c_all, w_ada, b_ada)


def _level_masks():
    i = lax.broadcasted_iota(jnp.int32, (CHUNK, 2 * CHUNK), 0)
    j = lax.broadcasted_iota(jnp.int32, (CHUNK, 2 * CHUNK), 1) & (CHUNK - 1)
    masks = {}
    for lvl in _LEVELS:
        sh = lvl.bit_length() - 1
        same = (i >> sh) == (j >> sh)
        upper = ((i >> (sh - 1)) & 1) == 1
        lower = ((j >> (sh - 1)) & 1) == 0
        masks[lvl] = same & upper & lower
    return masks, i == j


def _by_head(m):
    z = jnp.zeros((m.shape[0], HG_HEAD_DIM), m.dtype)
    return jnp.concatenate([jnp.concatenate([m[:, :HG_HEAD_DIM], z], axis=1),
                            jnp.concatenate([z, m[:, HG_HEAD_DIM:]], axis=1)], axis=0)


def _hgrn_chunk_pair(q, k, v, e_all, lane0, sts, masks, eye):
    hd = HG_HEAD_DIM

    def blk(n):
        return e_all[n * CHUNK:(n + 1) * CHUNK, lane0:lane0 + 2 * hd]

    b = blk(_BLK_CUM)
    kb = k.astype(BF16)
    a = jnp.where(eye, _dot_nt(q.astype(BF16), _by_head(kb)), 0.0)
    for lvl in _LEVELS:
        d = b - blk(_LEVEL_INDEX[lvl])
        ql = (q * jnp.exp(jnp.minimum(d, 0.0))).astype(BF16)
        kl = (k * jnp.exp(jnp.minimum(-d, 0.0))).astype(BF16) if lvl > 2 else kb
        a = a + jnp.where(masks[lvl], _dot_nt(ql, _by_head(kl)), 0.0)
    st_both = jnp.concatenate([jnp.concatenate([sts[0], jnp.zeros_like(sts[0])], axis=1),
                               jnp.concatenate([jnp.zeros_like(sts[1]), sts[1]], axis=1)], axis=0).astype(BF16)
    o = _dot(a.astype(BF16), _by_head(v.astype(BF16))) + _dot_nt((q * jnp.exp(b)).astype(BF16), st_both)
    b_last = b[CHUNK - 1:CHUNK, :]
    k_end = (k * jnp.exp(b_last - b)).astype(BF16)
    decay = jnp.exp(b_last)
    new = []
    for h in range(2):
        lanes = slice(h * hd, (h + 1) * hd)
        new.append(sts[h] * decay[:, lanes] + _dot(v[:, lanes].T.astype(BF16), k_end[:, lanes]))
    return o, new


def _mix_prompt_kernel(x_ref, mod_ref, gpre_ref, gpost_ref, gffn_ref, win_ref, wpool_ref, pscale_ref,
                       lbl_ref, gout_ref, wout_ref, wrh_ref, wrl_ref, m2_ref,
                       x1_ref, h2_ref, lg_ref, pool_ref, st_ref,
                       st_s, ubuf, q_s, k_s, v_s, g_s, o_s):
    t = pl.program_id(1)
    n_t = pl.num_programs(1)
    tt = TIME_TILE

    @pl.when(t == 0)
    def _():
        st_s[...] = jnp.zeros_like(st_s)
        ubuf[0:16, :] = jnp.zeros((16, POOL_WIDTH), F32)

    xt = x_ref[0]
    sh1, sc1, gt1, sh2, sc2, gt2 = _mods(mod_ref[0])
    h = _rms(xt, gpre_ref[...]) * (1.0 + sc1) + sh1
    proj = _dot(h.astype(BF16), win_ref[...])

    u = proj[:, :POOL_WIDTH]
    ubuf[16:16 + tt, :] = u
    pos = (t * tt + lax.broadcasted_iota(jnp.int32, (tt, 1), 0) + 1).astype(F32)
    ys = []
    for g, w in enumerate(POOL_WINDOWS):
        s = ubuf[:, g * POOL_GROUP:(g + 1) * POOL_GROUP]
        off = 0
        for step in range(w.bit_length() - 1):
            sh = 1 << step
            s = s[sh:, :] + s[:-sh, :]
            off += sh
        ws = s[16 - off:16 - off + tt, :]
        cnt = jnp.minimum(pos, float(w))
        d = ws / cnt - u[:, g * POOL_GROUP:(g + 1) * POOL_GROUP]
        ys.append(_dot(d.astype(BF16), wpool_ref[g]))
    y_pool = jnp.concatenate(ys, axis=1) * pscale_ref[...]
    ubuf[0:16, :] = ubuf[tt:tt + 16, :]

    @pl.when(t == n_t - 1)
    def _():
        pool_ref[0] = u[tt - POOL_HIST:, :]

    lb = _forget_lower_bound(lbl_ref[...])
    f = lb + (1.0 - lb) * jax.nn.sigmoid(proj[:, POOL_WIDTH + HG_WIDTH:POOL_WIDTH + 2 * HG_WIDTH])
    q_s[...] = _silu(proj[:, POOL_WIDTH:POOL_WIDTH + HG_WIDTH])
    k_s[...] = 1.0 - f
    v_s[...] = proj[:, POOL_WIDTH + 2 * HG_WIDTH:POOL_WIDTH + 3 * HG_WIDTH]
    g_s[...] = jnp.log(f)
    gate = _silu(proj[:, POOL_WIDTH + 3 * HG_WIDTH:])

    masks, eye = _level_masks()
    m2 = m2_ref[...]

    states = [st_s[hd] for hd in range(HG_HEADS)]
    for c in range(tt // CHUNK):
        rows = slice(c * CHUNK, (c + 1) * CHUNK)
        g_hi, g_lo = _split2(g_s[rows, :])
        e_all = _dot(m2, jnp.concatenate([g_hi, g_lo], axis=0))
        for hd in range(0, HG_HEADS, 2):
            lane0 = hd * HG_HEAD_DIM
            lanes = slice(lane0, lane0 + 2 * HG_HEAD_DIM)
            o, states[hd:hd + 2] = _hgrn_chunk_pair(q_s[rows, lanes], k_s[rows, lanes], v_s[rows, lanes], e_all,
                                                    lane0, states[hd:hd + 2], masks, eye)
            o_s[rows, lanes] = o
    for hd in range(HG_HEADS):
        st_s[hd] = states[hd]

    o = o_s[...]
    os_ = []
    for hd in range(HG_HEADS):
        oh = o[:, hd * HG_HEAD_DIM:(hd + 1) * HG_HEAD_DIM]
        os_.append(oh * lax.rsqrt(jnp.mean(oh * oh, axis=-1, keepdims=True) + EPS) * gout_ref[...])
    o_n = jnp.concatenate(os_, axis=1) * gate

    mix = _dot(y_pool.astype(BF16), wout_ref[0:POOL_WIDTH, :]) + _dot(o_n.astype(BF16), wout_ref[POOL_WIDTH:, :])
    x1 = xt + gt1 * _rms(mix, gpost_ref[...])
    h2 = _rms(x1, gffn_ref[...]) * (1.0 + sc2) + sh2
    x1_ref[0] = x1
    h2_ref[0] = _pack_pair(h2[:, :ROW_WORDS], h2[:, ROW_WORDS:])
    lg_ref[0] = _router_logits(h2, wrh_ref[...], wrl_ref[...])

    @pl.when(t == n_t - 1)
    def _():
        for hd in range(HG_HEADS):
            st_ref[0, hd] = st_s[hd].T


def _full(shape):
    nd = len(shape)
    return pl.BlockSpec(shape, lambda *_: (0,) * nd)


def _mix_prompt(x, mod, gpre, gpost, gffn, win, wpool, pscale, lbl, gout, wout, wrh, wrl, m2):
    b, l, d = x.shape
    tt = TIME_TILE
    tile = lambda i, j: (i, j, 0)
    per_b = lambda i, j: (i, 0, 0)
    return pl.pallas_call(
        _mix_prompt_kernel,
        out_shape=(jax.ShapeDtypeStruct((b, l, d), F32),
                   jax.ShapeDtypeStruct((b, l, ROW_WORDS), jnp.uint32),
                   jax.ShapeDtypeStruct((b, l, LANES), F32),
                   jax.ShapeDtypeStruct((b, POOL_HIST, POOL_WIDTH), F32),
                   jax.ShapeDtypeStruct((b, HG_HEADS, HG_HEAD_DIM, HG_HEAD_DIM), F32)),
        grid=(b, l // tt),
        in_specs=[pl.BlockSpec((1, tt, d), tile),
                  pl.BlockSpec((1, 1, 6 * d), per_b),
                  _full((1, d)), _full((1, d)), _full((1, d)),
                  _full(win.shape), _full(wpool.shape), _full((1, POOL_WIDTH)),
                  _full(lbl.shape), _full((1, HG_HEAD_DIM)), _full(wout.shape),
                  _full(wrh.shape), _full(wrl.shape), _full(m2.shape)],
        out_specs=(pl.BlockSpec((1, tt, d), tile),
                   pl.BlockSpec((1, tt, ROW_WORDS), tile),
                   pl.BlockSpec((1, tt, LANES), tile),
                   pl.BlockSpec((1, POOL_HIST, POOL_WIDTH), per_b),
                   pl.BlockSpec((1, HG_HEADS, HG_HEAD_DIM, HG_HEAD_DIM), lambda i, j: (i, 0, 0, 0))),
        scratch_shapes=[pltpu.VMEM((HG_HEADS, HG_HEAD_DIM, HG_HEAD_DIM), F32),
                        pltpu.VMEM((tt + 16, POOL_WIDTH), F32),
                        pltpu.VMEM((tt, HG_WIDTH), F32), pltpu.VMEM((tt, HG_WIDTH), F32),
                        pltpu.VMEM((tt, HG_WIDTH), F32), pltpu.VMEM((tt, HG_WIDTH), F32),
                        pltpu.VMEM((tt, HG_WIDTH), F32)],
        compiler_params=pltpu.CompilerParams(dimension_semantics=("parallel", "arbitrary"),
                                             vmem_limit_bytes=VMEM_LIMIT),
        name="mix_prompt",
    )(x, mod, gpre, gpost, gffn, win, wpool, pscale, lbl, gout, wout, wrh, wrl, m2)


def _mix_sample_in_kernel(x_ref, mod_ref, gpre_ref, win_ref, wpool_ref, pscale_ref, lbl_ref, hist_ref,
                          ypool_ref, npool_ref, ft_ref, qt_ref, v_ref, gate_ref):
    xt = x_ref[...]
    sh1, sc1 = _mods(mod_ref[...])[:2]
    h = _rms(xt, gpre_ref[...]) * (1.0 + sc1) + sh1
    proj = _dot(h.astype(BF16), win_ref[...])
    u = proj[:, :POOL_WIDTH]
    row = lax.broadcasted_iota(jnp.int32, (hist_ref.shape[0], POOL_HIST, POOL_GROUP), 1)
    ys = []
    for g, w in enumerate(POOL_WINDOWS):
        sl = slice(g * POOL_GROUP, (g + 1) * POOL_GROUP)
        past = jnp.sum(jnp.where(row >= POOL_HIST - (w - 1), hist_ref[:, :, sl], 0.0), axis=1)
        ug = u[:, sl]
        d = (past + ug) / float(w) - ug
        ys.append(_dot(d.astype(BF16), wpool_ref[g]))
    ypool_ref[...] = jnp.concatenate(ys, axis=1) * pscale_ref[...]
    npool_ref[:, 0:POOL_HIST - 1, :] = hist_ref[:, 1:POOL_HIST, :]
    npool_ref[:, POOL_HIST - 1, :] = u

    lb = _forget_lower_bound(lbl_ref[...])
    f = lb + (1.0 - lb) * jax.nn.sigmoid(proj[:, POOL_WIDTH + HG_WIDTH:POOL_WIDTH + 2 * HG_WIDTH])
    ft_ref[...] = f.T
    qt_ref[...] = _silu(proj[:, POOL_WIDTH:POOL_WIDTH + HG_WIDTH]).T
    v_ref[...] = proj[:, POOL_WIDTH + 2 * HG_WIDTH:POOL_WIDTH + 3 * HG_WIDTH]
    gate_ref[...] = _silu(proj[:, POOL_WIDTH + 3 * HG_WIDTH:])


def _mix_sample_state_kernel(s_ref, ft_ref, qt_ref, v_ref, snew_ref, o_ref):
    i = pl.program_id(0)
    lane = lax.broadcasted_iota(jnp.int32, (HG_HEAD_DIM, ft_ref.shape[1]), 1)
    for j in range(SAMPLE_STATE_BLOCK):
        mine = lane == i * SAMPLE_STATE_BLOCK + j
        for hd in range(HG_HEADS):
            r0 = hd * HG_HEAD_DIM
            f = jnp.sum(jnp.where(mine, ft_ref[r0:r0 + HG_HEAD_DIM, :], 0.0), axis=1, keepdims=True)
            q = jnp.sum(jnp.where(mine, qt_ref[r0:r0 + HG_HEAD_DIM, :], 0.0), axis=1, keepdims=True)
            v = v_ref[j:j + 1, r0:r0 + HG_HEAD_DIM]
            s_new = f * s_ref[j, hd] + (1.0 - f) * v
            snew_ref[j, hd] = s_new
            o_ref[j:j + 1, r0:r0 + HG_HEAD_DIM] = jnp.sum(q * s_new, axis=0, keepdims=True)


def _mix_sample_out_kernel(x_ref, mod_ref, o_ref, gate_ref, ypool_ref, gout_ref, wout_ref, gpost_ref,
                           gffn_ref, wrh_ref, wrl_ref, x1_ref, h2_ref, lg_ref):
    _, _, gt1, sh2, sc2, _ = _mods(mod_ref[...])
    o = o_ref[...]
    os_ = []
    for hd in range(HG_HEADS):
        oh = o[:, hd * HG_HEAD_DIM:(hd + 1) * HG_HEAD_DIM]
        os_.append(oh * lax.rsqrt(jnp.mean(oh * oh, axis=-1, keepdims=True) + EPS) * gout_ref[...])
    o_n = jnp.concatenate(os_, axis=1) * gate_ref[...]
    mix = (_dot(ypool_ref[...].astype(BF16), wout_ref[0:POOL_WIDTH, :])
           + _dot(o_n.astype(BF16), wout_ref[POOL_WIDTH:, :]))
    x1 = x_ref[...] + gt1 * _rms(mix, gpost_ref[...])
    h2 = _rms(x1, gffn_ref[...]) * (1.0 + sc2) + sh2
    x1_ref[...] = x1
    h2_ref[...] = h2.astype(BF16)
    lg_ref[...] = _router_logits(h2, wrh_ref[...], wrl_ref[...])


def _mix_sample(x, mod, hist, state, gpre, gpost, gffn, win, wpool, pscale, lbl, gout, wout, wrh, wrl):
    b = x.shape[0]
    bb = SAMPLE_STATE_BLOCK
    cp = pltpu.CompilerParams(vmem_limit_bytes=VMEM_LIMIT)
    ypool, npool, ft, qt, v, gate = pl.pallas_call(
        _mix_sample_in_kernel,
        out_shape=(jax.ShapeDtypeStruct((b, POOL_WIDTH), F32),
                   jax.ShapeDtypeStruct((b, POOL_HIST, POOL_WIDTH), F32),
                   jax.ShapeDtypeStruct((HG_WIDTH, b), F32),
                   jax.ShapeDtypeStruct((HG_WIDTH, b), F32),
                   jax.ShapeDtypeStruct((b, HG_WIDTH), F32),
                   jax.ShapeDtypeStruct((b, HG_WIDTH), F32)),
        compiler_params=cp,
        name="mix_sample_in",
    )(x, mod, gpre, win, wpool, pscale, lbl, hist)

    s_spec = pl.BlockSpec((bb, HG_HEADS, HG_HEAD_DIM, HG_HEAD_DIM), lambda i: (i, 0, 0, 0))
    col_spec = _full((HG_WIDTH, b))
    row_spec = pl.BlockSpec((bb, HG_WIDTH), lambda i: (i, 0))
    s_new, o = pl.pallas_call(
        _mix_sample_state_kernel,
        out_shape=(jax.ShapeDtypeStruct(state.shape, F32), jax.ShapeDtypeStruct((b, HG_WIDTH), F32)),
        grid=(b // bb,),
        in_specs=[s_spec, col_spec, col_spec, row_spec],
        out_specs=(s_spec, row_spec),
        compiler_params=pltpu.CompilerParams(dimension_semantics=("parallel",), vmem_limit_bytes=VMEM_LIMIT),
        name="mix_sample_state",
    )(state, ft, qt, v)

    x1, h2, lg = pl.pallas_call(
        _mix_sample_out_kernel,
        out_shape=(jax.ShapeDtypeStruct((b, D_MODEL), F32),
                   jax.ShapeDtypeStruct((b, D_MODEL), BF16),
                   jax.ShapeDtypeStruct((b, LANES), F32)),
        compiler_params=cp,
        name="mix_sample_out",
    )(x, mod, o, gate, ypool, gout, wout, gpost, gffn, wrh, wrl)
    return x1, h2, lg, npool, s_new


def _first_max(cur, idx, big):
    m = jnp.max(cur, axis=0, keepdims=True)
    first = jnp.min(jnp.where(cur == m, idx, big), axis=0, keepdims=True)
    return idx == first


def _route(logit, bias):
    n = logit.shape[1]
    scores = jax.nn.sigmoid(logit)
    sel = scores + bias
    neg = -jnp.inf

    sub = lax.broadcasted_iota(jnp.int32, (GROUP_SIZE, n), 0)
    gscore = []
    for g in range(N_GROUPS):
        sg = sel[g * GROUP_SIZE:(g + 1) * GROUP_SIZE, :]
        m1 = jnp.max(sg, axis=0, keepdims=True)
        rest = jnp.where(_first_max(sg, sub, GROUP_SIZE), neg, sg)
        gscore.append(m1 + jnp.max(rest, axis=0, keepdims=True))
    cur = jnp.concatenate(gscore, axis=0)
    gidx = lax.broadcasted_iota(jnp.int32, (N_GROUPS, n), 0)
    gmask = jnp.zeros((N_GROUPS, n), jnp.bool_)
    for _ in range(TOPK_GROUPS):
        pick = _first_max(cur, gidx, N_GROUPS)
        gmask = gmask | pick
        cur = jnp.where(pick, neg, cur)
    emask = jnp.concatenate(
        [jnp.broadcast_to(gmask[g:g + 1, :], (GROUP_SIZE, n)) for g in range(N_GROUPS)], axis=0)

    cur = jnp.where(emask, sel, neg)
    eidx = lax.broadcasted_iota(jnp.int32, (N_EXPERTS, n), 0)
    picks = []
    for _ in range(TOP_K):
        pick = _first_max(cur, eidx, N_EXPERTS)
        picks.append(pick)
        cur = jnp.where(pick, neg, cur)
    return scores, picks


def _any(masks):
    return functools.reduce(jnp.logical_or, masks)


def _router_kernel(lg_ref, bias_ref, wd_ref):
    n = lg_ref.shape[0]
    scores, picks = _route(lg_ref[...].T[0:N_EXPERTS, :], bias_ref[0:N_EXPERTS, :])
    chosen = _any(picks)
    wsum = jnp.sum(jnp.where(chosen, scores, 0.0), axis=0, keepdims=True)
    wd = jnp.where(chosen, scores / wsum * ROUTED_SCALE, 0.0)
    wd_ref[...] = jnp.concatenate([wd, jnp.zeros((LANES - N_EXPERTS, n), F32)], axis=0).T


def _router(logits, bias_col):
    t = logits.shape[0]
    tile = min(ROUTER_TILE, t)
    return pl.pallas_call(
        _router_kernel,
        out_shape=jax.ShapeDtypeStruct((t, LANES), F32),
        grid=(t // tile,),
        in_specs=[pl.BlockSpec((tile, LANES), lambda i: (i, 0)), _full(bias_col.shape)],
        out_specs=pl.BlockSpec((tile, LANES), lambda i: (i, 0)),
        compiler_params=pltpu.CompilerParams(dimension_semantics=("parallel",), vmem_limit_bytes=VMEM_LIMIT),
        name="router",
    )(logits, bias_col)


def _plan_kernel(lg_ref, bias_ref, upper_ref, dest_ref, wsel_ref, cnt_ref, loff_ref):
    scores, picks = _route(lg_ref[...].T[0:N_EXPERTS, :], bias_ref[0:N_EXPERTS, :])
    chosen = _any(picks)
    chosen_f = jnp.where(chosen, 1.0, 0.0)
    cnt = jnp.sum(chosen_f, axis=1, keepdims=True)
    units = jnp.floor((cnt + (RUN_ALIGN - 1)) / RUN_ALIGN)
    ei = lax.broadcasted_iota(jnp.int32, (N_EXPERTS, N_EXPERTS), 0)
    ej = lax.broadcasted_iota(jnp.int32, (N_EXPERTS, N_EXPERTS), 1)
    before = jnp.where(ej < ei, 1.0, 0.0).astype(BF16)
    loff = RUN_ALIGN * _dot(before, jnp.broadcast_to(units, (N_EXPERTS, LANES)).astype(BF16))[:, 0:1]
    slot = loff + _dot(chosen_f.astype(BF16), upper_ref[...])
    wsum = jnp.sum(jnp.where(chosen, scores, 0.0), axis=0, keepdims=True)
    pick_sum = lambda v: jnp.concatenate(
        [jnp.sum(jnp.where(p, v, 0.0), axis=0, keepdims=True) for p in picks], axis=0)
    dest_ref[0] = (pick_sum(slot) * ROW_SUB).astype(jnp.int32)
    wsel_ref[0] = pick_sum(scores) / wsum * ROUTED_SCALE
    cnt_ref[0] = jnp.broadcast_to(cnt, (N_EXPERTS, LANES))
    loff_ref[0] = jnp.broadcast_to(loff, (N_EXPERTS, LANES))


def _plan(logits, bias_col, upper):
    t = logits.shape[0]
    nw = t // PLAN_WINDOW
    per_w = pl.BlockSpec((1, N_EXPERTS, LANES), lambda w: (w, 0, 0))
    picks = pl.BlockSpec((1, TOP_K, PLAN_WINDOW), lambda w: (w, 0, 0))
    return pl.pallas_call(
        _plan_kernel,
        out_shape=(jax.ShapeDtypeStruct((nw, TOP_K, PLAN_WINDOW), jnp.int32),
                   jax.ShapeDtypeStruct((nw, TOP_K, PLAN_WINDOW), F32),
                   jax.ShapeDtypeStruct((nw, N_EXPERTS, LANES), F32),
                   jax.ShapeDtypeStruct((nw, N_EXPERTS, LANES), F32)),
        grid=(nw,),
        in_specs=[pl.BlockSpec((PLAN_WINDOW, LANES), lambda w: (w, 0)), _full(bias_col.shape), _full(upper.shape)],
        out_specs=(picks, picks, per_w, per_w),
        compiler_params=pltpu.CompilerParams(dimension_semantics=("parallel",), vmem_limit_bytes=VMEM_LIMIT),
        name="plan",
    )(logits, bias_col, upper)


def _slot_rows(slot):
    return pl.ds(pl.multiple_of(slot * ROW_SUB, ROW_SUB), ROW_SUB)


def _table_copies(table_ref, flat_s, sem):
    return [pltpu.make_async_copy(table_ref.at[0, k], flat_s.at[pl.ds(k * PLAN_WINDOW, PLAN_WINDOW)], sem)
            for k in range(TOP_K)]


def _rows_at(offset):
    return pl.ds(pl.multiple_of(offset, ROW_SUB), ROW_SUB)


def _dispatch_kernel(dest_ref, h_ref, x_hbm, dest_s, rows_s, xbuf, sem, wsem):
    w = pl.program_id(0)
    nw = pl.num_programs(0)
    region = REGION_ROWS * ROW_SUB
    part = region // REGION_PARTS

    def region_copies(win):
        return [pltpu.make_async_copy(xbuf.at[win % 2, pl.ds(p * part, part)],
                                      x_hbm.at[pl.ds(win * region + p * part, part)], wsem.at[win % 2, p])
                for p in range(REGION_PARTS)]

    plan_copies = _table_copies(dest_ref, dest_s, sem)
    for c in plan_copies:
        c.start()

    @pl.when(w >= 2)
    def _():
        for c in region_copies(w - 2):
            c.wait()

    x_ref = xbuf.at[w % 2]
    x_ref[...] = jnp.zeros((region, LANES), jnp.uint32)
    for j in range(ROW_SUB):
        rows_s[pl.ds(j, PLAN_WINDOW, stride=ROW_SUB), :] = h_ref[:, j * LANES:(j + 1) * LANES]
    for c in plan_copies:
        c.wait()

    def body(i, carry):
        for u in range(SCATTER_UNROLL):
            t = i * SCATTER_UNROLL + u
            row = rows_s[_slot_rows(t), :]
            for k in range(TOP_K):
                x_ref[_rows_at(dest_s[k * PLAN_WINDOW + t]), :] = row
        return carry

    lax.fori_loop(0, PLAN_WINDOW // SCATTER_UNROLL, body, 0)
    for c in region_copies(w):
        c.start()

    @pl.when(w == nw - 1)
    def _():
        @pl.when(w >= 1)
        def _():
            for c in region_copies(w - 1):
                c.wait()
        for c in region_copies(w):
            c.wait()


def _dispatch(dest_w, h2):
    nw = dest_w.shape[0]
    region = REGION_ROWS * ROW_SUB
    return pl.pallas_call(
        _dispatch_kernel,
        out_shape=jax.ShapeDtypeStruct((nw * region, LANES), jnp.uint32),
        grid=(nw,),
        in_specs=[pl.BlockSpec((1, TOP_K, PLAN_WINDOW), lambda w: (w, 0, 0)),
                  pl.BlockSpec((PLAN_WINDOW, ROW_WORDS), lambda w: (w, 0))],
        out_specs=pl.BlockSpec(memory_space=pl.ANY),
        scratch_shapes=[pltpu.SMEM((TOP_K * PLAN_WINDOW,), jnp.int32),
                        pltpu.VMEM((PLAN_WINDOW * ROW_SUB, LANES), jnp.uint32),
                        pltpu.VMEM((2, region, LANES), jnp.uint32),
                        pltpu.SemaphoreType.DMA, pltpu.SemaphoreType.DMA((2, REGION_PARTS))],
        compiler_params=pltpu.CompilerParams(dimension_semantics=("arbitrary",), vmem_limit_bytes=VMEM_LIMIT),
        name="dispatch",
    )(dest_w, h2)


def _unpack_rows(ref, n):
    parts = [_unpack_pair(ref[pl.ds(j, n, stride=ROW_SUB), :]) for j in range(ROW_SUB)]
    return jnp.concatenate([p[0] for p in parts] + [p[1] for p in parts], axis=1)


def _ffn_kernel(gq_ref, gs_ref, to_ref, tn_ref, x_hbm, w1_ref, w3_ref, w2_ref,
                hs_ref, wds_ref, x1s_ref, gt2s_ref, gpost_ref, ws1_ref, ws3_ref, ws2_ref, y_hbm, ys_ref,
                w13_s, w2_s, xbuf, ybuf, rsem, wsem, acc_s):
    del x_hbm
    e = pl.program_id(0)
    n_groups = gq_ref[N_EXPERTS]
    block = RUN_ALIGN * ROW_SUB
    tile = FFN_TILE * ROW_SUB

    def go(cp, start):
        if start:
            cp.start()
        else:
            cp.wait()

    def fetch(q, start):
        def one(j):
            src = y_hbm.at[pl.ds(pl.multiple_of(to_ref[gs_ref[q] + j] * block, block), tile)]
            go(pltpu.make_async_copy(src, xbuf.at[q % FFN_RING, pl.ds(j * tile, tile)], rsem.at[q % FFN_RING]), start)

        size = gs_ref[q + 1] - gs_ref[q]

        @pl.when(size == FFN_GROUP)
        def _():
            for j in range(FFN_GROUP):
                one(j)

        @pl.when(size < FFN_GROUP)
        def _():
            for j in range(FFN_GROUP - 1):
                pl.when(j < size)(functools.partial(one, j))

    def writeback(q, start):
        most = FFN_TILE // RUN_ALIGN - 1

        def blocks(j, b, n):
            dst = pl.multiple_of(to_ref[gs_ref[q] + j] * block, block) + b * block
            go(pltpu.make_async_copy(ybuf.at[q % FFN_RING, pl.ds(j * tile + b * block, n * block)],
                                     y_hbm.at[pl.ds(dst, n * block)], wsem.at[q % FFN_RING]), start)

        for j in range(FFN_GROUP):
            item = gs_ref[q] + j
            n = jnp.where(item < gs_ref[q + 1], tn_ref[item], 0)
            pl.when(n >= most)(functools.partial(blocks, j, 0, most))
            pl.when(n > most)(functools.partial(blocks, j, most, 1))

            @pl.when((n > 0) & (n < most))
            def _():
                for b in range(most - 1):
                    pl.when(b < n)(functools.partial(blocks, j, b, 1))

    @pl.when(e == 0)
    def _():
        xbuf[...] = jnp.zeros_like(xbuf)
        for d in range(FFN_RING - 1):
            pl.when(d < n_groups)(functools.partial(fetch, d, True))

    w13_s[:, 0:D_EXPERT] = w1_ref[0].astype(BF16)
    w13_s[:, D_EXPERT:] = w3_ref[0].astype(BF16)
    w2_s[...] = w2_ref[0].astype(BF16)

    xs = hs_ref[...]

    @pl.when(e == 0)
    def _():
        sh = _silu(_dot(xs, ws1_ref[...].astype(BF16))) * _dot(xs, ws3_ref[...].astype(BF16))
        acc_s[...] = _dot(sh.astype(BF16), ws2_ref[...].astype(BF16))

    lane = lax.broadcasted_iota(jnp.int32, wds_ref.shape, 1)
    wcol = jnp.sum(jnp.where(lane == e, wds_ref[...], 0.0), axis=1, keepdims=True)
    hd = _dot(xs, w13_s[...])
    acc_s[...] += _dot((_silu(hd[:, :D_EXPERT]) * hd[:, D_EXPERT:] * wcol).astype(BF16), w2_s[...])

    @pl.when(e == pl.num_programs(0) - 1)
    def _():
        ys_ref[...] = x1s_ref[...] + gt2s_ref[...] * _rms(acc_s[...], gpost_ref[...])

    def group_body(q, carry):
        @pl.when(q + FFN_RING - 1 < n_groups)
        def _():
            fetch(q + FFN_RING - 1, True)

        @pl.when(q >= FFN_RING)
        def _():
            writeback(q - FFN_RING, False)

        fetch(q, False)
        rows = FFN_GROUP * FFN_TILE // FFN_SPLIT
        for part in range(FFN_SPLIT):
            span = pl.ds(part * rows * ROW_SUB, rows * ROW_SUB)
            x = _unpack_rows(xbuf.at[q % FFN_RING, span], rows).astype(BF16)
            h = _dot(x, w13_s[...])
            act = _silu(h[:, :D_EXPERT]) * h[:, D_EXPERT:]
            y = _dot(act.astype(BF16), w2_s[...])
            out = ybuf.at[q % FFN_RING, span]
            for j in range(ROW_SUB):
                out[pl.ds(j, rows, stride=ROW_SUB), :] = _pack_pair(
                    y[:, j * LANES:(j + 1) * LANES], y[:, ROW_WORDS + j * LANES:ROW_WORDS + (j + 1) * LANES])
        writeback(q, True)
        return carry

    lax.fori_loop(gq_ref[e], gq_ref[e + 1], group_body, 0)

    @pl.when(e == pl.num_programs(0) - 1)
    def _():
        for d in range(FFN_RING):
            @pl.when(n_groups - 1 - d >= 0)
            def _():
                writeback(n_groups - 1 - d, False)


def _expert_ffn(group_of_expert, group_start, item_off, item_blocks, x_slots, w1, w3, w2,
                h2_s, wd_s, x1_s, gt2_s, gpost, ws1, ws3, ws2):
    of_expert = lambda e, *_: (e, 0, 0)
    whole = lambda a: pl.BlockSpec(a.shape, lambda e, *_: (0,) * a.ndim)
    tiles = pltpu.VMEM((FFN_RING, FFN_GROUP * FFN_TILE * ROW_SUB, LANES), jnp.uint32)
    sems = pltpu.SemaphoreType.DMA((FFN_RING,))
    dense = (h2_s, wd_s, x1_s, gt2_s, gpost, ws1, ws3, ws2)
    grid_spec = pltpu.PrefetchScalarGridSpec(
        num_scalar_prefetch=4,
        grid=(N_EXPERTS,),
        in_specs=[pl.BlockSpec(memory_space=pl.ANY),
                  pl.BlockSpec((1, D_MODEL, D_EXPERT), of_expert),
                  pl.BlockSpec((1, D_MODEL, D_EXPERT), of_expert),
                  pl.BlockSpec((1, D_EXPERT, D_MODEL), of_expert)] + [whole(a) for a in dense],
        out_specs=(pl.BlockSpec(memory_space=pl.ANY), whole(x1_s)),
        scratch_shapes=[pltpu.VMEM((D_MODEL, 2 * D_EXPERT), BF16), pltpu.VMEM((D_EXPERT, D_MODEL), BF16),
                        tiles, tiles, sems, sems, pltpu.VMEM(x1_s.shape, F32)])
    return pl.pallas_call(
        _ffn_kernel,
        out_shape=(jax.ShapeDtypeStruct(x_slots.shape, jnp.uint32), jax.ShapeDtypeStruct(x1_s.shape, F32)),
        grid_spec=grid_spec,
        input_output_aliases={4: 0},
        compiler_params=pltpu.CompilerParams(dimension_semantics=("arbitrary",), vmem_limit_bytes=VMEM_LIMIT),
        name="expert_ffn",
    )(group_of_expert, group_start, item_off, item_blocks, x_slots, w1, w3, w2, *dense)


def _work_items(cnt, loff):
    nw = cnt.shape[0]
    cnt = cnt.astype(jnp.int32).T.reshape(-1)
    loff = loff.astype(jnp.int32).T.reshape(-1)
    region = jnp.tile(jnp.arange(nw, dtype=jnp.int32) * REGION_ROWS, N_EXPERTS)
    n_items = nw * N_EXPERTS + (nw * PLAN_WINDOW * TOP_K) // FFN_TILE
    tiles = (cnt + FFN_TILE - 1) // FFN_TILE
    ends = jnp.cumsum(tiles)
    starts = ends - tiles
    i = jnp.arange(n_items + FFN_GROUP, dtype=jnp.int32)[:, None]
    mine = (starts[None, :] <= i) & (i < ends[None, :])
    of_pair = lambda v: jnp.sum(jnp.where(mine, v[None, :], 0), axis=1)
    r = i[:, 0] - of_pair(starts)
    row = of_pair(region + loff) + r * FFN_TILE
    own = jnp.clip(of_pair(cnt) - r * FFN_TILE, 0, FFN_TILE)

    zero = jnp.zeros((1,), jnp.int32)
    item_end = ends.reshape(N_EXPERTS, nw)[:, -1]
    item_start = jnp.concatenate([zero, item_end[:-1]])
    group_end = jnp.cumsum((item_end - item_start + FFN_GROUP - 1) // FFN_GROUP)
    group_first = jnp.concatenate([zero, group_end[:-1]])
    q = jnp.arange(N_EXPERTS + n_items // FFN_GROUP + 1, dtype=jnp.int32)[:, None]
    has = (group_first[None, :] <= q) & (q < group_end[None, :])
    of_expert = lambda v: jnp.sum(jnp.where(has, v[None, :], 0), axis=1)
    group_start = jnp.where(q[:, 0] < group_end[-1],
                            of_expert(item_start) + (q[:, 0] - of_expert(group_first)) * FFN_GROUP, item_end[-1])
    return (jnp.concatenate([zero, group_end]), group_start, row // RUN_ALIGN, (own + RUN_ALIGN - 1) // RUN_ALIGN)


def _combine_kernel(dest_ref, wsel_ref, y_hbm, h_ref, x1_ref, gt2_ref, gpost_ref, ws13_ref, ws2_ref, out_ref,
                    dest_s, wsel_s, lo_s, hi_s, ybuf, sem, ysem):
    w, s = pl.program_id(0), pl.program_id(1)
    region = REGION_ROWS * ROW_SUB
    part = region // REGION_PARTS

    def region_copies(win):
        return [pltpu.make_async_copy(y_hbm.at[pl.ds(win * region + p * part, part)],
                                      ybuf.at[win % 2, pl.ds(p * part, part)], ysem.at[win % 2, p])
                for p in range(REGION_PARTS)]

    @pl.when(s == 0)
    def _():
        copies = _table_copies(dest_ref, dest_s, sem.at[0]) + _table_copies(wsel_ref, wsel_s, sem.at[1])
        for c in copies:
            c.start()

        @pl.when(w == 0)
        def _():
            for c in region_copies(w):
                c.start()

        @pl.when(w + 1 < pl.num_programs(0))
        def _():
            for c in region_copies(w + 1):
                c.start()

        for c in copies + region_copies(w):
            c.wait()

    y_ref = ybuf.at[w % 2]

    base = s * COMBINE_TILE

    def body(i, carry):
        for u in range(COMBINE_UNROLL):
            t = i * COMBINE_UNROLL + u
            acc_lo = jnp.zeros((ROW_SUB, LANES), F32)
            acc_hi = jnp.zeros((ROW_SUB, LANES), F32)
            for k in range(TOP_K):
                at = k * PLAN_WINDOW + base + t
                lo, hi = _unpack_pair(y_ref[_rows_at(dest_s[at]), :])
                w = wsel_s[at]
                acc_lo = acc_lo + w * lo
                acc_hi = acc_hi + w * hi
            lo_s[_slot_rows(t), :] = acc_lo
            hi_s[_slot_rows(t), :] = acc_hi
        return carry

    lax.fori_loop(0, COMBINE_TILE // COMBINE_UNROLL, body, 0)
    rows = lambda ref: [ref[pl.ds(j, COMBINE_TILE, stride=ROW_SUB), :] for j in range(ROW_SUB)]
    routed = jnp.concatenate(rows(lo_s) + rows(hi_s), axis=1)
    h_lo, h_hi = _unpack_pair(h_ref[...])
    x = jnp.concatenate([h_lo, h_hi], axis=1).astype(BF16)
    hs = _dot(x, ws13_ref[...])
    act = _silu(hs[:, :D_EXPERT]) * hs[:, D_EXPERT:]
    ff = routed + _dot(act.astype(BF16), ws2_ref[...])
    out_ref[...] = x1_ref[...] + gt2_ref[0] * _rms(ff, gpost_ref[...])


def _combine(dest_w, wsel_w, y_slots, h2, x1, gt2, tokens_per_gt2, gpost, ws13, ws2):
    nw = dest_w.shape[0]
    sub = PLAN_WINDOW // COMBINE_TILE
    plan = pl.BlockSpec((1, TOP_K, PLAN_WINDOW), lambda w, s: (w, 0, 0))
    tok = lambda w, s: (w * sub + s, 0)
    return pl.pallas_call(
        _combine_kernel,
        out_shape=jax.ShapeDtypeStruct(x1.shape, F32),
        grid=(nw, sub),
        in_specs=[plan, plan,
                  pl.BlockSpec(memory_space=pl.ANY),
                  pl.BlockSpec((COMBINE_TILE, ROW_WORDS), tok),
                  pl.BlockSpec((COMBINE_TILE, D_MODEL), tok),
                  pl.BlockSpec((1, 1, D_MODEL), lambda w, s: ((w * PLAN_WINDOW) // tokens_per_gt2, 0, 0)),
                  _full((1, D_MODEL)), _full(ws13.shape), _full(ws2.shape)],
        out_specs=pl.BlockSpec((COMBINE_TILE, D_MODEL), tok),
        scratch_shapes=[pltpu.SMEM((TOP_K * PLAN_WINDOW,), jnp.int32),
                        pltpu.SMEM((TOP_K * PLAN_WINDOW,), F32),
                        pltpu.VMEM((COMBINE_TILE * ROW_SUB, LANES), F32),
                        pltpu.VMEM((COMBINE_TILE * ROW_SUB, LANES), F32),
                        pltpu.VMEM((2, REGION_ROWS * ROW_SUB, LANES), jnp.uint32),
                        pltpu.SemaphoreType.DMA((2,)), pltpu.SemaphoreType.DMA((2, REGION_PARTS))],
        compiler_params=pltpu.CompilerParams(dimension_semantics=("arbitrary", "arbitrary"),
                                             vmem_limit_bytes=VMEM_LIMIT),
        name="combine",
    )(dest_w, wsel_w, y_slots, h2, x1, gt2, gpost, ws13, ws2)


def kernel(x_prompt, x_sample, c_prompt, c_sample, state_pool, state_hgrn, w_ada, b_ada, g_pre_mix, g_post_mix,
           w_in, w_pool, pool_scale, lb_logits, g_out_norm, w_out, g_pre_ffn, g_post_ffn, w_router, router_bias,
           w_exp_gate, w_exp_up, w_exp_down, w_sh_gate, w_sh_up, w_sh_down):
    assert w_ada.shape[0] == 1 and lb_logits.shape[0] == 2, "single-layer trunk"
    bp, lp, d = x_prompt.shape
    bs = x_sample.shape[0]
    row = lambda a: a[0].reshape(1, -1)

    n_mod = bp + bs
    pad = (-n_mod) % 16
    c_all = jnp.concatenate([c_prompt, c_sample, jnp.zeros((pad, d), F32)], axis=0)
    mod = _ada(c_all, w_ada[0], b_ada)
    mod_p = mod[:bp].reshape(bp, 1, 6 * d)
    mod_s = mod[bp:n_mod]

    win = w_in[0].astype(BF16)
    wout = w_out[0].astype(BF16)
    wpool = w_pool[0].astype(BF16)
    wr = jnp.pad(w_router[0], ((0, 0), (0, LANES - N_EXPERTS)))
    wrh = wr.astype(BF16)
    wrl = (wr - wrh.astype(F32)).astype(BF16)
    m2 = jnp.asarray(_M2_NP, BF16)
    gpre, gpost, gffn, gffn_post = row(g_pre_mix), row(g_post_mix), row(g_pre_ffn), row(g_post_ffn)
    pscale, gout = row(pool_scale), row(g_out_norm)
    bias_col = jnp.pad(router_bias[0], (0, LANES - N_EXPERTS)).reshape(LANES, 1)

    x1_p, h2_p, lg_p, pool_p, st_p = _mix_prompt(x_prompt, mod_p, gpre, gpost, gffn, win, wpool, pscale,
                                                  lb_logits, gout, wout, wrh, wrl, m2)
    x1_s, h2_s, lg_s, pool_s, st_s = _mix_sample(x_sample[:, 0, :], mod_s, state_pool[0], state_hgrn[0],
                                                  gpre, gpost, gffn, win, wpool, pscale, lb_logits, gout, wout,
                                                  wrh, wrl)

    experts = (w_exp_gate[0], w_exp_up[0], w_exp_down[0], w_sh_gate[0], w_sh_up[0], w_sh_down[0])
    tp = bp * lp
    nw = tp // PLAN_WINDOW
    assert lp % PLAN_WINDOW == 0
    upper = jnp.asarray(np.triu(np.ones((PLAN_WINDOW, PLAN_WINDOW), np.float32), 1), BF16)
    dest, wsel, cnt, loff = _plan(lg_p.reshape(tp, LANES), bias_col, upper)
    dest_w, wsel_w = dest, wsel
    h2_rows = h2_p.reshape(tp, ROW_WORDS)
    x_slots = _dispatch(dest_w, h2_rows)
    wd_s = _router(lg_s, bias_col)
    y_slots, y_s = _expert_ffn(*_work_items(cnt[:, :, 0], loff[:, :, 0]), x_slots, *experts[:3],
                               h2_s, wd_s, x1_s, mod_s[:, 5 * d:], gffn_post, *experts[3:])
    ws13 = jnp.concatenate([experts[3], experts[4]], axis=1).astype(BF16)
    gt2_p = mod_p[:, :, 5 * d:]
    y_p = _combine(dest_w, wsel_w, y_slots, h2_rows, x1_p.reshape(tp, d), gt2_p, lp, gffn_post,
                   ws13, experts[5].astype(BF16))

    return (y_p.reshape(bp, lp, d), y_s.reshape(bs, 1, d), pool_p[None], st_p[None], pool_s[None], st_s[None])
```

```python
import functools

import numpy as np
import jax
import jax.numpy as jnp
from jax import lax
from jax.experimental import pallas as pl
from jax.experimental.pallas import tpu as pltpu

F32 = jnp.float32
BF16 = jnp.bfloat16

D_MODEL = 1024
POOL_WIDTH = 512
POOL_WINDOWS = (2, 4, 8, 16)
POOL_GROUP = 128
POOL_HIST = 15
HG_WIDTH = 512
HG_HEADS = 4
HG_HEAD_DIM = 128
IN_WIDTH = POOL_WIDTH + 4 * HG_WIDTH
N_EXPERTS = 64
TOP_K = 8
N_GROUPS = 8
TOPK_GROUPS = 4
GROUP_SIZE = N_EXPERTS // N_GROUPS
D_EXPERT = 256
ROUTED_SCALE = 2.5
EPS = 1e-6

LANES = 128
CHUNK = 64
TIME_TILE = 256
ADA_TILE_N = 1024
ROUTER_TILE = 512
SAMPLE_STATE_BLOCK = 16
PLAN_WINDOW = 1024
RUN_ALIGN = 32
FFN_TILE = 160
REGION_ROWS = PLAN_WINDOW * TOP_K + N_EXPERTS * (RUN_ALIGN - 1) + FFN_TILE
ROW_WORDS = D_MODEL // 2
ROW_SUB = ROW_WORDS // LANES
COMBINE_TILE = 256
REGION_PARTS = 8
SCATTER_UNROLL = 4
COMBINE_UNROLL = 16
FFN_GROUP = 4
FFN_SPLIT = 1
FFN_RING = 6
VMEM_LIMIT = 58 * 1024 * 1024

_LEVELS = (64, 32, 16, 8, 4, 2)
_BLK_CUM = 0


def _level_blocks():
    c = CHUNK
    i = np.arange(c)[:, None]
    s = np.arange(c)[None, :]
    blocks = [(s <= i)]
    index = {}
    for lvl in _LEVELS:
        ref = (i // lvl) * lvl + lvl // 2 - 1
        index[lvl] = len(blocks)
        blocks.append(s <= ref)
    m = np.concatenate(blocks, axis=0).astype(np.float32)
    return np.concatenate([m, m], axis=1), index


_M2_NP, _LEVEL_INDEX = _level_blocks()


def _dot(a, b):
    return jnp.dot(a, b, preferred_element_type=F32)


def _dot_nt(a, b):
    return lax.dot_general(a, b, (((1,), (1,)), ((), ())), preferred_element_type=F32)


def _split2(x):
    hi = x.astype(BF16)
    lo = (x - hi.astype(F32)).astype(BF16)
    return hi, lo


def _pack_pair(a, b):
    lo = lax.bitcast_convert_type(a.astype(BF16).astype(F32), jnp.uint32)
    hi = lax.bitcast_convert_type(b.astype(BF16).astype(F32), jnp.uint32)
    return (lo >> 16) | hi


def _unpack_pair(words):
    lo = lax.bitcast_convert_type(words << 16, F32)
    hi = lax.bitcast_convert_type(words & jnp.uint32(0xFFFF0000), F32)
    return lo, hi


def _silu(x):
    return x * jax.nn.sigmoid(x)


def _rms(x, g):
    return x * lax.rsqrt(jnp.mean(x * x, axis=-1, keepdims=True) + EPS) * g


def _mods(m):
    return [m[:, j * D_MODEL:(j + 1) * D_MODEL] for j in range(6)]


def _forget_lower_bound(lbl):
    mx = jnp.max(lbl, axis=0, keepdims=True)
    e = jnp.exp(lbl - mx)
    return e[0:1] / jnp.sum(e, axis=0, keepdims=True)


def _router_logits(h2, wrh, wrl):
    hi, lo = _split2(h2)
    return _dot(hi, wrh) + _dot(lo, wrh) + _dot(hi, wrl)


def _ada_kernel(c_ref, w_ref, b_ref, o_ref):
    a_hi, a_lo = _split2(_silu(c_ref[...]))
    w_hi, w_lo = _split2(w_ref[...])
    o_ref[...] = _dot(a_hi, w_hi) + _dot(a_lo, w_hi) + _dot(a_hi, w_lo) + b_ref[...]


def _ada(c_all, w_ada, b_ada):
    rows = c_all.shape[0]
    n = w_ada.shape[1]
    return pl.pallas_call(
        _ada_kernel,
        out_shape=jax.ShapeDtypeStruct((rows, n), F32),
        grid=(n // ADA_TILE_N,),
        in_specs=[pl.BlockSpec((rows, D_MODEL), lambda j: (0, 0)),
                  pl.BlockSpec((D_MODEL, ADA_TILE_N), lambda j: (0, j)),
                  pl.BlockSpec((1, ADA_TILE_N), lambda j: (0, j))],
        out_specs=pl.BlockSpec((rows, ADA_TILE_N), lambda j: (0, j)),
        compiler_params=pltpu.CompilerParams(dimension_semantics=("parallel",),
                                             vmem_limit_bytes=VMEM_LIMIT),
        name="ada",
    )(c_all, w_ada, b_ada)


def _level_masks():
    i = lax.broadcasted_iota(jnp.int32, (CHUNK, 2 * CHUNK), 0)
    j = lax.broadcasted_iota(jnp.int32, (CHUNK, 2 * CHUNK), 1) & (CHUNK - 1)
    masks = {}
    for lvl in _LEVELS:
        sh = lvl.bit_length() - 1
        same = (i >> sh) == (j >> sh)
        upper = ((i >> (sh - 1)) & 1) == 1
        lower = ((j >> (sh - 1)) & 1) == 0
        masks[lvl] = same & upper & lower
    return masks, i == j


def _by_head(m):
    z = jnp.zeros((m.shape[0], HG_HEAD_DIM), m.dtype)
    return jnp.concatenate([jnp.concatenate([m[:, :HG_HEAD_DIM], z], axis=1),
                            jnp.concatenate([z, m[:, HG_HEAD_DIM:]], axis=1)], axis=0)


def _hgrn_chunk_pair(q, k, v, e_all, lane0, sts, masks, eye):
    hd = HG_HEAD_DIM

    def blk(n):
        return e_all[n * CHUNK:(n + 1) * CHUNK, lane0:lane0 + 2 * hd]

    b = blk(_BLK_CUM)
    kb = k.astype(BF16)
    a = jnp.where(eye, _dot_nt(q.astype(BF16), _by_head(kb)), 0.0)
    for lvl in _LEVELS:
        d = b - blk(_LEVEL_INDEX[lvl])
        ql = (q * jnp.exp(jnp.minimum(d, 0.0))).astype(BF16)
        kl = (k * jnp.exp(jnp.minimum(-d, 0.0))).astype(BF16) if lvl > 2 else kb
        a = a + jnp.where(masks[lvl], _dot_nt(ql, _by_head(kl)), 0.0)
    st_both = jnp.concatenate([jnp.concatenate([sts[0], jnp.zeros_like(sts[0])], axis=1),
                               jnp.concatenate([jnp.zeros_like(sts[1]), sts[1]], axis=1)], axis=0).astype(BF16)
    o = _dot(a.astype(BF16), _by_head(v.astype(BF16))) + _dot_nt((q * jnp.exp(b)).astype(BF16), st_both)
    b_last = b[CHUNK - 1:CHUNK, :]
    k_end = (k * jnp.exp(b_last - b)).astype(BF16)
    decay = jnp.exp(b_last)
    new = []
    for h in range(2):
        lanes = slice(h * hd, (h + 1) * hd)
        new.append(sts[h] * decay[:, lanes] + _dot(v[:, lanes].T.astype(BF16), k_end[:, lanes]))
    return o, new


def _mix_prompt_kernel(x_ref, mod_ref, gpre_ref, gpost_ref, gffn_ref, win_ref, wpool_ref, pscale_ref,
                       lbl_ref, gout_ref, wout_ref, wrh_ref, wrl_ref, m2_ref,
                       x1_ref, h2_ref, lg_ref, pool_ref, st_ref,
                       st_s, ubuf, q_s, k_s, v_s, g_s, o_s):
    t = pl.program_id(1)
    n_t = pl.num_programs(1)
    tt = TIME_TILE

    @pl.when(t == 0)
    def _():
        st_s[...] = jnp.zeros_like(st_s)
        ubuf[0:16, :] = jnp.zeros((16, POOL_WIDTH), F32)

    xt = x_ref[0]
    sh1, sc1, gt1, sh2, sc2, gt2 = _mods(mod_ref[0])
    h = _rms(xt, gpre_ref[...]) * (1.0 + sc1) + sh1
    proj = _dot(h.astype(BF16), win_ref[...])

    u = proj[:, :POOL_WIDTH]
    ubuf[16:16 + tt, :] = u
    pos = (t * tt + lax.broadcasted_iota(jnp.int32, (tt, 1), 0) + 1).astype(F32)
    ys = []
    for g, w in enumerate(POOL_WINDOWS):
        s = ubuf[:, g * POOL_GROUP:(g + 1) * POOL_GROUP]
        off = 0
        for step in range(w.bit_length() - 1):
            sh = 1 << step
            s = s[sh:, :] + s[:-sh, :]
            off += sh
        ws = s[16 - off:16 - off + tt, :]
        cnt = jnp.minimum(pos, float(w))
        d = ws / cnt - u[:, g * POOL_GROUP:(g + 1) * POOL_GROUP]
        ys.append(_dot(d.astype(BF16), wpool_ref[g]))
    y_pool = jnp.concatenate(ys, axis=1) * pscale_ref[...]
    ubuf[0:16, :] = ubuf[tt:tt + 16, :]

    @pl.when(t == n_t - 1)
    def _():
        pool_ref[0] = u[tt - POOL_HIST:, :]

    lb = _forget_lower_bound(lbl_ref[...])
    f = lb + (1.0 - lb) * jax.nn.sigmoid(proj[:, POOL_WIDTH + HG_WIDTH:POOL_WIDTH + 2 * HG_WIDTH])
    q_s[...] = _silu(proj[:, POOL_WIDTH:POOL_WIDTH + HG_WIDTH])
    k_s[...] = 1.0 - f
    v_s[...] = proj[:, POOL_WIDTH + 2 * HG_WIDTH:POOL_WIDTH + 3 * HG_WIDTH]
    g_s[...] = jnp.log(f)
    gate = _silu(proj[:, POOL_WIDTH + 3 * HG_WIDTH:])

    masks, eye = _level_masks()
    m2 = m2_ref[...]

    states = [st_s[hd] for hd in range(HG_HEADS)]
    for c in range(tt // CHUNK):
        rows = slice(c * CHUNK, (c + 1) * CHUNK)
        g_hi, g_lo = _split2(g_s[rows, :])
        e_all = _dot(m2, jnp.concatenate([g_hi, g_lo], axis=0))
        for hd in range(0, HG_HEADS, 2):
            lane0 = hd * HG_HEAD_DIM
            lanes = slice(lane0, lane0 + 2 * HG_HEAD_DIM)
            o, states[hd:hd + 2] = _hgrn_chunk_pair(q_s[rows, lanes], k_s[rows, lanes], v_s[rows, lanes], e_all,
                                                    lane0, states[hd:hd + 2], masks, eye)
            o_s[rows, lanes] = o
    for hd in range(HG_HEADS):
        st_s[hd] = states[hd]

    o = o_s[...]
    os_ = []
    for hd in range(HG_HEADS):
        oh = o[:, hd * HG_HEAD_DIM:(hd + 1) * HG_HEAD_DIM]
        os_.append(oh * lax.rsqrt(jnp.mean(oh * oh, axis=-1, keepdims=True) + EPS) * gout_ref[...])
    o_n = jnp.concatenate(os_, axis=1) * gate

    mix = _dot(y_pool.astype(BF16), wout_ref[0:POOL_WIDTH, :]) + _dot(o_n.astype(BF16), wout_ref[POOL_WIDTH:, :])
    x1 = xt + gt1 * _rms(mix, gpost_ref[...])
    h2 = _rms(x1, gffn_ref[...]) * (1.0 + sc2) + sh2
    x1_ref[0] = x1
    h2_ref[0] = _pack_pair(h2[:, :ROW_WORDS], h2[:, ROW_WORDS:])
    lg_ref[0] = _router_logits(h2, wrh_ref[...], wrl_ref[...])

    @pl.when(t == n_t - 1)
    def _():
        for hd in range(HG_HEADS):
            st_ref[0, hd] = st_s[hd].T


def _full(shape):
    nd = len(shape)
    return pl.BlockSpec(shape, lambda *_: (0,) * nd)


def _mix_prompt(x, mod, gpre, gpost, gffn, win, wpool, pscale, lbl, gout, wout, wrh, wrl, m2):
    b, l, d = x.shape
    tt = TIME_TILE
    tile = lambda i, j: (i, j, 0)
    per_b = lambda i, j: (i, 0, 0)
    return pl.pallas_call(
        _mix_prompt_kernel,
        out_shape=(jax.ShapeDtypeStruct((b, l, d), F32),
                   jax.ShapeDtypeStruct((b, l, ROW_WORDS), jnp.uint32),
                   jax.ShapeDtypeStruct((b, l, LANES), F32),
                   jax.ShapeDtypeStruct((b, POOL_HIST, POOL_WIDTH), F32),
                   jax.ShapeDtypeStruct((b, HG_HEADS, HG_HEAD_DIM, HG_HEAD_DIM), F32)),
        grid=(b, l // tt),
        in_specs=[pl.BlockSpec((1, tt, d), tile),
                  pl.BlockSpec((1, 1, 6 * d), per_b),
                  _full((1, d)), _full((1, d)), _full((1, d)),
                  _full(win.shape), _full(wpool.shape), _full((1, POOL_WIDTH)),
                  _full(lbl.shape), _full((1, HG_HEAD_DIM)), _full(wout.shape),
                  _full(wrh.shape), _full(wrl.shape), _full(m2.shape)],
        out_specs=(pl.BlockSpec((1, tt, d), tile),
                   pl.BlockSpec((1, tt, ROW_WORDS), tile),
                   pl.BlockSpec((1, tt, LANES), tile),
                   pl.BlockSpec((1, POOL_HIST, POOL_WIDTH), per_b),
                   pl.BlockSpec((1, HG_HEADS, HG_HEAD_DIM, HG_HEAD_DIM), lambda i, j: (i, 0, 0, 0))),
        scratch_shapes=[pltpu.VMEM((HG_HEADS, HG_HEAD_DIM, HG_HEAD_DIM), F32),
                        pltpu.VMEM((tt + 16, POOL_WIDTH), F32),
                        pltpu.VMEM((tt, HG_WIDTH), F32), pltpu.VMEM((tt, HG_WIDTH), F32),
                        pltpu.VMEM((tt, HG_WIDTH), F32), pltpu.VMEM((tt, HG_WIDTH), F32),
                        pltpu.VMEM((tt, HG_WIDTH), F32)],
        compiler_params=pltpu.CompilerParams(dimension_semantics=("parallel", "arbitrary"),
                                             vmem_limit_bytes=VMEM_LIMIT),
        name="mix_prompt",
    )(x, mod, gpre, gpost, gffn, win, wpool, pscale, lbl, gout, wout, wrh, wrl, m2)


def _mix_sample_in_kernel(x_ref, mod_ref, gpre_ref, win_ref, wpool_ref, pscale_ref, lbl_ref, hist_ref,
                          ypool_ref, npool_ref, ft_ref, qt_ref, v_ref, gate_ref):
    xt = x_ref[...]
    sh1, sc1 = _mods(mod_ref[...])[:2]
    h = _rms(xt, gpre_ref[...]) * (1.0 + sc1) + sh1
    proj = _dot(h.astype(BF16), win_ref[...])
    u = proj[:, :POOL_WIDTH]
    row = lax.broadcasted_iota(jnp.int32, (hist_ref.shape[0], POOL_HIST, POOL_GROUP), 1)
    ys = []
    for g, w in enumerate(POOL_WINDOWS):
        sl = slice(g * POOL_GROUP, (g + 1) * POOL_GROUP)
        past = jnp.sum(jnp.where(row >= POOL_HIST - (w - 1), hist_ref[:, :, sl], 0.0), axis=1)
        ug = u[:, sl]
        d = (past + ug) / float(w) - ug
        ys.append(_dot(d.astype(BF16), wpool_ref[g]))
    ypool_ref[...] = jnp.concatenate(ys, axis=1) * pscale_ref[...]
    npool_ref[:, 0:POOL_HIST - 1, :] = hist_ref[:, 1:POOL_HIST, :]
    npool_ref[:, POOL_HIST - 1, :] = u

    lb = _forget_lower_bound(lbl_ref[...])
    f = lb + (1.0 - lb) * jax.nn.sigmoid(proj[:, POOL_WIDTH + HG_WIDTH:POOL_WIDTH + 2 * HG_WIDTH])
    ft_ref[...] = f.T
    qt_ref[...] = _silu(proj[:, POOL_WIDTH:POOL_WIDTH + HG_WIDTH]).T
    v_ref[...] = proj[:, POOL_WIDTH + 2 * HG_WIDTH:POOL_WIDTH + 3 * HG_WIDTH]
    gate_ref[...] = _silu(proj[:, POOL_WIDTH + 3 * HG_WIDTH:])


def _mix_sample_state_kernel(s_ref, ft_ref, qt_ref, v_ref, snew_ref, o_ref):
    i = pl.program_id(0)
    lane = lax.broadcasted_iota(jnp.int32, (HG_HEAD_DIM, ft_ref.shape[1]), 1)
    for j in range(SAMPLE_STATE_BLOCK):
        mine = lane == i * SAMPLE_STATE_BLOCK + j
        for hd in range(HG_HEADS):
            r0 = hd * HG_HEAD_DIM
            f = jnp.sum(jnp.where(mine, ft_ref[r0:r0 + HG_HEAD_DIM, :], 0.0), axis=1, keepdims=True)
            q = jnp.sum(jnp.where(mine, qt_ref[r0:r0 + HG_HEAD_DIM, :], 0.0), axis=1, keepdims=True)
            v = v_ref[j:j + 1, r0:r0 + HG_HEAD_DIM]
            s_new = f * s_ref[j, hd] + (1.0 - f) * v
            snew_ref[j, hd] = s_new
            o_ref[j:j + 1, r0:r0 + HG_HEAD_DIM] = jnp.sum(q * s_new, axis=0, keepdims=True)


def _mix_sample_out_kernel(x_ref, mod_ref, o_ref, gate_ref, ypool_ref, gout_ref, wout_ref, gpost_ref,
                           gffn_ref, wrh_ref, wrl_ref, x1_ref, h2_ref, lg_ref):
    _, _, gt1, sh2, sc2, _ = _mods(mod_ref[...])
    o = o_ref[...]
    os_ = []
    for hd in range(HG_HEADS):
        oh = o[:, hd * HG_HEAD_DIM:(hd + 1) * HG_HEAD_DIM]
        os_.append(oh * lax.rsqrt(jnp.mean(oh * oh, axis=-1, keepdims=True) + EPS) * gout_ref[...])
    o_n = jnp.concatenate(os_, axis=1) * gate_ref[...]
    mix = (_dot(ypool_ref[...].astype(BF16), wout_ref[0:POOL_WIDTH, :])
           + _dot(o_n.astype(BF16), wout_ref[POOL_WIDTH:, :]))
    x1 = x_ref[...] + gt1 * _rms(mix, gpost_ref[...])
    h2 = _rms(x1, gffn_ref[...]) * (1.0 + sc2) + sh2
    x1_ref[...] = x1
    h2_ref[...] = h2.astype(BF16)
    lg_ref[...] = _router_logits(h2, wrh_ref[...], wrl_ref[...])


def _mix_sample(x, mod, hist, state, gpre, gpost, gffn, win, wpool, pscale, lbl, gout, wout, wrh, wrl):
    b = x.shape[0]
    bb = SAMPLE_STATE_BLOCK
    cp = pltpu.CompilerParams(vmem_limit_bytes=VMEM_LIMIT)
    ypool, npool, ft, qt, v, gate = pl.pallas_call(
        _mix_sample_in_kernel,
        out_shape=(jax.ShapeDtypeStruct((b, POOL_WIDTH), F32),
                   jax.ShapeDtypeStruct((b, POOL_HIST, POOL_WIDTH), F32),
                   jax.ShapeDtypeStruct((HG_WIDTH, b), F32),
                   jax.ShapeDtypeStruct((HG_WIDTH, b), F32),
                   jax.ShapeDtypeStruct((b, HG_WIDTH), F32),
                   jax.ShapeDtypeStruct((b, HG_WIDTH), F32)),
        compiler_params=cp,
        name="mix_sample_in",
    )(x, mod, gpre, win, wpool, pscale, lbl, hist)

    s_spec = pl.BlockSpec((bb, HG_HEADS, HG_HEAD_DIM, HG_HEAD_DIM), lambda i: (i, 0, 0, 0))
    col_spec = _full((HG_WIDTH, b))
    row_spec = pl.BlockSpec((bb, HG_WIDTH), lambda i: (i, 0))
    s_new, o = pl.pallas_call(
        _mix_sample_state_kernel,
        out_shape=(jax.ShapeDtypeStruct(state.shape, F32), jax.ShapeDtypeStruct((b, HG_WIDTH), F32)),
        grid=(b // bb,),
        in_specs=[s_spec, col_spec, col_spec, row_spec],
        out_specs=(s_spec, row_spec),
        compiler_params=pltpu.CompilerParams(dimension_semantics=("parallel",), vmem_limit_bytes=VMEM_LIMIT),
        name="mix_sample_state",
    )(state, ft, qt, v)

    x1, h2, lg = pl.pallas_call(
        _mix_sample_out_kernel,
        out_shape=(jax.ShapeDtypeStruct((b, D_MODEL), F32),
                   jax.ShapeDtypeStruct((b, D_MODEL), BF16),
                   jax.ShapeDtypeStruct((b, LANES), F32)),
        compiler_params=cp,
        name="mix_sample_out",
    )(x, mod, o, gate, ypool, gout, wout, gpost, gffn, wrh, wrl)
    return x1, h2, lg, npool, s_new


def _first_max(cur, idx, big):
    m = jnp.max(cur, axis=0, keepdims=True)
    first = jnp.min(jnp.where(cur == m, idx, big), axis=0, keepdims=True)
    return idx == first


def _route(logit, bias):
    n = logit.shape[1]
    scores = jax.nn.sigmoid(logit)
    sel = scores + bias
    neg = -jnp.inf

    sub = lax.broadcasted_iota(jnp.int32, (GROUP_SIZE, n), 0)
    gscore = []
    for g in range(N_GROUPS):
        sg = sel[g * GROUP_SIZE:(g + 1) * GROUP_SIZE, :]
        m1 = jnp.max(sg, axis=0, keepdims=True)
        rest = jnp.where(_first_max(sg, sub, GROUP_SIZE), neg, sg)
        gscore.append(m1 + jnp.max(rest, axis=0, keepdims=True))
    cur = jnp.concatenate(gscore, axis=0)
    gidx = lax.broadcasted_iota(jnp.int32, (N_GROUPS, n), 0)
    gmask = jnp.zeros((N_GROUPS, n), jnp.bool_)
    for _ in range(TOPK_GROUPS):
        pick = _first_max(cur, gidx, N_GROUPS)
        gmask = gmask | pick
        cur = jnp.where(pick, neg, cur)
    emask = jnp.concatenate(
        [jnp.broadcast_to(gmask[g:g + 1, :], (GROUP_SIZE, n)) for g in range(N_GROUPS)], axis=0)

    cur = jnp.where(emask, sel, neg)
    eidx = lax.broadcasted_iota(jnp.int32, (N_EXPERTS, n), 0)
    picks = []
    for _ in range(TOP_K):
        pick = _first_max(cur, eidx, N_EXPERTS)
        picks.append(pick)
        cur = jnp.where(pick, neg, cur)
    return scores, picks


def _any(masks):
    return functools.reduce(jnp.logical_or, masks)


def _router_kernel(lg_ref, bias_ref, wd_ref):
    n = lg_ref.shape[0]
    scores, picks = _route(lg_ref[...].T[0:N_EXPERTS, :], bias_ref[0:N_EXPERTS, :])
    chosen = _any(picks)
    wsum = jnp.sum(jnp.where(chosen, scores, 0.0), axis=0, keepdims=True)
    wd = jnp.where(chosen, scores / wsum * ROUTED_SCALE, 0.0)
    wd_ref[...] = jnp.concatenate([wd, jnp.zeros((LANES - N_EXPERTS, n), F32)], axis=0).T


def _router(logits, bias_col):
    t = logits.shape[0]
    tile = min(ROUTER_TILE, t)
    return pl.pallas_call(
        _router_kernel,
        out_shape=jax.ShapeDtypeStruct((t, LANES), F32),
        grid=(t // tile,),
        in_specs=[pl.BlockSpec((tile, LANES), lambda i: (i, 0)), _full(bias_col.shape)],
        out_specs=pl.BlockSpec((tile, LANES), lambda i: (i, 0)),
        compiler_params=pltpu.CompilerParams(dimension_semantics=("parallel",), vmem_limit_bytes=VMEM_LIMIT),
        name="router",
    )(logits, bias_col)


def _plan_kernel(lg_ref, bias_ref, upper_ref, dest_ref, wsel_ref, cnt_ref, loff_ref):
    scores, picks = _route(lg_ref[...].T[0:N_EXPERTS, :], bias_ref[0:N_EXPERTS, :])
    chosen = _any(picks)
    chosen_f = jnp.where(chosen, 1.0, 0.0)
    cnt = jnp.sum(chosen_f, axis=1, keepdims=True)
    units = jnp.floor((cnt + (RUN_ALIGN - 1)) / RUN_ALIGN)
    ei = lax.broadcasted_iota(jnp.int32, (N_EXPERTS, N_EXPERTS), 0)
    ej = lax.broadcasted_iota(jnp.int32, (N_EXPERTS, N_EXPERTS), 1)
    before = jnp.where(ej < ei, 1.0, 0.0).astype(BF16)
    loff = RUN_ALIGN * _dot(before, jnp.broadcast_to(units, (N_EXPERTS, LANES)).astype(BF16))[:, 0:1]
    slot = loff + _dot(chosen_f.astype(BF16), upper_ref[...])
    wsum = jnp.sum(jnp.where(chosen, scores, 0.0), axis=0, keepdims=True)
    pick_sum = lambda v: jnp.concatenate(
        [jnp.sum(jnp.where(p, v, 0.0), axis=0, keepdims=True) for p in picks], axis=0)
    dest_ref[0] = (pick_sum(slot) * ROW_SUB).astype(jnp.int32)
    wsel_ref[0] = pick_sum(scores) / wsum * ROUTED_SCALE
    cnt_ref[0] = jnp.broadcast_to(cnt, (N_EXPERTS, LANES))
    loff_ref[0] = jnp.broadcast_to(loff, (N_EXPERTS, LANES))


def _plan(logits, bias_col, upper):
    t = logits.shape[0]
    nw = t // PLAN_WINDOW
    per_w = pl.BlockSpec((1, N_EXPERTS, LANES), lambda w: (w, 0, 0))
    picks = pl.BlockSpec((1, TOP_K, PLAN_WINDOW), lambda w: (w, 0, 0))
    return pl.pallas_call(
        _plan_kernel,
        out_shape=(jax.ShapeDtypeStruct((nw, TOP_K, PLAN_WINDOW), jnp.int32),
                   jax.ShapeDtypeStruct((nw, TOP_K, PLAN_WINDOW), F32),
                   jax.ShapeDtypeStruct((nw, N_EXPERTS, LANES), F32),
                   jax.ShapeDtypeStruct((nw, N_EXPERTS, LANES), F32)),
        grid=(nw,),
        in_specs=[pl.BlockSpec((PLAN_WINDOW, LANES), lambda w: (w, 0)), _full(bias_col.shape), _full(upper.shape)],
        out_specs=(picks, picks, per_w, per_w),
        compiler_params=pltpu.CompilerParams(dimension_semantics=("parallel",), vmem_limit_bytes=VMEM_LIMIT),
        name="plan",
    )(logits, bias_col, upper)


def _slot_rows(slot):
    return pl.ds(pl.multiple_of(slot * ROW_SUB, ROW_SUB), ROW_SUB)


def _table_copies(table_ref, flat_s, sem):
    return [pltpu.make_async_copy(table_ref.at[0, k], flat_s.at[pl.ds(k * PLAN_WINDOW, PLAN_WINDOW)], sem)
            for k in range(TOP_K)]


def _rows_at(offset):
    return pl.ds(pl.multiple_of(offset, ROW_SUB), ROW_SUB)


def _dispatch_kernel(dest_ref, h_ref, x_ref, dest_s, rows_s, sem):
    plan_copies = _table_copies(dest_ref, dest_s, sem)
    for c in plan_copies:
        c.start()
    x_ref[...] = jnp.zeros_like(x_ref)
    for j in range(ROW_SUB):
        rows_s[pl.ds(j, PLAN_WINDOW, stride=ROW_SUB), :] = h_ref[:, j * LANES:(j + 1) * LANES]
    for c in plan_copies:
        c.wait()

    def body(i, carry):
        for u in range(SCATTER_UNROLL):
            t = i * SCATTER_UNROLL + u
            row = rows_s[_slot_rows(t), :]
            for k in range(TOP_K):
                x_ref[_rows_at(dest_s[k * PLAN_WINDOW + t]), :] = row
        return carry

    lax.fori_loop(0, PLAN_WINDOW // SCATTER_UNROLL, body, 0)


def _dispatch(dest_w, h2):
    nw = dest_w.shape[0]
    region = REGION_ROWS * ROW_SUB
    return pl.pallas_call(
        _dispatch_kernel,
        out_shape=jax.ShapeDtypeStruct((nw * region, LANES), jnp.uint32),
        grid=(nw,),
        in_specs=[pl.BlockSpec((1, TOP_K, PLAN_WINDOW), lambda w: (w, 0, 0)),
                  pl.BlockSpec((PLAN_WINDOW, ROW_WORDS), lambda w: (w, 0))],
        out_specs=pl.BlockSpec((region, LANES), lambda w: (w, 0)),
        scratch_shapes=[pltpu.SMEM((TOP_K * PLAN_WINDOW,), jnp.int32),
                        pltpu.VMEM((PLAN_WINDOW * ROW_SUB, LANES), jnp.uint32),
                        pltpu.SemaphoreType.DMA],
        compiler_params=pltpu.CompilerParams(dimension_semantics=("arbitrary",), vmem_limit_bytes=VMEM_LIMIT),
        name="dispatch",
    )(dest_w, h2)


def _unpack_rows(ref, n):
    parts = [_unpack_pair(ref[pl.ds(j, n, stride=ROW_SUB), :]) for j in range(ROW_SUB)]
    return jnp.concatenate([p[0] for p in parts] + [p[1] for p in parts], axis=1)


def _ffn_kernel(gq_ref, gs_ref, to_ref, tn_ref, x_hbm, w1_ref, w3_ref, w2_ref,
                hs_ref, wds_ref, x1s_ref, gt2s_ref, gpost_ref, ws1_ref, ws3_ref, ws2_ref, y_hbm, ys_ref,
                w13_s, w2_s, xbuf, ybuf, rsem, wsem, acc_s):
    del x_hbm
    e = pl.program_id(0)
    n_groups = gq_ref[N_EXPERTS]
    block = RUN_ALIGN * ROW_SUB
    tile = FFN_TILE * ROW_SUB

    def go(cp, start):
        if start:
            cp.start()
        else:
            cp.wait()

    def fetch(q, start):
        def one(j):
            src = y_hbm.at[pl.ds(pl.multiple_of(to_ref[gs_ref[q] + j] * block, block), tile)]
            go(pltpu.make_async_copy(src, xbuf.at[q % FFN_RING, pl.ds(j * tile, tile)], rsem.at[q % FFN_RING]), start)

        size = gs_ref[q + 1] - gs_ref[q]

        @pl.when(size == FFN_GROUP)
        def _():
            if start:
                for j in range(FFN_GROUP):
                    one(j)
            else:
                pltpu.make_async_copy(y_hbm.at[pl.ds(0, FFN_GROUP * tile)], xbuf.at[q % FFN_RING],
                                      rsem.at[q % FFN_RING]).wait()

        @pl.when(size < FFN_GROUP)
        def _():
            for j in range(FFN_GROUP - 1):
                pl.when(j < size)(functools.partial(one, j))

    def writeback(q, start):
        most = FFN_TILE // RUN_ALIGN - 1

        def blocks(j, b, n):
            dst = pl.multiple_of(to_ref[gs_ref[q] + j] * block, block) + b * block
            go(pltpu.make_async_copy(ybuf.at[q % FFN_RING, pl.ds(j * tile + b * block, n * block)],
                                     y_hbm.at[pl.ds(dst, n * block)], wsem.at[q % FFN_RING]), start)

        for j in range(FFN_GROUP):
            item = gs_ref[q] + j
            n = jnp.where(item < gs_ref[q + 1], tn_ref[item], 0)
            pl.when(n >= most)(functools.partial(blocks, j, 0, most))
            pl.when(n > most)(functools.partial(blocks, j, most, 1))

            @pl.when((n > 0) & (n < most))
            def _():
                for b in range(most - 1):
                    pl.when(b < n)(functools.partial(blocks, j, b, 1))

    @pl.when(e == 0)
    def _():
        xbuf[...] = jnp.zeros_like(xbuf)
        for d in range(FFN_RING - 1):
            pl.when(d < n_groups)(functools.partial(fetch, d, True))

    w13_s[:, 0:D_EXPERT] = w1_ref[0].astype(BF16)
    w13_s[:, D_EXPERT:] = w3_ref[0].astype(BF16)
    w2_s[...] = w2_ref[0].astype(BF16)

    xs = hs_ref[...]

    @pl.when(e == 0)
    def _():
        sh = _silu(_dot(xs, ws1_ref[...].astype(BF16))) * _dot(xs, ws3_ref[...].astype(BF16))
        acc_s[...] = _dot(sh.astype(BF16), ws2_ref[...].astype(BF16))

    lane = lax.broadcasted_iota(jnp.int32, wds_ref.shape, 1)
    wcol = jnp.sum(jnp.where(lane == e, wds_ref[...], 0.0), axis=1, keepdims=True)
    hd = _dot(xs, w13_s[...])
    acc_s[...] += _dot((_silu(hd[:, :D_EXPERT]) * hd[:, D_EXPERT:] * wcol).astype(BF16), w2_s[...])

    @pl.when(e == pl.num_programs(0) - 1)
    def _():
        ys_ref[...] = x1s_ref[...] + gt2s_ref[...] * _rms(acc_s[...], gpost_ref[...])

    def group_body(q, carry):
        @pl.when(q + FFN_RING - 1 < n_groups)
        def _():
            fetch(q + FFN_RING - 1, True)

        @pl.when(q >= FFN_RING)
        def _():
            writeback(q - FFN_RING, False)

        fetch(q, False)
        rows = FFN_GROUP * FFN_TILE // FFN_SPLIT
        for part in range(FFN_SPLIT):
            span = pl.ds(part * rows * ROW_SUB, rows * ROW_SUB)
            x = _unpack_rows(xbuf.at[q % FFN_RING, span], rows).astype(BF16)
            h = _dot(x, w13_s[...])
            act = _silu(h[:, :D_EXPERT]) * h[:, D_EXPERT:]
            y = _dot(act.astype(BF16), w2_s[...])
            out = ybuf.at[q % FFN_RING, span]
            for j in range(ROW_SUB):
                out[pl.ds(j, rows, stride=ROW_SUB), :] = _pack_pair(
                    y[:, j * LANES:(j + 1) * LANES], y[:, ROW_WORDS + j * LANES:ROW_WORDS + (j + 1) * LANES])
        writeback(q, True)
        return carry

    lax.fori_loop(gq_ref[e], gq_ref[e + 1], group_body, 0)

    @pl.when(e == pl.num_programs(0) - 1)
    def _():
        for d in range(FFN_RING):
            @pl.when(n_groups - 1 - d >= 0)
            def _():
                writeback(n_groups - 1 - d, False)


def _expert_ffn(group_of_expert, group_start, item_off, item_blocks, x_slots, w1, w3, w2,
                h2_s, wd_s, x1_s, gt2_s, gpost, ws1, ws3, ws2):
    of_expert = lambda e, *_: (e, 0, 0)
    whole = lambda a: pl.BlockSpec(a.shape, lambda e, *_: (0,) * a.ndim)
    tiles = pltpu.VMEM((FFN_RING, FFN_GROUP * FFN_TILE * ROW_SUB, LANES), jnp.uint32)
    sems = pltpu.SemaphoreType.DMA((FFN_RING,))
    dense = (h2_s, wd_s, x1_s, gt2_s, gpost, ws1, ws3, ws2)
    grid_spec = pltpu.PrefetchScalarGridSpec(
        num_scalar_prefetch=4,
        grid=(N_EXPERTS,),
        in_specs=[pl.BlockSpec(memory_space=pl.ANY),
                  pl.BlockSpec((1, D_MODEL, D_EXPERT), of_expert),
                  pl.BlockSpec((1, D_MODEL, D_EXPERT), of_expert),
                  pl.BlockSpec((1, D_EXPERT, D_MODEL), of_expert)] + [whole(a) for a in dense],
        out_specs=(pl.BlockSpec(memory_space=pl.ANY), whole(x1_s)),
        scratch_shapes=[pltpu.VMEM((D_MODEL, 2 * D_EXPERT), BF16), pltpu.VMEM((D_EXPERT, D_MODEL), BF16),
                        tiles, tiles, sems, sems, pltpu.VMEM(x1_s.shape, F32)])
    return pl.pallas_call(
        _ffn_kernel,
        out_shape=(jax.ShapeDtypeStruct(x_slots.shape, jnp.uint32), jax.ShapeDtypeStruct(x1_s.shape, F32)),
        grid_spec=grid_spec,
        input_output_aliases={4: 0},
        compiler_params=pltpu.CompilerParams(dimension_semantics=("arbitrary",), vmem_limit_bytes=VMEM_LIMIT),
        name="expert_ffn",
    )(group_of_expert, group_start, item_off, item_blocks, x_slots, w1, w3, w2, *dense)


def _work_items(cnt, loff):
    nw = cnt.shape[0]
    cnt = cnt.astype(jnp.int32).T.reshape(-1)
    loff = loff.astype(jnp.int32).T.reshape(-1)
    region = jnp.tile(jnp.arange(nw, dtype=jnp.int32) * REGION_ROWS, N_EXPERTS)
    n_items = nw * N_EXPERTS + (nw * PLAN_WINDOW * TOP_K) // FFN_TILE
    tiles = (cnt + FFN_TILE - 1) // FFN_TILE
    ends = jnp.cumsum(tiles)
    starts = ends - tiles
    i = jnp.arange(n_items + FFN_GROUP, dtype=jnp.int32)[:, None]
    mine = (starts[None, :] <= i) & (i < ends[None, :])
    of_pair = lambda v: jnp.sum(jnp.where(mine, v[None, :], 0), axis=1)
    r = i[:, 0] - of_pair(starts)
    row = of_pair(region + loff) + r * FFN_TILE
    own = jnp.clip(of_pair(cnt) - r * FFN_TILE, 0, FFN_TILE)

    zero = jnp.zeros((1,), jnp.int32)
    item_end = ends.reshape(N_EXPERTS, nw)[:, -1]
    item_start = jnp.concatenate([zero, item_end[:-1]])
    group_end = jnp.cumsum((item_end - item_start + FFN_GROUP - 1) // FFN_GROUP)
    group_first = jnp.concatenate([zero, group_end[:-1]])
    q = jnp.arange(N_EXPERTS + n_items // FFN_GROUP + 1, dtype=jnp.int32)[:, None]
    has = (group_first[None, :] <= q) & (q < group_end[None, :])
    of_expert = lambda v: jnp.sum(jnp.where(has, v[None, :], 0), axis=1)
    group_start = jnp.where(q[:, 0] < group_end[-1],
                            of_expert(item_start) + (q[:, 0] - of_expert(group_first)) * FFN_GROUP, item_end[-1])
    return (jnp.concatenate([zero, group_end]), group_start, row // RUN_ALIGN, (own + RUN_ALIGN - 1) // RUN_ALIGN)


def _combine_kernel(dest_ref, wsel_ref, y_hbm, h_ref, x1_ref, gt2_ref, gpost_ref, ws13_ref, ws2_ref, out_ref,
                    dest_s, wsel_s, lo_s, hi_s, ybuf, sem, ysem):
    w, s = pl.program_id(0), pl.program_id(1)
    region = REGION_ROWS * ROW_SUB
    part = region // REGION_PARTS

    def region_copies(win):
        return [pltpu.make_async_copy(y_hbm.at[pl.ds(win * region + p * part, part)],
                                      ybuf.at[win % 2, pl.ds(p * part, part)], ysem.at[win % 2, p])
                for p in range(REGION_PARTS)]

    @pl.when(s == 0)
    def _():
        copies = _table_copies(dest_ref, dest_s, sem.at[0]) + _table_copies(wsel_ref, wsel_s, sem.at[1])
        for c in copies:
            c.start()

        @pl.when(w == 0)
        def _():
            for c in region_copies(w):
                c.start()

        @pl.when(w + 1 < pl.num_programs(0))
        def _():
            for c in region_copies(w + 1):
                c.start()

        for c in copies + region_copies(w):
            c.wait()

    y_ref = ybuf.at[w % 2]

    base = s * COMBINE_TILE

    def body(i, carry):
        for u in range(COMBINE_UNROLL):
            t = i * COMBINE_UNROLL + u
            acc_lo = jnp.zeros((ROW_SUB, LANES), F32)
            acc_hi = jnp.zeros((ROW_SUB, LANES), F32)
            for k in range(TOP_K):
                at = k * PLAN_WINDOW + base + t
                lo, hi = _unpack_pair(y_ref[_rows_at(dest_s[at]), :])
                w = wsel_s[at]
                acc_lo = acc_lo + w * lo
                acc_hi = acc_hi + w * hi
            lo_s[_slot_rows(t), :] = acc_lo
            hi_s[_slot_rows(t), :] = acc_hi
        return carry

    lax.fori_loop(0, COMBINE_TILE // COMBINE_UNROLL, body, 0)
    rows = lambda ref: [ref[pl.ds(j, COMBINE_TILE, stride=ROW_SUB), :] for j in range(ROW_SUB)]
    routed = jnp.concatenate(rows(lo_s) + rows(hi_s), axis=1)
    h_lo, h_hi = _unpack_pair(h_ref[...])
    x = jnp.concatenate([h_lo, h_hi], axis=1).astype(BF16)
    hs = _dot(x, ws13_ref[...])
    act = _silu(hs[:, :D_EXPERT]) * hs[:, D_EXPERT:]
    ff = routed + _dot(act.astype(BF16), ws2_ref[...])
    out_ref[...] = x1_ref[...] + gt2_ref[0] * _rms(ff, gpost_ref[...])


def _combine(dest_w, wsel_w, y_slots, h2, x1, gt2, tokens_per_gt2, gpost, ws13, ws2):
    nw = dest_w.shape[0]
    sub = PLAN_WINDOW // COMBINE_TILE
    plan = pl.BlockSpec((1, TOP_K, PLAN_WINDOW), lambda w, s: (w, 0, 0))
    tok = lambda w, s: (w * sub + s, 0)
    return pl.pallas_call(
        _combine_kernel,
        out_shape=jax.ShapeDtypeStruct(x1.shape, F32),
        grid=(nw, sub),
        in_specs=[plan, plan,
                  pl.BlockSpec(memory_space=pl.ANY),
                  pl.BlockSpec((COMBINE_TILE, ROW_WORDS), tok),
                  pl.BlockSpec((COMBINE_TILE, D_MODEL), tok),
                  pl.BlockSpec((1, 1, D_MODEL), lambda w, s: ((w * PLAN_WINDOW) // tokens_per_gt2, 0, 0)),
                  _full((1, D_MODEL)), _full(ws13.shape), _full(ws2.shape)],
        out_specs=pl.BlockSpec((COMBINE_TILE, D_MODEL), tok),
        scratch_shapes=[pltpu.SMEM((TOP_K * PLAN_WINDOW,), jnp.int32),
                        pltpu.SMEM((TOP_K * PLAN_WINDOW,), F32),
                        pltpu.VMEM((COMBINE_TILE * ROW_SUB, LANES), F32),
                        pltpu.VMEM((COMBINE_TILE * ROW_SUB, LANES), F32),
                        pltpu.VMEM((2, REGION_ROWS * ROW_SUB, LANES), jnp.uint32),
                        pltpu.SemaphoreType.DMA((2,)), pltpu.SemaphoreType.DMA((2, REGION_PARTS))],
        compiler_params=pltpu.CompilerParams(dimension_semantics=("arbitrary", "arbitrary"),
                                             vmem_limit_bytes=VMEM_LIMIT),
        name="combine",
    )(dest_w, wsel_w, y_slots, h2, x1, gt2, gpost, ws13, ws2)


def kernel(x_prompt, x_sample, c_prompt, c_sample, state_pool, state_hgrn, w_ada, b_ada, g_pre_mix, g_post_mix,
           w_in, w_pool, pool_scale, lb_logits, g_out_norm, w_out, g_pre_ffn, g_post_ffn, w_router, router_bias,
           w_exp_gate, w_exp_up, w_exp_down, w_sh_gate, w_sh_up, w_sh_down):
    assert w_ada.shape[0] == 1 and lb_logits.shape[0] == 2, "single-layer trunk"
    bp, lp, d = x_prompt.shape
    bs = x_sample.shape[0]
    row = lambda a: a[0].reshape(1, -1)

    n_mod = bp + bs
    pad = (-n_mod) % 16
    c_all = jnp.concatenate([c_prompt, c_sample, jnp.zeros((pad, d), F32)], axis=0)
    mod = _ada(c_all, w_ada[0], b_ada)
    mod_p = mod[:bp].reshape(bp, 1, 6 * d)
    mod_s = mod[bp:n_mod]

    win = w_in[0].astype(BF16)
    wout = w_out[0].astype(BF16)
    wpool = w_pool[0].astype(BF16)
    wr = jnp.pad(w_router[0], ((0, 0), (0, LANES - N_EXPERTS)))
    wrh = wr.astype(BF16)
    wrl = (wr - wrh.astype(F32)).astype(BF16)
    m2 = jnp.asarray(_M2_NP, BF16)
    gpre, gpost, gffn, gffn_post = row(g_pre_mix), row(g_post_mix), row(g_pre_ffn), row(g_post_ffn)
    pscale, gout = row(pool_scale), row(g_out_norm)
    bias_col = jnp.pad(router_bias[0], (0, LANES - N_EXPERTS)).reshape(LANES, 1)

    x1_p, h2_p, lg_p, pool_p, st_p = _mix_prompt(x_prompt, mod_p, gpre, gpost, gffn, win, wpool, pscale,
                                                  lb_logits, gout, wout, wrh, wrl, m2)
    x1_s, h2_s, lg_s, pool_s, st_s = _mix_sample(x_sample[:, 0, :], mod_s, state_pool[0], state_hgrn[0],
                                                  gpre, gpost, gffn, win, wpool, pscale, lb_logits, gout, wout,
                                                  wrh, wrl)

    experts = (w_exp_gate[0], w_exp_up[0], w_exp_down[0], w_sh_gate[0], w_sh_up[0], w_sh_down[0])
    tp = bp * lp
    nw = tp // PLAN_WINDOW
    assert lp % PLAN_WINDOW == 0
    upper = jnp.asarray(np.triu(np.ones((PLAN_WINDOW, PLAN_WINDOW), np.float32), 1), BF16)
    dest, wsel, cnt, loff = _plan(lg_p.reshape(tp, LANES), bias_col, upper)
    dest_w, wsel_w = dest, wsel
    h2_rows = h2_p.reshape(tp, ROW_WORDS)
    x_slots = _dispatch(dest_w, h2_rows)
    wd_s = _router(lg_s, bias_col)
    y_slots, y_s = _expert_ffn(*_work_items(cnt[:, :, 0], loff[:, :, 0]), x_slots, *experts[:3],
                               h2_s, wd_s, x1_s, mod_s[:, 5 * d:], gffn_post, *experts[3:])
    ws13 = jnp.concatenate([experts[3], experts[4]], axis=1).astype(BF16)
    gt2_p = mod_p[:, :, 5 * d:]
    y_p = _combine(dest_w, wsel_w, y_slots, h2_rows, x1_p.reshape(tp, d), gt2_p, lp, gffn_post,
                   ws13, experts[5].astype(BF16))

    return (y_p.reshape(bp, lp, d), y_s.reshape(bs, 1, d), pool_p[None], st_p[None], pool_s[None], st_s[None])
```

```python
import functools

import numpy as np
import jax
import jax.numpy as jnp
from jax import lax
from jax.experimental import pallas as pl
from jax.experimental.pallas import tpu as pltpu

F32 = jnp.float32
BF16 = jnp.bfloat16

D_MODEL = 1024
POOL_WIDTH = 512
POOL_WINDOWS = (2, 4, 8, 16)
POOL_GROUP = 128
POOL_HIST = 15
HG_WIDTH = 512
HG_HEADS = 4
HG_HEAD_DIM = 128
IN_WIDTH = POOL_WIDTH + 4 * HG_WIDTH
N_EXPERTS = 64
TOP_K = 8
N_GROUPS = 8
TOPK_GROUPS = 4
GROUP_SIZE = N_EXPERTS // N_GROUPS
D_EXPERT = 256
ROUTED_SCALE = 2.5
EPS = 1e-6

LANES = 128
CHUNK = 64
TIME_TILE = 256
ADA_TILE_N = 1024
ROUTER_TILE = 512
SAMPLE_STATE_BLOCK = 16
PLAN_WINDOW = 1024
RUN_ALIGN = 32
FFN_TILE = 160
REGION_ROWS = PLAN_WINDOW * TOP_K + N_EXPERTS * (RUN_ALIGN - 1) + FFN_TILE
ROW_WORDS = D_MODEL // 2
ROW_SUB = ROW_WORDS // LANES
COMBINE_TILE = 256
REGION_PARTS = 4
SCATTER_UNROLL = 4
COMBINE_UNROLL = 16
FFN_GROUP = 4
FFN_RING = 6
VMEM_LIMIT = 58 * 1024 * 1024

_LEVELS = (64, 32, 16, 8, 4, 2)
_BLK_CUM = 0


def _level_blocks():
    c = CHUNK
    i = np.arange(c)[:, None]
    s = np.arange(c)[None, :]
    blocks = [(s <= i)]
    index = {}
    for lvl in _LEVELS:
        ref = (i // lvl) * lvl + lvl // 2 - 1
        index[lvl] = len(blocks)
        blocks.append(s <= ref)
    m = np.concatenate(blocks, axis=0).astype(np.float32)
    return np.concatenate([m, m], axis=1), index


_M2_NP, _LEVEL_INDEX = _level_blocks()


def _dot(a, b):
    return jnp.dot(a, b, preferred_element_type=F32)


def _dot_nt(a, b):
    return lax.dot_general(a, b, (((1,), (1,)), ((), ())), preferred_element_type=F32)


def _split2(x):
    hi = x.astype(BF16)
    lo = (x - hi.astype(F32)).astype(BF16)
    return hi, lo


def _pack_pair(a, b):
    lo = lax.bitcast_convert_type(a.astype(BF16).astype(F32), jnp.uint32)
    hi = lax.bitcast_convert_type(b.astype(BF16).astype(F32), jnp.uint32)
    return (lo >> 16) | hi


def _unpack_pair(words):
    lo = lax.bitcast_convert_type(words << 16, F32)
    hi = lax.bitcast_convert_type(words & jnp.uint32(0xFFFF0000), F32)
    return lo, hi


def _silu(x):
    return x * jax.nn.sigmoid(x)


def _rms(x, g):
    return x * lax.rsqrt(jnp.mean(x * x, axis=-1, keepdims=True) + EPS) * g


def _mods(m):
    return [m[:, j * D_MODEL:(j + 1) * D_MODEL] for j in range(6)]


def _forget_lower_bound(lbl):
    mx = jnp.max(lbl, axis=0, keepdims=True)
    e = jnp.exp(lbl - mx)
    return e[0:1] / jnp.sum(e, axis=0, keepdims=True)


def _router_logits(h2, wrh, wrl):
    hi, lo = _split2(h2)
    return _dot(hi, wrh) + _dot(lo, wrh) + _dot(hi, wrl)


def _ada_kernel(c_ref, w_ref, b_ref, o_ref):
    a_hi, a_lo = _split2(_silu(c_ref[...]))
    w_hi, w_lo = _split2(w_ref[...])
    o_ref[...] = _dot(a_hi, w_hi) + _dot(a_lo, w_hi) + _dot(a_hi, w_lo) + b_ref[...]


def _ada(c_all, w_ada, b_ada):
    rows = c_all.shape[0]
    n = w_ada.shape[1]
    return pl.pallas_call(
        _ada_kernel,
        out_shape=jax.ShapeDtypeStruct((rows, n), F32),
        grid=(n // ADA_TILE_N,),
        in_specs=[pl.BlockSpec((rows, D_MODEL), lambda j: (0, 0)),
                  pl.BlockSpec((D_MODEL, ADA_TILE_N), lambda j: (0, j)),
                  pl.BlockSpec((1, ADA_TILE_N), lambda j: (0, j))],
        out_specs=pl.BlockSpec((rows, ADA_TILE_N), lambda j: (0, j)),
        compiler_params=pltpu.CompilerParams(dimension_semantics=("parallel",),
                                             vmem_limit_bytes=VMEM_LIMIT),
        name="ada",
    )(c_all, w_ada, b_ada)


def _level_masks():
    i = lax.broadcasted_iota(jnp.int32, (CHUNK, 2 * CHUNK), 0)
    j = lax.broadcasted_iota(jnp.int32, (CHUNK, 2 * CHUNK), 1) & (CHUNK - 1)
    masks = {}
    for lvl in _LEVELS:
        sh = lvl.bit_length() - 1
        same = (i >> sh) == (j >> sh)
        upper = ((i >> (sh - 1)) & 1) == 1
        lower = ((j >> (sh - 1)) & 1) == 0
        masks[lvl] = same & upper & lower
    return masks, i == j


def _by_head(m):
    z = jnp.zeros((m.shape[0], HG_HEAD_DIM), m.dtype)
    return jnp.concatenate([jnp.concatenate([m[:, :HG_HEAD_DIM], z], axis=1),
                            jnp.concatenate([z, m[:, HG_HEAD_DIM:]], axis=1)], axis=0)


def _hgrn_chunk_pair(q, k, v, e_all, lane0, sts, masks, eye):
    hd = HG_HEAD_DIM

    def blk(n):
        return e_all[n * CHUNK:(n + 1) * CHUNK, lane0:lane0 + 2 * hd]

    b = blk(_BLK_CUM)
    kb = k.astype(BF16)
    a = jnp.where(eye, _dot_nt(q.astype(BF16), _by_head(kb)), 0.0)
    for lvl in _LEVELS:
        d = b - blk(_LEVEL_INDEX[lvl])
        ql = (q * jnp.exp(jnp.minimum(d, 0.0))).astype(BF16)
        kl = (k * jnp.exp(jnp.minimum(-d, 0.0))).astype(BF16) if lvl > 2 else kb
        a = a + jnp.where(masks[lvl], _dot_nt(ql, _by_head(kl)), 0.0)
    st_both = jnp.concatenate([jnp.concatenate([sts[0], jnp.zeros_like(sts[0])], axis=1),
                               jnp.concatenate([jnp.zeros_like(sts[1]), sts[1]], axis=1)], axis=0).astype(BF16)
    o = _dot(a.astype(BF16), _by_head(v.astype(BF16))) + _dot_nt((q * jnp.exp(b)).astype(BF16), st_both)
    b_last = b[CHUNK - 1:CHUNK, :]
    k_end = (k * jnp.exp(b_last - b)).astype(BF16)
    decay = jnp.exp(b_last)
    new = []
    for h in range(2):
        lanes = slice(h * hd, (h + 1) * hd)
        new.append(sts[h] * decay[:, lanes] + _dot(v[:, lanes].T.astype(BF16), k_end[:, lanes]))
    return o, new


def _mix_prompt_kernel(x_ref, mod_ref, gpre_ref, gpost_ref, gffn_ref, win_ref, wpool_ref, pscale_ref,
                       lbl_ref, gout_ref, wout_ref, wrh_ref, wrl_ref, m2_ref,
                       x1_ref, h2_ref, lg_ref, pool_ref, st_ref,
                       st_s, ubuf, q_s, k_s, v_s, g_s, o_s):
    t = pl.program_id(1)
    n_t = pl.num_programs(1)
    tt = TIME_TILE

    @pl.when(t == 0)
    def _():
        st_s[...] = jnp.zeros_like(st_s)
        ubuf[0:16, :] = jnp.zeros((16, POOL_WIDTH), F32)

    xt = x_ref[0]
    sh1, sc1, gt1, sh2, sc2, gt2 = _mods(mod_ref[0])
    h = _rms(xt, gpre_ref[...]) * (1.0 + sc1) + sh1
    proj = _dot(h.astype(BF16), win_ref[...])

    u = proj[:, :POOL_WIDTH]
    ubuf[16:16 + tt, :] = u
    pos = (t * tt + lax.broadcasted_iota(jnp.int32, (tt, 1), 0) + 1).astype(F32)
    ys = []
    for g, w in enumerate(POOL_WINDOWS):
        s = ubuf[:, g * POOL_GROUP:(g + 1) * POOL_GROUP]
        off = 0
        for step in range(w.bit_length() - 1):
            sh = 1 << step
            s = s[sh:, :] + s[:-sh, :]
            off += sh
        ws = s[16 - off:16 - off + tt, :]
        cnt = jnp.minimum(pos, float(w))
        d = ws / cnt - u[:, g * POOL_GROUP:(g + 1) * POOL_GROUP]
        ys.append(_dot(d.astype(BF16), wpool_ref[g]))
    y_pool = jnp.concatenate(ys, axis=1) * pscale_ref[...]
    ubuf[0:16, :] = ubuf[tt:tt + 16, :]

    @pl.when(t == n_t - 1)
    def _():
        pool_ref[0] = u[tt - POOL_HIST:, :]

    lb = _forget_lower_bound(lbl_ref[...])
    f = lb + (1.0 - lb) * jax.nn.sigmoid(proj[:, POOL_WIDTH + HG_WIDTH:POOL_WIDTH + 2 * HG_WIDTH])
    q_s[...] = _silu(proj[:, POOL_WIDTH:POOL_WIDTH + HG_WIDTH])
    k_s[...] = 1.0 - f
    v_s[...] = proj[:, POOL_WIDTH + 2 * HG_WIDTH:POOL_WIDTH + 3 * HG_WIDTH]
    g_s[...] = jnp.log(f)
    gate = _silu(proj[:, POOL_WIDTH + 3 * HG_WIDTH:])

    masks, eye = _level_masks()
    m2 = m2_ref[...]

    states = [st_s[hd] for hd in range(HG_HEADS)]
    for c in range(tt // CHUNK):
        rows = slice(c * CHUNK, (c + 1) * CHUNK)
        g_hi, g_lo = _split2(g_s[rows, :])
        e_all = _dot(m2, jnp.concatenate([g_hi, g_lo], axis=0))
        for hd in range(0, HG_HEADS, 2):
            lane0 = hd * HG_HEAD_DIM
            lanes = slice(lane0, lane0 + 2 * HG_HEAD_DIM)
            o, states[hd:hd + 2] = _hgrn_chunk_pair(q_s[rows, lanes], k_s[rows, lanes], v_s[rows, lanes], e_all,
                                                    lane0, states[hd:hd + 2], masks, eye)
            o_s[rows, lanes] = o
    for hd in range(HG_HEADS):
        st_s[hd] = states[hd]

    o = o_s[...]
    os_ = []
    for hd in range(HG_HEADS):
        oh = o[:, hd * HG_HEAD_DIM:(hd + 1) * HG_HEAD_DIM]
        os_.append(oh * lax.rsqrt(jnp.mean(oh * oh, axis=-1, keepdims=True) + EPS) * gout_ref[...])
    o_n = jnp.concatenate(os_, axis=1) * gate

    mix = _dot(y_pool.astype(BF16), wout_ref[0:POOL_WIDTH, :]) + _dot(o_n.astype(BF16), wout_ref[POOL_WIDTH:, :])
    x1 = xt + gt1 * _rms(mix, gpost_ref[...])
    h2 = _rms(x1, gffn_ref[...]) * (1.0 + sc2) + sh2
    x1_ref[0] = x1
    h2_ref[0] = _pack_pair(h2[:, :ROW_WORDS], h2[:, ROW_WORDS:])
    lg_ref[0] = _router_logits(h2, wrh_ref[...], wrl_ref[...])

    @pl.when(t == n_t - 1)
    def _():
        for hd in range(HG_HEADS):
            st_ref[0, hd] = st_s[hd].T


def _full(shape):
    nd = len(shape)
    return pl.BlockSpec(shape, lambda *_: (0,) * nd)


def _mix_prompt(x, mod, gpre, gpost, gffn, win, wpool, pscale, lbl, gout, wout, wrh, wrl, m2):
    b, l, d = x.shape
    tt = TIME_TILE
    tile = lambda i, j: (i, j, 0)
    per_b = lambda i, j: (i, 0, 0)
    return pl.pallas_call(
        _mix_prompt_kernel,
        out_shape=(jax.ShapeDtypeStruct((b, l, d), F32),
                   jax.ShapeDtypeStruct((b, l, ROW_WORDS), jnp.uint32),
                   jax.ShapeDtypeStruct((b, l, LANES), F32),
                   jax.ShapeDtypeStruct((b, POOL_HIST, POOL_WIDTH), F32),
                   jax.ShapeDtypeStruct((b, HG_HEADS, HG_HEAD_DIM, HG_HEAD_DIM), F32)),
        grid=(b, l // tt),
        in_specs=[pl.BlockSpec((1, tt, d), tile),
                  pl.BlockSpec((1, 1, 6 * d), per_b),
                  _full((1, d)), _full((1, d)), _full((1, d)),
                  _full(win.shape), _full(wpool.shape), _full((1, POOL_WIDTH)),
                  _full(lbl.shape), _full((1, HG_HEAD_DIM)), _full(wout.shape),
                  _full(wrh.shape), _full(wrl.shape), _full(m2.shape)],
        out_specs=(pl.BlockSpec((1, tt, d), tile),
                   pl.BlockSpec((1, tt, ROW_WORDS), tile),
                   pl.BlockSpec((1, tt, LANES), tile),
                   pl.BlockSpec((1, POOL_HIST, POOL_WIDTH), per_b),
                   pl.BlockSpec((1, HG_HEADS, HG_HEAD_DIM, HG_HEAD_DIM), lambda i, j: (i, 0, 0, 0))),
        scratch_shapes=[pltpu.VMEM((HG_HEADS, HG_HEAD_DIM, HG_HEAD_DIM), F32),
                        pltpu.VMEM((tt + 16, POOL_WIDTH), F32),
                        pltpu.VMEM((tt, HG_WIDTH), F32), pltpu.VMEM((tt, HG_WIDTH), F32),
                        pltpu.VMEM((tt, HG_WIDTH), F32), pltpu.VMEM((tt, HG_WIDTH), F32),
                        pltpu.VMEM((tt, HG_WIDTH), F32)],
        compiler_params=pltpu.CompilerParams(dimension_semantics=("parallel", "arbitrary"),
                                             vmem_limit_bytes=VMEM_LIMIT),
        name="mix_prompt",
    )(x, mod, gpre, gpost, gffn, win, wpool, pscale, lbl, gout, wout, wrh, wrl, m2)


def _mix_sample_in_kernel(x_ref, mod_ref, gpre_ref, win_ref, wpool_ref, pscale_ref, lbl_ref, hist_ref,
                          ypool_ref, npool_ref, ft_ref, qt_ref, v_ref, gate_ref):
    xt = x_ref[...]
    sh1, sc1 = _mods(mod_ref[...])[:2]
    h = _rms(xt, gpre_ref[...]) * (1.0 + sc1) + sh1
    proj = _dot(h.astype(BF16), win_ref[...])
    u = proj[:, :POOL_WIDTH]
    row = lax.broadcasted_iota(jnp.int32, (hist_ref.shape[0], POOL_HIST, POOL_GROUP), 1)
    ys = []
    for g, w in enumerate(POOL_WINDOWS):
        sl = slice(g * POOL_GROUP, (g + 1) * POOL_GROUP)
        past = jnp.sum(jnp.where(row >= POOL_HIST - (w - 1), hist_ref[:, :, sl], 0.0), axis=1)
        ug = u[:, sl]
        d = (past + ug) / float(w) - ug
        ys.append(_dot(d.astype(BF16), wpool_ref[g]))
    ypool_ref[...] = jnp.concatenate(ys, axis=1) * pscale_ref[...]
    npool_ref[:, 0:POOL_HIST - 1, :] = hist_ref[:, 1:POOL_HIST, :]
    npool_ref[:, POOL_HIST - 1, :] = u

    lb = _forget_lower_bound(lbl_ref[...])
    f = lb + (1.0 - lb) * jax.nn.sigmoid(proj[:, POOL_WIDTH + HG_WIDTH:POOL_WIDTH + 2 * HG_WIDTH])
    ft_ref[...] = f.T
    qt_ref[...] = _silu(proj[:, POOL_WIDTH:POOL_WIDTH + HG_WIDTH]).T
    v_ref[...] = proj[:, POOL_WIDTH + 2 * HG_WIDTH:POOL_WIDTH + 3 * HG_WIDTH]
    gate_ref[...] = _silu(proj[:, POOL_WIDTH + 3 * HG_WIDTH:])


def _mix_sample_state_kernel(s_ref, ft_ref, qt_ref, v_ref, snew_ref, o_ref):
    i = pl.program_id(0)
    lane = lax.broadcasted_iota(jnp.int32, (HG_HEAD_DIM, ft_ref.shape[1]), 1)
    for j in range(SAMPLE_STATE_BLOCK):
        mine = lane == i * SAMPLE_STATE_BLOCK + j
        for hd in range(HG_HEADS):
            r0 = hd * HG_HEAD_DIM
            f = jnp.sum(jnp.where(mine, ft_ref[r0:r0 + HG_HEAD_DIM, :], 0.0), axis=1, keepdims=True)
            q = jnp.sum(jnp.where(mine, qt_ref[r0:r0 + HG_HEAD_DIM, :], 0.0), axis=1, keepdims=True)
            v = v_ref[j:j + 1, r0:r0 + HG_HEAD_DIM]
            s_new = f * s_ref[j, hd] + (1.0 - f) * v
            snew_ref[j, hd] = s_new
            o_ref[j:j + 1, r0:r0 + HG_HEAD_DIM] = jnp.sum(q * s_new, axis=0, keepdims=True)


def _mix_sample_out_kernel(x_ref, mod_ref, o_ref, gate_ref, ypool_ref, gout_ref, wout_ref, gpost_ref,
                           gffn_ref, wrh_ref, wrl_ref, x1_ref, h2_ref, lg_ref):
    _, _, gt1, sh2, sc2, _ = _mods(mod_ref[...])
    o = o_ref[...]
    os_ = []
    for hd in range(HG_HEADS):
        oh = o[:, hd * HG_HEAD_DIM:(hd + 1) * HG_HEAD_DIM]
        os_.append(oh * lax.rsqrt(jnp.mean(oh * oh, axis=-1, keepdims=True) + EPS) * gout_ref[...])
    o_n = jnp.concatenate(os_, axis=1) * gate_ref[...]
    mix = (_dot(ypool_ref[...].astype(BF16), wout_ref[0:POOL_WIDTH, :])
           + _dot(o_n.astype(BF16), wout_ref[POOL_WIDTH:, :]))
    x1 = x_ref[...] + gt1 * _rms(mix, gpost_ref[...])
    h2 = _rms(x1, gffn_ref[...]) * (1.0 + sc2) + sh2
    x1_ref[...] = x1
    h2_ref[...] = h2.astype(BF16)
    lg_ref[...] = _router_logits(h2, wrh_ref[...], wrl_ref[...])


def _mix_sample(x, mod, hist, state, gpre, gpost, gffn, win, wpool, pscale, lbl, gout, wout, wrh, wrl):
    b = x.shape[0]
    bb = SAMPLE_STATE_BLOCK
    cp = pltpu.CompilerParams(vmem_limit_bytes=VMEM_LIMIT)
    ypool, npool, ft, qt, v, gate = pl.pallas_call(
        _mix_sample_in_kernel,
        out_shape=(jax.ShapeDtypeStruct((b, POOL_WIDTH), F32),
                   jax.ShapeDtypeStruct((b, POOL_HIST, POOL_WIDTH), F32),
                   jax.ShapeDtypeStruct((HG_WIDTH, b), F32),
                   jax.ShapeDtypeStruct((HG_WIDTH, b), F32),
                   jax.ShapeDtypeStruct((b, HG_WIDTH), F32),
                   jax.ShapeDtypeStruct((b, HG_WIDTH), F32)),
        compiler_params=cp,
        name="mix_sample_in",
    )(x, mod, gpre, win, wpool, pscale, lbl, hist)

    s_spec = pl.BlockSpec((bb, HG_HEADS, HG_HEAD_DIM, HG_HEAD_DIM), lambda i: (i, 0, 0, 0))
    col_spec = _full((HG_WIDTH, b))
    row_spec = pl.BlockSpec((bb, HG_WIDTH), lambda i: (i, 0))
    s_new, o = pl.pallas_call(
        _mix_sample_state_kernel,
        out_shape=(jax.ShapeDtypeStruct(state.shape, F32), jax.ShapeDtypeStruct((b, HG_WIDTH), F32)),
        grid=(b // bb,),
        in_specs=[s_spec, col_spec, col_spec, row_spec],
        out_specs=(s_spec, row_spec),
        compiler_params=pltpu.CompilerParams(dimension_semantics=("parallel",), vmem_limit_bytes=VMEM_LIMIT),
        name="mix_sample_state",
    )(state, ft, qt, v)

    x1, h2, lg = pl.pallas_call(
        _mix_sample_out_kernel,
        out_shape=(jax.ShapeDtypeStruct((b, D_MODEL), F32),
                   jax.ShapeDtypeStruct((b, D_MODEL), BF16),
                   jax.ShapeDtypeStruct((b, LANES), F32)),
        compiler_params=cp,
        name="mix_sample_out",
    )(x, mod, o, gate, ypool, gout, wout, gpost, gffn, wrh, wrl)
    return x1, h2, lg, npool, s_new


def _first_max(cur, idx, big):
    m = jnp.max(cur, axis=0, keepdims=True)
    first = jnp.min(jnp.where(cur == m, idx, big), axis=0, keepdims=True)
    return idx == first


def _route(logit, bias):
    n = logit.shape[1]
    scores = jax.nn.sigmoid(logit)
    sel = scores + bias
    neg = -jnp.inf

    sub = lax.broadcasted_iota(jnp.int32, (GROUP_SIZE, n), 0)
    gscore = []
    for g in range(N_GROUPS):
        sg = sel[g * GROUP_SIZE:(g + 1) * GROUP_SIZE, :]
        m1 = jnp.max(sg, axis=0, keepdims=True)
        rest = jnp.where(_first_max(sg, sub, GROUP_SIZE), neg, sg)
        gscore.append(m1 + jnp.max(rest, axis=0, keepdims=True))
    cur = jnp.concatenate(gscore, axis=0)
    gidx = lax.broadcasted_iota(jnp.int32, (N_GROUPS, n), 0)
    gmask = jnp.zeros((N_GROUPS, n), jnp.bool_)
    for _ in range(TOPK_GROUPS):
        pick = _first_max(cur, gidx, N_GROUPS)
        gmask = gmask | pick
        cur = jnp.where(pick, neg, cur)
    emask = jnp.concatenate(
        [jnp.broadcast_to(gmask[g:g + 1, :], (GROUP_SIZE, n)) for g in range(N_GROUPS)], axis=0)

    cur = jnp.where(emask, sel, neg)
    eidx = lax.broadcasted_iota(jnp.int32, (N_EXPERTS, n), 0)
    picks = []
    for _ in range(TOP_K):
        pick = _first_max(cur, eidx, N_EXPERTS)
        picks.append(pick)
        cur = jnp.where(pick, neg, cur)
    return scores, picks


def _any(masks):
    return functools.reduce(jnp.logical_or, masks)


def _router_kernel(lg_ref, bias_ref, wd_ref):
    n = lg_ref.shape[0]
    scores, picks = _route(lg_ref[...].T[0:N_EXPERTS, :], bias_ref[0:N_EXPERTS, :])
    chosen = _any(picks)
    wsum = jnp.sum(jnp.where(chosen, scores, 0.0), axis=0, keepdims=True)
    wd = jnp.where(chosen, scores / wsum * ROUTED_SCALE, 0.0)
    wd_ref[...] = jnp.concatenate([wd, jnp.zeros((LANES - N_EXPERTS, n), F32)], axis=0).T


def _router(logits, bias_col):
    t = logits.shape[0]
    tile = min(ROUTER_TILE, t)
    return pl.pallas_call(
        _router_kernel,
        out_shape=jax.ShapeDtypeStruct((t, LANES), F32),
        grid=(t // tile,),
        in_specs=[pl.BlockSpec((tile, LANES), lambda i: (i, 0)), _full(bias_col.shape)],
        out_specs=pl.BlockSpec((tile, LANES), lambda i: (i, 0)),
        compiler_params=pltpu.CompilerParams(dimension_semantics=("parallel",), vmem_limit_bytes=VMEM_LIMIT),
        name="router",
    )(logits, bias_col)


def _plan_kernel(lg_ref, bias_ref, upper_ref, dest_ref, wsel_ref, cnt_ref, loff_ref):
    scores, picks = _route(lg_ref[...].T[0:N_EXPERTS, :], bias_ref[0:N_EXPERTS, :])
    chosen = _any(picks)
    chosen_f = jnp.where(chosen, 1.0, 0.0)
    cnt = jnp.sum(chosen_f, axis=1, keepdims=True)
    units = jnp.floor((cnt + (RUN_ALIGN - 1)) / RUN_ALIGN)
    ei = lax.broadcasted_iota(jnp.int32, (N_EXPERTS, N_EXPERTS), 0)
    ej = lax.broadcasted_iota(jnp.int32, (N_EXPERTS, N_EXPERTS), 1)
    before = jnp.where(ej < ei, 1.0, 0.0).astype(BF16)
    loff = RUN_ALIGN * _dot(before, jnp.broadcast_to(units, (N_EXPERTS, LANES)).astype(BF16))[:, 0:1]
    slot = loff + _dot(chosen_f.astype(BF16), upper_ref[...])
    wsum = jnp.sum(jnp.where(chosen, scores, 0.0), axis=0, keepdims=True)
    pick_sum = lambda v: jnp.concatenate(
        [jnp.sum(jnp.where(p, v, 0.0), axis=0, keepdims=True) for p in picks], axis=0)
    dest_ref[0] = (pick_sum(slot) * ROW_SUB).astype(jnp.int32)
    wsel_ref[0] = pick_sum(scores) / wsum * ROUTED_SCALE
    cnt_ref[0] = jnp.broadcast_to(cnt, (N_EXPERTS, LANES))
    loff_ref[0] = jnp.broadcast_to(loff, (N_EXPERTS, LANES))


def _plan(logits, bias_col, upper):
    t = logits.shape[0]
    nw = t // PLAN_WINDOW
    per_w = pl.BlockSpec((1, N_EXPERTS, LANES), lambda w: (w, 0, 0))
    picks = pl.BlockSpec((1, TOP_K, PLAN_WINDOW), lambda w: (w, 0, 0))
    return pl.pallas_call(
        _plan_kernel,
        out_shape=(jax.ShapeDtypeStruct((nw, TOP_K, PLAN_WINDOW), jnp.int32),
                   jax.ShapeDtypeStruct((nw, TOP_K, PLAN_WINDOW), F32),
                   jax.ShapeDtypeStruct((nw, N_EXPERTS, LANES), F32),
                   jax.ShapeDtypeStruct((nw, N_EXPERTS, LANES), F32)),
        grid=(nw,),
        in_specs=[pl.BlockSpec((PLAN_WINDOW, LANES), lambda w: (w, 0)), _full(bias_col.shape), _full(upper.shape)],
        out_specs=(picks, picks, per_w, per_w),
        compiler_params=pltpu.CompilerParams(dimension_semantics=("parallel",), vmem_limit_bytes=VMEM_LIMIT),
        name="plan",
    )(logits, bias_col, upper)


def _slot_rows(slot):
    return pl.ds(pl.multiple_of(slot * ROW_SUB, ROW_SUB), ROW_SUB)


def _table_copies(table_ref, flat_s, sem):
    return [pltpu.make_async_copy(table_ref.at[0, k], flat_s.at[pl.ds(k * PLAN_WINDOW, PLAN_WINDOW)], sem)
            for k in range(TOP_K)]


def _rows_at(offset):
    return pl.ds(pl.multiple_of(offset, ROW_SUB), ROW_SUB)


def _dispatch_kernel(dest_ref, h_ref, x_ref, dest_s, rows_s, sem):
    plan_copies = _table_copies(dest_ref, dest_s, sem)
    for c in plan_copies:
        c.start()
    x_ref[...] = jnp.zeros_like(x_ref)
    for j in range(ROW_SUB):
        rows_s[pl.ds(j, PLAN_WINDOW, stride=ROW_SUB), :] = h_ref[:, j * LANES:(j + 1) * LANES]
    for c in plan_copies:
        c.wait()

    def body(i, carry):
        for u in range(SCATTER_UNROLL):
            t = i * SCATTER_UNROLL + u
            row = rows_s[_slot_rows(t), :]
            for k in range(TOP_K):
                x_ref[_rows_at(dest_s[k * PLAN_WINDOW + t]), :] = row
        return carry

    lax.fori_loop(0, PLAN_WINDOW // SCATTER_UNROLL, body, 0)


def _dispatch(dest_w, h2):
    nw = dest_w.shape[0]
    region = REGION_ROWS * ROW_SUB
    return pl.pallas_call(
        _dispatch_kernel,
        out_shape=jax.ShapeDtypeStruct((nw * region, LANES), jnp.uint32),
        grid=(nw,),
        in_specs=[pl.BlockSpec((1, TOP_K, PLAN_WINDOW), lambda w: (w, 0, 0)),
                  pl.BlockSpec((PLAN_WINDOW, ROW_WORDS), lambda w: (w, 0))],
        out_specs=pl.BlockSpec((region, LANES), lambda w: (w, 0)),
        scratch_shapes=[pltpu.SMEM((TOP_K * PLAN_WINDOW,), jnp.int32),
                        pltpu.VMEM((PLAN_WINDOW * ROW_SUB, LANES), jnp.uint32),
                        pltpu.SemaphoreType.DMA],
        compiler_params=pltpu.CompilerParams(dimension_semantics=("arbitrary",), vmem_limit_bytes=VMEM_LIMIT),
        name="dispatch",
    )(dest_w, h2)


def _unpack_rows(ref, n):
    parts = [_unpack_pair(ref[pl.ds(j, n, stride=ROW_SUB), :]) for j in range(ROW_SUB)]
    return jnp.concatenate([p[0] for p in parts] + [p[1] for p in parts], axis=1)


def _ffn_kernel(gq_ref, gs_ref, to_ref, tn_ref, x_hbm, w1_ref, w3_ref, w2_ref,
                hs_ref, wds_ref, x1s_ref, gt2s_ref, gpost_ref, ws1_ref, ws3_ref, ws2_ref, y_hbm, ys_ref,
                w13_s, w2_s, xbuf, ybuf, rsem, wsem, acc_s):
    del x_hbm
    e = pl.program_id(0)
    n_groups = gq_ref[N_EXPERTS]
    block = RUN_ALIGN * ROW_SUB
    tile = FFN_TILE * ROW_SUB

    def go(cp, start):
        if start:
            cp.start()
        else:
            cp.wait()

    def fetch(q, start):
        def one(j):
            src = y_hbm.at[pl.ds(pl.multiple_of(to_ref[gs_ref[q] + j] * block, block), tile)]
            go(pltpu.make_async_copy(src, xbuf.at[q % FFN_RING, pl.ds(j * tile, tile)], rsem.at[q % FFN_RING]), start)

        size = gs_ref[q + 1] - gs_ref[q]

        @pl.when(size == FFN_GROUP)
        def _():
            for j in range(FFN_GROUP):
                one(j)

        @pl.when(size < FFN_GROUP)
        def _():
            for j in range(FFN_GROUP - 1):
                pl.when(j < size)(functools.partial(one, j))

    def writeback(q, start):
        most = FFN_TILE // RUN_ALIGN - 1

        def blocks(j, b, n):
            dst = pl.multiple_of(to_ref[gs_ref[q] + j] * block, block) + b * block
            go(pltpu.make_async_copy(ybuf.at[q % FFN_RING, pl.ds(j * tile + b * block, n * block)],
                                     y_hbm.at[pl.ds(dst, n * block)], wsem.at[q % FFN_RING]), start)

        for j in range(FFN_GROUP):
            item = gs_ref[q] + j
            n = jnp.where(item < gs_ref[q + 1], tn_ref[item], 0)
            pl.when(n >= most)(functools.partial(blocks, j, 0, most))
            pl.when(n > most)(functools.partial(blocks, j, most, 1))

            @pl.when((n > 0) & (n < most))
            def _():
                for b in range(most - 1):
                    pl.when(b < n)(functools.partial(blocks, j, b, 1))

    @pl.when(e == 0)
    def _():
        xbuf[...] = jnp.zeros_like(xbuf)
        for d in range(FFN_RING - 1):
            pl.when(d < n_groups)(functools.partial(fetch, d, True))

    w13_s[:, 0:D_EXPERT] = w1_ref[0].astype(BF16)
    w13_s[:, D_EXPERT:] = w3_ref[0].astype(BF16)
    w2_s[...] = w2_ref[0].astype(BF16)

    xs = hs_ref[...]

    @pl.when(e == 0)
    def _():
        sh = _silu(_dot(xs, ws1_ref[...].astype(BF16))) * _dot(xs, ws3_ref[...].astype(BF16))
        acc_s[...] = _dot(sh.astype(BF16), ws2_ref[...].astype(BF16))

    lane = lax.broadcasted_iota(jnp.int32, wds_ref.shape, 1)
    wcol = jnp.sum(jnp.where(lane == e, wds_ref[...], 0.0), axis=1, keepdims=True)
    rows = FFN_GROUP * FFN_TILE

    def ffn(x, ride):
        n = 0 if x is None else x.shape[0]
        if ride:
            x = xs if x is None else jnp.concatenate([x, xs], axis=0)
        h = _dot(x, w13_s[...])
        y = _dot((_silu(h[:, :D_EXPERT]) * h[:, D_EXPERT:]).astype(BF16), w2_s[...])
        if ride:
            acc_s[...] += y[n:] * wcol
        return y[:n] if n else None

    def group(q, ride):
        @pl.when(q + FFN_RING - 1 < n_groups)
        def _():
            fetch(q + FFN_RING - 1, True)

        @pl.when(q >= FFN_RING)
        def _():
            writeback(q - FFN_RING, False)

        fetch(q, False)
        y = ffn(_unpack_rows(xbuf.at[q % FFN_RING], rows).astype(BF16), ride)
        out = ybuf.at[q % FFN_RING]
        for j in range(ROW_SUB):
            out[pl.ds(j, rows, stride=ROW_SUB), :] = _pack_pair(
                y[:, j * LANES:(j + 1) * LANES], y[:, ROW_WORDS + j * LANES:ROW_WORDS + (j + 1) * LANES])
        writeback(q, True)

    first, end = gq_ref[e], gq_ref[e + 1]
    pl.when(first < end)(functools.partial(group, first, True))
    pl.when(first == end)(functools.partial(ffn, None, True))

    def group_body(q, carry):
        group(q, False)
        return carry

    lax.fori_loop(first + 1, end, group_body, 0)

    @pl.when(e == pl.num_programs(0) - 1)
    def _():
        ys_ref[...] = x1s_ref[...] + gt2s_ref[...] * _rms(acc_s[...], gpost_ref[...])

    @pl.when(e == pl.num_programs(0) - 1)
    def _():
        for d in range(FFN_RING):
            @pl.when(n_groups - 1 - d >= 0)
            def _():
                writeback(n_groups - 1 - d, False)


def _expert_ffn(group_of_expert, group_start, item_off, item_blocks, x_slots, w1, w3, w2,
                h2_s, wd_s, x1_s, gt2_s, gpost, ws1, ws3, ws2):
    of_expert = lambda e, *_: (e, 0, 0)
    whole = lambda a: pl.BlockSpec(a.shape, lambda e, *_: (0,) * a.ndim)
    tiles = pltpu.VMEM((FFN_RING, FFN_GROUP * FFN_TILE * ROW_SUB, LANES), jnp.uint32)
    sems = pltpu.SemaphoreType.DMA((FFN_RING,))
    dense = (h2_s, wd_s, x1_s, gt2_s, gpost, ws1, ws3, ws2)
    grid_spec = pltpu.PrefetchScalarGridSpec(
        num_scalar_prefetch=4,
        grid=(N_EXPERTS,),
        in_specs=[pl.BlockSpec(memory_space=pl.ANY),
                  pl.BlockSpec((1, D_MODEL, D_EXPERT), of_expert),
                  pl.BlockSpec((1, D_MODEL, D_EXPERT), of_expert),
                  pl.BlockSpec((1, D_EXPERT, D_MODEL), of_expert)] + [whole(a) for a in dense],
        out_specs=(pl.BlockSpec(memory_space=pl.ANY), whole(x1_s)),
        scratch_shapes=[pltpu.VMEM((D_MODEL, 2 * D_EXPERT), BF16), pltpu.VMEM((D_EXPERT, D_MODEL), BF16),
                        tiles, tiles, sems, sems, pltpu.VMEM(x1_s.shape, F32)])
    return pl.pallas_call(
        _ffn_kernel,
        out_shape=(jax.ShapeDtypeStruct(x_slots.shape, jnp.uint32), jax.ShapeDtypeStruct(x1_s.shape, F32)),
        grid_spec=grid_spec,
        input_output_aliases={4: 0},
        compiler_params=pltpu.CompilerParams(dimension_semantics=("arbitrary",), vmem_limit_bytes=VMEM_LIMIT),
        name="expert_ffn",
    )(group_of_expert, group_start, item_off, item_blocks, x_slots, w1, w3, w2, *dense)


def _work_items(cnt, loff):
    nw = cnt.shape[0]
    cnt = cnt.astype(jnp.int32).T.reshape(-1)
    loff = loff.astype(jnp.int32).T.reshape(-1)
    region = jnp.tile(jnp.arange(nw, dtype=jnp.int32) * REGION_ROWS, N_EXPERTS)
    n_items = nw * N_EXPERTS + (nw * PLAN_WINDOW * TOP_K) // FFN_TILE
    tiles = (cnt + FFN_TILE - 1) // FFN_TILE
    ends = jnp.cumsum(tiles)
    starts = ends - tiles
    i = jnp.arange(n_items + FFN_GROUP, dtype=jnp.int32)[:, None]
    mine = (starts[None, :] <= i) & (i < ends[None, :])
    of_pair = lambda v: jnp.sum(jnp.where(mine, v[None, :], 0), axis=1)
    r = i[:, 0] - of_pair(starts)
    row = of_pair(region + loff) + r * FFN_TILE
    own = jnp.clip(of_pair(cnt) - r * FFN_TILE, 0, FFN_TILE)

    zero = jnp.zeros((1,), jnp.int32)
    item_end = ends.reshape(N_EXPERTS, nw)[:, -1]
    item_start = jnp.concatenate([zero, item_end[:-1]])
    group_end = jnp.cumsum((item_end - item_start + FFN_GROUP - 1) // FFN_GROUP)
    group_first = jnp.concatenate([zero, group_end[:-1]])
    q = jnp.arange(N_EXPERTS + n_items // FFN_GROUP + 1, dtype=jnp.int32)[:, None]
    has = (group_first[None, :] <= q) & (q < group_end[None, :])
    of_expert = lambda v: jnp.sum(jnp.where(has, v[None, :], 0), axis=1)
    group_start = jnp.where(q[:, 0] < group_end[-1],
                            of_expert(item_start) + (q[:, 0] - of_expert(group_first)) * FFN_GROUP, item_end[-1])
    return (jnp.concatenate([zero, group_end]), group_start, row // RUN_ALIGN, (own + RUN_ALIGN - 1) // RUN_ALIGN)


def _combine_kernel(dest_ref, wsel_ref, y_hbm, h_ref, x1_ref, gt2_ref, gpost_ref, ws13_ref, ws2_ref, out_ref,
                    dest_s, wsel_s, lo_s, hi_s, ybuf, sem, ysem):
    w, s = pl.program_id(0), pl.program_id(1)
    region = REGION_ROWS * ROW_SUB
    part = region // REGION_PARTS

    def region_copies(win):
        return [pltpu.make_async_copy(y_hbm.at[pl.ds(win * region + p * part, part)],
                                      ybuf.at[win % 2, pl.ds(p * part, part)], ysem.at[win % 2, p])
                for p in range(REGION_PARTS)]

    @pl.when(s == 0)
    def _():
        copies = _table_copies(dest_ref, dest_s, sem.at[0]) + _table_copies(wsel_ref, wsel_s, sem.at[1])
        for c in copies:
            c.start()

        @pl.when(w == 0)
        def _():
            for c in region_copies(w):
                c.start()

        @pl.when(w + 1 < pl.num_programs(0))
        def _():
            for c in region_copies(w + 1):
                c.start()

        for c in copies + region_copies(w):
            c.wait()

    y_ref = ybuf.at[w % 2]

    base = s * COMBINE_TILE

    def body(i, carry):
        for u in range(COMBINE_UNROLL):
            t = i * COMBINE_UNROLL + u
            acc_lo = jnp.zeros((ROW_SUB, LANES), F32)
            acc_hi = jnp.zeros((ROW_SUB, LANES), F32)
            for k in range(TOP_K):
                at = k * PLAN_WINDOW + base + t
                lo, hi = _unpack_pair(y_ref[_rows_at(dest_s[at]), :])
                w = wsel_s[at]
                acc_lo = acc_lo + w * lo
                acc_hi = acc_hi + w * hi
            lo_s[_slot_rows(t), :] = acc_lo
            hi_s[_slot_rows(t), :] = acc_hi
        return carry

    lax.fori_loop(0, COMBINE_TILE // COMBINE_UNROLL, body, 0)
    rows = lambda ref: [ref[pl.ds(j, COMBINE_TILE, stride=ROW_SUB), :] for j in range(ROW_SUB)]
    routed = jnp.concatenate(rows(lo_s) + rows(hi_s), axis=1)
    h_lo, h_hi = _unpack_pair(h_ref[...])
    x = jnp.concatenate([h_lo, h_hi], axis=1).astype(BF16)
    hs = _dot(x, ws13_ref[...])
    act = _silu(hs[:, :D_EXPERT]) * hs[:, D_EXPERT:]
    ff = routed + _dot(act.astype(BF16), ws2_ref[...])
    out_ref[...] = x1_ref[...] + gt2_ref[0] * _rms(ff, gpost_ref[...])


def _combine(dest_w, wsel_w, y_slots, h2, x1, gt2, tokens_per_gt2, gpost, ws13, ws2):
    nw = dest_w.shape[0]
    sub = PLAN_WINDOW // COMBINE_TILE
    plan = pl.BlockSpec((1, TOP_K, PLAN_WINDOW), lambda w, s: (w, 0, 0))
    tok = lambda w, s: (w * sub + s, 0)
    return pl.pallas_call(
        _combine_kernel,
        out_shape=jax.ShapeDtypeStruct(x1.shape, F32),
        grid=(nw, sub),
        in_specs=[plan, plan,
                  pl.BlockSpec(memory_space=pl.ANY),
                  pl.BlockSpec((COMBINE_TILE, ROW_WORDS), tok),
                  pl.BlockSpec((COMBINE_TILE, D_MODEL), tok),
                  pl.BlockSpec((1, 1, D_MODEL), lambda w, s: ((w * PLAN_WINDOW) // tokens_per_gt2, 0, 0)),
                  _full((1, D_MODEL)), _full(ws13.shape), _full(ws2.shape)],
        out_specs=pl.BlockSpec((COMBINE_TILE, D_MODEL), tok),
        scratch_shapes=[pltpu.SMEM((TOP_K * PLAN_WINDOW,), jnp.int32),
                        pltpu.SMEM((TOP_K * PLAN_WINDOW,), F32),
                        pltpu.VMEM((COMBINE_TILE * ROW_SUB, LANES), F32),
                        pltpu.VMEM((COMBINE_TILE * ROW_SUB, LANES), F32),
                        pltpu.VMEM((2, REGION_ROWS * ROW_SUB, LANES), jnp.uint32),
                        pltpu.SemaphoreType.DMA((2,)), pltpu.SemaphoreType.DMA((2, REGION_PARTS))],
        compiler_params=pltpu.CompilerParams(dimension_semantics=("arbitrary", "arbitrary"),
                                             vmem_limit_bytes=VMEM_LIMIT),
        name="combine",
    )(dest_w, wsel_w, y_slots, h2, x1, gt2, gpost, ws13, ws2)


def kernel(x_prompt, x_sample, c_prompt, c_sample, state_pool, state_hgrn, w_ada, b_ada, g_pre_mix, g_post_mix,
           w_in, w_pool, pool_scale, lb_logits, g_out_norm, w_out, g_pre_ffn, g_post_ffn, w_router, router_bias,
           w_exp_gate, w_exp_up, w_exp_down, w_sh_gate, w_sh_up, w_sh_down):
    assert w_ada.shape[0] == 1 and lb_logits.shape[0] == 2, "single-layer trunk"
    bp, lp, d = x_prompt.shape
    bs = x_sample.shape[0]
    row = lambda a: a[0].reshape(1, -1)

    n_mod = bp + bs
    pad = (-n_mod) % 16
    c_all = jnp.concatenate([c_prompt, c_sample, jnp.zeros((pad, d), F32)], axis=0)
    mod = _ada(c_all, w_ada[0], b_ada)
    mod_p = mod[:bp].reshape(bp, 1, 6 * d)
    mod_s = mod[bp:n_mod]

    win = w_in[0].astype(BF16)
    wout = w_out[0].astype(BF16)
    wpool = w_pool[0].astype(BF16)
    wr = jnp.pad(w_router[0], ((0, 0), (0, LANES - N_EXPERTS)))
    wrh = wr.astype(BF16)
    wrl = (wr - wrh.astype(F32)).astype(BF16)
    m2 = jnp.asarray(_M2_NP, BF16)
    gpre, gpost, gffn, gffn_post = row(g_pre_mix), row(g_post_mix), row(g_pre_ffn), row(g_post_ffn)
    pscale, gout = row(pool_scale), row(g_out_norm)
    bias_col = jnp.pad(router_bias[0], (0, LANES - N_EXPERTS)).reshape(LANES, 1)

    x1_p, h2_p, lg_p, pool_p, st_p = _mix_prompt(x_prompt, mod_p, gpre, gpost, gffn, win, wpool, pscale,
                                                  lb_logits, gout, wout, wrh, wrl, m2)
    x1_s, h2_s, lg_s, pool_s, st_s = _mix_sample(x_sample[:, 0, :], mod_s, state_pool[0], state_hgrn[0],
                                                  gpre, gpost, gffn, win, wpool, pscale, lb_logits, gout, wout,
                                                  wrh, wrl)

    experts = (w_exp_gate[0], w_exp_up[0], w_exp_down[0], w_sh_gate[0], w_sh_up[0], w_sh_down[0])
    tp = bp * lp
    nw = tp // PLAN_WINDOW
    assert lp % PLAN_WINDOW == 0
    upper = jnp.asarray(np.triu(np.ones((PLAN_WINDOW, PLAN_WINDOW), np.float32), 1), BF16)
    dest, wsel, cnt, loff = _plan(lg_p.reshape(tp, LANES), bias_col, upper)
    dest_w, wsel_w = dest, wsel
    h2_rows = h2_p.reshape(tp, ROW_WORDS)
    x_slots = _dispatch(dest_w, h2_rows)
    wd_s = _router(lg_s, bias_col)
    y_slots, y_s = _expert_ffn(*_work_items(cnt[:, :, 0], loff[:, :, 0]), x_slots, *experts[:3],
                               h2_s, wd_s, x1_s, mod_s[:, 5 * d:], gffn_post, *experts[3:])
    ws13 = jnp.concatenate([experts[3], experts[4]], axis=1).astype(BF16)
    gt2_p = mod_p[:, :, 5 * d:]
    y_p = _combine(dest_w, wsel_w, y_slots, h2_rows, x1_p.reshape(tp, d), gt2_p, lp, gffn_post,
                   ws13, experts[5].astype(BF16))

    return (y_p.reshape(bp, lp, d), y_s.reshape(bs, 1, d), pool_p[None], st_p[None], pool_s[None], st_s[None])
```

```python
import functools

import numpy as np
import jax
import jax.numpy as jnp
from jax import lax
from jax.experimental import pallas as pl
from jax.experimental.pallas import tpu as pltpu

F32 = jnp.float32
BF16 = jnp.bfloat16

D_MODEL = 1024
POOL_WIDTH = 512
POOL_WINDOWS = (2, 4, 8, 16)
POOL_GROUP = 128
POOL_HIST = 15
HG_WIDTH = 512
HG_HEADS = 4
HG_HEAD_DIM = 128
IN_WIDTH = POOL_WIDTH + 4 * HG_WIDTH
N_EXPERTS = 64
TOP_K = 8
N_GROUPS = 8
TOPK_GROUPS = 4
GROUP_SIZE = N_EXPERTS // N_GROUPS
D_EXPERT = 256
ROUTED_SCALE = 2.5
EPS = 1e-6

LANES = 128
CHUNK = 64
TIME_TILE = 256
ADA_TILE_N = 1024
ROUTER_TILE = 512
SAMPLE_STATE_BLOCK = 16
PLAN_WINDOW = 1024
RUN_ALIGN = 32
FFN_TILE = 160
REGION_ROWS = PLAN_WINDOW * TOP_K + N_EXPERTS * (RUN_ALIGN - 1) + FFN_TILE
ROW_WORDS = D_MODEL // 2
ROW_SUB = ROW_WORDS // LANES
COMBINE_TILE = 256
REGION_PARTS = 4
SCATTER_UNROLL = 4
COMBINE_UNROLL = 16
FFN_GROUP = 4
CAST_PARTS = 4
FFN_RING = 6
VMEM_LIMIT = 58 * 1024 * 1024

_LEVELS = (64, 32, 16, 8, 4, 2)
_BLK_CUM = 0


def _level_blocks():
    c = CHUNK
    i = np.arange(c)[:, None]
    s = np.arange(c)[None, :]
    blocks = [(s <= i)]
    index = {}
    for lvl in _LEVELS:
        ref = (i // lvl) * lvl + lvl // 2 - 1
        index[lvl] = len(blocks)
        blocks.append(s <= ref)
    m = np.concatenate(blocks, axis=0).astype(np.float32)
    return np.concatenate([m, m], axis=1), index


_M2_NP, _LEVEL_INDEX = _level_blocks()


def _dot(a, b):
    return jnp.dot(a, b, preferred_element_type=F32)


def _dot_nt(a, b):
    return lax.dot_general(a, b, (((1,), (1,)), ((), ())), preferred_element_type=F32)


def _split2(x):
    hi = x.astype(BF16)
    lo = (x - hi.astype(F32)).astype(BF16)
    return hi, lo


def _pack_pair(a, b):
    lo = lax.bitcast_convert_type(a.astype(BF16).astype(F32), jnp.uint32)
    hi = lax.bitcast_convert_type(b.astype(BF16).astype(F32), jnp.uint32)
    return (lo >> 16) | hi


def _unpack_pair(words):
    lo = lax.bitcast_convert_type(words << 16, F32)
    hi = lax.bitcast_convert_type(words & jnp.uint32(0xFFFF0000), F32)
    return lo, hi


def _silu(x):
    return x * jax.nn.sigmoid(x)


def _rms(x, g):
    return x * lax.rsqrt(jnp.mean(x * x, axis=-1, keepdims=True) + EPS) * g


def _mods(m):
    return [m[:, j * D_MODEL:(j + 1) * D_MODEL] for j in range(6)]


def _forget_lower_bound(lbl):
    mx = jnp.max(lbl, axis=0, keepdims=True)
    e = jnp.exp(lbl - mx)
    return e[0:1] / jnp.sum(e, axis=0, keepdims=True)


def _router_logits(h2, wrh, wrl):
    hi, lo = _split2(h2)
    return _dot(hi, wrh) + _dot(lo, wrh) + _dot(hi, wrl)


def _ada_kernel(c_ref, w_ref, b_ref, o_ref):
    a_hi, a_lo = _split2(_silu(c_ref[...]))
    w_hi, w_lo = _split2(w_ref[...])
    o_ref[...] = _dot(a_hi, w_hi) + _dot(a_lo, w_hi) + _dot(a_hi, w_lo) + b_ref[...]


def _ada(c_all, w_ada, b_ada):
    rows = c_all.shape[0]
    n = w_ada.shape[1]
    return pl.pallas_call(
        _ada_kernel,
        out_shape=jax.ShapeDtypeStruct((rows, n), F32),
        grid=(n // ADA_TILE_N,),
        in_specs=[pl.BlockSpec((rows, D_MODEL), lambda j: (0, 0)),
                  pl.BlockSpec((D_MODEL, ADA_TILE_N), lambda j: (0, j)),
                  pl.BlockSpec((1, ADA_TILE_N), lambda j: (0, j))],
        out_specs=pl.BlockSpec((rows, ADA_TILE_N), lambda j: (0, j)),
        compiler_params=pltpu.CompilerParams(dimension_semantics=("parallel",),
                                             vmem_limit_bytes=VMEM_LIMIT),
        name="ada",
    )(c_all, w_ada, b_ada)


def _level_masks():
    i = lax.broadcasted_iota(jnp.int32, (CHUNK, 2 * CHUNK), 0)
    j = lax.broadcasted_iota(jnp.int32, (CHUNK, 2 * CHUNK), 1) & (CHUNK - 1)
    masks = {}
    for lvl in _LEVELS:
        sh = lvl.bit_length() - 1
        same = (i >> sh) == (j >> sh)
        upper = ((i >> (sh - 1)) & 1) == 1
        lower = ((j >> (sh - 1)) & 1) == 0
        masks[lvl] = same & upper & lower
    return masks, i == j


def _by_head(m):
    z = jnp.zeros((m.shape[0], HG_HEAD_DIM), m.dtype)
    return jnp.concatenate([jnp.concatenate([m[:, :HG_HEAD_DIM], z], axis=1),
                            jnp.concatenate([z, m[:, HG_HEAD_DIM:]], axis=1)], axis=0)


def _hgrn_chunk_pair(q, k, v, e_all, lane0, sts, masks, eye):
    hd = HG_HEAD_DIM

    def blk(n):
        return e_all[n * CHUNK:(n + 1) * CHUNK, lane0:lane0 + 2 * hd]

    b = blk(_BLK_CUM)
    kb = k.astype(BF16)
    a = jnp.where(eye, _dot_nt(q.astype(BF16), _by_head(kb)), 0.0)
    for lvl in _LEVELS:
        d = b - blk(_LEVEL_INDEX[lvl])
        ql = (q * jnp.exp(jnp.minimum(d, 0.0))).astype(BF16)
        kl = (k * jnp.exp(jnp.minimum(-d, 0.0))).astype(BF16) if lvl > 2 else kb
        a = a + jnp.where(masks[lvl], _dot_nt(ql, _by_head(kl)), 0.0)
    st_both = jnp.concatenate([jnp.concatenate([sts[0], jnp.zeros_like(sts[0])], axis=1),
                               jnp.concatenate([jnp.zeros_like(sts[1]), sts[1]], axis=1)], axis=0).astype(BF16)
    o = _dot(a.astype(BF16), _by_head(v.astype(BF16))) + _dot_nt((q * jnp.exp(b)).astype(BF16), st_both)
    b_last = b[CHUNK - 1:CHUNK, :]
    k_end = (k * jnp.exp(b_last - b)).astype(BF16)
    decay = jnp.exp(b_last)
    new = []
    for h in range(2):
        lanes = slice(h * hd, (h + 1) * hd)
        new.append(sts[h] * decay[:, lanes] + _dot(v[:, lanes].T.astype(BF16), k_end[:, lanes]))
    return o, new


def _mix_prompt_kernel(x_ref, mod_ref, gpre_ref, gpost_ref, gffn_ref, win_ref, wpool_ref, pscale_ref,
                       lbl_ref, gout_ref, wout_ref, wrh_ref, wrl_ref, m2_ref,
                       x1_ref, h2_ref, lg_ref, pool_ref, st_ref,
                       st_s, ubuf, q_s, k_s, v_s, g_s, o_s):
    t = pl.program_id(1)
    n_t = pl.num_programs(1)
    tt = TIME_TILE

    @pl.when(t == 0)
    def _():
        st_s[...] = jnp.zeros_like(st_s)
        ubuf[0:16, :] = jnp.zeros((16, POOL_WIDTH), F32)

    xt = x_ref[0]
    sh1, sc1, gt1, sh2, sc2, gt2 = _mods(mod_ref[0])
    h = _rms(xt, gpre_ref[...]) * (1.0 + sc1) + sh1
    proj = _dot(h.astype(BF16), win_ref[...])

    u = proj[:, :POOL_WIDTH]
    ubuf[16:16 + tt, :] = u
    pos = (t * tt + lax.broadcasted_iota(jnp.int32, (tt, 1), 0) + 1).astype(F32)
    ys = []
    for g, w in enumerate(POOL_WINDOWS):
        s = ubuf[:, g * POOL_GROUP:(g + 1) * POOL_GROUP]
        off = 0
        for step in range(w.bit_length() - 1):
            sh = 1 << step
            s = s[sh:, :] + s[:-sh, :]
            off += sh
        ws = s[16 - off:16 - off + tt, :]
        cnt = jnp.minimum(pos, float(w))
        d = ws / cnt - u[:, g * POOL_GROUP:(g + 1) * POOL_GROUP]
        ys.append(_dot(d.astype(BF16), wpool_ref[g]))
    y_pool = jnp.concatenate(ys, axis=1) * pscale_ref[...]
    ubuf[0:16, :] = ubuf[tt:tt + 16, :]

    @pl.when(t == n_t - 1)
    def _():
        pool_ref[0] = u[tt - POOL_HIST:, :]

    lb = _forget_lower_bound(lbl_ref[...])
    f = lb + (1.0 - lb) * jax.nn.sigmoid(proj[:, POOL_WIDTH + HG_WIDTH:POOL_WIDTH + 2 * HG_WIDTH])
    q_s[...] = _silu(proj[:, POOL_WIDTH:POOL_WIDTH + HG_WIDTH])
    k_s[...] = 1.0 - f
    v_s[...] = proj[:, POOL_WIDTH + 2 * HG_WIDTH:POOL_WIDTH + 3 * HG_WIDTH]
    g_s[...] = jnp.log(f)
    gate = _silu(proj[:, POOL_WIDTH + 3 * HG_WIDTH:])

    masks, eye = _level_masks()
    m2 = m2_ref[...]

    states = [st_s[hd] for hd in range(HG_HEADS)]
    for c in range(tt // CHUNK):
        rows = slice(c * CHUNK, (c + 1) * CHUNK)
        g_hi, g_lo = _split2(g_s[rows, :])
        e_all = _dot(m2, jnp.concatenate([g_hi, g_lo], axis=0))
        for hd in range(0, HG_HEADS, 2):
            lane0 = hd * HG_HEAD_DIM
            lanes = slice(lane0, lane0 + 2 * HG_HEAD_DIM)
            o, states[hd:hd + 2] = _hgrn_chunk_pair(q_s[rows, lanes], k_s[rows, lanes], v_s[rows, lanes], e_all,
                                                    lane0, states[hd:hd + 2], masks, eye)
            o_s[rows, lanes] = o
    for hd in range(HG_HEADS):
        st_s[hd] = states[hd]

    o = o_s[...]
    os_ = []
    for hd in range(HG_HEADS):
        oh = o[:, hd * HG_HEAD_DIM:(hd + 1) * HG_HEAD_DIM]
        os_.append(oh * lax.rsqrt(jnp.mean(oh * oh, axis=-1, keepdims=True) + EPS) * gout_ref[...])
    o_n = jnp.concatenate(os_, axis=1) * gate

    mix = _dot(y_pool.astype(BF16), wout_ref[0:POOL_WIDTH, :]) + _dot(o_n.astype(BF16), wout_ref[POOL_WIDTH:, :])
    x1 = xt + gt1 * _rms(mix, gpost_ref[...])
    h2 = _rms(x1, gffn_ref[...]) * (1.0 + sc2) + sh2
    x1_ref[0] = x1
    h2_ref[0] = _pack_pair(h2[:, :ROW_WORDS], h2[:, ROW_WORDS:])
    lg_ref[0] = _router_logits(h2, wrh_ref[...], wrl_ref[...])

    @pl.when(t == n_t - 1)
    def _():
        for hd in range(HG_HEADS):
            st_ref[0, hd] = st_s[hd].T


def _full(shape):
    nd = len(shape)
    return pl.BlockSpec(shape, lambda *_: (0,) * nd)


def _mix_prompt(x, mod, gpre, gpost, gffn, win, wpool, pscale, lbl, gout, wout, wrh, wrl, m2):
    b, l, d = x.shape
    tt = TIME_TILE
    tile = lambda i, j: (i, j, 0)
    per_b = lambda i, j: (i, 0, 0)
    return pl.pallas_call(
        _mix_prompt_kernel,
        out_shape=(jax.ShapeDtypeStruct((b, l, d), F32),
                   jax.ShapeDtypeStruct((b, l, ROW_WORDS), jnp.uint32),
                   jax.ShapeDtypeStruct((b, l, LANES), F32),
                   jax.ShapeDtypeStruct((b, POOL_HIST, POOL_WIDTH), F32),
                   jax.ShapeDtypeStruct((b, HG_HEADS, HG_HEAD_DIM, HG_HEAD_DIM), F32)),
        grid=(b, l // tt),
        in_specs=[pl.BlockSpec((1, tt, d), tile),
                  pl.BlockSpec((1, 1, 6 * d), per_b),
                  _full((1, d)), _full((1, d)), _full((1, d)),
                  _full(win.shape), _full(wpool.shape), _full((1, POOL_WIDTH)),
                  _full(lbl.shape), _full((1, HG_HEAD_DIM)), _full(wout.shape),
                  _full(wrh.shape), _full(wrl.shape), _full(m2.shape)],
        out_specs=(pl.BlockSpec((1, tt, d), tile),
                   pl.BlockSpec((1, tt, ROW_WORDS), tile),
                   pl.BlockSpec((1, tt, LANES), tile),
                   pl.BlockSpec((1, POOL_HIST, POOL_WIDTH), per_b),
                   pl.BlockSpec((1, HG_HEADS, HG_HEAD_DIM, HG_HEAD_DIM), lambda i, j: (i, 0, 0, 0))),
        scratch_shapes=[pltpu.VMEM((HG_HEADS, HG_HEAD_DIM, HG_HEAD_DIM), F32),
                        pltpu.VMEM((tt + 16, POOL_WIDTH), F32),
                        pltpu.VMEM((tt, HG_WIDTH), F32), pltpu.VMEM((tt, HG_WIDTH), F32),
                        pltpu.VMEM((tt, HG_WIDTH), F32), pltpu.VMEM((tt, HG_WIDTH), F32),
                        pltpu.VMEM((tt, HG_WIDTH), F32)],
        compiler_params=pltpu.CompilerParams(dimension_semantics=("parallel", "arbitrary"),
                                             vmem_limit_bytes=VMEM_LIMIT),
        name="mix_prompt",
    )(x, mod, gpre, gpost, gffn, win, wpool, pscale, lbl, gout, wout, wrh, wrl, m2)


def _mix_sample_in_kernel(x_ref, mod_ref, gpre_ref, win_ref, wpool_ref, pscale_ref, lbl_ref, hist_ref,
                          ypool_ref, npool_ref, ft_ref, qt_ref, v_ref, gate_ref):
    xt = x_ref[...]
    sh1, sc1 = _mods(mod_ref[...])[:2]
    h = _rms(xt, gpre_ref[...]) * (1.0 + sc1) + sh1
    proj = _dot(h.astype(BF16), win_ref[...])
    u = proj[:, :POOL_WIDTH]
    row = lax.broadcasted_iota(jnp.int32, (hist_ref.shape[0], POOL_HIST, POOL_GROUP), 1)
    ys = []
    for g, w in enumerate(POOL_WINDOWS):
        sl = slice(g * POOL_GROUP, (g + 1) * POOL_GROUP)
        past = jnp.sum(jnp.where(row >= POOL_HIST - (w - 1), hist_ref[:, :, sl], 0.0), axis=1)
        ug = u[:, sl]
        d = (past + ug) / float(w) - ug
        ys.append(_dot(d.astype(BF16), wpool_ref[g]))
    ypool_ref[...] = jnp.concatenate(ys, axis=1) * pscale_ref[...]
    npool_ref[:, 0:POOL_HIST - 1, :] = hist_ref[:, 1:POOL_HIST, :]
    npool_ref[:, POOL_HIST - 1, :] = u

    lb = _forget_lower_bound(lbl_ref[...])
    f = lb + (1.0 - lb) * jax.nn.sigmoid(proj[:, POOL_WIDTH + HG_WIDTH:POOL_WIDTH + 2 * HG_WIDTH])
    ft_ref[...] = f.T
    qt_ref[...] = _silu(proj[:, POOL_WIDTH:POOL_WIDTH + HG_WIDTH]).T
    v_ref[...] = proj[:, POOL_WIDTH + 2 * HG_WIDTH:POOL_WIDTH + 3 * HG_WIDTH]
    gate_ref[...] = _silu(proj[:, POOL_WIDTH + 3 * HG_WIDTH:])


def _mix_sample_state_kernel(s_ref, ft_ref, qt_ref, v_ref, snew_ref, o_ref):
    i = pl.program_id(0)
    lane = lax.broadcasted_iota(jnp.int32, (HG_HEAD_DIM, ft_ref.shape[1]), 1)
    for j in range(SAMPLE_STATE_BLOCK):
        mine = lane == i * SAMPLE_STATE_BLOCK + j
        for hd in range(HG_HEADS):
            r0 = hd * HG_HEAD_DIM
            f = jnp.sum(jnp.where(mine, ft_ref[r0:r0 + HG_HEAD_DIM, :], 0.0), axis=1, keepdims=True)
            q = jnp.sum(jnp.where(mine, qt_ref[r0:r0 + HG_HEAD_DIM, :], 0.0), axis=1, keepdims=True)
            v = v_ref[j:j + 1, r0:r0 + HG_HEAD_DIM]
            s_new = f * s_ref[j, hd] + (1.0 - f) * v
            snew_ref[j, hd] = s_new
            o_ref[j:j + 1, r0:r0 + HG_HEAD_DIM] = jnp.sum(q * s_new, axis=0, keepdims=True)


def _mix_sample_out_kernel(x_ref, mod_ref, o_ref, gate_ref, ypool_ref, gout_ref, wout_ref, gpost_ref,
                           gffn_ref, wrh_ref, wrl_ref, x1_ref, h2_ref, lg_ref):
    _, _, gt1, sh2, sc2, _ = _mods(mod_ref[...])
    o = o_ref[...]
    os_ = []
    for hd in range(HG_HEADS):
        oh = o[:, hd * HG_HEAD_DIM:(hd + 1) * HG_HEAD_DIM]
        os_.append(oh * lax.rsqrt(jnp.mean(oh * oh, axis=-1, keepdims=True) + EPS) * gout_ref[...])
    o_n = jnp.concatenate(os_, axis=1) * gate_ref[...]
    mix = (_dot(ypool_ref[...].astype(BF16), wout_ref[0:POOL_WIDTH, :])
           + _dot(o_n.astype(BF16), wout_ref[POOL_WIDTH:, :]))
    x1 = x_ref[...] + gt1 * _rms(mix, gpost_ref[...])
    h2 = _rms(x1, gffn_ref[...]) * (1.0 + sc2) + sh2
    x1_ref[...] = x1
    h2_ref[...] = h2.astype(BF16)
    lg_ref[...] = _router_logits(h2, wrh_ref[...], wrl_ref[...])


def _mix_sample(x, mod, hist, state, gpre, gpost, gffn, win, wpool, pscale, lbl, gout, wout, wrh, wrl):
    b = x.shape[0]
    bb = SAMPLE_STATE_BLOCK
    cp = pltpu.CompilerParams(vmem_limit_bytes=VMEM_LIMIT)
    ypool, npool, ft, qt, v, gate = pl.pallas_call(
        _mix_sample_in_kernel,
        out_shape=(jax.ShapeDtypeStruct((b, POOL_WIDTH), F32),
                   jax.ShapeDtypeStruct((b, POOL_HIST, POOL_WIDTH), F32),
                   jax.ShapeDtypeStruct((HG_WIDTH, b), F32),
                   jax.ShapeDtypeStruct((HG_WIDTH, b), F32),
                   jax.ShapeDtypeStruct((b, HG_WIDTH), F32),
                   jax.ShapeDtypeStruct((b, HG_WIDTH), F32)),
        compiler_params=cp,
        name="mix_sample_in",
    )(x, mod, gpre, win, wpool, pscale, lbl, hist)

    s_spec = pl.BlockSpec((bb, HG_HEADS, HG_HEAD_DIM, HG_HEAD_DIM), lambda i: (i, 0, 0, 0))
    col_spec = _full((HG_WIDTH, b))
    row_spec = pl.BlockSpec((bb, HG_WIDTH), lambda i: (i, 0))
    s_new, o = pl.pallas_call(
        _mix_sample_state_kernel,
        out_shape=(jax.ShapeDtypeStruct(state.shape, F32), jax.ShapeDtypeStruct((b, HG_WIDTH), F32)),
        grid=(b // bb,),
        in_specs=[s_spec, col_spec, col_spec, row_spec],
        out_specs=(s_spec, row_spec),
        compiler_params=pltpu.CompilerParams(dimension_semantics=("parallel",), vmem_limit_bytes=VMEM_LIMIT),
        name="mix_sample_state",
    )(state, ft, qt, v)

    x1, h2, lg = pl.pallas_call(
        _mix_sample_out_kernel,
        out_shape=(jax.ShapeDtypeStruct((b, D_MODEL), F32),
                   jax.ShapeDtypeStruct((b, D_MODEL), BF16),
                   jax.ShapeDtypeStruct((b, LANES), F32)),
        compiler_params=cp,
        name="mix_sample_out",
    )(x, mod, o, gate, ypool, gout, wout, gpost, gffn, wrh, wrl)
    return x1, h2, lg, npool, s_new


def _first_max(cur, idx, big):
    m = jnp.max(cur, axis=0, keepdims=True)
    first = jnp.min(jnp.where(cur == m, idx, big), axis=0, keepdims=True)
    return idx == first


def _route(logit, bias):
    n = logit.shape[1]
    scores = jax.nn.sigmoid(logit)
    sel = scores + bias
    neg = -jnp.inf

    sub = lax.broadcasted_iota(jnp.int32, (GROUP_SIZE, n), 0)
    gscore = []
    for g in range(N_GROUPS):
        sg = sel[g * GROUP_SIZE:(g + 1) * GROUP_SIZE, :]
        m1 = jnp.max(sg, axis=0, keepdims=True)
        rest = jnp.where(_first_max(sg, sub, GROUP_SIZE), neg, sg)
        gscore.append(m1 + jnp.max(rest, axis=0, keepdims=True))
    cur = jnp.concatenate(gscore, axis=0)
    gidx = lax.broadcasted_iota(jnp.int32, (N_GROUPS, n), 0)
    gmask = jnp.zeros((N_GROUPS, n), jnp.bool_)
    for _ in range(TOPK_GROUPS):
        pick = _first_max(cur, gidx, N_GROUPS)
        gmask = gmask | pick
        cur = jnp.where(pick, neg, cur)
    emask = jnp.concatenate(
        [jnp.broadcast_to(gmask[g:g + 1, :], (GROUP_SIZE, n)) for g in range(N_GROUPS)], axis=0)

    cur = jnp.where(emask, sel, neg)
    eidx = lax.broadcasted_iota(jnp.int32, (N_EXPERTS, n), 0)
    picks = []
    for _ in range(TOP_K):
        pick = _first_max(cur, eidx, N_EXPERTS)
        picks.append(pick)
        cur = jnp.where(pick, neg, cur)
    return scores, picks


def _any(masks):
    return functools.reduce(jnp.logical_or, masks)


def _router_kernel(lg_ref, bias_ref, wd_ref):
    n = lg_ref.shape[0]
    scores, picks = _route(lg_ref[...].T[0:N_EXPERTS, :], bias_ref[0:N_EXPERTS, :])
    chosen = _any(picks)
    wsum = jnp.sum(jnp.where(chosen, scores, 0.0), axis=0, keepdims=True)
    wd = jnp.where(chosen, scores / wsum * ROUTED_SCALE, 0.0)
    wd_ref[...] = jnp.concatenate([wd, jnp.zeros((LANES - N_EXPERTS, n), F32)], axis=0).T


def _router(logits, bias_col):
    t = logits.shape[0]
    tile = min(ROUTER_TILE, t)
    return pl.pallas_call(
        _router_kernel,
        out_shape=jax.ShapeDtypeStruct((t, LANES), F32),
        grid=(t // tile,),
        in_specs=[pl.BlockSpec((tile, LANES), lambda i: (i, 0)), _full(bias_col.shape)],
        out_specs=pl.BlockSpec((tile, LANES), lambda i: (i, 0)),
        compiler_params=pltpu.CompilerParams(dimension_semantics=("parallel",), vmem_limit_bytes=VMEM_LIMIT),
        name="router",
    )(logits, bias_col)


def _plan_kernel(lg_ref, bias_ref, upper_ref, dest_ref, wsel_ref, cnt_ref, loff_ref):
    scores, picks = _route(lg_ref[...].T[0:N_EXPERTS, :], bias_ref[0:N_EXPERTS, :])
    chosen = _any(picks)
    chosen_f = jnp.where(chosen, 1.0, 0.0)
    cnt = jnp.sum(chosen_f, axis=1, keepdims=True)
    units = jnp.floor((cnt + (RUN_ALIGN - 1)) / RUN_ALIGN)
    ei = lax.broadcasted_iota(jnp.int32, (N_EXPERTS, N_EXPERTS), 0)
    ej = lax.broadcasted_iota(jnp.int32, (N_EXPERTS, N_EXPERTS), 1)
    before = jnp.where(ej < ei, 1.0, 0.0).astype(BF16)
    loff = RUN_ALIGN * _dot(before, jnp.broadcast_to(units, (N_EXPERTS, LANES)).astype(BF16))[:, 0:1]
    slot = loff + _dot(chosen_f.astype(BF16), upper_ref[...])
    wsum = jnp.sum(jnp.where(chosen, scores, 0.0), axis=0, keepdims=True)
    pick_sum = lambda v: jnp.concatenate(
        [jnp.sum(jnp.where(p, v, 0.0), axis=0, keepdims=True) for p in picks], axis=0)
    dest_ref[0] = (pick_sum(slot) * ROW_SUB).astype(jnp.int32)
    wsel_ref[0] = pick_sum(scores) / wsum * ROUTED_SCALE
    cnt_ref[0] = jnp.broadcast_to(cnt, (N_EXPERTS, LANES))
    loff_ref[0] = jnp.broadcast_to(loff, (N_EXPERTS, LANES))


def _plan(logits, bias_col, upper):
    t = logits.shape[0]
    nw = t // PLAN_WINDOW
    per_w = pl.BlockSpec((1, N_EXPERTS, LANES), lambda w: (w, 0, 0))
    picks = pl.BlockSpec((1, TOP_K, PLAN_WINDOW), lambda w: (w, 0, 0))
    return pl.pallas_call(
        _plan_kernel,
        out_shape=(jax.ShapeDtypeStruct((nw, TOP_K, PLAN_WINDOW), jnp.int32),
                   jax.ShapeDtypeStruct((nw, TOP_K, PLAN_WINDOW), F32),
                   jax.ShapeDtypeStruct((nw, N_EXPERTS, LANES), F32),
                   jax.ShapeDtypeStruct((nw, N_EXPERTS, LANES), F32)),
        grid=(nw,),
        in_specs=[pl.BlockSpec((PLAN_WINDOW, LANES), lambda w: (w, 0)), _full(bias_col.shape), _full(upper.shape)],
        out_specs=(picks, picks, per_w, per_w),
        compiler_params=pltpu.CompilerParams(dimension_semantics=("parallel",), vmem_limit_bytes=VMEM_LIMIT),
        name="plan",
    )(logits, bias_col, upper)


def _slot_rows(slot):
    return pl.ds(pl.multiple_of(slot * ROW_SUB, ROW_SUB), ROW_SUB)


def _table_copies(table_ref, flat_s, sem):
    return [pltpu.make_async_copy(table_ref.at[0, k], flat_s.at[pl.ds(k * PLAN_WINDOW, PLAN_WINDOW)], sem)
            for k in range(TOP_K)]


def _rows_at(offset):
    return pl.ds(pl.multiple_of(offset, ROW_SUB), ROW_SUB)


def _dispatch_kernel(dest_ref, h_ref, x_ref, dest_s, rows_s, sem):
    plan_copies = _table_copies(dest_ref, dest_s, sem)
    for c in plan_copies:
        c.start()
    x_ref[...] = jnp.zeros_like(x_ref)
    for j in range(ROW_SUB):
        rows_s[pl.ds(j, PLAN_WINDOW, stride=ROW_SUB), :] = h_ref[:, j * LANES:(j + 1) * LANES]
    for c in plan_copies:
        c.wait()

    def body(i, carry):
        for u in range(SCATTER_UNROLL):
            t = i * SCATTER_UNROLL + u
            row = rows_s[_slot_rows(t), :]
            for k in range(TOP_K):
                x_ref[_rows_at(dest_s[k * PLAN_WINDOW + t]), :] = row
        return carry

    lax.fori_loop(0, PLAN_WINDOW // SCATTER_UNROLL, body, 0)


def _dispatch(dest_w, h2):
    nw = dest_w.shape[0]
    region = REGION_ROWS * ROW_SUB
    return pl.pallas_call(
        _dispatch_kernel,
        out_shape=jax.ShapeDtypeStruct((nw * region, LANES), jnp.uint32),
        grid=(nw,),
        in_specs=[pl.BlockSpec((1, TOP_K, PLAN_WINDOW), lambda w: (w, 0, 0)),
                  pl.BlockSpec((PLAN_WINDOW, ROW_WORDS), lambda w: (w, 0))],
        out_specs=pl.BlockSpec((region, LANES), lambda w: (w, 0)),
        scratch_shapes=[pltpu.SMEM((TOP_K * PLAN_WINDOW,), jnp.int32),
                        pltpu.VMEM((PLAN_WINDOW * ROW_SUB, LANES), jnp.uint32),
                        pltpu.SemaphoreType.DMA],
        compiler_params=pltpu.CompilerParams(dimension_semantics=("arbitrary",), vmem_limit_bytes=VMEM_LIMIT),
        name="dispatch",
    )(dest_w, h2)


def _unpack_rows(ref, n):
    parts = [_unpack_pair(ref[pl.ds(j, n, stride=ROW_SUB), :]) for j in range(ROW_SUB)]
    return jnp.concatenate([p[0] for p in parts] + [p[1] for p in parts], axis=1)


def _ffn_kernel(gq_ref, gs_ref, to_ref, tn_ref, x_hbm, w1_ref, w3_ref, w2_ref,
                hs_ref, wds_ref, x1s_ref, gt2s_ref, gpost_ref, ws1_ref, ws3_ref, ws2_ref, y_hbm, ys_ref,
                w13_s, w2_s, xbuf, ybuf, rsem, wsem, acc_s):
    del x_hbm
    s = pl.program_id(0)
    e = s - 1
    n_groups = gq_ref[N_EXPERTS]
    w13_now, w2_now = w13_s.at[(s + 1) % 2], w2_s.at[(s + 1) % 2]
    block = RUN_ALIGN * ROW_SUB
    tile = FFN_TILE * ROW_SUB

    def go(cp, start):
        if start:
            cp.start()
        else:
            cp.wait()

    def fetch(q, start):
        def one(j):
            src = y_hbm.at[pl.ds(pl.multiple_of(to_ref[gs_ref[q] + j] * block, block), tile)]
            go(pltpu.make_async_copy(src, xbuf.at[q % FFN_RING, pl.ds(j * tile, tile)], rsem.at[q % FFN_RING]), start)

        size = gs_ref[q + 1] - gs_ref[q]

        @pl.when(size == FFN_GROUP)
        def _():
            for j in range(FFN_GROUP):
                one(j)

        @pl.when(size < FFN_GROUP)
        def _():
            for j in range(FFN_GROUP - 1):
                pl.when(j < size)(functools.partial(one, j))

    def writeback(q, start):
        most = FFN_TILE // RUN_ALIGN - 1

        def blocks(j, b, n):
            dst = pl.multiple_of(to_ref[gs_ref[q] + j] * block, block) + b * block
            go(pltpu.make_async_copy(ybuf.at[q % FFN_RING, pl.ds(j * tile + b * block, n * block)],
                                     y_hbm.at[pl.ds(dst, n * block)], wsem.at[q % FFN_RING]), start)

        for j in range(FFN_GROUP):
            item = gs_ref[q] + j
            n = jnp.where(item < gs_ref[q + 1], tn_ref[item], 0)
            pl.when(n >= most)(functools.partial(blocks, j, 0, most))
            pl.when(n > most)(functools.partial(blocks, j, most, 1))

            @pl.when((n > 0) & (n < most))
            def _():
                for b in range(most - 1):
                    pl.when(b < n)(functools.partial(blocks, j, b, 1))

    @pl.when(s == 0)
    def _():
        xbuf[...] = jnp.zeros_like(xbuf)
        for d in range(FFN_RING - 1):
            pl.when(d < n_groups)(functools.partial(fetch, d, True))

    def cast_part(c):
        r13 = pl.ds(pl.multiple_of(c * (D_MODEL // CAST_PARTS), D_MODEL // CAST_PARTS), D_MODEL // CAST_PARTS)
        r2 = pl.ds(pl.multiple_of(c * (D_EXPERT // CAST_PARTS), D_EXPERT // CAST_PARTS), D_EXPERT // CAST_PARTS)
        w13_s[s % 2, r13, 0:D_EXPERT] = w1_ref[0, r13, :].astype(BF16)
        w13_s[s % 2, r13, D_EXPERT:] = w3_ref[0, r13, :].astype(BF16)
        w2_s[s % 2, r2, :] = w2_ref[0, r2, :].astype(BF16)

    xs = hs_ref[...]

    @pl.when(s == 0)
    def _():
        sh = _silu(_dot(xs, ws1_ref[...].astype(BF16))) * _dot(xs, ws3_ref[...].astype(BF16))
        acc_s[...] = _dot(sh.astype(BF16), ws2_ref[...].astype(BF16))

    lane = lax.broadcasted_iota(jnp.int32, wds_ref.shape, 1)
    wcol = jnp.sum(jnp.where(lane == e, wds_ref[...], 0.0), axis=1, keepdims=True)
    rows = FFN_GROUP * FFN_TILE

    def ffn(x, ride):
        n = 0 if x is None else x.shape[0]
        if ride:
            x = xs if x is None else jnp.concatenate([x, xs], axis=0)
        h = _dot(x, w13_now[...])
        y = _dot((_silu(h[:, :D_EXPERT]) * h[:, D_EXPERT:]).astype(BF16), w2_now[...])
        if ride:
            acc_s[...] += y[n:] * wcol
        return y[:n] if n else None

    def group(q, ride, part):
        @pl.when(q + FFN_RING - 1 < n_groups)
        def _():
            fetch(q + FFN_RING - 1, True)

        @pl.when(q >= FFN_RING)
        def _():
            writeback(q - FFN_RING, False)

        fetch(q, False)
        cast_part(part)
        y = ffn(_unpack_rows(xbuf.at[q % FFN_RING], rows).astype(BF16), ride)
        out = ybuf.at[q % FFN_RING]
        for j in range(ROW_SUB):
            out[pl.ds(j, rows, stride=ROW_SUB), :] = _pack_pair(
                y[:, j * LANES:(j + 1) * LANES], y[:, ROW_WORDS + j * LANES:ROW_WORDS + (j + 1) * LANES])
        writeback(q, True)

    first = gq_ref[jnp.maximum(e, 0)]
    end = jnp.where(s > 0, gq_ref[s], first)
    pl.when(first < end)(functools.partial(group, first, True, 0))

    @pl.when(first == end)
    def _():
        cast_part(0)
        pl.when(s > 0)(functools.partial(ffn, None, True))

    def group_body(q, carry):
        group(q, False, jnp.minimum(q - first, CAST_PARTS - 1))
        return carry

    lax.fori_loop(first + 1, end, group_body, 0)
    for c in range(1, CAST_PARTS):
        pl.when(end - first <= c)(functools.partial(cast_part, c))

    @pl.when(s == pl.num_programs(0) - 1)
    def _():
        ys_ref[...] = x1s_ref[...] + gt2s_ref[...] * _rms(acc_s[...], gpost_ref[...])

    @pl.when(s == pl.num_programs(0) - 1)
    def _():
        for d in range(FFN_RING):
            @pl.when(n_groups - 1 - d >= 0)
            def _():
                writeback(n_groups - 1 - d, False)


def _expert_ffn(group_of_expert, group_start, item_off, item_blocks, x_slots, w1, w3, w2,
                h2_s, wd_s, x1_s, gt2_s, gpost, ws1, ws3, ws2):
    of_expert = lambda s, *_: (jnp.minimum(s, N_EXPERTS - 1), 0, 0)
    whole = lambda a: pl.BlockSpec(a.shape, lambda e, *_: (0,) * a.ndim)
    tiles = pltpu.VMEM((FFN_RING, FFN_GROUP * FFN_TILE * ROW_SUB, LANES), jnp.uint32)
    sems = pltpu.SemaphoreType.DMA((FFN_RING,))
    dense = (h2_s, wd_s, x1_s, gt2_s, gpost, ws1, ws3, ws2)
    grid_spec = pltpu.PrefetchScalarGridSpec(
        num_scalar_prefetch=4,
        grid=(N_EXPERTS + 1,),
        in_specs=[pl.BlockSpec(memory_space=pl.ANY),
                  pl.BlockSpec((1, D_MODEL, D_EXPERT), of_expert),
                  pl.BlockSpec((1, D_MODEL, D_EXPERT), of_expert),
                  pl.BlockSpec((1, D_EXPERT, D_MODEL), of_expert)] + [whole(a) for a in dense],
        out_specs=(pl.BlockSpec(memory_space=pl.ANY), whole(x1_s)),
        scratch_shapes=[pltpu.VMEM((2, D_MODEL, 2 * D_EXPERT), BF16), pltpu.VMEM((2, D_EXPERT, D_MODEL), BF16),
                        tiles, tiles, sems, sems, pltpu.VMEM(x1_s.shape, F32)])
    return pl.pallas_call(
        _ffn_kernel,
        out_shape=(jax.ShapeDtypeStruct(x_slots.shape, jnp.uint32), jax.ShapeDtypeStruct(x1_s.shape, F32)),
        grid_spec=grid_spec,
        input_output_aliases={4: 0},
        compiler_params=pltpu.CompilerParams(dimension_semantics=("arbitrary",), vmem_limit_bytes=VMEM_LIMIT),
        name="expert_ffn",
    )(group_of_expert, group_start, item_off, item_blocks, x_slots, w1, w3, w2, *dense)


def _work_items(cnt, loff):
    nw = cnt.shape[0]
    cnt = cnt.astype(jnp.int32).T.reshape(-1)
    loff = loff.astype(jnp.int32).T.reshape(-1)
    region = jnp.tile(jnp.arange(nw, dtype=jnp.int32) * REGION_ROWS, N_EXPERTS)
    n_items = nw * N_EXPERTS + (nw * PLAN_WINDOW * TOP_K) // FFN_TILE
    tiles = (cnt + FFN_TILE - 1) // FFN_TILE
    ends = jnp.cumsum(tiles)
    starts = ends - tiles
    i = jnp.arange(n_items + FFN_GROUP, dtype=jnp.int32)[:, None]
    mine = (starts[None, :] <= i) & (i < ends[None, :])
    of_pair = lambda v: jnp.sum(jnp.where(mine, v[None, :], 0), axis=1)
    r = i[:, 0] - of_pair(starts)
    row = of_pair(region + loff) + r * FFN_TILE
    own = jnp.clip(of_pair(cnt) - r * FFN_TILE, 0, FFN_TILE)

    zero = jnp.zeros((1,), jnp.int32)
    item_end = ends.reshape(N_EXPERTS, nw)[:, -1]
    item_start = jnp.concatenate([zero, item_end[:-1]])
    group_end = jnp.cumsum((item_end - item_start + FFN_GROUP - 1) // FFN_GROUP)
    group_first = jnp.concatenate([zero, group_end[:-1]])
    q = jnp.arange(N_EXPERTS + n_items // FFN_GROUP + 1, dtype=jnp.int32)[:, None]
    has = (group_first[None, :] <= q) & (q < group_end[None, :])
    of_expert = lambda v: jnp.sum(jnp.where(has, v[None, :], 0), axis=1)
    group_start = jnp.where(q[:, 0] < group_end[-1],
                            of_expert(item_start) + (q[:, 0] - of_expert(group_first)) * FFN_GROUP, item_end[-1])
    return (jnp.concatenate([zero, group_end]), group_start, row // RUN_ALIGN, (own + RUN_ALIGN - 1) // RUN_ALIGN)


def _combine_kernel(dest_ref, wsel_ref, y_hbm, h_ref, x1_ref, gt2_ref, gpost_ref, ws13_ref, ws2_ref, out_ref,
                    dest_s, wsel_s, lo_s, hi_s, ybuf, sem, ysem):
    w, s = pl.program_id(0), pl.program_id(1)
    region = REGION_ROWS * ROW_SUB
    part = region // REGION_PARTS

    def region_copies(win):
        return [pltpu.make_async_copy(y_hbm.at[pl.ds(win * region + p * part, part)],
                                      ybuf.at[win % 2, pl.ds(p * part, part)], ysem.at[win % 2, p])
                for p in range(REGION_PARTS)]

    @pl.when(s == 0)
    def _():
        copies = _table_copies(dest_ref, dest_s, sem.at[0]) + _table_copies(wsel_ref, wsel_s, sem.at[1])
        for c in copies:
            c.start()

        @pl.when(w == 0)
        def _():
            for c in region_copies(w):
                c.start()

        @pl.when(w + 1 < pl.num_programs(0))
        def _():
            for c in region_copies(w + 1):
                c.start()

        for c in copies + region_copies(w):
            c.wait()

    y_ref = ybuf.at[w % 2]

    base = s * COMBINE_TILE

    def body(i, carry):
        for u in range(COMBINE_UNROLL):
            t = i * COMBINE_UNROLL + u
            acc_lo = jnp.zeros((ROW_SUB, LANES), F32)
            acc_hi = jnp.zeros((ROW_SUB, LANES), F32)
            for k in range(TOP_K):
                at = k * PLAN_WINDOW + base + t
                lo, hi = _unpack_pair(y_ref[_rows_at(dest_s[at]), :])
                w = wsel_s[at]
                acc_lo = acc_lo + w * lo
                acc_hi = acc_hi + w * hi
            lo_s[_slot_rows(t), :] = acc_lo
            hi_s[_slot_rows(t), :] = acc_hi
        return carry

    lax.fori_loop(0, COMBINE_TILE // COMBINE_UNROLL, body, 0)
    rows = lambda ref: [ref[pl.ds(j, COMBINE_TILE, stride=ROW_SUB), :] for j in range(ROW_SUB)]
    routed = jnp.concatenate(rows(lo_s) + rows(hi_s), axis=1)
    h_lo, h_hi = _unpack_pair(h_ref[...])
    x = jnp.concatenate([h_lo, h_hi], axis=1).astype(BF16)
    hs = _dot(x, ws13_ref[...])
    act = _silu(hs[:, :D_EXPERT]) * hs[:, D_EXPERT:]
    ff = routed + _dot(act.astype(BF16), ws2_ref[...])
    out_ref[...] = x1_ref[...] + gt2_ref[0] * _rms(ff, gpost_ref[...])


def _combine(dest_w, wsel_w, y_slots, h2, x1, gt2, tokens_per_gt2, gpost, ws13, ws2):
    nw = dest_w.shape[0]
    sub = PLAN_WINDOW // COMBINE_TILE
    plan = pl.BlockSpec((1, TOP_K, PLAN_WINDOW), lambda w, s: (w, 0, 0))
    tok = lambda w, s: (w * sub + s, 0)
    return pl.pallas_call(
        _combine_kernel,
        out_shape=jax.ShapeDtypeStruct(x1.shape, F32),
        grid=(nw, sub),
        in_specs=[plan, plan,
                  pl.BlockSpec(memory_space=pl.ANY),
                  pl.BlockSpec((COMBINE_TILE, ROW_WORDS), tok),
                  pl.BlockSpec((COMBINE_TILE, D_MODEL), tok),
                  pl.BlockSpec((1, 1, D_MODEL), lambda w, s: ((w * PLAN_WINDOW) // tokens_per_gt2, 0, 0)),
                  _full((1, D_MODEL)), _full(ws13.shape), _full(ws2.shape)],
        out_specs=pl.BlockSpec((COMBINE_TILE, D_MODEL), tok),
        scratch_shapes=[pltpu.SMEM((TOP_K * PLAN_WINDOW,), jnp.int32),
                        pltpu.SMEM((TOP_K * PLAN_WINDOW,), F32),
                        pltpu.VMEM((COMBINE_TILE * ROW_SUB, LANES), F32),
                        pltpu.VMEM((COMBINE_TILE * ROW_SUB, LANES), F32),
                        pltpu.VMEM((2, REGION_ROWS * ROW_SUB, LANES), jnp.uint32),
                        pltpu.SemaphoreType.DMA((2,)), pltpu.SemaphoreType.DMA((2, REGION_PARTS))],
        compiler_params=pltpu.CompilerParams(dimension_semantics=("arbitrary", "arbitrary"),
                                             vmem_limit_bytes=VMEM_LIMIT),
        name="combine",
    )(dest_w, wsel_w, y_slots, h2, x1, gt2, gpost, ws13, ws2)


def kernel(x_prompt, x_sample, c_prompt, c_sample, state_pool, state_hgrn, w_ada, b_ada, g_pre_mix, g_post_mix,
           w_in, w_pool, pool_scale, lb_logits, g_out_norm, w_out, g_pre_ffn, g_post_ffn, w_router, router_bias,
           w_exp_gate, w_exp_up, w_exp_down, w_sh_gate, w_sh_up, w_sh_down):
    assert w_ada.shape[0] == 1 and lb_logits.shape[0] == 2, "single-layer trunk"
    bp, lp, d = x_prompt.shape
    bs = x_sample.shape[0]
    row = lambda a: a[0].reshape(1, -1)

    n_mod = bp + bs
    pad = (-n_mod) % 16
    c_all = jnp.concatenate([c_prompt, c_sample, jnp.zeros((pad, d), F32)], axis=0)
    mod = _ada(c_all, w_ada[0], b_ada)
    mod_p = mod[:bp].reshape(bp, 1, 6 * d)
    mod_s = mod[bp:n_mod]

    win = w_in[0].astype(BF16)
    wout = w_out[0].astype(BF16)
    wpool = w_pool[0].astype(BF16)
    wr = jnp.pad(w_router[0], ((0, 0), (0, LANES - N_EXPERTS)))
    wrh = wr.astype(BF16)
    wrl = (wr - wrh.astype(F32)).astype(BF16)
    m2 = jnp.asarray(_M2_NP, BF16)
    gpre, gpost, gffn, gffn_post = row(g_pre_mix), row(g_post_mix), row(g_pre_ffn), row(g_post_ffn)
    pscale, gout = row(pool_scale), row(g_out_norm)
    bias_col = jnp.pad(router_bias[0], (0, LANES - N_EXPERTS)).reshape(LANES, 1)

    x1_p, h2_p, lg_p, pool_p, st_p = _mix_prompt(x_prompt, mod_p, gpre, gpost, gffn, win, wpool, pscale,
                                                  lb_logits, gout, wout, wrh, wrl, m2)
    x1_s, h2_s, lg_s, pool_s, st_s = _mix_sample(x_sample[:, 0, :], mod_s, state_pool[0], state_hgrn[0],
                                                  gpre, gpost, gffn, win, wpool, pscale, lb_logits, gout, wout,
                                                  wrh, wrl)

    experts = (w_exp_gate[0], w_exp_up[0], w_exp_down[0], w_sh_gate[0], w_sh_up[0], w_sh_down[0])
    tp = bp * lp
    nw = tp // PLAN_WINDOW
    assert lp % PLAN_WINDOW == 0
    upper = jnp.asarray(np.triu(np.ones((PLAN_WINDOW, PLAN_WINDOW), np.float32), 1), BF16)
    dest, wsel, cnt, loff = _plan(lg_p.reshape(tp, LANES), bias_col, upper)
    dest_w, wsel_w = dest, wsel
    h2_rows = h2_p.reshape(tp, ROW_WORDS)
    x_slots = _dispatch(dest_w, h2_rows)
    wd_s = _router(lg_s, bias_col)
    y_slots, y_s = _expert_ffn(*_work_items(cnt[:, :, 0], loff[:, :, 0]), x_slots, *experts[:3],
                               h2_s, wd_s, x1_s, mod_s[:, 5 * d:], gffn_post, *experts[3:])
    ws13 = jnp.concatenate([experts[3], experts[4]], axis=1).astype(BF16)
    gt2_p = mod_p[:, :, 5 * d:]
    y_p = _combine(dest_w, wsel_w, y_slots, h2_rows, x1_p.reshape(tp, d), gt2_p, lp, gffn_post,
                   ws13, experts[5].astype(BF16))

    return (y_p.reshape(bp, lp, d), y_s.reshape(bs, 1, d), pool_p[None], st_p[None], pool_s[None], st_s[None])
```

```python
import functools
import itertools

import numpy as np
import jax
import jax.numpy as jnp
from jax import lax
from jax.experimental import pallas as pl
from jax.experimental.pallas import tpu as pltpu

F32 = jnp.float32
BF16 = jnp.bfloat16

D_MODEL = 1024
POOL_WIDTH = 512
POOL_WINDOWS = (2, 4, 8, 16)
POOL_GROUP = 128
POOL_HIST = 15
HG_WIDTH = 512
HG_HEADS = 4
HG_HEAD_DIM = 128
IN_WIDTH = POOL_WIDTH + 4 * HG_WIDTH
N_EXPERTS = 64
TOP_K = 8
N_GROUPS = 8
TOPK_GROUPS = 4
GROUP_SIZE = N_EXPERTS // N_GROUPS
D_EXPERT = 256
ROUTED_SCALE = 2.5
EPS = 1e-6

LANES = 128
CHUNK = 64
TIME_TILE = 512
ADA_TILE_N = 1024
ROUTER_TILE = 512
SAMPLE_STATE_BLOCK = 16
PLAN_WINDOW = 1024
RUN_ALIGN = 32
FFN_TILE = 160
REGION_ROWS = PLAN_WINDOW * TOP_K + N_EXPERTS * (RUN_ALIGN - 1) + FFN_TILE
ROW_WORDS = D_MODEL // 2
ROW_SUB = ROW_WORDS // LANES
COMBINE_TILE = 256
REGION_PARTS = 4
SCATTER_UNROLL = 4
COMBINE_UNROLL = 16
FFN_GROUP = 4
FFN_RING = 6
VMEM_LIMIT = 58 * 1024 * 1024

_LEVELS = (64, 32, 16, 8, 4, 2)
_BLK_CUM = 0


def _level_blocks():
    c = CHUNK
    i = np.arange(c)[:, None]
    s = np.arange(c)[None, :]
    blocks = [(s <= i)]
    index = {}
    for lvl in _LEVELS:
        ref = (i // lvl) * lvl + lvl // 2 - 1
        index[lvl] = len(blocks)
        blocks.append(s <= ref)
    m = np.concatenate(blocks, axis=0).astype(np.float32)
    return np.concatenate([m, m], axis=1), index


_M2_NP, _LEVEL_INDEX = _level_blocks()


def _dot(a, b):
    return jnp.dot(a, b, preferred_element_type=F32)


def _dot_nt(a, b):
    return lax.dot_general(a, b, (((1,), (1,)), ((), ())), preferred_element_type=F32)


def _split2(x):
    hi = x.astype(BF16)
    lo = (x - hi.astype(F32)).astype(BF16)
    return hi, lo


def _pack_pair(a, b):
    lo = lax.bitcast_convert_type(a.astype(BF16).astype(F32), jnp.uint32)
    hi = lax.bitcast_convert_type(b.astype(BF16).astype(F32), jnp.uint32)
    return (lo >> 16) | hi


def _unpack_pair(words):
    lo = lax.bitcast_convert_type(words << 16, F32)
    hi = lax.bitcast_convert_type(words & jnp.uint32(0xFFFF0000), F32)
    return lo, hi


def _silu(x):
    return x * jax.nn.sigmoid(x)


def _rms(x, g):
    return x * lax.rsqrt(jnp.mean(x * x, axis=-1, keepdims=True) + EPS) * g


def _mods(m):
    return [m[:, j * D_MODEL:(j + 1) * D_MODEL] for j in range(6)]


def _forget_lower_bound(lbl):
    mx = jnp.max(lbl, axis=0, keepdims=True)
    e = jnp.exp(lbl - mx)
    return e[0:1] / jnp.sum(e, axis=0, keepdims=True)


def _router_logits(h2, wrh, wrl):
    hi, lo = _split2(h2)
    return _dot(hi, wrh) + _dot(lo, wrh) + _dot(hi, wrl)


def _ada_kernel(c_ref, w_ref, b_ref, o_ref):
    a_hi, a_lo = _split2(_silu(c_ref[...]))
    w_hi, w_lo = _split2(w_ref[...])
    o_ref[...] = _dot(a_hi, w_hi) + _dot(a_lo, w_hi) + _dot(a_hi, w_lo) + b_ref[...]


def _ada(c_all, w_ada, b_ada):
    rows = c_all.shape[0]
    n = w_ada.shape[1]
    return pl.pallas_call(
        _ada_kernel,
        out_shape=jax.ShapeDtypeStruct((rows, n), F32),
        grid=(n // ADA_TILE_N,),
        in_specs=[pl.BlockSpec((rows, D_MODEL), lambda j: (0, 0)),
                  pl.BlockSpec((D_MODEL, ADA_TILE_N), lambda j: (0, j)),
                  pl.BlockSpec((1, ADA_TILE_N), lambda j: (0, j))],
        out_specs=pl.BlockSpec((rows, ADA_TILE_N), lambda j: (0, j)),
        compiler_params=pltpu.CompilerParams(dimension_semantics=("parallel",),
                                             vmem_limit_bytes=VMEM_LIMIT),
        name="ada",
    )(c_all, w_ada, b_ada)


def _level_masks():
    i = lax.broadcasted_iota(jnp.int32, (CHUNK, 2 * CHUNK), 0)
    j = lax.broadcasted_iota(jnp.int32, (CHUNK, 2 * CHUNK), 1) & (CHUNK - 1)
    masks = {}
    for lvl in _LEVELS:
        sh = lvl.bit_length() - 1
        same = (i >> sh) == (j >> sh)
        upper = ((i >> (sh - 1)) & 1) == 1
        lower = ((j >> (sh - 1)) & 1) == 0
        masks[lvl] = same & upper & lower
    return masks, i == j


def _by_head(m):
    z = jnp.zeros((m.shape[0], HG_HEAD_DIM), m.dtype)
    return jnp.concatenate([jnp.concatenate([m[:, :HG_HEAD_DIM], z], axis=1),
                            jnp.concatenate([z, m[:, HG_HEAD_DIM:]], axis=1)], axis=0)


def _hgrn_chunk_pair(q, k, v, e_all, lane0, sts, masks, eye):
    hd = HG_HEAD_DIM

    def blk(n):
        return e_all[n * CHUNK:(n + 1) * CHUNK, lane0:lane0 + 2 * hd]

    b = blk(_BLK_CUM)
    kb = k.astype(BF16)
    a = jnp.where(eye, _dot_nt(q.astype(BF16), _by_head(kb)), 0.0)
    for lvl in _LEVELS:
        d = b - blk(_LEVEL_INDEX[lvl])
        ql = (q * jnp.exp(jnp.minimum(d, 0.0))).astype(BF16)
        kl = (k * jnp.exp(jnp.minimum(-d, 0.0))).astype(BF16) if lvl > 2 else kb
        a = a + jnp.where(masks[lvl], _dot_nt(ql, _by_head(kl)), 0.0)
    st_both = jnp.concatenate([jnp.concatenate([sts[0], jnp.zeros_like(sts[0])], axis=1),
                               jnp.concatenate([jnp.zeros_like(sts[1]), sts[1]], axis=1)], axis=0).astype(BF16)
    o = _dot(a.astype(BF16), _by_head(v.astype(BF16))) + _dot_nt((q * jnp.exp(b)).astype(BF16), st_both)
    b_last = b[CHUNK - 1:CHUNK, :]
    k_end = (k * jnp.exp(b_last - b)).astype(BF16)
    decay = jnp.exp(b_last)
    new = []
    for h in range(2):
        lanes = slice(h * hd, (h + 1) * hd)
        new.append(sts[h] * decay[:, lanes] + _dot(v[:, lanes].T.astype(BF16), k_end[:, lanes]))
    return o, new


def _mix_in(x_ref, mod_ref, t, gpre_ref, win_ref, wpool_ref, pscale_ref, lbl_ref, ubuf, dst):
    tt = TIME_TILE
    q_s, k_s, v_s, g_s, gate_s, yp_s = dst
    xt = x_ref[0]
    sh1, sc1 = _mods(mod_ref[0])[:2]
    h = (_rms(xt, gpre_ref[...]) * (1.0 + sc1) + sh1).astype(BF16)
    yield
    proj = _dot(h, win_ref[...])
    yield

    u = proj[:, :POOL_WIDTH]
    ubuf[16:16 + tt, :] = u
    pos = (t * tt + lax.broadcasted_iota(jnp.int32, (tt, 1), 0) + 1).astype(F32)
    ys = []
    for g, w in enumerate(POOL_WINDOWS):
        s = ubuf[:, g * POOL_GROUP:(g + 1) * POOL_GROUP]
        off = 0
        for step in range(w.bit_length() - 1):
            sh = 1 << step
            s = s[sh:, :] + s[:-sh, :]
            off += sh
        ws = s[16 - off:16 - off + tt, :]
        cnt = jnp.minimum(pos, float(w))
        d = ws / cnt - u[:, g * POOL_GROUP:(g + 1) * POOL_GROUP]
        ys.append(_dot(d.astype(BF16), wpool_ref[g]))
    yp_s[...] = jnp.concatenate(ys, axis=1) * pscale_ref[...]
    ubuf[0:16, :] = ubuf[tt:tt + 16, :]

    lb = _forget_lower_bound(lbl_ref[...])
    f = lb + (1.0 - lb) * jax.nn.sigmoid(proj[:, POOL_WIDTH + HG_WIDTH:POOL_WIDTH + 2 * HG_WIDTH])
    q_s[...] = _silu(proj[:, POOL_WIDTH:POOL_WIDTH + HG_WIDTH])
    k_s[...] = 1.0 - f
    v_s[...] = proj[:, POOL_WIDTH + 2 * HG_WIDTH:POOL_WIDTH + 3 * HG_WIDTH]
    g_s[...] = jnp.log(f)
    gate_s[...] = _silu(proj[:, POOL_WIDTH + 3 * HG_WIDTH:])


def _mix_out(x_ref, mod_ref, src, gpost_ref, gffn_ref, gout_ref, wout_ref, wrh_ref, wrl_ref, m2_ref,
             st_s, o_s, x1_ref, h2_ref, lg_ref):
    tt = TIME_TILE
    q_s, k_s, v_s, g_s, gate_s, yp_s = src
    masks, eye = _level_masks()
    m2 = m2_ref[...]

    states = [st_s[hd] for hd in range(HG_HEADS)]
    for c in range(tt // CHUNK):
        rows = slice(c * CHUNK, (c + 1) * CHUNK)
        g_hi, g_lo = _split2(g_s[rows, :])
        e_all = _dot(m2, jnp.concatenate([g_hi, g_lo], axis=0))
        for hd in range(0, HG_HEADS, 2):
            lane0 = hd * HG_HEAD_DIM
            lanes = slice(lane0, lane0 + 2 * HG_HEAD_DIM)
            o, states[hd:hd + 2] = _hgrn_chunk_pair(q_s[rows, lanes], k_s[rows, lanes], v_s[rows, lanes], e_all,
                                                    lane0, states[hd:hd + 2], masks, eye)
            o_s[rows, lanes] = o
    for hd in range(HG_HEADS):
        st_s[hd] = states[hd]
    yield

    o = o_s[...]
    os_ = []
    for hd in range(HG_HEADS):
        oh = o[:, hd * HG_HEAD_DIM:(hd + 1) * HG_HEAD_DIM]
        os_.append(oh * lax.rsqrt(jnp.mean(oh * oh, axis=-1, keepdims=True) + EPS) * gout_ref[...])
    o_n = jnp.concatenate(os_, axis=1) * gate_s[...]

    mix = (_dot(yp_s[...].astype(BF16), wout_ref[0:POOL_WIDTH, :])
           + _dot(o_n.astype(BF16), wout_ref[POOL_WIDTH:, :]))
    yield
    _, _, gt1, sh2, sc2, _ = _mods(mod_ref[0])
    x1 = x_ref[0] + gt1 * _rms(mix, gpost_ref[...])
    h2 = _rms(x1, gffn_ref[...]) * (1.0 + sc2) + sh2
    x1_ref[0] = x1
    h2_ref[0] = _pack_pair(h2[:, :ROW_WORDS], h2[:, ROW_WORDS:])
    lg_ref[0] = _router_logits(h2, wrh_ref[...], wrl_ref[...])


def _mix_prompt_kernel(x_ref, xn_ref, mod_ref, modn_ref, gpre_ref, gpost_ref, gffn_ref, win_ref, wpool_ref,
                       pscale_ref, lbl_ref, gout_ref, wout_ref, wrh_ref, wrl_ref, m2_ref,
                       x1_ref, h2_ref, lg_ref, pool_ref, st_ref,
                       st_s, ubuf, o_s, *sets, n_t):
    i = pl.program_id(0)
    t = i % n_t
    t_next = jnp.minimum(i + 1, pl.num_programs(0) - 1) % n_t
    mix_in = functools.partial(_mix_in, gpre_ref=gpre_ref, win_ref=win_ref, wpool_ref=wpool_ref,
                               pscale_ref=pscale_ref, lbl_ref=lbl_ref, ubuf=ubuf)
    cur = [s.at[i % 2] for s in sets]
    nxt = [s.at[(i + 1) % 2] for s in sets]

    def clear_history():
        ubuf[0:16, :] = jnp.zeros((16, POOL_WIDTH), F32)

    @pl.when(t == 0)
    def _():
        st_s[...] = jnp.zeros_like(st_s)

    @pl.when(i == 0)
    def _():
        clear_history()
        for _ in mix_in(x_ref, mod_ref, 0, dst=cur):
            pass

    pl.when((i + 1) % n_t == 0)(clear_history)

    finish = _mix_out(x_ref, mod_ref, cur, gpost_ref, gffn_ref, gout_ref, wout_ref, wrh_ref, wrl_ref,
                      m2_ref, st_s, o_s, x1_ref, h2_ref, lg_ref)
    prepare = mix_in(xn_ref, modn_ref, t_next, dst=nxt)
    for _ in itertools.zip_longest(finish, prepare):
        pass

    @pl.when((i + 1) % n_t == n_t - 1)
    def _():
        pool_ref[0] = ubuf[16 - POOL_HIST:16, :]

    @pl.when(t == n_t - 1)
    def _():
        for hd in range(HG_HEADS):
            st_ref[0, hd] = st_s[hd].T


def _full(shape):
    nd = len(shape)
    return pl.BlockSpec(shape, lambda *_: (0,) * nd)


def _mix_prompt(x, mod, gpre, gpost, gffn, win, wpool, pscale, lbl, gout, wout, wrh, wrl, m2):
    b, l, d = x.shape
    tt = TIME_TILE
    n_t = l // tt
    steps = b * n_t
    tile = lambda i: (i // n_t, i % n_t, 0)
    per_b = lambda i: (i // n_t, 0, 0)
    ahead = lambda i: jnp.minimum(i + 1, steps - 1)
    tile_next = lambda i: (ahead(i) // n_t, ahead(i) % n_t, 0)
    per_b_next = lambda i: (ahead(i) // n_t, 0, 0)
    recurrence_inputs = [pltpu.VMEM((2, tt, HG_WIDTH), F32)] * 5 + [pltpu.VMEM((2, tt, POOL_WIDTH), F32)]
    return pl.pallas_call(
        functools.partial(_mix_prompt_kernel, n_t=n_t),
        out_shape=(jax.ShapeDtypeStruct((b, l, d), F32),
                   jax.ShapeDtypeStruct((b, l, ROW_WORDS), jnp.uint32),
                   jax.ShapeDtypeStruct((b, l, LANES), F32),
                   jax.ShapeDtypeStruct((b, POOL_HIST, POOL_WIDTH), F32),
                   jax.ShapeDtypeStruct((b, HG_HEADS, HG_HEAD_DIM, HG_HEAD_DIM), F32)),
        grid=(steps,),
        in_specs=[pl.BlockSpec((1, tt, d), tile), pl.BlockSpec((1, tt, d), tile_next),
                  pl.BlockSpec((1, 1, 6 * d), per_b), pl.BlockSpec((1, 1, 6 * d), per_b_next),
                  _full((1, d)), _full((1, d)), _full((1, d)),
                  _full(win.shape), _full(wpool.shape), _full((1, POOL_WIDTH)),
                  _full(lbl.shape), _full((1, HG_HEAD_DIM)), _full(wout.shape),
                  _full(wrh.shape), _full(wrl.shape), _full(m2.shape)],
        out_specs=(pl.BlockSpec((1, tt, d), tile),
                   pl.BlockSpec((1, tt, ROW_WORDS), tile),
                   pl.BlockSpec((1, tt, LANES), tile),
                   pl.BlockSpec((1, POOL_HIST, POOL_WIDTH), per_b),
                   pl.BlockSpec((1, HG_HEADS, HG_HEAD_DIM, HG_HEAD_DIM), lambda i: (i // n_t, 0, 0, 0))),
        scratch_shapes=[pltpu.VMEM((HG_HEADS, HG_HEAD_DIM, HG_HEAD_DIM), F32),
                        pltpu.VMEM((tt + 16, POOL_WIDTH), F32),
                        pltpu.VMEM((tt, HG_WIDTH), F32)] + recurrence_inputs,
        compiler_params=pltpu.CompilerParams(dimension_semantics=("arbitrary",),
                                             vmem_limit_bytes=VMEM_LIMIT),
        name="mix_prompt",
    )(x, x, mod, mod, gpre, gpost, gffn, win, wpool, pscale, lbl, gout, wout, wrh, wrl, m2)


def _mix_sample_in_kernel(x_ref, mod_ref, gpre_ref, win_ref, wpool_ref, pscale_ref, lbl_ref, hist_ref,
                          ypool_ref, npool_ref, ft_ref, qt_ref, v_ref, gate_ref):
    xt = x_ref[...]
    sh1, sc1 = _mods(mod_ref[...])[:2]
    h = _rms(xt, gpre_ref[...]) * (1.0 + sc1) + sh1
    proj = _dot(h.astype(BF16), win_ref[...])
    u = proj[:, :POOL_WIDTH]
    row = lax.broadcasted_iota(jnp.int32, (hist_ref.shape[0], POOL_HIST, POOL_GROUP), 1)
    ys = []
    for g, w in enumerate(POOL_WINDOWS):
        sl = slice(g * POOL_GROUP, (g + 1) * POOL_GROUP)
        past = jnp.sum(jnp.where(row >= POOL_HIST - (w - 1), hist_ref[:, :, sl], 0.0), axis=1)
        ug = u[:, sl]
        d = (past + ug) / float(w) - ug
        ys.append(_dot(d.astype(BF16), wpool_ref[g]))
    ypool_ref[...] = jnp.concatenate(ys, axis=1) * pscale_ref[...]
    npool_ref[:, 0:POOL_HIST - 1, :] = hist_ref[:, 1:POOL_HIST, :]
    npool_ref[:, POOL_HIST - 1, :] = u

    lb = _forget_lower_bound(lbl_ref[...])
    f = lb + (1.0 - lb) * jax.nn.sigmoid(proj[:, POOL_WIDTH + HG_WIDTH:POOL_WIDTH + 2 * HG_WIDTH])
    ft_ref[...] = f.T
    qt_ref[...] = _silu(proj[:, POOL_WIDTH:POOL_WIDTH + HG_WIDTH]).T
    v_ref[...] = proj[:, POOL_WIDTH + 2 * HG_WIDTH:POOL_WIDTH + 3 * HG_WIDTH]
    gate_ref[...] = _silu(proj[:, POOL_WIDTH + 3 * HG_WIDTH:])


def _mix_sample_state_kernel(s_ref, ft_ref, qt_ref, v_ref, snew_ref, o_ref):
    i = pl.program_id(0)
    lane = lax.broadcasted_iota(jnp.int32, (HG_HEAD_DIM, ft_ref.shape[1]), 1)
    for j in range(SAMPLE_STATE_BLOCK):
        mine = lane == i * SAMPLE_STATE_BLOCK + j
        for hd in range(HG_HEADS):
            r0 = hd * HG_HEAD_DIM
            f = jnp.sum(jnp.where(mine, ft_ref[r0:r0 + HG_HEAD_DIM, :], 0.0), axis=1, keepdims=True)
            q = jnp.sum(jnp.where(mine, qt_ref[r0:r0 + HG_HEAD_DIM, :], 0.0), axis=1, keepdims=True)
            v = v_ref[j:j + 1, r0:r0 + HG_HEAD_DIM]
            s_new = f * s_ref[j, hd] + (1.0 - f) * v
            snew_ref[j, hd] = s_new
            o_ref[j:j + 1, r0:r0 + HG_HEAD_DIM] = jnp.sum(q * s_new, axis=0, keepdims=True)


def _mix_sample_out_kernel(x_ref, mod_ref, o_ref, gate_ref, ypool_ref, gout_ref, wout_ref, gpost_ref,
                           gffn_ref, wrh_ref, wrl_ref, x1_ref, h2_ref, lg_ref):
    _, _, gt1, sh2, sc2, _ = _mods(mod_ref[...])
    o = o_ref[...]
    os_ = []
    for hd in range(HG_HEADS):
        oh = o[:, hd * HG_HEAD_DIM:(hd + 1) * HG_HEAD_DIM]
        os_.append(oh * lax.rsqrt(jnp.mean(oh * oh, axis=-1, keepdims=True) + EPS) * gout_ref[...])
    o_n = jnp.concatenate(os_, axis=1) * gate_ref[...]
    mix = (_dot(ypool_ref[...].astype(BF16), wout_ref[0:POOL_WIDTH, :])
           + _dot(o_n.astype(BF16), wout_ref[POOL_WIDTH:, :]))
    x1 = x_ref[...] + gt1 * _rms(mix, gpost_ref[...])
    h2 = _rms(x1, gffn_ref[...]) * (1.0 + sc2) + sh2
    x1_ref[...] = x1
    h2_ref[...] = h2.astype(BF16)
    lg_ref[...] = _router_logits(h2, wrh_ref[...], wrl_ref[...])


def _mix_sample(x, mod, hist, state, gpre, gpost, gffn, win, wpool, pscale, lbl, gout, wout, wrh, wrl):
    b = x.shape[0]
    bb = SAMPLE_STATE_BLOCK
    cp = pltpu.CompilerParams(vmem_limit_bytes=VMEM_LIMIT)
    ypool, npool, ft, qt, v, gate = pl.pallas_call(
        _mix_sample_in_kernel,
        out_shape=(jax.ShapeDtypeStruct((b, POOL_WIDTH), F32),
                   jax.ShapeDtypeStruct((b, POOL_HIST, POOL_WIDTH), F32),
                   jax.ShapeDtypeStruct((HG_WIDTH, b), F32),
                   jax.ShapeDtypeStruct((HG_WIDTH, b), F32),
                   jax.ShapeDtypeStruct((b, HG_WIDTH), F32),
                   jax.ShapeDtypeStruct((b, HG_WIDTH), F32)),
        compiler_params=cp,
        name="mix_sample_in",
    )(x, mod, gpre, win, wpool, pscale, lbl, hist)

    s_spec = pl.BlockSpec((bb, HG_HEADS, HG_HEAD_DIM, HG_HEAD_DIM), lambda i: (i, 0, 0, 0))
    col_spec = _full((HG_WIDTH, b))
    row_spec = pl.BlockSpec((bb, HG_WIDTH), lambda i: (i, 0))
    s_new, o = pl.pallas_call(
        _mix_sample_state_kernel,
        out_shape=(jax.ShapeDtypeStruct(state.shape, F32), jax.ShapeDtypeStruct((b, HG_WIDTH), F32)),
        grid=(b // bb,),
        in_specs=[s_spec, col_spec, col_spec, row_spec],
        out_specs=(s_spec, row_spec),
        compiler_params=pltpu.CompilerParams(dimension_semantics=("parallel",), vmem_limit_bytes=VMEM_LIMIT),
        name="mix_sample_state",
    )(state, ft, qt, v)

    x1, h2, lg = pl.pallas_call(
        _mix_sample_out_kernel,
        out_shape=(jax.ShapeDtypeStruct((b, D_MODEL), F32),
                   jax.ShapeDtypeStruct((b, D_MODEL), BF16),
                   jax.ShapeDtypeStruct((b, LANES), F32)),
        compiler_params=cp,
        name="mix_sample_out",
    )(x, mod, o, gate, ypool, gout, wout, gpost, gffn, wrh, wrl)
    return x1, h2, lg, npool, s_new


def _first_max(cur, idx, big):
    m = jnp.max(cur, axis=0, keepdims=True)
    first = jnp.min(jnp.where(cur == m, idx, big), axis=0, keepdims=True)
    return idx == first


def _route(logit, bias):
    n = logit.shape[1]
    scores = jax.nn.sigmoid(logit)
    sel = scores + bias
    neg = -jnp.inf

    sub = lax.broadcasted_iota(jnp.int32, (GROUP_SIZE, n), 0)
    gscore = []
    for g in range(N_GROUPS):
        sg = sel[g * GROUP_SIZE:(g + 1) * GROUP_SIZE, :]
        m1 = jnp.max(sg, axis=0, keepdims=True)
        rest = jnp.where(_first_max(sg, sub, GROUP_SIZE), neg, sg)
        gscore.append(m1 + jnp.max(rest, axis=0, keepdims=True))
    cur = jnp.concatenate(gscore, axis=0)
    gidx = lax.broadcasted_iota(jnp.int32, (N_GROUPS, n), 0)
    gmask = jnp.zeros((N_GROUPS, n), jnp.bool_)
    for _ in range(TOPK_GROUPS):
        pick = _first_max(cur, gidx, N_GROUPS)
        gmask = gmask | pick
        cur = jnp.where(pick, neg, cur)
    emask = jnp.concatenate(
        [jnp.broadcast_to(gmask[g:g + 1, :], (GROUP_SIZE, n)) for g in range(N_GROUPS)], axis=0)

    cur = jnp.where(emask, sel, neg)
    eidx = lax.broadcasted_iota(jnp.int32, (N_EXPERTS, n), 0)
    picks = []
    for _ in range(TOP_K):
        pick = _first_max(cur, eidx, N_EXPERTS)
        picks.append(pick)
        cur = jnp.where(pick, neg, cur)
    return scores, picks


def _any(masks):
    return functools.reduce(jnp.logical_or, masks)


def _router_kernel(lg_ref, bias_ref, wd_ref):
    n = lg_ref.shape[0]
    scores, picks = _route(lg_ref[...].T[0:N_EXPERTS, :], bias_ref[0:N_EXPERTS, :])
    chosen = _any(picks)
    wsum = jnp.sum(jnp.where(chosen, scores, 0.0), axis=0, keepdims=True)
    wd = jnp.where(chosen, scores / wsum * ROUTED_SCALE, 0.0)
    wd_ref[...] = jnp.concatenate([wd, jnp.zeros((LANES - N_EXPERTS, n), F32)], axis=0).T


def _router(logits, bias_col):
    t = logits.shape[0]
    tile = min(ROUTER_TILE, t)
    return pl.pallas_call(
        _router_kernel,
        out_shape=jax.ShapeDtypeStruct((t, LANES), F32),
        grid=(t // tile,),
        in_specs=[pl.BlockSpec((tile, LANES), lambda i: (i, 0)), _full(bias_col.shape)],
        out_specs=pl.BlockSpec((tile, LANES), lambda i: (i, 0)),
        compiler_params=pltpu.CompilerParams(dimension_semantics=("parallel",), vmem_limit_bytes=VMEM_LIMIT),
        name="router",
    )(logits, bias_col)


def _plan_kernel(lg_ref, bias_ref, upper_ref, dest_ref, wsel_ref, cnt_ref, loff_ref):
    scores, picks = _route(lg_ref[...].T[0:N_EXPERTS, :], bias_ref[0:N_EXPERTS, :])
    chosen = _any(picks)
    chosen_f = jnp.where(chosen, 1.0, 0.0)
    cnt = jnp.sum(chosen_f, axis=1, keepdims=True)
    units = jnp.floor((cnt + (RUN_ALIGN - 1)) / RUN_ALIGN)
    ei = lax.broadcasted_iota(jnp.int32, (N_EXPERTS, N_EXPERTS), 0)
    ej = lax.broadcasted_iota(jnp.int32, (N_EXPERTS, N_EXPERTS), 1)
    before = jnp.where(ej < ei, 1.0, 0.0).astype(BF16)
    loff = RUN_ALIGN * _dot(before, jnp.broadcast_to(units, (N_EXPERTS, LANES)).astype(BF16))[:, 0:1]
    slot = loff + _dot(chosen_f.astype(BF16), upper_ref[...])
    wsum = jnp.sum(jnp.where(chosen, scores, 0.0), axis=0, keepdims=True)
    pick_sum = lambda v: jnp.concatenate(
        [jnp.sum(jnp.where(p, v, 0.0), axis=0, keepdims=True) for p in picks], axis=0)
    dest_ref[0] = (pick_sum(slot) * ROW_SUB).astype(jnp.int32)
    wsel_ref[0] = pick_sum(scores) / wsum * ROUTED_SCALE
    cnt_ref[0] = jnp.broadcast_to(cnt, (N_EXPERTS, LANES))
    loff_ref[0] = jnp.broadcast_to(loff, (N_EXPERTS, LANES))


def _plan(logits, bias_col, upper):
    t = logits.shape[0]
    nw = t // PLAN_WINDOW
    per_w = pl.BlockSpec((1, N_EXPERTS, LANES), lambda w: (w, 0, 0))
    picks = pl.BlockSpec((1, TOP_K, PLAN_WINDOW), lambda w: (w, 0, 0))
    return pl.pallas_call(
        _plan_kernel,
        out_shape=(jax.ShapeDtypeStruct((nw, TOP_K, PLAN_WINDOW), jnp.int32),
                   jax.ShapeDtypeStruct((nw, TOP_K, PLAN_WINDOW), F32),
                   jax.ShapeDtypeStruct((nw, N_EXPERTS, LANES), F32),
                   jax.ShapeDtypeStruct((nw, N_EXPERTS, LANES), F32)),
        grid=(nw,),
        in_specs=[pl.BlockSpec((PLAN_WINDOW, LANES), lambda w: (w, 0)), _full(bias_col.shape), _full(upper.shape)],
        out_specs=(picks, picks, per_w, per_w),
        compiler_params=pltpu.CompilerParams(dimension_semantics=("parallel",), vmem_limit_bytes=VMEM_LIMIT),
        name="plan",
    )(logits, bias_col, upper)


def _slot_rows(slot):
    return pl.ds(pl.multiple_of(slot * ROW_SUB, ROW_SUB), ROW_SUB)


def _table_copies(table_ref, flat_s, sem):
    return [pltpu.make_async_copy(table_ref.at[0, k], flat_s.at[pl.ds(k * PLAN_WINDOW, PLAN_WINDOW)], sem)
            for k in range(TOP_K)]


def _rows_at(offset):
    return pl.ds(pl.multiple_of(offset, ROW_SUB), ROW_SUB)


def _dispatch_kernel(dest_ref, h_ref, x_ref, dest_s, rows_s, sem):
    plan_copies = _table_copies(dest_ref, dest_s, sem)
    for c in plan_copies:
        c.start()
    x_ref[...] = jnp.zeros_like(x_ref)
    for j in range(ROW_SUB):
        rows_s[pl.ds(j, PLAN_WINDOW, stride=ROW_SUB), :] = h_ref[:, j * LANES:(j + 1) * LANES]
    for c in plan_copies:
        c.wait()

    def body(i, carry):
        for u in range(SCATTER_UNROLL):
            t = i * SCATTER_UNROLL + u
            row = rows_s[_slot_rows(t), :]
            for k in range(TOP_K):
                x_ref[_rows_at(dest_s[k * PLAN_WINDOW + t]), :] = row
        return carry

    lax.fori_loop(0, PLAN_WINDOW // SCATTER_UNROLL, body, 0)


def _dispatch(dest_w, h2):
    nw = dest_w.shape[0]
    region = REGION_ROWS * ROW_SUB
    return pl.pallas_call(
        _dispatch_kernel,
        out_shape=jax.ShapeDtypeStruct((nw * region, LANES), jnp.uint32),
        grid=(nw,),
        in_specs=[pl.BlockSpec((1, TOP_K, PLAN_WINDOW), lambda w: (w, 0, 0)),
                  pl.BlockSpec((PLAN_WINDOW, ROW_WORDS), lambda w: (w, 0))],
        out_specs=pl.BlockSpec((region, LANES), lambda w: (w, 0)),
        scratch_shapes=[pltpu.SMEM((TOP_K * PLAN_WINDOW,), jnp.int32),
                        pltpu.VMEM((PLAN_WINDOW * ROW_SUB, LANES), jnp.uint32),
                        pltpu.SemaphoreType.DMA],
        compiler_params=pltpu.CompilerParams(dimension_semantics=("arbitrary",), vmem_limit_bytes=VMEM_LIMIT),
        name="dispatch",
    )(dest_w, h2)


def _unpack_rows(ref, n):
    parts = [_unpack_pair(ref[pl.ds(j, n, stride=ROW_SUB), :]) for j in range(ROW_SUB)]
    return jnp.concatenate([p[0] for p in parts] + [p[1] for p in parts], axis=1)


def _ffn_kernel(gq_ref, gs_ref, to_ref, tn_ref, x_hbm, w1_ref, w3_ref, w2_ref,
                hs_ref, wds_ref, x1s_ref, gt2s_ref, gpost_ref, ws1_ref, ws3_ref, ws2_ref, y_hbm, ys_ref,
                w13_s, w2_s, xbuf, ybuf, rsem, wsem, acc_s):
    del x_hbm
    e = pl.program_id(0)
    n_groups = gq_ref[N_EXPERTS]
    block = RUN_ALIGN * ROW_SUB
    tile = FFN_TILE * ROW_SUB

    def go(cp, start):
        if start:
            cp.start()
        else:
            cp.wait()

    def fetch(q, start):
        def one(j):
            src = y_hbm.at[pl.ds(pl.multiple_of(to_ref[gs_ref[q] + j] * block, block), tile)]
            go(pltpu.make_async_copy(src, xbuf.at[q % FFN_RING, pl.ds(j * tile, tile)], rsem.at[q % FFN_RING]), start)

        size = gs_ref[q + 1] - gs_ref[q]

        @pl.when(size == FFN_GROUP)
        def _():
            for j in range(FFN_GROUP):
                one(j)

        @pl.when(size < FFN_GROUP)
        def _():
            for j in range(FFN_GROUP - 1):
                pl.when(j < size)(functools.partial(one, j))

    def writeback(q, start):
        most = FFN_TILE // RUN_ALIGN - 1

        def blocks(j, b, n):
            dst = pl.multiple_of(to_ref[gs_ref[q] + j] * block, block) + b * block
            go(pltpu.make_async_copy(ybuf.at[q % FFN_RING, pl.ds(j * tile + b * block, n * block)],
                                     y_hbm.at[pl.ds(dst, n * block)], wsem.at[q % FFN_RING]), start)

        for j in range(FFN_GROUP):
            item = gs_ref[q] + j
            n = jnp.where(item < gs_ref[q + 1], tn_ref[item], 0)
            pl.when(n >= most)(functools.partial(blocks, j, 0, most))
            pl.when(n > most)(functools.partial(blocks, j, most, 1))

            @pl.when((n > 0) & (n < most))
            def _():
                for b in range(most - 1):
                    pl.when(b < n)(functools.partial(blocks, j, b, 1))

    @pl.when(e == 0)
    def _():
        xbuf[...] = jnp.zeros_like(xbuf)
        for d in range(FFN_RING - 1):
            pl.when(d < n_groups)(functools.partial(fetch, d, True))

    w13_s[:, 0:D_EXPERT] = w1_ref[0].astype(BF16)
    w13_s[:, D_EXPERT:] = w3_ref[0].astype(BF16)
    w2_s[...] = w2_ref[0].astype(BF16)

    xs = hs_ref[...]

    @pl.when(e == 0)
    def _():
        sh = _silu(_dot(xs, ws1_ref[...].astype(BF16))) * _dot(xs, ws3_ref[...].astype(BF16))
        acc_s[...] = _dot(sh.astype(BF16), ws2_ref[...].astype(BF16))

    lane = lax.broadcasted_iota(jnp.int32, wds_ref.shape, 1)
    wcol = jnp.sum(jnp.where(lane == e, wds_ref[...], 0.0), axis=1, keepdims=True)
    rows = FFN_GROUP * FFN_TILE

    def ffn(x, ride):
        n = 0 if x is None else x.shape[0]
        if ride:
            x = xs if x is None else jnp.concatenate([x, xs], axis=0)
        h = _dot(x, w13_s[...])
        y = _dot((_silu(h[:, :D_EXPERT]) * h[:, D_EXPERT:]).astype(BF16), w2_s[...])
        if ride:
            acc_s[...] += y[n:] * wcol
        return y[:n] if n else None

    def group(q, ride):
        @pl.when(q + FFN_RING - 1 < n_groups)
        def _():
            fetch(q + FFN_RING - 1, True)

        @pl.when(q >= FFN_RING)
        def _():
            writeback(q - FFN_RING, False)

        fetch(q, False)
        y = ffn(_unpack_rows(xbuf.at[q % FFN_RING], rows).astype(BF16), ride)
        out = ybuf.at[q % FFN_RING]
        for j in range(ROW_SUB):
            out[pl.ds(j, rows, stride=ROW_SUB), :] = _pack_pair(
                y[:, j * LANES:(j + 1) * LANES], y[:, ROW_WORDS + j * LANES:ROW_WORDS + (j + 1) * LANES])
        writeback(q, True)

    first, end = gq_ref[e], gq_ref[e + 1]
    pl.when(first < end)(functools.partial(group, first, True))
    pl.when(first == end)(functools.partial(ffn, None, True))

    def group_body(q, carry):
        group(q, False)
        return carry

    lax.fori_loop(first + 1, end, group_body, 0)

    @pl.when(e == pl.num_programs(0) - 1)
    def _():
        ys_ref[...] = x1s_ref[...] + gt2s_ref[...] * _rms(acc_s[...], gpost_ref[...])

    @pl.when(e == pl.num_programs(0) - 1)
    def _():
        for d in range(FFN_RING):
            @pl.when(n_groups - 1 - d >= 0)
            def _():
                writeback(n_groups - 1 - d, False)


def _expert_ffn(group_of_expert, group_start, item_off, item_blocks, x_slots, w1, w3, w2,
                h2_s, wd_s, x1_s, gt2_s, gpost, ws1, ws3, ws2):
    of_expert = lambda e, *_: (e, 0, 0)
    whole = lambda a: pl.BlockSpec(a.shape, lambda e, *_: (0,) * a.ndim)
    tiles = pltpu.VMEM((FFN_RING, FFN_GROUP * FFN_TILE * ROW_SUB, LANES), jnp.uint32)
    sems = pltpu.SemaphoreType.DMA((FFN_RING,))
    dense = (h2_s, wd_s, x1_s, gt2_s, gpost, ws1, ws3, ws2)
    grid_spec = pltpu.PrefetchScalarGridSpec(
        num_scalar_prefetch=4,
        grid=(N_EXPERTS,),
        in_specs=[pl.BlockSpec(memory_space=pl.ANY),
                  pl.BlockSpec((1, D_MODEL, D_EXPERT), of_expert),
                  pl.BlockSpec((1, D_MODEL, D_EXPERT), of_expert),
                  pl.BlockSpec((1, D_EXPERT, D_MODEL), of_expert)] + [whole(a) for a in dense],
        out_specs=(pl.BlockSpec(memory_space=pl.ANY), whole(x1_s)),
        scratch_shapes=[pltpu.VMEM((D_MODEL, 2 * D_EXPERT), BF16), pltpu.VMEM((D_EXPERT, D_MODEL), BF16),
                        tiles, tiles, sems, sems, pltpu.VMEM(x1_s.shape, F32)])
    return pl.pallas_call(
        _ffn_kernel,
        out_shape=(jax.ShapeDtypeStruct(x_slots.shape, jnp.uint32), jax.ShapeDtypeStruct(x1_s.shape, F32)),
        grid_spec=grid_spec,
        input_output_aliases={4: 0},
        compiler_params=pltpu.CompilerParams(dimension_semantics=("arbitrary",), vmem_limit_bytes=VMEM_LIMIT),
        name="expert_ffn",
    )(group_of_expert, group_start, item_off, item_blocks, x_slots, w1, w3, w2, *dense)


def _work_items(cnt, loff):
    nw = cnt.shape[0]
    cnt = cnt.astype(jnp.int32).T.reshape(-1)
    loff = loff.astype(jnp.int32).T.reshape(-1)
    region = jnp.tile(jnp.arange(nw, dtype=jnp.int32) * REGION_ROWS, N_EXPERTS)
    n_items = nw * N_EXPERTS + (nw * PLAN_WINDOW * TOP_K) // FFN_TILE
    tiles = (cnt + FFN_TILE - 1) // FFN_TILE
    ends = jnp.cumsum(tiles)
    starts = ends - tiles
    i = jnp.arange(n_items + FFN_GROUP, dtype=jnp.int32)[:, None]
    mine = (starts[None, :] <= i) & (i < ends[None, :])
    of_pair = lambda v: jnp.sum(jnp.where(mine, v[None, :], 0), axis=1)
    r = i[:, 0] - of_pair(starts)
    row = of_pair(region + loff) + r * FFN_TILE
    own = jnp.clip(of_pair(cnt) - r * FFN_TILE, 0, FFN_TILE)

    zero = jnp.zeros((1,), jnp.int32)
    item_end = ends.reshape(N_EXPERTS, nw)[:, -1]
    item_start = jnp.concatenate([zero, item_end[:-1]])
    group_end = jnp.cumsum((item_end - item_start + FFN_GROUP - 1) // FFN_GROUP)
    group_first = jnp.concatenate([zero, group_end[:-1]])
    q = jnp.arange(N_EXPERTS + n_items // FFN_GROUP + 1, dtype=jnp.int32)[:, None]
    has = (group_first[None, :] <= q) & (q < group_end[None, :])
    of_expert = lambda v: jnp.sum(jnp.where(has, v[None, :], 0), axis=1)
    group_start = jnp.where(q[:, 0] < group_end[-1],
                            of_expert(item_start) + (q[:, 0] - of_expert(group_first)) * FFN_GROUP, item_end[-1])
    return (jnp.concatenate([zero, group_end]), group_start, row // RUN_ALIGN, (own + RUN_ALIGN - 1) // RUN_ALIGN)


def _combine_kernel(dest_ref, wsel_ref, y_hbm, h_ref, x1_ref, gt2_ref, gpost_ref, ws13_ref, ws2_ref, out_ref,
                    dest_s, wsel_s, lo_s, hi_s, ybuf, sem, ysem):
    w, s = pl.program_id(0), pl.program_id(1)
    region = REGION_ROWS * ROW_SUB
    part = region // REGION_PARTS

    def region_copies(win):
        return [pltpu.make_async_copy(y_hbm.at[pl.ds(win * region + p * part, part)],
                                      ybuf.at[win % 2, pl.ds(p * part, part)], ysem.at[win % 2, p])
                for p in range(REGION_PARTS)]

    @pl.when(s == 0)
    def _():
        copies = _table_copies(dest_ref, dest_s, sem.at[0]) + _table_copies(wsel_ref, wsel_s, sem.at[1])
        for c in copies:
            c.start()

        @pl.when(w == 0)
        def _():
            for c in region_copies(w):
                c.start()

        @pl.when(w + 1 < pl.num_programs(0))
        def _():
            for c in region_copies(w + 1):
                c.start()

        for c in copies + region_copies(w):
            c.wait()

    y_ref = ybuf.at[w % 2]

    base = s * COMBINE_TILE

    def body(i, carry):
        for u in range(COMBINE_UNROLL):
            t = i * COMBINE_UNROLL + u
            acc_lo = jnp.zeros((ROW_SUB, LANES), F32)
            acc_hi = jnp.zeros((ROW_SUB, LANES), F32)
            for k in range(TOP_K):
                at = k * PLAN_WINDOW + base + t
                lo, hi = _unpack_pair(y_ref[_rows_at(dest_s[at]), :])
                w = wsel_s[at]
                acc_lo = acc_lo + w * lo
                acc_hi = acc_hi + w * hi
            lo_s[_slot_rows(t), :] = acc_lo
            hi_s[_slot_rows(t), :] = acc_hi
        return carry

    lax.fori_loop(0, COMBINE_TILE // COMBINE_UNROLL, body, 0)
    rows = lambda ref: [ref[pl.ds(j, COMBINE_TILE, stride=ROW_SUB), :] for j in range(ROW_SUB)]
    routed = jnp.concatenate(rows(lo_s) + rows(hi_s), axis=1)
    h_lo, h_hi = _unpack_pair(h_ref[...])
    x = jnp.concatenate([h_lo, h_hi], axis=1).astype(BF16)
    hs = _dot(x, ws13_ref[...])
    act = _silu(hs[:, :D_EXPERT]) * hs[:, D_EXPERT:]
    ff = routed + _dot(act.astype(BF16), ws2_ref[...])
    out_ref[...] = x1_ref[...] + gt2_ref[0] * _rms(ff, gpost_ref[...])


def _combine(dest_w, wsel_w, y_slots, h2, x1, gt2, tokens_per_gt2, gpost, ws13, ws2):
    nw = dest_w.shape[0]
    sub = PLAN_WINDOW // COMBINE_TILE
    plan = pl.BlockSpec((1, TOP_K, PLAN_WINDOW), lambda w, s: (w, 0, 0))
    tok = lambda w, s: (w * sub + s, 0)
    return pl.pallas_call(
        _combine_kernel,
        out_shape=jax.ShapeDtypeStruct(x1.shape, F32),
        grid=(nw, sub),
        in_specs=[plan, plan,
                  pl.BlockSpec(memory_space=pl.ANY),
                  pl.BlockSpec((COMBINE_TILE, ROW_WORDS), tok),
                  pl.BlockSpec((COMBINE_TILE, D_MODEL), tok),
                  pl.BlockSpec((1, 1, D_MODEL), lambda w, s: ((w * PLAN_WINDOW) // tokens_per_gt2, 0, 0)),
                  _full((1, D_MODEL)), _full(ws13.shape), _full(ws2.shape)],
        out_specs=pl.BlockSpec((COMBINE_TILE, D_MODEL), tok),
        scratch_shapes=[pltpu.SMEM((TOP_K * PLAN_WINDOW,), jnp.int32),
                        pltpu.SMEM((TOP_K * PLAN_WINDOW,), F32),
                        pltpu.VMEM((COMBINE_TILE * ROW_SUB, LANES), F32),
                        pltpu.VMEM((COMBINE_TILE * ROW_SUB, LANES), F32),
                        pltpu.VMEM((2, REGION_ROWS * ROW_SUB, LANES), jnp.uint32),
                        pltpu.SemaphoreType.DMA((2,)), pltpu.SemaphoreType.DMA((2, REGION_PARTS))],
        compiler_params=pltpu.CompilerParams(dimension_semantics=("arbitrary", "arbitrary"),
                                             vmem_limit_bytes=VMEM_LIMIT),
        name="combine",
    )(dest_w, wsel_w, y_slots, h2, x1, gt2, gpost, ws13, ws2)


def kernel(x_prompt, x_sample, c_prompt, c_sample, state_pool, state_hgrn, w_ada, b_ada, g_pre_mix, g_post_mix,
           w_in, w_pool, pool_scale, lb_logits, g_out_norm, w_out, g_pre_ffn, g_post_ffn, w_router, router_bias,
           w_exp_gate, w_exp_up, w_exp_down, w_sh_gate, w_sh_up, w_sh_down):
    assert w_ada.shape[0] == 1 and lb_logits.shape[0] == 2, "single-layer trunk"
    bp, lp, d = x_prompt.shape
    bs = x_sample.shape[0]
    row = lambda a: a[0].reshape(1, -1)

    n_mod = bp + bs
    pad = (-n_mod) % 16
    c_all = jnp.concatenate([c_prompt, c_sample, jnp.zeros((pad, d), F32)], axis=0)
    mod = _ada(c_all, w_ada[0], b_ada)
    mod_p = mod[:bp].reshape(bp, 1, 6 * d)
    mod_s = mod[bp:n_mod]

    win = w_in[0].astype(BF16)
    wout = w_out[0].astype(BF16)
    wpool = w_pool[0].astype(BF16)
    wr = jnp.pad(w_router[0], ((0, 0), (0, LANES - N_EXPERTS)))
    wrh = wr.astype(BF16)
    wrl = (wr - wrh.astype(F32)).astype(BF16)
    m2 = jnp.asarray(_M2_NP, BF16)
    gpre, gpost, gffn, gffn_post = row(g_pre_mix), row(g_post_mix), row(g_pre_ffn), row(g_post_ffn)
    pscale, gout = row(pool_scale), row(g_out_norm)
    bias_col = jnp.pad(router_bias[0], (0, LANES - N_EXPERTS)).reshape(LANES, 1)

    x1_p, h2_p, lg_p, pool_p, st_p = _mix_prompt(x_prompt, mod_p, gpre, gpost, gffn, win, wpool, pscale,
                                                  lb_logits, gout, wout, wrh, wrl, m2)
    x1_s, h2_s, lg_s, pool_s, st_s = _mix_sample(x_sample[:, 0, :], mod_s, state_pool[0], state_hgrn[0],
                                                  gpre, gpost, gffn, win, wpool, pscale, lb_logits, gout, wout,
                                                  wrh, wrl)

    experts = (w_exp_gate[0], w_exp_up[0], w_exp_down[0], w_sh_gate[0], w_sh_up[0], w_sh_down[0])
    tp = bp * lp
    nw = tp // PLAN_WINDOW
    assert lp % PLAN_WINDOW == 0
    upper = jnp.asarray(np.triu(np.ones((PLAN_WINDOW, PLAN_WINDOW), np.float32), 1), BF16)
    dest, wsel, cnt, loff = _plan(lg_p.reshape(tp, LANES), bias_col, upper)
    dest_w, wsel_w = dest, wsel
    h2_rows = h2_p.reshape(tp, ROW_WORDS)
    x_slots = _dispatch(dest_w, h2_rows)
    wd_s = _router(lg_s, bias_col)
    y_slots, y_s = _expert_ffn(*_work_items(cnt[:, :, 0], loff[:, :, 0]), x_slots, *experts[:3],
                               h2_s, wd_s, x1_s, mod_s[:, 5 * d:], gffn_post, *experts[3:])
    ws13 = jnp.concatenate([experts[3], experts[4]], axis=1).astype(BF16)
    gt2_p = mod_p[:, :, 5 * d:]
    y_p = _combine(dest_w, wsel_w, y_slots, h2_rows, x1_p.reshape(tp, d), gt2_p, lp, gffn_post,
                   ws13, experts[5].astype(BF16))

    return (y_p.reshape(bp, lp, d), y_s.reshape(bs, 1, d), pool_p[None], st_p[None], pool_s[None], st_s[None])
```

```python
import functools
import itertools

import numpy as np
import jax
import jax.numpy as jnp
from jax import lax
from jax.experimental import pallas as pl
from jax.experimental.pallas import tpu as pltpu

F32 = jnp.float32
BF16 = jnp.bfloat16

D_MODEL = 1024
POOL_WIDTH = 512
POOL_WINDOWS = (2, 4, 8, 16)
POOL_GROUP = 128
POOL_HIST = 15
HG_WIDTH = 512
HG_HEADS = 4
HG_HEAD_DIM = 128
IN_WIDTH = POOL_WIDTH + 4 * HG_WIDTH
N_EXPERTS = 64
TOP_K = 8
N_GROUPS = 8
TOPK_GROUPS = 4
GROUP_SIZE = N_EXPERTS // N_GROUPS
D_EXPERT = 256
ROUTED_SCALE = 2.5
EPS = 1e-6

LANES = 128
CHUNK = 64
TIME_TILE = 512
ADA_TILE_N = 1024
ROUTER_TILE = 512
SAMPLE_STATE_BLOCK = 16
PLAN_WINDOW = 1024
RUN_ALIGN = 32
FFN_TILE = 160
REGION_ROWS = PLAN_WINDOW * TOP_K + N_EXPERTS * (RUN_ALIGN - 1) + FFN_TILE
ROW_WORDS = D_MODEL // 2
ROW_SUB = ROW_WORDS // LANES
COMBINE_TILE = 256
REGION_PARTS = 4
SCATTER_UNROLL = 4
COMBINE_UNROLL = 16
FFN_GROUP = 4
FFN_RING = 6
VMEM_LIMIT = 58 * 1024 * 1024

_LEVELS = (64, 32, 16, 8, 4, 2)
_BLK_CUM = 0


def _level_blocks():
    c = CHUNK
    i = np.arange(c)[:, None]
    s = np.arange(c)[None, :]
    blocks = [(s <= i)]
    index = {}
    for lvl in _LEVELS:
        ref = (i // lvl) * lvl + lvl // 2 - 1
        index[lvl] = len(blocks)
        blocks.append(s <= ref)
    m = np.concatenate(blocks, axis=0).astype(np.float32)
    return np.concatenate([m, m], axis=1), index


_M2_NP, _LEVEL_INDEX = _level_blocks()


def _dot(a, b):
    return jnp.dot(a, b, preferred_element_type=F32)


def _dot_nt(a, b):
    return lax.dot_general(a, b, (((1,), (1,)), ((), ())), preferred_element_type=F32)


def _split2(x):
    hi = x.astype(BF16)
    lo = (x - hi.astype(F32)).astype(BF16)
    return hi, lo


def _pack_pair(a, b):
    lo = lax.bitcast_convert_type(a.astype(BF16).astype(F32), jnp.uint32)
    hi = lax.bitcast_convert_type(b.astype(BF16).astype(F32), jnp.uint32)
    return (lo >> 16) | hi


def _unpack_pair(words):
    lo = lax.bitcast_convert_type(words << 16, F32)
    hi = lax.bitcast_convert_type(words & jnp.uint32(0xFFFF0000), F32)
    return lo, hi


def _silu(x):
    return x * jax.nn.sigmoid(x)


def _rms(x, g):
    return x * lax.rsqrt(jnp.mean(x * x, axis=-1, keepdims=True) + EPS) * g


def _mods(m):
    return [m[:, j * D_MODEL:(j + 1) * D_MODEL] for j in range(6)]


def _forget_lower_bound(lbl):
    mx = jnp.max(lbl, axis=0, keepdims=True)
    e = jnp.exp(lbl - mx)
    return e[0:1] / jnp.sum(e, axis=0, keepdims=True)


def _router_logits(h2, wrh, wrl):
    hi, lo = _split2(h2)
    return _dot(hi, wrh) + _dot(lo, wrh) + _dot(hi, wrl)


def _ada_kernel(c_ref, w_ref, b_ref, o_ref):
    a_hi, a_lo = _split2(_silu(c_ref[...]))
    w_hi, w_lo = _split2(w_ref[...])
    o_ref[...] = _dot(a_hi, w_hi) + _dot(a_lo, w_hi) + _dot(a_hi, w_lo) + b_ref[...]


def _ada(c_all, w_ada, b_ada):
    rows = c_all.shape[0]
    n = w_ada.shape[1]
    return pl.pallas_call(
        _ada_kernel,
        out_shape=jax.ShapeDtypeStruct((rows, n), F32),
        grid=(n // ADA_TILE_N,),
        in_specs=[pl.BlockSpec((rows, D_MODEL), lambda j: (0, 0)),
                  pl.BlockSpec((D_MODEL, ADA_TILE_N), lambda j: (0, j)),
                  pl.BlockSpec((1, ADA_TILE_N), lambda j: (0, j))],
        out_specs=pl.BlockSpec((rows, ADA_TILE_N), lambda j: (0, j)),
        compiler_params=pltpu.CompilerParams(dimension_semantics=("parallel",),
                                             vmem_limit_bytes=VMEM_LIMIT),
        name="ada",
    )(c_all, w_ada, b_ada)


def _level_masks():
    i = lax.broadcasted_iota(jnp.int32, (CHUNK, 2 * CHUNK), 0)
    j = lax.broadcasted_iota(jnp.int32, (CHUNK, 2 * CHUNK), 1) & (CHUNK - 1)
    masks = {}
    for lvl in _LEVELS:
        sh = lvl.bit_length() - 1
        same = (i >> sh) == (j >> sh)
        upper = ((i >> (sh - 1)) & 1) == 1
        lower = ((j >> (sh - 1)) & 1) == 0
        masks[lvl] = same & upper & lower
    return masks, i == j


def _by_head(m):
    z = jnp.zeros((m.shape[0], HG_HEAD_DIM), m.dtype)
    return jnp.concatenate([jnp.concatenate([m[:, :HG_HEAD_DIM], z], axis=1),
                            jnp.concatenate([z, m[:, HG_HEAD_DIM:]], axis=1)], axis=0)


def _hgrn_chunk_pair(q, k, v, e_all, lane0, sts, masks, eye):
    hd = HG_HEAD_DIM

    def blk(n):
        return e_all[n * CHUNK:(n + 1) * CHUNK, lane0:lane0 + 2 * hd]

    b = blk(_BLK_CUM)
    kb = k.astype(BF16)
    a = jnp.where(eye, _dot_nt(q.astype(BF16), _by_head(kb)), 0.0)
    for lvl in _LEVELS:
        d = b - blk(_LEVEL_INDEX[lvl])
        ql = (q * jnp.exp(jnp.minimum(d, 0.0))).astype(BF16)
        kl = (k * jnp.exp(jnp.minimum(-d, 0.0))).astype(BF16) if lvl > 2 else kb
        a = a + jnp.where(masks[lvl], _dot_nt(ql, _by_head(kl)), 0.0)
    st_both = jnp.concatenate([jnp.concatenate([sts[0], jnp.zeros_like(sts[0])], axis=1),
                               jnp.concatenate([jnp.zeros_like(sts[1]), sts[1]], axis=1)], axis=0).astype(BF16)
    o = _dot(a.astype(BF16), _by_head(v.astype(BF16))) + _dot_nt((q * jnp.exp(b)).astype(BF16), st_both)
    b_last = b[CHUNK - 1:CHUNK, :]
    k_end = (k * jnp.exp(b_last - b)).astype(BF16)
    decay = jnp.exp(b_last)
    new = []
    for h in range(2):
        lanes = slice(h * hd, (h + 1) * hd)
        new.append(sts[h] * decay[:, lanes] + _dot(v[:, lanes].T.astype(BF16), k_end[:, lanes]))
    return o, new


def _mix_in(x_ref, mod_ref, t, gpre_ref, win_ref, wpool_ref, pscale_ref, lbl_ref, ubuf, dst):
    tt = TIME_TILE
    q_s, k_s, v_s, g_s, gate_s, yp_s = dst
    xt = x_ref[0]
    sh1, sc1 = _mods(mod_ref[0])[:2]
    h = (_rms(xt, gpre_ref[...]) * (1.0 + sc1) + sh1).astype(BF16)
    yield
    proj = _dot(h, win_ref[...])
    yield

    u = proj[:, :POOL_WIDTH]
    ubuf[16:16 + tt, :] = u
    pos = (t * tt + lax.broadcasted_iota(jnp.int32, (tt, 1), 0) + 1).astype(F32)
    ys = []
    for g, w in enumerate(POOL_WINDOWS):
        s = ubuf[:, g * POOL_GROUP:(g + 1) * POOL_GROUP]
        off = 0
        for step in range(w.bit_length() - 1):
            sh = 1 << step
            s = s[sh:, :] + s[:-sh, :]
            off += sh
        ws = s[16 - off:16 - off + tt, :]
        cnt = jnp.minimum(pos, float(w))
        d = ws / cnt - u[:, g * POOL_GROUP:(g + 1) * POOL_GROUP]
        ys.append(_dot(d.astype(BF16), wpool_ref[g]))
    yp_s[...] = jnp.concatenate(ys, axis=1) * pscale_ref[...]
    ubuf[0:16, :] = ubuf[tt:tt + 16, :]

    lb = _forget_lower_bound(lbl_ref[...])
    f = lb + (1.0 - lb) * jax.nn.sigmoid(proj[:, POOL_WIDTH + HG_WIDTH:POOL_WIDTH + 2 * HG_WIDTH])
    q_s[...] = _silu(proj[:, POOL_WIDTH:POOL_WIDTH + HG_WIDTH])
    k_s[...] = 1.0 - f
    v_s[...] = proj[:, POOL_WIDTH + 2 * HG_WIDTH:POOL_WIDTH + 3 * HG_WIDTH]
    g_s[...] = jnp.log(f)
    gate_s[...] = _silu(proj[:, POOL_WIDTH + 3 * HG_WIDTH:])


def _mix_out(x_ref, mod_ref, src, gpost_ref, gffn_ref, gout_ref, wout_ref, wrh_ref, wrl_ref, m2_ref,
             st_s, o_s, x1_ref, h2_ref, lg_ref):
    tt = TIME_TILE
    q_s, k_s, v_s, g_s, gate_s, yp_s = src
    masks, eye = _level_masks()
    m2 = m2_ref[...]

    states = [st_s[hd] for hd in range(HG_HEADS)]
    for c in range(tt // CHUNK):
        rows = slice(c * CHUNK, (c + 1) * CHUNK)
        g_hi, g_lo = _split2(g_s[rows, :])
        e_all = _dot(m2, jnp.concatenate([g_hi, g_lo], axis=0))
        for hd in range(0, HG_HEADS, 2):
            lane0 = hd * HG_HEAD_DIM
            lanes = slice(lane0, lane0 + 2 * HG_HEAD_DIM)
            o, states[hd:hd + 2] = _hgrn_chunk_pair(q_s[rows, lanes], k_s[rows, lanes], v_s[rows, lanes], e_all,
                                                    lane0, states[hd:hd + 2], masks, eye)
            o_s[rows, lanes] = o
    for hd in range(HG_HEADS):
        st_s[hd] = states[hd]
    yield

    o = o_s[...]
    os_ = []
    for hd in range(HG_HEADS):
        oh = o[:, hd * HG_HEAD_DIM:(hd + 1) * HG_HEAD_DIM]
        os_.append(oh * lax.rsqrt(jnp.mean(oh * oh, axis=-1, keepdims=True) + EPS) * gout_ref[...])
    o_n = jnp.concatenate(os_, axis=1) * gate_s[...]

    mix = (_dot(yp_s[...].astype(BF16), wout_ref[0:POOL_WIDTH, :])
           + _dot(o_n.astype(BF16), wout_ref[POOL_WIDTH:, :]))
    yield
    _, _, gt1, sh2, sc2, _ = _mods(mod_ref[0])
    x1 = x_ref[0] + gt1 * _rms(mix, gpost_ref[...])
    h2 = _rms(x1, gffn_ref[...]) * (1.0 + sc2) + sh2
    x1_ref[0] = x1
    h2_ref[0] = _pack_pair(h2[:, :ROW_WORDS], h2[:, ROW_WORDS:])
    lg_ref[0] = _router_logits(h2, wrh_ref[...], wrl_ref[...])


def _mix_prompt_kernel(x_ref, xn_ref, mod_ref, modn_ref, gpre_ref, gpost_ref, gffn_ref, win_ref, wpool_ref,
                       pscale_ref, lbl_ref, gout_ref, wout_ref, wrh_ref, wrl_ref, m2_ref,
                       x1_ref, h2_ref, lg_ref, pool_ref, st_ref,
                       st_s, ubuf, o_s, *sets, n_t):
    i = pl.program_id(0)
    t = i % n_t
    t_next = jnp.minimum(i + 1, pl.num_programs(0) - 1) % n_t
    mix_in = functools.partial(_mix_in, gpre_ref=gpre_ref, win_ref=win_ref, wpool_ref=wpool_ref,
                               pscale_ref=pscale_ref, lbl_ref=lbl_ref, ubuf=ubuf)
    cur = [s.at[i % 2] for s in sets]
    nxt = [s.at[(i + 1) % 2] for s in sets]

    def clear_history():
        ubuf[0:16, :] = jnp.zeros((16, POOL_WIDTH), F32)

    @pl.when(t == 0)
    def _():
        st_s[...] = jnp.zeros_like(st_s)

    @pl.when(i == 0)
    def _():
        clear_history()
        for _ in mix_in(x_ref, mod_ref, 0, dst=cur):
            pass

    pl.when((i + 1) % n_t == 0)(clear_history)

    finish = _mix_out(x_ref, mod_ref, cur, gpost_ref, gffn_ref, gout_ref, wout_ref, wrh_ref, wrl_ref,
                      m2_ref, st_s, o_s, x1_ref, h2_ref, lg_ref)
    prepare = mix_in(xn_ref, modn_ref, t_next, dst=nxt)
    for _ in itertools.zip_longest(finish, prepare):
        pass

    @pl.when((i + 1) % n_t == n_t - 1)
    def _():
        pool_ref[0] = ubuf[16 - POOL_HIST:16, :]

    @pl.when(t == n_t - 1)
    def _():
        for hd in range(HG_HEADS):
            st_ref[0, hd] = st_s[hd].T


def _full(shape):
    nd = len(shape)
    return pl.BlockSpec(shape, lambda *_: (0,) * nd)


def _mix_prompt(x, mod, gpre, gpost, gffn, win, wpool, pscale, lbl, gout, wout, wrh, wrl, m2):
    b, l, d = x.shape
    tt = TIME_TILE
    n_t = l // tt
    steps = b * n_t
    tile = lambda i: (i // n_t, i % n_t, 0)
    per_b = lambda i: (i // n_t, 0, 0)
    ahead = lambda i: jnp.minimum(i + 1, steps - 1)
    tile_next = lambda i: (ahead(i) // n_t, ahead(i) % n_t, 0)
    per_b_next = lambda i: (ahead(i) // n_t, 0, 0)
    recurrence_inputs = [pltpu.VMEM((2, tt, HG_WIDTH), F32)] * 5 + [pltpu.VMEM((2, tt, POOL_WIDTH), F32)]
    return pl.pallas_call(
        functools.partial(_mix_prompt_kernel, n_t=n_t),
        out_shape=(jax.ShapeDtypeStruct((b, l, d), F32),
                   jax.ShapeDtypeStruct((b, l, ROW_WORDS), jnp.uint32),
                   jax.ShapeDtypeStruct((b, l, LANES), F32),
                   jax.ShapeDtypeStruct((b, POOL_HIST, POOL_WIDTH), F32),
                   jax.ShapeDtypeStruct((b, HG_HEADS, HG_HEAD_DIM, HG_HEAD_DIM), F32)),
        grid=(steps,),
        in_specs=[pl.BlockSpec((1, tt, d), tile), pl.BlockSpec((1, tt, d), tile_next),
                  pl.BlockSpec((1, 1, 6 * d), per_b), pl.BlockSpec((1, 1, 6 * d), per_b_next),
                  _full((1, d)), _full((1, d)), _full((1, d)),
                  _full(win.shape), _full(wpool.shape), _full((1, POOL_WIDTH)),
                  _full(lbl.shape), _full((1, HG_HEAD_DIM)), _full(wout.shape),
                  _full(wrh.shape), _full(wrl.shape), _full(m2.shape)],
        out_specs=(pl.BlockSpec((1, tt, d), tile),
                   pl.BlockSpec((1, tt, ROW_WORDS), tile),
                   pl.BlockSpec((1, tt, LANES), tile),
                   pl.BlockSpec((1, POOL_HIST, POOL_WIDTH), per_b),
                   pl.BlockSpec((1, HG_HEADS, HG_HEAD_DIM, HG_HEAD_DIM), lambda i: (i // n_t, 0, 0, 0))),
        scratch_shapes=[pltpu.VMEM((HG_HEADS, HG_HEAD_DIM, HG_HEAD_DIM), F32),
                        pltpu.VMEM((tt + 16, POOL_WIDTH), F32),
                        pltpu.VMEM((tt, HG_WIDTH), F32)] + recurrence_inputs,
        compiler_params=pltpu.CompilerParams(dimension_semantics=("arbitrary",),
                                             vmem_limit_bytes=VMEM_LIMIT),
        name="mix_prompt",
    )(x, x, mod, mod, gpre, gpost, gffn, win, wpool, pscale, lbl, gout, wout, wrh, wrl, m2)


def _mix_sample_in_kernel(x_ref, mod_ref, gpre_ref, win_ref, wpool_ref, pscale_ref, lbl_ref, hist_ref,
                          ypool_ref, npool_ref, ft_ref, qt_ref, v_ref, gate_ref):
    xt = x_ref[...]
    sh1, sc1 = _mods(mod_ref[...])[:2]
    h = _rms(xt, gpre_ref[...]) * (1.0 + sc1) + sh1
    proj = _dot(h.astype(BF16), win_ref[...])
    u = proj[:, :POOL_WIDTH]
    row = lax.broadcasted_iota(jnp.int32, (hist_ref.shape[0], POOL_HIST, POOL_GROUP), 1)
    ys = []
    for g, w in enumerate(POOL_WINDOWS):
        sl = slice(g * POOL_GROUP, (g + 1) * POOL_GROUP)
        past = jnp.sum(jnp.where(row >= POOL_HIST - (w - 1), hist_ref[:, :, sl], 0.0), axis=1)
        ug = u[:, sl]
        d = (past + ug) / float(w) - ug
        ys.append(_dot(d.astype(BF16), wpool_ref[g]))
    ypool_ref[...] = jnp.concatenate(ys, axis=1) * pscale_ref[...]
    npool_ref[:, 0:POOL_HIST - 1, :] = hist_ref[:, 1:POOL_HIST, :]
    npool_ref[:, POOL_HIST - 1, :] = u

    lb = _forget_lower_bound(lbl_ref[...])
    f = lb + (1.0 - lb) * jax.nn.sigmoid(proj[:, POOL_WIDTH + HG_WIDTH:POOL_WIDTH + 2 * HG_WIDTH])
    ft_ref[...] = f.T
    qt_ref[...] = _silu(proj[:, POOL_WIDTH:POOL_WIDTH + HG_WIDTH]).T
    v_ref[...] = proj[:, POOL_WIDTH + 2 * HG_WIDTH:POOL_WIDTH + 3 * HG_WIDTH]
    gate_ref[...] = _silu(proj[:, POOL_WIDTH + 3 * HG_WIDTH:])


def _mix_sample_state_kernel(s_ref, ft_ref, qt_ref, v_ref, snew_ref, o_ref):
    i = pl.program_id(0)
    lane = lax.broadcasted_iota(jnp.int32, (HG_HEAD_DIM, ft_ref.shape[1]), 1)
    for j in range(SAMPLE_STATE_BLOCK):
        mine = lane == i * SAMPLE_STATE_BLOCK + j
        for hd in range(HG_HEADS):
            r0 = hd * HG_HEAD_DIM
            f = jnp.sum(jnp.where(mine, ft_ref[r0:r0 + HG_HEAD_DIM, :], 0.0), axis=1, keepdims=True)
            q = jnp.sum(jnp.where(mine, qt_ref[r0:r0 + HG_HEAD_DIM, :], 0.0), axis=1, keepdims=True)
            v = v_ref[j:j + 1, r0:r0 + HG_HEAD_DIM]
            s_new = f * s_ref[j, hd] + (1.0 - f) * v
            snew_ref[j, hd] = s_new
            o_ref[j:j + 1, r0:r0 + HG_HEAD_DIM] = jnp.sum(q * s_new, axis=0, keepdims=True)


def _mix_sample_out_kernel(x_ref, mod_ref, o_ref, gate_ref, ypool_ref, gout_ref, wout_ref, gpost_ref,
                           gffn_ref, wrh_ref, wrl_ref, x1_ref, h2_ref, lg_ref):
    _, _, gt1, sh2, sc2, _ = _mods(mod_ref[...])
    o = o_ref[...]
    os_ = []
    for hd in range(HG_HEADS):
        oh = o[:, hd * HG_HEAD_DIM:(hd + 1) * HG_HEAD_DIM]
        os_.append(oh * lax.rsqrt(jnp.mean(oh * oh, axis=-1, keepdims=True) + EPS) * gout_ref[...])
    o_n = jnp.concatenate(os_, axis=1) * gate_ref[...]
    mix = (_dot(ypool_ref[...].astype(BF16), wout_ref[0:POOL_WIDTH, :])
           + _dot(o_n.astype(BF16), wout_ref[POOL_WIDTH:, :]))
    x1 = x_ref[...] + gt1 * _rms(mix, gpost_ref[...])
    h2 = _rms(x1, gffn_ref[...]) * (1.0 + sc2) + sh2
    x1_ref[...] = x1
    h2_ref[...] = h2.astype(BF16)
    lg_ref[...] = _router_logits(h2, wrh_ref[...], wrl_ref[...])


def _mix_sample(x, mod, hist, state, gpre, gpost, gffn, win, wpool, pscale, lbl, gout, wout, wrh, wrl):
    b = x.shape[0]
    bb = SAMPLE_STATE_BLOCK
    cp = pltpu.CompilerParams(vmem_limit_bytes=VMEM_LIMIT)
    ypool, npool, ft, qt, v, gate = pl.pallas_call(
        _mix_sample_in_kernel,
        out_shape=(jax.ShapeDtypeStruct((b, POOL_WIDTH), F32),
                   jax.ShapeDtypeStruct((b, POOL_HIST, POOL_WIDTH), F32),
                   jax.ShapeDtypeStruct((HG_WIDTH, b), F32),
                   jax.ShapeDtypeStruct((HG_WIDTH, b), F32),
                   jax.ShapeDtypeStruct((b, HG_WIDTH), F32),
                   jax.ShapeDtypeStruct((b, HG_WIDTH), F32)),
        compiler_params=cp,
        name="mix_sample_in",
    )(x, mod, gpre, win, wpool, pscale, lbl, hist)

    s_spec = pl.BlockSpec((bb, HG_HEADS, HG_HEAD_DIM, HG_HEAD_DIM), lambda i: (i, 0, 0, 0))
    col_spec = _full((HG_WIDTH, b))
    row_spec = pl.BlockSpec((bb, HG_WIDTH), lambda i: (i, 0))
    s_new, o = pl.pallas_call(
        _mix_sample_state_kernel,
        out_shape=(jax.ShapeDtypeStruct(state.shape, F32), jax.ShapeDtypeStruct((b, HG_WIDTH), F32)),
        grid=(b // bb,),
        in_specs=[s_spec, col_spec, col_spec, row_spec],
        out_specs=(s_spec, row_spec),
        compiler_params=pltpu.CompilerParams(dimension_semantics=("parallel",), vmem_limit_bytes=VMEM_LIMIT),
        name="mix_sample_state",
    )(state, ft, qt, v)

    x1, h2, lg = pl.pallas_call(
        _mix_sample_out_kernel,
        out_shape=(jax.ShapeDtypeStruct((b, D_MODEL), F32),
                   jax.ShapeDtypeStruct((b, D_MODEL), BF16),
                   jax.ShapeDtypeStruct((b, LANES), F32)),
        compiler_params=cp,
        name="mix_sample_out",
    )(x, mod, o, gate, ypool, gout, wout, gpost, gffn, wrh, wrl)
    return x1, h2, lg, npool, s_new


def _first_max(cur, idx, big):
    m = jnp.max(cur, axis=0, keepdims=True)
    first = jnp.min(jnp.where(cur == m, idx, big), axis=0, keepdims=True)
    return idx == first


def _route(logit, bias):
    n = logit.shape[1]
    scores = jax.nn.sigmoid(logit)
    sel = scores + bias
    neg = -jnp.inf

    sub = lax.broadcasted_iota(jnp.int32, (GROUP_SIZE, n), 0)
    gscore = []
    for g in range(N_GROUPS):
        sg = sel[g * GROUP_SIZE:(g + 1) * GROUP_SIZE, :]
        m1 = jnp.max(sg, axis=0, keepdims=True)
        rest = jnp.where(_first_max(sg, sub, GROUP_SIZE), neg, sg)
        gscore.append(m1 + jnp.max(rest, axis=0, keepdims=True))
    cur = jnp.concatenate(gscore, axis=0)
    gidx = lax.broadcasted_iota(jnp.int32, (N_GROUPS, n), 0)
    gmask = jnp.zeros((N_GROUPS, n), jnp.bool_)
    for _ in range(TOPK_GROUPS):
        pick = _first_max(cur, gidx, N_GROUPS)
        gmask = gmask | pick
        cur = jnp.where(pick, neg, cur)
    emask = jnp.concatenate(
        [jnp.broadcast_to(gmask[g:g + 1, :], (GROUP_SIZE, n)) for g in range(N_GROUPS)], axis=0)

    cur = jnp.where(emask, sel, neg)
    eidx = lax.broadcasted_iota(jnp.int32, (N_EXPERTS, n), 0)
    picks = []
    for _ in range(TOP_K):
        pick = _first_max(cur, eidx, N_EXPERTS)
        picks.append(pick)
        cur = jnp.where(pick, neg, cur)
    return scores, picks


def _any(masks):
    return functools.reduce(jnp.logical_or, masks)


def _router_kernel(lg_ref, bias_ref, wd_ref):
    n = lg_ref.shape[0]
    scores, picks = _route(lg_ref[...].T[0:N_EXPERTS, :], bias_ref[0:N_EXPERTS, :])
    chosen = _any(picks)
    wsum = jnp.sum(jnp.where(chosen, scores, 0.0), axis=0, keepdims=True)
    wd = jnp.where(chosen, scores / wsum * ROUTED_SCALE, 0.0)
    wd_ref[...] = jnp.concatenate([wd, jnp.zeros((LANES - N_EXPERTS, n), F32)], axis=0).T


def _router(logits, bias_col):
    t = logits.shape[0]
    tile = min(ROUTER_TILE, t)
    return pl.pallas_call(
        _router_kernel,
        out_shape=jax.ShapeDtypeStruct((t, LANES), F32),
        grid=(t // tile,),
        in_specs=[pl.BlockSpec((tile, LANES), lambda i: (i, 0)), _full(bias_col.shape)],
        out_specs=pl.BlockSpec((tile, LANES), lambda i: (i, 0)),
        compiler_params=pltpu.CompilerParams(dimension_semantics=("parallel",), vmem_limit_bytes=VMEM_LIMIT),
        name="router",
    )(logits, bias_col)


def _plan_kernel(lg_ref, bias_ref, upper_ref, dest_ref, wsel_ref, cnt_ref, loff_ref):
    scores, picks = _route(lg_ref[...].T[0:N_EXPERTS, :], bias_ref[0:N_EXPERTS, :])
    chosen = _any(picks)
    chosen_f = jnp.where(chosen, 1.0, 0.0)
    cnt = jnp.sum(chosen_f, axis=1, keepdims=True)
    units = jnp.floor((cnt + (RUN_ALIGN - 1)) / RUN_ALIGN)
    ei = lax.broadcasted_iota(jnp.int32, (N_EXPERTS, N_EXPERTS), 0)
    ej = lax.broadcasted_iota(jnp.int32, (N_EXPERTS, N_EXPERTS), 1)
    before = jnp.where(ej < ei, 1.0, 0.0).astype(BF16)
    loff = RUN_ALIGN * _dot(before, jnp.broadcast_to(units, (N_EXPERTS, LANES)).astype(BF16))[:, 0:1]
    slot = loff + _dot(chosen_f.astype(BF16), upper_ref[...])
    wsum = jnp.sum(jnp.where(chosen, scores, 0.0), axis=0, keepdims=True)
    pick_sum = lambda v: jnp.concatenate(
        [jnp.sum(jnp.where(p, v, 0.0), axis=0, keepdims=True) for p in picks], axis=0)
    dest_ref[0] = (pick_sum(slot) * ROW_SUB).astype(jnp.int32)
    wsel_ref[0] = pick_sum(scores) / wsum * ROUTED_SCALE
    cnt_ref[0] = jnp.broadcast_to(cnt, (N_EXPERTS, LANES))
    loff_ref[0] = jnp.broadcast_to(loff, (N_EXPERTS, LANES))


def _plan(logits, bias_col, upper):
    t = logits.shape[0]
    nw = t // PLAN_WINDOW
    per_w = pl.BlockSpec((1, N_EXPERTS, LANES), lambda w: (w, 0, 0))
    picks = pl.BlockSpec((1, TOP_K, PLAN_WINDOW), lambda w: (w, 0, 0))
    return pl.pallas_call(
        _plan_kernel,
        out_shape=(jax.ShapeDtypeStruct((nw, TOP_K, PLAN_WINDOW), jnp.int32),
                   jax.ShapeDtypeStruct((nw, TOP_K, PLAN_WINDOW), F32),
                   jax.ShapeDtypeStruct((nw, N_EXPERTS, LANES), F32),
                   jax.ShapeDtypeStruct((nw, N_EXPERTS, LANES), F32)),
        grid=(nw,),
        in_specs=[pl.BlockSpec((PLAN_WINDOW, LANES), lambda w: (w, 0)), _full(bias_col.shape), _full(upper.shape)],
        out_specs=(picks, picks, per_w, per_w),
        compiler_params=pltpu.CompilerParams(dimension_semantics=("parallel",), vmem_limit_bytes=VMEM_LIMIT),
        name="plan",
    )(logits, bias_col, upper)


def _slot_rows(slot):
    return pl.ds(pl.multiple_of(slot * ROW_SUB, ROW_SUB), ROW_SUB)


def _table_copies(table_ref, flat_s, sem):
    return [pltpu.make_async_copy(table_ref.at[0, k], flat_s.at[pl.ds(k * PLAN_WINDOW, PLAN_WINDOW)], sem)
            for k in range(TOP_K)]


def _rows_at(offset):
    return pl.ds(pl.multiple_of(offset, ROW_SUB), ROW_SUB)


def _dispatch_kernel(dest_ref, h_ref, x_ref, dest_s, rows_s, sem):
    plan_copies = _table_copies(dest_ref, dest_s, sem)
    for c in plan_copies:
        c.start()
    x_ref[...] = jnp.zeros_like(x_ref)
    for j in range(ROW_SUB):
        rows_s[pl.ds(j, PLAN_WINDOW, stride=ROW_SUB), :] = h_ref[:, j * LANES:(j + 1) * LANES]
    for c in plan_copies:
        c.wait()

    def body(i, carry):
        for u in range(SCATTER_UNROLL):
            t = i * SCATTER_UNROLL + u
            row = rows_s[_slot_rows(t), :]
            for k in range(TOP_K):
                x_ref[_rows_at(dest_s[k * PLAN_WINDOW + t]), :] = row
        return carry

    lax.fori_loop(0, PLAN_WINDOW // SCATTER_UNROLL, body, 0)


def _dispatch(dest_w, h2):
    nw = dest_w.shape[0]
    region = REGION_ROWS * ROW_SUB
    return pl.pallas_call(
        _dispatch_kernel,
        out_shape=jax.ShapeDtypeStruct((nw * region, LANES), jnp.uint32),
        grid=(nw,),
        in_specs=[pl.BlockSpec((1, TOP_K, PLAN_WINDOW), lambda w: (w, 0, 0)),
                  pl.BlockSpec((PLAN_WINDOW, ROW_WORDS), lambda w: (w, 0))],
        out_specs=pl.BlockSpec((region, LANES), lambda w: (w, 0)),
        scratch_shapes=[pltpu.SMEM((TOP_K * PLAN_WINDOW,), jnp.int32),
                        pltpu.VMEM((PLAN_WINDOW * ROW_SUB, LANES), jnp.uint32),
                        pltpu.SemaphoreType.DMA],
        compiler_params=pltpu.CompilerParams(dimension_semantics=("arbitrary",), vmem_limit_bytes=VMEM_LIMIT),
        name="dispatch",
    )(dest_w, h2)


def _unpack_rows(ref, n):
    parts = [_unpack_pair(ref[pl.ds(j, n, stride=ROW_SUB), :]) for j in range(ROW_SUB)]
    return jnp.concatenate([p[0] for p in parts] + [p[1] for p in parts], axis=1)


def _ffn_kernel(gq_ref, gs_ref, to_ref, tn_ref, x_hbm, w1_ref, w3_ref, w2_ref,
                hs_ref, wds_ref, x1s_ref, gt2s_ref, gpost_ref, ws1_ref, ws3_ref, ws2_ref, y_hbm, ys_ref,
                w13_s, w2_s, xbuf, ybuf, rsem, wsem, acc_s):
    del x_hbm
    e = pl.program_id(0)
    n_groups = gq_ref[N_EXPERTS]
    block = RUN_ALIGN * ROW_SUB
    tile = FFN_TILE * ROW_SUB

    def go(cp, start, priority=0):
        if start:
            cp.start(priority=priority)
        else:
            cp.wait()

    def fetch(q, start):
        def one(j):
            src = y_hbm.at[pl.ds(pl.multiple_of(to_ref[gs_ref[q] + j] * block, block), tile)]
            go(pltpu.make_async_copy(src, xbuf.at[q % FFN_RING, pl.ds(j * tile, tile)], rsem.at[q % FFN_RING]),
               start, priority=1)

        size = gs_ref[q + 1] - gs_ref[q]

        @pl.when(size == FFN_GROUP)
        def _():
            for j in range(FFN_GROUP):
                one(j)

        @pl.when(size < FFN_GROUP)
        def _():
            for j in range(FFN_GROUP - 1):
                pl.when(j < size)(functools.partial(one, j))

    def writeback(q, start):
        most = FFN_TILE // RUN_ALIGN - 1

        def blocks(j, b, n):
            dst = pl.multiple_of(to_ref[gs_ref[q] + j] * block, block) + b * block
            go(pltpu.make_async_copy(ybuf.at[q % FFN_RING, pl.ds(j * tile + b * block, n * block)],
                                     y_hbm.at[pl.ds(dst, n * block)], wsem.at[q % FFN_RING]), start)

        for j in range(FFN_GROUP):
            item = gs_ref[q] + j
            n = jnp.where(item < gs_ref[q + 1], tn_ref[item], 0)
            pl.when(n >= most)(functools.partial(blocks, j, 0, most))
            pl.when(n > most)(functools.partial(blocks, j, most, 1))

            @pl.when((n > 0) & (n < most))
            def _():
                for b in range(most - 1):
                    pl.when(b < n)(functools.partial(blocks, j, b, 1))

    @pl.when(e == 0)
    def _():
        xbuf[...] = jnp.zeros_like(xbuf)
        for d in range(FFN_RING - 1):
            pl.when(d < n_groups)(functools.partial(fetch, d, True))

    w13_s[:, 0:D_EXPERT] = w1_ref[0].astype(BF16)
    w13_s[:, D_EXPERT:] = w3_ref[0].astype(BF16)
    w2_s[...] = w2_ref[0].astype(BF16)

    xs = hs_ref[...]

    @pl.when(e == 0)
    def _():
        sh = _silu(_dot(xs, ws1_ref[...].astype(BF16))) * _dot(xs, ws3_ref[...].astype(BF16))
        acc_s[...] = _dot(sh.astype(BF16), ws2_ref[...].astype(BF16))

    lane = lax.broadcasted_iota(jnp.int32, wds_ref.shape, 1)
    wcol = jnp.sum(jnp.where(lane == e, wds_ref[...], 0.0), axis=1, keepdims=True)
    rows = FFN_GROUP * FFN_TILE

    def ffn(x, ride):
        n = 0 if x is None else x.shape[0]
        if ride:
            x = xs if x is None else jnp.concatenate([x, xs], axis=0)
        h = _dot(x, w13_s[...])
        y = _dot((_silu(h[:, :D_EXPERT]) * h[:, D_EXPERT:]).astype(BF16), w2_s[...])
        if ride:
            acc_s[...] += y[n:] * wcol
        return y[:n] if n else None

    def group(q, ride):
        @pl.when(q + FFN_RING - 1 < n_groups)
        def _():
            fetch(q + FFN_RING - 1, True)

        @pl.when(q >= FFN_RING)
        def _():
            writeback(q - FFN_RING, False)

        fetch(q, False)
        y = ffn(_unpack_rows(xbuf.at[q % FFN_RING], rows).astype(BF16), ride)
        out = ybuf.at[q % FFN_RING]
        for j in range(ROW_SUB):
            out[pl.ds(j, rows, stride=ROW_SUB), :] = _pack_pair(
                y[:, j * LANES:(j + 1) * LANES], y[:, ROW_WORDS + j * LANES:ROW_WORDS + (j + 1) * LANES])
        writeback(q, True)

    first, end = gq_ref[e], gq_ref[e + 1]
    pl.when(first < end)(functools.partial(group, first, True))
    pl.when(first == end)(functools.partial(ffn, None, True))

    def group_body(q, carry):
        group(q, False)
        return carry

    lax.fori_loop(first + 1, end, group_body, 0)

    @pl.when(e == pl.num_programs(0) - 1)
    def _():
        ys_ref[...] = x1s_ref[...] + gt2s_ref[...] * _rms(acc_s[...], gpost_ref[...])

    @pl.when(e == pl.num_programs(0) - 1)
    def _():
        for d in range(FFN_RING):
            @pl.when(n_groups - 1 - d >= 0)
            def _():
                writeback(n_groups - 1 - d, False)


def _expert_ffn(group_of_expert, group_start, item_off, item_blocks, x_slots, w1, w3, w2,
                h2_s, wd_s, x1_s, gt2_s, gpost, ws1, ws3, ws2):
    of_expert = lambda e, *_: (e, 0, 0)
    whole = lambda a: pl.BlockSpec(a.shape, lambda e, *_: (0,) * a.ndim)
    tiles = pltpu.VMEM((FFN_RING, FFN_GROUP * FFN_TILE * ROW_SUB, LANES), jnp.uint32)
    sems = pltpu.SemaphoreType.DMA((FFN_RING,))
    dense = (h2_s, wd_s, x1_s, gt2_s, gpost, ws1, ws3, ws2)
    grid_spec = pltpu.PrefetchScalarGridSpec(
        num_scalar_prefetch=4,
        grid=(N_EXPERTS,),
        in_specs=[pl.BlockSpec(memory_space=pl.ANY),
                  pl.BlockSpec((1, D_MODEL, D_EXPERT), of_expert),
                  pl.BlockSpec((1, D_MODEL, D_EXPERT), of_expert),
                  pl.BlockSpec((1, D_EXPERT, D_MODEL), of_expert)] + [whole(a) for a in dense],
        out_specs=(pl.BlockSpec(memory_space=pl.ANY), whole(x1_s)),
        scratch_shapes=[pltpu.VMEM((D_MODEL, 2 * D_EXPERT), BF16), pltpu.VMEM((D_EXPERT, D_MODEL), BF16),
                        tiles, tiles, sems, sems, pltpu.VMEM(x1_s.shape, F32)])
    return pl.pallas_call(
        _ffn_kernel,
        out_shape=(jax.ShapeDtypeStruct(x_slots.shape, jnp.uint32), jax.ShapeDtypeStruct(x1_s.shape, F32)),
        grid_spec=grid_spec,
        input_output_aliases={4: 0},
        compiler_params=pltpu.CompilerParams(dimension_semantics=("arbitrary",), vmem_limit_bytes=VMEM_LIMIT),
        name="expert_ffn",
    )(group_of_expert, group_start, item_off, item_blocks, x_slots, w1, w3, w2, *dense)


def _work_items(cnt, loff):
    nw = cnt.shape[0]
    cnt = cnt.astype(jnp.int32).T.reshape(-1)
    loff = loff.astype(jnp.int32).T.reshape(-1)
    region = jnp.tile(jnp.arange(nw, dtype=jnp.int32) * REGION_ROWS, N_EXPERTS)
    n_items = nw * N_EXPERTS + (nw * PLAN_WINDOW * TOP_K) // FFN_TILE
    tiles = (cnt + FFN_TILE - 1) // FFN_TILE
    ends = jnp.cumsum(tiles)
    starts = ends - tiles
    i = jnp.arange(n_items + FFN_GROUP, dtype=jnp.int32)[:, None]
    mine = (starts[None, :] <= i) & (i < ends[None, :])
    of_pair = lambda v: jnp.sum(jnp.where(mine, v[None, :], 0), axis=1)
    r = i[:, 0] - of_pair(starts)
    row = of_pair(region + loff) + r * FFN_TILE
    own = jnp.clip(of_pair(cnt) - r * FFN_TILE, 0, FFN_TILE)

    zero = jnp.zeros((1,), jnp.int32)
    item_end = ends.reshape(N_EXPERTS, nw)[:, -1]
    item_start = jnp.concatenate([zero, item_end[:-1]])
    group_end = jnp.cumsum((item_end - item_start + FFN_GROUP - 1) // FFN_GROUP)
    group_first = jnp.concatenate([zero, group_end[:-1]])
    q = jnp.arange(N_EXPERTS + n_items // FFN_GROUP + 1, dtype=jnp.int32)[:, None]
    has = (group_first[None, :] <= q) & (q < group_end[None, :])
    of_expert = lambda v: jnp.sum(jnp.where(has, v[None, :], 0), axis=1)
    group_start = jnp.where(q[:, 0] < group_end[-1],
                            of_expert(item_start) + (q[:, 0] - of_expert(group_first)) * FFN_GROUP, item_end[-1])
    return (jnp.concatenate([zero, group_end]), group_start, row // RUN_ALIGN, (own + RUN_ALIGN - 1) // RUN_ALIGN)


def _combine_kernel(dest_ref, wsel_ref, y_hbm, h_ref, x1_ref, gt2_ref, gpost_ref, ws13_ref, ws2_ref, out_ref,
                    dest_s, wsel_s, lo_s, hi_s, ybuf, sem, ysem):
    w, s = pl.program_id(0), pl.program_id(1)
    region = REGION_ROWS * ROW_SUB
    part = region // REGION_PARTS

    def region_copies(win):
        return [pltpu.make_async_copy(y_hbm.at[pl.ds(win * region + p * part, part)],
                                      ybuf.at[win % 2, pl.ds(p * part, part)], ysem.at[win % 2, p])
                for p in range(REGION_PARTS)]

    @pl.when(s == 0)
    def _():
        copies = _table_copies(dest_ref, dest_s, sem.at[0]) + _table_copies(wsel_ref, wsel_s, sem.at[1])
        for c in copies:
            c.start()

        @pl.when(w == 0)
        def _():
            for c in region_copies(w):
                c.start()

        @pl.when(w + 1 < pl.num_programs(0))
        def _():
            for c in region_copies(w + 1):
                c.start()

        for c in copies + region_copies(w):
            c.wait()

    y_ref = ybuf.at[w % 2]

    base = s * COMBINE_TILE

    def body(i, carry):
        for u in range(COMBINE_UNROLL):
            t = i * COMBINE_UNROLL + u
            acc_lo = jnp.zeros((ROW_SUB, LANES), F32)
            acc_hi = jnp.zeros((ROW_SUB, LANES), F32)
            for k in range(TOP_K):
                at = k * PLAN_WINDOW + base + t
                lo, hi = _unpack_pair(y_ref[_rows_at(dest_s[at]), :])
                w = wsel_s[at]
                acc_lo = acc_lo + w * lo
                acc_hi = acc_hi + w * hi
            lo_s[_slot_rows(t), :] = acc_lo
            hi_s[_slot_rows(t), :] = acc_hi
        return carry

    lax.fori_loop(0, COMBINE_TILE // COMBINE_UNROLL, body, 0)
    rows = lambda ref: [ref[pl.ds(j, COMBINE_TILE, stride=ROW_SUB), :] for j in range(ROW_SUB)]
    routed = jnp.concatenate(rows(lo_s) + rows(hi_s), axis=1)
    h_lo, h_hi = _unpack_pair(h_ref[...])
    x = jnp.concatenate([h_lo, h_hi], axis=1).astype(BF16)
    hs = _dot(x, ws13_ref[...])
    act = _silu(hs[:, :D_EXPERT]) * hs[:, D_EXPERT:]
    ff = routed + _dot(act.astype(BF16), ws2_ref[...])
    out_ref[...] = x1_ref[...] + gt2_ref[0] * _rms(ff, gpost_ref[...])


def _combine(dest_w, wsel_w, y_slots, h2, x1, gt2, tokens_per_gt2, gpost, ws13, ws2):
    nw = dest_w.shape[0]
    sub = PLAN_WINDOW // COMBINE_TILE
    plan = pl.BlockSpec((1, TOP_K, PLAN_WINDOW), lambda w, s: (w, 0, 0))
    tok = lambda w, s: (w * sub + s, 0)
    return pl.pallas_call(
        _combine_kernel,
        out_shape=jax.ShapeDtypeStruct(x1.shape, F32),
        grid=(nw, sub),
        in_specs=[plan, plan,
                  pl.BlockSpec(memory_space=pl.ANY),
                  pl.BlockSpec((COMBINE_TILE, ROW_WORDS), tok),
                  pl.BlockSpec((COMBINE_TILE, D_MODEL), tok),
                  pl.BlockSpec((1, 1, D_MODEL), lambda w, s: ((w * PLAN_WINDOW) // tokens_per_gt2, 0, 0)),
                  _full((1, D_MODEL)), _full(ws13.shape), _full(ws2.shape)],
        out_specs=pl.BlockSpec((COMBINE_TILE, D_MODEL), tok),
        scratch_shapes=[pltpu.SMEM((TOP_K * PLAN_WINDOW,), jnp.int32),
                        pltpu.SMEM((TOP_K * PLAN_WINDOW,), F32),
                        pltpu.VMEM((COMBINE_TILE * ROW_SUB, LANES), F32),
                        pltpu.VMEM((COMBINE_TILE * ROW_SUB, LANES), F32),
                        pltpu.VMEM((2, REGION_ROWS * ROW_SUB, LANES), jnp.uint32),
                        pltpu.SemaphoreType.DMA((2,)), pltpu.SemaphoreType.DMA((2, REGION_PARTS))],
        compiler_params=pltpu.CompilerParams(dimension_semantics=("arbitrary", "arbitrary"),
                                             vmem_limit_bytes=VMEM_LIMIT),
        name="combine",
    )(dest_w, wsel_w, y_slots, h2, x1, gt2, gpost, ws13, ws2)


def kernel(x_prompt, x_sample, c_prompt, c_sample, state_pool, state_hgrn, w_ada, b_ada, g_pre_mix, g_post_mix,
           w_in, w_pool, pool_scale, lb_logits, g_out_norm, w_out, g_pre_ffn, g_post_ffn, w_router, router_bias,
           w_exp_gate, w_exp_up, w_exp_down, w_sh_gate, w_sh_up, w_sh_down):
    assert w_ada.shape[0] == 1 and lb_logits.shape[0] == 2, "single-layer trunk"
    bp, lp, d = x_prompt.shape
    bs = x_sample.shape[0]
    row = lambda a: a[0].reshape(1, -1)

    n_mod = bp + bs
    pad = (-n_mod) % 16
    c_all = jnp.concatenate([c_prompt, c_sample, jnp.zeros((pad, d), F32)], axis=0)
    mod = _ada(c_all, w_ada[0], b_ada)
    mod_p = mod[:bp].reshape(bp, 1, 6 * d)
    mod_s = mod[bp:n_mod]

    win = w_in[0].astype(BF16)
    wout = w_out[0].astype(BF16)
    wpool = w_pool[0].astype(BF16)
    wr = jnp.pad(w_router[0], ((0, 0), (0, LANES - N_EXPERTS)))
    wrh = wr.astype(BF16)
    wrl = (wr - wrh.astype(F32)).astype(BF16)
    m2 = jnp.asarray(_M2_NP, BF16)
    gpre, gpost, gffn, gffn_post = row(g_pre_mix), row(g_post_mix), row(g_pre_ffn), row(g_post_ffn)
    pscale, gout = row(pool_scale), row(g_out_norm)
    bias_col = jnp.pad(router_bias[0], (0, LANES - N_EXPERTS)).reshape(LANES, 1)

    x1_p, h2_p, lg_p, pool_p, st_p = _mix_prompt(x_prompt, mod_p, gpre, gpost, gffn, win, wpool, pscale,
                                                  lb_logits, gout, wout, wrh, wrl, m2)
    x1_s, h2_s, lg_s, pool_s, st_s = _mix_sample(x_sample[:, 0, :], mod_s, state_pool[0], state_hgrn[0],
                                                  gpre, gpost, gffn, win, wpool, pscale, lb_logits, gout, wout,
                                                  wrh, wrl)

    experts = (w_exp_gate[0], w_exp_up[0], w_exp_down[0], w_sh_gate[0], w_sh_up[0], w_sh_down[0])
    tp = bp * lp
    nw = tp // PLAN_WINDOW
    assert lp % PLAN_WINDOW == 0
    upper = jnp.asarray(np.triu(np.ones((PLAN_WINDOW, PLAN_WINDOW), np.float32), 1), BF16)
    dest, wsel, cnt, loff = _plan(lg_p.reshape(tp, LANES), bias_col, upper)
    dest_w, wsel_w = dest, wsel
    h2_rows = h2_p.reshape(tp, ROW_WORDS)
    x_slots = _dispatch(dest_w, h2_rows)
    wd_s = _router(lg_s, bias_col)
    y_slots, y_s = _expert_ffn(*_work_items(cnt[:, :, 0], loff[:, :, 0]), x_slots, *experts[:3],
                               h2_s, wd_s, x1_s, mod_s[:, 5 * d:], gffn_post, *experts[3:])
    ws13 = jnp.concatenate([experts[3], experts[4]], axis=1).astype(BF16)
    gt2_p = mod_p[:, :, 5 * d:]
    y_p = _combine(dest_w, wsel_w, y_slots, h2_rows, x1_p.reshape(tp, d), gt2_p, lp, gffn_post,
                   ws13, experts[5].astype(BF16))

    return (y_p.reshape(bp, lp, d), y_s.reshape(bs, 1, d), pool_p[None], st_p[None], pool_s[None], st_s[None])
```

```python
import functools
import itertools

import numpy as np
import jax
import jax.numpy as jnp
from jax import lax
from jax.experimental import pallas as pl
from jax.experimental.pallas import tpu as pltpu

F32 = jnp.float32
BF16 = jnp.bfloat16

D_MODEL = 1024
POOL_WIDTH = 512
POOL_WINDOWS = (2, 4, 8, 16)
POOL_GROUP = 128
POOL_HIST = 15
HG_WIDTH = 512
HG_HEADS = 4
HG_HEAD_DIM = 128
IN_WIDTH = POOL_WIDTH + 4 * HG_WIDTH
N_EXPERTS = 64
TOP_K = 8
N_GROUPS = 8
TOPK_GROUPS = 4
GROUP_SIZE = N_EXPERTS // N_GROUPS
D_EXPERT = 256
ROUTED_SCALE = 2.5
EPS = 1e-6

LANES = 128
CHUNK = 64
TIME_TILE = 512
ADA_TILE_N = 1024
ROUTER_TILE = 512
SAMPLE_STATE_BLOCK = 16
PLAN_WINDOW = 1024
RUN_ALIGN = 32
FFN_TILE = 160
REGION_ROWS = PLAN_WINDOW * TOP_K + N_EXPERTS * (RUN_ALIGN - 1) + FFN_TILE
ROW_WORDS = D_MODEL // 2
ROW_SUB = ROW_WORDS // LANES
COMBINE_TILE = 256
REGION_PARTS = 4
SCATTER_UNROLL = 4
DISPATCH_AHEAD = 2
COMBINE_UNROLL = 16
FFN_GROUP = 4
FFN_RING = 6
VMEM_LIMIT = 58 * 1024 * 1024

_LEVELS = (64, 32, 16, 8, 4, 2)
_BLK_CUM = 0


def _level_blocks():
    c = CHUNK
    i = np.arange(c)[:, None]
    s = np.arange(c)[None, :]
    blocks = [(s <= i)]
    index = {}
    for lvl in _LEVELS:
        ref = (i // lvl) * lvl + lvl // 2 - 1
        index[lvl] = len(blocks)
        blocks.append(s <= ref)
    m = np.concatenate(blocks, axis=0).astype(np.float32)
    return np.concatenate([m, m], axis=1), index


_M2_NP, _LEVEL_INDEX = _level_blocks()


def _dot(a, b):
    return jnp.dot(a, b, preferred_element_type=F32)


def _dot_nt(a, b):
    return lax.dot_general(a, b, (((1,), (1,)), ((), ())), preferred_element_type=F32)


def _split2(x):
    hi = x.astype(BF16)
    lo = (x - hi.astype(F32)).astype(BF16)
    return hi, lo


def _pack_pair(a, b):
    lo = lax.bitcast_convert_type(a.astype(BF16).astype(F32), jnp.uint32)
    hi = lax.bitcast_convert_type(b.astype(BF16).astype(F32), jnp.uint32)
    return (lo >> 16) | hi


def _unpack_pair(words):
    lo = lax.bitcast_convert_type(words << 16, F32)
    hi = lax.bitcast_convert_type(words & jnp.uint32(0xFFFF0000), F32)
    return lo, hi


def _silu(x):
    return x * jax.nn.sigmoid(x)


def _rms(x, g):
    return x * lax.rsqrt(jnp.mean(x * x, axis=-1, keepdims=True) + EPS) * g


def _mods(m):
    return [m[:, j * D_MODEL:(j + 1) * D_MODEL] for j in range(6)]


def _forget_lower_bound(lbl):
    mx = jnp.max(lbl, axis=0, keepdims=True)
    e = jnp.exp(lbl - mx)
    return e[0:1] / jnp.sum(e, axis=0, keepdims=True)


def _router_logits(h2, wrh, wrl):
    hi, lo = _split2(h2)
    return _dot(hi, wrh) + _dot(lo, wrh) + _dot(hi, wrl)


def _ada_kernel(c_ref, w_ref, b_ref, o_ref):
    a_hi, a_lo = _split2(_silu(c_ref[...]))
    w_hi, w_lo = _split2(w_ref[...])
    o_ref[...] = _dot(a_hi, w_hi) + _dot(a_lo, w_hi) + _dot(a_hi, w_lo) + b_ref[...]


def _ada(c_all, w_ada, b_ada):
    rows = c_all.shape[0]
    n = w_ada.shape[1]
    return pl.pallas_call(
        _ada_kernel,
        out_shape=jax.ShapeDtypeStruct((rows, n), F32),
        grid=(n // ADA_TILE_N,),
        in_specs=[pl.BlockSpec((rows, D_MODEL), lambda j: (0, 0)),
                  pl.BlockSpec((D_MODEL, ADA_TILE_N), lambda j: (0, j)),
                  pl.BlockSpec((1, ADA_TILE_N), lambda j: (0, j))],
        out_specs=pl.BlockSpec((rows, ADA_TILE_N), lambda j: (0, j)),
        compiler_params=pltpu.CompilerParams(dimension_semantics=("parallel",),
                                             vmem_limit_bytes=VMEM_LIMIT),
        name="ada",
    )(c_all, w_ada, b_ada)


def _level_masks():
    i = lax.broadcasted_iota(jnp.int32, (CHUNK, 2 * CHUNK), 0)
    j = lax.broadcasted_iota(jnp.int32, (CHUNK, 2 * CHUNK), 1) & (CHUNK - 1)
    masks = {}
    for lvl in _LEVELS:
        sh = lvl.bit_length() - 1
        same = (i >> sh) == (j >> sh)
        upper = ((i >> (sh - 1)) & 1) == 1
        lower = ((j >> (sh - 1)) & 1) == 0
        masks[lvl] = same & upper & lower
    return masks, i == j


def _by_head(m):
    z = jnp.zeros((m.shape[0], HG_HEAD_DIM), m.dtype)
    return jnp.concatenate([jnp.concatenate([m[:, :HG_HEAD_DIM], z], axis=1),
                            jnp.concatenate([z, m[:, HG_HEAD_DIM:]], axis=1)], axis=0)


def _hgrn_chunk_pair(q, k, v, e_all, lane0, sts, masks, eye):
    hd = HG_HEAD_DIM

    def blk(n):
        return e_all[n * CHUNK:(n + 1) * CHUNK, lane0:lane0 + 2 * hd]

    b = blk(_BLK_CUM)
    kb = k.astype(BF16)
    a = jnp.where(eye, _dot_nt(q.astype(BF16), _by_head(kb)), 0.0)
    for lvl in _LEVELS:
        d = b - blk(_LEVEL_INDEX[lvl])
        ql = (q * jnp.exp(jnp.minimum(d, 0.0))).astype(BF16)
        kl = (k * jnp.exp(jnp.minimum(-d, 0.0))).astype(BF16) if lvl > 2 else kb
        a = a + jnp.where(masks[lvl], _dot_nt(ql, _by_head(kl)), 0.0)
    st_both = jnp.concatenate([jnp.concatenate([sts[0], jnp.zeros_like(sts[0])], axis=1),
                               jnp.concatenate([jnp.zeros_like(sts[1]), sts[1]], axis=1)], axis=0).astype(BF16)
    o = _dot(a.astype(BF16), _by_head(v.astype(BF16))) + _dot_nt((q * jnp.exp(b)).astype(BF16), st_both)
    b_last = b[CHUNK - 1:CHUNK, :]
    k_end = (k * jnp.exp(b_last - b)).astype(BF16)
    decay = jnp.exp(b_last)
    new = []
    for h in range(2):
        lanes = slice(h * hd, (h + 1) * hd)
        new.append(sts[h] * decay[:, lanes] + _dot(v[:, lanes].T.astype(BF16), k_end[:, lanes]))
    return o, new


def _mix_in(x_ref, mod_ref, t, gpre_ref, win_ref, wpool_ref, pscale_ref, lbl_ref, ubuf, dst):
    tt = TIME_TILE
    q_s, k_s, v_s, g_s, gate_s, yp_s = dst
    xt = x_ref[0]
    sh1, sc1 = _mods(mod_ref[0])[:2]
    h = (_rms(xt, gpre_ref[...]) * (1.0 + sc1) + sh1).astype(BF16)
    yield
    proj = _dot(h, win_ref[...])
    yield

    u = proj[:, :POOL_WIDTH]
    ubuf[16:16 + tt, :] = u
    pos = (t * tt + lax.broadcasted_iota(jnp.int32, (tt, 1), 0) + 1).astype(F32)
    ys = []
    for g, w in enumerate(POOL_WINDOWS):
        s = ubuf[:, g * POOL_GROUP:(g + 1) * POOL_GROUP]
        off = 0
        for step in range(w.bit_length() - 1):
            sh = 1 << step
            s = s[sh:, :] + s[:-sh, :]
            off += sh
        ws = s[16 - off:16 - off + tt, :]
        cnt = jnp.minimum(pos, float(w))
        d = ws / cnt - u[:, g * POOL_GROUP:(g + 1) * POOL_GROUP]
        ys.append(_dot(d.astype(BF16), wpool_ref[g]))
    yp_s[...] = jnp.concatenate(ys, axis=1) * pscale_ref[...]
    ubuf[0:16, :] = ubuf[tt:tt + 16, :]

    lb = _forget_lower_bound(lbl_ref[...])
    f = lb + (1.0 - lb) * jax.nn.sigmoid(proj[:, POOL_WIDTH + HG_WIDTH:POOL_WIDTH + 2 * HG_WIDTH])
    q_s[...] = _silu(proj[:, POOL_WIDTH:POOL_WIDTH + HG_WIDTH])
    k_s[...] = 1.0 - f
    v_s[...] = proj[:, POOL_WIDTH + 2 * HG_WIDTH:POOL_WIDTH + 3 * HG_WIDTH]
    g_s[...] = jnp.log(f)
    gate_s[...] = _silu(proj[:, POOL_WIDTH + 3 * HG_WIDTH:])


def _mix_out(x_ref, mod_ref, src, gpost_ref, gffn_ref, gout_ref, wout_ref, wrh_ref, wrl_ref, m2_ref,
             st_s, o_s, x1_ref, h2_ref, lg_ref):
    tt = TIME_TILE
    q_s, k_s, v_s, g_s, gate_s, yp_s = src
    masks, eye = _level_masks()
    m2 = m2_ref[...]

    states = [st_s[hd] for hd in range(HG_HEADS)]
    for c in range(tt // CHUNK):
        rows = slice(c * CHUNK, (c + 1) * CHUNK)
        g_hi, g_lo = _split2(g_s[rows, :])
        e_all = _dot(m2, jnp.concatenate([g_hi, g_lo], axis=0))
        for hd in range(0, HG_HEADS, 2):
            lane0 = hd * HG_HEAD_DIM
            lanes = slice(lane0, lane0 + 2 * HG_HEAD_DIM)
            o, states[hd:hd + 2] = _hgrn_chunk_pair(q_s[rows, lanes], k_s[rows, lanes], v_s[rows, lanes], e_all,
                                                    lane0, states[hd:hd + 2], masks, eye)
            o_s[rows, lanes] = o
    for hd in range(HG_HEADS):
        st_s[hd] = states[hd]
    yield

    o = o_s[...]
    os_ = []
    for hd in range(HG_HEADS):
        oh = o[:, hd * HG_HEAD_DIM:(hd + 1) * HG_HEAD_DIM]
        os_.append(oh * lax.rsqrt(jnp.mean(oh * oh, axis=-1, keepdims=True) + EPS) * gout_ref[...])
    o_n = jnp.concatenate(os_, axis=1) * gate_s[...]

    mix = (_dot(yp_s[...].astype(BF16), wout_ref[0:POOL_WIDTH, :])
           + _dot(o_n.astype(BF16), wout_ref[POOL_WIDTH:, :]))
    yield
    _, _, gt1, sh2, sc2, _ = _mods(mod_ref[0])
    x1 = x_ref[0] + gt1 * _rms(mix, gpost_ref[...])
    h2 = _rms(x1, gffn_ref[...]) * (1.0 + sc2) + sh2
    x1_ref[0] = x1
    h2_ref[0] = _pack_pair(h2[:, :ROW_WORDS], h2[:, ROW_WORDS:])
    lg_ref[0] = _router_logits(h2, wrh_ref[...], wrl_ref[...])


def _mix_prompt_kernel(x_ref, xn_ref, mod_ref, modn_ref, gpre_ref, gpost_ref, gffn_ref, win_ref, wpool_ref,
                       pscale_ref, lbl_ref, gout_ref, wout_ref, wrh_ref, wrl_ref, m2_ref,
                       x1_ref, h2_ref, lg_ref, pool_ref, st_ref,
                       st_s, ubuf, o_s, *sets, n_t):
    i = pl.program_id(0)
    t = i % n_t
    t_next = jnp.minimum(i + 1, pl.num_programs(0) - 1) % n_t
    mix_in = functools.partial(_mix_in, gpre_ref=gpre_ref, win_ref=win_ref, wpool_ref=wpool_ref,
                               pscale_ref=pscale_ref, lbl_ref=lbl_ref, ubuf=ubuf)
    cur = [s.at[i % 2] for s in sets]
    nxt = [s.at[(i + 1) % 2] for s in sets]

    def clear_history():
        ubuf[0:16, :] = jnp.zeros((16, POOL_WIDTH), F32)

    @pl.when(t == 0)
    def _():
        st_s[...] = jnp.zeros_like(st_s)

    @pl.when(i == 0)
    def _():
        clear_history()
        for _ in mix_in(x_ref, mod_ref, 0, dst=cur):
            pass

    pl.when((i + 1) % n_t == 0)(clear_history)

    finish = _mix_out(x_ref, mod_ref, cur, gpost_ref, gffn_ref, gout_ref, wout_ref, wrh_ref, wrl_ref,
                      m2_ref, st_s, o_s, x1_ref, h2_ref, lg_ref)
    prepare = mix_in(xn_ref, modn_ref, t_next, dst=nxt)
    for _ in itertools.zip_longest(finish, prepare):
        pass

    @pl.when((i + 1) % n_t == n_t - 1)
    def _():
        pool_ref[0] = ubuf[16 - POOL_HIST:16, :]

    @pl.when(t == n_t - 1)
    def _():
        for hd in range(HG_HEADS):
            st_ref[0, hd] = st_s[hd].T


def _full(shape):
    nd = len(shape)
    return pl.BlockSpec(shape, lambda *_: (0,) * nd)


def _mix_prompt(x, mod, gpre, gpost, gffn, win, wpool, pscale, lbl, gout, wout, wrh, wrl, m2):
    b, l, d = x.shape
    tt = TIME_TILE
    n_t = l // tt
    steps = b * n_t
    tile = lambda i: (i // n_t, i % n_t, 0)
    per_b = lambda i: (i // n_t, 0, 0)
    ahead = lambda i: jnp.minimum(i + 1, steps - 1)
    tile_next = lambda i: (ahead(i) // n_t, ahead(i) % n_t, 0)
    per_b_next = lambda i: (ahead(i) // n_t, 0, 0)
    recurrence_inputs = [pltpu.VMEM((2, tt, HG_WIDTH), F32)] * 5 + [pltpu.VMEM((2, tt, POOL_WIDTH), F32)]
    return pl.pallas_call(
        functools.partial(_mix_prompt_kernel, n_t=n_t),
        out_shape=(jax.ShapeDtypeStruct((b, l, d), F32),
                   jax.ShapeDtypeStruct((b, l, ROW_WORDS), jnp.uint32),
                   jax.ShapeDtypeStruct((b, l, LANES), F32),
                   jax.ShapeDtypeStruct((b, POOL_HIST, POOL_WIDTH), F32),
                   jax.ShapeDtypeStruct((b, HG_HEADS, HG_HEAD_DIM, HG_HEAD_DIM), F32)),
        grid=(steps,),
        in_specs=[pl.BlockSpec((1, tt, d), tile), pl.BlockSpec((1, tt, d), tile_next),
                  pl.BlockSpec((1, 1, 6 * d), per_b), pl.BlockSpec((1, 1, 6 * d), per_b_next),
                  _full((1, d)), _full((1, d)), _full((1, d)),
                  _full(win.shape), _full(wpool.shape), _full((1, POOL_WIDTH)),
                  _full(lbl.shape), _full((1, HG_HEAD_DIM)), _full(wout.shape),
                  _full(wrh.shape), _full(wrl.shape), _full(m2.shape)],
        out_specs=(pl.BlockSpec((1, tt, d), tile),
                   pl.BlockSpec((1, tt, ROW_WORDS), tile),
                   pl.BlockSpec((1, tt, LANES), tile),
                   pl.BlockSpec((1, POOL_HIST, POOL_WIDTH), per_b),
                   pl.BlockSpec((1, HG_HEADS, HG_HEAD_DIM, HG_HEAD_DIM), lambda i: (i // n_t, 0, 0, 0))),
        scratch_shapes=[pltpu.VMEM((HG_HEADS, HG_HEAD_DIM, HG_HEAD_DIM), F32),
                        pltpu.VMEM((tt + 16, POOL_WIDTH), F32),
                        pltpu.VMEM((tt, HG_WIDTH), F32)] + recurrence_inputs,
        compiler_params=pltpu.CompilerParams(dimension_semantics=("arbitrary",),
                                             vmem_limit_bytes=VMEM_LIMIT),
        name="mix_prompt",
    )(x, x, mod, mod, gpre, gpost, gffn, win, wpool, pscale, lbl, gout, wout, wrh, wrl, m2)


def _mix_sample_in_kernel(x_ref, mod_ref, gpre_ref, win_ref, wpool_ref, pscale_ref, lbl_ref, hist_ref,
                          ypool_ref, npool_ref, ft_ref, qt_ref, v_ref, gate_ref):
    xt = x_ref[...]
    sh1, sc1 = _mods(mod_ref[...])[:2]
    h = _rms(xt, gpre_ref[...]) * (1.0 + sc1) + sh1
    proj = _dot(h.astype(BF16), win_ref[...])
    u = proj[:, :POOL_WIDTH]
    row = lax.broadcasted_iota(jnp.int32, (hist_ref.shape[0], POOL_HIST, POOL_GROUP), 1)
    ys = []
    for g, w in enumerate(POOL_WINDOWS):
        sl = slice(g * POOL_GROUP, (g + 1) * POOL_GROUP)
        past = jnp.sum(jnp.where(row >= POOL_HIST - (w - 1), hist_ref[:, :, sl], 0.0), axis=1)
        ug = u[:, sl]
        d = (past + ug) / float(w) - ug
        ys.append(_dot(d.astype(BF16), wpool_ref[g]))
    ypool_ref[...] = jnp.concatenate(ys, axis=1) * pscale_ref[...]
    npool_ref[:, 0:POOL_HIST - 1, :] = hist_ref[:, 1:POOL_HIST, :]
    npool_ref[:, POOL_HIST - 1, :] = u

    lb = _forget_lower_bound(lbl_ref[...])
    f = lb + (1.0 - lb) * jax.nn.sigmoid(proj[:, POOL_WIDTH + HG_WIDTH:POOL_WIDTH + 2 * HG_WIDTH])
    ft_ref[...] = f.T
    qt_ref[...] = _silu(proj[:, POOL_WIDTH:POOL_WIDTH + HG_WIDTH]).T
    v_ref[...] = proj[:, POOL_WIDTH + 2 * HG_WIDTH:POOL_WIDTH + 3 * HG_WIDTH]
    gate_ref[...] = _silu(proj[:, POOL_WIDTH + 3 * HG_WIDTH:])


def _mix_sample_state_kernel(s_ref, ft_ref, qt_ref, v_ref, snew_ref, o_ref):
    i = pl.program_id(0)
    lane = lax.broadcasted_iota(jnp.int32, (HG_HEAD_DIM, ft_ref.shape[1]), 1)
    for j in range(SAMPLE_STATE_BLOCK):
        mine = lane == i * SAMPLE_STATE_BLOCK + j
        for hd in range(HG_HEADS):
            r0 = hd * HG_HEAD_DIM
            f = jnp.sum(jnp.where(mine, ft_ref[r0:r0 + HG_HEAD_DIM, :], 0.0), axis=1, keepdims=True)
            q = jnp.sum(jnp.where(mine, qt_ref[r0:r0 + HG_HEAD_DIM, :], 0.0), axis=1, keepdims=True)
            v = v_ref[j:j + 1, r0:r0 + HG_HEAD_DIM]
            s_new = f * s_ref[j, hd] + (1.0 - f) * v
            snew_ref[j, hd] = s_new
            o_ref[j:j + 1, r0:r0 + HG_HEAD_DIM] = jnp.sum(q * s_new, axis=0, keepdims=True)


def _mix_sample_out_kernel(x_ref, mod_ref, o_ref, gate_ref, ypool_ref, gout_ref, wout_ref, gpost_ref,
                           gffn_ref, wrh_ref, wrl_ref, x1_ref, h2_ref, lg_ref):
    _, _, gt1, sh2, sc2, _ = _mods(mod_ref[...])
    o = o_ref[...]
    os_ = []
    for hd in range(HG_HEADS):
        oh = o[:, hd * HG_HEAD_DIM:(hd + 1) * HG_HEAD_DIM]
        os_.append(oh * lax.rsqrt(jnp.mean(oh * oh, axis=-1, keepdims=True) + EPS) * gout_ref[...])
    o_n = jnp.concatenate(os_, axis=1) * gate_ref[...]
    mix = (_dot(ypool_ref[...].astype(BF16), wout_ref[0:POOL_WIDTH, :])
           + _dot(o_n.astype(BF16), wout_ref[POOL_WIDTH:, :]))
    x1 = x_ref[...] + gt1 * _rms(mix, gpost_ref[...])
    h2 = _rms(x1, gffn_ref[...]) * (1.0 + sc2) + sh2
    x1_ref[...] = x1
    h2_ref[...] = h2.astype(BF16)
    lg_ref[...] = _router_logits(h2, wrh_ref[...], wrl_ref[...])


def _mix_sample(x, mod, hist, state, gpre, gpost, gffn, win, wpool, pscale, lbl, gout, wout, wrh, wrl):
    b = x.shape[0]
    bb = SAMPLE_STATE_BLOCK
    cp = pltpu.CompilerParams(vmem_limit_bytes=VMEM_LIMIT)
    ypool, npool, ft, qt, v, gate = pl.pallas_call(
        _mix_sample_in_kernel,
        out_shape=(jax.ShapeDtypeStruct((b, POOL_WIDTH), F32),
                   jax.ShapeDtypeStruct((b, POOL_HIST, POOL_WIDTH), F32),
                   jax.ShapeDtypeStruct((HG_WIDTH, b), F32),
                   jax.ShapeDtypeStruct((HG_WIDTH, b), F32),
                   jax.ShapeDtypeStruct((b, HG_WIDTH), F32),
                   jax.ShapeDtypeStruct((b, HG_WIDTH), F32)),
        compiler_params=cp,
        name="mix_sample_in",
    )(x, mod, gpre, win, wpool, pscale, lbl, hist)

    s_spec = pl.BlockSpec((bb, HG_HEADS, HG_HEAD_DIM, HG_HEAD_DIM), lambda i: (i, 0, 0, 0))
    col_spec = _full((HG_WIDTH, b))
    row_spec = pl.BlockSpec((bb, HG_WIDTH), lambda i: (i, 0))
    s_new, o = pl.pallas_call(
        _mix_sample_state_kernel,
        out_shape=(jax.ShapeDtypeStruct(state.shape, F32), jax.ShapeDtypeStruct((b, HG_WIDTH), F32)),
        grid=(b // bb,),
        in_specs=[s_spec, col_spec, col_spec, row_spec],
        out_specs=(s_spec, row_spec),
        compiler_params=pltpu.CompilerParams(dimension_semantics=("parallel",), vmem_limit_bytes=VMEM_LIMIT),
        name="mix_sample_state",
    )(state, ft, qt, v)

    x1, h2, lg = pl.pallas_call(
        _mix_sample_out_kernel,
        out_shape=(jax.ShapeDtypeStruct((b, D_MODEL), F32),
                   jax.ShapeDtypeStruct((b, D_MODEL), BF16),
                   jax.ShapeDtypeStruct((b, LANES), F32)),
        compiler_params=cp,
        name="mix_sample_out",
    )(x, mod, o, gate, ypool, gout, wout, gpost, gffn, wrh, wrl)
    return x1, h2, lg, npool, s_new


def _first_max(cur, idx, big):
    m = jnp.max(cur, axis=0, keepdims=True)
    first = jnp.min(jnp.where(cur == m, idx, big), axis=0, keepdims=True)
    return idx == first


def _route(logit, bias):
    n = logit.shape[1]
    scores = jax.nn.sigmoid(logit)
    sel = scores + bias
    neg = -jnp.inf

    sub = lax.broadcasted_iota(jnp.int32, (GROUP_SIZE, n), 0)
    gscore = []
    for g in range(N_GROUPS):
        sg = sel[g * GROUP_SIZE:(g + 1) * GROUP_SIZE, :]
        m1 = jnp.max(sg, axis=0, keepdims=True)
        rest = jnp.where(_first_max(sg, sub, GROUP_SIZE), neg, sg)
        gscore.append(m1 + jnp.max(rest, axis=0, keepdims=True))
    cur = jnp.concatenate(gscore, axis=0)
    gidx = lax.broadcasted_iota(jnp.int32, (N_GROUPS, n), 0)
    gmask = jnp.zeros((N_GROUPS, n), jnp.bool_)
    for _ in range(TOPK_GROUPS):
        pick = _first_max(cur, gidx, N_GROUPS)
        gmask = gmask | pick
        cur = jnp.where(pick, neg, cur)
    emask = jnp.concatenate(
        [jnp.broadcast_to(gmask[g:g + 1, :], (GROUP_SIZE, n)) for g in range(N_GROUPS)], axis=0)

    cur = jnp.where(emask, sel, neg)
    eidx = lax.broadcasted_iota(jnp.int32, (N_EXPERTS, n), 0)
    picks = []
    for _ in range(TOP_K):
        pick = _first_max(cur, eidx, N_EXPERTS)
        picks.append(pick)
        cur = jnp.where(pick, neg, cur)
    return scores, picks


def _any(masks):
    return functools.reduce(jnp.logical_or, masks)


def _router_kernel(lg_ref, bias_ref, wd_ref):
    n = lg_ref.shape[0]
    scores, picks = _route(lg_ref[...].T[0:N_EXPERTS, :], bias_ref[0:N_EXPERTS, :])
    chosen = _any(picks)
    wsum = jnp.sum(jnp.where(chosen, scores, 0.0), axis=0, keepdims=True)
    wd = jnp.where(chosen, scores / wsum * ROUTED_SCALE, 0.0)
    wd_ref[...] = jnp.concatenate([wd, jnp.zeros((LANES - N_EXPERTS, n), F32)], axis=0).T


def _router(logits, bias_col):
    t = logits.shape[0]
    tile = min(ROUTER_TILE, t)
    return pl.pallas_call(
        _router_kernel,
        out_shape=jax.ShapeDtypeStruct((t, LANES), F32),
        grid=(t // tile,),
        in_specs=[pl.BlockSpec((tile, LANES), lambda i: (i, 0)), _full(bias_col.shape)],
        out_specs=pl.BlockSpec((tile, LANES), lambda i: (i, 0)),
        compiler_params=pltpu.CompilerParams(dimension_semantics=("parallel",), vmem_limit_bytes=VMEM_LIMIT),
        name="router",
    )(logits, bias_col)


def _plan_kernel(lg_ref, bias_ref, upper_ref, dest_ref, wsel_ref, cnt_ref, loff_ref):
    scores, picks = _route(lg_ref[...].T[0:N_EXPERTS, :], bias_ref[0:N_EXPERTS, :])
    chosen = _any(picks)
    chosen_f = jnp.where(chosen, 1.0, 0.0)
    cnt = jnp.sum(chosen_f, axis=1, keepdims=True)
    units = jnp.floor((cnt + (RUN_ALIGN - 1)) / RUN_ALIGN)
    ei = lax.broadcasted_iota(jnp.int32, (N_EXPERTS, N_EXPERTS), 0)
    ej = lax.broadcasted_iota(jnp.int32, (N_EXPERTS, N_EXPERTS), 1)
    before = jnp.where(ej < ei, 1.0, 0.0).astype(BF16)
    loff = RUN_ALIGN * _dot(before, jnp.broadcast_to(units, (N_EXPERTS, LANES)).astype(BF16))[:, 0:1]
    slot = loff + _dot(chosen_f.astype(BF16), upper_ref[...])
    wsum = jnp.sum(jnp.where(chosen, scores, 0.0), axis=0, keepdims=True)
    pick_sum = lambda v: jnp.concatenate(
        [jnp.sum(jnp.where(p, v, 0.0), axis=0, keepdims=True) for p in picks], axis=0)
    dest_ref[0] = (pick_sum(slot) * ROW_SUB).astype(jnp.int32)
    wsel_ref[0] = pick_sum(scores) / wsum * ROUTED_SCALE
    cnt_ref[0] = jnp.broadcast_to(cnt, (N_EXPERTS, LANES))
    loff_ref[0] = jnp.broadcast_to(loff, (N_EXPERTS, LANES))


def _plan(logits, bias_col, upper):
    t = logits.shape[0]
    nw = t // PLAN_WINDOW
    per_w = pl.BlockSpec((1, N_EXPERTS, LANES), lambda w: (w, 0, 0))
    picks = pl.BlockSpec((1, TOP_K, PLAN_WINDOW), lambda w: (w, 0, 0))
    return pl.pallas_call(
        _plan_kernel,
        out_shape=(jax.ShapeDtypeStruct((nw, TOP_K, PLAN_WINDOW), jnp.int32),
                   jax.ShapeDtypeStruct((nw, TOP_K, PLAN_WINDOW), F32),
                   jax.ShapeDtypeStruct((nw, N_EXPERTS, LANES), F32),
                   jax.ShapeDtypeStruct((nw, N_EXPERTS, LANES), F32)),
        grid=(nw,),
        in_specs=[pl.BlockSpec((PLAN_WINDOW, LANES), lambda w: (w, 0)), _full(bias_col.shape), _full(upper.shape)],
        out_specs=(picks, picks, per_w, per_w),
        compiler_params=pltpu.CompilerParams(dimension_semantics=("parallel",), vmem_limit_bytes=VMEM_LIMIT),
        name="plan",
    )(logits, bias_col, upper)


def _slot_rows(slot):
    return pl.ds(pl.multiple_of(slot * ROW_SUB, ROW_SUB), ROW_SUB)


def _table_copies(table_ref, flat_s, sem):
    return [pltpu.make_async_copy(table_ref.at[0, k], flat_s.at[pl.ds(k * PLAN_WINDOW, PLAN_WINDOW)], sem)
            for k in range(TOP_K)]


def _rows_at(offset):
    return pl.ds(pl.multiple_of(offset, ROW_SUB), ROW_SUB)


def _dispatch_kernel(dest_ref, h_hbm, x_ref, dest_s, rows_s, sem, hbuf, hsem):
    w = pl.program_id(0)
    nw = pl.num_programs(0)

    def rows_copy(win):
        slot = win % (DISPATCH_AHEAD + 1)
        return pltpu.make_async_copy(h_hbm.at[pl.ds(win * PLAN_WINDOW, PLAN_WINDOW)], hbuf.at[slot], hsem.at[slot])

    @pl.when(w == 0)
    def _():
        for d in range(DISPATCH_AHEAD):
            pl.when(d < nw)(lambda d=d: rows_copy(d).start())

    pl.when(w + DISPATCH_AHEAD < nw)(lambda: rows_copy(w + DISPATCH_AHEAD).start())

    plan_copies = _table_copies(dest_ref, dest_s, sem)
    for c in plan_copies:
        c.start()
    x_ref[...] = jnp.zeros_like(x_ref)
    rows_copy(w).wait()
    h_ref = hbuf.at[w % (DISPATCH_AHEAD + 1)]
    for j in range(ROW_SUB):
        rows_s[pl.ds(j, PLAN_WINDOW, stride=ROW_SUB), :] = h_ref[:, j * LANES:(j + 1) * LANES]
    for c in plan_copies:
        c.wait()

    def body(i, carry):
        for u in range(SCATTER_UNROLL):
            t = i * SCATTER_UNROLL + u
            row = rows_s[_slot_rows(t), :]
            for k in range(TOP_K):
                x_ref[_rows_at(dest_s[k * PLAN_WINDOW + t]), :] = row
        return carry

    lax.fori_loop(0, PLAN_WINDOW // SCATTER_UNROLL, body, 0)


def _dispatch(dest_w, h2):
    nw = dest_w.shape[0]
    region = REGION_ROWS * ROW_SUB
    return pl.pallas_call(
        _dispatch_kernel,
        out_shape=jax.ShapeDtypeStruct((nw * region, LANES), jnp.uint32),
        grid=(nw,),
        in_specs=[pl.BlockSpec((1, TOP_K, PLAN_WINDOW), lambda w: (w, 0, 0)),
                  pl.BlockSpec(memory_space=pl.ANY)],
        out_specs=pl.BlockSpec((region, LANES), lambda w: (w, 0)),
        scratch_shapes=[pltpu.SMEM((TOP_K * PLAN_WINDOW,), jnp.int32),
                        pltpu.VMEM((PLAN_WINDOW * ROW_SUB, LANES), jnp.uint32),
                        pltpu.SemaphoreType.DMA,
                        pltpu.VMEM((DISPATCH_AHEAD + 1, PLAN_WINDOW, ROW_WORDS), jnp.uint32),
                        pltpu.SemaphoreType.DMA((DISPATCH_AHEAD + 1,))],
        compiler_params=pltpu.CompilerParams(dimension_semantics=("arbitrary",), vmem_limit_bytes=VMEM_LIMIT),
        name="dispatch",
    )(dest_w, h2)


def _unpack_rows(ref, n):
    parts = [_unpack_pair(ref[pl.ds(j, n, stride=ROW_SUB), :]) for j in range(ROW_SUB)]
    return jnp.concatenate([p[0] for p in parts] + [p[1] for p in parts], axis=1)


def _ffn_kernel(gq_ref, gs_ref, to_ref, tn_ref, x_hbm, w1_ref, w3_ref, w2_ref,
                hs_ref, wds_ref, x1s_ref, gt2s_ref, gpost_ref, ws1_ref, ws3_ref, ws2_ref, y_hbm, ys_ref,
                w13_s, w2_s, xbuf, ybuf, rsem, wsem, acc_s):
    del x_hbm
    e = pl.program_id(0)
    n_groups = gq_ref[N_EXPERTS]
    block = RUN_ALIGN * ROW_SUB
    tile = FFN_TILE * ROW_SUB

    def go(cp, start):
        if start:
            cp.start()
        else:
            cp.wait()

    def fetch(q, start):
        def one(j):
            src = y_hbm.at[pl.ds(pl.multiple_of(to_ref[gs_ref[q] + j] * block, block), tile)]
            go(pltpu.make_async_copy(src, xbuf.at[q % FFN_RING, pl.ds(j * tile, tile)], rsem.at[q % FFN_RING]), start)

        size = gs_ref[q + 1] - gs_ref[q]

        @pl.when(size == FFN_GROUP)
        def _():
            for j in range(FFN_GROUP):
                one(j)

        @pl.when(size < FFN_GROUP)
        def _():
            for j in range(FFN_GROUP - 1):
                pl.when(j < size)(functools.partial(one, j))

    def writeback(q, start):
        most = FFN_TILE // RUN_ALIGN - 1

        def blocks(j, b, n):
            dst = pl.multiple_of(to_ref[gs_ref[q] + j] * block, block) + b * block
            go(pltpu.make_async_copy(ybuf.at[q % FFN_RING, pl.ds(j * tile + b * block, n * block)],
                                     y_hbm.at[pl.ds(dst, n * block)], wsem.at[q % FFN_RING]), start)

        for j in range(FFN_GROUP):
            item = gs_ref[q] + j
            n = jnp.where(item < gs_ref[q + 1], tn_ref[item], 0)
            pl.when(n >= most)(functools.partial(blocks, j, 0, most))
            pl.when(n > most)(functools.partial(blocks, j, most, 1))

            @pl.when((n > 0) & (n < most))
            def _():
                for b in range(most - 1):
                    pl.when(b < n)(functools.partial(blocks, j, b, 1))

    @pl.when(e == 0)
    def _():
        xbuf[...] = jnp.zeros_like(xbuf)
        for d in range(FFN_RING - 1):
            pl.when(d < n_groups)(functools.partial(fetch, d, True))

    w13_s[:, 0:D_EXPERT] = w1_ref[0].astype(BF16)
    w13_s[:, D_EXPERT:] = w3_ref[0].astype(BF16)
    w2_s[...] = w2_ref[0].astype(BF16)

    xs = hs_ref[...]

    @pl.when(e == 0)
    def _():
        sh = _silu(_dot(xs, ws1_ref[...].astype(BF16))) * _dot(xs, ws3_ref[...].astype(BF16))
        acc_s[...] = _dot(sh.astype(BF16), ws2_ref[...].astype(BF16))

    lane = lax.broadcasted_iota(jnp.int32, wds_ref.shape, 1)
    wcol = jnp.sum(jnp.where(lane == e, wds_ref[...], 0.0), axis=1, keepdims=True)
    rows = FFN_GROUP * FFN_TILE

    def ffn(x, ride):
        n = 0 if x is None else x.shape[0]
        if ride:
            x = xs if x is None else jnp.concatenate([x, xs], axis=0)
        h = _dot(x, w13_s[...])
        y = _dot((_silu(h[:, :D_EXPERT]) * h[:, D_EXPERT:]).astype(BF16), w2_s[...])
        if ride:
            acc_s[...] += y[n:] * wcol
        return y[:n] if n else None

    def group(q, ride):
        @pl.when(q + FFN_RING - 1 < n_groups)
        def _():
            fetch(q + FFN_RING - 1, True)

        @pl.when(q >= FFN_RING)
        def _():
            writeback(q - FFN_RING, False)

        fetch(q, False)
        y = ffn(_unpack_rows(xbuf.at[q % FFN_RING], rows).astype(BF16), ride)
        out = ybuf.at[q % FFN_RING]
        for j in range(ROW_SUB):
            out[pl.ds(j, rows, stride=ROW_SUB), :] = _pack_pair(
                y[:, j * LANES:(j + 1) * LANES], y[:, ROW_WORDS + j * LANES:ROW_WORDS + (j + 1) * LANES])
        writeback(q, True)

    first, end = gq_ref[e], gq_ref[e + 1]
    pl.when(first < end)(functools.partial(group, first, True))
    pl.when(first == end)(functools.partial(ffn, None, True))

    def group_body(q, carry):
        group(q, False)
        return carry

    lax.fori_loop(first + 1, end, group_body, 0)

    @pl.when(e == pl.num_programs(0) - 1)
    def _():
        ys_ref[...] = x1s_ref[...] + gt2s_ref[...] * _rms(acc_s[...], gpost_ref[...])

    @pl.when(e == pl.num_programs(0) - 1)
    def _():
        for d in range(FFN_RING):
            @pl.when(n_groups - 1 - d >= 0)
            def _():
                writeback(n_groups - 1 - d, False)


def _expert_ffn(group_of_expert, group_start, item_off, item_blocks, x_slots, w1, w3, w2,
                h2_s, wd_s, x1_s, gt2_s, gpost, ws1, ws3, ws2):
    of_expert = lambda e, *_: (e, 0, 0)
    whole = lambda a: pl.BlockSpec(a.shape, lambda e, *_: (0,) * a.ndim)
    tiles = pltpu.VMEM((FFN_RING, FFN_GROUP * FFN_TILE * ROW_SUB, LANES), jnp.uint32)
    sems = pltpu.SemaphoreType.DMA((FFN_RING,))
    dense = (h2_s, wd_s, x1_s, gt2_s, gpost, ws1, ws3, ws2)
    grid_spec = pltpu.PrefetchScalarGridSpec(
        num_scalar_prefetch=4,
        grid=(N_EXPERTS,),
        in_specs=[pl.BlockSpec(memory_space=pl.ANY),
                  pl.BlockSpec((1, D_MODEL, D_EXPERT), of_expert),
                  pl.BlockSpec((1, D_MODEL, D_EXPERT), of_expert),
                  pl.BlockSpec((1, D_EXPERT, D_MODEL), of_expert)] + [whole(a) for a in dense],
        out_specs=(pl.BlockSpec(memory_space=pl.ANY), whole(x1_s)),
        scratch_shapes=[pltpu.VMEM((D_MODEL, 2 * D_EXPERT), BF16), pltpu.VMEM((D_EXPERT, D_MODEL), BF16),
                        tiles, tiles, sems, sems, pltpu.VMEM(x1_s.shape, F32)])
    return pl.pallas_call(
        _ffn_kernel,
        out_shape=(jax.ShapeDtypeStruct(x_slots.shape, jnp.uint32), jax.ShapeDtypeStruct(x1_s.shape, F32)),
        grid_spec=grid_spec,
        input_output_aliases={4: 0},
        compiler_params=pltpu.CompilerParams(dimension_semantics=("arbitrary",), vmem_limit_bytes=VMEM_LIMIT),
        name="expert_ffn",
    )(group_of_expert, group_start, item_off, item_blocks, x_slots, w1, w3, w2, *dense)


def _work_items(cnt, loff):
    nw = cnt.shape[0]
    cnt = cnt.astype(jnp.int32).T.reshape(-1)
    loff = loff.astype(jnp.int32).T.reshape(-1)
    region = jnp.tile(jnp.arange(nw, dtype=jnp.int32) * REGION_ROWS, N_EXPERTS)
    n_items = nw * N_EXPERTS + (nw * PLAN_WINDOW * TOP_K) // FFN_TILE
    tiles = (cnt + FFN_TILE - 1) // FFN_TILE
    ends = jnp.cumsum(tiles)
    starts = ends - tiles
    i = jnp.arange(n_items + FFN_GROUP, dtype=jnp.int32)[:, None]
    mine = (starts[None, :] <= i) & (i < ends[None, :])
    of_pair = lambda v: jnp.sum(jnp.where(mine, v[None, :], 0), axis=1)
    r = i[:, 0] - of_pair(starts)
    row = of_pair(region + loff) + r * FFN_TILE
    own = jnp.clip(of_pair(cnt) - r * FFN_TILE, 0, FFN_TILE)

    zero = jnp.zeros((1,), jnp.int32)
    item_end = ends.reshape(N_EXPERTS, nw)[:, -1]
    item_start = jnp.concatenate([zero, item_end[:-1]])
    group_end = jnp.cumsum((item_end - item_start + FFN_GROUP - 1) // FFN_GROUP)
    group_first = jnp.concatenate([zero, group_end[:-1]])
    q = jnp.arange(N_EXPERTS + n_items // FFN_GROUP + 1, dtype=jnp.int32)[:, None]
    has = (group_first[None, :] <= q) & (q < group_end[None, :])
    of_expert = lambda v: jnp.sum(jnp.where(has, v[None, :], 0), axis=1)
    group_start = jnp.where(q[:, 0] < group_end[-1],
                            of_expert(item_start) + (q[:, 0] - of_expert(group_first)) * FFN_GROUP, item_end[-1])
    return (jnp.concatenate([zero, group_end]), group_start, row // RUN_ALIGN, (own + RUN_ALIGN - 1) // RUN_ALIGN)


def _combine_kernel(dest_ref, wsel_ref, y_hbm, h_ref, x1_ref, gt2_ref, gpost_ref, ws13_ref, ws2_ref, out_ref,
                    dest_s, wsel_s, lo_s, hi_s, ybuf, sem, ysem):
    w, s = pl.program_id(0), pl.program_id(1)
    region = REGION_ROWS * ROW_SUB
    part = region // REGION_PARTS

    def region_copies(win):
        return [pltpu.make_async_copy(y_hbm.at[pl.ds(win * region + p * part, part)],
                                      ybuf.at[win % 2, pl.ds(p * part, part)], ysem.at[win % 2, p])
                for p in range(REGION_PARTS)]

    @pl.when(s == 0)
    def _():
        copies = _table_copies(dest_ref, dest_s, sem.at[0]) + _table_copies(wsel_ref, wsel_s, sem.at[1])
        for c in copies:
            c.start()

        @pl.when(w == 0)
        def _():
            for c in region_copies(w):
                c.start()

        @pl.when(w + 1 < pl.num_programs(0))
        def _():
            for c in region_copies(w + 1):
                c.start()

        for c in copies + region_copies(w):
            c.wait()

    y_ref = ybuf.at[w % 2]

    base = s * COMBINE_TILE

    def body(i, carry):
        for u in range(COMBINE_UNROLL):
            t = i * COMBINE_UNROLL + u
            acc_lo = jnp.zeros((ROW_SUB, LANES), F32)
            acc_hi = jnp.zeros((ROW_SUB, LANES), F32)
            for k in range(TOP_K):
                at = k * PLAN_WINDOW + base + t
                lo, hi = _unpack_pair(y_ref[_rows_at(dest_s[at]), :])
                w = wsel_s[at]
                acc_lo = acc_lo + w * lo
                acc_hi = acc_hi + w * hi
            lo_s[_slot_rows(t), :] = acc_lo
            hi_s[_slot_rows(t), :] = acc_hi
        return carry

    lax.fori_loop(0, COMBINE_TILE // COMBINE_UNROLL, body, 0)
    rows = lambda ref: [ref[pl.ds(j, COMBINE_TILE, stride=ROW_SUB), :] for j in range(ROW_SUB)]
    routed = jnp.concatenate(rows(lo_s) + rows(hi_s), axis=1)
    h_lo, h_hi = _unpack_pair(h_ref[...])
    x = jnp.concatenate([h_lo, h_hi], axis=1).astype(BF16)
    hs = _dot(x, ws13_ref[...])
    act = _silu(hs[:, :D_EXPERT]) * hs[:, D_EXPERT:]
    ff = routed + _dot(act.astype(BF16), ws2_ref[...])
    out_ref[...] = x1_ref[...] + gt2_ref[0] * _rms(ff, gpost_ref[...])


def _combine(dest_w, wsel_w, y_slots, h2, x1, gt2, tokens_per_gt2, gpost, ws13, ws2):
    nw = dest_w.shape[0]
    sub = PLAN_WINDOW // COMBINE_TILE
    plan = pl.BlockSpec((1, TOP_K, PLAN_WINDOW), lambda w, s: (w, 0, 0))
    tok = lambda w, s: (w * sub + s, 0)
    return pl.pallas_call(
        _combine_kernel,
        out_shape=jax.ShapeDtypeStruct(x1.shape, F32),
        grid=(nw, sub),
        in_specs=[plan, plan,
                  pl.BlockSpec(memory_space=pl.ANY),
                  pl.BlockSpec((COMBINE_TILE, ROW_WORDS), tok),
                  pl.BlockSpec((COMBINE_TILE, D_MODEL), tok),
                  pl.BlockSpec((1, 1, D_MODEL), lambda w, s: ((w * PLAN_WINDOW) // tokens_per_gt2, 0, 0)),
                  _full((1, D_MODEL)), _full(ws13.shape), _full(ws2.shape)],
        out_specs=pl.BlockSpec((COMBINE_TILE, D_MODEL), tok),
        scratch_shapes=[pltpu.SMEM((TOP_K * PLAN_WINDOW,), jnp.int32),
                        pltpu.SMEM((TOP_K * PLAN_WINDOW,), F32),
                        pltpu.VMEM((COMBINE_TILE * ROW_SUB, LANES), F32),
                        pltpu.VMEM((COMBINE_TILE * ROW_SUB, LANES), F32),
                        pltpu.VMEM((2, REGION_ROWS * ROW_SUB, LANES), jnp.uint32),
                        pltpu.SemaphoreType.DMA((2,)), pltpu.SemaphoreType.DMA((2, REGION_PARTS))],
        compiler_params=pltpu.CompilerParams(dimension_semantics=("arbitrary", "arbitrary"),
                                             vmem_limit_bytes=VMEM_LIMIT),
        name="combine",
    )(dest_w, wsel_w, y_slots, h2, x1, gt2, gpost, ws13, ws2)


def kernel(x_prompt, x_sample, c_prompt, c_sample, state_pool, state_hgrn, w_ada, b_ada, g_pre_mix, g_post_mix,
           w_in, w_pool, pool_scale, lb_logits, g_out_norm, w_out, g_pre_ffn, g_post_ffn, w_router, router_bias,
           w_exp_gate, w_exp_up, w_exp_down, w_sh_gate, w_sh_up, w_sh_down):
    assert w_ada.shape[0] == 1 and lb_logits.shape[0] == 2, "single-layer trunk"
    bp, lp, d = x_prompt.shape
    bs = x_sample.shape[0]
    row = lambda a: a[0].reshape(1, -1)

    n_mod = bp + bs
    pad = (-n_mod) % 16
    c_all = jnp.concatenate([c_prompt, c_sample, jnp.zeros((pad, d), F32)], axis=0)
    mod = _ada(c_all, w_ada[0], b_ada)
    mod_p = mod[:bp].reshape(bp, 1, 6 * d)
    mod_s = mod[bp:n_mod]

    win = w_in[0].astype(BF16)
    wout = w_out[0].astype(BF16)
    wpool = w_pool[0].astype(BF16)
    wr = jnp.pad(w_router[0], ((0, 0), (0, LANES - N_EXPERTS)))
    wrh = wr.astype(BF16)
    wrl = (wr - wrh.astype(F32)).astype(BF16)
    m2 = jnp.asarray(_M2_NP, BF16)
    gpre, gpost, gffn, gffn_post = row(g_pre_mix), row(g_post_mix), row(g_pre_ffn), row(g_post_ffn)
    pscale, gout = row(pool_scale), row(g_out_norm)
    bias_col = jnp.pad(router_bias[0], (0, LANES - N_EXPERTS)).reshape(LANES, 1)

    x1_p, h2_p, lg_p, pool_p, st_p = _mix_prompt(x_prompt, mod_p, gpre, gpost, gffn, win, wpool, pscale,
                                                  lb_logits, gout, wout, wrh, wrl, m2)
    x1_s, h2_s, lg_s, pool_s, st_s = _mix_sample(x_sample[:, 0, :], mod_s, state_pool[0], state_hgrn[0],
                                                  gpre, gpost, gffn, win, wpool, pscale, lb_logits, gout, wout,
                                                  wrh, wrl)

    experts = (w_exp_gate[0], w_exp_up[0], w_exp_down[0], w_sh_gate[0], w_sh_up[0], w_sh_down[0])
    tp = bp * lp
    nw = tp // PLAN_WINDOW
    assert lp % PLAN_WINDOW == 0
    upper = jnp.asarray(np.triu(np.ones((PLAN_WINDOW, PLAN_WINDOW), np.float32), 1), BF16)
    dest, wsel, cnt, loff = _plan(lg_p.reshape(tp, LANES), bias_col, upper)
    dest_w, wsel_w = dest, wsel
    h2_rows = h2_p.reshape(tp, ROW_WORDS)
    x_slots = _dispatch(dest_w, h2_rows)
    wd_s = _router(lg_s, bias_col)
    y_slots, y_s = _expert_ffn(*_work_items(cnt[:, :, 0], loff[:, :, 0]), x_slots, *experts[:3],
                               h2_s, wd_s, x1_s, mod_s[:, 5 * d:], gffn_post, *experts[3:])
    ws13 = jnp.concatenate([experts[3], experts[4]], axis=1).astype(BF16)
    gt2_p = mod_p[:, :, 5 * d:]
    y_p = _combine(dest_w, wsel_w, y_slots, h2_rows, x1_p.reshape(tp, d), gt2_p, lp, gffn_post,
                   ws13, experts[5].astype(BF16))

    return (y_p.reshape(bp, lp, d), y_s.reshape(bs, 1, d), pool_p[None], st_p[None], pool_s[None], st_s[None])
```

```python
import functools
import itertools

import numpy as np
import jax
import jax.numpy as jnp
from jax import lax
from jax.experimental import pallas as pl
from jax.experimental.pallas import tpu as pltpu

F32 = jnp.float32
BF16 = jnp.bfloat16

D_MODEL = 1024
POOL_WIDTH = 512
POOL_WINDOWS = (2, 4, 8, 16)
POOL_GROUP = 128
POOL_HIST = 15
HG_WIDTH = 512
HG_HEADS = 4
HG_HEAD_DIM = 128
IN_WIDTH = POOL_WIDTH + 4 * HG_WIDTH
N_EXPERTS = 64
TOP_K = 8
N_GROUPS = 8
TOPK_GROUPS = 4
GROUP_SIZE = N_EXPERTS // N_GROUPS
D_EXPERT = 256
ROUTED_SCALE = 2.5
EPS = 1e-6

LANES = 128
CHUNK = 64
TIME_TILE = 512
ADA_TILE_N = 2048
ROUTER_TILE = 512
SAMPLE_STATE_BLOCK = 32
PLAN_WINDOW = 1024
RUN_ALIGN = 32
FFN_TILE = 160
REGION_ROWS = PLAN_WINDOW * TOP_K + N_EXPERTS * (RUN_ALIGN - 1) + FFN_TILE
ROW_WORDS = D_MODEL // 2
ROW_SUB = ROW_WORDS // LANES
COMBINE_TILE = 256
REGION_PARTS = 4
SCATTER_UNROLL = 4
COMBINE_UNROLL = 16
FFN_GROUP = 4
FFN_RING = 6
VMEM_LIMIT = 58 * 1024 * 1024

_LEVELS = (64, 32, 16, 8, 4, 2)
_BLK_CUM = 0


def _level_blocks():
    c = CHUNK
    i = np.arange(c)[:, None]
    s = np.arange(c)[None, :]
    blocks = [(s <= i)]
    index = {}
    for lvl in _LEVELS:
        ref = (i // lvl) * lvl + lvl // 2 - 1
        index[lvl] = len(blocks)
        blocks.append(s <= ref)
    m = np.concatenate(blocks, axis=0).astype(np.float32)
    return np.concatenate([m, m], axis=1), index


_M2_NP, _LEVEL_INDEX = _level_blocks()


def _dot(a, b):
    return jnp.dot(a, b, preferred_element_type=F32)


def _dot_nt(a, b):
    return lax.dot_general(a, b, (((1,), (1,)), ((), ())), preferred_element_type=F32)


def _split2(x):
    hi = x.astype(BF16)
    lo = (x - hi.astype(F32)).astype(BF16)
    return hi, lo


def _pack_pair(a, b):
    lo = lax.bitcast_convert_type(a.astype(BF16).astype(F32), jnp.uint32)
    hi = lax.bitcast_convert_type(b.astype(BF16).astype(F32), jnp.uint32)
    return (lo >> 16) | hi


def _unpack_pair(words):
    lo = lax.bitcast_convert_type(words << 16, F32)
    hi = lax.bitcast_convert_type(words & jnp.uint32(0xFFFF0000), F32)
    return lo, hi


def _silu(x):
    return x * jax.nn.sigmoid(x)


def _rms(x, g):
    return x * lax.rsqrt(jnp.mean(x * x, axis=-1, keepdims=True) + EPS) * g


def _mods(m):
    return [m[:, j * D_MODEL:(j + 1) * D_MODEL] for j in range(6)]


def _forget_lower_bound(lbl):
    mx = jnp.max(lbl, axis=0, keepdims=True)
    e = jnp.exp(lbl - mx)
    return e[0:1] / jnp.sum(e, axis=0, keepdims=True)


def _router_logits(h2, wrh, wrl):
    hi, lo = _split2(h2)
    return _dot(hi, wrh) + _dot(lo, wrh) + _dot(hi, wrl)


def _ada_kernel(c_ref, w_ref, b_ref, o_ref):
    a_hi, a_lo = _split2(_silu(c_ref[...]))
    w_hi, w_lo = _split2(w_ref[...])
    o_ref[...] = _dot(a_hi, w_hi) + _dot(a_lo, w_hi) + _dot(a_hi, w_lo) + b_ref[...]


def _ada(c_all, w_ada, b_ada):
    rows = c_all.shape[0]
    n = w_ada.shape[1]
    return pl.pallas_call(
        _ada_kernel,
        out_shape=jax.ShapeDtypeStruct((rows, n), F32),
        grid=(n // ADA_TILE_N,),
        in_specs=[pl.BlockSpec((rows, D_MODEL), lambda j: (0, 0)),
                  pl.BlockSpec((D_MODEL, ADA_TILE_N), lambda j: (0, j)),
                  pl.BlockSpec((1, ADA_TILE_N), lambda j: (0, j))],
        out_specs=pl.BlockSpec((rows, ADA_TILE_N), lambda j: (0, j)),
        compiler_params=pltpu.CompilerParams(dimension_semantics=("parallel",),
                                             vmem_limit_bytes=VMEM_LIMIT),
        name="ada",
    )(c_all, w_ada, b_ada)


def _level_masks():
    i = lax.broadcasted_iota(jnp.int32, (CHUNK, 2 * CHUNK), 0)
    j = lax.broadcasted_iota(jnp.int32, (CHUNK, 2 * CHUNK), 1) & (CHUNK - 1)
    masks = {}
    for lvl in _LEVELS:
        sh = lvl.bit_length() - 1
        same = (i >> sh) == (j >> sh)
        upper = ((i >> (sh - 1)) & 1) == 1
        lower = ((j >> (sh - 1)) & 1) == 0
        masks[lvl] = same & upper & lower
    return masks, i == j


def _by_head(m):
    z = jnp.zeros((m.shape[0], HG_HEAD_DIM), m.dtype)
    return jnp.concatenate([jnp.concatenate([m[:, :HG_HEAD_DIM], z], axis=1),
                            jnp.concatenate([z, m[:, HG_HEAD_DIM:]], axis=1)], axis=0)


def _hgrn_chunk_pair(q, k, v, e_all, lane0, sts, masks, eye):
    hd = HG_HEAD_DIM

    def blk(n):
        return e_all[n * CHUNK:(n + 1) * CHUNK, lane0:lane0 + 2 * hd]

    b = blk(_BLK_CUM)
    kb = k.astype(BF16)
    a = jnp.where(eye, _dot_nt(q.astype(BF16), _by_head(kb)), 0.0)
    for lvl in _LEVELS:
        d = b - blk(_LEVEL_INDEX[lvl])
        ql = (q * jnp.exp(jnp.minimum(d, 0.0))).astype(BF16)
        kl = (k * jnp.exp(jnp.minimum(-d, 0.0))).astype(BF16) if lvl > 2 else kb
        a = a + jnp.where(masks[lvl], _dot_nt(ql, _by_head(kl)), 0.0)
    st_both = jnp.concatenate([jnp.concatenate([sts[0], jnp.zeros_like(sts[0])], axis=1),
                               jnp.concatenate([jnp.zeros_like(sts[1]), sts[1]], axis=1)], axis=0).astype(BF16)
    o = _dot(a.astype(BF16), _by_head(v.astype(BF16))) + _dot_nt((q * jnp.exp(b)).astype(BF16), st_both)
    b_last = b[CHUNK - 1:CHUNK, :]
    k_end = (k * jnp.exp(b_last - b)).astype(BF16)
    decay = jnp.exp(b_last)
    new = []
    for h in range(2):
        lanes = slice(h * hd, (h + 1) * hd)
        new.append(sts[h] * decay[:, lanes] + _dot(v[:, lanes].T.astype(BF16), k_end[:, lanes]))
    return o, new


def _mix_in(x_ref, mod_ref, t, gpre_ref, win_ref, wpool_ref, pscale_ref, lbl_ref, ubuf, dst):
    tt = TIME_TILE
    q_s, k_s, v_s, g_s, gate_s, yp_s = dst
    xt = x_ref[0]
    sh1, sc1 = _mods(mod_ref[0])[:2]
    h = (_rms(xt, gpre_ref[...]) * (1.0 + sc1) + sh1).astype(BF16)
    yield
    proj = _dot(h, win_ref[...])
    yield

    u = proj[:, :POOL_WIDTH]
    ubuf[16:16 + tt, :] = u
    pos = (t * tt + lax.broadcasted_iota(jnp.int32, (tt, 1), 0) + 1).astype(F32)
    ys = []
    for g, w in enumerate(POOL_WINDOWS):
        s = ubuf[:, g * POOL_GROUP:(g + 1) * POOL_GROUP]
        off = 0
        for step in range(w.bit_length() - 1):
            sh = 1 << step
            s = s[sh:, :] + s[:-sh, :]
            off += sh
        ws = s[16 - off:16 - off + tt, :]
        cnt = jnp.minimum(pos, float(w))
        d = ws / cnt - u[:, g * POOL_GROUP:(g + 1) * POOL_GROUP]
        ys.append(_dot(d.astype(BF16), wpool_ref[g]))
    yp_s[...] = jnp.concatenate(ys, axis=1) * pscale_ref[...]
    ubuf[0:16, :] = ubuf[tt:tt + 16, :]

    lb = _forget_lower_bound(lbl_ref[...])
    f = lb + (1.0 - lb) * jax.nn.sigmoid(proj[:, POOL_WIDTH + HG_WIDTH:POOL_WIDTH + 2 * HG_WIDTH])
    q_s[...] = _silu(proj[:, POOL_WIDTH:POOL_WIDTH + HG_WIDTH])
    k_s[...] = 1.0 - f
    v_s[...] = proj[:, POOL_WIDTH + 2 * HG_WIDTH:POOL_WIDTH + 3 * HG_WIDTH]
    g_s[...] = jnp.log(f)
    gate_s[...] = _silu(proj[:, POOL_WIDTH + 3 * HG_WIDTH:])


def _mix_out(x_ref, mod_ref, src, gpost_ref, gffn_ref, gout_ref, wout_ref, wrh_ref, wrl_ref, m2_ref,
             st_s, o_s, x1_ref, h2_ref, lg_ref):
    tt = TIME_TILE
    q_s, k_s, v_s, g_s, gate_s, yp_s = src
    masks, eye = _level_masks()
    m2 = m2_ref[...]

    states = [st_s[hd] for hd in range(HG_HEADS)]
    for c in range(tt // CHUNK):
        rows = slice(c * CHUNK, (c + 1) * CHUNK)
        g_hi, g_lo = _split2(g_s[rows, :])
        e_all = _dot(m2, jnp.concatenate([g_hi, g_lo], axis=0))
        for hd in range(0, HG_HEADS, 2):
            lane0 = hd * HG_HEAD_DIM
            lanes = slice(lane0, lane0 + 2 * HG_HEAD_DIM)
            o, states[hd:hd + 2] = _hgrn_chunk_pair(q_s[rows, lanes], k_s[rows, lanes], v_s[rows, lanes], e_all,
                                                    lane0, states[hd:hd + 2], masks, eye)
            o_s[rows, lanes] = o
    for hd in range(HG_HEADS):
        st_s[hd] = states[hd]
    yield

    o = o_s[...]
    os_ = []
    for hd in range(HG_HEADS):
        oh = o[:, hd * HG_HEAD_DIM:(hd + 1) * HG_HEAD_DIM]
        os_.append(oh * lax.rsqrt(jnp.mean(oh * oh, axis=-1, keepdims=True) + EPS) * gout_ref[...])
    o_n = jnp.concatenate(os_, axis=1) * gate_s[...]

    mix = (_dot(yp_s[...].astype(BF16), wout_ref[0:POOL_WIDTH, :])
           + _dot(o_n.astype(BF16), wout_ref[POOL_WIDTH:, :]))
    yield
    _, _, gt1, sh2, sc2, _ = _mods(mod_ref[0])
    x1 = x_ref[0] + gt1 * _rms(mix, gpost_ref[...])
    h2 = _rms(x1, gffn_ref[...]) * (1.0 + sc2) + sh2
    x1_ref[0] = x1
    h2_ref[0] = _pack_pair(h2[:, :ROW_WORDS], h2[:, ROW_WORDS:])
    lg_ref[0] = _router_logits(h2, wrh_ref[...], wrl_ref[...])


def _mix_prompt_kernel(x_ref, xn_ref, mod_ref, modn_ref, gpre_ref, gpost_ref, gffn_ref, win_ref, wpool_ref,
                       pscale_ref, lbl_ref, gout_ref, wout_ref, wrh_ref, wrl_ref, m2_ref,
                       x1_ref, h2_ref, lg_ref, pool_ref, st_ref,
                       st_s, ubuf, o_s, *sets, n_t):
    i = pl.program_id(0)
    t = i % n_t
    t_next = jnp.minimum(i + 1, pl.num_programs(0) - 1) % n_t
    mix_in = functools.partial(_mix_in, gpre_ref=gpre_ref, win_ref=win_ref, wpool_ref=wpool_ref,
                               pscale_ref=pscale_ref, lbl_ref=lbl_ref, ubuf=ubuf)
    cur = [s.at[i % 2] for s in sets]
    nxt = [s.at[(i + 1) % 2] for s in sets]

    def clear_history():
        ubuf[0:16, :] = jnp.zeros((16, POOL_WIDTH), F32)

    @pl.when(t == 0)
    def _():
        st_s[...] = jnp.zeros_like(st_s)

    @pl.when(i == 0)
    def _():
        clear_history()
        for _ in mix_in(x_ref, mod_ref, 0, dst=cur):
            pass

    pl.when((i + 1) % n_t == 0)(clear_history)

    finish = _mix_out(x_ref, mod_ref, cur, gpost_ref, gffn_ref, gout_ref, wout_ref, wrh_ref, wrl_ref,
                      m2_ref, st_s, o_s, x1_ref, h2_ref, lg_ref)
    prepare = mix_in(xn_ref, modn_ref, t_next, dst=nxt)
    for _ in itertools.zip_longest(finish, prepare):
        pass

    @pl.when((i + 1) % n_t == n_t - 1)
    def _():
        pool_ref[0] = ubuf[16 - POOL_HIST:16, :]

    @pl.when(t == n_t - 1)
    def _():
        for hd in range(HG_HEADS):
            st_ref[0, hd] = st_s[hd].T


def _full(shape):
    nd = len(shape)
    return pl.BlockSpec(shape, lambda *_: (0,) * nd)


def _mix_prompt(x, mod, gpre, gpost, gffn, win, wpool, pscale, lbl, gout, wout, wrh, wrl, m2):
    b, l, d = x.shape
    tt = TIME_TILE
    n_t = l // tt
    steps = b * n_t
    tile = lambda i: (i // n_t, i % n_t, 0)
    per_b = lambda i: (i // n_t, 0, 0)
    ahead = lambda i: jnp.minimum(i + 1, steps - 1)
    tile_next = lambda i: (ahead(i) // n_t, ahead(i) % n_t, 0)
    per_b_next = lambda i: (ahead(i) // n_t, 0, 0)
    recurrence_inputs = [pltpu.VMEM((2, tt, HG_WIDTH), F32)] * 5 + [pltpu.VMEM((2, tt, POOL_WIDTH), F32)]
    return pl.pallas_call(
        functools.partial(_mix_prompt_kernel, n_t=n_t),
        out_shape=(jax.ShapeDtypeStruct((b, l, d), F32),
                   jax.ShapeDtypeStruct((b, l, ROW_WORDS), jnp.uint32),
                   jax.ShapeDtypeStruct((b, l, LANES), F32),
                   jax.ShapeDtypeStruct((b, POOL_HIST, POOL_WIDTH), F32),
                   jax.ShapeDtypeStruct((b, HG_HEADS, HG_HEAD_DIM, HG_HEAD_DIM), F32)),
        grid=(steps,),
        in_specs=[pl.BlockSpec((1, tt, d), tile), pl.BlockSpec((1, tt, d), tile_next),
                  pl.BlockSpec((1, 1, 6 * d), per_b), pl.BlockSpec((1, 1, 6 * d), per_b_next),
                  _full((1, d)), _full((1, d)), _full((1, d)),
                  _full(win.shape), _full(wpool.shape), _full((1, POOL_WIDTH)),
                  _full(lbl.shape), _full((1, HG_HEAD_DIM)), _full(wout.shape),
                  _full(wrh.shape), _full(wrl.shape), _full(m2.shape)],
        out_specs=(pl.BlockSpec((1, tt, d), tile),
                   pl.BlockSpec((1, tt, ROW_WORDS), tile),
                   pl.BlockSpec((1, tt, LANES), tile),
                   pl.BlockSpec((1, POOL_HIST, POOL_WIDTH), per_b),
                   pl.BlockSpec((1, HG_HEADS, HG_HEAD_DIM, HG_HEAD_DIM), lambda i: (i // n_t, 0, 0, 0))),
        scratch_shapes=[pltpu.VMEM((HG_HEADS, HG_HEAD_DIM, HG_HEAD_DIM), F32),
                        pltpu.VMEM((tt + 16, POOL_WIDTH), F32),
                        pltpu.VMEM((tt, HG_WIDTH), F32)] + recurrence_inputs,
        compiler_params=pltpu.CompilerParams(dimension_semantics=("arbitrary",),
                                             vmem_limit_bytes=VMEM_LIMIT),
        name="mix_prompt",
    )(x, x, mod, mod, gpre, gpost, gffn, win, wpool, pscale, lbl, gout, wout, wrh, wrl, m2)


def _mix_sample_in_kernel(x_ref, mod_ref, gpre_ref, win_ref, wpool_ref, pscale_ref, lbl_ref, hist_ref,
                          ypool_ref, npool_ref, ft_ref, qt_ref, v_ref, gate_ref):
    xt = x_ref[...]
    sh1, sc1 = _mods(mod_ref[...])[:2]
    h = _rms(xt, gpre_ref[...]) * (1.0 + sc1) + sh1
    proj = _dot(h.astype(BF16), win_ref[...])
    u = proj[:, :POOL_WIDTH]
    row = lax.broadcasted_iota(jnp.int32, (hist_ref.shape[0], POOL_HIST, POOL_GROUP), 1)
    ys = []
    for g, w in enumerate(POOL_WINDOWS):
        sl = slice(g * POOL_GROUP, (g + 1) * POOL_GROUP)
        past = jnp.sum(jnp.where(row >= POOL_HIST - (w - 1), hist_ref[:, :, sl], 0.0), axis=1)
        ug = u[:, sl]
        d = (past + ug) / float(w) - ug
        ys.append(_dot(d.astype(BF16), wpool_ref[g]))
    ypool_ref[...] = jnp.concatenate(ys, axis=1) * pscale_ref[...]
    npool_ref[:, 0:POOL_HIST - 1, :] = hist_ref[:, 1:POOL_HIST, :]
    npool_ref[:, POOL_HIST - 1, :] = u

    lb = _forget_lower_bound(lbl_ref[...])
    f = lb + (1.0 - lb) * jax.nn.sigmoid(proj[:, POOL_WIDTH + HG_WIDTH:POOL_WIDTH + 2 * HG_WIDTH])
    ft_ref[...] = f.T
    qt_ref[...] = _silu(proj[:, POOL_WIDTH:POOL_WIDTH + HG_WIDTH]).T
    v_ref[...] = proj[:, POOL_WIDTH + 2 * HG_WIDTH:POOL_WIDTH + 3 * HG_WIDTH]
    gate_ref[...] = _silu(proj[:, POOL_WIDTH + 3 * HG_WIDTH:])


def _mix_sample_state_kernel(s_ref, ft_ref, qt_ref, v_ref, snew_ref, o_ref):
    i = pl.program_id(0)
    lane = lax.broadcasted_iota(jnp.int32, (HG_HEAD_DIM, ft_ref.shape[1]), 1)
    for j in range(SAMPLE_STATE_BLOCK):
        mine = lane == i * SAMPLE_STATE_BLOCK + j
        for hd in range(HG_HEADS):
            r0 = hd * HG_HEAD_DIM
            f = jnp.sum(jnp.where(mine, ft_ref[r0:r0 + HG_HEAD_DIM, :], 0.0), axis=1, keepdims=True)
            q = jnp.sum(jnp.where(mine, qt_ref[r0:r0 + HG_HEAD_DIM, :], 0.0), axis=1, keepdims=True)
            v = v_ref[j:j + 1, r0:r0 + HG_HEAD_DIM]
            s_new = f * s_ref[j, hd] + (1.0 - f) * v
            snew_ref[j, hd] = s_new
            o_ref[j:j + 1, r0:r0 + HG_HEAD_DIM] = jnp.sum(q * s_new, axis=0, keepdims=True)


def _mix_sample_out_kernel(x_ref, mod_ref, o_ref, gate_ref, ypool_ref, gout_ref, wout_ref, gpost_ref,
                           gffn_ref, wrh_ref, wrl_ref, x1_ref, h2_ref, lg_ref):
    _, _, gt1, sh2, sc2, _ = _mods(mod_ref[...])
    o = o_ref[...]
    os_ = []
    for hd in range(HG_HEADS):
        oh = o[:, hd * HG_HEAD_DIM:(hd + 1) * HG_HEAD_DIM]
        os_.append(oh * lax.rsqrt(jnp.mean(oh * oh, axis=-1, keepdims=True) + EPS) * gout_ref[...])
    o_n = jnp.concatenate(os_, axis=1) * gate_ref[...]
    mix = (_dot(ypool_ref[...].astype(BF16), wout_ref[0:POOL_WIDTH, :])
           + _dot(o_n.astype(BF16), wout_ref[POOL_WIDTH:, :]))
    x1 = x_ref[...] + gt1 * _rms(mix, gpost_ref[...])
    h2 = _rms(x1, gffn_ref[...]) * (1.0 + sc2) + sh2
    x1_ref[...] = x1
    h2_ref[...] = h2.astype(BF16)
    lg_ref[...] = _router_logits(h2, wrh_ref[...], wrl_ref[...])


def _mix_sample(x, mod, hist, state, gpre, gpost, gffn, win, wpool, pscale, lbl, gout, wout, wrh, wrl):
    b = x.shape[0]
    bb = SAMPLE_STATE_BLOCK
    cp = pltpu.CompilerParams(vmem_limit_bytes=VMEM_LIMIT)
    ypool, npool, ft, qt, v, gate = pl.pallas_call(
        _mix_sample_in_kernel,
        out_shape=(jax.ShapeDtypeStruct((b, POOL_WIDTH), F32),
                   jax.ShapeDtypeStruct((b, POOL_HIST, POOL_WIDTH), F32),
                   jax.ShapeDtypeStruct((HG_WIDTH, b), F32),
                   jax.ShapeDtypeStruct((HG_WIDTH, b), F32),
                   jax.ShapeDtypeStruct((b, HG_WIDTH), F32),
                   jax.ShapeDtypeStruct((b, HG_WIDTH), F32)),
        compiler_params=cp,
        name="mix_sample_in",
    )(x, mod, gpre, win, wpool, pscale, lbl, hist)

    s_spec = pl.BlockSpec((bb, HG_HEADS, HG_HEAD_DIM, HG_HEAD_DIM), lambda i: (i, 0, 0, 0))
    col_spec = _full((HG_WIDTH, b))
    row_spec = pl.BlockSpec((bb, HG_WIDTH), lambda i: (i, 0))
    s_new, o = pl.pallas_call(
        _mix_sample_state_kernel,
        out_shape=(jax.ShapeDtypeStruct(state.shape, F32), jax.ShapeDtypeStruct((b, HG_WIDTH), F32)),
        grid=(b // bb,),
        in_specs=[s_spec, col_spec, col_spec, row_spec],
        out_specs=(s_spec, row_spec),
        compiler_params=pltpu.CompilerParams(dimension_semantics=("parallel",), vmem_limit_bytes=VMEM_LIMIT),
        name="mix_sample_state",
    )(state, ft, qt, v)

    x1, h2, lg = pl.pallas_call(
        _mix_sample_out_kernel,
        out_shape=(jax.ShapeDtypeStruct((b, D_MODEL), F32),
                   jax.ShapeDtypeStruct((b, D_MODEL), BF16),
                   jax.ShapeDtypeStruct((b, LANES), F32)),
        compiler_params=cp,
        name="mix_sample_out",
    )(x, mod, o, gate, ypool, gout, wout, gpost, gffn, wrh, wrl)
    return x1, h2, lg, npool, s_new


def _first_max(cur, idx, big):
    m = jnp.max(cur, axis=0, keepdims=True)
    first = jnp.min(jnp.where(cur == m, idx, big), axis=0, keepdims=True)
    return idx == first


def _route(logit, bias):
    n = logit.shape[1]
    scores = jax.nn.sigmoid(logit)
    sel = scores + bias
    neg = -jnp.inf

    sub = lax.broadcasted_iota(jnp.int32, (GROUP_SIZE, n), 0)
    gscore = []
    for g in range(N_GROUPS):
        sg = sel[g * GROUP_SIZE:(g + 1) * GROUP_SIZE, :]
        m1 = jnp.max(sg, axis=0, keepdims=True)
        rest = jnp.where(_first_max(sg, sub, GROUP_SIZE), neg, sg)
        gscore.append(m1 + jnp.max(rest, axis=0, keepdims=True))
    cur = jnp.concatenate(gscore, axis=0)
    gidx = lax.broadcasted_iota(jnp.int32, (N_GROUPS, n), 0)
    gmask = jnp.zeros((N_GROUPS, n), jnp.bool_)
    for _ in range(TOPK_GROUPS):
        pick = _first_max(cur, gidx, N_GROUPS)
        gmask = gmask | pick
        cur = jnp.where(pick, neg, cur)
    emask = jnp.concatenate(
        [jnp.broadcast_to(gmask[g:g + 1, :], (GROUP_SIZE, n)) for g in range(N_GROUPS)], axis=0)

    cur = jnp.where(emask, sel, neg)
    eidx = lax.broadcasted_iota(jnp.int32, (N_EXPERTS, n), 0)
    picks = []
    for _ in range(TOP_K):
        pick = _first_max(cur, eidx, N_EXPERTS)
        picks.append(pick)
        cur = jnp.where(pick, neg, cur)
    return scores, picks


def _any(masks):
    return functools.reduce(jnp.logical_or, masks)


def _router_kernel(lg_ref, bias_ref, wd_ref):
    n = lg_ref.shape[0]
    scores, picks = _route(lg_ref[...].T[0:N_EXPERTS, :], bias_ref[0:N_EXPERTS, :])
    chosen = _any(picks)
    wsum = jnp.sum(jnp.where(chosen, scores, 0.0), axis=0, keepdims=True)
    wd = jnp.where(chosen, scores / wsum * ROUTED_SCALE, 0.0)
    wd_ref[...] = jnp.concatenate([wd, jnp.zeros((LANES - N_EXPERTS, n), F32)], axis=0).T


def _router(logits, bias_col):
    t = logits.shape[0]
    tile = min(ROUTER_TILE, t)
    return pl.pallas_call(
        _router_kernel,
        out_shape=jax.ShapeDtypeStruct((t, LANES), F32),
        grid=(t // tile,),
        in_specs=[pl.BlockSpec((tile, LANES), lambda i: (i, 0)), _full(bias_col.shape)],
        out_specs=pl.BlockSpec((tile, LANES), lambda i: (i, 0)),
        compiler_params=pltpu.CompilerParams(dimension_semantics=("parallel",), vmem_limit_bytes=VMEM_LIMIT),
        name="router",
    )(logits, bias_col)


def _plan_kernel(lg_ref, bias_ref, upper_ref, dest_ref, wsel_ref, cnt_ref, loff_ref):
    scores, picks = _route(lg_ref[...].T[0:N_EXPERTS, :], bias_ref[0:N_EXPERTS, :])
    chosen = _any(picks)
    chosen_f = jnp.where(chosen, 1.0, 0.0)
    cnt = jnp.sum(chosen_f, axis=1, keepdims=True)
    units = jnp.floor((cnt + (RUN_ALIGN - 1)) / RUN_ALIGN)
    ei = lax.broadcasted_iota(jnp.int32, (N_EXPERTS, N_EXPERTS), 0)
    ej = lax.broadcasted_iota(jnp.int32, (N_EXPERTS, N_EXPERTS), 1)
    before = jnp.where(ej < ei, 1.0, 0.0).astype(BF16)
    loff = RUN_ALIGN * _dot(before, jnp.broadcast_to(units, (N_EXPERTS, LANES)).astype(BF16))[:, 0:1]
    slot = loff + _dot(chosen_f.astype(BF16), upper_ref[...])
    wsum = jnp.sum(jnp.where(chosen, scores, 0.0), axis=0, keepdims=True)
    pick_sum = lambda v: jnp.concatenate(
        [jnp.sum(jnp.where(p, v, 0.0), axis=0, keepdims=True) for p in picks], axis=0)
    dest_ref[0] = (pick_sum(slot) * ROW_SUB).astype(jnp.int32)
    wsel_ref[0] = pick_sum(scores) / wsum * ROUTED_SCALE
    cnt_ref[0] = jnp.broadcast_to(cnt, (N_EXPERTS, LANES))
    loff_ref[0] = jnp.broadcast_to(loff, (N_EXPERTS, LANES))


def _plan(logits, bias_col, upper):
    t = logits.shape[0]
    nw = t // PLAN_WINDOW
    per_w = pl.BlockSpec((1, N_EXPERTS, LANES), lambda w: (w, 0, 0))
    picks = pl.BlockSpec((1, TOP_K, PLAN_WINDOW), lambda w: (w, 0, 0))
    return pl.pallas_call(
        _plan_kernel,
        out_shape=(jax.ShapeDtypeStruct((nw, TOP_K, PLAN_WINDOW), jnp.int32),
                   jax.ShapeDtypeStruct((nw, TOP_K, PLAN_WINDOW), F32),
                   jax.ShapeDtypeStruct((nw, N_EXPERTS, LANES), F32),
                   jax.ShapeDtypeStruct((nw, N_EXPERTS, LANES), F32)),
        grid=(nw,),
        in_specs=[pl.BlockSpec((PLAN_WINDOW, LANES), lambda w: (w, 0)), _full(bias_col.shape), _full(upper.shape)],
        out_specs=(picks, picks, per_w, per_w),
        compiler_params=pltpu.CompilerParams(dimension_semantics=("parallel",), vmem_limit_bytes=VMEM_LIMIT),
        name="plan",
    )(logits, bias_col, upper)


def _slot_rows(slot):
    return pl.ds(pl.multiple_of(slot * ROW_SUB, ROW_SUB), ROW_SUB)


def _table_copies(table_ref, flat_s, sem):
    return [pltpu.make_async_copy(table_ref.at[0, k], flat_s.at[pl.ds(k * PLAN_WINDOW, PLAN_WINDOW)], sem)
            for k in range(TOP_K)]


def _rows_at(offset):
    return pl.ds(pl.multiple_of(offset, ROW_SUB), ROW_SUB)


def _dispatch_kernel(dest_ref, h_ref, x_ref, dest_s, rows_s, sem):
    plan_copies = _table_copies(dest_ref, dest_s, sem)
    for c in plan_copies:
        c.start()
    x_ref[...] = jnp.zeros_like(x_ref)
    for j in range(ROW_SUB):
        rows_s[pl.ds(j, PLAN_WINDOW, stride=ROW_SUB), :] = h_ref[:, j * LANES:(j + 1) * LANES]
    for c in plan_copies:
        c.wait()

    def body(i, carry):
        for u in range(SCATTER_UNROLL):
            t = i * SCATTER_UNROLL + u
            row = rows_s[_slot_rows(t), :]
            for k in range(TOP_K):
                x_ref[_rows_at(dest_s[k * PLAN_WINDOW + t]), :] = row
        return carry

    lax.fori_loop(0, PLAN_WINDOW // SCATTER_UNROLL, body, 0)


def _dispatch(dest_w, h2):
    nw = dest_w.shape[0]
    region = REGION_ROWS * ROW_SUB
    return pl.pallas_call(
        _dispatch_kernel,
        out_shape=jax.ShapeDtypeStruct((nw * region, LANES), jnp.uint32),
        grid=(nw,),
        in_specs=[pl.BlockSpec((1, TOP_K, PLAN_WINDOW), lambda w: (w, 0, 0)),
                  pl.BlockSpec((PLAN_WINDOW, ROW_WORDS), lambda w: (w, 0))],
        out_specs=pl.BlockSpec((region, LANES), lambda w: (w, 0)),
        scratch_shapes=[pltpu.SMEM((TOP_K * PLAN_WINDOW,), jnp.int32),
                        pltpu.VMEM((PLAN_WINDOW * ROW_SUB, LANES), jnp.uint32),
                        pltpu.SemaphoreType.DMA],
        compiler_params=pltpu.CompilerParams(dimension_semantics=("arbitrary",), vmem_limit_bytes=VMEM_LIMIT),
        name="dispatch",
    )(dest_w, h2)


def _unpack_rows(ref, n):
    parts = [_unpack_pair(ref[pl.ds(j, n, stride=ROW_SUB), :]) for j in range(ROW_SUB)]
    return jnp.concatenate([p[0] for p in parts] + [p[1] for p in parts], axis=1)


def _ffn_kernel(gq_ref, gs_ref, to_ref, tn_ref, x_hbm, w1_ref, w3_ref, w2_ref,
                hs_ref, wds_ref, x1s_ref, gt2s_ref, gpost_ref, ws1_ref, ws3_ref, ws2_ref, y_hbm, ys_ref,
                w13_s, w2_s, xbuf, ybuf, rsem, wsem, acc_s):
    del x_hbm
    e = pl.program_id(0)
    n_groups = gq_ref[N_EXPERTS]
    block = RUN_ALIGN * ROW_SUB
    tile = FFN_TILE * ROW_SUB

    def go(cp, start):
        if start:
            cp.start()
        else:
            cp.wait()

    def fetch(q, start):
        def one(j):
            src = y_hbm.at[pl.ds(pl.multiple_of(to_ref[gs_ref[q] + j] * block, block), tile)]
            go(pltpu.make_async_copy(src, xbuf.at[q % FFN_RING, pl.ds(j * tile, tile)], rsem.at[q % FFN_RING]), start)

        size = gs_ref[q + 1] - gs_ref[q]

        @pl.when(size == FFN_GROUP)
        def _():
            for j in range(FFN_GROUP):
                one(j)

        @pl.when(size < FFN_GROUP)
        def _():
            for j in range(FFN_GROUP - 1):
                pl.when(j < size)(functools.partial(one, j))

    def writeback(q, start):
        most = FFN_TILE // RUN_ALIGN - 1

        def blocks(j, b, n):
            dst = pl.multiple_of(to_ref[gs_ref[q] + j] * block, block) + b * block
            go(pltpu.make_async_copy(ybuf.at[q % FFN_RING, pl.ds(j * tile + b * block, n * block)],
                                     y_hbm.at[pl.ds(dst, n * block)], wsem.at[q % FFN_RING]), start)

        for j in range(FFN_GROUP):
            item = gs_ref[q] + j
            n = jnp.where(item < gs_ref[q + 1], tn_ref[item], 0)
            pl.when(n >= most)(functools.partial(blocks, j, 0, most))
            pl.when(n > most)(functools.partial(blocks, j, most, 1))

            @pl.when((n > 0) & (n < most))
            def _():
                for b in range(most - 1):
                    pl.when(b < n)(functools.partial(blocks, j, b, 1))

    @pl.when(e == 0)
    def _():
        xbuf[...] = jnp.zeros_like(xbuf)
        for d in range(FFN_RING - 1):
            pl.when(d < n_groups)(functools.partial(fetch, d, True))

    w13_s[:, 0:D_EXPERT] = w1_ref[0].astype(BF16)
    w13_s[:, D_EXPERT:] = w3_ref[0].astype(BF16)
    w2_s[...] = w2_ref[0].astype(BF16)

    xs = hs_ref[...]

    @pl.when(e == 0)
    def _():
        sh = _silu(_dot(xs, ws1_ref[...].astype(BF16))) * _dot(xs, ws3_ref[...].astype(BF16))
        acc_s[...] = _dot(sh.astype(BF16), ws2_ref[...].astype(BF16))

    lane = lax.broadcasted_iota(jnp.int32, wds_ref.shape, 1)
    wcol = jnp.sum(jnp.where(lane == e, wds_ref[...], 0.0), axis=1, keepdims=True)
    rows = FFN_GROUP * FFN_TILE

    def ffn(x, ride):
        n = 0 if x is None else x.shape[0]
        if ride:
            x = xs if x is None else jnp.concatenate([x, xs], axis=0)
        h = _dot(x, w13_s[...])
        y = _dot((_silu(h[:, :D_EXPERT]) * h[:, D_EXPERT:]).astype(BF16), w2_s[...])
        if ride:
            acc_s[...] += y[n:] * wcol
        return y[:n] if n else None

    def group(q, ride):
        @pl.when(q + FFN_RING - 1 < n_groups)
        def _():
            fetch(q + FFN_RING - 1, True)

        @pl.when(q >= FFN_RING)
        def _():
            writeback(q - FFN_RING, False)

        fetch(q, False)
        y = ffn(_unpack_rows(xbuf.at[q % FFN_RING], rows).astype(BF16), ride)
        out = ybuf.at[q % FFN_RING]
        for j in range(ROW_SUB):
            out[pl.ds(j, rows, stride=ROW_SUB), :] = _pack_pair(
                y[:, j * LANES:(j + 1) * LANES], y[:, ROW_WORDS + j * LANES:ROW_WORDS + (j + 1) * LANES])
        writeback(q, True)

    first, end = gq_ref[e], gq_ref[e + 1]
    pl.when(first < end)(functools.partial(group, first, True))
    pl.when(first == end)(functools.partial(ffn, None, True))

    def group_body(q, carry):
        group(q, False)
        return carry

    lax.fori_loop(first + 1, end, group_body, 0)

    @pl.when(e == pl.num_programs(0) - 1)
    def _():
        ys_ref[...] = x1s_ref[...] + gt2s_ref[...] * _rms(acc_s[...], gpost_ref[...])

    @pl.when(e == pl.num_programs(0) - 1)
    def _():
        for d in range(FFN_RING):
            @pl.when(n_groups - 1 - d >= 0)
            def _():
                writeback(n_groups - 1 - d, False)


def _expert_ffn(group_of_expert, group_start, item_off, item_blocks, x_slots, w1, w3, w2,
                h2_s, wd_s, x1_s, gt2_s, gpost, ws1, ws3, ws2):
    of_expert = lambda e, *_: (e, 0, 0)
    whole = lambda a: pl.BlockSpec(a.shape, lambda e, *_: (0,) * a.ndim)
    tiles = pltpu.VMEM((FFN_RING, FFN_GROUP * FFN_TILE * ROW_SUB, LANES), jnp.uint32)
    sems = pltpu.SemaphoreType.DMA((FFN_RING,))
    dense = (h2_s, wd_s, x1_s, gt2_s, gpost, ws1, ws3, ws2)
    grid_spec = pltpu.PrefetchScalarGridSpec(
        num_scalar_prefetch=4,
        grid=(N_EXPERTS,),
        in_specs=[pl.BlockSpec(memory_space=pl.ANY),
                  pl.BlockSpec((1, D_MODEL, D_EXPERT), of_expert),
                  pl.BlockSpec((1, D_MODEL, D_EXPERT), of_expert),
                  pl.BlockSpec((1, D_EXPERT, D_MODEL), of_expert)] + [whole(a) for a in dense],
        out_specs=(pl.BlockSpec(memory_space=pl.ANY), whole(x1_s)),
        scratch_shapes=[pltpu.VMEM((D_MODEL, 2 * D_EXPERT), BF16), pltpu.VMEM((D_EXPERT, D_MODEL), BF16),
                        tiles, tiles, sems, sems, pltpu.VMEM(x1_s.shape, F32)])
    return pl.pallas_call(
        _ffn_kernel,
        out_shape=(jax.ShapeDtypeStruct(x_slots.shape, jnp.uint32), jax.ShapeDtypeStruct(x1_s.shape, F32)),
        grid_spec=grid_spec,
        input_output_aliases={4: 0},
        compiler_params=pltpu.CompilerParams(dimension_semantics=("arbitrary",), vmem_limit_bytes=VMEM_LIMIT),
        name="expert_ffn",
    )(group_of_expert, group_start, item_off, item_blocks, x_slots, w1, w3, w2, *dense)


def _work_items(cnt, loff):
    nw = cnt.shape[0]
    cnt = cnt.astype(jnp.int32).T.reshape(-1)
    loff = loff.astype(jnp.int32).T.reshape(-1)
    region = jnp.tile(jnp.arange(nw, dtype=jnp.int32) * REGION_ROWS, N_EXPERTS)
    n_items = nw * N_EXPERTS + (nw * PLAN_WINDOW * TOP_K) // FFN_TILE
    tiles = (cnt + FFN_TILE - 1) // FFN_TILE
    ends = jnp.cumsum(tiles)
    starts = ends - tiles
    i = jnp.arange(n_items + FFN_GROUP, dtype=jnp.int32)[:, None]
    mine = (starts[None, :] <= i) & (i < ends[None, :])
    of_pair = lambda v: jnp.sum(jnp.where(mine, v[None, :], 0), axis=1)
    r = i[:, 0] - of_pair(starts)
    row = of_pair(region + loff) + r * FFN_TILE
    own = jnp.clip(of_pair(cnt) - r * FFN_TILE, 0, FFN_TILE)

    zero = jnp.zeros((1,), jnp.int32)
    item_end = ends.reshape(N_EXPERTS, nw)[:, -1]
    item_start = jnp.concatenate([zero, item_end[:-1]])
    group_end = jnp.cumsum((item_end - item_start + FFN_GROUP - 1) // FFN_GROUP)
    group_first = jnp.concatenate([zero, group_end[:-1]])
    q = jnp.arange(N_EXPERTS + n_items // FFN_GROUP + 1, dtype=jnp.int32)[:, None]
    has = (group_first[None, :] <= q) & (q < group_end[None, :])
    of_expert = lambda v: jnp.sum(jnp.where(has, v[None, :], 0), axis=1)
    group_start = jnp.where(q[:, 0] < group_end[-1],
                            of_expert(item_start) + (q[:, 0] - of_expert(group_first)) * FFN_GROUP, item_end[-1])
    return (jnp.concatenate([zero, group_end]), group_start, row // RUN_ALIGN, (own + RUN_ALIGN - 1) // RUN_ALIGN)


def _combine_kernel(dest_ref, wsel_ref, y_hbm, h_ref, x1_ref, gt2_ref, gpost_ref, ws13_ref, ws2_ref, out_ref,
                    dest_s, wsel_s, lo_s, hi_s, ybuf, sem, ysem):
    w, s = pl.program_id(0), pl.program_id(1)
    region = REGION_ROWS * ROW_SUB
    part = region // REGION_PARTS

    def region_copies(win):
        return [pltpu.make_async_copy(y_hbm.at[pl.ds(win * region + p * part, part)],
                                      ybuf.at[win % 2, pl.ds(p * part, part)], ysem.at[win % 2, p])
                for p in range(REGION_PARTS)]

    @pl.when(s == 0)
    def _():
        copies = _table_copies(dest_ref, dest_s, sem.at[0]) + _table_copies(wsel_ref, wsel_s, sem.at[1])
        for c in copies:
            c.start()

        @pl.when(w == 0)
        def _():
            for c in region_copies(w):
                c.start()

        @pl.when(w + 1 < pl.num_programs(0))
        def _():
            for c in region_copies(w + 1):
                c.start()

        for c in copies + region_copies(w):
            c.wait()

    y_ref = ybuf.at[w % 2]

    base = s * COMBINE_TILE

    def body(i, carry):
        for u in range(COMBINE_UNROLL):
            t = i * COMBINE_UNROLL + u
            acc_lo = jnp.zeros((ROW_SUB, LANES), F32)
            acc_hi = jnp.zeros((ROW_SUB, LANES), F32)
            for k in range(TOP_K):
                at = k * PLAN_WINDOW + base + t
                lo, hi = _unpack_pair(y_ref[_rows_at(dest_s[at]), :])
                w = wsel_s[at]
                acc_lo = acc_lo + w * lo
                acc_hi = acc_hi + w * hi
            lo_s[_slot_rows(t), :] = acc_lo
            hi_s[_slot_rows(t), :] = acc_hi
        return carry

    lax.fori_loop(0, COMBINE_TILE // COMBINE_UNROLL, body, 0)
    rows = lambda ref: [ref[pl.ds(j, COMBINE_TILE, stride=ROW_SUB), :] for j in range(ROW_SUB)]
    routed = jnp.concatenate(rows(lo_s) + rows(hi_s), axis=1)
    h_lo, h_hi = _unpack_pair(h_ref[...])
    x = jnp.concatenate([h_lo, h_hi], axis=1).astype(BF16)
    hs = _dot(x, ws13_ref[...])
    act = _silu(hs[:, :D_EXPERT]) * hs[:, D_EXPERT:]
    ff = routed + _dot(act.astype(BF16), ws2_ref[...])
    out_ref[...] = x1_ref[...] + gt2_ref[0] * _rms(ff, gpost_ref[...])


def _combine(dest_w, wsel_w, y_slots, h2, x1, gt2, tokens_per_gt2, gpost, ws13, ws2):
    nw = dest_w.shape[0]
    sub = PLAN_WINDOW // COMBINE_TILE
    plan = pl.BlockSpec((1, TOP_K, PLAN_WINDOW), lambda w, s: (w, 0, 0))
    tok = lambda w, s: (w * sub + s, 0)
    return pl.pallas_call(
        _combine_kernel,
        out_shape=jax.ShapeDtypeStruct(x1.shape, F32),
        grid=(nw, sub),
        in_specs=[plan, plan,
                  pl.BlockSpec(memory_space=pl.ANY),
                  pl.BlockSpec((COMBINE_TILE, ROW_WORDS), tok),
                  pl.BlockSpec((COMBINE_TILE, D_MODEL), tok),
                  pl.BlockSpec((1, 1, D_MODEL), lambda w, s: ((w * PLAN_WINDOW) // tokens_per_gt2, 0, 0)),
                  _full((1, D_MODEL)), _full(ws13.shape), _full(ws2.shape)],
        out_specs=pl.BlockSpec((COMBINE_TILE, D_MODEL), tok),
        scratch_shapes=[pltpu.SMEM((TOP_K * PLAN_WINDOW,), jnp.int32),
                        pltpu.SMEM((TOP_K * PLAN_WINDOW,), F32),
                        pltpu.VMEM((COMBINE_TILE * ROW_SUB, LANES), F32),
                        pltpu.VMEM((COMBINE_TILE * ROW_SUB, LANES), F32),
                        pltpu.VMEM((2, REGION_ROWS * ROW_SUB, LANES), jnp.uint32),
                        pltpu.SemaphoreType.DMA((2,)), pltpu.SemaphoreType.DMA((2, REGION_PARTS))],
        compiler_params=pltpu.CompilerParams(dimension_semantics=("arbitrary", "arbitrary"),
                                             vmem_limit_bytes=VMEM_LIMIT),
        name="combine",
    )(dest_w, wsel_w, y_slots, h2, x1, gt2, gpost, ws13, ws2)


def kernel(x_prompt, x_sample, c_prompt, c_sample, state_pool, state_hgrn, w_ada, b_ada, g_pre_mix, g_post_mix,
           w_in, w_pool, pool_scale, lb_logits, g_out_norm, w_out, g_pre_ffn, g_post_ffn, w_router, router_bias,
           w_exp_gate, w_exp_up, w_exp_down, w_sh_gate, w_sh_up, w_sh_down):
    assert w_ada.shape[0] == 1 and lb_logits.shape[0] == 2, "single-layer trunk"
    bp, lp, d = x_prompt.shape
    bs = x_sample.shape[0]
    row = lambda a: a[0].reshape(1, -1)

    n_mod = bp + bs
    pad = (-n_mod) % 16
    c_all = jnp.concatenate([c_prompt, c_sample, jnp.zeros((pad, d), F32)], axis=0)
    mod = _ada(c_all, w_ada[0], b_ada)
    mod_p = mod[:bp].reshape(bp, 1, 6 * d)
    mod_s = mod[bp:n_mod]

    win = w_in[0].astype(BF16)
    wout = w_out[0].astype(BF16)
    wpool = w_pool[0].astype(BF16)
    wr = jnp.pad(w_router[0], ((0, 0), (0, LANES - N_EXPERTS)))
    wrh = wr.astype(BF16)
    wrl = (wr - wrh.astype(F32)).astype(BF16)
    m2 = jnp.asarray(_M2_NP, BF16)
    gpre, gpost, gffn, gffn_post = row(g_pre_mix), row(g_post_mix), row(g_pre_ffn), row(g_post_ffn)
    pscale, gout = row(pool_scale), row(g_out_norm)
    bias_col = jnp.pad(router_bias[0], (0, LANES - N_EXPERTS)).reshape(LANES, 1)

    x1_p, h2_p, lg_p, pool_p, st_p = _mix_prompt(x_prompt, mod_p, gpre, gpost, gffn, win, wpool, pscale,
                                                  lb_logits, gout, wout, wrh, wrl, m2)
    x1_s, h2_s, lg_s, pool_s, st_s = _mix_sample(x_sample[:, 0, :], mod_s, state_pool[0], state_hgrn[0],
                                                  gpre, gpost, gffn, win, wpool, pscale, lb_logits, gout, wout,
                                                  wrh, wrl)

    experts = (w_exp_gate[0], w_exp_up[0], w_exp_down[0], w_sh_gate[0], w_sh_up[0], w_sh_down[0])
    tp = bp * lp
    nw = tp // PLAN_WINDOW
    assert lp % PLAN_WINDOW == 0
    upper = jnp.asarray(np.triu(np.ones((PLAN_WINDOW, PLAN_WINDOW), np.float32), 1), BF16)
    dest, wsel, cnt, loff = _plan(lg_p.reshape(tp, LANES), bias_col, upper)
    dest_w, wsel_w = dest, wsel
    h2_rows = h2_p.reshape(tp, ROW_WORDS)
    x_slots = _dispatch(dest_w, h2_rows)
    wd_s = _router(lg_s, bias_col)
    y_slots, y_s = _expert_ffn(*_work_items(cnt[:, :, 0], loff[:, :, 0]), x_slots, *experts[:3],
                               h2_s, wd_s, x1_s, mod_s[:, 5 * d:], gffn_post, *experts[3:])
    ws13 = jnp.concatenate([experts[3], experts[4]], axis=1).astype(BF16)
    gt2_p = mod_p[:, :, 5 * d:]
    y_p = _combine(dest_w, wsel_w, y_slots, h2_rows, x1_p.reshape(tp, d), gt2_p, lp, gffn_post,
                   ws13, experts[5].astype(BF16))

    return (y_p.reshape(bp, lp, d), y_s.reshape(bs, 1, d), pool_p[None], st_p[None], pool_s[None], st_s[None])
```

```python
import functools
import itertools

import numpy as np
import jax
import jax.numpy as jnp
from jax import lax
from jax.experimental import pallas as pl
from jax.experimental.pallas import tpu as pltpu

F32 = jnp.float32
BF16 = jnp.bfloat16

D_MODEL = 1024
POOL_WIDTH = 512
POOL_WINDOWS = (2, 4, 8, 16)
POOL_GROUP = 128
POOL_HIST = 15
HG_WIDTH = 512
HG_HEADS = 4
HG_HEAD_DIM = 128
IN_WIDTH = POOL_WIDTH + 4 * HG_WIDTH
N_EXPERTS = 64
TOP_K = 8
N_GROUPS = 8
TOPK_GROUPS = 4
GROUP_SIZE = N_EXPERTS // N_GROUPS
D_EXPERT = 256
ROUTED_SCALE = 2.5
EPS = 1e-6

LANES = 128
CHUNK = 64
TIME_TILE = 512
ADA_TILE_N = 1024
ROUTER_TILE = 512
SAMPLE_STATE_BLOCK = 16
PLAN_WINDOW = 1024
RUN_ALIGN = 32
FFN_TILE = 160
REGION_ROWS = PLAN_WINDOW * TOP_K + N_EXPERTS * (RUN_ALIGN - 1) + FFN_TILE
ROW_WORDS = D_MODEL // 2
ROW_SUB = ROW_WORDS // LANES
COMBINE_TILE = 256
REGION_PARTS = 4
SCATTER_UNROLL = 4
COMBINE_UNROLL = 16
FFN_GROUP = 4
FFN_RING = 6
VMEM_LIMIT = 58 * 1024 * 1024

_LEVELS = (64, 32, 16, 8, 4, 2)
_BLK_CUM = 0


def _level_blocks():
    c = CHUNK
    i = np.arange(c)[:, None]
    s = np.arange(c)[None, :]
    blocks = [(s <= i)]
    index = {}
    for lvl in _LEVELS:
        ref = (i // lvl) * lvl + lvl // 2 - 1
        index[lvl] = len(blocks)
        blocks.append(s <= ref)
    m = np.concatenate(blocks, axis=0).astype(np.float32)
    return np.concatenate([m, m], axis=1), index


_M2_NP, _LEVEL_INDEX = _level_blocks()


def _dot(a, b):
    return jnp.dot(a, b, preferred_element_type=F32)


def _dot_nt(a, b):
    return lax.dot_general(a, b, (((1,), (1,)), ((), ())), preferred_element_type=F32)


def _split2(x):
    hi = x.astype(BF16)
    lo = (x - hi.astype(F32)).astype(BF16)
    return hi, lo


def _pack_pair(a, b):
    lo = lax.bitcast_convert_type(a.astype(BF16).astype(F32), jnp.uint32)
    hi = lax.bitcast_convert_type(b.astype(BF16).astype(F32), jnp.uint32)
    return (lo >> 16) | hi


def _unpack_pair(words):
    lo = lax.bitcast_convert_type(words << 16, F32)
    hi = lax.bitcast_convert_type(words & jnp.uint32(0xFFFF0000), F32)
    return lo, hi


def _silu(x):
    return x * jax.nn.sigmoid(x)


def _rms(x, g):
    return x * lax.rsqrt(jnp.mean(x * x, axis=-1, keepdims=True) + EPS) * g


def _mods(m):
    return [m[:, j * D_MODEL:(j + 1) * D_MODEL] for j in range(6)]


def _forget_lower_bound(lbl):
    mx = jnp.max(lbl, axis=0, keepdims=True)
    e = jnp.exp(lbl - mx)
    return e[0:1] / jnp.sum(e, axis=0, keepdims=True)


def _router_logits(h2, wrh, wrl):
    hi, lo = _split2(h2)
    return _dot(hi, wrh) + _dot(lo, wrh) + _dot(hi, wrl)


def _ada_kernel(c_ref, w_ref, b_ref, o_ref):
    a_hi, a_lo = _split2(_silu(c_ref[...]))
    w_hi, w_lo = _split2(w_ref[...])
    o_ref[...] = _dot(a_hi, w_hi) + _dot(a_lo, w_hi) + _dot(a_hi, w_lo) + b_ref[...]


def _ada(c_all, w_ada, b_ada):
    rows = c_all.shape[0]
    n = w_ada.shape[1]
    return pl.pallas_call(
        _ada_kernel,
        out_shape=jax.ShapeDtypeStruct((rows, n), F32),
        grid=(n // ADA_TILE_N,),
        in_specs=[pl.BlockSpec((rows, D_MODEL), lambda j: (0, 0)),
                  pl.BlockSpec((D_MODEL, ADA_TILE_N), lambda j: (0, j)),
                  pl.BlockSpec((1, ADA_TILE_N), lambda j: (0, j))],
        out_specs=pl.BlockSpec((rows, ADA_TILE_N), lambda j: (0, j)),
        compiler_params=pltpu.CompilerParams(dimension_semantics=("parallel",),
                                             vmem_limit_bytes=VMEM_LIMIT),
        name="ada",
    )(c_all, w_ada, b_ada)


def _level_masks():
    i = lax.broadcasted_iota(jnp.int32, (CHUNK, 2 * CHUNK), 0)
    j = lax.broadcasted_iota(jnp.int32, (CHUNK, 2 * CHUNK), 1) & (CHUNK - 1)
    masks = {}
    for lvl in _LEVELS:
        sh = lvl.bit_length() - 1
        same = (i >> sh) == (j >> sh)
        upper = ((i >> (sh - 1)) & 1) == 1
        lower = ((j >> (sh - 1)) & 1) == 0
        masks[lvl] = same & upper & lower
    return masks, i == j


def _by_head(m):
    z = jnp.zeros((m.shape[0], HG_HEAD_DIM), m.dtype)
    return jnp.concatenate([jnp.concatenate([m[:, :HG_HEAD_DIM], z], axis=1),
                            jnp.concatenate([z, m[:, HG_HEAD_DIM:]], axis=1)], axis=0)


def _hgrn_chunk_pair(q, k, v, e_all, lane0, sts, masks, eye):
    hd = HG_HEAD_DIM

    def blk(n):
        return e_all[n * CHUNK:(n + 1) * CHUNK, lane0:lane0 + 2 * hd]

    b = blk(_BLK_CUM)
    kb = k.astype(BF16)
    a = jnp.where(eye, _dot_nt(q.astype(BF16), _by_head(kb)), 0.0)
    for lvl in _LEVELS:
        d = b - blk(_LEVEL_INDEX[lvl])
        ql = (q * jnp.exp(jnp.minimum(d, 0.0))).astype(BF16)
        kl = (k * jnp.exp(jnp.minimum(-d, 0.0))).astype(BF16) if lvl > 2 else kb
        a = jnp.where(masks[lvl], _dot_nt(ql, _by_head(kl)), a)
    st_both = jnp.concatenate([jnp.concatenate([sts[0], jnp.zeros_like(sts[0])], axis=1),
                               jnp.concatenate([jnp.zeros_like(sts[1]), sts[1]], axis=1)], axis=0).astype(BF16)
    o = _dot(a.astype(BF16), _by_head(v.astype(BF16))) + _dot_nt((q * jnp.exp(b)).astype(BF16), st_both)
    b_last = b[CHUNK - 1:CHUNK, :]
    k_end = (k * jnp.exp(b_last - b)).astype(BF16)
    decay = jnp.exp(b_last)
    new = []
    for h in range(2):
        lanes = slice(h * hd, (h + 1) * hd)
        new.append(sts[h] * decay[:, lanes] + _dot(v[:, lanes].T.astype(BF16), k_end[:, lanes]))
    return o, new


def _mix_in(x_ref, mod_ref, t, gpre_ref, win_ref, wpool_ref, pscale_ref, lbl_ref, ubuf, dst):
    tt = TIME_TILE
    q_s, k_s, v_s, g_s, gate_s, yp_s = dst
    xt = x_ref[0]
    sh1, sc1 = _mods(mod_ref[0])[:2]
    h = (_rms(xt, gpre_ref[...]) * (1.0 + sc1) + sh1).astype(BF16)
    yield
    proj = _dot(h, win_ref[...])
    yield

    u = proj[:, :POOL_WIDTH]
    ubuf[16:16 + tt, :] = u
    pos = (t * tt + lax.broadcasted_iota(jnp.int32, (tt, 1), 0) + 1).astype(F32)
    ys = []
    for g, w in enumerate(POOL_WINDOWS):
        s = ubuf[:, g * POOL_GROUP:(g + 1) * POOL_GROUP]
        off = 0
        for step in range(w.bit_length() - 1):
            sh = 1 << step
            s = s[sh:, :] + s[:-sh, :]
            off += sh
        ws = s[16 - off:16 - off + tt, :]
        cnt = jnp.minimum(pos, float(w))
        d = ws / cnt - u[:, g * POOL_GROUP:(g + 1) * POOL_GROUP]
        ys.append(_dot(d.astype(BF16), wpool_ref[g]))
    yp_s[...] = jnp.concatenate(ys, axis=1) * pscale_ref[...]
    ubuf[0:16, :] = ubuf[tt:tt + 16, :]

    lb = _forget_lower_bound(lbl_ref[...])
    f = lb + (1.0 - lb) * jax.nn.sigmoid(proj[:, POOL_WIDTH + HG_WIDTH:POOL_WIDTH + 2 * HG_WIDTH])
    q_s[...] = _silu(proj[:, POOL_WIDTH:POOL_WIDTH + HG_WIDTH])
    k_s[...] = 1.0 - f
    v_s[...] = proj[:, POOL_WIDTH + 2 * HG_WIDTH:POOL_WIDTH + 3 * HG_WIDTH]
    g_s[...] = jnp.log(f)
    gate_s[...] = _silu(proj[:, POOL_WIDTH + 3 * HG_WIDTH:])


def _mix_out(x_ref, mod_ref, src, gpost_ref, gffn_ref, gout_ref, wout_ref, wrh_ref, wrl_ref, m2_ref,
             st_s, o_s, x1_ref, h2_ref, lg_ref):
    tt = TIME_TILE
    q_s, k_s, v_s, g_s, gate_s, yp_s = src
    masks, eye = _level_masks()
    m2 = m2_ref[...]

    states = [st_s[hd] for hd in range(HG_HEADS)]
    for c in range(tt // CHUNK):
        rows = slice(c * CHUNK, (c + 1) * CHUNK)
        g_hi, g_lo = _split2(g_s[rows, :])
        e_all = _dot(m2, jnp.concatenate([g_hi, g_lo], axis=0))
        for hd in range(0, HG_HEADS, 2):
            lane0 = hd * HG_HEAD_DIM
            lanes = slice(lane0, lane0 + 2 * HG_HEAD_DIM)
            o, states[hd:hd + 2] = _hgrn_chunk_pair(q_s[rows, lanes], k_s[rows, lanes], v_s[rows, lanes], e_all,
                                                    lane0, states[hd:hd + 2], masks, eye)
            o_s[rows, lanes] = o
    for hd in range(HG_HEADS):
        st_s[hd] = states[hd]
    yield

    o = o_s[...]
    os_ = []
    for hd in range(HG_HEADS):
        oh = o[:, hd * HG_HEAD_DIM:(hd + 1) * HG_HEAD_DIM]
        os_.append(oh * lax.rsqrt(jnp.mean(oh * oh, axis=-1, keepdims=True) + EPS) * gout_ref[...])
    o_n = jnp.concatenate(os_, axis=1) * gate_s[...]

    mix = (_dot(yp_s[...].astype(BF16), wout_ref[0:POOL_WIDTH, :])
           + _dot(o_n.astype(BF16), wout_ref[POOL_WIDTH:, :]))
    yield
    _, _, gt1, sh2, sc2, _ = _mods(mod_ref[0])
    x1 = x_ref[0] + gt1 * _rms(mix, gpost_ref[...])
    h2 = _rms(x1, gffn_ref[...]) * (1.0 + sc2) + sh2
    x1_ref[0] = x1
    h2_ref[0] = _pack_pair(h2[:, :ROW_WORDS], h2[:, ROW_WORDS:])
    lg_ref[0] = _router_logits(h2, wrh_ref[...], wrl_ref[...])


def _mix_prompt_kernel(x_ref, xn_ref, mod_ref, modn_ref, gpre_ref, gpost_ref, gffn_ref, win_ref, wpool_ref,
                       pscale_ref, lbl_ref, gout_ref, wout_ref, wrh_ref, wrl_ref, m2_ref,
                       x1_ref, h2_ref, lg_ref, pool_ref, st_ref,
                       st_s, ubuf, o_s, *sets, n_t):
    i = pl.program_id(0)
    t = i % n_t
    t_next = jnp.minimum(i + 1, pl.num_programs(0) - 1) % n_t
    mix_in = functools.partial(_mix_in, gpre_ref=gpre_ref, win_ref=win_ref, wpool_ref=wpool_ref,
                               pscale_ref=pscale_ref, lbl_ref=lbl_ref, ubuf=ubuf)
    cur = [s.at[i % 2] for s in sets]
    nxt = [s.at[(i + 1) % 2] for s in sets]

    def clear_history():
        ubuf[0:16, :] = jnp.zeros((16, POOL_WIDTH), F32)

    @pl.when(t == 0)
    def _():
        st_s[...] = jnp.zeros_like(st_s)

    @pl.when(i == 0)
    def _():
        clear_history()
        for _ in mix_in(x_ref, mod_ref, 0, dst=cur):
            pass

    pl.when((i + 1) % n_t == 0)(clear_history)

    finish = _mix_out(x_ref, mod_ref, cur, gpost_ref, gffn_ref, gout_ref, wout_ref, wrh_ref, wrl_ref,
                      m2_ref, st_s, o_s, x1_ref, h2_ref, lg_ref)
    prepare = mix_in(xn_ref, modn_ref, t_next, dst=nxt)
    for _ in itertools.zip_longest(finish, prepare):
        pass

    @pl.when((i + 1) % n_t == n_t - 1)
    def _():
        pool_ref[0] = ubuf[16 - POOL_HIST:16, :]

    @pl.when(t == n_t - 1)
    def _():
        for hd in range(HG_HEADS):
            st_ref[0, hd] = st_s[hd].T


def _full(shape):
    nd = len(shape)
    return pl.BlockSpec(shape, lambda *_: (0,) * nd)


def _mix_prompt(x, mod, gpre, gpost, gffn, win, wpool, pscale, lbl, gout, wout, wrh, wrl, m2):
    b, l, d = x.shape
    tt = TIME_TILE
    n_t = l // tt
    steps = b * n_t
    tile = lambda i: (i // n_t, i % n_t, 0)
    per_b = lambda i: (i // n_t, 0, 0)
    ahead = lambda i: jnp.minimum(i + 1, steps - 1)
    tile_next = lambda i: (ahead(i) // n_t, ahead(i) % n_t, 0)
    per_b_next = lambda i: (ahead(i) // n_t, 0, 0)
    recurrence_inputs = [pltpu.VMEM((2, tt, HG_WIDTH), F32)] * 5 + [pltpu.VMEM((2, tt, POOL_WIDTH), F32)]
    return pl.pallas_call(
        functools.partial(_mix_prompt_kernel, n_t=n_t),
        out_shape=(jax.ShapeDtypeStruct((b, l, d), F32),
                   jax.ShapeDtypeStruct((b, l, ROW_WORDS), jnp.uint32),
                   jax.ShapeDtypeStruct((b, l, LANES), F32),
                   jax.ShapeDtypeStruct((b, POOL_HIST, POOL_WIDTH), F32),
                   jax.ShapeDtypeStruct((b, HG_HEADS, HG_HEAD_DIM, HG_HEAD_DIM), F32)),
        grid=(steps,),
        in_specs=[pl.BlockSpec((1, tt, d), tile), pl.BlockSpec((1, tt, d), tile_next),
                  pl.BlockSpec((1, 1, 6 * d), per_b), pl.BlockSpec((1, 1, 6 * d), per_b_next),
                  _full((1, d)), _full((1, d)), _full((1, d)),
                  _full(win.shape), _full(wpool.shape), _full((1, POOL_WIDTH)),
                  _full(lbl.shape), _full((1, HG_HEAD_DIM)), _full(wout.shape),
                  _full(wrh.shape), _full(wrl.shape), _full(m2.shape)],
        out_specs=(pl.BlockSpec((1, tt, d), tile),
                   pl.BlockSpec((1, tt, ROW_WORDS), tile),
                   pl.BlockSpec((1, tt, LANES), tile),
                   pl.BlockSpec((1, POOL_HIST, POOL_WIDTH), per_b),
                   pl.BlockSpec((1, HG_HEADS, HG_HEAD_DIM, HG_HEAD_DIM), lambda i: (i // n_t, 0, 0, 0))),
        scratch_shapes=[pltpu.VMEM((HG_HEADS, HG_HEAD_DIM, HG_HEAD_DIM), F32),
                        pltpu.VMEM((tt + 16, POOL_WIDTH), F32),
                        pltpu.VMEM((tt, HG_WIDTH), F32)] + recurrence_inputs,
        compiler_params=pltpu.CompilerParams(dimension_semantics=("arbitrary",),
                                             vmem_limit_bytes=VMEM_LIMIT),
        name="mix_prompt",
    )(x, x, mod, mod, gpre, gpost, gffn, win, wpool, pscale, lbl, gout, wout, wrh, wrl, m2)


def _mix_sample_in_kernel(x_ref, mod_ref, gpre_ref, win_ref, wpool_ref, pscale_ref, lbl_ref, hist_ref,
                          ypool_ref, npool_ref, ft_ref, qt_ref, v_ref, gate_ref):
    xt = x_ref[...]
    sh1, sc1 = _mods(mod_ref[...])[:2]
    h = _rms(xt, gpre_ref[...]) * (1.0 + sc1) + sh1
    proj = _dot(h.astype(BF16), win_ref[...])
    u = proj[:, :POOL_WIDTH]
    row = lax.broadcasted_iota(jnp.int32, (hist_ref.shape[0], POOL_HIST, POOL_GROUP), 1)
    ys = []
    for g, w in enumerate(POOL_WINDOWS):
        sl = slice(g * POOL_GROUP, (g + 1) * POOL_GROUP)
        past = jnp.sum(jnp.where(row >= POOL_HIST - (w - 1), hist_ref[:, :, sl], 0.0), axis=1)
        ug = u[:, sl]
        d = (past + ug) / float(w) - ug
        ys.append(_dot(d.astype(BF16), wpool_ref[g]))
    ypool_ref[...] = jnp.concatenate(ys, axis=1) * pscale_ref[...]
    npool_ref[:, 0:POOL_HIST - 1, :] = hist_ref[:, 1:POOL_HIST, :]
    npool_ref[:, POOL_HIST - 1, :] = u

    lb = _forget_lower_bound(lbl_ref[...])
    f = lb + (1.0 - lb) * jax.nn.sigmoid(proj[:, POOL_WIDTH + HG_WIDTH:POOL_WIDTH + 2 * HG_WIDTH])
    ft_ref[...] = f.T
    qt_ref[...] = _silu(proj[:, POOL_WIDTH:POOL_WIDTH + HG_WIDTH]).T
    v_ref[...] = proj[:, POOL_WIDTH + 2 * HG_WIDTH:POOL_WIDTH + 3 * HG_WIDTH]
    gate_ref[...] = _silu(proj[:, POOL_WIDTH + 3 * HG_WIDTH:])


def _mix_sample_state_kernel(s_ref, ft_ref, qt_ref, v_ref, snew_ref, o_ref):
    i = pl.program_id(0)
    lane = lax.broadcasted_iota(jnp.int32, (HG_HEAD_DIM, ft_ref.shape[1]), 1)
    for j in range(SAMPLE_STATE_BLOCK):
        mine = lane == i * SAMPLE_STATE_BLOCK + j
        for hd in range(HG_HEADS):
            r0 = hd * HG_HEAD_DIM
            f = jnp.sum(jnp.where(mine, ft_ref[r0:r0 + HG_HEAD_DIM, :], 0.0), axis=1, keepdims=True)
            q = jnp.sum(jnp.where(mine, qt_ref[r0:r0 + HG_HEAD_DIM, :], 0.0), axis=1, keepdims=True)
            v = v_ref[j:j + 1, r0:r0 + HG_HEAD_DIM]
            s_new = f * s_ref[j, hd] + (1.0 - f) * v
            snew_ref[j, hd] = s_new
            o_ref[j:j + 1, r0:r0 + HG_HEAD_DIM] = jnp.sum(q * s_new, axis=0, keepdims=True)


def _mix_sample_out_kernel(x_ref, mod_ref, o_ref, gate_ref, ypool_ref, gout_ref, wout_ref, gpost_ref,
                           gffn_ref, wrh_ref, wrl_ref, x1_ref, h2_ref, lg_ref):
    _, _, gt1, sh2, sc2, _ = _mods(mod_ref[...])
    o = o_ref[...]
    os_ = []
    for hd in range(HG_HEADS):
        oh = o[:, hd * HG_HEAD_DIM:(hd + 1) * HG_HEAD_DIM]
        os_.append(oh * lax.rsqrt(jnp.mean(oh * oh, axis=-1, keepdims=True) + EPS) * gout_ref[...])
    o_n = jnp.concatenate(os_, axis=1) * gate_ref[...]
    mix = (_dot(ypool_ref[...].astype(BF16), wout_ref[0:POOL_WIDTH, :])
           + _dot(o_n.astype(BF16), wout_ref[POOL_WIDTH:, :]))
    x1 = x_ref[...] + gt1 * _rms(mix, gpost_ref[...])
    h2 = _rms(x1, gffn_ref[...]) * (1.0 + sc2) + sh2
    x1_ref[...] = x1
    h2_ref[...] = h2.astype(BF16)
    lg_ref[...] = _router_logits(h2, wrh_ref[...], wrl_ref[...])


def _mix_sample(x, mod, hist, state, gpre, gpost, gffn, win, wpool, pscale, lbl, gout, wout, wrh, wrl):
    b = x.shape[0]
    bb = SAMPLE_STATE_BLOCK
    cp = pltpu.CompilerParams(vmem_limit_bytes=VMEM_LIMIT)
    ypool, npool, ft, qt, v, gate = pl.pallas_call(
        _mix_sample_in_kernel,
        out_shape=(jax.ShapeDtypeStruct((b, POOL_WIDTH), F32),
                   jax.ShapeDtypeStruct((b, POOL_HIST, POOL_WIDTH), F32),
                   jax.ShapeDtypeStruct((HG_WIDTH, b), F32),
                   jax.ShapeDtypeStruct((HG_WIDTH, b), F32),
                   jax.ShapeDtypeStruct((b, HG_WIDTH), F32),
                   jax.ShapeDtypeStruct((b, HG_WIDTH), F32)),
        compiler_params=cp,
        name="mix_sample_in",
    )(x, mod, gpre, win, wpool, pscale, lbl, hist)

    s_spec = pl.BlockSpec((bb, HG_HEADS, HG_HEAD_DIM, HG_HEAD_DIM), lambda i: (i, 0, 0, 0))
    col_spec = _full((HG_WIDTH, b))
    row_spec = pl.BlockSpec((bb, HG_WIDTH), lambda i: (i, 0))
    s_new, o = pl.pallas_call(
        _mix_sample_state_kernel,
        out_shape=(jax.ShapeDtypeStruct(state.shape, F32), jax.ShapeDtypeStruct((b, HG_WIDTH), F32)),
        grid=(b // bb,),
        in_specs=[s_spec, col_spec, col_spec, row_spec],
        out_specs=(s_spec, row_spec),
        compiler_params=pltpu.CompilerParams(dimension_semantics=("parallel",), vmem_limit_bytes=VMEM_LIMIT),
        name="mix_sample_state",
    )(state, ft, qt, v)

    x1, h2, lg = pl.pallas_call(
        _mix_sample_out_kernel,
        out_shape=(jax.ShapeDtypeStruct((b, D_MODEL), F32),
                   jax.ShapeDtypeStruct((b, D_MODEL), BF16),
                   jax.ShapeDtypeStruct((b, LANES), F32)),
        compiler_params=cp,
        name="mix_sample_out",
    )(x, mod, o, gate, ypool, gout, wout, gpost, gffn, wrh, wrl)
    return x1, h2, lg, npool, s_new


def _first_max(cur, idx, big):
    m = jnp.max(cur, axis=0, keepdims=True)
    first = jnp.min(jnp.where(cur == m, idx, big), axis=0, keepdims=True)
    return idx == first


def _route(logit, bias):
    n = logit.shape[1]
    scores = jax.nn.sigmoid(logit)
    sel = scores + bias
    neg = -jnp.inf

    sub = lax.broadcasted_iota(jnp.int32, (GROUP_SIZE, n), 0)
    gscore = []
    for g in range(N_GROUPS):
        sg = sel[g * GROUP_SIZE:(g + 1) * GROUP_SIZE, :]
        m1 = jnp.max(sg, axis=0, keepdims=True)
        rest = jnp.where(_first_max(sg, sub, GROUP_SIZE), neg, sg)
        gscore.append(m1 + jnp.max(rest, axis=0, keepdims=True))
    cur = jnp.concatenate(gscore, axis=0)
    gidx = lax.broadcasted_iota(jnp.int32, (N_GROUPS, n), 0)
    gmask = jnp.zeros((N_GROUPS, n), jnp.bool_)
    for _ in range(TOPK_GROUPS):
        pick = _first_max(cur, gidx, N_GROUPS)
        gmask = gmask | pick
        cur = jnp.where(pick, neg, cur)
    emask = jnp.concatenate(
        [jnp.broadcast_to(gmask[g:g + 1, :], (GROUP_SIZE, n)) for g in range(N_GROUPS)], axis=0)

    cur = jnp.where(emask, sel, neg)
    eidx = lax.broadcasted_iota(jnp.int32, (N_EXPERTS, n), 0)
    picks = []
    for _ in range(TOP_K):
        pick = _first_max(cur, eidx, N_EXPERTS)
        picks.append(pick)
        cur = jnp.where(pick, neg, cur)
    return scores, picks


def _any(masks):
    return functools.reduce(jnp.logical_or, masks)


def _router_kernel(lg_ref, bias_ref, wd_ref):
    n = lg_ref.shape[0]
    scores, picks = _route(lg_ref[...].T[0:N_EXPERTS, :], bias_ref[0:N_EXPERTS, :])
    chosen = _any(picks)
    wsum = jnp.sum(jnp.where(chosen, scores, 0.0), axis=0, keepdims=True)
    wd = jnp.where(chosen, scores / wsum * ROUTED_SCALE, 0.0)
    wd_ref[...] = jnp.concatenate([wd, jnp.zeros((LANES - N_EXPERTS, n), F32)], axis=0).T


def _router(logits, bias_col):
    t = logits.shape[0]
    tile = min(ROUTER_TILE, t)
    return pl.pallas_call(
        _router_kernel,
        out_shape=jax.ShapeDtypeStruct((t, LANES), F32),
        grid=(t // tile,),
        in_specs=[pl.BlockSpec((tile, LANES), lambda i: (i, 0)), _full(bias_col.shape)],
        out_specs=pl.BlockSpec((tile, LANES), lambda i: (i, 0)),
        compiler_params=pltpu.CompilerParams(dimension_semantics=("parallel",), vmem_limit_bytes=VMEM_LIMIT),
        name="router",
    )(logits, bias_col)


def _plan_kernel(lg_ref, bias_ref, upper_ref, dest_ref, wsel_ref, cnt_ref, loff_ref):
    scores, picks = _route(lg_ref[...].T[0:N_EXPERTS, :], bias_ref[0:N_EXPERTS, :])
    chosen = _any(picks)
    chosen_f = jnp.where(chosen, 1.0, 0.0)
    cnt = jnp.sum(chosen_f, axis=1, keepdims=True)
    units = jnp.floor((cnt + (RUN_ALIGN - 1)) / RUN_ALIGN)
    ei = lax.broadcasted_iota(jnp.int32, (N_EXPERTS, N_EXPERTS), 0)
    ej = lax.broadcasted_iota(jnp.int32, (N_EXPERTS, N_EXPERTS), 1)
    before = jnp.where(ej < ei, 1.0, 0.0).astype(BF16)
    loff = RUN_ALIGN * _dot(before, jnp.broadcast_to(units, (N_EXPERTS, LANES)).astype(BF16))[:, 0:1]
    slot = loff + _dot(chosen_f.astype(BF16), upper_ref[...])
    wsum = jnp.sum(jnp.where(chosen, scores, 0.0), axis=0, keepdims=True)
    pick_sum = lambda v: jnp.concatenate(
        [jnp.sum(jnp.where(p, v, 0.0), axis=0, keepdims=True) for p in picks], axis=0)
    dest_ref[0] = (pick_sum(slot) * ROW_SUB).astype(jnp.int32)
    wsel_ref[0] = pick_sum(scores) / wsum * ROUTED_SCALE
    cnt_ref[0] = jnp.broadcast_to(cnt, (N_EXPERTS, LANES))
    loff_ref[0] = jnp.broadcast_to(loff, (N_EXPERTS, LANES))


def _plan(logits, bias_col, upper):
    t = logits.shape[0]
    nw = t // PLAN_WINDOW
    per_w = pl.BlockSpec((1, N_EXPERTS, LANES), lambda w: (w, 0, 0))
    picks = pl.BlockSpec((1, TOP_K, PLAN_WINDOW), lambda w: (w, 0, 0))
    return pl.pallas_call(
        _plan_kernel,
        out_shape=(jax.ShapeDtypeStruct((nw, TOP_K, PLAN_WINDOW), jnp.int32),
                   jax.ShapeDtypeStruct((nw, TOP_K, PLAN_WINDOW), F32),
                   jax.ShapeDtypeStruct((nw, N_EXPERTS, LANES), F32),
                   jax.ShapeDtypeStruct((nw, N_EXPERTS, LANES), F32)),
        grid=(nw,),
        in_specs=[pl.BlockSpec((PLAN_WINDOW, LANES), lambda w: (w, 0)), _full(bias_col.shape), _full(upper.shape)],
        out_specs=(picks, picks, per_w, per_w),
        compiler_params=pltpu.CompilerParams(dimension_semantics=("parallel",), vmem_limit_bytes=VMEM_LIMIT),
        name="plan",
    )(logits, bias_col, upper)


def _slot_rows(slot):
    return pl.ds(pl.multiple_of(slot * ROW_SUB, ROW_SUB), ROW_SUB)


def _table_copies(table_ref, flat_s, sem):
    return [pltpu.make_async_copy(table_ref.at[0, k], flat_s.at[pl.ds(k * PLAN_WINDOW, PLAN_WINDOW)], sem)
            for k in range(TOP_K)]


def _rows_at(offset):
    return pl.ds(pl.multiple_of(offset, ROW_SUB), ROW_SUB)


def _dispatch_kernel(dest_ref, h_ref, x_ref, dest_s, rows_s, sem):
    plan_copies = _table_copies(dest_ref, dest_s, sem)
    for c in plan_copies:
        c.start()
    x_ref[...] = jnp.zeros_like(x_ref)
    for j in range(ROW_SUB):
        rows_s[pl.ds(j, PLAN_WINDOW, stride=ROW_SUB), :] = h_ref[:, j * LANES:(j + 1) * LANES]
    for c in plan_copies:
        c.wait()

    def body(i, carry):
        for u in range(SCATTER_UNROLL):
            t = i * SCATTER_UNROLL + u
            row = rows_s[_slot_rows(t), :]
            for k in range(TOP_K):
                x_ref[_rows_at(dest_s[k * PLAN_WINDOW + t]), :] = row
        return carry

    lax.fori_loop(0, PLAN_WINDOW // SCATTER_UNROLL, body, 0)


def _dispatch(dest_w, h2):
    nw = dest_w.shape[0]
    region = REGION_ROWS * ROW_SUB
    return pl.pallas_call(
        _dispatch_kernel,
        out_shape=jax.ShapeDtypeStruct((nw * region, LANES), jnp.uint32),
        grid=(nw,),
        in_specs=[pl.BlockSpec((1, TOP_K, PLAN_WINDOW), lambda w: (w, 0, 0)),
                  pl.BlockSpec((PLAN_WINDOW, ROW_WORDS), lambda w: (w, 0))],
        out_specs=pl.BlockSpec((region, LANES), lambda w: (w, 0)),
        scratch_shapes=[pltpu.SMEM((TOP_K * PLAN_WINDOW,), jnp.int32),
                        pltpu.VMEM((PLAN_WINDOW * ROW_SUB, LANES), jnp.uint32),
                        pltpu.SemaphoreType.DMA],
        compiler_params=pltpu.CompilerParams(dimension_semantics=("arbitrary",), vmem_limit_bytes=VMEM_LIMIT),
        name="dispatch",
    )(dest_w, h2)


def _unpack_rows(ref, n):
    parts = [_unpack_pair(ref[pl.ds(j, n, stride=ROW_SUB), :]) for j in range(ROW_SUB)]
    return jnp.concatenate([p[0] for p in parts] + [p[1] for p in parts], axis=1)


def _ffn_kernel(gq_ref, gs_ref, to_ref, tn_ref, x_hbm, w1_ref, w3_ref, w2_ref,
                hs_ref, wds_ref, x1s_ref, gt2s_ref, gpost_ref, ws1_ref, ws3_ref, ws2_ref, y_hbm, ys_ref,
                w13_s, w2_s, xbuf, ybuf, rsem, wsem, acc_s):
    del x_hbm
    e = pl.program_id(0)
    n_groups = gq_ref[N_EXPERTS]
    block = RUN_ALIGN * ROW_SUB
    tile = FFN_TILE * ROW_SUB

    def go(cp, start):
        if start:
            cp.start()
        else:
            cp.wait()

    def fetch(q, start):
        def one(j):
            src = y_hbm.at[pl.ds(pl.multiple_of(to_ref[gs_ref[q] + j] * block, block), tile)]
            go(pltpu.make_async_copy(src, xbuf.at[q % FFN_RING, pl.ds(j * tile, tile)], rsem.at[q % FFN_RING]), start)

        size = gs_ref[q + 1] - gs_ref[q]

        @pl.when(size == FFN_GROUP)
        def _():
            for j in range(FFN_GROUP):
                one(j)

        @pl.when(size < FFN_GROUP)
        def _():
            for j in range(FFN_GROUP - 1):
                pl.when(j < size)(functools.partial(one, j))

    def writeback(q, start):
        most = FFN_TILE // RUN_ALIGN - 1

        def blocks(j, b, n):
            dst = pl.multiple_of(to_ref[gs_ref[q] + j] * block, block) + b * block
            go(pltpu.make_async_copy(ybuf.at[q % FFN_RING, pl.ds(j * tile + b * block, n * block)],
                                     y_hbm.at[pl.ds(dst, n * block)], wsem.at[q % FFN_RING]), start)

        for j in range(FFN_GROUP):
            item = gs_ref[q] + j
            n = jnp.where(item < gs_ref[q + 1], tn_ref[item], 0)
            pl.when(n >= most)(functools.partial(blocks, j, 0, most))
            pl.when(n > most)(functools.partial(blocks, j, most, 1))

            @pl.when((n > 0) & (n < most))
            def _():
                for b in range(most - 1):
                    pl.when(b < n)(functools.partial(blocks, j, b, 1))

    @pl.when(e == 0)
    def _():
        xbuf[...] = jnp.zeros_like(xbuf)
        for d in range(FFN_RING - 1):
            pl.when(d < n_groups)(functools.partial(fetch, d, True))

    w13_s[:, 0:D_EXPERT] = w1_ref[0].astype(BF16)
    w13_s[:, D_EXPERT:] = w3_ref[0].astype(BF16)
    w2_s[...] = w2_ref[0].astype(BF16)

    xs = hs_ref[...]

    @pl.when(e == 0)
    def _():
        sh = _silu(_dot(xs, ws1_ref[...].astype(BF16))) * _dot(xs, ws3_ref[...].astype(BF16))
        acc_s[...] = _dot(sh.astype(BF16), ws2_ref[...].astype(BF16))

    lane = lax.broadcasted_iota(jnp.int32, wds_ref.shape, 1)
    wcol = jnp.sum(jnp.where(lane == e, wds_ref[...], 0.0), axis=1, keepdims=True)
    rows = FFN_GROUP * FFN_TILE

    def ffn(x, ride):
        n = 0 if x is None else x.shape[0]
        if ride:
            x = xs if x is None else jnp.concatenate([x, xs], axis=0)
        h = _dot(x, w13_s[...])
        y = _dot((_silu(h[:, :D_EXPERT]) * h[:, D_EXPERT:]).astype(BF16), w2_s[...])
        if ride:
            acc_s[...] += y[n:] * wcol
        return y[:n] if n else None

    def group(q, ride):
        @pl.when(q + FFN_RING - 1 < n_groups)
        def _():
            fetch(q + FFN_RING - 1, True)

        @pl.when(q >= FFN_RING)
        def _():
            writeback(q - FFN_RING, False)

        fetch(q, False)
        y = ffn(_unpack_rows(xbuf.at[q % FFN_RING], rows).astype(BF16), ride)
        out = ybuf.at[q % FFN_RING]
        for j in range(ROW_SUB):
            out[pl.ds(j, rows, stride=ROW_SUB), :] = _pack_pair(
                y[:, j * LANES:(j + 1) * LANES], y[:, ROW_WORDS + j * LANES:ROW_WORDS + (j + 1) * LANES])
        writeback(q, True)

    first, end = gq_ref[e], gq_ref[e + 1]
    pl.when(first < end)(functools.partial(group, first, True))
    pl.when(first == end)(functools.partial(ffn, None, True))

    def group_body(q, carry):
        group(q, False)
        return carry

    lax.fori_loop(first + 1, end, group_body, 0)

    @pl.when(e == pl.num_programs(0) - 1)
    def _():
        ys_ref[...] = x1s_ref[...] + gt2s_ref[...] * _rms(acc_s[...], gpost_ref[...])

    @pl.when(e == pl.num_programs(0) - 1)
    def _():
        for d in range(FFN_RING):
            @pl.when(n_groups - 1 - d >= 0)
            def _():
                writeback(n_groups - 1 - d, False)


def _expert_ffn(group_of_expert, group_start, item_off, item_blocks, x_slots, w1, w3, w2,
                h2_s, wd_s, x1_s, gt2_s, gpost, ws1, ws3, ws2):
    of_expert = lambda e, *_: (e, 0, 0)
    whole = lambda a: pl.BlockSpec(a.shape, lambda e, *_: (0,) * a.ndim)
    tiles = pltpu.VMEM((FFN_RING, FFN_GROUP * FFN_TILE * ROW_SUB, LANES), jnp.uint32)
    sems = pltpu.SemaphoreType.DMA((FFN_RING,))
    dense = (h2_s, wd_s, x1_s, gt2_s, gpost, ws1, ws3, ws2)
    grid_spec = pltpu.PrefetchScalarGridSpec(
        num_scalar_prefetch=4,
        grid=(N_EXPERTS,),
        in_specs=[pl.BlockSpec(memory_space=pl.ANY),
                  pl.BlockSpec((1, D_MODEL, D_EXPERT), of_expert),
                  pl.BlockSpec((1, D_MODEL, D_EXPERT), of_expert),
                  pl.BlockSpec((1, D_EXPERT, D_MODEL), of_expert)] + [whole(a) for a in dense],
        out_specs=(pl.BlockSpec(memory_space=pl.ANY), whole(x1_s)),
        scratch_shapes=[pltpu.VMEM((D_MODEL, 2 * D_EXPERT), BF16), pltpu.VMEM((D_EXPERT, D_MODEL), BF16),
                        tiles, tiles, sems, sems, pltpu.VMEM(x1_s.shape, F32)])
    return pl.pallas_call(
        _ffn_kernel,
        out_shape=(jax.ShapeDtypeStruct(x_slots.shape, jnp.uint32), jax.ShapeDtypeStruct(x1_s.shape, F32)),
        grid_spec=grid_spec,
        input_output_aliases={4: 0},
        compiler_params=pltpu.CompilerParams(dimension_semantics=("arbitrary",), vmem_limit_bytes=VMEM_LIMIT),
        name="expert_ffn",
    )(group_of_expert, group_start, item_off, item_blocks, x_slots, w1, w3, w2, *dense)


def _work_items(cnt, loff):
    nw = cnt.shape[0]
    cnt = cnt.astype(jnp.int32).T.reshape(-1)
    loff = loff.astype(jnp.int32).T.reshape(-1)
    region = jnp.tile(jnp.arange(nw, dtype=jnp.int32) * REGION_ROWS, N_EXPERTS)
    n_items = nw * N_EXPERTS + (nw * PLAN_WINDOW * TOP_K) // FFN_TILE
    tiles = (cnt + FFN_TILE - 1) // FFN_TILE
    ends = jnp.cumsum(tiles)
    starts = ends - tiles
    i = jnp.arange(n_items + FFN_GROUP, dtype=jnp.int32)[:, None]
    mine = (starts[None, :] <= i) & (i < ends[None, :])
    of_pair = lambda v: jnp.sum(jnp.where(mine, v[None, :], 0), axis=1)
    r = i[:, 0] - of_pair(starts)
    row = of_pair(region + loff) + r * FFN_TILE
    own = jnp.clip(of_pair(cnt) - r * FFN_TILE, 0, FFN_TILE)

    zero = jnp.zeros((1,), jnp.int32)
    item_end = ends.reshape(N_EXPERTS, nw)[:, -1]
    item_start = jnp.concatenate([zero, item_end[:-1]])
    group_end = jnp.cumsum((item_end - item_start + FFN_GROUP - 1) // FFN_GROUP)
    group_first = jnp.concatenate([zero, group_end[:-1]])
    q = jnp.arange(N_EXPERTS + n_items // FFN_GROUP + 1, dtype=jnp.int32)[:, None]
    has = (group_first[None, :] <= q) & (q < group_end[None, :])
    of_expert = lambda v: jnp.sum(jnp.where(has, v[None, :], 0), axis=1)
    group_start = jnp.where(q[:, 0] < group_end[-1],
                            of_expert(item_start) + (q[:, 0] - of_expert(group_first)) * FFN_GROUP, item_end[-1])
    return (jnp.concatenate([zero, group_end]), group_start, row // RUN_ALIGN, (own + RUN_ALIGN - 1) // RUN_ALIGN)


def _combine_kernel(dest_ref, wsel_ref, y_hbm, h_ref, x1_ref, gt2_ref, gpost_ref, ws13_ref, ws2_ref, out_ref,
                    dest_s, wsel_s, lo_s, hi_s, ybuf, sem, ysem):
    w, s = pl.program_id(0), pl.program_id(1)
    region = REGION_ROWS * ROW_SUB
    part = region // REGION_PARTS

    def region_copies(win):
        return [pltpu.make_async_copy(y_hbm.at[pl.ds(win * region + p * part, part)],
                                      ybuf.at[win % 2, pl.ds(p * part, part)], ysem.at[win % 2, p])
                for p in range(REGION_PARTS)]

    @pl.when(s == 0)
    def _():
        copies = _table_copies(dest_ref, dest_s, sem.at[0]) + _table_copies(wsel_ref, wsel_s, sem.at[1])
        for c in copies:
            c.start()

        @pl.when(w == 0)
        def _():
            for c in region_copies(w):
                c.start()

        @pl.when(w + 1 < pl.num_programs(0))
        def _():
            for c in region_copies(w + 1):
                c.start()

        for c in copies + region_copies(w):
            c.wait()

    y_ref = ybuf.at[w % 2]

    base = s * COMBINE_TILE

    def body(i, carry):
        for u in range(COMBINE_UNROLL):
            t = i * COMBINE_UNROLL + u
            acc_lo = jnp.zeros((ROW_SUB, LANES), F32)
            acc_hi = jnp.zeros((ROW_SUB, LANES), F32)
            for k in range(TOP_K):
                at = k * PLAN_WINDOW + base + t
                lo, hi = _unpack_pair(y_ref[_rows_at(dest_s[at]), :])
                w = wsel_s[at]
                acc_lo = acc_lo + w * lo
                acc_hi = acc_hi + w * hi
            lo_s[_slot_rows(t), :] = acc_lo
            hi_s[_slot_rows(t), :] = acc_hi
        return carry

    lax.fori_loop(0, COMBINE_TILE // COMBINE_UNROLL, body, 0)
    rows = lambda ref: [ref[pl.ds(j, COMBINE_TILE, stride=ROW_SUB), :] for j in range(ROW_SUB)]
    routed = jnp.concatenate(rows(lo_s) + rows(hi_s), axis=1)
    h_lo, h_hi = _unpack_pair(h_ref[...])
    x = jnp.concatenate([h_lo, h_hi], axis=1).astype(BF16)
    hs = _dot(x, ws13_ref[...])
    act = _silu(hs[:, :D_EXPERT]) * hs[:, D_EXPERT:]
    ff = routed + _dot(act.astype(BF16), ws2_ref[...])
    out_ref[...] = x1_ref[...] + gt2_ref[0] * _rms(ff, gpost_ref[...])


def _combine(dest_w, wsel_w, y_slots, h2, x1, gt2, tokens_per_gt2, gpost, ws13, ws2):
    nw = dest_w.shape[0]
    sub = PLAN_WINDOW // COMBINE_TILE
    plan = pl.BlockSpec((1, TOP_K, PLAN_WINDOW), lambda w, s: (w, 0, 0))
    tok = lambda w, s: (w * sub + s, 0)
    return pl.pallas_call(
        _combine_kernel,
        out_shape=jax.ShapeDtypeStruct(x1.shape, F32),
        grid=(nw, sub),
        in_specs=[plan, plan,
                  pl.BlockSpec(memory_space=pl.ANY),
                  pl.BlockSpec((COMBINE_TILE, ROW_WORDS), tok),
                  pl.BlockSpec((COMBINE_TILE, D_MODEL), tok),
                  pl.BlockSpec((1, 1, D_MODEL), lambda w, s: ((w * PLAN_WINDOW) // tokens_per_gt2, 0, 0)),
                  _full((1, D_MODEL)), _full(ws13.shape), _full(ws2.shape)],
        out_specs=pl.BlockSpec((COMBINE_TILE, D_MODEL), tok),
        scratch_shapes=[pltpu.SMEM((TOP_K * PLAN_WINDOW,), jnp.int32),
                        pltpu.SMEM((TOP_K * PLAN_WINDOW,), F32),
                        pltpu.VMEM((COMBINE_TILE * ROW_SUB, LANES), F32),
                        pltpu.VMEM((COMBINE_TILE * ROW_SUB, LANES), F32),
                        pltpu.VMEM((2, REGION_ROWS * ROW_SUB, LANES), jnp.uint32),
                        pltpu.SemaphoreType.DMA((2,)), pltpu.SemaphoreType.DMA((2, REGION_PARTS))],
        compiler_params=pltpu.CompilerParams(dimension_semantics=("arbitrary", "arbitrary"),
                                             vmem_limit_bytes=VMEM_LIMIT),
        name="combine",
    )(dest_w, wsel_w, y_slots, h2, x1, gt2, gpost, ws13, ws2)


def kernel(x_prompt, x_sample, c_prompt, c_sample, state_pool, state_hgrn, w_ada, b_ada, g_pre_mix, g_post_mix,
           w_in, w_pool, pool_scale, lb_logits, g_out_norm, w_out, g_pre_ffn, g_post_ffn, w_router, router_bias,
           w_exp_gate, w_exp_up, w_exp_down, w_sh_gate, w_sh_up, w_sh_down):
    assert w_ada.shape[0] == 1 and lb_logits.shape[0] == 2, "single-layer trunk"
    bp, lp, d = x_prompt.shape
    bs = x_sample.shape[0]
    row = lambda a: a[0].reshape(1, -1)

    n_mod = bp + bs
    pad = (-n_mod) % 16
    c_all = jnp.concatenate([c_prompt, c_sample, jnp.zeros((pad, d), F32)], axis=0)
    mod = _ada(c_all, w_ada[0], b_ada)
    mod_p = mod[:bp].reshape(bp, 1, 6 * d)
    mod_s = mod[bp:n_mod]

    win = w_in[0].astype(BF16)
    wout = w_out[0].astype(BF16)
    wpool = w_pool[0].astype(BF16)
    wr = jnp.pad(w_router[0], ((0, 0), (0, LANES - N_EXPERTS)))
    wrh = wr.astype(BF16)
    wrl = (wr - wrh.astype(F32)).astype(BF16)
    m2 = jnp.asarray(_M2_NP, BF16)
    gpre, gpost, gffn, gffn_post = row(g_pre_mix), row(g_post_mix), row(g_pre_ffn), row(g_post_ffn)
    pscale, gout = row(pool_scale), row(g_out_norm)
    bias_col = jnp.pad(router_bias[0], (0, LANES - N_EXPERTS)).reshape(LANES, 1)

    x1_p, h2_p, lg_p, pool_p, st_p = _mix_prompt(x_prompt, mod_p, gpre, gpost, gffn, win, wpool, pscale,
                                                  lb_logits, gout, wout, wrh, wrl, m2)
    x1_s, h2_s, lg_s, pool_s, st_s = _mix_sample(x_sample[:, 0, :], mod_s, state_pool[0], state_hgrn[0],
                                                  gpre, gpost, gffn, win, wpool, pscale, lb_logits, gout, wout,
                                                  wrh, wrl)

    experts = (w_exp_gate[0], w_exp_up[0], w_exp_down[0], w_sh_gate[0], w_sh_up[0], w_sh_down[0])
    tp = bp * lp
    nw = tp // PLAN_WINDOW
    assert lp % PLAN_WINDOW == 0
    upper = jnp.asarray(np.triu(np.ones((PLAN_WINDOW, PLAN_WINDOW), np.float32), 1), BF16)
    dest, wsel, cnt, loff = _plan(lg_p.reshape(tp, LANES), bias_col, upper)
    dest_w, wsel_w = dest, wsel
    h2_rows = h2_p.reshape(tp, ROW_WORDS)
    x_slots = _dispatch(dest_w, h2_rows)
    wd_s = _router(lg_s, bias_col)
    y_slots, y_s = _expert_ffn(*_work_items(cnt[:, :, 0], loff[:, :, 0]), x_slots, *experts[:3],
                               h2_s, wd_s, x1_s, mod_s[:, 5 * d:], gffn_post, *experts[3:])
    ws13 = jnp.concatenate([experts[3], experts[4]], axis=1).astype(BF16)
    gt2_p = mod_p[:, :, 5 * d:]
    y_p = _combine(dest_w, wsel_w, y_slots, h2_rows, x1_p.reshape(tp, d), gt2_p, lp, gffn_post,
                   ws13, experts[5].astype(BF16))

    return (y_p.reshape(bp, lp, d), y_s.reshape(bs, 1, d), pool_p[None], st_p[None], pool_s[None], st_s[None])
```
